```python
import jax, jax.numpy as jnp
from jax import lax
import numpy as np

D_MODEL = 1024
BATCH = 8
SEQ = 2048
DEPTH = 4

CTX_LEN = 256
GRID_W = 64
N_MIXERS = 3
EPS = 1e-6
CONV_WIDTH = 31
HEAD_DIM = 64
N_Q_HEADS = D_MODEL // HEAD_DIM
N_KV_HEADS = N_Q_HEADS // 4
Q_PER_KV = N_Q_HEADS // N_KV_HEADS
WINDOW = 128
ATTN_BLOCK = 128
ROPE_BASE = 10000.0
GMLP_CHUNK = 128
GMLP_WIDTH = 2 * D_MODEL
GMLP_GROUP_DIM = 128
GMLP_GROUPS = GMLP_WIDTH // GMLP_GROUP_DIM
FFN_DIM = 2816
FFN_CONV_WIDTH = 3

kernel_name = "hybrid_interleaved_conv_swa_gmlp_dit"


def rms_norm(x, g):
    xf = x.astype(jnp.float32)
    y = xf * lax.rsqrt(jnp.mean(xf * xf, axis=-1, keepdims=True) + EPS)
    return (y * g.astype(jnp.float32)).astype(x.dtype)


def layer_norm(x, g, b):
    xf = x.astype(jnp.float32)
    mu = jnp.mean(xf, axis=-1, keepdims=True)
    var = jnp.mean(jnp.square(xf - mu), axis=-1, keepdims=True)
    y = (xf - mu) * lax.rsqrt(var + EPS)
    return (y * g.astype(jnp.float32) + b.astype(jnp.float32)).astype(x.dtype)


def depthwise_conv(x, w, b):
    k = w.shape[0]
    half = (k - 1) // 2
    y = lax.conv_general_dilated(
        x, w[:, None, :].astype(x.dtype), window_strides=(1,), padding=[(half, half)],
        dimension_numbers=("NWC", "WIO", "NWC"), feature_group_count=x.shape[-1])
    return y + b


def axial_rope_tables(rows):
    row = jnp.repeat(jnp.arange(rows), GRID_W).astype(jnp.float32)
    col = jnp.tile(jnp.arange(GRID_W), rows).astype(jnp.float32)
    axis_dim = HEAD_DIM // 2
    inv_freq = ROPE_BASE ** (-jnp.arange(0, axis_dim, 2, dtype=jnp.float32) / axis_dim)
    ang_r = row[:, None] * inv_freq[None, :]
    ang_c = col[:, None] * inv_freq[None, :]
    ang = jnp.concatenate([ang_r, ang_r, ang_c, ang_c], axis=-1)
    return jnp.cos(ang), jnp.sin(ang)


def rotate_half(x):
    x1, x2 = jnp.split(x, 2, axis=-1)
    return jnp.concatenate([-x2, x1], axis=-1)


def apply_axial_rope(x, cos, sin):
    xr, xc = jnp.split(x, 2, axis=-1)
    rot = jnp.concatenate([rotate_half(xr), rotate_half(xc)], axis=-1)
    out = x * cos[None, :, None, :] + rot * sin[None, :, None, :]
    return out.astype(x.dtype)


def adaln_params(cond, w, b):
    mod = (jax.nn.silu(cond) @ w + b)[:, None, :]
    return jnp.split(mod, 6, axis=-1)


def conformer_conv(h, w_in, b_in, dw_w, dw_b, ln_g, ln_b, w_out, b_out):
    a, gt = jnp.split(h @ w_in + b_in, 2, axis=-1)
    z = a * jax.nn.sigmoid(gt)
    z = depthwise_conv(z, dw_w, dw_b)
    z = jax.nn.silu(layer_norm(z, ln_g, ln_b))
    return z @ w_out + b_out


def chunk_gmlp(h, w_in, b_in, ln_g, ln_b, w_s, b_s, w_out):
    bsz, length, _ = h.shape
    u, v = jnp.split(jax.nn.gelu(h @ w_in + b_in), 2, axis=-1)
    v = layer_norm(v, ln_g, ln_b)
    v = v.reshape(bsz, length // GMLP_CHUNK, GMLP_CHUNK, GMLP_GROUPS, GMLP_GROUP_DIM)
    s = jnp.einsum("gpq,bnqgc->bnpgc", w_s, v) + b_s.T[None, None, :, :, None]
    s = s.reshape(bsz, length, GMLP_WIDTH)
    return (u * s) @ w_out


def windowed_gqa(h, hc, w_qkv, sink, w_o, cos, sin, ctx_out):
    bsz, length, _ = h.shape
    n_ctx = hc.shape[1]
    nb = length // ATTN_BLOCK
    q_dim = N_Q_HEADS * HEAD_DIM
    kv_dim = N_KV_HEADS * HEAD_DIM
    scale = HEAD_DIM ** -0.5
    neg = jnp.float32(-1e30)

    q, k, v = jnp.split(h @ w_qkv, [q_dim, q_dim + kv_dim], axis=-1)
    q = apply_axial_rope(q.reshape(bsz, length, N_Q_HEADS, HEAD_DIM), cos, sin)
    k = apply_axial_rope(k.reshape(bsz, length, N_KV_HEADS, HEAD_DIM), cos, sin)
    v = v.reshape(bsz, length, N_KV_HEADS, HEAD_DIM)
    if ctx_out:
        qc, kc, vc = jnp.split(hc @ w_qkv, [q_dim, q_dim + kv_dim], axis=-1)
    else:
        kc, vc = jnp.split(hc @ w_qkv[:, q_dim:], 2, axis=-1)
    kc = kc.reshape(bsz, n_ctx, N_KV_HEADS, HEAD_DIM)
    vc = vc.reshape(bsz, n_ctx, N_KV_HEADS, HEAD_DIM)
    sink_f = sink.astype(jnp.float32).reshape(N_KV_HEADS, Q_PER_KV)

    pad = ((0, 0), (ATTN_BLOCK, ATTN_BLOCK), (0, 0), (0, 0))
    kp = jnp.pad(k, pad).reshape(bsz, nb + 2, ATTN_BLOCK, N_KV_HEADS, HEAD_DIM)
    vp = jnp.pad(v, pad).reshape(bsz, nb + 2, ATTN_BLOCK, N_KV_HEADS, HEAD_DIM)
    band = lambda t: jnp.concatenate([t[:, :-2], t[:, 1:-1], t[:, 2:]], axis=2)
    kw, vw = band(kp), band(vp)
    qb = q.reshape(bsz, nb, ATTN_BLOCK, N_KV_HEADS, Q_PER_KV, HEAD_DIM)

    s_win = jnp.einsum("bnqhgd,bnkhd->bnhgqk", qb, kw).astype(jnp.float32) * scale
    s_ctx = jnp.einsum("bnqhgd,bkhd->bnhgqk", qb, kc).astype(jnp.float32) * scale
    q_off = jnp.arange(ATTN_BLOCK)[:, None] + ATTN_BLOCK
    k_off = jnp.arange(3 * ATTN_BLOCK)[None, :]
    in_window = jnp.abs(q_off - k_off) <= WINDOW
    key_abs = (jnp.arange(nb)[:, None] - 1) * ATTN_BLOCK + jnp.arange(3 * ATTN_BLOCK)[None, :]
    valid = (key_abs >= 0) & (key_abs < length)
    mask = in_window[None] & valid[:, None, :]
    s_win = jnp.where(mask[None, :, None, None], s_win, neg)

    sk = sink_f[None, None, :, :, None, None]
    m = jnp.maximum(jnp.maximum(s_win.max(-1, keepdims=True), s_ctx.max(-1, keepdims=True)), sk)
    p_win = jnp.exp(s_win - m)
    p_ctx = jnp.exp(s_ctx - m)
    inv = 1.0 / (p_win.sum(-1, keepdims=True) + p_ctx.sum(-1, keepdims=True) + jnp.exp(sk - m))
    o = (jnp.einsum("bnhgqk,bnkhd->bnqhgd", p_win * inv, vw)
         + jnp.einsum("bnhgqk,bkhd->bnqhgd", p_ctx * inv, vc))
    y = o.astype(h.dtype).reshape(bsz, length, q_dim) @ w_o

    if not ctx_out:
        return y, None
    qc = qc.reshape(bsz, n_ctx, N_KV_HEADS, Q_PER_KV, HEAD_DIM)
    sc = jnp.einsum("bqhgd,bkhd->bhgqk", qc, kc).astype(jnp.float32) * scale
    skc = sink_f[None, :, :, None, None]
    mc = jnp.maximum(sc.max(-1, keepdims=True), skc)
    pc = jnp.exp(sc - mc)
    pc = pc / (pc.sum(-1, keepdims=True) + jnp.exp(skc - mc))
    oc = jnp.einsum("bhgqk,bkhd->bqhgd", pc, vc).astype(hc.dtype)
    yc = oc.reshape(bsz, n_ctx, q_dim) @ w_o
    return y, yc


def conv_ffn(h, w_up, conv_w, conv_b, w_down):
    z = depthwise_conv(h @ w_up, conv_w, conv_b)
    gate, val = jnp.split(z, 2, axis=-1)
    return (jax.nn.silu(gate) * val) @ w_down


def _fwd_setup_inputs(seed: int = 0) -> dict:
    key = jax.random.key(seed)
    keys = jax.random.split(key, 32)
    ks = [keys[i] for i in range(32)]
    counter = [0]

    def nrm(shape, s):
        k = ks[counter[0]]
        counter[0] += 1
        return jax.random.normal(k, shape, jnp.float32) * s

    D = D_MODEL
    n_a = len(range(0, DEPTH, N_MIXERS))
    n_b = len(range(1, DEPTH, N_MIXERS))
    n_c = len(range(2, DEPTH, N_MIXERS))
    qkv_dim = (N_Q_HEADS + 2 * N_KV_HEADS) * HEAD_DIM
    return {
        "x": nrm((BATCH, SEQ, D), 1.0),
        "c": nrm((BATCH, D), 1.0),
        "ctx": nrm((BATCH, CTX_LEN, D), 1.0),
        "c_ctx": nrm((D,), 1.0),
        "ada_w": nrm((DEPTH, D, 6 * D), 0.5 * D ** -0.5),
        "ada_b": nrm((DEPTH, 6 * D), 0.02),
        "norm_g": 1.0 + nrm((DEPTH, 4, D), 0.02),
        "ffn_w_up": nrm((DEPTH, D, 2 * FFN_DIM), D ** -0.5),
        "ffn_conv_w": nrm((DEPTH, FFN_CONV_WIDTH, 2 * FFN_DIM), FFN_CONV_WIDTH ** -0.5),
        "ffn_conv_b": nrm((DEPTH, 2 * FFN_DIM), 0.02),
        "ffn_w_down": nrm((DEPTH, FFN_DIM, D), FFN_DIM ** -0.5),
        "cm_w_in": nrm((n_a, D, 2 * D), D ** -0.5),
        "cm_b_in": nrm((n_a, 2 * D), 0.02),
        "cm_dw_w": nrm((n_a, CONV_WIDTH, D), CONV_WIDTH ** -0.5),
        "cm_dw_b": nrm((n_a, D), 0.02),
        "cm_ln_g": 1.0 + nrm((n_a, D), 0.02),
        "cm_ln_b": nrm((n_a, D), 0.02),
        "cm_w_out": nrm((n_a, D, D), D ** -0.5),
        "cm_b_out": nrm((n_a, D), 0.02),
        "attn_w_qkv": nrm((n_b, D, qkv_dim), D ** -0.5),
        "attn_sink": nrm((n_b, N_Q_HEADS), 0.5),
        "attn_w_o": nrm((n_b, N_Q_HEADS * HEAD_DIM, D), D ** -0.5),
        "gm_w_in": nrm((n_c, D, 2 * GMLP_WIDTH), D ** -0.5),
        "gm_b_in": nrm((n_c, 2 * GMLP_WIDTH), 0.02),
        "gm_ln_g": 1.0 + nrm((n_c, GMLP_WIDTH), 0.02),
        "gm_ln_b": nrm((n_c, GMLP_WIDTH), 0.02),
        "gm_w_s": nrm((n_c, GMLP_GROUPS, GMLP_CHUNK, GMLP_CHUNK), GMLP_CHUNK ** -0.5),
        "gm_b_s": 1.0 + nrm((n_c, GMLP_GROUPS, GMLP_CHUNK), 0.02),
        "gm_w_out": nrm((n_c, GMLP_WIDTH, D), GMLP_WIDTH ** -0.5),
    }


def _fwd_reference(x, c, ctx, c_ctx, ada_w, ada_b, norm_g, ffn_w_up, ffn_conv_w, ffn_conv_b, ffn_w_down,
              cm_w_in, cm_b_in, cm_dw_w, cm_dw_b, cm_ln_g, cm_ln_b, cm_w_out, cm_b_out,
              attn_w_qkv, attn_sink, attn_w_o,
              gm_w_in, gm_b_in, gm_ln_g, gm_ln_b, gm_w_s, gm_b_s, gm_w_out):
    length = x.shape[1]
    ROWS = length // GRID_W
    cos, sin = axial_rope_tables(ROWS)
    ctx_readers = [i for i in range(DEPTH) if i % N_MIXERS == 1]
    last_reader = max(ctx_readers) if ctx_readers else -1

    h, hc = x, ctx
    for i in range(DEPTH):
        kind, j = i % N_MIXERS, i // N_MIXERS
        use_ctx = i <= last_reader
        ctx_full = i < last_reader

        sh1, sc1, g1, sh2, sc2, g2 = adaln_params(c, ada_w[i], ada_b[i])
        a = rms_norm(h, norm_g[i, 0]) * (1.0 + sc1) + sh1
        if use_ctx:
            csh1, csc1, cg1, csh2, csc2, cg2 = adaln_params(c_ctx[None, :], ada_w[i], ada_b[i])
            ac = rms_norm(hc, norm_g[i, 0]) * (1.0 + csc1) + csh1

        if kind == 0:
            cm = (cm_w_in[j], cm_b_in[j], cm_dw_w[j], cm_dw_b[j], cm_ln_g[j], cm_ln_b[j], cm_w_out[j], cm_b_out[j])
            y = conformer_conv(a, *cm)
            yc = conformer_conv(ac, *cm) if ctx_full else None
        elif kind == 1:
            y, yc = windowed_gqa(a, ac, attn_w_qkv[j], attn_sink[j], attn_w_o[j], cos, sin, ctx_full)
        else:
            gm = (gm_w_in[j], gm_b_in[j], gm_ln_g[j], gm_ln_b[j], gm_w_s[j], gm_b_s[j], gm_w_out[j])
            y = chunk_gmlp(a, *gm)
            yc = chunk_gmlp(ac, *gm) if ctx_full else None

        h = h + g1 * rms_norm(y, norm_g[i, 1])
        f = conv_ffn(rms_norm(h, norm_g[i, 2]) * (1.0 + sc2) + sh2,
                     ffn_w_up[i], ffn_conv_w[i], ffn_conv_b[i], ffn_w_down[i])
        h = h + g2 * rms_norm(f, norm_g[i, 3])

        if ctx_full:
            hc = hc + cg1 * rms_norm(yc, norm_g[i, 1])
            fc = conv_ffn(rms_norm(hc, norm_g[i, 2]) * (1.0 + csc2) + csh2,
                          ffn_w_up[i], ffn_conv_w[i], ffn_conv_b[i], ffn_w_down[i])
            hc = hc + cg2 * rms_norm(fc, norm_g[i, 3])
    return h


import jax as _jax
import jax.numpy as _jnp

TWIN_FORMAT = 'train_step'
FWD_PARAMS = ['x', 'c', 'ctx', 'c_ctx', 'ada_w', 'ada_b', 'norm_g', 'ffn_w_up', 'ffn_conv_w', 'ffn_conv_b', 'ffn_w_down', 'cm_w_in', 'cm_b_in', 'cm_dw_w', 'cm_dw_b', 'cm_ln_g', 'cm_ln_b', 'cm_w_out', 'cm_b_out', 'attn_w_qkv', 'attn_sink', 'attn_w_o', 'gm_w_in', 'gm_b_in', 'gm_ln_g', 'gm_ln_b', 'gm_w_s', 'gm_b_s', 'gm_w_out']
TWIN_WEIGHTS = ['c_ctx', 'ada_w', 'ada_b', 'norm_g', 'ffn_w_up', 'ffn_conv_w', 'ffn_conv_b', 'ffn_w_down', 'cm_w_in', 'cm_b_in', 'cm_dw_w', 'cm_dw_b', 'cm_ln_g', 'cm_ln_b', 'cm_w_out', 'cm_b_out', 'attn_w_qkv', 'attn_sink', 'attn_w_o', 'gm_w_in', 'gm_b_in', 'gm_ln_g', 'gm_ln_b', 'gm_w_s', 'gm_b_s', 'gm_w_out']
TWIN_DIFF_INPUT = 'x'
TWIN_INPUTS = ['x', 'c', 'ctx', 'c_ctx', 'ada_w', 'ada_b', 'norm_g', 'ffn_w_up', 'ffn_conv_w', 'ffn_conv_b', 'ffn_w_down', 'cm_w_in', 'cm_b_in', 'cm_dw_w', 'cm_dw_b', 'cm_ln_g', 'cm_ln_b', 'cm_w_out', 'cm_b_out', 'attn_w_qkv', 'attn_sink', 'attn_w_o', 'gm_w_in', 'gm_b_in', 'gm_ln_g', 'gm_ln_b', 'gm_w_s', 'gm_b_s', 'gm_w_out', 'loss_target', 'm_c_ctx', 'm_ada_w', 'm_ada_b', 'm_norm_g', 'm_ffn_w_up', 'm_ffn_conv_w', 'm_ffn_conv_b', 'm_ffn_w_down', 'm_cm_w_in', 'm_cm_b_in', 'm_cm_dw_w', 'm_cm_dw_b', 'm_cm_ln_g', 'm_cm_ln_b', 'm_cm_w_out', 'm_cm_b_out', 'm_attn_w_qkv', 'm_attn_sink', 'm_attn_w_o', 'm_gm_w_in', 'm_gm_b_in', 'm_gm_ln_g', 'm_gm_ln_b', 'm_gm_w_s', 'm_gm_b_s', 'm_gm_w_out', 'v_c_ctx', 'v_ada_w', 'v_ada_b', 'v_norm_g', 'v_ffn_w_up', 'v_ffn_conv_w', 'v_ffn_conv_b', 'v_ffn_w_down', 'v_cm_w_in', 'v_cm_b_in', 'v_cm_dw_w', 'v_cm_dw_b', 'v_cm_ln_g', 'v_cm_ln_b', 'v_cm_w_out', 'v_cm_b_out', 'v_attn_w_qkv', 'v_attn_sink', 'v_attn_w_o', 'v_gm_w_in', 'v_gm_b_in', 'v_gm_ln_g', 'v_gm_ln_b', 'v_gm_w_s', 'v_gm_b_s', 'v_gm_w_out']
TWIN_OUTPUTS = ['loss', 'grad_x', 'grad_c_ctx', 'grad_ada_w', 'grad_ada_b', 'grad_norm_g', 'grad_ffn_w_up', 'grad_ffn_conv_w', 'grad_ffn_conv_b', 'grad_ffn_w_down', 'grad_cm_w_in', 'grad_cm_b_in', 'grad_cm_dw_w', 'grad_cm_dw_b', 'grad_cm_ln_g', 'grad_cm_ln_b', 'grad_cm_w_out', 'grad_cm_b_out', 'grad_attn_w_qkv', 'grad_attn_sink', 'grad_attn_w_o', 'grad_gm_w_in', 'grad_gm_b_in', 'grad_gm_ln_g', 'grad_gm_ln_b', 'grad_gm_w_s', 'grad_gm_b_s', 'grad_gm_w_out', 'delta_c_ctx', 'delta_ada_w', 'delta_ada_b', 'delta_norm_g', 'delta_ffn_w_up', 'delta_ffn_conv_w', 'delta_ffn_conv_b', 'delta_ffn_w_down', 'delta_cm_w_in', 'delta_cm_b_in', 'delta_cm_dw_w', 'delta_cm_dw_b', 'delta_cm_ln_g', 'delta_cm_ln_b', 'delta_cm_w_out', 'delta_cm_b_out', 'delta_attn_w_qkv', 'delta_attn_sink', 'delta_attn_w_o', 'delta_gm_w_in', 'delta_gm_b_in', 'delta_gm_ln_g', 'delta_gm_ln_b', 'delta_gm_w_s', 'delta_gm_b_s', 'delta_gm_w_out', 'new_m_c_ctx', 'new_m_ada_w', 'new_m_ada_b', 'new_m_norm_g', 'new_m_ffn_w_up', 'new_m_ffn_conv_w', 'new_m_ffn_conv_b', 'new_m_ffn_w_down', 'new_m_cm_w_in', 'new_m_cm_b_in', 'new_m_cm_dw_w', 'new_m_cm_dw_b', 'new_m_cm_ln_g', 'new_m_cm_ln_b', 'new_m_cm_w_out', 'new_m_cm_b_out', 'new_m_attn_w_qkv', 'new_m_attn_sink', 'new_m_attn_w_o', 'new_m_gm_w_in', 'new_m_gm_b_in', 'new_m_gm_ln_g', 'new_m_gm_ln_b', 'new_m_gm_w_s', 'new_m_gm_b_s', 'new_m_gm_w_out', 'new_v_c_ctx', 'new_v_ada_w', 'new_v_ada_b', 'new_v_norm_g', 'new_v_ffn_w_up', 'new_v_ffn_conv_w', 'new_v_ffn_conv_b', 'new_v_ffn_w_down', 'new_v_cm_w_in', 'new_v_cm_b_in', 'new_v_cm_dw_w', 'new_v_cm_dw_b', 'new_v_cm_ln_g', 'new_v_cm_ln_b', 'new_v_cm_w_out', 'new_v_cm_b_out', 'new_v_attn_w_qkv', 'new_v_attn_sink', 'new_v_attn_w_o', 'new_v_gm_w_in', 'new_v_gm_b_in', 'new_v_gm_ln_g', 'new_v_gm_ln_b', 'new_v_gm_w_s', 'new_v_gm_b_s', 'new_v_gm_w_out']
TWIN_LEAF_KINDS = {'loss': 'loss', 'grad_x': 'grad_x', 'grad_c_ctx': 'grad_w', 'grad_ada_w': 'grad_w', 'grad_ada_b': 'grad_w', 'grad_norm_g': 'grad_w', 'grad_ffn_w_up': 'grad_w', 'grad_ffn_conv_w': 'grad_w', 'grad_ffn_conv_b': 'grad_w', 'grad_ffn_w_down': 'grad_w', 'grad_cm_w_in': 'grad_w', 'grad_cm_b_in': 'grad_w', 'grad_cm_dw_w': 'grad_w', 'grad_cm_dw_b': 'grad_w', 'grad_cm_ln_g': 'grad_w', 'grad_cm_ln_b': 'grad_w', 'grad_cm_w_out': 'grad_w', 'grad_cm_b_out': 'grad_w', 'grad_attn_w_qkv': 'grad_w', 'grad_attn_sink': 'grad_w', 'grad_attn_w_o': 'grad_w', 'grad_gm_w_in': 'grad_w', 'grad_gm_b_in': 'grad_w', 'grad_gm_ln_g': 'grad_w', 'grad_gm_ln_b': 'grad_w', 'grad_gm_w_s': 'grad_w', 'grad_gm_b_s': 'grad_w', 'grad_gm_w_out': 'grad_w', 'delta_c_ctx': 'delta_w', 'delta_ada_w': 'delta_w', 'delta_ada_b': 'delta_w', 'delta_norm_g': 'delta_w', 'delta_ffn_w_up': 'delta_w', 'delta_ffn_conv_w': 'delta_w', 'delta_ffn_conv_b': 'delta_w', 'delta_ffn_w_down': 'delta_w', 'delta_cm_w_in': 'delta_w', 'delta_cm_b_in': 'delta_w', 'delta_cm_dw_w': 'delta_w', 'delta_cm_dw_b': 'delta_w', 'delta_cm_ln_g': 'delta_w', 'delta_cm_ln_b': 'delta_w', 'delta_cm_w_out': 'delta_w', 'delta_cm_b_out': 'delta_w', 'delta_attn_w_qkv': 'delta_w', 'delta_attn_sink': 'delta_w', 'delta_attn_w_o': 'delta_w', 'delta_gm_w_in': 'delta_w', 'delta_gm_b_in': 'delta_w', 'delta_gm_ln_g': 'delta_w', 'delta_gm_ln_b': 'delta_w', 'delta_gm_w_s': 'delta_w', 'delta_gm_b_s': 'delta_w', 'delta_gm_w_out': 'delta_w', 'new_m_c_ctx': 'new_m', 'new_m_ada_w': 'new_m', 'new_m_ada_b': 'new_m', 'new_m_norm_g': 'new_m', 'new_m_ffn_w_up': 'new_m', 'new_m_ffn_conv_w': 'new_m', 'new_m_ffn_conv_b': 'new_m', 'new_m_ffn_w_down': 'new_m', 'new_m_cm_w_in': 'new_m', 'new_m_cm_b_in': 'new_m', 'new_m_cm_dw_w': 'new_m', 'new_m_cm_dw_b': 'new_m', 'new_m_cm_ln_g': 'new_m', 'new_m_cm_ln_b': 'new_m', 'new_m_cm_w_out': 'new_m', 'new_m_cm_b_out': 'new_m', 'new_m_attn_w_qkv': 'new_m', 'new_m_attn_sink': 'new_m', 'new_m_attn_w_o': 'new_m', 'new_m_gm_w_in': 'new_m', 'new_m_gm_b_in': 'new_m', 'new_m_gm_ln_g': 'new_m', 'new_m_gm_ln_b': 'new_m', 'new_m_gm_w_s': 'new_m', 'new_m_gm_b_s': 'new_m', 'new_m_gm_w_out': 'new_m', 'new_v_c_ctx': 'new_v', 'new_v_ada_w': 'new_v', 'new_v_ada_b': 'new_v', 'new_v_norm_g': 'new_v', 'new_v_ffn_w_up': 'new_v', 'new_v_ffn_conv_w': 'new_v', 'new_v_ffn_conv_b': 'new_v', 'new_v_ffn_w_down': 'new_v', 'new_v_cm_w_in': 'new_v', 'new_v_cm_b_in': 'new_v', 'new_v_cm_dw_w': 'new_v', 'new_v_cm_dw_b': 'new_v', 'new_v_cm_ln_g': 'new_v', 'new_v_cm_ln_b': 'new_v', 'new_v_cm_w_out': 'new_v', 'new_v_cm_b_out': 'new_v', 'new_v_attn_w_qkv': 'new_v', 'new_v_attn_sink': 'new_v', 'new_v_attn_w_o': 'new_v', 'new_v_gm_w_in': 'new_v', 'new_v_gm_b_in': 'new_v', 'new_v_gm_ln_g': 'new_v', 'new_v_gm_ln_b': 'new_v', 'new_v_gm_w_s': 'new_v', 'new_v_gm_b_s': 'new_v', 'new_v_gm_w_out': 'new_v'}


def _forward(args):
    return _fwd_reference(*[args[k] for k in FWD_PARAMS])


def _output_shape():
    out = _jax.eval_shape(lambda: _forward(_fwd_setup_inputs(0)))
    return out.shape, out.dtype

N_MICROBATCH = 1
ADAM_LR = 0.001
ADAM_B1 = 0.9
ADAM_B2 = 0.999
ADAM_EPS = 1e-08
ADAM_WD = 0.01
ADAM_STEP = 10
PER_EXAMPLE_BATCH_AXIS = {'x': 0, 'c': 0, 'ctx': 0, 'loss_target': 0}
SHARED_INPUTS = []
_WEIGHT_DTYPES = {'c_ctx': _jnp.float32, 'ada_w': _jnp.float32, 'ada_b': _jnp.float32, 'norm_g': _jnp.float32, 'ffn_w_up': _jnp.float32, 'ffn_conv_w': _jnp.float32, 'ffn_conv_b': _jnp.float32, 'ffn_w_down': _jnp.float32, 'cm_w_in': _jnp.float32, 'cm_b_in': _jnp.float32, 'cm_dw_w': _jnp.float32, 'cm_dw_b': _jnp.float32, 'cm_ln_g': _jnp.float32, 'cm_ln_b': _jnp.float32, 'cm_w_out': _jnp.float32, 'cm_b_out': _jnp.float32, 'attn_w_qkv': _jnp.float32, 'attn_sink': _jnp.float32, 'attn_w_o': _jnp.float32, 'gm_w_in': _jnp.float32, 'gm_b_in': _jnp.float32, 'gm_ln_g': _jnp.float32, 'gm_ln_b': _jnp.float32, 'gm_w_s': _jnp.float32, 'gm_b_s': _jnp.float32, 'gm_w_out': _jnp.float32}
MOMENT_SCALE = {'c_ctx': 7.035781e-01, 'ada_w': 1.201120e+00, 'ada_b': 2.235392e+00, 'norm_g': 1.471001e+00, 'ffn_w_up': 9.772756e-02, 'ffn_conv_w': 1.060258e-01, 'ffn_conv_b': 2.507294e-01, 'ffn_w_down': 1.938973e-01, 'cm_w_in': 2.222868e-01, 'cm_b_in': 7.130201e-01, 'cm_dw_w': 3.202506e-01, 'cm_dw_b': 1.749244e+00, 'cm_ln_g': 8.016887e-01, 'cm_ln_b': 1.056733e+00, 'cm_w_out': 5.305440e-01, 'cm_b_out': 2.081688e+00, 'attn_w_qkv': 9.229239e-01, 'attn_sink': 1.357059e-02, 'attn_w_o': 1.174578e+00, 'gm_w_in': 1.072976e-01, 'gm_b_in': 2.733246e-01, 'gm_ln_g': 4.124972e-02, 'gm_ln_b': 4.382100e-02, 'gm_w_s': 3.925090e-02, 'gm_b_s': 3.929086e-02, 'gm_w_out': 3.867566e-01}


def _to_microbatches(a, axis):
    t = _jnp.moveaxis(a, axis, 0)
    t = t.reshape((N_MICROBATCH, t.shape[0] // N_MICROBATCH) + t.shape[1:])
    return _jnp.moveaxis(t, 1, axis + 1)


def setup_inputs(seed: int = 0) -> dict:
    inp = _fwd_setup_inputs(seed)
    key = _jax.random.fold_in(_jax.random.key(seed), 7919)
    shape, _ = _output_shape()
    out = dict(inp)
    out["loss_target"] = _jax.random.normal(_jax.random.fold_in(key, 0), shape, _jnp.float32)
    for i, name in enumerate(TWIN_WEIGHTS):
        w = inp[name].astype(_jnp.float32)
        if MOMENT_SCALE is None:
            s = _jnp.sqrt(_jnp.mean(_jnp.square(w)) + 1e-30)
        else:
            s = MOMENT_SCALE[name]
        km, kv = _jax.random.split(_jax.random.fold_in(key, i + 1))
        out[name] = w
        out["m_" + name] = s * _jax.random.normal(km, w.shape, _jnp.float32)
        out["v_" + name] = (s * s) * _jax.random.uniform(kv, w.shape, _jnp.float32, 0.5, 1.5)
    if N_MICROBATCH > 1:
        for name, axis in PER_EXAMPLE_BATCH_AXIS.items():
            out[name] = _to_microbatches(out[name], axis)
    return {'x': out['x'], 'c': out['c'], 'ctx': out['ctx'], 'c_ctx': out['c_ctx'], 'ada_w': out['ada_w'], 'ada_b': out['ada_b'], 'norm_g': out['norm_g'], 'ffn_w_up': out['ffn_w_up'], 'ffn_conv_w': out['ffn_conv_w'], 'ffn_conv_b': out['ffn_conv_b'], 'ffn_w_down': out['ffn_w_down'], 'cm_w_in': out['cm_w_in'], 'cm_b_in': out['cm_b_in'], 'cm_dw_w': out['cm_dw_w'], 'cm_dw_b': out['cm_dw_b'], 'cm_ln_g': out['cm_ln_g'], 'cm_ln_b': out['cm_ln_b'], 'cm_w_out': out['cm_w_out'], 'cm_b_out': out['cm_b_out'], 'attn_w_qkv': out['attn_w_qkv'], 'attn_sink': out['attn_sink'], 'attn_w_o': out['attn_w_o'], 'gm_w_in': out['gm_w_in'], 'gm_b_in': out['gm_b_in'], 'gm_ln_g': out['gm_ln_g'], 'gm_ln_b': out['gm_ln_b'], 'gm_w_s': out['gm_w_s'], 'gm_b_s': out['gm_b_s'], 'gm_w_out': out['gm_w_out'], 'loss_target': out['loss_target'], 'm_c_ctx': out['m_c_ctx'], 'm_ada_w': out['m_ada_w'], 'm_ada_b': out['m_ada_b'], 'm_norm_g': out['m_norm_g'], 'm_ffn_w_up': out['m_ffn_w_up'], 'm_ffn_conv_w': out['m_ffn_conv_w'], 'm_ffn_conv_b': out['m_ffn_conv_b'], 'm_ffn_w_down': out['m_ffn_w_down'], 'm_cm_w_in': out['m_cm_w_in'], 'm_cm_b_in': out['m_cm_b_in'], 'm_cm_dw_w': out['m_cm_dw_w'], 'm_cm_dw_b': out['m_cm_dw_b'], 'm_cm_ln_g': out['m_cm_ln_g'], 'm_cm_ln_b': out['m_cm_ln_b'], 'm_cm_w_out': out['m_cm_w_out'], 'm_cm_b_out': out['m_cm_b_out'], 'm_attn_w_qkv': out['m_attn_w_qkv'], 'm_attn_sink': out['m_attn_sink'], 'm_attn_w_o': out['m_attn_w_o'], 'm_gm_w_in': out['m_gm_w_in'], 'm_gm_b_in': out['m_gm_b_in'], 'm_gm_ln_g': out['m_gm_ln_g'], 'm_gm_ln_b': out['m_gm_ln_b'], 'm_gm_w_s': out['m_gm_w_s'], 'm_gm_b_s': out['m_gm_b_s'], 'm_gm_w_out': out['m_gm_w_out'], 'v_c_ctx': out['v_c_ctx'], 'v_ada_w': out['v_ada_w'], 'v_ada_b': out['v_ada_b'], 'v_norm_g': out['v_norm_g'], 'v_ffn_w_up': out['v_ffn_w_up'], 'v_ffn_conv_w': out['v_ffn_conv_w'], 'v_ffn_conv_b': out['v_ffn_conv_b'], 'v_ffn_w_down': out['v_ffn_w_down'], 'v_cm_w_in': out['v_cm_w_in'], 'v_cm_b_in': out['v_cm_b_in'], 'v_cm_dw_w': out['v_cm_dw_w'], 'v_cm_dw_b': out['v_cm_dw_b'], 'v_cm_ln_g': out['v_cm_ln_g'], 'v_cm_ln_b': out['v_cm_ln_b'], 'v_cm_w_out': out['v_cm_w_out'], 'v_cm_b_out': out['v_cm_b_out'], 'v_attn_w_qkv': out['v_attn_w_qkv'], 'v_attn_sink': out['v_attn_sink'], 'v_attn_w_o': out['v_attn_w_o'], 'v_gm_w_in': out['v_gm_w_in'], 'v_gm_b_in': out['v_gm_b_in'], 'v_gm_ln_g': out['v_gm_ln_g'], 'v_gm_ln_b': out['v_gm_ln_b'], 'v_gm_w_s': out['v_gm_w_s'], 'v_gm_b_s': out['v_gm_b_s'], 'v_gm_w_out': out['v_gm_w_out']}


def _loss(weights, diff, rest, loss_target):
    with _jax.named_scope("forward"):
        args = {**rest, TWIN_DIFF_INPUT: diff, **{k: w.astype(_WEIGHT_DTYPES[k]) for k, w in weights.items()}}
        y = _forward(args)
    with _jax.named_scope("loss_head"):
        err = _jnp.square(y.astype(_jnp.float32) - loss_target)
        return 0.5 * _jnp.sum(_jnp.mean(err, axis=-1)) if err.ndim else 0.5 * err


def _adamw(w, g, m, v):
    m = ADAM_B1 * m + (1.0 - ADAM_B1) * g
    v = ADAM_B2 * v + (1.0 - ADAM_B2) * _jnp.square(g)
    m_hat = m / (1.0 - ADAM_B1 ** ADAM_STEP)
    v_hat = v / (1.0 - ADAM_B2 ** ADAM_STEP)
    delta = -ADAM_LR * (m_hat / (_jnp.sqrt(v_hat) + ADAM_EPS) + ADAM_WD * w)
    return delta, m, v


def reference(x, c, ctx, c_ctx, ada_w, ada_b, norm_g, ffn_w_up, ffn_conv_w, ffn_conv_b, ffn_w_down, cm_w_in, cm_b_in, cm_dw_w, cm_dw_b, cm_ln_g, cm_ln_b, cm_w_out, cm_b_out, attn_w_qkv, attn_sink, attn_w_o, gm_w_in, gm_b_in, gm_ln_g, gm_ln_b, gm_w_s, gm_b_s, gm_w_out, loss_target, m_c_ctx, m_ada_w, m_ada_b, m_norm_g, m_ffn_w_up, m_ffn_conv_w, m_ffn_conv_b, m_ffn_w_down, m_cm_w_in, m_cm_b_in, m_cm_dw_w, m_cm_dw_b, m_cm_ln_g, m_cm_ln_b, m_cm_w_out, m_cm_b_out, m_attn_w_qkv, m_attn_sink, m_attn_w_o, m_gm_w_in, m_gm_b_in, m_gm_ln_g, m_gm_ln_b, m_gm_w_s, m_gm_b_s, m_gm_w_out, v_c_ctx, v_ada_w, v_ada_b, v_norm_g, v_ffn_w_up, v_ffn_conv_w, v_ffn_conv_b, v_ffn_w_down, v_cm_w_in, v_cm_b_in, v_cm_dw_w, v_cm_dw_b, v_cm_ln_g, v_cm_ln_b, v_cm_w_out, v_cm_b_out, v_attn_w_qkv, v_attn_sink, v_attn_w_o, v_gm_w_in, v_gm_b_in, v_gm_ln_g, v_gm_ln_b, v_gm_w_s, v_gm_b_s, v_gm_w_out):
    given = dict(x=x, c=c, ctx=ctx, c_ctx=c_ctx, ada_w=ada_w, ada_b=ada_b, norm_g=norm_g, ffn_w_up=ffn_w_up, ffn_conv_w=ffn_conv_w, ffn_conv_b=ffn_conv_b, ffn_w_down=ffn_w_down, cm_w_in=cm_w_in, cm_b_in=cm_b_in, cm_dw_w=cm_dw_w, cm_dw_b=cm_dw_b, cm_ln_g=cm_ln_g, cm_ln_b=cm_ln_b, cm_w_out=cm_w_out, cm_b_out=cm_b_out, attn_w_qkv=attn_w_qkv, attn_sink=attn_sink, attn_w_o=attn_w_o, gm_w_in=gm_w_in, gm_b_in=gm_b_in, gm_ln_g=gm_ln_g, gm_ln_b=gm_ln_b, gm_w_s=gm_w_s, gm_b_s=gm_b_s, gm_w_out=gm_w_out, loss_target=loss_target, m_c_ctx=m_c_ctx, m_ada_w=m_ada_w, m_ada_b=m_ada_b, m_norm_g=m_norm_g, m_ffn_w_up=m_ffn_w_up, m_ffn_conv_w=m_ffn_conv_w, m_ffn_conv_b=m_ffn_conv_b, m_ffn_w_down=m_ffn_w_down, m_cm_w_in=m_cm_w_in, m_cm_b_in=m_cm_b_in, m_cm_dw_w=m_cm_dw_w, m_cm_dw_b=m_cm_dw_b, m_cm_ln_g=m_cm_ln_g, m_cm_ln_b=m_cm_ln_b, m_cm_w_out=m_cm_w_out, m_cm_b_out=m_cm_b_out, m_attn_w_qkv=m_attn_w_qkv, m_attn_sink=m_attn_sink, m_attn_w_o=m_attn_w_o, m_gm_w_in=m_gm_w_in, m_gm_b_in=m_gm_b_in, m_gm_ln_g=m_gm_ln_g, m_gm_ln_b=m_gm_ln_b, m_gm_w_s=m_gm_w_s, m_gm_b_s=m_gm_b_s, m_gm_w_out=m_gm_w_out, v_c_ctx=v_c_ctx, v_ada_w=v_ada_w, v_ada_b=v_ada_b, v_norm_g=v_norm_g, v_ffn_w_up=v_ffn_w_up, v_ffn_conv_w=v_ffn_conv_w, v_ffn_conv_b=v_ffn_conv_b, v_ffn_w_down=v_ffn_w_down, v_cm_w_in=v_cm_w_in, v_cm_b_in=v_cm_b_in, v_cm_dw_w=v_cm_dw_w, v_cm_dw_b=v_cm_dw_b, v_cm_ln_g=v_cm_ln_g, v_cm_ln_b=v_cm_ln_b, v_cm_w_out=v_cm_w_out, v_cm_b_out=v_cm_b_out, v_attn_w_qkv=v_attn_w_qkv, v_attn_sink=v_attn_sink, v_attn_w_o=v_attn_w_o, v_gm_w_in=v_gm_w_in, v_gm_b_in=v_gm_b_in, v_gm_ln_g=v_gm_ln_g, v_gm_ln_b=v_gm_ln_b, v_gm_w_s=v_gm_w_s, v_gm_b_s=v_gm_b_s, v_gm_w_out=v_gm_w_out)
    weights = {n: given[n] for n in TWIN_WEIGHTS}
    shared = {n: given[n] for n in SHARED_INPUTS}
    per_example = {n: given[n] for n in ['x', 'c', 'ctx']}
    grad_fn = _jax.value_and_grad(_loss, argnums=(0, 1))

    def one_microbatch(ex, loss_target):
        ex = dict(ex)
        diff = ex.pop(TWIN_DIFF_INPUT)
        return grad_fn(weights, diff, {**shared, **ex}, loss_target)

    if N_MICROBATCH == 1:
        loss, (grad_w, grad_x) = one_microbatch(per_example, given["loss_target"])
    else:
        def body(carry, xs):
            loss_sum, grad_sum = carry
            l_k, (gw_k, gx_k) = one_microbatch(xs[0], xs[1])
            with _jax.named_scope("update"):
                return (loss_sum + l_k, _jax.tree.map(_jnp.add, grad_sum, gw_k)), gx_k

        init = (_jnp.zeros((), _jnp.float32), _jax.tree.map(_jnp.zeros_like, weights))
        (loss, grad_w), grad_x = _jax.lax.scan(body, init, (per_example, given["loss_target"]))
    with _jax.named_scope("update"):
        delta_w, new_m, new_v = {}, {}, {}
        for n in TWIN_WEIGHTS:
            delta_w[n], new_m[n], new_v[n] = _adamw(weights[n], grad_w[n], given["m_" + n], given["v_" + n])
    return (loss, grad_x, *[grad_w[n] for n in TWIN_WEIGHTS], *[delta_w[n] for n in TWIN_WEIGHTS],
            *[new_m[n] for n in TWIN_WEIGHTS], *[new_v[n] for n in TWIN_WEIGHTS])
```

```python
import functools

import jax
import jax.numpy as jnp
from jax import lax
from jax.experimental import pallas as pl
from jax.experimental.pallas import tpu as pltpu

F32, BF16 = jnp.float32, jnp.bfloat16
MESH = pl.DeviceIdType.MESH
AXES = ("x", "y", "c")
NDEV = 8

D = 1024
L = 2048
LC = 256
TA = L + LC
DEPTH = 4
EPS = 1e-6
HEAD_DIM = 64
N_Q, N_KV, Q_PER_KV = 16, 4, 4
ATTN_BLOCK = 128
GRID_W = 64
ROPE_BASE = 10000.0
GM_W = 2048
GM_CHUNK = 128
GM_GROUPS = 16
FFN_BLK = 704
CM_K, FFN_K = 31, 3

ADAM_LR, ADAM_B1, ADAM_B2, ADAM_EPS, ADAM_WD, ADAM_STEP = 0.001, 0.9, 0.999, 1e-08, 0.01, 10

VMEM_LIMIT_V7X = 56 * 1024 * 1024
ROW_TILE_ELEMS = 256 * 1024


def _cparams(sem=None):
    kw = dict(vmem_limit_bytes=VMEM_LIMIT_V7X)
    if sem is not None:
        kw["dimension_semantics"] = sem
    return pltpu.CompilerParams(**kw)


def _pick(n, cands):
    for c in cands:
        if n % c == 0:
            return c
    return n


def _as3(a):
    return a if a.ndim == 3 else a[None]


def _all_gather(arrs, name):
    n = len(arrs)

    def body(*refs):
        ins, outs = refs[:n], refs[n:2 * n]
        send_sems, recv_sems, local_sems = refs[2 * n:]
        x, y, c = lax.axis_index("x"), lax.axis_index("y"), lax.axis_index("c")
        me, sibling = (x, y, c), (x, y, 1 - c)
        chips = [(1 - x, y), (x, 1 - y), (1 - x, 1 - y)]

        def slot(a, p):
            return outs[a].at[4 * p[0] + 2 * p[1] + p[2]]

        def copy(a, k, block, to, src=None):
            return pltpu.make_async_remote_copy(
                src_ref=slot(a, block) if src is None else src, dst_ref=slot(a, block),
                send_sem=send_sems.at[a, k], recv_sem=recv_sems.at[a, k],
                device_id=to, device_id_type=MESH)

        mine = [pltpu.make_async_copy(ins[a], slot(a, me), local_sems.at[a]) for a in range(n)]
        for m in mine:
            m.start()
        first = []
        for a in range(n):
            first.append(copy(a, 0, me, sibling, src=ins[a]))
            first += [copy(a, 1 + j, me, (*chip, c), src=ins[a]) for j, chip in enumerate(chips)]
        for cp in first:
            cp.start()
        passed = []
        for j, chip in enumerate(chips):
            for a in range(n):
                copy(a, 1 + j, (*chip, c), me).wait_recv()
                p = copy(a, 4 + j, (*chip, c), sibling)
                p.start()
                passed.append(p)
        for a in range(n):
            copy(a, 0, sibling, me).wait_recv()
            for j, chip in enumerate(chips):
                copy(a, 4 + j, (*chip, 1 - c), me).wait_recv()
        for cp in first + passed:
            cp.wait_send()
        for m in mine:
            m.wait()

    any_spec = pl.BlockSpec(memory_space=pl.ANY)
    outs = pl.pallas_call(
        body, name=name,
        out_shape=[jax.ShapeDtypeStruct((NDEV,) + a.shape, a.dtype) for a in arrs],
        in_specs=[any_spec] * n, out_specs=[any_spec] * n,
        scratch_shapes=[pltpu.SemaphoreType.DMA((n, 7)), pltpu.SemaphoreType.DMA((n, 7)),
                        pltpu.SemaphoreType.DMA((n,))],
    )(*arrs)
    return list(outs)


def _all_to_all(groups, name):
    flat = [(gi, li, a) for gi, g in enumerate(groups) for li, a in enumerate(g)]
    n, ng = len(flat), len(groups)

    def body(*refs):
        ins, outs = refs[:n], refs[n:n + ng]
        send_sems, recv_sems, local_sems = refs[n + ng:]
        x, y, c = lax.axis_index("x"), lax.axis_index("y"), lax.axis_index("c")
        me = 4 * x + 2 * y + c
        copies = []
        for a, (gi, li, _) in enumerate(flat):
            loc = pltpu.make_async_copy(ins[a].at[me], outs[gi].at[li, me], local_sems.at[a])
            loc.start()
            copies.append(loc)
            for k in range(1, NDEV):
                px = 1 - x if (k >> 2) & 1 else x
                py = 1 - y if (k >> 1) & 1 else y
                pc = 1 - c if k & 1 else c
                cp = pltpu.make_async_remote_copy(
                    src_ref=ins[a].at[4 * px + 2 * py + pc], dst_ref=outs[gi].at[li, me],
                    send_sem=send_sems.at[a, k - 1], recv_sem=recv_sems.at[a, k - 1],
                    device_id=(px, py, pc), device_id_type=MESH)
                cp.start()
                copies.append(cp)
        for cp in copies:
            cp.wait()

    any_spec = pl.BlockSpec(memory_space=pl.ANY)
    outs = pl.pallas_call(
        body, name=name,
        out_shape=[jax.ShapeDtypeStruct((len(g),) + g[0].shape, g[0].dtype) for g in groups],
        in_specs=[any_spec] * n, out_specs=[any_spec] * ng,
        scratch_shapes=[pltpu.SemaphoreType.DMA((n, 7)), pltpu.SemaphoreType.DMA((n, 7)),
                        pltpu.SemaphoreType.DMA((n,))],
    )(*[a for _, _, a in flat])
    return list(outs)


def _mm(a, b, kind, *, name, out_dtype=F32, reduce_blocks=False):
    a, b = _as3(a), _as3(b)
    nba, nbb = a.shape[0], b.shape[0]
    nb = max(nba, nbb)
    assert nba in (1, nb) and nbb in (1, nb)
    if kind == "tn":
        t, m = a.shape[1:]
        n = b.shape[2]
        assert b.shape[1] == t and not reduce_blocks
        tk = _pick(t, (512, 768, 256))
        tm = m if m <= 1024 else _pick(m, (1024,))
        nred = t // tk
    else:
        m, k = a.shape[1:]
        n = b.shape[2] if kind == "nn" else b.shape[1]
        assert (b.shape[1] if kind == "nn" else b.shape[2]) == k
        tm = _pick(m, (512, 768, 256))
        nred = nb if reduce_blocks else 1
    tn = n if n <= 1024 else _pick(n, (512,))
    nbo = 1 if reduce_blocks else nb

    def blk(nbx, g, r):
        if nbx == 1:
            return 0
        return r if reduce_blocks else g

    if kind == "nn":
        a_spec = pl.BlockSpec((1, tm, k), lambda g, j, i, r: (blk(nba, g, r), i, 0))
        b_spec = pl.BlockSpec((1, k, tn), lambda g, j, i, r: (blk(nbb, g, r), 0, j))
        dims = (((1,), (0,)), ((), ()))
    elif kind == "nt":
        a_spec = pl.BlockSpec((1, tm, k), lambda g, j, i, r: (blk(nba, g, r), i, 0))
        b_spec = pl.BlockSpec((1, tn, k), lambda g, j, i, r: (blk(nbb, g, r), j, 0))
        dims = (((1,), (1,)), ((), ()))
    else:
        a_spec = pl.BlockSpec((1, tk, tm), lambda g, j, i, r: (blk(nba, g, r), r, i))
        b_spec = pl.BlockSpec((1, tk, tn), lambda g, j, i, r: (blk(nbb, g, r), r, j))
        dims = (((0,), (0,)), ((), ()))
    o_spec = pl.BlockSpec((1, tm, tn), lambda g, j, i, r: (g, i, j))

    def body(a_ref, b_ref, o_ref, *scratch):
        prod = lax.dot_general(a_ref[0].astype(BF16), b_ref[0].astype(BF16), dims,
                               preferred_element_type=F32)
        if nred == 1:
            o_ref[0] = prod.astype(o_ref.dtype)
        else:
            acc = scratch[0]
            r = pl.program_id(3)

            @pl.when(r == 0)
            def _():
                acc[...] = prod

            @pl.when(r > 0)
            def _():
                acc[...] += prod

            @pl.when(r == nred - 1)
            def _():
                o_ref[0] = acc[...].astype(o_ref.dtype)

    return pl.pallas_call(
        body, name=name,
        out_shape=jax.ShapeDtypeStruct((nbo, m, n), out_dtype),
        grid=(nbo, n // tn, m // tm, nred),
        in_specs=[a_spec, b_spec], out_specs=o_spec,
        scratch_shapes=[pltpu.VMEM((tm, tn), F32)] if nred > 1 else [],
        compiler_params=_cparams(("parallel", "parallel", "parallel", "arbitrary")),
    )(a, b)


def _row_tile(t, widths):
    tm = max(16, ROW_TILE_ELEMS // max(widths))
    tm = min(tm, 256)
    return t if t < tm else tm


def _sel_index(sel, g, i, tm):
    if sel == "one":
        return 0
    if sel == "seg":
        return (i * tm) // L
    return g + sel


def _row_spec(arr, off, tm):
    return pl.BlockSpec((1, tm, arr.shape[2]), lambda g, i: (g + off, i, 0))


def _par_spec(arr, sel, tm):
    return pl.BlockSpec((1, 1, arr.shape[2]), lambda g, i: (_sel_index(sel, g, i, tm), 0, 0))


def _norm_ops(ops):
    return [(o[0], o[1], o[2] if len(o) > 2 else 1) for o in ops]


def _split_cols(vals, nsplit):
    out = []
    for v, ns in zip(vals, nsplit):
        w = v.shape[1] // ns
        out += [v] if ns == 1 else [v[:, q * w:(q + 1) * w] for q in range(ns)]
    return out


def _join_cols(flat, nsplit):
    out, pos = [], 0
    for ns in nsplit:
        out.append(flat[pos] if ns == 1 else jnp.concatenate(flat[pos:pos + ns], axis=1))
        pos += ns
    return out


def _rw_fwd(fn, rows, params, outs, *, name, nblk=None):
    rows, params = _norm_ops(rows), _norm_ops(params)
    t = rows[0][0].shape[1]
    nblk = nblk or rows[0][0].shape[0]
    tm = _row_tile(t, [r.shape[2] for r, _, _ in rows] + [w for w, _ in outs])
    nr, npar = len(rows), len(params)
    nsplit = [ns for _, _, ns in rows + params]

    def body(*refs):
        vals = _split_cols([r[0].astype(F32) for r in refs[:nr + npar]], nsplit)
        res = fn(*vals)
        for o_ref, o in zip(refs[nr + npar:], res):
            o_ref[0] = o.astype(o_ref.dtype)

    res = pl.pallas_call(
        body, name=name,
        out_shape=[jax.ShapeDtypeStruct((nblk, t, w), dt) for w, dt in outs],
        grid=(nblk, t // tm),
        in_specs=[_row_spec(r, off, tm) for r, off, _ in rows] + [_par_spec(p, s, tm) for p, s, _ in params],
        out_specs=[pl.BlockSpec((1, tm, w), lambda g, i: (g, i, 0)) for w, _ in outs],
        compiler_params=_cparams(("parallel", "parallel")),
    )(*[r for r, _, _ in rows], *[p for p, _, _ in params])
    return list(res)


def _rw_bwd(fn, rows, params, cts, *, name, row_grad=(), param_grad=(), add=None, nblk=None):
    rows, params = _norm_ops(rows), _norm_ops(params)
    t = cts[0].shape[1]
    nblk = nblk or cts[0].shape[0]
    tm = _row_tile(t, [r.shape[2] for r, _, _ in rows] + [c.shape[2] for c in cts])
    ni = t // tm
    nr, npar, nct = len(rows), len(params), len(cts)
    nadd = 0 if add is None else 1
    n_in = nr + npar + nct + nadd
    nsplit = [ns for _, _, ns in rows + params]

    def body(*refs):
        prim = _split_cols([r[0].astype(F32) for r in refs[:nr + npar]], nsplit)
        ct = tuple(r[0].astype(F32) for r in refs[nr + npar:nr + npar + nct])
        _, vjp = jax.vjp(fn, *prim)
        grads = _join_cols(list(vjp(ct)), nsplit)
        out_refs = refs[n_in:]
        for q, (ri, _) in enumerate(row_grad):
            gr = grads[ri]
            if q == 0 and nadd:
                gr = gr + refs[n_in - 1][0].astype(F32)
            out_refs[q][0] = gr.astype(out_refs[q].dtype)
        g, i = pl.program_id(0), pl.program_id(1)
        step = g * ni + i
        pg, pi = (step - 1) // ni, (step - 1) % ni
        for q, pidx in enumerate(param_grad):
            o_ref = out_refs[len(row_grad) + q]
            sel = params[pidx][1]
            val = grads[nr + pidx]
            if sel == "one":
                first = step == 0
            else:
                first = (step == 0) | (_sel_index(sel, g, i, tm) != _sel_index(sel, pg, pi, tm))

            @pl.when(first)
            def _(o_ref=o_ref, val=val):
                o_ref[0] = val

            @pl.when(jnp.logical_not(first))
            def _(o_ref=o_ref, val=val):
                o_ref[0] += val

    in_arrays = [r for r, _, _ in rows] + [p for p, _, _ in params] + list(cts) + ([add] if nadd else [])
    in_specs = ([_row_spec(r, off, tm) for r, off, _ in rows] + [_par_spec(p, s, tm) for p, s, _ in params]
                + [_row_spec(c, 0, tm) for c in cts] + ([_row_spec(add, 0, tm)] if nadd else []))
    out_shape, out_specs = [], []
    for ri, dt in row_grad:
        w = rows[ri][0].shape[2]
        out_shape.append(jax.ShapeDtypeStruct((nblk, t, w), dt))
        out_specs.append(pl.BlockSpec((1, tm, w), lambda g, i: (g, i, 0)))
    for pidx in param_grad:
        p, sel, _ = params[pidx]
        out_shape.append(jax.ShapeDtypeStruct(p.shape, F32))
        out_specs.append(_par_spec(p, sel, tm))
    res = pl.pallas_call(
        body, name=name, out_shape=out_shape, grid=(nblk, ni),
        in_specs=in_specs, out_specs=out_specs,
        compiler_params=_cparams(("arbitrary", "arbitrary")),
    )(*in_arrays)
    return list(res)


def _sigmoid(x):
    return 1.0 / (1.0 + jnp.exp(-x))


def _rms(x, g):
    return x * lax.rsqrt(jnp.mean(x * x, axis=-1, keepdims=True) + EPS) * g


def _ln(x, g, b):
    mu = jnp.mean(x, axis=-1, keepdims=True)
    xc = x - mu
    var = jnp.mean(xc * xc, axis=-1, keepdims=True)
    return xc * lax.rsqrt(var + EPS) * g + b


def _gelu_tanh(x):
    return 0.5 * x * (1.0 + jnp.tanh(0.7978845608028654 * (x + 0.044715 * (x * x * x))))


def f_modnorm(h, g, sc, sh):
    return (_rms(h, g) * (1.0 + sc) + sh,)


def f_gate_rms(y, gate, g):
    return (gate * _rms(y, g),)


def f_gate_rms_bias(y, gate, g, b):
    return (gate * _rms(y + b, g),)


def f_resgate(h, y, gate, g):
    return (h + gate * _rms(y, g),)


def f_resgate_bias(h, y, gate, g, b):
    return (h + gate * _rms(y + b, g),)


def f_glu(pa, pg, ba, bg):
    return ((pa + ba) * _sigmoid(pg + bg),)


def f_lnsilu(z, g, b):
    t = _ln(z, g, b)
    return (t * _sigmoid(t),)


def f_gmlp_pre(pu, pv, bu, bv, g, bb):
    return _gelu_tanh(pu + bu), _ln(_gelu_tanh(pv + bv), g, bb)


def f_ffn_gate(zg, zv):
    return (zg * _sigmoid(zg) * zv,)


def f_silu(x):
    return (x * _sigmoid(x),)


def f_silu_rows(dummy, cc):
    return (cc * _sigmoid(cc) + 0.0 * dummy,)


def _rope(x_in, tables, neg_sin, out_dtype, name):
    w = x_in.shape[2]
    sign = -1.0 if neg_sin else 1.0

    def fn(x, cos, sin):
        cos = jnp.tile(cos, (1, w // 128))
        sin = jnp.tile(sin, (1, w // 128)) * sign
        lane = lax.broadcasted_iota(jnp.int32, x.shape, 1) & 31
        rot = jnp.where(lane < 16, -pltpu.roll(x, w - 16, 1), pltpu.roll(x, 16, 1))
        return (x * cos + rot * sin,)

    return _rw_fwd(fn, [(x_in, 0), (tables[0], 0), (tables[1], 0)], [], [(w, out_dtype)], name=name)[0]


CONV_TM = 256
CONV_RC = 32


def _conv_geometry(x, k):
    nb, t, w = x.shape
    halo = 16 if k > 17 else 8
    cb = _pick(w, (512,)) if w > 768 else w
    return nb, t, w, halo, cb, (k - 1) // 2


def _conv_in_specs(t, halo, cb):
    per = CONV_TM // halo
    last = t // halo - 1
    return [
        pl.BlockSpec((1, CONV_TM, cb), lambda g, jc, i: (g, i, jc)),
        pl.BlockSpec((1, halo, cb), lambda g, jc, i: (g, jnp.maximum(i * per - 1, 0), jc)),
        pl.BlockSpec((1, halo, cb), lambda g, jc, i: (g, jnp.minimum((i + 1) * per, last), jc)),
    ]


def _conv_fill(xp, x_ref, prev_ref, next_ref, halo, t):
    i = pl.program_id(2)
    seg_first = (i * CONV_TM == 0) | (i * CONV_TM == L)
    seg_last = ((i + 1) * CONV_TM == L) | ((i + 1) * CONV_TM == t)
    xp[0:halo, :] = jnp.where(seg_first, 0.0, prev_ref[0].astype(F32))
    xp[halo:halo + CONV_TM, :] = x_ref[0].astype(F32)
    xp[halo + CONV_TM:, :] = jnp.where(seg_last, 0.0, next_ref[0].astype(F32))


def _dwconv(x, w, b, *, name, out_dtype=F32):
    k = w.shape[1]
    nb, t, wd, halo, cb, half = _conv_geometry(x, k)
    base = halo - half

    def body(*refs):
        x_ref, prev_ref, next_ref, w_ref = refs[:4]
        b_ref = refs[4] if b is not None else None
        o_ref, xp = refs[-2], refs[-1]
        _conv_fill(xp, x_ref, prev_ref, next_ref, halo, t)
        for r0 in range(0, CONV_TM, CONV_RC):
            acc = jnp.zeros((CONV_RC, cb), F32)
            for kk in range(k):
                acc = acc + w_ref[0, kk:kk + 1, :] * xp[r0 + base + kk:r0 + base + kk + CONV_RC, :]
            if b_ref is not None:
                acc = acc + b_ref[0]
            o_ref[0, r0:r0 + CONV_RC, :] = acc.astype(o_ref.dtype)

    in_specs = _conv_in_specs(t, halo, cb) + [pl.BlockSpec((1, k, cb), lambda g, jc, i: (g, 0, jc))]
    args = [x, x, x, w]
    if b is not None:
        in_specs.append(pl.BlockSpec((1, 1, cb), lambda g, jc, i: (g, 0, jc)))
        args.append(b)
    return pl.pallas_call(
        body, name=name, out_shape=jax.ShapeDtypeStruct((nb, t, wd), out_dtype),
        grid=(nb, wd // cb, t // CONV_TM), in_specs=in_specs,
        out_specs=pl.BlockSpec((1, CONV_TM, cb), lambda g, jc, i: (g, i, jc)),
        scratch_shapes=[pltpu.VMEM((CONV_TM + 2 * halo, cb), F32)],
        compiler_params=_cparams(("parallel", "parallel", "parallel")),
    )(*args)


def _dwconv_wgrad(x, dy, k, *, name):
    nb, t, wd, halo, cb, half = _conv_geometry(x, k)
    base = halo - half

    def body(x_ref, prev_ref, next_ref, dy_ref, dw_ref, db_ref, xp):
        _conv_fill(xp, x_ref, prev_ref, next_ref, halo, t)
        i = pl.program_id(2)

        @pl.when(i == 0)
        def _():
            dw_ref[...] = jnp.zeros_like(dw_ref)
            db_ref[...] = jnp.zeros_like(db_ref)

        dyv = dy_ref[0].astype(F32)
        db_ref[0] += jnp.sum(dyv, axis=0, keepdims=True)
        for kk in range(k):
            dw_ref[0, kk:kk + 1, :] += jnp.sum(dyv * xp[base + kk:base + kk + CONV_TM, :], axis=0, keepdims=True)

    dw, db = pl.pallas_call(
        body, name=name,
        out_shape=[jax.ShapeDtypeStruct((nb, k, wd), F32), jax.ShapeDtypeStruct((nb, 1, wd), F32)],
        grid=(nb, wd // cb, t // CONV_TM),
        in_specs=_conv_in_specs(t, halo, cb) + [pl.BlockSpec((1, CONV_TM, cb), lambda g, jc, i: (g, i, jc))],
        out_specs=[pl.BlockSpec((1, k, cb), lambda g, jc, i: (g, 0, jc)),
                   pl.BlockSpec((1, 1, cb), lambda g, jc, i: (g, 0, jc))],
        scratch_shapes=[pltpu.VMEM((CONV_TM + 2 * halo, cb), F32)],
        compiler_params=_cparams(("parallel", "parallel", "arbitrary")),
    )(x, x, x, dy)
    return dw, db


ATTN_SCALE = HEAD_DIM ** -0.5
QROWS = Q_PER_KV * ATTN_BLOCK
NEG = -1e30


def _attn_scores(q, kw, kc, n):
    nt = (((1,), (1,)), ((), ()))
    s_w = lax.dot_general(q, kw, nt, preferred_element_type=F32) * ATTN_SCALE
    qi = lax.broadcasted_iota(jnp.int32, s_w.shape, 0) & (ATTN_BLOCK - 1)
    kj = lax.broadcasted_iota(jnp.int32, s_w.shape, 1)
    key_abs = (n - 1) * ATTN_BLOCK + kj
    ok = (jnp.abs(qi + ATTN_BLOCK - kj) <= ATTN_BLOCK) & (key_abs >= 0) & (key_abs < L)
    s_w = jnp.where(ok, s_w, NEG)
    s_c = lax.dot_general(q, kc, nt, preferred_element_type=F32) * ATTN_SCALE
    return s_w, s_c


def _sink_col(sink_ref, hk):
    return jnp.concatenate([jnp.full((ATTN_BLOCK, 1), sink_ref[hk * Q_PER_KV + g], F32) for g in range(Q_PER_KV)], axis=0)


def _attn_specs():
    qspec = pl.BlockSpec((Q_PER_KV, ATTN_BLOCK, HEAD_DIM), lambda hk, n: (hk, n, 0))
    kspec = pl.BlockSpec((1, L + 2 * ATTN_BLOCK, HEAD_DIM), lambda hk, n: (hk, 0, 0))
    cspec = pl.BlockSpec((1, LC, HEAD_DIM), lambda hk, n: (hk, 0, 0))
    lspec = pl.BlockSpec((Q_PER_KV, ATTN_BLOCK, 1), lambda hk, n: (hk, n, 0))
    sspec = pl.BlockSpec(memory_space=pltpu.SMEM)
    return qspec, kspec, cspec, lspec, sspec


def _attn_fwd(q, k, v, kc, vc, sink):
    qspec, kspec, cspec, lspec, sspec = _attn_specs()

    def body(q_ref, k_ref, v_ref, kc_ref, vc_ref, sink_ref, o_ref, lse_ref):
        hk, n = pl.program_id(0), pl.program_id(1)
        qv = q_ref[...].reshape(QROWS, HEAD_DIM)
        start = pl.multiple_of(n * ATTN_BLOCK, ATTN_BLOCK)
        kw = k_ref[0, pl.ds(start, 3 * ATTN_BLOCK), :]
        vw = v_ref[0, pl.ds(start, 3 * ATTN_BLOCK), :]
        s_w, s_c = _attn_scores(qv, kw, kc_ref[0], n)
        sk = _sink_col(sink_ref, hk)
        m = jnp.maximum(jnp.maximum(jnp.max(s_w, -1, keepdims=True), jnp.max(s_c, -1, keepdims=True)), sk)
        p_w, p_c = jnp.exp(s_w - m), jnp.exp(s_c - m)
        den = jnp.sum(p_w, -1, keepdims=True) + jnp.sum(p_c, -1, keepdims=True) + jnp.exp(sk - m)
        o = (jnp.dot(p_w.astype(BF16), vw, preferred_element_type=F32)
             + jnp.dot(p_c.astype(BF16), vc_ref[0], preferred_element_type=F32)) / den
        o_ref[...] = o.reshape(Q_PER_KV, ATTN_BLOCK, HEAD_DIM).astype(o_ref.dtype)
        lse_ref[...] = (m + jnp.log(den)).reshape(Q_PER_KV, ATTN_BLOCK, 1)

    return pl.pallas_call(
        body, name="attn_fwd",
        out_shape=[jax.ShapeDtypeStruct((N_Q, L, HEAD_DIM), BF16), jax.ShapeDtypeStruct((N_Q, L, 1), F32)],
        grid=(N_KV, L // ATTN_BLOCK),
        in_specs=[qspec, kspec, kspec, cspec, cspec, sspec], out_specs=[qspec, lspec],
        compiler_params=_cparams(("parallel", "parallel")),
    )(q, k, v, kc, vc, sink)


def _attn_bwd(q, k, v, kc, vc, sink, o, lse, do):
    qspec, kspec, cspec, lspec, sspec = _attn_specs()
    tn = (((0,), (0,)), ((), ()))
    nt = (((1,), (1,)), ((), ()))

    def body(q_ref, k_ref, v_ref, kc_ref, vc_ref, sink_ref, o_ref, lse_ref, do_ref,
             dq_ref, dk_ref, dv_ref, dkc_ref, dvc_ref, dsink_ref):
        hk, n = pl.program_id(0), pl.program_id(1)
        qv = q_ref[...].reshape(QROWS, HEAD_DIM)
        start = pl.multiple_of(n * ATTN_BLOCK, ATTN_BLOCK)
        win = pl.ds(start, 3 * ATTN_BLOCK)
        kw, vw = k_ref[0, win, :], v_ref[0, win, :]
        kcv, vcv = kc_ref[0], vc_ref[0]
        s_w, s_c = _attn_scores(qv, kw, kcv, n)
        lse_v = lse_ref[...].reshape(QROWS, 1)
        p_w, p_c = jnp.exp(s_w - lse_v), jnp.exp(s_c - lse_v)
        dov = do_ref[...].reshape(QROWS, HEAD_DIM).astype(F32)
        ov = o_ref[...].reshape(QROWS, HEAD_DIM).astype(F32)
        delta = jnp.sum(dov * ov, -1, keepdims=True)
        dob = dov.astype(BF16)
        dp_w = lax.dot_general(dob, vw, nt, preferred_element_type=F32)
        dp_c = lax.dot_general(dob, vcv, nt, preferred_element_type=F32)
        ds_w = (p_w * (dp_w - delta) * ATTN_SCALE).astype(BF16)
        ds_c = (p_c * (dp_c - delta) * ATTN_SCALE).astype(BF16)
        dq = jnp.dot(ds_w, kw, preferred_element_type=F32) + jnp.dot(ds_c, kcv, preferred_element_type=F32)
        dq_ref[...] = dq.reshape(Q_PER_KV, ATTN_BLOCK, HEAD_DIM)

        @pl.when(n == 0)
        def _():
            dk_ref[...] = jnp.zeros_like(dk_ref)
            dv_ref[...] = jnp.zeros_like(dv_ref)
            dkc_ref[...] = jnp.zeros_like(dkc_ref)
            dvc_ref[...] = jnp.zeros_like(dvc_ref)

        dk_ref[0, win, :] += lax.dot_general(ds_w, qv, tn, preferred_element_type=F32)
        dv_ref[0, win, :] += lax.dot_general(p_w.astype(BF16), dob, tn, preferred_element_type=F32)
        dkc_ref[0] += lax.dot_general(ds_c, qv, tn, preferred_element_type=F32)
        dvc_ref[0] += lax.dot_general(p_c.astype(BF16), dob, tn, preferred_element_type=F32)
        dsk = -jnp.exp(_sink_col(sink_ref, hk) - lse_v) * delta
        for g in range(Q_PER_KV):
            part = jnp.sum(dsk[g * ATTN_BLOCK:(g + 1) * ATTN_BLOCK])
            idx = hk * Q_PER_KV + g

            @pl.when(n == 0)
            def _(part=part, idx=idx):
                dsink_ref[idx] = part

            @pl.when(n > 0)
            def _(part=part, idx=idx):
                dsink_ref[idx] += part

    kshape = jax.ShapeDtypeStruct((N_KV, L + 2 * ATTN_BLOCK, HEAD_DIM), F32)
    cshape = jax.ShapeDtypeStruct((N_KV, LC, HEAD_DIM), F32)
    return pl.pallas_call(
        body, name="attn_bwd",
        out_shape=[jax.ShapeDtypeStruct((N_Q, L, HEAD_DIM), F32), kshape, kshape, cshape, cshape,
                   jax.ShapeDtypeStruct((N_Q,), F32)],
        grid=(N_KV, L // ATTN_BLOCK),
        in_specs=[qspec, kspec, kspec, cspec, cspec, sspec, qspec, lspec, qspec],
        out_specs=[qspec, kspec, kspec, cspec, cspec, sspec],
        compiler_params=_cparams(("arbitrary", "arbitrary")),
    )(q, k, v, kc, vc, sink, o, lse, do)


def _gm_specs():
    rspec = pl.BlockSpec((1, GM_CHUNK, GM_W), lambda n: (0, n, 0))
    wspec = pl.BlockSpec((GM_GROUPS, GM_CHUNK, GM_CHUNK), lambda n: (0, 0, 0))
    bspec = pl.BlockSpec((GM_GROUPS, GM_CHUNK, 1), lambda n: (0, 0, 0))
    return rspec, wspec, bspec


def _gm_spatial_fwd(u, v, ws, bs):
    rspec, wspec, bspec = _gm_specs()

    def body(u_ref, v_ref, ws_ref, bs_ref, o_ref):
        for g in range(GM_GROUPS):
            cols = slice(g * GM_CHUNK, (g + 1) * GM_CHUNK)
            s = jnp.dot(ws_ref[g], v_ref[0, :, cols], preferred_element_type=F32) + bs_ref[g]
            o_ref[0, :, cols] = (u_ref[0, :, cols] * s).astype(o_ref.dtype)

    return pl.pallas_call(
        body, name="gm_spatial_fwd", out_shape=jax.ShapeDtypeStruct((1, L, GM_W), BF16),
        grid=(L // GM_CHUNK,), in_specs=[rspec, rspec, wspec, bspec], out_specs=rspec,
        compiler_params=_cparams(("parallel",)),
    )(u, v, ws, bs)


def _gm_spatial_bwd(u, v, ws, bs, dus):
    rspec, wspec, bspec = _gm_specs()
    tn = (((0,), (0,)), ((), ()))
    nt = (((1,), (1,)), ((), ()))

    def body(u_ref, v_ref, ws_ref, bs_ref, d_ref, du_ref, dv_ref, dws_ref, dbs_ref):
        n = pl.program_id(0)

        @pl.when(n == 0)
        def _():
            dws_ref[...] = jnp.zeros_like(dws_ref)
            dbs_ref[...] = jnp.zeros_like(dbs_ref)

        for g in range(GM_GROUPS):
            cols = slice(g * GM_CHUNK, (g + 1) * GM_CHUNK)
            vb = v_ref[0, :, cols]
            s = jnp.dot(ws_ref[g], vb, preferred_element_type=F32) + bs_ref[g]
            d = d_ref[0, :, cols].astype(F32)
            du_ref[0, :, cols] = d * s
            ds = d * u_ref[0, :, cols]
            dsb = ds.astype(BF16)
            dv_ref[0, :, cols] = lax.dot_general(ws_ref[g], dsb, tn, preferred_element_type=F32)
            dws_ref[g] += lax.dot_general(dsb, vb, nt, preferred_element_type=F32)
            dbs_ref[g] += jnp.sum(ds, axis=1, keepdims=True)

    row = jax.ShapeDtypeStruct((1, L, GM_W), F32)
    return pl.pallas_call(
        body, name="gm_spatial_bwd",
        out_shape=[row, row, jax.ShapeDtypeStruct((GM_GROUPS, GM_CHUNK, GM_CHUNK), F32),
                   jax.ShapeDtypeStruct((GM_GROUPS, GM_CHUNK, 1), F32)],
        grid=(L // GM_CHUNK,), in_specs=[rspec, rspec, wspec, bspec, rspec],
        out_specs=[rspec, rspec, wspec, bspec],
        compiler_params=_cparams(("arbitrary",)),
    )(u, v, ws, bs, dus)


def _loss_head(h, target):
    tm = 256

    def body(h_ref, t_ref, dh_ref, loss_ref):
        d = h_ref[0] - t_ref[0]
        dh_ref[0] = d * (1.0 / D)

        @pl.when(pl.program_id(0) == 0)
        def _():
            loss_ref[...] = jnp.zeros_like(loss_ref)

        loss_ref[...] += jnp.sum(d * d) * (0.5 / D)

    spec = pl.BlockSpec((1, tm, D), lambda i: (0, i, 0))
    dh, loss = pl.pallas_call(
        body, name="loss_head",
        out_shape=[jax.ShapeDtypeStruct((1, L, D), F32), jax.ShapeDtypeStruct((8, 128), F32)],
        grid=(L // tm,), in_specs=[spec, spec],
        out_specs=[spec, pl.BlockSpec((8, 128), lambda i: (0, 0))],
        compiler_params=_cparams(("arbitrary",)),
    )(h, target)
    return dh, loss[0, 0]


def _adamw(parts, w, m, v, name):
    nl, s, r, c = parts.shape
    tr = r
    for cand in (512, 256, 128, 64, 32, 16):
        if r % cand == 0 and cand * c <= 131072:
            tr = cand
            break
    c1 = 1.0 / (1.0 - ADAM_B1 ** ADAM_STEP)
    c2 = 1.0 / (1.0 - ADAM_B2 ** ADAM_STEP)

    def body(p_ref, w_ref, m_ref, v_ref, g_ref, d_ref, nm_ref, nv_ref):
        g = p_ref[0, 0].astype(F32)
        for q in range(1, s):
            g = g + p_ref[0, q].astype(F32)
        mn = ADAM_B1 * m_ref[0] + (1.0 - ADAM_B1) * g
        vn = ADAM_B2 * v_ref[0] + (1.0 - ADAM_B2) * (g * g)
        g_ref[0] = g
        nm_ref[0] = mn
        nv_ref[0] = vn
        d_ref[0] = -ADAM_LR * ((mn * c1) / (jnp.sqrt(vn * c2) + ADAM_EPS) + ADAM_WD * w_ref[0])

    spec = pl.BlockSpec((1, tr, c), lambda li, i: (li, i, 0))
    shp = jax.ShapeDtypeStruct((nl, r, c), F32)
    return pl.pallas_call(
        body, name=name, out_shape=[shp] * 4, grid=(nl, r // tr),
        in_specs=[pl.BlockSpec((1, s, tr, c), lambda li, i: (li, 0, i, 0)), spec, spec, spec],
        out_specs=[spec] * 4,
        compiler_params=_cparams(("parallel", "parallel")),
    )(parts, w, m, v)


def _pack_rows(vecs, lanes=128, mult=8):
    flat = jnp.concatenate([v.reshape(-1) for v in vecs])
    n = flat.shape[0]
    rows = -(-n // (mult * lanes)) * mult
    return jnp.pad(flat, (0, rows * lanes - n)).reshape(rows, lanes)


def _unpack_rows(packed, shapes):
    flat = packed.reshape(-1)
    out, pos = [], 0
    for s in shapes:
        n = 1
        for d_ in s:
            n *= d_
        out.append(flat[pos:pos + n].reshape(s))
        pos += n
    return out


def _unshard_last(g):
    lead = g.shape[1:-1]
    return jnp.moveaxis(g, 0, -2).reshape(*lead, NDEV * g.shape[-1])


def _shard_last(full):
    lead, w = full.shape[:-1], full.shape[-1] // NDEV
    return jnp.moveaxis(full.reshape(*lead, NDEV, w), -2, 0)


def _rope_tables():
    rows = L // GRID_W
    row = jnp.repeat(jnp.arange(rows), GRID_W).astype(F32)
    col = jnp.tile(jnp.arange(GRID_W), rows).astype(F32)
    axis_dim = HEAD_DIM // 2
    inv_freq = ROPE_BASE ** (-jnp.arange(0, axis_dim, 2, dtype=F32) / axis_dim)
    ang_r, ang_c = row[:, None] * inv_freq[None, :], col[:, None] * inv_freq[None, :]
    ang = jnp.concatenate([ang_r, ang_r, ang_c, ang_c], axis=-1)
    ang = jnp.concatenate([ang, ang], axis=-1)[None]
    return jnp.cos(ang), jnp.sin(ang)


def _heads(x, nh):
    t = x.shape[1]
    return x.reshape(t, nh, HEAD_DIM).transpose(1, 0, 2)


def _unheads(x):
    nh, t, _ = x.shape
    return x.transpose(1, 0, 2).reshape(1, t, nh * HEAD_DIM)


def _ffn_fwd(i, h, mod, ng, wts):
    a2 = _rw_fwd(f_modnorm, [(h, 0)], [(ng[2], "one"), (mod["sc2"], "seg"), (mod["sh2"], "seg")],
                 [(D, BF16)], name=f"ffn{i}_norm")[0]
    z = _mm(a2, wts["up"], "nn", name=f"ffn{i}_up")
    zc = _dwconv(z, wts["cw"], wts["cb"], name=f"ffn{i}_conv")
    u = _rw_fwd(f_ffn_gate, [(zc, 0), (zc, 4)], [], [(FFN_BLK, BF16)], name=f"ffn{i}_gate", nblk=4)[0]
    f = _mm(u, wts["down"], "nn", reduce_blocks=True, name=f"ffn{i}_down")
    h2 = _rw_fwd(f_resgate, [(h, 0), (f, 0)], [(mod["g2"], "seg"), (ng[3], "one")], [(D, F32)],
                 name=f"ffn{i}_res")[0]
    return h2, (h, a2, z, zc, u, f)


def _ffn_bwd(i, dh, res, mod, ng, wts):
    h, a2, z, zc, u, f = res
    df, dg2, dng3 = _rw_bwd(f_gate_rms, [(f, 0)], [(mod["g2"], "seg"), (ng[3], "one")], [dh],
                            name=f"ffn{i}_res_b", row_grad=[(0, BF16)], param_grad=[0, 1])
    du = _mm(df, wts["down"], "nt", name=f"ffn{i}_down_bx")
    d_down = _mm(u, df, "tn", out_dtype=BF16, name=f"ffn{i}_down_bw")
    dzg, dzv = _rw_bwd(f_ffn_gate, [(zc, 0), (zc, 4)], [], [du], name=f"ffn{i}_gate_b",
                       row_grad=[(0, F32), (1, F32)], nblk=4)
    dzc = jnp.concatenate([dzg, dzv], axis=0)
    dcw, dcb = _dwconv_wgrad(z, dzc, FFN_K, name=f"ffn{i}_conv_bw")
    dz = _dwconv(dzc, wts["cw"][:, ::-1, :], None, out_dtype=BF16, name=f"ffn{i}_conv_bx")
    d_up = _mm(a2, dz, "tn", out_dtype=BF16, name=f"ffn{i}_up_bw")
    da2 = _mm(dz, wts["up"], "nt", reduce_blocks=True, name=f"ffn{i}_up_bx")
    dh_in, dng2, dsc2, dsh2 = _rw_bwd(
        f_modnorm, [(h, 0)], [(ng[2], "one"), (mod["sc2"], "seg"), (mod["sh2"], "seg")], [da2],
        name=f"ffn{i}_norm_b", row_grad=[(0, F32)], param_grad=[0, 1, 2], add=dh)
    grads = dict(up=d_up, down=d_down, cw=dcw, cb=dcb, ng2=dng2, ng3=dng3, sc2=dsc2, sh2=dsh2, g2=dg2)
    return dh_in, grads


def _mixer_norm_fwd(i, h, mod, ng):
    return _rw_fwd(f_modnorm, [(h, 0)], [(ng[0], "one"), (mod["sc1"], "seg"), (mod["sh1"], "seg")],
                   [(D, BF16)], name=f"mix{i}_norm")[0]


def _mixer_norm_bwd(i, h, mod, ng, da, dh):
    return _rw_bwd(f_modnorm, [(h, 0)], [(ng[0], "one"), (mod["sc1"], "seg"), (mod["sh1"], "seg")], [da],
                   name=f"mix{i}_norm_b", row_grad=[(0, F32)], param_grad=[0, 1, 2], add=dh)


def _conformer_fwd(i, h, mod, ng, wts):
    a = _mixer_norm_fwd(i, h, mod, ng)
    p = _mm(a, wts["w_in"], "nn", name=f"cm{i}_in")
    z = _rw_fwd(f_glu, [(p, 0, 2)], [(wts["b_in"], "one", 2)], [(D, F32)], name=f"cm{i}_glu")[0]
    zc = _dwconv(z, wts["dw_w"], wts["dw_b"], name=f"cm{i}_conv")
    r = _rw_fwd(f_lnsilu, [(zc, 0)], [(wts["ln_g"], "one"), (wts["ln_b"], "one")], [(D, BF16)],
                name=f"cm{i}_ln")[0]
    y = _mm(r, wts["w_out"], "nn", name=f"cm{i}_out")
    h2 = _rw_fwd(f_resgate_bias, [(h, 0), (y, 0)], [(mod["g1"], "seg"), (ng[1], "one"), (wts["b_out"], "one")],
                 [(D, F32)], name=f"cm{i}_res")[0]
    return h2, (h, a, p, z, zc, r, y)


def _conformer_bwd(i, dh, res, mod, ng, wts):
    h, a, p, z, zc, r, y = res
    dy, dg1, dng1, db_out = _rw_bwd(
        f_gate_rms_bias, [(y, 0)], [(mod["g1"], "seg"), (ng[1], "one"), (wts["b_out"], "one")], [dh],
        name=f"cm{i}_res_b", row_grad=[(0, BF16)], param_grad=[0, 1, 2])
    dr = _mm(dy, wts["w_out"], "nt", name=f"cm{i}_out_bx")
    d_w_out = _mm(r, dy, "tn", out_dtype=BF16, name=f"cm{i}_out_bw")
    dzc, dln_g, dln_b = _rw_bwd(f_lnsilu, [(zc, 0)], [(wts["ln_g"], "one"), (wts["ln_b"], "one")], [dr],
                                name=f"cm{i}_ln_b", row_grad=[(0, F32)], param_grad=[0, 1])
    ddw_w, ddw_b = _dwconv_wgrad(z, dzc, CM_K, name=f"cm{i}_conv_bw")
    dz = _dwconv(dzc, wts["dw_w"][:, ::-1, :], None, name=f"cm{i}_conv_bx")
    dp, db_in = _rw_bwd(f_glu, [(p, 0, 2)], [(wts["b_in"], "one", 2)], [dz], name=f"cm{i}_glu_b",
                        row_grad=[(0, BF16)], param_grad=[0])
    d_w_in = _mm(a, dp, "tn", out_dtype=BF16, name=f"cm{i}_in_bw")
    da = _mm(dp, wts["w_in"], "nt", name=f"cm{i}_in_bx")
    dh_in, dng0, dsc1, dsh1 = _mixer_norm_bwd(i, h, mod, ng, da, dh)
    grads = dict(w_in=d_w_in, w_out=d_w_out, b_in=db_in, dw_w=ddw_w, dw_b=ddw_b, ln_g=dln_g, ln_b=dln_b,
                 b_out=db_out, ng0=dng0, ng1=dng1, sc1=dsc1, sh1=dsh1, g1=dg1)
    return dh_in, grads


def _attention_fwd(i, h_all, mod, ng, wts, tables):
    a = _mixer_norm_fwd(i, h_all, mod, ng)
    qkv = _mm(a, wts["w_qkv"], "nn", name="attn_qkv")
    kv0 = N_Q * HEAD_DIM
    kv1 = kv0 + N_KV * HEAD_DIM
    q = _rope(qkv[:, :L, :kv0], tables, False, BF16, "attn_rope_q")
    k = _rope(qkv[:, :L, kv0:kv1], tables, False, BF16, "attn_rope_k")
    pad = ((0, 0), (ATTN_BLOCK, ATTN_BLOCK), (0, 0))
    q_h = _heads(q, N_Q)
    k_h = jnp.pad(_heads(k, N_KV), pad)
    v_h = jnp.pad(_heads(qkv[:, :L, kv1:].astype(BF16), N_KV), pad)
    kc_h = _heads(qkv[:, L:, kv0:kv1].astype(BF16), N_KV)
    vc_h = _heads(qkv[:, L:, kv1:].astype(BF16), N_KV)
    o_h, lse = _attn_fwd(q_h, k_h, v_h, kc_h, vc_h, wts["sink"])
    o = _unheads(o_h)
    y = _mm(o, wts["w_o"], "nn", name="attn_o")
    h_lat = h_all[:, :L]
    mod_lat = {k_: v_[:1] for k_, v_ in mod.items()}
    h2 = _rw_fwd(f_resgate, [(h_lat, 0), (y, 0)], [(mod_lat["g1"], "seg"), (ng[1], "one")], [(D, F32)],
                 name="attn_res")[0]
    return h2, (h_all, a, q_h, k_h, v_h, kc_h, vc_h, o_h, lse, o, y)


def _attention_bwd(i, dh, res, mod, ng, wts, tables):
    h_all, a, q_h, k_h, v_h, kc_h, vc_h, o_h, lse, o, y = res
    mod_lat = {k_: v_[:1] for k_, v_ in mod.items()}
    dy, dg1, dng1 = _rw_bwd(f_gate_rms, [(y, 0)], [(mod_lat["g1"], "seg"), (ng[1], "one")], [dh],
                            name="attn_res_b", row_grad=[(0, BF16)], param_grad=[0, 1])
    do = _mm(dy, wts["w_o"], "nt", name="attn_o_bx")
    d_w_o = _mm(o, dy, "tn", out_dtype=BF16, name="attn_o_bw")
    dq_h, dk_h, dv_h, dkc_h, dvc_h, dsink = _attn_bwd(q_h, k_h, v_h, kc_h, vc_h, wts["sink"], o_h, lse,
                                                        _heads(do, N_Q))
    dq = _rope(_unheads(dq_h), tables, True, BF16, "attn_rope_q_b")
    dk = _rope(_unheads(dk_h[:, ATTN_BLOCK:-ATTN_BLOCK]), tables, True, BF16, "attn_rope_k_b")
    dv = _unheads(dv_h[:, ATTN_BLOCK:-ATTN_BLOCK]).astype(BF16)
    d_lat = jnp.concatenate([dq, dk, dv], axis=2)
    d_ctx = jnp.concatenate([jnp.zeros((1, LC, N_Q * HEAD_DIM), BF16), _unheads(dkc_h).astype(BF16),
                             _unheads(dvc_h).astype(BF16)], axis=2)
    dqkv = jnp.concatenate([d_lat, d_ctx], axis=1)
    d_w_qkv = _mm(a, dqkv, "tn", out_dtype=BF16, name="attn_qkv_bw")
    da = _mm(dqkv, wts["w_qkv"], "nt", name="attn_qkv_bx")
    dh_res = jnp.concatenate([dh, jnp.zeros((1, LC, D), F32)], axis=1)
    dh_in, dng0, dsc1, dsh1 = _mixer_norm_bwd(i, h_all, mod, ng, da, dh_res)
    grads = dict(w_qkv=d_w_qkv, w_o=d_w_o, sink=dsink, ng0=dng0, ng1=dng1, sc1=dsc1, sh1=dsh1, g1=dg1)
    return dh_in, grads


def _gmlp_fwd(i, h, mod, ng, wts):
    a = _mixer_norm_fwd(i, h, mod, ng)
    p = _mm(a, wts["w_in"], "nn", name="gm_in")
    u, v = _rw_fwd(f_gmlp_pre, [(p, 0, 2)], [(wts["b_in"], "one", 2), (wts["ln_g"], "one"), (wts["ln_b"], "one")],
                   [(GM_W, F32), (GM_W, BF16)], name="gm_pre")
    us = _gm_spatial_fwd(u, v, wts["w_s"], wts["b_s"])
    y = _mm(us, wts["w_out"], "nn", name="gm_out")
    h2 = _rw_fwd(f_resgate, [(h, 0), (y, 0)], [(mod["g1"], "seg"), (ng[1], "one")], [(D, F32)],
                 name="gm_res")[0]
    return h2, (h, a, p, u, v, us, y)


def _gmlp_bwd(i, dh, res, mod, ng, wts):
    h, a, p, u, v, us, y = res
    dy, dg1, dng1 = _rw_bwd(f_gate_rms, [(y, 0)], [(mod["g1"], "seg"), (ng[1], "one")], [dh],
                            name="gm_res_b", row_grad=[(0, BF16)], param_grad=[0, 1])
    dus = _mm(dy, wts["w_out"], "nt", name="gm_out_bx")
    d_w_out = _mm(us, dy, "tn", out_dtype=BF16, name="gm_out_bw")
    du, dv, dws, dbs = _gm_spatial_bwd(u, v, wts["w_s"], wts["b_s"], dus)
    dp, db_in, dln_g, dln_b = _rw_bwd(
        f_gmlp_pre, [(p, 0, 2)], [(wts["b_in"], "one", 2), (wts["ln_g"], "one"), (wts["ln_b"], "one")], [du, dv],
        name="gm_pre_b", row_grad=[(0, BF16)], param_grad=[0, 1, 2])
    d_w_in = _mm(a, dp, "tn", out_dtype=BF16, name="gm_in_bw")
    da = _mm(dp, wts["w_in"], "nt", name="gm_in_bx")
    dh_in, dng0, dsc1, dsh1 = _mixer_norm_bwd(i, h, mod, ng, da, dh)
    grads = dict(w_in=d_w_in, w_out=d_w_out, b_in=db_in, ln_g=dln_g, ln_b=dln_b, w_s=dws, b_s=dbs,
                 ng0=dng0, ng1=dng1, sc1=dsc1, sh1=dsh1, g1=dg1)
    return dh_in, grads


MOD_NAMES = ("sh1", "sc1", "g1", "sh2", "sc2", "g2")
SMALL = (
    ("norm_g", (4, 4, 128)), ("ffn_conv_w", (4, 3, 704)), ("cm_b_in", (2, 256)), ("cm_dw_w", (2, 31, 128)),
    ("cm_dw_b", (2, 128)), ("cm_ln_g", (2, 128)), ("cm_ln_b", (2, 128)), ("cm_b_out", (2, 128)),
    ("gm_b_in", (1, 512)), ("gm_ln_g", (1, 256)), ("gm_ln_b", (1, 256)))


def _local_step(x, ctx, target, lat_mod, ctx_mod, P):
    tables = _rope_tables()
    ng = [[P["norm_g"][i, j].reshape(1, 1, D) for j in range(4)] for i in range(DEPTH)]

    def mods(i, with_ctx):
        out = {}
        for j, nme in enumerate(MOD_NAMES):
            rows = [lat_mod[i, j]] + ([ctx_mod[i, j]] if with_ctx else [])
            out[nme] = jnp.stack(rows).reshape(len(rows), 1, D)
        return out

    def ffn_w(i):
        return dict(up=P["ffn_w_up"][i], down=P["ffn_w_down"][i], cw=P["ffn_conv_w"][i], cb=P["ffn_conv_b"][i])

    def cm_w(j):
        return dict(w_in=P["cm_w_in"][j], w_out=P["cm_w_out"][j], b_in=P["cm_b_in"][j].reshape(1, 1, 2 * D),
                    dw_w=P["cm_dw_w"][j][None], dw_b=P["cm_dw_b"][j].reshape(1, 1, D),
                    ln_g=P["cm_ln_g"][j].reshape(1, 1, D), ln_b=P["cm_ln_b"][j].reshape(1, 1, D),
                    b_out=P["cm_b_out"][j].reshape(1, 1, D))

    at_w = dict(w_qkv=P["attn_w_qkv"], w_o=P["attn_w_o"], sink=P["attn_sink"].reshape(N_Q))
    gm_w = dict(w_in=P["gm_w_in"], w_out=P["gm_w_out"], b_in=P["gm_b_in"].reshape(1, 1, 2 * GM_W),
                ln_g=P["gm_ln_g"].reshape(1, 1, GM_W), ln_b=P["gm_ln_b"].reshape(1, 1, GM_W),
                w_s=P["gm_w_s"].reshape(GM_GROUPS, GM_CHUNK, GM_CHUNK).astype(BF16),
                b_s=P["gm_b_s"].reshape(GM_GROUPS, GM_CHUNK, 1))

    m0, m1, m1l, m2, m3 = mods(0, True), mods(1, True), mods(1, False), mods(2, False), mods(3, False)
    h_all = jnp.concatenate([x, ctx], axis=1)
    h, r0m = _conformer_fwd(0, h_all, m0, ng[0], cm_w(0))
    h, r0f = _ffn_fwd(0, h, m0, ng[0], ffn_w(0))
    h, r1m = _attention_fwd(1, h, m1, ng[1], at_w, tables)
    h, r1f = _ffn_fwd(1, h, m1l, ng[1], ffn_w(1))
    h, r2m = _gmlp_fwd(2, h, m2, ng[2], gm_w)
    h, r2f = _ffn_fwd(2, h, m2, ng[2], ffn_w(2))
    h, r3m = _conformer_fwd(3, h, m3, ng[3], cm_w(1))
    h, r3f = _ffn_fwd(3, h, m3, ng[3], ffn_w(3))
    dh, loss = _loss_head(h, target)

    G = {}
    dh, G["f3"] = _ffn_bwd(3, dh, r3f, m3, ng[3], ffn_w(3))
    dh, G["m3"] = _conformer_bwd(3, dh, r3m, m3, ng[3], cm_w(1))
    dh, G["f2"] = _ffn_bwd(2, dh, r2f, m2, ng[2], ffn_w(2))
    dh, G["m2"] = _gmlp_bwd(2, dh, r2m, m2, ng[2], gm_w)
    dh, G["f1"] = _ffn_bwd(1, dh, r1f, m1l, ng[1], ffn_w(1))
    dh, G["m1"] = _attention_bwd(1, dh, r1m, m1, ng[1], at_w, tables)
    dh, G["f0"] = _ffn_bwd(0, dh, r0f, m0, ng[0], ffn_w(0))
    dh, G["m0"] = _conformer_bwd(0, dh, r0m, m0, ng[0], cm_w(0))
    grad_x = dh[:, :L]

    zero = jnp.zeros((D,), F32)
    dmod = []
    for seg in range(2):
        per_layer = []
        for i in range(DEPTH):
            vals = []
            for nme in MOD_NAMES:
                src = G[("m" if nme.endswith("1") else "f") + str(i)][nme]
                vals.append(src[seg, 0] if src.shape[0] > seg else zero)
            per_layer.append(jnp.concatenate(vals))
        dmod.append(jnp.stack(per_layer))
    dmod = jnp.stack(dmod)
    return loss, grad_x, G, dmod


def kernel(x, c, ctx, c_ctx, ada_w, ada_b, norm_g, ffn_w_up, ffn_conv_w, ffn_conv_b, ffn_w_down, cm_w_in, cm_b_in, cm_dw_w, cm_dw_b, cm_ln_g, cm_ln_b, cm_w_out, cm_b_out, attn_w_qkv, attn_sink, attn_w_o, gm_w_in, gm_b_in, gm_ln_g, gm_ln_b, gm_w_s, gm_b_s, gm_w_out, loss_target, m_c_ctx, m_ada_w, m_ada_b, m_norm_g, m_ffn_w_up, m_ffn_conv_w, m_ffn_conv_b, m_ffn_w_down, m_cm_w_in, m_cm_b_in, m_cm_dw_w, m_cm_dw_b, m_cm_ln_g, m_cm_ln_b, m_cm_w_out, m_cm_b_out, m_attn_w_qkv, m_attn_sink, m_attn_w_o, m_gm_w_in, m_gm_b_in, m_gm_ln_g, m_gm_ln_b, m_gm_w_s, m_gm_b_s, m_gm_w_out, v_c_ctx, v_ada_w, v_ada_b, v_norm_g, v_ffn_w_up, v_ffn_conv_w, v_ffn_conv_b, v_ffn_w_down, v_cm_w_in, v_cm_b_in, v_cm_dw_w, v_cm_dw_b, v_cm_ln_g, v_cm_ln_b, v_cm_w_out, v_cm_b_out, v_attn_w_qkv, v_attn_sink, v_attn_w_o, v_gm_w_in, v_gm_b_in, v_gm_ln_g, v_gm_ln_b, v_gm_w_s, v_gm_b_s, v_gm_w_out):
    W = dict(c_ctx=c_ctx, ada_w=ada_w, ada_b=ada_b, norm_g=norm_g, ffn_w_up=ffn_w_up, ffn_conv_w=ffn_conv_w, ffn_conv_b=ffn_conv_b, ffn_w_down=ffn_w_down, cm_w_in=cm_w_in, cm_b_in=cm_b_in, cm_dw_w=cm_dw_w, cm_dw_b=cm_dw_b, cm_ln_g=cm_ln_g, cm_ln_b=cm_ln_b, cm_w_out=cm_w_out, cm_b_out=cm_b_out, attn_w_qkv=attn_w_qkv, attn_sink=attn_sink, attn_w_o=attn_w_o, gm_w_in=gm_w_in, gm_b_in=gm_b_in, gm_ln_g=gm_ln_g, gm_ln_b=gm_ln_b, gm_w_s=gm_w_s, gm_b_s=gm_b_s, gm_w_out=gm_w_out)
    M = dict(c_ctx=m_c_ctx, ada_w=m_ada_w, ada_b=m_ada_b, norm_g=m_norm_g, ffn_w_up=m_ffn_w_up, ffn_conv_w=m_ffn_conv_w, ffn_conv_b=m_ffn_conv_b, ffn_w_down=m_ffn_w_down, cm_w_in=m_cm_w_in, cm_b_in=m_cm_b_in, cm_dw_w=m_cm_dw_w, cm_dw_b=m_cm_dw_b, cm_ln_g=m_cm_ln_g, cm_ln_b=m_cm_ln_b, cm_w_out=m_cm_w_out, cm_b_out=m_cm_b_out, attn_w_qkv=m_attn_w_qkv, attn_sink=m_attn_sink, attn_w_o=m_attn_w_o, gm_w_in=m_gm_w_in, gm_b_in=m_gm_b_in, gm_ln_g=m_gm_ln_g, gm_ln_b=m_gm_ln_b, gm_w_s=m_gm_w_s, gm_b_s=m_gm_b_s, gm_w_out=m_gm_w_out)
    V = dict(c_ctx=v_c_ctx, ada_w=v_ada_w, ada_b=v_ada_b, norm_g=v_norm_g, ffn_w_up=v_ffn_w_up, ffn_conv_w=v_ffn_conv_w, ffn_conv_b=v_ffn_conv_b, ffn_w_down=v_ffn_w_down, cm_w_in=v_cm_w_in, cm_b_in=v_cm_b_in, cm_dw_w=v_cm_dw_w, cm_dw_b=v_cm_dw_b, cm_ln_g=v_cm_ln_g, cm_ln_b=v_cm_ln_b, cm_w_out=v_cm_w_out, cm_b_out=v_cm_b_out, attn_w_qkv=v_attn_w_qkv, attn_sink=v_attn_sink, attn_w_o=v_attn_w_o, gm_w_in=v_gm_w_in, gm_b_in=v_gm_b_in, gm_ln_g=v_gm_ln_g, gm_ln_b=v_gm_ln_b, gm_w_s=v_gm_w_s, gm_b_s=v_gm_b_s, gm_w_out=v_gm_w_out)
    me = 4 * lax.axis_index("x") + 2 * lax.axis_index("y") + lax.axis_index("c")
    small_shapes = [s for _, s in SMALL]

    small = _pack_rows([W[n] for n, _ in SMALL] + [c])
    big_names = [("ffn_w_up", 4), ("ffn_w_down", 4), ("cm_w_in", 2), ("cm_w_out", 2), ("attn_w_qkv", 1),
                 ("attn_w_o", 1), ("gm_w_in", 1), ("gm_w_out", 1)]
    big_local = [W[n][i].astype(BF16) for n, nl in big_names for i in range(nl)]
    gathered = _all_gather([small] + big_local, "gather_params")
    small_g, big_g = gathered[0], gathered[1:]
    P, pos = {}, 0
    for n, nl in big_names:
        P[n] = big_g[pos:pos + nl]
        pos += nl
    col_to_full = lambda g: g.transpose(1, 0, 2).reshape(g.shape[1], NDEV * g.shape[2])
    P["ffn_w_down"] = [g.reshape(4, FFN_BLK, D) for g in P["ffn_w_down"]]
    P["cm_w_in"] = [col_to_full(g) for g in P["cm_w_in"]]
    P["cm_w_out"] = [g.reshape(D, D) for g in P["cm_w_out"]]
    P["attn_w_qkv"] = col_to_full(P["attn_w_qkv"][0])
    P["attn_w_o"] = P["attn_w_o"][0].reshape(D, D)
    P["gm_w_in"] = col_to_full(P["gm_w_in"][0])
    P["gm_w_out"] = P["gm_w_out"][0].reshape(GM_W, D)
    unpacked = jax.vmap(lambda r: tuple(_unpack_rows(r, small_shapes + [(D,)])))(small_g)
    for (n, _), g in zip(SMALL, unpacked[:-1]):
        if n == "ffn_conv_w":
            P[n] = [g[:, i] for i in range(DEPTH)]
        else:
            P[n] = _unshard_last(g)
    c_all = unpacked[-1]
    P["ffn_conv_b"] = [ffn_conv_b[i].reshape(NDEV, 1, FFN_BLK) for i in range(DEPTH)]
    P["attn_sink"], P["gm_w_s"], P["gm_b_s"] = attn_sink, gm_w_s, gm_b_s

    cond = jnp.concatenate([c_all, c_ctx[None], jnp.zeros((7, D), F32)])[None]
    scond = _rw_fwd(f_silu, [(cond, 0)], [], [(D, BF16)], name="ada_silu")[0]
    ada_bf = ada_w.astype(BF16)
    ncol = ada_w.shape[2]
    mod_loc = _mm(scond, ada_bf, "nn", name="ada_proj")
    mod_loc = mod_loc + lax.dynamic_slice_in_dim(ada_b, me * ncol, ncol, axis=1)[:, None, :]
    mod_g = _all_gather([mod_loc], "gather_mod")[0]
    mod_full = mod_g.transpose(1, 2, 0, 3).reshape(DEPTH, 16, 6, D)
    lat_mod = lax.dynamic_index_in_dim(mod_full, me, axis=1, keepdims=False)
    ctx_mod = mod_full[:, NDEV]

    loss_part, grad_x, G, dmod = _local_step(x, ctx, loss_target, lat_mod, ctx_mod, P)
    loss = lax.psum(loss_part, AXES)

    dmod_g = _all_gather([dmod], "gather_dmod")[0]
    dm_cols = lax.dynamic_slice_in_dim(dmod_g, me * ncol, ncol, axis=3)
    dm_ext = dm_cols.transpose(2, 1, 0, 3).reshape(DEPTH, 16, ncol)
    cond_ext = jnp.concatenate([c_all, jnp.broadcast_to(c_ctx[None], (NDEV, D))])[None]
    scond_ext = _rw_fwd(f_silu, [(cond_ext, 0)], [], [(D, BF16)], name="ada_silu_ext")[0]
    g_ada_w = _mm(scond_ext, dm_ext, "tn", name="ada_proj_bw")
    dsil = _mm(dm_ext, ada_bf, "nt", reduce_blocks=True, name="ada_proj_bx")
    dcc = _rw_bwd(f_silu_rows, [(jnp.zeros((1, NDEV, D), F32), 0)], [(c_ctx.reshape(1, 1, D), "one")],
                  [dsil[:, NDEV:]], name="ada_silu_b", param_grad=[0])[0]

    out = {}

    def put(name, res):
        out[name] = res

    col_to_parts = lambda g: g[0].reshape(g.shape[1], NDEV, g.shape[2] // NDEV).transpose(1, 0, 2)
    row_to_parts = lambda g: g.reshape(NDEV, -1, g.shape[-1])
    send = [
        [G[f"f{i}"]["up"] for i in range(DEPTH)],
        [row_to_parts(G[f"f{i}"]["down"]) for i in range(DEPTH)],
        [col_to_parts(G["m0"]["w_in"]), col_to_parts(G["m3"]["w_in"])],
        [row_to_parts(G["m0"]["w_out"]), row_to_parts(G["m3"]["w_out"])],
        [col_to_parts(G["m1"]["w_qkv"])], [row_to_parts(G["m1"]["w_o"])],
        [col_to_parts(G["m2"]["w_in"])], [row_to_parts(G["m2"]["w_out"])],
    ]
    d_norm_g = jnp.stack([jnp.stack([G[f"m{i}"]["ng0"], G[f"m{i}"]["ng1"], G[f"f{i}"]["ng2"], G[f"f{i}"]["ng3"]])
                          for i in range(DEPTH)]).reshape(DEPTH, 4, D)
    small_full = dict(
        norm_g=d_norm_g,
        cm_b_in=jnp.stack([G["m0"]["b_in"], G["m3"]["b_in"]]).reshape(2, 2 * D),
        cm_dw_w=jnp.stack([G["m0"]["dw_w"][0], G["m3"]["dw_w"][0]]),
        cm_dw_b=jnp.stack([G["m0"]["dw_b"], G["m3"]["dw_b"]]).reshape(2, D),
        cm_ln_g=jnp.stack([G["m0"]["ln_g"], G["m3"]["ln_g"]]).reshape(2, D),
        cm_ln_b=jnp.stack([G["m0"]["ln_b"], G["m3"]["ln_b"]]).reshape(2, D),
        cm_b_out=jnp.stack([G["m0"]["b_out"], G["m3"]["b_out"]]).reshape(2, D),
        gm_b_in=G["m2"]["b_in"].reshape(1, 2 * GM_W),
        gm_ln_g=G["m2"]["ln_g"].reshape(1, GM_W), gm_ln_b=G["m2"]["ln_b"].reshape(1, GM_W))
    by_dest = []
    for n, _ in SMALL:
        if n == "ffn_conv_w":
            by_dest.append(jnp.stack([G[f"f{i}"]["cw"] for i in range(DEPTH)], axis=1))
        else:
            by_dest.append(_shard_last(small_full[n]))
    small_send = jax.vmap(lambda *vs: _pack_rows(list(vs)))(*by_dest)
    recv = _all_to_all(send + [[small_send]], "exchange_grads")

    def shard3(a):
        return a.reshape(a.shape[0], -1, a.shape[-1])

    for (n, nl), parts in zip(big_names, recv[:-1]):
        shp = W[n].shape
        res = _adamw(parts, shard3(W[n]), shard3(M[n]), shard3(V[n]), "adamw_" + n)
        put(n, [r.reshape(shp) for r in res])

    small_local = lambda d_: _pack_rows([d_[n] for n, _ in SMALL])[None]
    res = _adamw(recv[-1], small_local(W), small_local(M), small_local(V), "adamw_small")
    unp = [_unpack_rows(r[0], small_shapes) for r in res]
    for q, (n, _) in enumerate(SMALL):
        put(n, [unp[t][q] for t in range(4)])

    repl_names = ["c_ctx", "ffn_conv_b", "attn_sink", "gm_b_s", "gm_w_s"]
    repl_part = dict(
        c_ctx=dcc.reshape(D),
        ffn_conv_b=jnp.stack([G[f"f{i}"]["cb"].reshape(2 * 2816) for i in range(DEPTH)]),
        attn_sink=G["m1"]["sink"].reshape(1, N_Q),
        gm_b_s=G["m2"]["b_s"].reshape(1, GM_GROUPS, GM_CHUNK),
        gm_w_s=G["m2"]["w_s"].reshape(1, GM_GROUPS, GM_CHUNK, GM_CHUNK))
    repl_shapes = [W[n].shape for n in repl_names]
    repl_g = _all_gather([_pack_rows([repl_part[n] for n in repl_names], mult=256)], "gather_repl")[0]
    repl_local = lambda d_: _pack_rows([d_[n] for n in repl_names], mult=256)[None]
    res = _adamw(repl_g[None], repl_local(W), repl_local(M), repl_local(V), "adamw_repl")
    unp = [_unpack_rows(r[0], repl_shapes) for r in res]
    for q, n in enumerate(repl_names):
        put(n, [unp[t][q] for t in range(4)])

    put("ada_w", _adamw(g_ada_w[:, None], ada_w, m_ada_w, v_ada_w, "adamw_ada_w"))
    ada_b_parts = dmod_g.reshape(1, 2 * NDEV, DEPTH, 6 * D)
    res = _adamw(ada_b_parts, ada_b[None], m_ada_b[None], v_ada_b[None], "adamw_ada_b")
    put("ada_b", [r[0] for r in res])

    names = ["c_ctx", "ada_w", "ada_b", "norm_g", "ffn_w_up", "ffn_conv_w", "ffn_conv_b", "ffn_w_down", "cm_w_in",
             "cm_b_in", "cm_dw_w", "cm_dw_b", "cm_ln_g", "cm_ln_b", "cm_w_out", "cm_b_out", "attn_w_qkv",
             "attn_sink", "attn_w_o", "gm_w_in", "gm_b_in", "gm_ln_g", "gm_ln_b", "gm_w_s", "gm_b_s", "gm_w_out"]
    return (loss, grad_x, *[out[n][0] for n in names], *[out[n][1] for n in names],
            *[out[n][2] for n in names], *[out[n][3] for n in names])
```

```python
import functools

import jax
import jax.numpy as jnp
from jax import lax
from jax.experimental import pallas as pl
from jax.experimental.pallas import tpu as pltpu

F32, BF16 = jnp.float32, jnp.bfloat16
MESH = pl.DeviceIdType.MESH
AXES = ("x", "y", "c")
NDEV = 8

D = 1024
L = 2048
LC = 256
TA = L + LC
DEPTH = 4
EPS = 1e-6
HEAD_DIM = 64
N_Q, N_KV, Q_PER_KV = 16, 4, 4
ATTN_BLOCK = 128
GRID_W = 64
ROPE_BASE = 10000.0
GM_W = 2048
GM_CHUNK = 128
GM_GROUPS = 16
FFN_BLK = 704
CM_K, FFN_K = 31, 3

ADAM_LR, ADAM_B1, ADAM_B2, ADAM_EPS, ADAM_WD, ADAM_STEP = 0.001, 0.9, 0.999, 1e-08, 0.01, 10

VMEM_LIMIT_V7X = 56 * 1024 * 1024
ROW_TILE_ELEMS = 256 * 1024


def _cparams(sem=None):
    kw = dict(vmem_limit_bytes=VMEM_LIMIT_V7X)
    if sem is not None:
        kw["dimension_semantics"] = sem
    return pltpu.CompilerParams(**kw)


def _pick(n, cands):
    for c in cands:
        if n % c == 0:
            return c
    return n


def _as3(a):
    return a if a.ndim == 3 else a[None]


def _all_gather(arrs, name):
    n = len(arrs)

    def body(*refs):
        ins, outs = refs[:n], refs[n:2 * n]
        send_sems, recv_sems, local_sems = refs[2 * n:]
        x, y, c = lax.axis_index("x"), lax.axis_index("y"), lax.axis_index("c")
        me, sibling = (x, y, c), (x, y, 1 - c)
        chips = [(1 - x, y), (x, 1 - y), (1 - x, 1 - y)]

        def slot(a, p):
            return outs[a].at[4 * p[0] + 2 * p[1] + p[2]]

        def copy(a, k, block, to, src=None):
            return pltpu.make_async_remote_copy(
                src_ref=slot(a, block) if src is None else src, dst_ref=slot(a, block),
                send_sem=send_sems.at[a, k], recv_sem=recv_sems.at[a, k],
                device_id=to, device_id_type=MESH)

        mine = [pltpu.make_async_copy(ins[a], slot(a, me), local_sems.at[a]) for a in range(n)]
        for m in mine:
            m.start()
        first = []
        for a in range(n):
            first.append(copy(a, 0, me, sibling, src=ins[a]))
            first += [copy(a, 1 + j, me, (*chip, c), src=ins[a]) for j, chip in enumerate(chips)]
        for cp in first:
            cp.start()
        passed = []
        for j, chip in enumerate(chips):
            for a in range(n):
                copy(a, 1 + j, (*chip, c), me).wait_recv()
                p = copy(a, 4 + j, (*chip, c), sibling)
                p.start()
                passed.append(p)
        for a in range(n):
            copy(a, 0, sibling, me).wait_recv()
            for j, chip in enumerate(chips):
                copy(a, 4 + j, (*chip, 1 - c), me).wait_recv()
        for cp in first + passed:
            cp.wait_send()
        for m in mine:
            m.wait()

    any_spec = pl.BlockSpec(memory_space=pl.ANY)
    outs = pl.pallas_call(
        body, name=name,
        out_shape=[jax.ShapeDtypeStruct((NDEV,) + a.shape, a.dtype) for a in arrs],
        in_specs=[any_spec] * n, out_specs=[any_spec] * n,
        scratch_shapes=[pltpu.SemaphoreType.DMA((n, 7)), pltpu.SemaphoreType.DMA((n, 7)),
                        pltpu.SemaphoreType.DMA((n,))],
    )(*arrs)
    return list(outs)


def _all_to_all(groups, name):
    flat = [(gi, li, a) for gi, g in enumerate(groups) for li, a in enumerate(g)]
    n, ng = len(flat), len(groups)

    def body(*refs):
        ins, outs = refs[:n], refs[n:n + ng]
        send_sems, recv_sems, local_sems = refs[n + ng:]
        x, y, c = lax.axis_index("x"), lax.axis_index("y"), lax.axis_index("c")
        me = 4 * x + 2 * y + c
        copies = []
        for a, (gi, li, _) in enumerate(flat):
            loc = pltpu.make_async_copy(ins[a].at[me], outs[gi].at[li, me], local_sems.at[a])
            loc.start()
            copies.append(loc)
            for k in range(1, NDEV):
                px = 1 - x if (k >> 2) & 1 else x
                py = 1 - y if (k >> 1) & 1 else y
                pc = 1 - c if k & 1 else c
                cp = pltpu.make_async_remote_copy(
                    src_ref=ins[a].at[4 * px + 2 * py + pc], dst_ref=outs[gi].at[li, me],
                    send_sem=send_sems.at[a, k - 1], recv_sem=recv_sems.at[a, k - 1],
                    device_id=(px, py, pc), device_id_type=MESH)
                cp.start()
                copies.append(cp)
        for cp in copies:
            cp.wait()

    any_spec = pl.BlockSpec(memory_space=pl.ANY)
    outs = pl.pallas_call(
        body, name=name,
        out_shape=[jax.ShapeDtypeStruct((len(g),) + g[0].shape, g[0].dtype) for g in groups],
        in_specs=[any_spec] * n, out_specs=[any_spec] * ng,
        scratch_shapes=[pltpu.SemaphoreType.DMA((n, 7)), pltpu.SemaphoreType.DMA((n, 7)),
                        pltpu.SemaphoreType.DMA((n,))],
    )(*[a for _, _, a in flat])
    return list(outs)


def _mm(a, b, kind, *, name, out_dtype=F32, reduce_blocks=False):
    a, b = _as3(a), _as3(b)
    nba, nbb = a.shape[0], b.shape[0]
    nb = max(nba, nbb)
    assert nba in (1, nb) and nbb in (1, nb)
    if kind == "tn":
        t, m = a.shape[1:]
        n = b.shape[2]
        assert b.shape[1] == t and not reduce_blocks
        tk = _pick(t, (512, 768, 256))
        tm = m if m <= 1024 else _pick(m, (1024,))
        nred = t // tk
    else:
        m, k = a.shape[1:]
        n = b.shape[2] if kind == "nn" else b.shape[1]
        assert (b.shape[1] if kind == "nn" else b.shape[2]) == k
        tm = _pick(m, (512, 768, 256))
        nred = nb if reduce_blocks else 1
    tn = n if n <= 1024 else _pick(n, (512,))
    nbo = 1 if reduce_blocks else nb

    def blk(nbx, g, r):
        if nbx == 1:
            return 0
        return r if reduce_blocks else g

    if kind == "nn":
        a_spec = pl.BlockSpec((1, tm, k), lambda g, j, i, r: (blk(nba, g, r), i, 0))
        b_spec = pl.BlockSpec((1, k, tn), lambda g, j, i, r: (blk(nbb, g, r), 0, j))
        dims = (((1,), (0,)), ((), ()))
    elif kind == "nt":
        a_spec = pl.BlockSpec((1, tm, k), lambda g, j, i, r: (blk(nba, g, r), i, 0))
        b_spec = pl.BlockSpec((1, tn, k), lambda g, j, i, r: (blk(nbb, g, r), j, 0))
        dims = (((1,), (1,)), ((), ()))
    else:
        a_spec = pl.BlockSpec((1, tk, tm), lambda g, j, i, r: (blk(nba, g, r), r, i))
        b_spec = pl.BlockSpec((1, tk, tn), lambda g, j, i, r: (blk(nbb, g, r), r, j))
        dims = (((0,), (0,)), ((), ()))
    o_spec = pl.BlockSpec((1, tm, tn), lambda g, j, i, r: (g, i, j))

    def body(a_ref, b_ref, o_ref, *scratch):
        prod = lax.dot_general(a_ref[0].astype(BF16), b_ref[0].astype(BF16), dims,
                               preferred_element_type=F32)
        if nred == 1:
            o_ref[0] = prod.astype(o_ref.dtype)
        else:
            acc = scratch[0]
            r = pl.program_id(3)

            @pl.when(r == 0)
            def _():
                acc[...] = prod

            @pl.when(r > 0)
            def _():
                acc[...] += prod

            @pl.when(r == nred - 1)
            def _():
                o_ref[0] = acc[...].astype(o_ref.dtype)

    return pl.pallas_call(
        body, name=name,
        out_shape=jax.ShapeDtypeStruct((nbo, m, n), out_dtype),
        grid=(nbo, n // tn, m // tm, nred),
        in_specs=[a_spec, b_spec], out_specs=o_spec,
        scratch_shapes=[pltpu.VMEM((tm, tn), F32)] if nred > 1 else [],
        compiler_params=_cparams(("parallel", "parallel", "parallel", "arbitrary")),
    )(a, b)


def _row_tile(t, widths):
    tm = max(16, ROW_TILE_ELEMS // max(widths))
    tm = min(tm, 256)
    return t if t < tm else tm


def _sel_index(sel, g, i, tm):
    if sel == "one":
        return 0
    if sel == "seg":
        return (i * tm) // L
    return g + sel


def _row_spec(arr, off, tm):
    return pl.BlockSpec((1, tm, arr.shape[2]), lambda g, i: (g + off, i, 0))


def _par_spec(arr, sel, tm):
    return pl.BlockSpec((1, 1, arr.shape[2]), lambda g, i: (_sel_index(sel, g, i, tm), 0, 0))


def _norm_ops(ops):
    return [(o[0], o[1], o[2] if len(o) > 2 else 1) for o in ops]


def _split_cols(vals, nsplit):
    out = []
    for v, ns in zip(vals, nsplit):
        w = v.shape[1] // ns
        out += [v] if ns == 1 else [v[:, q * w:(q + 1) * w] for q in range(ns)]
    return out


def _join_cols(flat, nsplit):
    out, pos = [], 0
    for ns in nsplit:
        out.append(flat[pos] if ns == 1 else jnp.concatenate(flat[pos:pos + ns], axis=1))
        pos += ns
    return out


def _rw_fwd(fn, rows, params, outs, *, name, nblk=None):
    rows, params = _norm_ops(rows), _norm_ops(params)
    t = rows[0][0].shape[1]
    nblk = nblk or rows[0][0].shape[0]
    tm = _row_tile(t, [r.shape[2] for r, _, _ in rows] + [w for w, _ in outs])
    nr, npar = len(rows), len(params)
    nsplit = [ns for _, _, ns in rows + params]

    def body(*refs):
        vals = _split_cols([r[0].astype(F32) for r in refs[:nr + npar]], nsplit)
        res = fn(*vals)
        for o_ref, o in zip(refs[nr + npar:], res):
            o_ref[0] = o.astype(o_ref.dtype)

    res = pl.pallas_call(
        body, name=name,
        out_shape=[jax.ShapeDtypeStruct((nblk, t, w), dt) for w, dt in outs],
        grid=(nblk, t // tm),
        in_specs=[_row_spec(r, off, tm) for r, off, _ in rows] + [_par_spec(p, s, tm) for p, s, _ in params],
        out_specs=[pl.BlockSpec((1, tm, w), lambda g, i: (g, i, 0)) for w, _ in outs],
        compiler_params=_cparams(("parallel", "parallel")),
    )(*[r for r, _, _ in rows], *[p for p, _, _ in params])
    return list(res)


def _rw_bwd(fn, rows, params, cts, *, name, row_grad=(), param_grad=(), add=None, nblk=None):
    rows, params = _norm_ops(rows), _norm_ops(params)
    t = cts[0].shape[1]
    nblk = nblk or cts[0].shape[0]
    tm = _row_tile(t, [r.shape[2] for r, _, _ in rows] + [c.shape[2] for c in cts])
    ni = t // tm
    nr, npar, nct = len(rows), len(params), len(cts)
    nadd = 0 if add is None else 1
    n_in = nr + npar + nct + nadd
    nsplit = [ns for _, _, ns in rows + params]

    def body(*refs):
        prim = _split_cols([r[0].astype(F32) for r in refs[:nr + npar]], nsplit)
        ct = tuple(r[0].astype(F32) for r in refs[nr + npar:nr + npar + nct])
        _, vjp = jax.vjp(fn, *prim)
        grads = _join_cols(list(vjp(ct)), nsplit)
        out_refs = refs[n_in:]
        for q, (ri, _) in enumerate(row_grad):
            gr = grads[ri]
            if q == 0 and nadd:
                gr = gr + refs[n_in - 1][0].astype(F32)
            out_refs[q][0] = gr.astype(out_refs[q].dtype)
        g, i = pl.program_id(0), pl.program_id(1)
        step = g * ni + i
        pg, pi = (step - 1) // ni, (step - 1) % ni
        for q, pidx in enumerate(param_grad):
            o_ref = out_refs[len(row_grad) + q]
            sel = params[pidx][1]
            val = grads[nr + pidx]
            if sel == "one":
                first = step == 0
            else:
                first = (step == 0) | (_sel_index(sel, g, i, tm) != _sel_index(sel, pg, pi, tm))

            @pl.when(first)
            def _(o_ref=o_ref, val=val):
                o_ref[0] = val

            @pl.when(jnp.logical_not(first))
            def _(o_ref=o_ref, val=val):
                o_ref[0] += val

    in_arrays = [r for r, _, _ in rows] + [p for p, _, _ in params] + list(cts) + ([add] if nadd else [])
    in_specs = ([_row_spec(r, off, tm) for r, off, _ in rows] + [_par_spec(p, s, tm) for p, s, _ in params]
                + [_row_spec(c, 0, tm) for c in cts] + ([_row_spec(add, 0, tm)] if nadd else []))
    out_shape, out_specs = [], []
    for ri, dt in row_grad:
        w = rows[ri][0].shape[2]
        out_shape.append(jax.ShapeDtypeStruct((nblk, t, w), dt))
        out_specs.append(pl.BlockSpec((1, tm, w), lambda g, i: (g, i, 0)))
    for pidx in param_grad:
        p, sel, _ = params[pidx]
        out_shape.append(jax.ShapeDtypeStruct(p.shape, F32))
        out_specs.append(_par_spec(p, sel, tm))
    res = pl.pallas_call(
        body, name=name, out_shape=out_shape, grid=(nblk, ni),
        in_specs=in_specs, out_specs=out_specs,
        compiler_params=_cparams(("arbitrary", "arbitrary")),
    )(*in_arrays)
    return list(res)


def _sigmoid(x):
    return 1.0 / (1.0 + jnp.exp(-x))


def _rms(x, g):
    return x * lax.rsqrt(jnp.mean(x * x, axis=-1, keepdims=True) + EPS) * g


def _ln(x, g, b):
    mu = jnp.mean(x, axis=-1, keepdims=True)
    xc = x - mu
    var = jnp.mean(xc * xc, axis=-1, keepdims=True)
    return xc * lax.rsqrt(var + EPS) * g + b


def _gelu_tanh(x):
    return 0.5 * x * (1.0 + jnp.tanh(0.7978845608028654 * (x + 0.044715 * (x * x * x))))


def f_modnorm(h, g, sc, sh):
    return (_rms(h, g) * (1.0 + sc) + sh,)


def f_gate_rms(y, gate, g):
    return (gate * _rms(y, g),)


def f_gate_rms_bias(y, gate, g, b):
    return (gate * _rms(y + b, g),)


def f_resgate(h, y, gate, g):
    return (h + gate * _rms(y, g),)


def f_resgate_bias(h, y, gate, g, b):
    return (h + gate * _rms(y + b, g),)


def f_glu(pa, pg, ba, bg):
    return ((pa + ba) * _sigmoid(pg + bg),)


def f_lnsilu(z, g, b):
    t = _ln(z, g, b)
    return (t * _sigmoid(t),)


def f_gmlp_pre(pu, pv, bu, bv, g, bb):
    return _gelu_tanh(pu + bu), _ln(_gelu_tanh(pv + bv), g, bb)


def f_ffn_gate(zg, zv):
    return (zg * _sigmoid(zg) * zv,)


def f_silu(x):
    return (x * _sigmoid(x),)


def f_silu_rows(dummy, cc):
    return (cc * _sigmoid(cc) + 0.0 * dummy,)


def _rope(x_in, tables, neg_sin, out_dtype, name):
    w = x_in.shape[2]
    sign = -1.0 if neg_sin else 1.0

    def fn(x, cos, sin):
        cos = jnp.tile(cos, (1, w // 128))
        sin = jnp.tile(sin, (1, w // 128)) * sign
        lane = lax.broadcasted_iota(jnp.int32, x.shape, 1) & 31
        rot = jnp.where(lane < 16, -pltpu.roll(x, w - 16, 1), pltpu.roll(x, 16, 1))
        return (x * cos + rot * sin,)

    return _rw_fwd(fn, [(x_in, 0), (tables[0], 0), (tables[1], 0)], [], [(w, out_dtype)], name=name)[0]


CONV_TM = 256
CONV_RC = 32


def _conv_geometry(x, k):
    nb, t, w = x.shape
    halo = 16 if k > 17 else 8
    cb = _pick(w, (512,)) if w > 768 else w
    return nb, t, w, halo, cb, (k - 1) // 2


def _conv_in_specs(t, halo, cb):
    per = CONV_TM // halo
    last = t // halo - 1
    return [
        pl.BlockSpec((1, CONV_TM, cb), lambda g, jc, i: (g, i, jc)),
        pl.BlockSpec((1, halo, cb), lambda g, jc, i: (g, jnp.maximum(i * per - 1, 0), jc)),
        pl.BlockSpec((1, halo, cb), lambda g, jc, i: (g, jnp.minimum((i + 1) * per, last), jc)),
    ]


def _conv_fill(xp, x_ref, prev_ref, next_ref, halo, t):
    i = pl.program_id(2)
    seg_first = (i * CONV_TM == 0) | (i * CONV_TM == L)
    seg_last = ((i + 1) * CONV_TM == L) | ((i + 1) * CONV_TM == t)
    xp[0:halo, :] = jnp.where(seg_first, 0.0, prev_ref[0].astype(F32))
    xp[halo:halo + CONV_TM, :] = x_ref[0].astype(F32)
    xp[halo + CONV_TM:, :] = jnp.where(seg_last, 0.0, next_ref[0].astype(F32))


def _dwconv(x, w, b, *, name, out_dtype=F32):
    k = w.shape[1]
    nb, t, wd, halo, cb, half = _conv_geometry(x, k)
    base = halo - half

    def body(*refs):
        x_ref, prev_ref, next_ref, w_ref = refs[:4]
        b_ref = refs[4] if b is not None else None
        o_ref, xp = refs[-2], refs[-1]
        _conv_fill(xp, x_ref, prev_ref, next_ref, halo, t)
        for r0 in range(0, CONV_TM, CONV_RC):
            acc = jnp.zeros((CONV_RC, cb), F32)
            for kk in range(k):
                acc = acc + w_ref[0, kk:kk + 1, :] * xp[r0 + base + kk:r0 + base + kk + CONV_RC, :]
            if b_ref is not None:
                acc = acc + b_ref[0]
            o_ref[0, r0:r0 + CONV_RC, :] = acc.astype(o_ref.dtype)

    in_specs = _conv_in_specs(t, halo, cb) + [pl.BlockSpec((1, k, cb), lambda g, jc, i: (g, 0, jc))]
    args = [x, x, x, w]
    if b is not None:
        in_specs.append(pl.BlockSpec((1, 1, cb), lambda g, jc, i: (g, 0, jc)))
        args.append(b)
    return pl.pallas_call(
        body, name=name, out_shape=jax.ShapeDtypeStruct((nb, t, wd), out_dtype),
        grid=(nb, wd // cb, t // CONV_TM), in_specs=in_specs,
        out_specs=pl.BlockSpec((1, CONV_TM, cb), lambda g, jc, i: (g, i, jc)),
        scratch_shapes=[pltpu.VMEM((CONV_TM + 2 * halo, cb), F32)],
        compiler_params=_cparams(("parallel", "parallel", "parallel")),
    )(*args)


def _dwconv_wgrad(x, dy, k, *, name):
    nb, t, wd, halo, cb, half = _conv_geometry(x, k)
    base = halo - half

    def body(x_ref, prev_ref, next_ref, dy_ref, dw_ref, db_ref, xp):
        _conv_fill(xp, x_ref, prev_ref, next_ref, halo, t)
        i = pl.program_id(2)

        @pl.when(i == 0)
        def _():
            dw_ref[...] = jnp.zeros_like(dw_ref)
            db_ref[...] = jnp.zeros_like(db_ref)

        dyv = dy_ref[0].astype(F32)
        db_ref[0] += jnp.sum(dyv, axis=0, keepdims=True)
        for kk in range(k):
            dw_ref[0, kk:kk + 1, :] += jnp.sum(dyv * xp[base + kk:base + kk + CONV_TM, :], axis=0, keepdims=True)

    dw, db = pl.pallas_call(
        body, name=name,
        out_shape=[jax.ShapeDtypeStruct((nb, k, wd), F32), jax.ShapeDtypeStruct((nb, 1, wd), F32)],
        grid=(nb, wd // cb, t // CONV_TM),
        in_specs=_conv_in_specs(t, halo, cb) + [pl.BlockSpec((1, CONV_TM, cb), lambda g, jc, i: (g, i, jc))],
        out_specs=[pl.BlockSpec((1, k, cb), lambda g, jc, i: (g, 0, jc)),
                   pl.BlockSpec((1, 1, cb), lambda g, jc, i: (g, 0, jc))],
        scratch_shapes=[pltpu.VMEM((CONV_TM + 2 * halo, cb), F32)],
        compiler_params=_cparams(("parallel", "parallel", "arbitrary")),
    )(x, x, x, dy)
    return dw, db


ATTN_SCALE = HEAD_DIM ** -0.5
QROWS = Q_PER_KV * ATTN_BLOCK
NEG = -1e30


def _attn_scores(q, kw, kc, n):
    nt = (((1,), (1,)), ((), ()))
    s_w = lax.dot_general(q, kw, nt, preferred_element_type=F32) * ATTN_SCALE
    qi = lax.broadcasted_iota(jnp.int32, s_w.shape, 0) & (ATTN_BLOCK - 1)
    kj = lax.broadcasted_iota(jnp.int32, s_w.shape, 1)
    key_abs = (n - 1) * ATTN_BLOCK + kj
    ok = (jnp.abs(qi + ATTN_BLOCK - kj) <= ATTN_BLOCK) & (key_abs >= 0) & (key_abs < L)
    s_w = jnp.where(ok, s_w, NEG)
    s_c = lax.dot_general(q, kc, nt, preferred_element_type=F32) * ATTN_SCALE
    return s_w, s_c


def _sink_col(sink_ref, hk):
    return jnp.concatenate([jnp.full((ATTN_BLOCK, 1), sink_ref[hk * Q_PER_KV + g], F32) for g in range(Q_PER_KV)], axis=0)


def _attn_specs():
    qspec = pl.BlockSpec((Q_PER_KV, ATTN_BLOCK, HEAD_DIM), lambda hk, n: (hk, n, 0))
    kspec = pl.BlockSpec((1, L + 2 * ATTN_BLOCK, HEAD_DIM), lambda hk, n: (hk, 0, 0))
    cspec = pl.BlockSpec((1, LC, HEAD_DIM), lambda hk, n: (hk, 0, 0))
    lspec = pl.BlockSpec((Q_PER_KV, ATTN_BLOCK, 1), lambda hk, n: (hk, n, 0))
    sspec = pl.BlockSpec(memory_space=pltpu.SMEM)
    return qspec, kspec, cspec, lspec, sspec


def _attn_fwd(q, k, v, kc, vc, sink):
    qspec, kspec, cspec, lspec, sspec = _attn_specs()

    def body(q_ref, k_ref, v_ref, kc_ref, vc_ref, sink_ref, o_ref, lse_ref):
        hk, n = pl.program_id(0), pl.program_id(1)
        qv = q_ref[...].reshape(QROWS, HEAD_DIM)
        start = pl.multiple_of(n * ATTN_BLOCK, ATTN_BLOCK)
        kw = k_ref[0, pl.ds(start, 3 * ATTN_BLOCK), :]
        vw = v_ref[0, pl.ds(start, 3 * ATTN_BLOCK), :]
        s_w, s_c = _attn_scores(qv, kw, kc_ref[0], n)
        sk = _sink_col(sink_ref, hk)
        m = jnp.maximum(jnp.maximum(jnp.max(s_w, -1, keepdims=True), jnp.max(s_c, -1, keepdims=True)), sk)
        p_w, p_c = jnp.exp(s_w - m), jnp.exp(s_c - m)
        den = jnp.sum(p_w, -1, keepdims=True) + jnp.sum(p_c, -1, keepdims=True) + jnp.exp(sk - m)
        o = (jnp.dot(p_w.astype(BF16), vw, preferred_element_type=F32)
             + jnp.dot(p_c.astype(BF16), vc_ref[0], preferred_element_type=F32)) / den
        o_ref[...] = o.reshape(Q_PER_KV, ATTN_BLOCK, HEAD_DIM).astype(o_ref.dtype)
        lse_ref[...] = (m + jnp.log(den)).reshape(Q_PER_KV, ATTN_BLOCK, 1)

    return pl.pallas_call(
        body, name="attn_fwd",
        out_shape=[jax.ShapeDtypeStruct((N_Q, L, HEAD_DIM), BF16), jax.ShapeDtypeStruct((N_Q, L, 1), F32)],
        grid=(N_KV, L // ATTN_BLOCK),
        in_specs=[qspec, kspec, kspec, cspec, cspec, sspec], out_specs=[qspec, lspec],
        compiler_params=_cparams(("parallel", "parallel")),
    )(q, k, v, kc, vc, sink)


def _attn_bwd(q, k, v, kc, vc, sink, o, lse, do):
    qspec, kspec, cspec, lspec, sspec = _attn_specs()
    tn = (((0,), (0,)), ((), ()))
    nt = (((1,), (1,)), ((), ()))

    def body(q_ref, k_ref, v_ref, kc_ref, vc_ref, sink_ref, o_ref, lse_ref, do_ref,
             dq_ref, dk_ref, dv_ref, dkc_ref, dvc_ref, dsink_ref):
        hk, n = pl.program_id(0), pl.program_id(1)
        qv = q_ref[...].reshape(QROWS, HEAD_DIM)
        start = pl.multiple_of(n * ATTN_BLOCK, ATTN_BLOCK)
        win = pl.ds(start, 3 * ATTN_BLOCK)
        kw, vw = k_ref[0, win, :], v_ref[0, win, :]
        kcv, vcv = kc_ref[0], vc_ref[0]
        s_w, s_c = _attn_scores(qv, kw, kcv, n)
        lse_v = lse_ref[...].reshape(QROWS, 1)
        p_w, p_c = jnp.exp(s_w - lse_v), jnp.exp(s_c - lse_v)
        dov = do_ref[...].reshape(QROWS, HEAD_DIM).astype(F32)
        ov = o_ref[...].reshape(QROWS, HEAD_DIM).astype(F32)
        delta = jnp.sum(dov * ov, -1, keepdims=True)
        dob = dov.astype(BF16)
        dp_w = lax.dot_general(dob, vw, nt, preferred_element_type=F32)
        dp_c = lax.dot_general(dob, vcv, nt, preferred_element_type=F32)
        ds_w = (p_w * (dp_w - delta) * ATTN_SCALE).astype(BF16)
        ds_c = (p_c * (dp_c - delta) * ATTN_SCALE).astype(BF16)
        dq = jnp.dot(ds_w, kw, preferred_element_type=F32) + jnp.dot(ds_c, kcv, preferred_element_type=F32)
        dq_ref[...] = dq.reshape(Q_PER_KV, ATTN_BLOCK, HEAD_DIM)

        @pl.when(n == 0)
        def _():
            dk_ref[...] = jnp.zeros_like(dk_ref)
            dv_ref[...] = jnp.zeros_like(dv_ref)
            dkc_ref[...] = jnp.zeros_like(dkc_ref)
            dvc_ref[...] = jnp.zeros_like(dvc_ref)

        dk_ref[0, win, :] += lax.dot_general(ds_w, qv, tn, preferred_element_type=F32)
        dv_ref[0, win, :] += lax.dot_general(p_w.astype(BF16), dob, tn, preferred_element_type=F32)
        dkc_ref[0] += lax.dot_general(ds_c, qv, tn, preferred_element_type=F32)
        dvc_ref[0] += lax.dot_general(p_c.astype(BF16), dob, tn, preferred_element_type=F32)
        dsk = -jnp.exp(_sink_col(sink_ref, hk) - lse_v) * delta
        for g in range(Q_PER_KV):
            part = jnp.sum(dsk[g * ATTN_BLOCK:(g + 1) * ATTN_BLOCK])
            idx = hk * Q_PER_KV + g

            @pl.when(n == 0)
            def _(part=part, idx=idx):
                dsink_ref[idx] = part

            @pl.when(n > 0)
            def _(part=part, idx=idx):
                dsink_ref[idx] += part

    kshape = jax.ShapeDtypeStruct((N_KV, L + 2 * ATTN_BLOCK, HEAD_DIM), F32)
    cshape = jax.ShapeDtypeStruct((N_KV, LC, HEAD_DIM), F32)
    return pl.pallas_call(
        body, name="attn_bwd",
        out_shape=[jax.ShapeDtypeStruct((N_Q, L, HEAD_DIM), F32), kshape, kshape, cshape, cshape,
                   jax.ShapeDtypeStruct((N_Q,), F32)],
        grid=(N_KV, L // ATTN_BLOCK),
        in_specs=[qspec, kspec, kspec, cspec, cspec, sspec, qspec, lspec, qspec],
        out_specs=[qspec, kspec, kspec, cspec, cspec, sspec],
        compiler_params=_cparams(("arbitrary", "arbitrary")),
    )(q, k, v, kc, vc, sink, o, lse, do)


def _gm_specs():
    rspec = pl.BlockSpec((1, GM_CHUNK, GM_W), lambda n: (0, n, 0))
    wspec = pl.BlockSpec((GM_GROUPS, GM_CHUNK, GM_CHUNK), lambda n: (0, 0, 0))
    bspec = pl.BlockSpec((GM_GROUPS, GM_CHUNK, 1), lambda n: (0, 0, 0))
    return rspec, wspec, bspec


def _gm_spatial_fwd(u, v, ws, bs):
    rspec, wspec, bspec = _gm_specs()

    def body(u_ref, v_ref, ws_ref, bs_ref, o_ref):
        for g in range(GM_GROUPS):
            cols = slice(g * GM_CHUNK, (g + 1) * GM_CHUNK)
            s = jnp.dot(ws_ref[g], v_ref[0, :, cols], preferred_element_type=F32) + bs_ref[g]
            o_ref[0, :, cols] = (u_ref[0, :, cols] * s).astype(o_ref.dtype)

    return pl.pallas_call(
        body, name="gm_spatial_fwd", out_shape=jax.ShapeDtypeStruct((1, L, GM_W), BF16),
        grid=(L // GM_CHUNK,), in_specs=[rspec, rspec, wspec, bspec], out_specs=rspec,
        compiler_params=_cparams(("parallel",)),
    )(u, v, ws, bs)


def _gm_spatial_bwd(u, v, ws, bs, dus):
    rspec, wspec, bspec = _gm_specs()
    tn = (((0,), (0,)), ((), ()))
    nt = (((1,), (1,)), ((), ()))

    def body(u_ref, v_ref, ws_ref, bs_ref, d_ref, du_ref, dv_ref, dws_ref, dbs_ref):
        n = pl.program_id(0)

        @pl.when(n == 0)
        def _():
            dws_ref[...] = jnp.zeros_like(dws_ref)
            dbs_ref[...] = jnp.zeros_like(dbs_ref)

        for g in range(GM_GROUPS):
            cols = slice(g * GM_CHUNK, (g + 1) * GM_CHUNK)
            vb = v_ref[0, :, cols]
            s = jnp.dot(ws_ref[g], vb, preferred_element_type=F32) + bs_ref[g]
            d = d_ref[0, :, cols].astype(F32)
            du_ref[0, :, cols] = d * s
            ds = d * u_ref[0, :, cols]
            dsb = ds.astype(BF16)
            dv_ref[0, :, cols] = lax.dot_general(ws_ref[g], dsb, tn, preferred_element_type=F32)
            dws_ref[g] += lax.dot_general(dsb, vb, nt, preferred_element_type=F32)
            dbs_ref[g] += jnp.sum(ds, axis=1, keepdims=True)

    row = jax.ShapeDtypeStruct((1, L, GM_W), F32)
    return pl.pallas_call(
        body, name="gm_spatial_bwd",
        out_shape=[row, row, jax.ShapeDtypeStruct((GM_GROUPS, GM_CHUNK, GM_CHUNK), F32),
                   jax.ShapeDtypeStruct((GM_GROUPS, GM_CHUNK, 1), F32)],
        grid=(L // GM_CHUNK,), in_specs=[rspec, rspec, wspec, bspec, rspec],
        out_specs=[rspec, rspec, wspec, bspec],
        compiler_params=_cparams(("arbitrary",)),
    )(u, v, ws, bs, dus)


def _loss_head(h, target):
    tm = 256

    def body(h_ref, t_ref, dh_ref, loss_ref):
        d = h_ref[0] - t_ref[0]
        dh_ref[0] = d * (1.0 / D)

        @pl.when(pl.program_id(0) == 0)
        def _():
            loss_ref[...] = jnp.zeros_like(loss_ref)

        loss_ref[...] += jnp.sum(d * d) * (0.5 / D)

    spec = pl.BlockSpec((1, tm, D), lambda i: (0, i, 0))
    dh, loss = pl.pallas_call(
        body, name="loss_head",
        out_shape=[jax.ShapeDtypeStruct((1, L, D), F32), jax.ShapeDtypeStruct((8, 128), F32)],
        grid=(L // tm,), in_specs=[spec, spec],
        out_specs=[spec, pl.BlockSpec((8, 128), lambda i: (0, 0))],
        compiler_params=_cparams(("arbitrary",)),
    )(h, target)
    return dh, loss[0, 0]


def _adamw(parts, w, m, v, name):
    nl, s, r, c = parts.shape
    tr = r
    for cand in (512, 256, 128, 64, 32, 16):
        if r % cand == 0 and cand * c <= 131072:
            tr = cand
            break
    c1 = 1.0 / (1.0 - ADAM_B1 ** ADAM_STEP)
    c2 = 1.0 / (1.0 - ADAM_B2 ** ADAM_STEP)

    def body(p_ref, w_ref, m_ref, v_ref, g_ref, d_ref, nm_ref, nv_ref):
        g = p_ref[0, 0].astype(F32)
        for q in range(1, s):
            g = g + p_ref[0, q].astype(F32)
        mn = ADAM_B1 * m_ref[0] + (1.0 - ADAM_B1) * g
        vn = ADAM_B2 * v_ref[0] + (1.0 - ADAM_B2) * (g * g)
        g_ref[0] = g
        nm_ref[0] = mn
        nv_ref[0] = vn
        d_ref[0] = -ADAM_LR * ((mn * c1) / (jnp.sqrt(vn * c2) + ADAM_EPS) + ADAM_WD * w_ref[0])

    spec = pl.BlockSpec((1, tr, c), lambda li, i: (li, i, 0))
    shp = jax.ShapeDtypeStruct((nl, r, c), F32)
    return pl.pallas_call(
        body, name=name, out_shape=[shp] * 4, grid=(nl, r // tr),
        in_specs=[pl.BlockSpec((1, s, tr, c), lambda li, i: (li, 0, i, 0)), spec, spec, spec],
        out_specs=[spec] * 4,
        compiler_params=_cparams(("parallel", "parallel")),
    )(parts, w, m, v)


def _pack_rows(vecs, lanes=128, mult=8):
    flat = jnp.concatenate([v.reshape(-1) for v in vecs])
    n = flat.shape[0]
    rows = -(-n // (mult * lanes)) * mult
    return jnp.pad(flat, (0, rows * lanes - n)).reshape(rows, lanes)


def _unpack_rows(packed, shapes):
    flat = packed.reshape(-1)
    out, pos = [], 0
    for s in shapes:
        n = 1
        for d_ in s:
            n *= d_
        out.append(flat[pos:pos + n].reshape(s))
        pos += n
    return out


def _unshard_last(g):
    lead = g.shape[1:-1]
    return jnp.moveaxis(g, 0, -2).reshape(*lead, NDEV * g.shape[-1])


def _shard_last(full):
    lead, w = full.shape[:-1], full.shape[-1] // NDEV
    return jnp.moveaxis(full.reshape(*lead, NDEV, w), -2, 0)


def _rope_tables():
    rows = L // GRID_W
    row = jnp.repeat(jnp.arange(rows), GRID_W).astype(F32)
    col = jnp.tile(jnp.arange(GRID_W), rows).astype(F32)
    axis_dim = HEAD_DIM // 2
    inv_freq = ROPE_BASE ** (-jnp.arange(0, axis_dim, 2, dtype=F32) / axis_dim)
    ang_r, ang_c = row[:, None] * inv_freq[None, :], col[:, None] * inv_freq[None, :]
    ang = jnp.concatenate([ang_r, ang_r, ang_c, ang_c], axis=-1)
    ang = jnp.concatenate([ang, ang], axis=-1)[None]
    return jnp.cos(ang), jnp.sin(ang)


def _heads(x, nh):
    t = x.shape[1]
    return x.reshape(t, nh, HEAD_DIM).transpose(1, 0, 2)


def _unheads(x):
    nh, t, _ = x.shape
    return x.transpose(1, 0, 2).reshape(1, t, nh * HEAD_DIM)


FFN_HALO = 16
FFN_PAIRS = 4


def _ffn_tile(t):
    return 512 if t == L else 256


def _halo_specs(t, tm, block, index):
    per, last = tm // FFN_HALO, t // FFN_HALO - 1
    return [pl.BlockSpec(block(tm), lambda d, i: index(d, i)),
            pl.BlockSpec(block(FFN_HALO), lambda d, i: index(d, jnp.maximum(i * per - 1, 0))),
            pl.BlockSpec(block(FFN_HALO), lambda d, i: index(d, jnp.minimum((i + 1) * per, last)))]


def _seg_edges(i, tm, t):
    return (i * tm == 0) | (i * tm == L), ((i + 1) * tm == L) | ((i + 1) * tm == t)


def _conv3(buf, s, w, r0, n):
    return (w[0:1] * buf[s, r0 - 1:r0 - 1 + n, :] + w[1:2] * buf[s, r0:r0 + n, :]
            + w[2:3] * buf[s, r0 + 1:r0 + 1 + n, :])


def _ffn_core_fwd(a2, up, cw, cb, down, name):
    t = a2.shape[1]
    tm = _ffn_tile(t)
    h0 = FFN_HALO

    def body(a_ref, ap_ref, an_ref, up_ref, cw_ref, cb_ref, dn_ref, z_ref, f_ref, abuf, zbuf):
        d, i = pl.program_id(0), pl.program_id(1)
        seg_first, seg_last = _seg_edges(i, tm, t)
        abuf[0:h0, :] = ap_ref[0]
        abuf[h0:h0 + tm, :] = a_ref[0]
        abuf[h0 + tm:, :] = an_ref[0]
        for s in range(2):
            zbuf[s] = jnp.dot(abuf[...], up_ref[s, 0], preferred_element_type=F32)

        @pl.when(seg_first)
        def _():
            zbuf[:, 0:h0, :] = jnp.zeros((2, h0, FFN_BLK), F32)

        @pl.when(seg_last)
        def _():
            zbuf[:, h0 + tm:, :] = jnp.zeros((2, h0, FFN_BLK), F32)

        for s in range(2):
            z_ref[s, 0] = zbuf[s, h0:h0 + tm, :].astype(z_ref.dtype)
        zg = _conv3(zbuf, 0, cw_ref[0, 0], h0, tm) + cb_ref[0, 0]
        zv = _conv3(zbuf, 1, cw_ref[1, 0], h0, tm) + cb_ref[1, 0]
        u = (zg * _sigmoid(zg) * zv).astype(BF16)
        prod = jnp.dot(u, dn_ref[0], preferred_element_type=F32)
        rows = pl.ds(pl.multiple_of(i * tm, tm), tm)

        @pl.when(d == 0)
        def _():
            f_ref[0, rows, :] = prod

        @pl.when(d > 0)
        def _():
            f_ref[0, rows, :] += prod

    pair = lambda r, c: pl.BlockSpec((2, 1, r, c), lambda d, i: (0, d, 0, 0))
    return pl.pallas_call(
        body, name=name,
        out_shape=[jax.ShapeDtypeStruct((2, FFN_PAIRS, t, FFN_BLK), BF16), jax.ShapeDtypeStruct((1, t, D), F32)],
        grid=(FFN_PAIRS, t // tm),
        in_specs=_halo_specs(t, tm, lambda r: (1, r, D), lambda d, i: (0, i, 0))
        + [pair(D, FFN_BLK), pair(FFN_K, FFN_BLK), pair(1, FFN_BLK),
           pl.BlockSpec((1, FFN_BLK, D), lambda d, i: (d, 0, 0))],
        out_specs=[pl.BlockSpec((2, 1, tm, FFN_BLK), lambda d, i: (0, d, i, 0)),
                   pl.BlockSpec((1, t, D), lambda d, i: (0, 0, 0))],
        scratch_shapes=[pltpu.VMEM((tm + 2 * h0, D), BF16), pltpu.VMEM((2, tm + 2 * h0, FFN_BLK), F32)],
        compiler_params=_cparams(("arbitrary", "arbitrary")),
    )(a2, a2, a2, up, cw, cb, down)


def _ffn_core_bwd(df, z, cw, cb, down, name):
    t = df.shape[1]
    tm = _ffn_tile(t)
    h0 = FFN_HALO
    ni = t // tm
    w0, wn = h0 // 2, tm + h0
    tn = (((0,), (0,)), ((), ()))
    nt = (((1,), (1,)), ((), ()))

    def body(df_ref, dfp_ref, dfn_ref, z_ref, zp_ref, zn_ref, cw_ref, cb_ref, dn_ref,
             dz_ref, dcw_ref, dcb_ref, ddn_ref, dfbuf, zbuf, dzbuf, acc):
        d, i = pl.program_id(0), pl.program_id(1)
        seg_first, seg_last = _seg_edges(i, tm, t)
        dfbuf[0:h0, :] = dfp_ref[0]
        dfbuf[h0:h0 + tm, :] = df_ref[0]
        dfbuf[h0 + tm:, :] = dfn_ref[0]
        for s in range(2):
            zbuf[s, 0:h0, :] = zp_ref[s, 0].astype(F32)
            zbuf[s, h0:h0 + tm, :] = z_ref[s, 0].astype(F32)
            zbuf[s, h0 + tm:, :] = zn_ref[s, 0].astype(F32)

        @pl.when(seg_first)
        def _():
            zbuf[:, 0:h0, :] = jnp.zeros((2, h0, FFN_BLK), F32)

        @pl.when(seg_last)
        def _():
            zbuf[:, h0 + tm:, :] = jnp.zeros((2, h0, FFN_BLK), F32)

        du = lax.dot_general(dfbuf[...], dn_ref[0], nt, preferred_element_type=F32)[w0:w0 + wn]
        zg = _conv3(zbuf, 0, cw_ref[0, 0], w0, wn) + cb_ref[0, 0]
        zv = _conv3(zbuf, 1, cw_ref[1, 0], w0, wn) + cb_ref[1, 0]
        sg = _sigmoid(zg)
        silu = zg * sg
        dzbuf[0, w0:w0 + wn, :] = du * zv * (sg * (1.0 + zg * (1.0 - sg)))
        dzbuf[1, w0:w0 + wn, :] = du * silu

        @pl.when(seg_first)
        def _():
            dzbuf[:, w0:h0, :] = jnp.zeros((2, h0 - w0, FFN_BLK), F32)

        @pl.when(seg_last)
        def _():
            dzbuf[:, h0 + tm:w0 + wn, :] = jnp.zeros((2, w0, FFN_BLK), F32)

        @pl.when(i == 0)
        def _():
            dcw_ref[...] = jnp.zeros_like(dcw_ref)
            dcb_ref[...] = jnp.zeros_like(dcb_ref)

        for s in range(2):
            w = cw_ref[s, 0]
            dz = (w[2:3] * dzbuf[s, h0 - 1:h0 - 1 + tm, :] + w[1:2] * dzbuf[s, h0:h0 + tm, :]
                  + w[0:1] * dzbuf[s, h0 + 1:h0 + 1 + tm, :])
            dz_ref[s, 0] = dz.astype(dz_ref.dtype)
            dzc = dzbuf[s, h0:h0 + tm, :]
            dcb_ref[s, 0] += jnp.sum(dzc, axis=0, keepdims=True)
            for k in range(FFN_K):
                dcw_ref[s, 0, k:k + 1, :] += jnp.sum(dzc * zbuf[s, h0 - 1 + k:h0 - 1 + k + tm, :], axis=0, keepdims=True)
        u = (silu * zv)[h0 - w0:h0 - w0 + tm].astype(BF16)
        prod = lax.dot_general(u, dfbuf[h0:h0 + tm, :], tn, preferred_element_type=F32)

        @pl.when(i == 0)
        def _():
            acc[...] = prod

        @pl.when(i > 0)
        def _():
            acc[...] += prod

        @pl.when(i == ni - 1)
        def _():
            ddn_ref[0] = acc[...].astype(ddn_ref.dtype)

    pair = lambda r, c: pl.BlockSpec((2, 1, r, c), lambda d, i: (0, d, 0, 0))
    return pl.pallas_call(
        body, name=name,
        out_shape=[jax.ShapeDtypeStruct((2, FFN_PAIRS, t, FFN_BLK), BF16),
                   jax.ShapeDtypeStruct((2, FFN_PAIRS, FFN_K, FFN_BLK), F32),
                   jax.ShapeDtypeStruct((2, FFN_PAIRS, 1, FFN_BLK), F32),
                   jax.ShapeDtypeStruct((FFN_PAIRS, FFN_BLK, D), BF16)],
        grid=(FFN_PAIRS, ni),
        in_specs=_halo_specs(t, tm, lambda r: (1, r, D), lambda d, i: (0, i, 0))
        + _halo_specs(t, tm, lambda r: (2, 1, r, FFN_BLK), lambda d, i: (0, d, i, 0))
        + [pair(FFN_K, FFN_BLK), pair(1, FFN_BLK), pl.BlockSpec((1, FFN_BLK, D), lambda d, i: (d, 0, 0))],
        out_specs=[pl.BlockSpec((2, 1, tm, FFN_BLK), lambda d, i: (0, d, i, 0)), pair(FFN_K, FFN_BLK),
                   pair(1, FFN_BLK), pl.BlockSpec((1, FFN_BLK, D), lambda d, i: (d, 0, 0))],
        scratch_shapes=[pltpu.VMEM((tm + 2 * h0, D), BF16), pltpu.VMEM((2, tm + 2 * h0, FFN_BLK), F32),
                        pltpu.VMEM((2, tm + 2 * h0, FFN_BLK), F32), pltpu.VMEM((FFN_BLK, D), F32)],
        compiler_params=_cparams(("arbitrary", "arbitrary")),
    )(df, df, df, z, z, z, cw, cb, down)


def _ffn_fwd(i, h, mod, ng, wts):
    a2 = _rw_fwd(f_modnorm, [(h, 0)], [(ng[2], "one"), (mod["sc2"], "seg"), (mod["sh2"], "seg")],
                 [(D, BF16)], name=f"ffn{i}_norm")[0]
    z, f = _ffn_core_fwd(a2, wts["up"], wts["cw"], wts["cb"], wts["down"], f"ffn{i}_core")
    h2 = _rw_fwd(f_resgate, [(h, 0), (f, 0)], [(mod["g2"], "seg"), (ng[3], "one")], [(D, F32)],
                 name=f"ffn{i}_res")[0]
    return h2, (h, a2, z, f)


def _ffn_bwd(i, dh, res, mod, ng, wts):
    h, a2, z, f = res
    t = h.shape[1]
    df, dg2, dng3 = _rw_bwd(f_gate_rms, [(f, 0)], [(mod["g2"], "seg"), (ng[3], "one")], [dh],
                            name=f"ffn{i}_res_b", row_grad=[(0, BF16)], param_grad=[0, 1])
    dz, dcw, dcb, d_down = _ffn_core_bwd(df, z, wts["cw"], wts["cb"], wts["down"], f"ffn{i}_core_b")
    dz = dz.reshape(NDEV, t, FFN_BLK)
    dcw, dcb = dcw.reshape(NDEV, FFN_K, FFN_BLK), dcb.reshape(NDEV, 1, FFN_BLK)
    up8 = wts["up"].reshape(NDEV, D, FFN_BLK)
    d_up = _mm(a2, dz, "tn", out_dtype=BF16, name=f"ffn{i}_up_bw")
    da2 = _mm(dz, up8, "nt", reduce_blocks=True, name=f"ffn{i}_up_bx")
    dh_in, dng2, dsc2, dsh2 = _rw_bwd(
        f_modnorm, [(h, 0)], [(ng[2], "one"), (mod["sc2"], "seg"), (mod["sh2"], "seg")], [da2],
        name=f"ffn{i}_norm_b", row_grad=[(0, F32)], param_grad=[0, 1, 2], add=dh)
    grads = dict(up=d_up, down=d_down, cw=dcw, cb=dcb, ng2=dng2, ng3=dng3, sc2=dsc2, sh2=dsh2, g2=dg2)
    return dh_in, grads


def _mixer_norm_fwd(i, h, mod, ng):
    return _rw_fwd(f_modnorm, [(h, 0)], [(ng[0], "one"), (mod["sc1"], "seg"), (mod["sh1"], "seg")],
                   [(D, BF16)], name=f"mix{i}_norm")[0]


def _mixer_norm_bwd(i, h, mod, ng, da, dh):
    return _rw_bwd(f_modnorm, [(h, 0)], [(ng[0], "one"), (mod["sc1"], "seg"), (mod["sh1"], "seg")], [da],
                   name=f"mix{i}_norm_b", row_grad=[(0, F32)], param_grad=[0, 1, 2], add=dh)


def _conformer_fwd(i, h, mod, ng, wts):
    a = _mixer_norm_fwd(i, h, mod, ng)
    p = _mm(a, wts["w_in"], "nn", name=f"cm{i}_in")
    z = _rw_fwd(f_glu, [(p, 0, 2)], [(wts["b_in"], "one", 2)], [(D, F32)], name=f"cm{i}_glu")[0]
    zc = _dwconv(z, wts["dw_w"], wts["dw_b"], name=f"cm{i}_conv")
    r = _rw_fwd(f_lnsilu, [(zc, 0)], [(wts["ln_g"], "one"), (wts["ln_b"], "one")], [(D, BF16)],
                name=f"cm{i}_ln")[0]
    y = _mm(r, wts["w_out"], "nn", name=f"cm{i}_out")
    h2 = _rw_fwd(f_resgate_bias, [(h, 0), (y, 0)], [(mod["g1"], "seg"), (ng[1], "one"), (wts["b_out"], "one")],
                 [(D, F32)], name=f"cm{i}_res")[0]
    return h2, (h, a, p, z, zc, r, y)


def _conformer_bwd(i, dh, res, mod, ng, wts):
    h, a, p, z, zc, r, y = res
    dy, dg1, dng1, db_out = _rw_bwd(
        f_gate_rms_bias, [(y, 0)], [(mod["g1"], "seg"), (ng[1], "one"), (wts["b_out"], "one")], [dh],
        name=f"cm{i}_res_b", row_grad=[(0, BF16)], param_grad=[0, 1, 2])
    dr = _mm(dy, wts["w_out"], "nt", name=f"cm{i}_out_bx")
    d_w_out = _mm(r, dy, "tn", out_dtype=BF16, name=f"cm{i}_out_bw")
    dzc, dln_g, dln_b = _rw_bwd(f_lnsilu, [(zc, 0)], [(wts["ln_g"], "one"), (wts["ln_b"], "one")], [dr],
                                name=f"cm{i}_ln_b", row_grad=[(0, F32)], param_grad=[0, 1])
    ddw_w, ddw_b = _dwconv_wgrad(z, dzc, CM_K, name=f"cm{i}_conv_bw")
    dz = _dwconv(dzc, wts["dw_w"][:, ::-1, :], None, name=f"cm{i}_conv_bx")
    dp, db_in = _rw_bwd(f_glu, [(p, 0, 2)], [(wts["b_in"], "one", 2)], [dz], name=f"cm{i}_glu_b",
                        row_grad=[(0, BF16)], param_grad=[0])
    d_w_in = _mm(a, dp, "tn", out_dtype=BF16, name=f"cm{i}_in_bw")
    da = _mm(dp, wts["w_in"], "nt", name=f"cm{i}_in_bx")
    dh_in, dng0, dsc1, dsh1 = _mixer_norm_bwd(i, h, mod, ng, da, dh)
    grads = dict(w_in=d_w_in, w_out=d_w_out, b_in=db_in, dw_w=ddw_w, dw_b=ddw_b, ln_g=dln_g, ln_b=dln_b,
                 b_out=db_out, ng0=dng0, ng1=dng1, sc1=dsc1, sh1=dsh1, g1=dg1)
    return dh_in, grads


def _attention_fwd(i, h_all, mod, ng, wts, tables):
    a = _mixer_norm_fwd(i, h_all, mod, ng)
    qkv = _mm(a, wts["w_qkv"], "nn", name="attn_qkv")
    kv0 = N_Q * HEAD_DIM
    kv1 = kv0 + N_KV * HEAD_DIM
    q = _rope(qkv[:, :L, :kv0], tables, False, BF16, "attn_rope_q")
    k = _rope(qkv[:, :L, kv0:kv1], tables, False, BF16, "attn_rope_k")
    pad = ((0, 0), (ATTN_BLOCK, ATTN_BLOCK), (0, 0))
    q_h = _heads(q, N_Q)
    k_h = jnp.pad(_heads(k, N_KV), pad)
    v_h = jnp.pad(_heads(qkv[:, :L, kv1:].astype(BF16), N_KV), pad)
    kc_h = _heads(qkv[:, L:, kv0:kv1].astype(BF16), N_KV)
    vc_h = _heads(qkv[:, L:, kv1:].astype(BF16), N_KV)
    o_h, lse = _attn_fwd(q_h, k_h, v_h, kc_h, vc_h, wts["sink"])
    o = _unheads(o_h)
    y = _mm(o, wts["w_o"], "nn", name="attn_o")
    h_lat = h_all[:, :L]
    mod_lat = {k_: v_[:1] for k_, v_ in mod.items()}
    h2 = _rw_fwd(f_resgate, [(h_lat, 0), (y, 0)], [(mod_lat["g1"], "seg"), (ng[1], "one")], [(D, F32)],
                 name="attn_res")[0]
    return h2, (h_all, a, q_h, k_h, v_h, kc_h, vc_h, o_h, lse, o, y)


def _attention_bwd(i, dh, res, mod, ng, wts, tables):
    h_all, a, q_h, k_h, v_h, kc_h, vc_h, o_h, lse, o, y = res
    mod_lat = {k_: v_[:1] for k_, v_ in mod.items()}
    dy, dg1, dng1 = _rw_bwd(f_gate_rms, [(y, 0)], [(mod_lat["g1"], "seg"), (ng[1], "one")], [dh],
                            name="attn_res_b", row_grad=[(0, BF16)], param_grad=[0, 1])
    do = _mm(dy, wts["w_o"], "nt", name="attn_o_bx")
    d_w_o = _mm(o, dy, "tn", out_dtype=BF16, name="attn_o_bw")
    dq_h, dk_h, dv_h, dkc_h, dvc_h, dsink = _attn_bwd(q_h, k_h, v_h, kc_h, vc_h, wts["sink"], o_h, lse,
                                                        _heads(do, N_Q))
    dq = _rope(_unheads(dq_h), tables, True, BF16, "attn_rope_q_b")
    dk = _rope(_unheads(dk_h[:, ATTN_BLOCK:-ATTN_BLOCK]), tables, True, BF16, "attn_rope_k_b")
    dv = _unheads(dv_h[:, ATTN_BLOCK:-ATTN_BLOCK]).astype(BF16)
    d_lat = jnp.concatenate([dq, dk, dv], axis=2)
    d_ctx = jnp.concatenate([jnp.zeros((1, LC, N_Q * HEAD_DIM), BF16), _unheads(dkc_h).astype(BF16),
                             _unheads(dvc_h).astype(BF16)], axis=2)
    dqkv = jnp.concatenate([d_lat, d_ctx], axis=1)
    d_w_qkv = _mm(a, dqkv, "tn", out_dtype=BF16, name="attn_qkv_bw")
    da = _mm(dqkv, wts["w_qkv"], "nt", name="attn_qkv_bx")
    dh_res = jnp.concatenate([dh, jnp.zeros((1, LC, D), F32)], axis=1)
    dh_in, dng0, dsc1, dsh1 = _mixer_norm_bwd(i, h_all, mod, ng, da, dh_res)
    grads = dict(w_qkv=d_w_qkv, w_o=d_w_o, sink=dsink, ng0=dng0, ng1=dng1, sc1=dsc1, sh1=dsh1, g1=dg1)
    return dh_in, grads


def _gmlp_fwd(i, h, mod, ng, wts):
    a = _mixer_norm_fwd(i, h, mod, ng)
    p = _mm(a, wts["w_in"], "nn", name="gm_in")
    u, v = _rw_fwd(f_gmlp_pre, [(p, 0, 2)], [(wts["b_in"], "one", 2), (wts["ln_g"], "one"), (wts["ln_b"], "one")],
                   [(GM_W, F32), (GM_W, BF16)], name="gm_pre")
    us = _gm_spatial_fwd(u, v, wts["w_s"], wts["b_s"])
    y = _mm(us, wts["w_out"], "nn", name="gm_out")
    h2 = _rw_fwd(f_resgate, [(h, 0), (y, 0)], [(mod["g1"], "seg"), (ng[1], "one")], [(D, F32)],
                 name="gm_res")[0]
    return h2, (h, a, p, u, v, us, y)


def _gmlp_bwd(i, dh, res, mod, ng, wts):
    h, a, p, u, v, us, y = res
    dy, dg1, dng1 = _rw_bwd(f_gate_rms, [(y, 0)], [(mod["g1"], "seg"), (ng[1], "one")], [dh],
                            name="gm_res_b", row_grad=[(0, BF16)], param_grad=[0, 1])
    dus = _mm(dy, wts["w_out"], "nt", name="gm_out_bx")
    d_w_out = _mm(us, dy, "tn", out_dtype=BF16, name="gm_out_bw")
    du, dv, dws, dbs = _gm_spatial_bwd(u, v, wts["w_s"], wts["b_s"], dus)
    dp, db_in, dln_g, dln_b = _rw_bwd(
        f_gmlp_pre, [(p, 0, 2)], [(wts["b_in"], "one", 2), (wts["ln_g"], "one"), (wts["ln_b"], "one")], [du, dv],
        name="gm_pre_b", row_grad=[(0, BF16)], param_grad=[0, 1, 2])
    d_w_in = _mm(a, dp, "tn", out_dtype=BF16, name="gm_in_bw")
    da = _mm(dp, wts["w_in"], "nt", name="gm_in_bx")
    dh_in, dng0, dsc1, dsh1 = _mixer_norm_bwd(i, h, mod, ng, da, dh)
    grads = dict(w_in=d_w_in, w_out=d_w_out, b_in=db_in, ln_g=dln_g, ln_b=dln_b, w_s=dws, b_s=dbs,
                 ng0=dng0, ng1=dng1, sc1=dsc1, sh1=dsh1, g1=dg1)
    return dh_in, grads


MOD_NAMES = ("sh1", "sc1", "g1", "sh2", "sc2", "g2")
SMALL = (
    ("norm_g", (4, 4, 128)), ("ffn_conv_w", (4, 3, 704)), ("cm_b_in", (2, 256)), ("cm_dw_w", (2, 31, 128)),
    ("cm_dw_b", (2, 128)), ("cm_ln_g", (2, 128)), ("cm_ln_b", (2, 128)), ("cm_b_out", (2, 128)),
    ("gm_b_in", (1, 512)), ("gm_ln_g", (1, 256)), ("gm_ln_b", (1, 256)))


def _local_step(x, ctx, target, lat_mod, ctx_mod, P):
    tables = _rope_tables()
    ng = [[P["norm_g"][i, j].reshape(1, 1, D) for j in range(4)] for i in range(DEPTH)]

    def mods(i, with_ctx):
        out = {}
        for j, nme in enumerate(MOD_NAMES):
            rows = [lat_mod[i, j]] + ([ctx_mod[i, j]] if with_ctx else [])
            out[nme] = jnp.stack(rows).reshape(len(rows), 1, D)
        return out

    def ffn_w(i):
        return dict(up=P["ffn_w_up"][i].reshape(2, FFN_PAIRS, D, FFN_BLK), down=P["ffn_w_down"][i],
                    cw=P["ffn_conv_w"][i].reshape(2, FFN_PAIRS, FFN_K, FFN_BLK),
                    cb=P["ffn_conv_b"][i].reshape(2, FFN_PAIRS, 1, FFN_BLK))

    def cm_w(j):
        return dict(w_in=P["cm_w_in"][j], w_out=P["cm_w_out"][j], b_in=P["cm_b_in"][j].reshape(1, 1, 2 * D),
                    dw_w=P["cm_dw_w"][j][None], dw_b=P["cm_dw_b"][j].reshape(1, 1, D),
                    ln_g=P["cm_ln_g"][j].reshape(1, 1, D), ln_b=P["cm_ln_b"][j].reshape(1, 1, D),
                    b_out=P["cm_b_out"][j].reshape(1, 1, D))

    at_w = dict(w_qkv=P["attn_w_qkv"], w_o=P["attn_w_o"], sink=P["attn_sink"].reshape(N_Q))
    gm_w = dict(w_in=P["gm_w_in"], w_out=P["gm_w_out"], b_in=P["gm_b_in"].reshape(1, 1, 2 * GM_W),
                ln_g=P["gm_ln_g"].reshape(1, 1, GM_W), ln_b=P["gm_ln_b"].reshape(1, 1, GM_W),
                w_s=P["gm_w_s"].reshape(GM_GROUPS, GM_CHUNK, GM_CHUNK).astype(BF16),
                b_s=P["gm_b_s"].reshape(GM_GROUPS, GM_CHUNK, 1))

    m0, m1, m1l, m2, m3 = mods(0, True), mods(1, True), mods(1, False), mods(2, False), mods(3, False)
    h_all = jnp.concatenate([x, ctx], axis=1)
    h, r0m = _conformer_fwd(0, h_all, m0, ng[0], cm_w(0))
    h, r0f = _ffn_fwd(0, h, m0, ng[0], ffn_w(0))
    h, r1m = _attention_fwd(1, h, m1, ng[1], at_w, tables)
    h, r1f = _ffn_fwd(1, h, m1l, ng[1], ffn_w(1))
    h, r2m = _gmlp_fwd(2, h, m2, ng[2], gm_w)
    h, r2f = _ffn_fwd(2, h, m2, ng[2], ffn_w(2))
    h, r3m = _conformer_fwd(3, h, m3, ng[3], cm_w(1))
    h, r3f = _ffn_fwd(3, h, m3, ng[3], ffn_w(3))
    dh, loss = _loss_head(h, target)

    G = {}
    dh, G["f3"] = _ffn_bwd(3, dh, r3f, m3, ng[3], ffn_w(3))
    dh, G["m3"] = _conformer_bwd(3, dh, r3m, m3, ng[3], cm_w(1))
    dh, G["f2"] = _ffn_bwd(2, dh, r2f, m2, ng[2], ffn_w(2))
    dh, G["m2"] = _gmlp_bwd(2, dh, r2m, m2, ng[2], gm_w)
    dh, G["f1"] = _ffn_bwd(1, dh, r1f, m1l, ng[1], ffn_w(1))
    dh, G["m1"] = _attention_bwd(1, dh, r1m, m1, ng[1], at_w, tables)
    dh, G["f0"] = _ffn_bwd(0, dh, r0f, m0, ng[0], ffn_w(0))
    dh, G["m0"] = _conformer_bwd(0, dh, r0m, m0, ng[0], cm_w(0))
    grad_x = dh[:, :L]

    zero = jnp.zeros((D,), F32)
    dmod = []
    for seg in range(2):
        per_layer = []
        for i in range(DEPTH):
            vals = []
            for nme in MOD_NAMES:
                src = G[("m" if nme.endswith("1") else "f") + str(i)][nme]
                vals.append(src[seg, 0] if src.shape[0] > seg else zero)
            per_layer.append(jnp.concatenate(vals))
        dmod.append(jnp.stack(per_layer))
    dmod = jnp.stack(dmod)
    return loss, grad_x, G, dmod


def kernel(x, c, ctx, c_ctx, ada_w, ada_b, norm_g, ffn_w_up, ffn_conv_w, ffn_conv_b, ffn_w_down, cm_w_in, cm_b_in, cm_dw_w, cm_dw_b, cm_ln_g, cm_ln_b, cm_w_out, cm_b_out, attn_w_qkv, attn_sink, attn_w_o, gm_w_in, gm_b_in, gm_ln_g, gm_ln_b, gm_w_s, gm_b_s, gm_w_out, loss_target, m_c_ctx, m_ada_w, m_ada_b, m_norm_g, m_ffn_w_up, m_ffn_conv_w, m_ffn_conv_b, m_ffn_w_down, m_cm_w_in, m_cm_b_in, m_cm_dw_w, m_cm_dw_b, m_cm_ln_g, m_cm_ln_b, m_cm_w_out, m_cm_b_out, m_attn_w_qkv, m_attn_sink, m_attn_w_o, m_gm_w_in, m_gm_b_in, m_gm_ln_g, m_gm_ln_b, m_gm_w_s, m_gm_b_s, m_gm_w_out, v_c_ctx, v_ada_w, v_ada_b, v_norm_g, v_ffn_w_up, v_ffn_conv_w, v_ffn_conv_b, v_ffn_w_down, v_cm_w_in, v_cm_b_in, v_cm_dw_w, v_cm_dw_b, v_cm_ln_g, v_cm_ln_b, v_cm_w_out, v_cm_b_out, v_attn_w_qkv, v_attn_sink, v_attn_w_o, v_gm_w_in, v_gm_b_in, v_gm_ln_g, v_gm_ln_b, v_gm_w_s, v_gm_b_s, v_gm_w_out):
    W = dict(c_ctx=c_ctx, ada_w=ada_w, ada_b=ada_b, norm_g=norm_g, ffn_w_up=ffn_w_up, ffn_conv_w=ffn_conv_w, ffn_conv_b=ffn_conv_b, ffn_w_down=ffn_w_down, cm_w_in=cm_w_in, cm_b_in=cm_b_in, cm_dw_w=cm_dw_w, cm_dw_b=cm_dw_b, cm_ln_g=cm_ln_g, cm_ln_b=cm_ln_b, cm_w_out=cm_w_out, cm_b_out=cm_b_out, attn_w_qkv=attn_w_qkv, attn_sink=attn_sink, attn_w_o=attn_w_o, gm_w_in=gm_w_in, gm_b_in=gm_b_in, gm_ln_g=gm_ln_g, gm_ln_b=gm_ln_b, gm_w_s=gm_w_s, gm_b_s=gm_b_s, gm_w_out=gm_w_out)
    M = dict(c_ctx=m_c_ctx, ada_w=m_ada_w, ada_b=m_ada_b, norm_g=m_norm_g, ffn_w_up=m_ffn_w_up, ffn_conv_w=m_ffn_conv_w, ffn_conv_b=m_ffn_conv_b, ffn_w_down=m_ffn_w_down, cm_w_in=m_cm_w_in, cm_b_in=m_cm_b_in, cm_dw_w=m_cm_dw_w, cm_dw_b=m_cm_dw_b, cm_ln_g=m_cm_ln_g, cm_ln_b=m_cm_ln_b, cm_w_out=m_cm_w_out, cm_b_out=m_cm_b_out, attn_w_qkv=m_attn_w_qkv, attn_sink=m_attn_sink, attn_w_o=m_attn_w_o, gm_w_in=m_gm_w_in, gm_b_in=m_gm_b_in, gm_ln_g=m_gm_ln_g, gm_ln_b=m_gm_ln_b, gm_w_s=m_gm_w_s, gm_b_s=m_gm_b_s, gm_w_out=m_gm_w_out)
    V = dict(c_ctx=v_c_ctx, ada_w=v_ada_w, ada_b=v_ada_b, norm_g=v_norm_g, ffn_w_up=v_ffn_w_up, ffn_conv_w=v_ffn_conv_w, ffn_conv_b=v_ffn_conv_b, ffn_w_down=v_ffn_w_down, cm_w_in=v_cm_w_in, cm_b_in=v_cm_b_in, cm_dw_w=v_cm_dw_w, cm_dw_b=v_cm_dw_b, cm_ln_g=v_cm_ln_g, cm_ln_b=v_cm_ln_b, cm_w_out=v_cm_w_out, cm_b_out=v_cm_b_out, attn_w_qkv=v_attn_w_qkv, attn_sink=v_attn_sink, attn_w_o=v_attn_w_o, gm_w_in=v_gm_w_in, gm_b_in=v_gm_b_in, gm_ln_g=v_gm_ln_g, gm_ln_b=v_gm_ln_b, gm_w_s=v_gm_w_s, gm_b_s=v_gm_b_s, gm_w_out=v_gm_w_out)
    me = 4 * lax.axis_index("x") + 2 * lax.axis_index("y") + lax.axis_index("c")
    small_shapes = [s for _, s in SMALL]

    small = _pack_rows([W[n] for n, _ in SMALL] + [c])
    big_names = [("ffn_w_up", 4), ("ffn_w_down", 4), ("cm_w_in", 2), ("cm_w_out", 2), ("attn_w_qkv", 1),
                 ("attn_w_o", 1), ("gm_w_in", 1), ("gm_w_out", 1)]
    big_local = [W[n][i].astype(BF16) for n, nl in big_names for i in range(nl)]
    gathered = _all_gather([small] + big_local, "gather_params")
    small_g, big_g = gathered[0], gathered[1:]
    P, pos = {}, 0
    for n, nl in big_names:
        P[n] = big_g[pos:pos + nl]
        pos += nl
    col_to_full = lambda g: g.transpose(1, 0, 2).reshape(g.shape[1], NDEV * g.shape[2])
    P["ffn_w_down"] = [g.reshape(4, FFN_BLK, D) for g in P["ffn_w_down"]]
    P["cm_w_in"] = [col_to_full(g) for g in P["cm_w_in"]]
    P["cm_w_out"] = [g.reshape(D, D) for g in P["cm_w_out"]]
    P["attn_w_qkv"] = col_to_full(P["attn_w_qkv"][0])
    P["attn_w_o"] = P["attn_w_o"][0].reshape(D, D)
    P["gm_w_in"] = col_to_full(P["gm_w_in"][0])
    P["gm_w_out"] = P["gm_w_out"][0].reshape(GM_W, D)
    unpacked = jax.vmap(lambda r: tuple(_unpack_rows(r, small_shapes + [(D,)])))(small_g)
    for (n, _), g in zip(SMALL, unpacked[:-1]):
        if n == "ffn_conv_w":
            P[n] = [g[:, i] for i in range(DEPTH)]
        else:
            P[n] = _unshard_last(g)
    c_all = unpacked[-1]
    P["ffn_conv_b"] = [ffn_conv_b[i].reshape(NDEV, 1, FFN_BLK) for i in range(DEPTH)]
    P["attn_sink"], P["gm_w_s"], P["gm_b_s"] = attn_sink, gm_w_s, gm_b_s

    cond = jnp.concatenate([c_all, c_ctx[None], jnp.zeros((7, D), F32)])[None]
    scond = _rw_fwd(f_silu, [(cond, 0)], [], [(D, BF16)], name="ada_silu")[0]
    ada_bf = ada_w.astype(BF16)
    ncol = ada_w.shape[2]
    mod_loc = _mm(scond, ada_bf, "nn", name="ada_proj")
    mod_loc = mod_loc + lax.dynamic_slice_in_dim(ada_b, me * ncol, ncol, axis=1)[:, None, :]
    mod_g = _all_gather([mod_loc], "gather_mod")[0]
    mod_full = mod_g.transpose(1, 2, 0, 3).reshape(DEPTH, 16, 6, D)
    lat_mod = lax.dynamic_index_in_dim(mod_full, me, axis=1, keepdims=False)
    ctx_mod = mod_full[:, NDEV]

    loss_part, grad_x, G, dmod = _local_step(x, ctx, loss_target, lat_mod, ctx_mod, P)
    loss = lax.psum(loss_part, AXES)

    dmod_g = _all_gather([dmod], "gather_dmod")[0]
    dm_cols = lax.dynamic_slice_in_dim(dmod_g, me * ncol, ncol, axis=3)
    dm_ext = dm_cols.transpose(2, 1, 0, 3).reshape(DEPTH, 16, ncol)
    cond_ext = jnp.concatenate([c_all, jnp.broadcast_to(c_ctx[None], (NDEV, D))])[None]
    scond_ext = _rw_fwd(f_silu, [(cond_ext, 0)], [], [(D, BF16)], name="ada_silu_ext")[0]
    g_ada_w = _mm(scond_ext, dm_ext, "tn", name="ada_proj_bw")
    dsil = _mm(dm_ext, ada_bf, "nt", reduce_blocks=True, name="ada_proj_bx")
    dcc = _rw_bwd(f_silu_rows, [(jnp.zeros((1, NDEV, D), F32), 0)], [(c_ctx.reshape(1, 1, D), "one")],
                  [dsil[:, NDEV:]], name="ada_silu_b", param_grad=[0])[0]

    out = {}

    def put(name, res):
        out[name] = res

    col_to_parts = lambda g: g[0].reshape(g.shape[1], NDEV, g.shape[2] // NDEV).transpose(1, 0, 2)
    row_to_parts = lambda g: g.reshape(NDEV, -1, g.shape[-1])
    send = [
        [G[f"f{i}"]["up"] for i in range(DEPTH)],
        [row_to_parts(G[f"f{i}"]["down"]) for i in range(DEPTH)],
        [col_to_parts(G["m0"]["w_in"]), col_to_parts(G["m3"]["w_in"])],
        [row_to_parts(G["m0"]["w_out"]), row_to_parts(G["m3"]["w_out"])],
        [col_to_parts(G["m1"]["w_qkv"])], [row_to_parts(G["m1"]["w_o"])],
        [col_to_parts(G["m2"]["w_in"])], [row_to_parts(G["m2"]["w_out"])],
    ]
    d_norm_g = jnp.stack([jnp.stack([G[f"m{i}"]["ng0"], G[f"m{i}"]["ng1"], G[f"f{i}"]["ng2"], G[f"f{i}"]["ng3"]])
                          for i in range(DEPTH)]).reshape(DEPTH, 4, D)
    small_full = dict(
        norm_g=d_norm_g,
        cm_b_in=jnp.stack([G["m0"]["b_in"], G["m3"]["b_in"]]).reshape(2, 2 * D),
        cm_dw_w=jnp.stack([G["m0"]["dw_w"][0], G["m3"]["dw_w"][0]]),
        cm_dw_b=jnp.stack([G["m0"]["dw_b"], G["m3"]["dw_b"]]).reshape(2, D),
        cm_ln_g=jnp.stack([G["m0"]["ln_g"], G["m3"]["ln_g"]]).reshape(2, D),
        cm_ln_b=jnp.stack([G["m0"]["ln_b"], G["m3"]["ln_b"]]).reshape(2, D),
        cm_b_out=jnp.stack([G["m0"]["b_out"], G["m3"]["b_out"]]).reshape(2, D),
        gm_b_in=G["m2"]["b_in"].reshape(1, 2 * GM_W),
        gm_ln_g=G["m2"]["ln_g"].reshape(1, GM_W), gm_ln_b=G["m2"]["ln_b"].reshape(1, GM_W))
    by_dest = []
    for n, _ in SMALL:
        if n == "ffn_conv_w":
            by_dest.append(jnp.stack([G[f"f{i}"]["cw"] for i in range(DEPTH)], axis=1))
        else:
            by_dest.append(_shard_last(small_full[n]))
    small_send = jax.vmap(lambda *vs: _pack_rows(list(vs)))(*by_dest)
    recv = _all_to_all(send + [[small_send]], "exchange_grads")

    def shard3(a):
        return a.reshape(a.shape[0], -1, a.shape[-1])

    for (n, nl), parts in zip(big_names, recv[:-1]):
        shp = W[n].shape
        res = _adamw(parts, shard3(W[n]), shard3(M[n]), shard3(V[n]), "adamw_" + n)
        put(n, [r.reshape(shp) for r in res])

    small_local = lambda d_: _pack_rows([d_[n] for n, _ in SMALL])[None]
    res = _adamw(recv[-1], small_local(W), small_local(M), small_local(V), "adamw_small")
    unp = [_unpack_rows(r[0], small_shapes) for r in res]
    for q, (n, _) in enumerate(SMALL):
        put(n, [unp[t][q] for t in range(4)])

    repl_names = ["c_ctx", "ffn_conv_b", "attn_sink", "gm_b_s", "gm_w_s"]
    repl_part = dict(
        c_ctx=dcc.reshape(D),
        ffn_conv_b=jnp.stack([G[f"f{i}"]["cb"].reshape(2 * 2816) for i in range(DEPTH)]),
        attn_sink=G["m1"]["sink"].reshape(1, N_Q),
        gm_b_s=G["m2"]["b_s"].reshape(1, GM_GROUPS, GM_CHUNK),
        gm_w_s=G["m2"]["w_s"].reshape(1, GM_GROUPS, GM_CHUNK, GM_CHUNK))
    repl_shapes = [W[n].shape for n in repl_names]
    repl_g = _all_gather([_pack_rows([repl_part[n] for n in repl_names], mult=256)], "gather_repl")[0]
    repl_local = lambda d_: _pack_rows([d_[n] for n in repl_names], mult=256)[None]
    res = _adamw(repl_g[None], repl_local(W), repl_local(M), repl_local(V), "adamw_repl")
    unp = [_unpack_rows(r[0], repl_shapes) for r in res]
    for q, n in enumerate(repl_names):
        put(n, [unp[t][q] for t in range(4)])

    put("ada_w", _adamw(g_ada_w[:, None], ada_w, m_ada_w, v_ada_w, "adamw_ada_w"))
    ada_b_parts = dmod_g.reshape(1, 2 * NDEV, DEPTH, 6 * D)
    res = _adamw(ada_b_parts, ada_b[None], m_ada_b[None], v_ada_b[None], "adamw_ada_b")
    put("ada_b", [r[0] for r in res])

    names = ["c_ctx", "ada_w", "ada_b", "norm_g", "ffn_w_up", "ffn_conv_w", "ffn_conv_b", "ffn_w_down", "cm_w_in",
             "cm_b_in", "cm_dw_w", "cm_dw_b", "cm_ln_g", "cm_ln_b", "cm_w_out", "cm_b_out", "attn_w_qkv",
             "attn_sink", "attn_w_o", "gm_w_in", "gm_b_in", "gm_ln_g", "gm_ln_b", "gm_w_s", "gm_b_s", "gm_w_out"]
    return (loss, grad_x, *[out[n][0] for n in names], *[out[n][1] for n in names],
            *[out[n][2] for n in names], *[out[n][3] for n in names])
```

```python
import functools

import jax
import jax.numpy as jnp
from jax import lax
from jax.experimental import pallas as pl
from jax.experimental.pallas import tpu as pltpu

F32, BF16 = jnp.float32, jnp.bfloat16
MESH = pl.DeviceIdType.MESH
AXES = ("x", "y", "c")
NDEV = 8

D = 1024
L = 2048
LC = 256
TA = L + LC
DEPTH = 4
EPS = 1e-6
HEAD_DIM = 64
N_Q, N_KV, Q_PER_KV = 16, 4, 4
ATTN_BLOCK = 128
GRID_W = 64
ROPE_BASE = 10000.0
GM_W = 2048
GM_CHUNK = 128
GM_GROUPS = 16
FFN_BLK = 704
CM_K, FFN_K = 31, 3

ADAM_LR, ADAM_B1, ADAM_B2, ADAM_EPS, ADAM_WD, ADAM_STEP = 0.001, 0.9, 0.999, 1e-08, 0.01, 10

VMEM_LIMIT_V7X = 56 * 1024 * 1024
ROW_TILE_ELEMS = 256 * 1024


def _cparams(sem=None):
    kw = dict(vmem_limit_bytes=VMEM_LIMIT_V7X)
    if sem is not None:
        kw["dimension_semantics"] = sem
    return pltpu.CompilerParams(**kw)


def _pick(n, cands):
    for c in cands:
        if n % c == 0:
            return c
    return n


def _as3(a):
    return a if a.ndim == 3 else a[None]


def _all_gather(arrs, name):
    n = len(arrs)

    def body(*refs):
        ins, outs = refs[:n], refs[n:2 * n]
        send_sems, recv_sems, local_sems = refs[2 * n:]
        x, y, c = lax.axis_index("x"), lax.axis_index("y"), lax.axis_index("c")
        me, sibling = (x, y, c), (x, y, 1 - c)
        chips = [(1 - x, y), (x, 1 - y), (1 - x, 1 - y)]

        def slot(a, p):
            return outs[a].at[4 * p[0] + 2 * p[1] + p[2]]

        def copy(a, k, block, to, src=None):
            return pltpu.make_async_remote_copy(
                src_ref=slot(a, block) if src is None else src, dst_ref=slot(a, block),
                send_sem=send_sems.at[a, k], recv_sem=recv_sems.at[a, k],
                device_id=to, device_id_type=MESH)

        mine = [pltpu.make_async_copy(ins[a], slot(a, me), local_sems.at[a]) for a in range(n)]
        for m in mine:
            m.start()
        first = []
        for a in range(n):
            first.append(copy(a, 0, me, sibling, src=ins[a]))
            first += [copy(a, 1 + j, me, (*chip, c), src=ins[a]) for j, chip in enumerate(chips)]
        for cp in first:
            cp.start()
        passed = []
        for j, chip in enumerate(chips):
            for a in range(n):
                copy(a, 1 + j, (*chip, c), me).wait_recv()
                p = copy(a, 4 + j, (*chip, c), sibling)
                p.start()
                passed.append(p)
        for a in range(n):
            copy(a, 0, sibling, me).wait_recv()
            for j, chip in enumerate(chips):
                copy(a, 4 + j, (*chip, 1 - c), me).wait_recv()
        for cp in first + passed:
            cp.wait_send()
        for m in mine:
            m.wait()

    any_spec = pl.BlockSpec(memory_space=pl.ANY)
    outs = pl.pallas_call(
        body, name=name,
        out_shape=[jax.ShapeDtypeStruct((NDEV,) + a.shape, a.dtype) for a in arrs],
        in_specs=[any_spec] * n, out_specs=[any_spec] * n,
        scratch_shapes=[pltpu.SemaphoreType.DMA((n, 7)), pltpu.SemaphoreType.DMA((n, 7)),
                        pltpu.SemaphoreType.DMA((n,))],
    )(*arrs)
    return list(outs)


def _all_to_all(groups, name):
    flat = [(gi, li, a) for gi, g in enumerate(groups) for li, a in enumerate(g)]
    n, ng = len(flat), len(groups)

    def body(*refs):
        ins, outs = refs[:n], refs[n:n + ng]
        send_sems, recv_sems, local_sems = refs[n + ng:]
        x, y, c = lax.axis_index("x"), lax.axis_index("y"), lax.axis_index("c")
        me = 4 * x + 2 * y + c
        copies = []
        for a, (gi, li, _) in enumerate(flat):
            loc = pltpu.make_async_copy(ins[a].at[me], outs[gi].at[li, me], local_sems.at[a])
            loc.start()
            copies.append(loc)
            for k in range(1, NDEV):
                px = 1 - x if (k >> 2) & 1 else x
                py = 1 - y if (k >> 1) & 1 else y
                pc = 1 - c if k & 1 else c
                cp = pltpu.make_async_remote_copy(
                    src_ref=ins[a].at[4 * px + 2 * py + pc], dst_ref=outs[gi].at[li, me],
                    send_sem=send_sems.at[a, k - 1], recv_sem=recv_sems.at[a, k - 1],
                    device_id=(px, py, pc), device_id_type=MESH)
                cp.start()
                copies.append(cp)
        for cp in copies:
            cp.wait()

    any_spec = pl.BlockSpec(memory_space=pl.ANY)
    outs = pl.pallas_call(
        body, name=name,
        out_shape=[jax.ShapeDtypeStruct((len(g),) + g[0].shape, g[0].dtype) for g in groups],
        in_specs=[any_spec] * n, out_specs=[any_spec] * ng,
        scratch_shapes=[pltpu.SemaphoreType.DMA((n, 7)), pltpu.SemaphoreType.DMA((n, 7)),
                        pltpu.SemaphoreType.DMA((n,))],
    )(*[a for _, _, a in flat])
    return list(outs)


HBM_SPEC = pl.BlockSpec(memory_space=pltpu.HBM)
SEM_SPEC = pl.BlockSpec(memory_space=pltpu.SEMAPHORE)
ANY_SPEC = pl.BlockSpec(memory_space=pl.ANY)
SPLIT_EFFECT = pltpu.SideEffectType.DATAFLOW_SIDE_EFFECTING


def _remote_copies(kind, ins, lands, send_sems, recv_sems):
    x, y, c = lax.axis_index("x"), lax.axis_index("y"), lax.axis_index("c")
    me = 4 * x + 2 * y + c
    out = []
    for a in range(len(ins)):
        for k in range(1, NDEV):
            px = 1 - x if (k >> 2) & 1 else x
            py = 1 - y if (k >> 1) & 1 else y
            pc = 1 - c if k & 1 else c
            src = ins[a] if kind == "gather" else ins[a].at[4 * px + 2 * py + pc]
            out.append(pltpu.make_async_remote_copy(
                src_ref=src, dst_ref=lands[a].at[me], send_sem=send_sems.at[a * (NDEV - 1) + k - 1],
                recv_sem=recv_sems.at[a * (NDEV - 1) + k - 1], device_id=(px, py, pc), device_id_type=MESH))
    return out


def _own_slots(kind, arrs, name):
    n = len(arrs)

    def body(*refs):
        ins, lands, sems = refs[:n], refs[n:2 * n], refs[2 * n]
        me = 4 * lax.axis_index("x") + 2 * lax.axis_index("y") + lax.axis_index("c")
        cps = [pltpu.make_async_copy(ins[a] if kind == "gather" else ins[a].at[me], lands[a].at[me], sems.at[a])
               for a in range(n)]
        for cp in cps:
            cp.start()
        for cp in cps:
            cp.wait()

    shapes = [((NDEV,) + a.shape if kind == "gather" else a.shape) for a in arrs]
    return list(pl.pallas_call(
        body, name=name, out_shape=[jax.ShapeDtypeStruct(s, a.dtype) for s, a in zip(shapes, arrs)],
        in_specs=[ANY_SPEC] * n, out_specs=[ANY_SPEC] * n,
        scratch_shapes=[pltpu.SemaphoreType.DMA((n,))],
    )(*arrs))


def _xfer_start(kind, arrs, after, name):
    n = len(arrs)
    lands = _own_slots(kind, arrs, name + "_own")

    def body(*refs):
        ins, lnd = refs[:n], refs[n:2 * n]
        send_sems, recv_sems = refs[2 * n + 1], refs[2 * n + 2]
        for cp in _remote_copies(kind, ins, lnd, send_sems, recv_sems):
            cp.start()
        refs[-1][...] = jnp.zeros_like(refs[-1])

    outs = pl.pallas_call(
        body, name=name,
        out_shape=(pltpu.SemaphoreType.DMA((n * (NDEV - 1),)), pltpu.SemaphoreType.DMA((n * (NDEV - 1),)),
                   *[pltpu.HBM(a.shape, a.dtype) for a in arrs + lands], jax.ShapeDtypeStruct((8, 128), F32)),
        in_specs=[HBM_SPEC] * (2 * n) + [ANY_SPEC],
        out_specs=(SEM_SPEC, SEM_SPEC, *[HBM_SPEC] * (2 * n), pl.BlockSpec(memory_space=pltpu.VMEM)),
        input_output_aliases={a: 2 + a for a in range(2 * n)},
        compiler_params=pltpu.CompilerParams(has_side_effects=SPLIT_EFFECT),
    )(*[pltpu.with_memory_space_constraint(a, pltpu.HBM) for a in arrs + lands], after)
    return dict(kind=kind, n=n, send=outs[0], recv=outs[1], bufs=list(outs[2:2 + 2 * n]), token=outs[-1], name=name)


def _xfer_wait(st, after):
    kind, n = st["kind"], st["n"]

    def body(*refs):
        ins, lnd = refs[:n], refs[n:2 * n]
        send_sems, recv_sems = refs[2 * n], refs[2 * n + 1]
        for cp in _remote_copies(kind, ins, lnd, send_sems, recv_sems):
            cp.wait_send()
            cp.wait_recv()

    outs = pl.pallas_call(
        body, name=st["name"] + "_wait",
        out_shape=tuple(pltpu.HBM(b.shape, b.dtype) for b in st["bufs"]),
        in_specs=[HBM_SPEC] * (2 * n) + [SEM_SPEC, SEM_SPEC, ANY_SPEC],
        out_specs=tuple([HBM_SPEC] * (2 * n)),
        input_output_aliases={a: a for a in range(2 * n)},
        compiler_params=pltpu.CompilerParams(has_side_effects=SPLIT_EFFECT),
    )(*st["bufs"], st["send"], st["recv"], after)
    return list(outs[n:])


def _mm(a, b, kind, *, name, out_dtype=F32, reduce_blocks=False):
    a, b = _as3(a), _as3(b)
    nba, nbb = a.shape[0], b.shape[0]
    nb = max(nba, nbb)
    assert nba in (1, nb) and nbb in (1, nb)
    if kind == "tn":
        t, m = a.shape[1:]
        n = b.shape[2]
        assert b.shape[1] == t and not reduce_blocks
        tk = _pick(t, (512, 768, 256))
        tm = m if m <= 1024 else _pick(m, (1024,))
        nred = t // tk
    else:
        m, k = a.shape[1:]
        n = b.shape[2] if kind == "nn" else b.shape[1]
        assert (b.shape[1] if kind == "nn" else b.shape[2]) == k
        tm = _pick(m, (512, 768, 256))
        nred = nb if reduce_blocks else 1
    tn = n if n <= 1024 else _pick(n, (512,))
    nbo = 1 if reduce_blocks else nb

    def blk(nbx, g, r):
        if nbx == 1:
            return 0
        return r if reduce_blocks else g

    if kind == "nn":
        a_spec = pl.BlockSpec((1, tm, k), lambda g, j, i, r: (blk(nba, g, r), i, 0))
        b_spec = pl.BlockSpec((1, k, tn), lambda g, j, i, r: (blk(nbb, g, r), 0, j))
        dims = (((1,), (0,)), ((), ()))
    elif kind == "nt":
        a_spec = pl.BlockSpec((1, tm, k), lambda g, j, i, r: (blk(nba, g, r), i, 0))
        b_spec = pl.BlockSpec((1, tn, k), lambda g, j, i, r: (blk(nbb, g, r), j, 0))
        dims = (((1,), (1,)), ((), ()))
    else:
        a_spec = pl.BlockSpec((1, tk, tm), lambda g, j, i, r: (blk(nba, g, r), r, i))
        b_spec = pl.BlockSpec((1, tk, tn), lambda g, j, i, r: (blk(nbb, g, r), r, j))
        dims = (((0,), (0,)), ((), ()))
    o_spec = pl.BlockSpec((1, tm, tn), lambda g, j, i, r: (g, i, j))

    def body(a_ref, b_ref, o_ref, *scratch):
        prod = lax.dot_general(a_ref[0].astype(BF16), b_ref[0].astype(BF16), dims,
                               preferred_element_type=F32)
        if nred == 1:
            o_ref[0] = prod.astype(o_ref.dtype)
        else:
            acc = scratch[0]
            r = pl.program_id(3)

            @pl.when(r == 0)
            def _():
                acc[...] = prod

            @pl.when(r > 0)
            def _():
                acc[...] += prod

            @pl.when(r == nred - 1)
            def _():
                o_ref[0] = acc[...].astype(o_ref.dtype)

    return pl.pallas_call(
        body, name=name,
        out_shape=jax.ShapeDtypeStruct((nbo, m, n), out_dtype),
        grid=(nbo, n // tn, m // tm, nred),
        in_specs=[a_spec, b_spec], out_specs=o_spec,
        scratch_shapes=[pltpu.VMEM((tm, tn), F32)] if nred > 1 else [],
        compiler_params=_cparams(("parallel", "parallel", "parallel", "arbitrary")),
    )(a, b)


def _row_tile(t, widths):
    tm = max(16, ROW_TILE_ELEMS // max(widths))
    tm = min(tm, 256)
    return t if t < tm else tm


def _sel_index(sel, g, i, tm):
    if sel == "one":
        return 0
    if sel == "seg":
        return (i * tm) // L
    return g + sel


def _row_spec(arr, off, tm):
    return pl.BlockSpec((1, tm, arr.shape[2]), lambda g, i: (g + off, i, 0))


def _par_spec(arr, sel, tm):
    return pl.BlockSpec((1, 1, arr.shape[2]), lambda g, i: (_sel_index(sel, g, i, tm), 0, 0))


def _norm_ops(ops):
    return [(o[0], o[1], o[2] if len(o) > 2 else 1) for o in ops]


def _split_cols(vals, nsplit):
    out = []
    for v, ns in zip(vals, nsplit):
        w = v.shape[1] // ns
        out += [v] if ns == 1 else [v[:, q * w:(q + 1) * w] for q in range(ns)]
    return out


def _join_cols(flat, nsplit):
    out, pos = [], 0
    for ns in nsplit:
        out.append(flat[pos] if ns == 1 else jnp.concatenate(flat[pos:pos + ns], axis=1))
        pos += ns
    return out


def _rw_fwd(fn, rows, params, outs, *, name, nblk=None):
    rows, params = _norm_ops(rows), _norm_ops(params)
    t = rows[0][0].shape[1]
    nblk = nblk or rows[0][0].shape[0]
    tm = _row_tile(t, [r.shape[2] for r, _, _ in rows] + [w for w, _ in outs])
    nr, npar = len(rows), len(params)
    nsplit = [ns for _, _, ns in rows + params]

    def body(*refs):
        vals = _split_cols([r[0].astype(F32) for r in refs[:nr + npar]], nsplit)
        res = fn(*vals)
        for o_ref, o in zip(refs[nr + npar:], res):
            o_ref[0] = o.astype(o_ref.dtype)

    res = pl.pallas_call(
        body, name=name,
        out_shape=[jax.ShapeDtypeStruct((nblk, t, w), dt) for w, dt in outs],
        grid=(nblk, t // tm),
        in_specs=[_row_spec(r, off, tm) for r, off, _ in rows] + [_par_spec(p, s, tm) for p, s, _ in params],
        out_specs=[pl.BlockSpec((1, tm, w), lambda g, i: (g, i, 0)) for w, _ in outs],
        compiler_params=_cparams(("parallel", "parallel")),
    )(*[r for r, _, _ in rows], *[p for p, _, _ in params])
    return list(res)


def _rw_bwd(fn, rows, params, cts, *, name, row_grad=(), param_grad=(), add=None, nblk=None):
    rows, params = _norm_ops(rows), _norm_ops(params)
    t = cts[0].shape[1]
    nblk = nblk or cts[0].shape[0]
    tm = _row_tile(t, [r.shape[2] for r, _, _ in rows] + [c.shape[2] for c in cts])
    ni = t // tm
    nr, npar, nct = len(rows), len(params), len(cts)
    nadd = 0 if add is None else 1
    n_in = nr + npar + nct + nadd
    nsplit = [ns for _, _, ns in rows + params]

    def body(*refs):
        prim = _split_cols([r[0].astype(F32) for r in refs[:nr + npar]], nsplit)
        ct = tuple(r[0].astype(F32) for r in refs[nr + npar:nr + npar + nct])
        _, vjp = jax.vjp(fn, *prim)
        grads = _join_cols(list(vjp(ct)), nsplit)
        out_refs = refs[n_in:]
        for q, (ri, _) in enumerate(row_grad):
            gr = grads[ri]
            if q == 0 and nadd:
                gr = gr + refs[n_in - 1][0].astype(F32)
            out_refs[q][0] = gr.astype(out_refs[q].dtype)
        g, i = pl.program_id(0), pl.program_id(1)
        step = g * ni + i
        pg, pi = (step - 1) // ni, (step - 1) % ni
        for q, pidx in enumerate(param_grad):
            o_ref = out_refs[len(row_grad) + q]
            sel = params[pidx][1]
            val = grads[nr + pidx]
            if sel == "one":
                first = step == 0
            else:
                first = (step == 0) | (_sel_index(sel, g, i, tm) != _sel_index(sel, pg, pi, tm))

            @pl.when(first)
            def _(o_ref=o_ref, val=val):
                o_ref[0] = val

            @pl.when(jnp.logical_not(first))
            def _(o_ref=o_ref, val=val):
                o_ref[0] += val

    in_arrays = [r for r, _, _ in rows] + [p for p, _, _ in params] + list(cts) + ([add] if nadd else [])
    in_specs = ([_row_spec(r, off, tm) for r, off, _ in rows] + [_par_spec(p, s, tm) for p, s, _ in params]
                + [_row_spec(c, 0, tm) for c in cts] + ([_row_spec(add, 0, tm)] if nadd else []))
    out_shape, out_specs = [], []
    for ri, dt in row_grad:
        w = rows[ri][0].shape[2]
        out_shape.append(jax.ShapeDtypeStruct((nblk, t, w), dt))
        out_specs.append(pl.BlockSpec((1, tm, w), lambda g, i: (g, i, 0)))
    for pidx in param_grad:
        p, sel, _ = params[pidx]
        out_shape.append(jax.ShapeDtypeStruct(p.shape, F32))
        out_specs.append(_par_spec(p, sel, tm))
    res = pl.pallas_call(
        body, name=name, out_shape=out_shape, grid=(nblk, ni),
        in_specs=in_specs, out_specs=out_specs,
        compiler_params=_cparams(("arbitrary", "arbitrary")),
    )(*in_arrays)
    return list(res)


def _sigmoid(x):
    return 1.0 / (1.0 + jnp.exp(-x))


def _rms(x, g):
    return x * lax.rsqrt(jnp.mean(x * x, axis=-1, keepdims=True) + EPS) * g


def _ln(x, g, b):
    mu = jnp.mean(x, axis=-1, keepdims=True)
    xc = x - mu
    var = jnp.mean(xc * xc, axis=-1, keepdims=True)
    return xc * lax.rsqrt(var + EPS) * g + b


def _gelu_tanh(x):
    return 0.5 * x * (1.0 + jnp.tanh(0.7978845608028654 * (x + 0.044715 * (x * x * x))))


def f_modnorm(h, g, sc, sh):
    return (_rms(h, g) * (1.0 + sc) + sh,)


def f_gate_rms(y, gate, g):
    return (gate * _rms(y, g),)


def f_gate_rms_bias(y, gate, g, b):
    return (gate * _rms(y + b, g),)


def f_resgate(h, y, gate, g):
    return (h + gate * _rms(y, g),)


def f_resgate_bias(h, y, gate, g, b):
    return (h + gate * _rms(y + b, g),)


def f_glu(pa, pg, ba, bg):
    return ((pa + ba) * _sigmoid(pg + bg),)


def f_lnsilu(z, g, b):
    t = _ln(z, g, b)
    return (t * _sigmoid(t),)


def f_gmlp_pre(pu, pv, bu, bv, g, bb):
    return _gelu_tanh(pu + bu), _ln(_gelu_tanh(pv + bv), g, bb)


def f_ffn_gate(zg, zv):
    return (zg * _sigmoid(zg) * zv,)


def f_silu(x):
    return (x * _sigmoid(x),)


def f_silu_rows(dummy, cc):
    return (cc * _sigmoid(cc) + 0.0 * dummy,)


def _rope(x_in, tables, neg_sin, out_dtype, name):
    w = x_in.shape[2]
    sign = -1.0 if neg_sin else 1.0

    def fn(x, cos, sin):
        cos = jnp.tile(cos, (1, w // 128))
        sin = jnp.tile(sin, (1, w // 128)) * sign
        lane = lax.broadcasted_iota(jnp.int32, x.shape, 1) & 31
        rot = jnp.where(lane < 16, -pltpu.roll(x, w - 16, 1), pltpu.roll(x, 16, 1))
        return (x * cos + rot * sin,)

    return _rw_fwd(fn, [(x_in, 0), (tables[0], 0), (tables[1], 0)], [], [(w, out_dtype)], name=name)[0]


CONV_TM = 256
CONV_RC = 32


def _conv_geometry(x, k):
    nb, t, w = x.shape
    halo = 16 if k > 17 else 8
    cb = _pick(w, (512,)) if w > 768 else w
    return nb, t, w, halo, cb, (k - 1) // 2


def _conv_in_specs(t, halo, cb):
    per = CONV_TM // halo
    last = t // halo - 1
    return [
        pl.BlockSpec((1, CONV_TM, cb), lambda g, jc, i: (g, i, jc)),
        pl.BlockSpec((1, halo, cb), lambda g, jc, i: (g, jnp.maximum(i * per - 1, 0), jc)),
        pl.BlockSpec((1, halo, cb), lambda g, jc, i: (g, jnp.minimum((i + 1) * per, last), jc)),
    ]


def _conv_fill(xp, x_ref, prev_ref, next_ref, halo, t):
    i = pl.program_id(2)
    seg_first = (i * CONV_TM == 0) | (i * CONV_TM == L)
    seg_last = ((i + 1) * CONV_TM == L) | ((i + 1) * CONV_TM == t)
    xp[0:halo, :] = jnp.where(seg_first, 0.0, prev_ref[0].astype(F32))
    xp[halo:halo + CONV_TM, :] = x_ref[0].astype(F32)
    xp[halo + CONV_TM:, :] = jnp.where(seg_last, 0.0, next_ref[0].astype(F32))


def _dwconv(x, w, b, *, name, out_dtype=F32):
    k = w.shape[1]
    nb, t, wd, halo, cb, half = _conv_geometry(x, k)
    base = halo - half

    def body(*refs):
        x_ref, prev_ref, next_ref, w_ref = refs[:4]
        b_ref = refs[4] if b is not None else None
        o_ref, xp = refs[-2], refs[-1]
        _conv_fill(xp, x_ref, prev_ref, next_ref, halo, t)
        for r0 in range(0, CONV_TM, CONV_RC):
            acc = jnp.zeros((CONV_RC, cb), F32)
            for kk in range(k):
                acc = acc + w_ref[0, kk:kk + 1, :] * xp[r0 + base + kk:r0 + base + kk + CONV_RC, :]
            if b_ref is not None:
                acc = acc + b_ref[0]
            o_ref[0, r0:r0 + CONV_RC, :] = acc.astype(o_ref.dtype)

    in_specs = _conv_in_specs(t, halo, cb) + [pl.BlockSpec((1, k, cb), lambda g, jc, i: (g, 0, jc))]
    args = [x, x, x, w]
    if b is not None:
        in_specs.append(pl.BlockSpec((1, 1, cb), lambda g, jc, i: (g, 0, jc)))
        args.append(b)
    return pl.pallas_call(
        body, name=name, out_shape=jax.ShapeDtypeStruct((nb, t, wd), out_dtype),
        grid=(nb, wd // cb, t // CONV_TM), in_specs=in_specs,
        out_specs=pl.BlockSpec((1, CONV_TM, cb), lambda g, jc, i: (g, i, jc)),
        scratch_shapes=[pltpu.VMEM((CONV_TM + 2 * halo, cb), F32)],
        compiler_params=_cparams(("parallel", "parallel", "parallel")),
    )(*args)


def _dwconv_wgrad(x, dy, k, *, name):
    nb, t, wd, halo, cb, half = _conv_geometry(x, k)
    base = halo - half

    def body(x_ref, prev_ref, next_ref, dy_ref, dw_ref, db_ref, xp):
        _conv_fill(xp, x_ref, prev_ref, next_ref, halo, t)
        i = pl.program_id(2)

        @pl.when(i == 0)
        def _():
            dw_ref[...] = jnp.zeros_like(dw_ref)
            db_ref[...] = jnp.zeros_like(db_ref)

        dyv = dy_ref[0].astype(F32)
        db_ref[0] += jnp.sum(dyv, axis=0, keepdims=True)
        for kk in range(k):
            dw_ref[0, kk:kk + 1, :] += jnp.sum(dyv * xp[base + kk:base + kk + CONV_TM, :], axis=0, keepdims=True)

    dw, db = pl.pallas_call(
        body, name=name,
        out_shape=[jax.ShapeDtypeStruct((nb, k, wd), F32), jax.ShapeDtypeStruct((nb, 1, wd), F32)],
        grid=(nb, wd // cb, t // CONV_TM),
        in_specs=_conv_in_specs(t, halo, cb) + [pl.BlockSpec((1, CONV_TM, cb), lambda g, jc, i: (g, i, jc))],
        out_specs=[pl.BlockSpec((1, k, cb), lambda g, jc, i: (g, 0, jc)),
                   pl.BlockSpec((1, 1, cb), lambda g, jc, i: (g, 0, jc))],
        scratch_shapes=[pltpu.VMEM((CONV_TM + 2 * halo, cb), F32)],
        compiler_params=_cparams(("parallel", "parallel", "arbitrary")),
    )(x, x, x, dy)
    return dw, db


ATTN_SCALE = HEAD_DIM ** -0.5
QROWS = Q_PER_KV * ATTN_BLOCK
NEG = -1e30


def _attn_scores(q, kw, kc, n):
    nt = (((1,), (1,)), ((), ()))
    s_w = lax.dot_general(q, kw, nt, preferred_element_type=F32) * ATTN_SCALE
    qi = lax.broadcasted_iota(jnp.int32, s_w.shape, 0) & (ATTN_BLOCK - 1)
    kj = lax.broadcasted_iota(jnp.int32, s_w.shape, 1)
    key_abs = (n - 1) * ATTN_BLOCK + kj
    ok = (jnp.abs(qi + ATTN_BLOCK - kj) <= ATTN_BLOCK) & (key_abs >= 0) & (key_abs < L)
    s_w = jnp.where(ok, s_w, NEG)
    s_c = lax.dot_general(q, kc, nt, preferred_element_type=F32) * ATTN_SCALE
    return s_w, s_c


def _sink_col(sink_ref, hk):
    return jnp.concatenate([jnp.full((ATTN_BLOCK, 1), sink_ref[hk * Q_PER_KV + g], F32) for g in range(Q_PER_KV)], axis=0)


def _attn_specs():
    qspec = pl.BlockSpec((Q_PER_KV, ATTN_BLOCK, HEAD_DIM), lambda hk, n: (hk, n, 0))
    kspec = pl.BlockSpec((1, L + 2 * ATTN_BLOCK, HEAD_DIM), lambda hk, n: (hk, 0, 0))
    cspec = pl.BlockSpec((1, LC, HEAD_DIM), lambda hk, n: (hk, 0, 0))
    lspec = pl.BlockSpec((Q_PER_KV, ATTN_BLOCK, 1), lambda hk, n: (hk, n, 0))
    sspec = pl.BlockSpec(memory_space=pltpu.SMEM)
    return qspec, kspec, cspec, lspec, sspec


def _attn_fwd(q, k, v, kc, vc, sink):
    qspec, kspec, cspec, lspec, sspec = _attn_specs()

    def body(q_ref, k_ref, v_ref, kc_ref, vc_ref, sink_ref, o_ref, lse_ref):
        hk, n = pl.program_id(0), pl.program_id(1)
        qv = q_ref[...].reshape(QROWS, HEAD_DIM)
        start = pl.multiple_of(n * ATTN_BLOCK, ATTN_BLOCK)
        kw = k_ref[0, pl.ds(start, 3 * ATTN_BLOCK), :]
        vw = v_ref[0, pl.ds(start, 3 * ATTN_BLOCK), :]
        s_w, s_c = _attn_scores(qv, kw, kc_ref[0], n)
        sk = _sink_col(sink_ref, hk)
        m = jnp.maximum(jnp.maximum(jnp.max(s_w, -1, keepdims=True), jnp.max(s_c, -1, keepdims=True)), sk)
        p_w, p_c = jnp.exp(s_w - m), jnp.exp(s_c - m)
        den = jnp.sum(p_w, -1, keepdims=True) + jnp.sum(p_c, -1, keepdims=True) + jnp.exp(sk - m)
        o = (jnp.dot(p_w.astype(BF16), vw, preferred_element_type=F32)
             + jnp.dot(p_c.astype(BF16), vc_ref[0], preferred_element_type=F32)) / den
        o_ref[...] = o.reshape(Q_PER_KV, ATTN_BLOCK, HEAD_DIM).astype(o_ref.dtype)
        lse_ref[...] = (m + jnp.log(den)).reshape(Q_PER_KV, ATTN_BLOCK, 1)

    return pl.pallas_call(
        body, name="attn_fwd",
        out_shape=[jax.ShapeDtypeStruct((N_Q, L, HEAD_DIM), BF16), jax.ShapeDtypeStruct((N_Q, L, 1), F32)],
        grid=(N_KV, L // ATTN_BLOCK),
        in_specs=[qspec, kspec, kspec, cspec, cspec, sspec], out_specs=[qspec, lspec],
        compiler_params=_cparams(("parallel", "parallel")),
    )(q, k, v, kc, vc, sink)


def _attn_bwd(q, k, v, kc, vc, sink, o, lse, do):
    qspec, kspec, cspec, lspec, sspec = _attn_specs()
    tn = (((0,), (0,)), ((), ()))
    nt = (((1,), (1,)), ((), ()))

    def body(q_ref, k_ref, v_ref, kc_ref, vc_ref, sink_ref, o_ref, lse_ref, do_ref,
             dq_ref, dk_ref, dv_ref, dkc_ref, dvc_ref, dsink_ref):
        hk, n = pl.program_id(0), pl.program_id(1)
        qv = q_ref[...].reshape(QROWS, HEAD_DIM)
        start = pl.multiple_of(n * ATTN_BLOCK, ATTN_BLOCK)
        win = pl.ds(start, 3 * ATTN_BLOCK)
        kw, vw = k_ref[0, win, :], v_ref[0, win, :]
        kcv, vcv = kc_ref[0], vc_ref[0]
        s_w, s_c = _attn_scores(qv, kw, kcv, n)
        lse_v = lse_ref[...].reshape(QROWS, 1)
        p_w, p_c = jnp.exp(s_w - lse_v), jnp.exp(s_c - lse_v)
        dov = do_ref[...].reshape(QROWS, HEAD_DIM).astype(F32)
        ov = o_ref[...].reshape(QROWS, HEAD_DIM).astype(F32)
        delta = jnp.sum(dov * ov, -1, keepdims=True)
        dob = dov.astype(BF16)
        dp_w = lax.dot_general(dob, vw, nt, preferred_element_type=F32)
        dp_c = lax.dot_general(dob, vcv, nt, preferred_element_type=F32)
        ds_w = (p_w * (dp_w - delta) * ATTN_SCALE).astype(BF16)
        ds_c = (p_c * (dp_c - delta) * ATTN_SCALE).astype(BF16)
        dq = jnp.dot(ds_w, kw, preferred_element_type=F32) + jnp.dot(ds_c, kcv, preferred_element_type=F32)
        dq_ref[...] = dq.reshape(Q_PER_KV, ATTN_BLOCK, HEAD_DIM)

        @pl.when(n == 0)
        def _():
            dk_ref[...] = jnp.zeros_like(dk_ref)
            dv_ref[...] = jnp.zeros_like(dv_ref)
            dkc_ref[...] = jnp.zeros_like(dkc_ref)
            dvc_ref[...] = jnp.zeros_like(dvc_ref)

        dk_ref[0, win, :] += lax.dot_general(ds_w, qv, tn, preferred_element_type=F32)
        dv_ref[0, win, :] += lax.dot_general(p_w.astype(BF16), dob, tn, preferred_element_type=F32)
        dkc_ref[0] += lax.dot_general(ds_c, qv, tn, preferred_element_type=F32)
        dvc_ref[0] += lax.dot_general(p_c.astype(BF16), dob, tn, preferred_element_type=F32)
        dsk = -jnp.exp(_sink_col(sink_ref, hk) - lse_v) * delta
        for g in range(Q_PER_KV):
            part = jnp.sum(dsk[g * ATTN_BLOCK:(g + 1) * ATTN_BLOCK])
            idx = hk * Q_PER_KV + g

            @pl.when(n == 0)
            def _(part=part, idx=idx):
                dsink_ref[idx] = part

            @pl.when(n > 0)
            def _(part=part, idx=idx):
                dsink_ref[idx] += part

    kshape = jax.ShapeDtypeStruct((N_KV, L + 2 * ATTN_BLOCK, HEAD_DIM), F32)
    cshape = jax.ShapeDtypeStruct((N_KV, LC, HEAD_DIM), F32)
    return pl.pallas_call(
        body, name="attn_bwd",
        out_shape=[jax.ShapeDtypeStruct((N_Q, L, HEAD_DIM), F32), kshape, kshape, cshape, cshape,
                   jax.ShapeDtypeStruct((N_Q,), F32)],
        grid=(N_KV, L // ATTN_BLOCK),
        in_specs=[qspec, kspec, kspec, cspec, cspec, sspec, qspec, lspec, qspec],
        out_specs=[qspec, kspec, kspec, cspec, cspec, sspec],
        compiler_params=_cparams(("arbitrary", "arbitrary")),
    )(q, k, v, kc, vc, sink, o, lse, do)


def _gm_specs():
    rspec = pl.BlockSpec((1, GM_CHUNK, GM_W), lambda n: (0, n, 0))
    wspec = pl.BlockSpec((GM_GROUPS, GM_CHUNK, GM_CHUNK), lambda n: (0, 0, 0))
    bspec = pl.BlockSpec((GM_GROUPS, GM_CHUNK, 1), lambda n: (0, 0, 0))
    return rspec, wspec, bspec


def _gm_spatial_fwd(u, v, ws, bs):
    rspec, wspec, bspec = _gm_specs()

    def body(u_ref, v_ref, ws_ref, bs_ref, o_ref):
        for g in range(GM_GROUPS):
            cols = slice(g * GM_CHUNK, (g + 1) * GM_CHUNK)
            s = jnp.dot(ws_ref[g], v_ref[0, :, cols], preferred_element_type=F32) + bs_ref[g]
            o_ref[0, :, cols] = (u_ref[0, :, cols] * s).astype(o_ref.dtype)

    return pl.pallas_call(
        body, name="gm_spatial_fwd", out_shape=jax.ShapeDtypeStruct((1, L, GM_W), BF16),
        grid=(L // GM_CHUNK,), in_specs=[rspec, rspec, wspec, bspec], out_specs=rspec,
        compiler_params=_cparams(("parallel",)),
    )(u, v, ws, bs)


def _gm_spatial_bwd(u, v, ws, bs, dus):
    rspec, wspec, bspec = _gm_specs()
    tn = (((0,), (0,)), ((), ()))
    nt = (((1,), (1,)), ((), ()))

    def body(u_ref, v_ref, ws_ref, bs_ref, d_ref, du_ref, dv_ref, dws_ref, dbs_ref):
        n = pl.program_id(0)

        @pl.when(n == 0)
        def _():
            dws_ref[...] = jnp.zeros_like(dws_ref)
            dbs_ref[...] = jnp.zeros_like(dbs_ref)

        for g in range(GM_GROUPS):
            cols = slice(g * GM_CHUNK, (g + 1) * GM_CHUNK)
            vb = v_ref[0, :, cols]
            s = jnp.dot(ws_ref[g], vb, preferred_element_type=F32) + bs_ref[g]
            d = d_ref[0, :, cols].astype(F32)
            du_ref[0, :, cols] = d * s
            ds = d * u_ref[0, :, cols]
            dsb = ds.astype(BF16)
            dv_ref[0, :, cols] = lax.dot_general(ws_ref[g], dsb, tn, preferred_element_type=F32)
            dws_ref[g] += lax.dot_general(dsb, vb, nt, preferred_element_type=F32)
            dbs_ref[g] += jnp.sum(ds, axis=1, keepdims=True)

    row = jax.ShapeDtypeStruct((1, L, GM_W), F32)
    return pl.pallas_call(
        body, name="gm_spatial_bwd",
        out_shape=[row, row, jax.ShapeDtypeStruct((GM_GROUPS, GM_CHUNK, GM_CHUNK), F32),
                   jax.ShapeDtypeStruct((GM_GROUPS, GM_CHUNK, 1), F32)],
        grid=(L // GM_CHUNK,), in_specs=[rspec, rspec, wspec, bspec, rspec],
        out_specs=[rspec, rspec, wspec, bspec],
        compiler_params=_cparams(("arbitrary",)),
    )(u, v, ws, bs, dus)


def _loss_head(h, target):
    tm = 256

    def body(h_ref, t_ref, dh_ref, loss_ref):
        d = h_ref[0] - t_ref[0]
        dh_ref[0] = d * (1.0 / D)

        @pl.when(pl.program_id(0) == 0)
        def _():
            loss_ref[...] = jnp.zeros_like(loss_ref)

        loss_ref[...] += jnp.sum(d * d) * (0.5 / D)

    spec = pl.BlockSpec((1, tm, D), lambda i: (0, i, 0))
    dh, loss = pl.pallas_call(
        body, name="loss_head",
        out_shape=[jax.ShapeDtypeStruct((1, L, D), F32), jax.ShapeDtypeStruct((8, 128), F32)],
        grid=(L // tm,), in_specs=[spec, spec],
        out_specs=[spec, pl.BlockSpec((8, 128), lambda i: (0, 0))],
        compiler_params=_cparams(("arbitrary",)),
    )(h, target)
    return dh, loss[0, 0]


def _adamw(parts, w, m, v, name):
    per_layer = isinstance(parts, (list, tuple))
    plist = list(parts) if per_layer else [parts]
    nl = len(plist) if per_layer else parts.shape[0]
    s, r, c = plist[0].shape[-3:]
    tr = r
    for cand in (512, 256, 128, 64, 32, 16):
        if r % cand == 0 and cand * c <= 131072:
            tr = cand
            break
    nr = r // tr
    npart = len(plist)
    c1 = 1.0 / (1.0 - ADAM_B1 ** ADAM_STEP)
    c2 = 1.0 / (1.0 - ADAM_B2 ** ADAM_STEP)

    def body(*refs):
        w_ref, m_ref, v_ref, g_ref, d_ref, nm_ref, nv_ref = refs[npart:]

        def update(read):
            g = read(0).astype(F32)
            for q in range(1, s):
                g = g + read(q).astype(F32)
            mn = ADAM_B1 * m_ref[0] + (1.0 - ADAM_B1) * g
            vn = ADAM_B2 * v_ref[0] + (1.0 - ADAM_B2) * (g * g)
            g_ref[0] = g
            nm_ref[0] = mn
            nv_ref[0] = vn
            d_ref[0] = -ADAM_LR * ((mn * c1) / (jnp.sqrt(vn * c2) + ADAM_EPS) + ADAM_WD * w_ref[0])

        if not per_layer:
            update(lambda q: refs[0][0, q])
        else:
            for l in range(nl):
                @pl.when(pl.program_id(0) == l)
                def _(l=l):
                    update(lambda q: refs[l][q])

    spec = pl.BlockSpec((1, tr, c), lambda li, i: (li, i, 0))
    shp = jax.ShapeDtypeStruct((nl, r, c), F32)
    if per_layer:
        pspecs = [pl.BlockSpec((s, tr, c), lambda li, i, l=l: (0, jnp.where(li == l, i, jnp.where(li > l, nr - 1, 0)), 0))
                  for l in range(nl)]
    else:
        pspecs = [pl.BlockSpec((1, s, tr, c), lambda li, i: (li, 0, i, 0))]
    return pl.pallas_call(
        body, name=name, out_shape=[shp] * 4, grid=(nl, nr),
        in_specs=pspecs + [spec, spec, spec], out_specs=[spec] * 4,
        compiler_params=_cparams(("arbitrary", "arbitrary")),
    )(*plist, w, m, v)


def _pack_rows(vecs, lanes=128, mult=8):
    flat = jnp.concatenate([v.reshape(-1) for v in vecs])
    n = flat.shape[0]
    rows = -(-n // (mult * lanes)) * mult
    return jnp.pad(flat, (0, rows * lanes - n)).reshape(rows, lanes)


def _unpack_rows(packed, shapes):
    flat = packed.reshape(-1)
    out, pos = [], 0
    for s in shapes:
        n = 1
        for d_ in s:
            n *= d_
        out.append(flat[pos:pos + n].reshape(s))
        pos += n
    return out


def _unshard_last(g):
    lead = g.shape[1:-1]
    return jnp.moveaxis(g, 0, -2).reshape(*lead, NDEV * g.shape[-1])


def _shard_last(full):
    lead, w = full.shape[:-1], full.shape[-1] // NDEV
    return jnp.moveaxis(full.reshape(*lead, NDEV, w), -2, 0)


def _rope_tables():
    rows = L // GRID_W
    row = jnp.repeat(jnp.arange(rows), GRID_W).astype(F32)
    col = jnp.tile(jnp.arange(GRID_W), rows).astype(F32)
    axis_dim = HEAD_DIM // 2
    inv_freq = ROPE_BASE ** (-jnp.arange(0, axis_dim, 2, dtype=F32) / axis_dim)
    ang_r, ang_c = row[:, None] * inv_freq[None, :], col[:, None] * inv_freq[None, :]
    ang = jnp.concatenate([ang_r, ang_r, ang_c, ang_c], axis=-1)
    ang = jnp.concatenate([ang, ang], axis=-1)[None]
    return jnp.cos(ang), jnp.sin(ang)


def _heads(x, nh):
    t = x.shape[1]
    return x.reshape(t, nh, HEAD_DIM).transpose(1, 0, 2)


def _unheads(x):
    nh, t, _ = x.shape
    return x.transpose(1, 0, 2).reshape(1, t, nh * HEAD_DIM)


FFN_HALO = 16
FFN_PAIRS = 4


def _ffn_tile(t):
    return 512 if t == L else 256


def _halo_specs(t, tm, block, index):
    per, last = tm // FFN_HALO, t // FFN_HALO - 1
    return [pl.BlockSpec(block(tm), lambda d, i: index(d, i)),
            pl.BlockSpec(block(FFN_HALO), lambda d, i: index(d, jnp.maximum(i * per - 1, 0))),
            pl.BlockSpec(block(FFN_HALO), lambda d, i: index(d, jnp.minimum((i + 1) * per, last)))]


def _seg_edges(i, tm, t):
    return (i * tm == 0) | (i * tm == L), ((i + 1) * tm == L) | ((i + 1) * tm == t)


def _conv3(buf, s, w, r0, n):
    return (w[0:1] * buf[s, r0 - 1:r0 - 1 + n, :] + w[1:2] * buf[s, r0:r0 + n, :]
            + w[2:3] * buf[s, r0 + 1:r0 + 1 + n, :])


def _ffn_core_fwd(a2, up, cw, cb, down, name):
    t = a2.shape[1]
    tm = _ffn_tile(t)
    h0 = FFN_HALO

    def body(a_ref, ap_ref, an_ref, up_ref, cw_ref, cb_ref, dn_ref, z_ref, f_ref, abuf, zbuf):
        d, i = pl.program_id(0), pl.program_id(1)
        seg_first, seg_last = _seg_edges(i, tm, t)
        abuf[0:h0, :] = ap_ref[0]
        abuf[h0:h0 + tm, :] = a_ref[0]
        abuf[h0 + tm:, :] = an_ref[0]
        for s in range(2):
            zbuf[s] = jnp.dot(abuf[...], up_ref[s, 0], preferred_element_type=F32)

        @pl.when(seg_first)
        def _():
            zbuf[:, 0:h0, :] = jnp.zeros((2, h0, FFN_BLK), F32)

        @pl.when(seg_last)
        def _():
            zbuf[:, h0 + tm:, :] = jnp.zeros((2, h0, FFN_BLK), F32)

        for s in range(2):
            z_ref[s, 0] = zbuf[s, h0:h0 + tm, :].astype(z_ref.dtype)
        zg = _conv3(zbuf, 0, cw_ref[0, 0], h0, tm) + cb_ref[0, 0]
        zv = _conv3(zbuf, 1, cw_ref[1, 0], h0, tm) + cb_ref[1, 0]
        u = (zg * _sigmoid(zg) * zv).astype(BF16)
        prod = jnp.dot(u, dn_ref[0], preferred_element_type=F32)
        rows = pl.ds(pl.multiple_of(i * tm, tm), tm)

        @pl.when(d == 0)
        def _():
            f_ref[0, rows, :] = prod

        @pl.when(d > 0)
        def _():
            f_ref[0, rows, :] += prod

    pair = lambda r, c: pl.BlockSpec((2, 1, r, c), lambda d, i: (0, d, 0, 0))
    return pl.pallas_call(
        body, name=name,
        out_shape=[jax.ShapeDtypeStruct((2, FFN_PAIRS, t, FFN_BLK), BF16), jax.ShapeDtypeStruct((1, t, D), F32)],
        grid=(FFN_PAIRS, t // tm),
        in_specs=_halo_specs(t, tm, lambda r: (1, r, D), lambda d, i: (0, i, 0))
        + [pair(D, FFN_BLK), pair(FFN_K, FFN_BLK), pair(1, FFN_BLK),
           pl.BlockSpec((1, FFN_BLK, D), lambda d, i: (d, 0, 0))],
        out_specs=[pl.BlockSpec((2, 1, tm, FFN_BLK), lambda d, i: (0, d, i, 0)),
                   pl.BlockSpec((1, t, D), lambda d, i: (0, 0, 0))],
        scratch_shapes=[pltpu.VMEM((tm + 2 * h0, D), BF16), pltpu.VMEM((2, tm + 2 * h0, FFN_BLK), F32)],
        compiler_params=_cparams(("arbitrary", "arbitrary")),
    )(a2, a2, a2, up, cw, cb, down)


def _ffn_core_bwd(df, z, cw, cb, down, name):
    t = df.shape[1]
    tm = _ffn_tile(t)
    h0 = FFN_HALO
    ni = t // tm
    w0, wn = h0 // 2, tm + h0
    tn = (((0,), (0,)), ((), ()))
    nt = (((1,), (1,)), ((), ()))

    def body(df_ref, dfp_ref, dfn_ref, z_ref, zp_ref, zn_ref, cw_ref, cb_ref, dn_ref,
             dz_ref, dcw_ref, dcb_ref, ddn_ref, dfbuf, zbuf, dzbuf, acc):
        d, i = pl.program_id(0), pl.program_id(1)
        seg_first, seg_last = _seg_edges(i, tm, t)
        dfbuf[0:h0, :] = dfp_ref[0]
        dfbuf[h0:h0 + tm, :] = df_ref[0]
        dfbuf[h0 + tm:, :] = dfn_ref[0]
        for s in range(2):
            zbuf[s, 0:h0, :] = zp_ref[s, 0].astype(F32)
            zbuf[s, h0:h0 + tm, :] = z_ref[s, 0].astype(F32)
            zbuf[s, h0 + tm:, :] = zn_ref[s, 0].astype(F32)

        @pl.when(seg_first)
        def _():
            zbuf[:, 0:h0, :] = jnp.zeros((2, h0, FFN_BLK), F32)

        @pl.when(seg_last)
        def _():
            zbuf[:, h0 + tm:, :] = jnp.zeros((2, h0, FFN_BLK), F32)

        du = lax.dot_general(dfbuf[...], dn_ref[0], nt, preferred_element_type=F32)[w0:w0 + wn]
        zg = _conv3(zbuf, 0, cw_ref[0, 0], w0, wn) + cb_ref[0, 0]
        zv = _conv3(zbuf, 1, cw_ref[1, 0], w0, wn) + cb_ref[1, 0]
        sg = _sigmoid(zg)
        silu = zg * sg
        dzbuf[0, w0:w0 + wn, :] = du * zv * (sg * (1.0 + zg * (1.0 - sg)))
        dzbuf[1, w0:w0 + wn, :] = du * silu

        @pl.when(seg_first)
        def _():
            dzbuf[:, w0:h0, :] = jnp.zeros((2, h0 - w0, FFN_BLK), F32)

        @pl.when(seg_last)
        def _():
            dzbuf[:, h0 + tm:w0 + wn, :] = jnp.zeros((2, w0, FFN_BLK), F32)

        @pl.when(i == 0)
        def _():
            dcw_ref[...] = jnp.zeros_like(dcw_ref)
            dcb_ref[...] = jnp.zeros_like(dcb_ref)

        for s in range(2):
            w = cw_ref[s, 0]
            dz = (w[2:3] * dzbuf[s, h0 - 1:h0 - 1 + tm, :] + w[1:2] * dzbuf[s, h0:h0 + tm, :]
                  + w[0:1] * dzbuf[s, h0 + 1:h0 + 1 + tm, :])
            dz_ref[s, 0] = dz.astype(dz_ref.dtype)
            dzc = dzbuf[s, h0:h0 + tm, :]
            dcb_ref[s, 0] += jnp.sum(dzc, axis=0, keepdims=True)
            for k in range(FFN_K):
                dcw_ref[s, 0, k:k + 1, :] += jnp.sum(dzc * zbuf[s, h0 - 1 + k:h0 - 1 + k + tm, :], axis=0, keepdims=True)
        u = (silu * zv)[h0 - w0:h0 - w0 + tm].astype(BF16)
        prod = lax.dot_general(u, dfbuf[h0:h0 + tm, :], tn, preferred_element_type=F32)

        @pl.when(i == 0)
        def _():
            acc[...] = prod

        @pl.when(i > 0)
        def _():
            acc[...] += prod

        @pl.when(i == ni - 1)
        def _():
            ddn_ref[0] = acc[...].astype(ddn_ref.dtype)

    pair = lambda r, c: pl.BlockSpec((2, 1, r, c), lambda d, i: (0, d, 0, 0))
    return pl.pallas_call(
        body, name=name,
        out_shape=[jax.ShapeDtypeStruct((2, FFN_PAIRS, t, FFN_BLK), BF16),
                   jax.ShapeDtypeStruct((2, FFN_PAIRS, FFN_K, FFN_BLK), F32),
                   jax.ShapeDtypeStruct((2, FFN_PAIRS, 1, FFN_BLK), F32),
                   jax.ShapeDtypeStruct((FFN_PAIRS, FFN_BLK, D), BF16)],
        grid=(FFN_PAIRS, ni),
        in_specs=_halo_specs(t, tm, lambda r: (1, r, D), lambda d, i: (0, i, 0))
        + _halo_specs(t, tm, lambda r: (2, 1, r, FFN_BLK), lambda d, i: (0, d, i, 0))
        + [pair(FFN_K, FFN_BLK), pair(1, FFN_BLK), pl.BlockSpec((1, FFN_BLK, D), lambda d, i: (d, 0, 0))],
        out_specs=[pl.BlockSpec((2, 1, tm, FFN_BLK), lambda d, i: (0, d, i, 0)), pair(FFN_K, FFN_BLK),
                   pair(1, FFN_BLK), pl.BlockSpec((1, FFN_BLK, D), lambda d, i: (d, 0, 0))],
        scratch_shapes=[pltpu.VMEM((tm + 2 * h0, D), BF16), pltpu.VMEM((2, tm + 2 * h0, FFN_BLK), F32),
                        pltpu.VMEM((2, tm + 2 * h0, FFN_BLK), F32), pltpu.VMEM((FFN_BLK, D), F32)],
        compiler_params=_cparams(("arbitrary", "arbitrary")),
    )(df, df, df, z, z, z, cw, cb, down)


def _ffn_fwd(i, h, mod, ng, wts):
    a2 = _rw_fwd(f_modnorm, [(h, 0)], [(ng[2], "one"), (mod["sc2"], "seg"), (mod["sh2"], "seg")],
                 [(D, BF16)], name=f"ffn{i}_norm")[0]
    z, f = _ffn_core_fwd(a2, wts["up"], wts["cw"], wts["cb"], wts["down"], f"ffn{i}_core")
    h2 = _rw_fwd(f_resgate, [(h, 0), (f, 0)], [(mod["g2"], "seg"), (ng[3], "one")], [(D, F32)],
                 name=f"ffn{i}_res")[0]
    return h2, (h, a2, z, f)


def _ffn_bwd(i, dh, res, mod, ng, wts):
    h, a2, z, f = res
    t = h.shape[1]
    df, dg2, dng3 = _rw_bwd(f_gate_rms, [(f, 0)], [(mod["g2"], "seg"), (ng[3], "one")], [dh],
                            name=f"ffn{i}_res_b", row_grad=[(0, BF16)], param_grad=[0, 1])
    dz, dcw, dcb, d_down = _ffn_core_bwd(df, z, wts["cw"], wts["cb"], wts["down"], f"ffn{i}_core_b")
    dz = dz.reshape(NDEV, t, FFN_BLK)
    dcw, dcb = dcw.reshape(NDEV, FFN_K, FFN_BLK), dcb.reshape(NDEV, 1, FFN_BLK)
    up8 = wts["up"].reshape(NDEV, D, FFN_BLK)
    d_up = _mm(a2, dz, "tn", out_dtype=BF16, name=f"ffn{i}_up_bw")
    da2 = _mm(dz, up8, "nt", reduce_blocks=True, name=f"ffn{i}_up_bx")
    dh_in, dng2, dsc2, dsh2 = _rw_bwd(
        f_modnorm, [(h, 0)], [(ng[2], "one"), (mod["sc2"], "seg"), (mod["sh2"], "seg")], [da2],
        name=f"ffn{i}_norm_b", row_grad=[(0, F32)], param_grad=[0, 1, 2], add=dh)
    grads = dict(up=d_up, down=d_down, cw=dcw, cb=dcb, ng2=dng2, ng3=dng3, sc2=dsc2, sh2=dsh2, g2=dg2)
    return dh_in, grads


def _mixer_norm_fwd(i, h, mod, ng):
    return _rw_fwd(f_modnorm, [(h, 0)], [(ng[0], "one"), (mod["sc1"], "seg"), (mod["sh1"], "seg")],
                   [(D, BF16)], name=f"mix{i}_norm")[0]


def _mixer_norm_bwd(i, h, mod, ng, da, dh):
    return _rw_bwd(f_modnorm, [(h, 0)], [(ng[0], "one"), (mod["sc1"], "seg"), (mod["sh1"], "seg")], [da],
                   name=f"mix{i}_norm_b", row_grad=[(0, F32)], param_grad=[0, 1, 2], add=dh)


def _conformer_fwd(i, h, mod, ng, wts):
    a = _mixer_norm_fwd(i, h, mod, ng)
    p = _mm(a, wts["w_in"], "nn", name=f"cm{i}_in")
    z = _rw_fwd(f_glu, [(p, 0, 2)], [(wts["b_in"], "one", 2)], [(D, F32)], name=f"cm{i}_glu")[0]
    zc = _dwconv(z, wts["dw_w"], wts["dw_b"], name=f"cm{i}_conv")
    r = _rw_fwd(f_lnsilu, [(zc, 0)], [(wts["ln_g"], "one"), (wts["ln_b"], "one")], [(D, BF16)],
                name=f"cm{i}_ln")[0]
    y = _mm(r, wts["w_out"], "nn", name=f"cm{i}_out")
    h2 = _rw_fwd(f_resgate_bias, [(h, 0), (y, 0)], [(mod["g1"], "seg"), (ng[1], "one"), (wts["b_out"], "one")],
                 [(D, F32)], name=f"cm{i}_res")[0]
    return h2, (h, a, p, z, zc, r, y)


def _conformer_bwd(i, dh, res, mod, ng, wts):
    h, a, p, z, zc, r, y = res
    dy, dg1, dng1, db_out = _rw_bwd(
        f_gate_rms_bias, [(y, 0)], [(mod["g1"], "seg"), (ng[1], "one"), (wts["b_out"], "one")], [dh],
        name=f"cm{i}_res_b", row_grad=[(0, BF16)], param_grad=[0, 1, 2])
    dr = _mm(dy, wts["w_out"], "nt", name=f"cm{i}_out_bx")
    d_w_out = _mm(r, dy, "tn", out_dtype=BF16, name=f"cm{i}_out_bw")
    dzc, dln_g, dln_b = _rw_bwd(f_lnsilu, [(zc, 0)], [(wts["ln_g"], "one"), (wts["ln_b"], "one")], [dr],
                                name=f"cm{i}_ln_b", row_grad=[(0, F32)], param_grad=[0, 1])
    ddw_w, ddw_b = _dwconv_wgrad(z, dzc, CM_K, name=f"cm{i}_conv_bw")
    dz = _dwconv(dzc, wts["dw_w"][:, ::-1, :], None, name=f"cm{i}_conv_bx")
    dp, db_in = _rw_bwd(f_glu, [(p, 0, 2)], [(wts["b_in"], "one", 2)], [dz], name=f"cm{i}_glu_b",
                        row_grad=[(0, BF16)], param_grad=[0])
    d_w_in = _mm(a, dp, "tn", out_dtype=BF16, name=f"cm{i}_in_bw")
    da = _mm(dp, wts["w_in"], "nt", name=f"cm{i}_in_bx")
    dh_in, dng0, dsc1, dsh1 = _mixer_norm_bwd(i, h, mod, ng, da, dh)
    grads = dict(w_in=d_w_in, w_out=d_w_out, b_in=db_in, dw_w=ddw_w, dw_b=ddw_b, ln_g=dln_g, ln_b=dln_b,
                 b_out=db_out, ng0=dng0, ng1=dng1, sc1=dsc1, sh1=dsh1, g1=dg1)
    return dh_in, grads


def _attention_fwd(i, h_all, mod, ng, wts, tables):
    a = _mixer_norm_fwd(i, h_all, mod, ng)
    qkv = _mm(a, wts["w_qkv"], "nn", name="attn_qkv")
    kv0 = N_Q * HEAD_DIM
    kv1 = kv0 + N_KV * HEAD_DIM
    q = _rope(qkv[:, :L, :kv0], tables, False, BF16, "attn_rope_q")
    k = _rope(qkv[:, :L, kv0:kv1], tables, False, BF16, "attn_rope_k")
    pad = ((0, 0), (ATTN_BLOCK, ATTN_BLOCK), (0, 0))
    q_h = _heads(q, N_Q)
    k_h = jnp.pad(_heads(k, N_KV), pad)
    v_h = jnp.pad(_heads(qkv[:, :L, kv1:].astype(BF16), N_KV), pad)
    kc_h = _heads(qkv[:, L:, kv0:kv1].astype(BF16), N_KV)
    vc_h = _heads(qkv[:, L:, kv1:].astype(BF16), N_KV)
    o_h, lse = _attn_fwd(q_h, k_h, v_h, kc_h, vc_h, wts["sink"])
    o = _unheads(o_h)
    y = _mm(o, wts["w_o"], "nn", name="attn_o")
    h_lat = h_all[:, :L]
    mod_lat = {k_: v_[:1] for k_, v_ in mod.items()}
    h2 = _rw_fwd(f_resgate, [(h_lat, 0), (y, 0)], [(mod_lat["g1"], "seg"), (ng[1], "one")], [(D, F32)],
                 name="attn_res")[0]
    return h2, (h_all, a, q_h, k_h, v_h, kc_h, vc_h, o_h, lse, o, y)


def _attention_bwd(i, dh, res, mod, ng, wts, tables):
    h_all, a, q_h, k_h, v_h, kc_h, vc_h, o_h, lse, o, y = res
    mod_lat = {k_: v_[:1] for k_, v_ in mod.items()}
    dy, dg1, dng1 = _rw_bwd(f_gate_rms, [(y, 0)], [(mod_lat["g1"], "seg"), (ng[1], "one")], [dh],
                            name="attn_res_b", row_grad=[(0, BF16)], param_grad=[0, 1])
    do = _mm(dy, wts["w_o"], "nt", name="attn_o_bx")
    d_w_o = _mm(o, dy, "tn", out_dtype=BF16, name="attn_o_bw")
    dq_h, dk_h, dv_h, dkc_h, dvc_h, dsink = _attn_bwd(q_h, k_h, v_h, kc_h, vc_h, wts["sink"], o_h, lse,
                                                        _heads(do, N_Q))
    dq = _rope(_unheads(dq_h), tables, True, BF16, "attn_rope_q_b")
    dk = _rope(_unheads(dk_h[:, ATTN_BLOCK:-ATTN_BLOCK]), tables, True, BF16, "attn_rope_k_b")
    dv = _unheads(dv_h[:, ATTN_BLOCK:-ATTN_BLOCK]).astype(BF16)
    d_lat = jnp.concatenate([dq, dk, dv], axis=2)
    d_ctx = jnp.concatenate([jnp.zeros((1, LC, N_Q * HEAD_DIM), BF16), _unheads(dkc_h).astype(BF16),
                             _unheads(dvc_h).astype(BF16)], axis=2)
    dqkv = jnp.concatenate([d_lat, d_ctx], axis=1)
    d_w_qkv = _mm(a, dqkv, "tn", out_dtype=BF16, name="attn_qkv_bw")
    da = _mm(dqkv, wts["w_qkv"], "nt", name="attn_qkv_bx")
    dh_res = jnp.concatenate([dh, jnp.zeros((1, LC, D), F32)], axis=1)
    dh_in, dng0, dsc1, dsh1 = _mixer_norm_bwd(i, h_all, mod, ng, da, dh_res)
    grads = dict(w_qkv=d_w_qkv, w_o=d_w_o, sink=dsink, ng0=dng0, ng1=dng1, sc1=dsc1, sh1=dsh1, g1=dg1)
    return dh_in, grads


def _gmlp_fwd(i, h, mod, ng, wts):
    a = _mixer_norm_fwd(i, h, mod, ng)
    p = _mm(a, wts["w_in"], "nn", name="gm_in")
    u, v = _rw_fwd(f_gmlp_pre, [(p, 0, 2)], [(wts["b_in"], "one", 2), (wts["ln_g"], "one"), (wts["ln_b"], "one")],
                   [(GM_W, F32), (GM_W, BF16)], name="gm_pre")
    us = _gm_spatial_fwd(u, v, wts["w_s"], wts["b_s"])
    y = _mm(us, wts["w_out"], "nn", name="gm_out")
    h2 = _rw_fwd(f_resgate, [(h, 0), (y, 0)], [(mod["g1"], "seg"), (ng[1], "one")], [(D, F32)],
                 name="gm_res")[0]
    return h2, (h, a, p, u, v, us, y)


def _gmlp_bwd(i, dh, res, mod, ng, wts):
    h, a, p, u, v, us, y = res
    dy, dg1, dng1 = _rw_bwd(f_gate_rms, [(y, 0)], [(mod["g1"], "seg"), (ng[1], "one")], [dh],
                            name="gm_res_b", row_grad=[(0, BF16)], param_grad=[0, 1])
    dus = _mm(dy, wts["w_out"], "nt", name="gm_out_bx")
    d_w_out = _mm(us, dy, "tn", out_dtype=BF16, name="gm_out_bw")
    du, dv, dws, dbs = _gm_spatial_bwd(u, v, wts["w_s"], wts["b_s"], dus)
    dp, db_in, dln_g, dln_b = _rw_bwd(
        f_gmlp_pre, [(p, 0, 2)], [(wts["b_in"], "one", 2), (wts["ln_g"], "one"), (wts["ln_b"], "one")], [du, dv],
        name="gm_pre_b", row_grad=[(0, BF16)], param_grad=[0, 1, 2])
    d_w_in = _mm(a, dp, "tn", out_dtype=BF16, name="gm_in_bw")
    da = _mm(dp, wts["w_in"], "nt", name="gm_in_bx")
    dh_in, dng0, dsc1, dsh1 = _mixer_norm_bwd(i, h, mod, ng, da, dh)
    grads = dict(w_in=d_w_in, w_out=d_w_out, b_in=db_in, ln_g=dln_g, ln_b=dln_b, w_s=dws, b_s=dbs,
                 ng0=dng0, ng1=dng1, sc1=dsc1, sh1=dsh1, g1=dg1)
    return dh_in, grads


MOD_NAMES = ("sh1", "sc1", "g1", "sh2", "sc2", "g2")
SMALL = (
    ("norm_g", (4, 4, 128)), ("ffn_conv_w", (4, 3, 704)), ("cm_b_in", (2, 256)), ("cm_dw_w", (2, 31, 128)),
    ("cm_dw_b", (2, 128)), ("cm_ln_g", (2, 128)), ("cm_ln_b", (2, 128)), ("cm_b_out", (2, 128)),
    ("gm_b_in", (1, 512)), ("gm_ln_g", (1, 256)), ("gm_ln_b", (1, 256)))


def _mixer_weights(i, P):
    if i % 3 == 0:
        j = i // 3
        return dict(w_in=P["cm_w_in"][j], w_out=P["cm_w_out"][j], b_in=P["cm_b_in"][j].reshape(1, 1, 2 * D),
                    dw_w=P["cm_dw_w"][j][None], dw_b=P["cm_dw_b"][j].reshape(1, 1, D),
                    ln_g=P["cm_ln_g"][j].reshape(1, 1, D), ln_b=P["cm_ln_b"][j].reshape(1, 1, D),
                    b_out=P["cm_b_out"][j].reshape(1, 1, D))
    if i % 3 == 1:
        return dict(w_qkv=P["attn_w_qkv"], w_o=P["attn_w_o"], sink=P["attn_sink"].reshape(N_Q))
    return dict(w_in=P["gm_w_in"], w_out=P["gm_w_out"], b_in=P["gm_b_in"].reshape(1, 1, 2 * GM_W),
                ln_g=P["gm_ln_g"].reshape(1, 1, GM_W), ln_b=P["gm_ln_b"].reshape(1, 1, GM_W),
                w_s=P["gm_w_s"].reshape(GM_GROUPS, GM_CHUNK, GM_CHUNK).astype(BF16),
                b_s=P["gm_b_s"].reshape(GM_GROUPS, GM_CHUNK, 1))


def _ffn_weights(i, P):
    return dict(up=P["ffn_w_up"][i].reshape(2, FFN_PAIRS, D, FFN_BLK), down=P["ffn_w_down"][i],
                cw=P["ffn_conv_w"][i].reshape(2, FFN_PAIRS, FFN_K, FFN_BLK),
                cb=P["ffn_conv_b"][i].reshape(2, FFN_PAIRS, 1, FFN_BLK))


def _local_step(x, ctx, target, lat_mod, ctx_mod, norm_g, layer_weights, grads_ready):
    tables = _rope_tables()
    ng = [[norm_g[i, j].reshape(1, 1, D) for j in range(4)] for i in range(DEPTH)]

    def mods(i, with_ctx, token):
        out = {}
        for j, nme in enumerate(MOD_NAMES):
            rows = [lat_mod[i, j]] + ([ctx_mod[i, j]] if with_ctx else [])
            out[nme] = jnp.stack(rows).reshape(len(rows), 1, D) + token[0, 0]
        return out

    def after(mod, token):
        return mod if token is None else {k_: v_ + token[0, 0] for k_, v_ in mod.items()}

    h_all = jnp.concatenate([x, ctx], axis=1)
    wm0, wf0, tok = layer_weights(0, h_all)
    m0 = mods(0, True, tok)
    h, r0m = _conformer_fwd(0, h_all, m0, ng[0], wm0)
    h, r0f = _ffn_fwd(0, h, m0, ng[0], wf0)
    wm1, wf1, tok = layer_weights(1, h)
    m1 = mods(1, True, tok)
    m1l = {k_: v_[:1] for k_, v_ in m1.items()}
    h, r1m = _attention_fwd(1, h, m1, ng[1], wm1, tables)
    h, r1f = _ffn_fwd(1, h, m1l, ng[1], wf1)
    wm2, wf2, tok = layer_weights(2, h)
    m2 = mods(2, False, tok)
    h, r2m = _gmlp_fwd(2, h, m2, ng[2], wm2)
    h, r2f = _ffn_fwd(2, h, m2, ng[2], wf2)
    wm3, wf3, tok = layer_weights(3, h)
    m3 = mods(3, False, tok)
    h, r3m = _conformer_fwd(3, h, m3, ng[3], wm3)
    h, r3f = _ffn_fwd(3, h, m3, ng[3], wf3)
    dh, loss = _loss_head(h, target)

    G = {}
    dh, G["f3"] = _ffn_bwd(3, dh, r3f, m3, ng[3], wf3)
    tok = grads_ready("f3", G["f3"])
    dh, G["m3"] = _conformer_bwd(3, dh, r3m, after(m3, tok), ng[3], wm3)
    tok = grads_ready("m3", G["m3"])
    dh, G["f2"] = _ffn_bwd(2, dh, r2f, after(m2, tok), ng[2], wf2)
    tok = grads_ready("f2", G["f2"])
    dh, G["m2"] = _gmlp_bwd(2, dh, r2m, after(m2, tok), ng[2], wm2)
    tok = grads_ready("m2", G["m2"])
    dh, G["f1"] = _ffn_bwd(1, dh, r1f, after(m1l, tok), ng[1], wf1)
    tok = grads_ready("f1", G["f1"])
    dh, G["m1"] = _attention_bwd(1, dh, r1m, after(m1, tok), ng[1], wm1, tables)
    tok = grads_ready("m1", G["m1"])
    dh, G["f0"] = _ffn_bwd(0, dh, r0f, after(m0, tok), ng[0], wf0)
    tok = grads_ready("f0", G["f0"])
    dh, G["m0"] = _conformer_bwd(0, dh, r0m, after(m0, tok), ng[0], wm0)
    grads_ready("m0", G["m0"])
    grad_x = dh[:, :L]

    zero = jnp.zeros((D,), F32)
    dmod = []
    for seg in range(2):
        per_layer = []
        for i in range(DEPTH):
            vals = []
            for nme in MOD_NAMES:
                src = G[("m" if nme.endswith("1") else "f") + str(i)][nme]
                vals.append(src[seg, 0] if src.shape[0] > seg else zero)
            per_layer.append(jnp.concatenate(vals))
        dmod.append(jnp.stack(per_layer))
    dmod = jnp.stack(dmod)
    return loss, grad_x, G, dmod


def kernel(x, c, ctx, c_ctx, ada_w, ada_b, norm_g, ffn_w_up, ffn_conv_w, ffn_conv_b, ffn_w_down, cm_w_in, cm_b_in, cm_dw_w, cm_dw_b, cm_ln_g, cm_ln_b, cm_w_out, cm_b_out, attn_w_qkv, attn_sink, attn_w_o, gm_w_in, gm_b_in, gm_ln_g, gm_ln_b, gm_w_s, gm_b_s, gm_w_out, loss_target, m_c_ctx, m_ada_w, m_ada_b, m_norm_g, m_ffn_w_up, m_ffn_conv_w, m_ffn_conv_b, m_ffn_w_down, m_cm_w_in, m_cm_b_in, m_cm_dw_w, m_cm_dw_b, m_cm_ln_g, m_cm_ln_b, m_cm_w_out, m_cm_b_out, m_attn_w_qkv, m_attn_sink, m_attn_w_o, m_gm_w_in, m_gm_b_in, m_gm_ln_g, m_gm_ln_b, m_gm_w_s, m_gm_b_s, m_gm_w_out, v_c_ctx, v_ada_w, v_ada_b, v_norm_g, v_ffn_w_up, v_ffn_conv_w, v_ffn_conv_b, v_ffn_w_down, v_cm_w_in, v_cm_b_in, v_cm_dw_w, v_cm_dw_b, v_cm_ln_g, v_cm_ln_b, v_cm_w_out, v_cm_b_out, v_attn_w_qkv, v_attn_sink, v_attn_w_o, v_gm_w_in, v_gm_b_in, v_gm_ln_g, v_gm_ln_b, v_gm_w_s, v_gm_b_s, v_gm_w_out):
    W = dict(c_ctx=c_ctx, ada_w=ada_w, ada_b=ada_b, norm_g=norm_g, ffn_w_up=ffn_w_up, ffn_conv_w=ffn_conv_w, ffn_conv_b=ffn_conv_b, ffn_w_down=ffn_w_down, cm_w_in=cm_w_in, cm_b_in=cm_b_in, cm_dw_w=cm_dw_w, cm_dw_b=cm_dw_b, cm_ln_g=cm_ln_g, cm_ln_b=cm_ln_b, cm_w_out=cm_w_out, cm_b_out=cm_b_out, attn_w_qkv=attn_w_qkv, attn_sink=attn_sink, attn_w_o=attn_w_o, gm_w_in=gm_w_in, gm_b_in=gm_b_in, gm_ln_g=gm_ln_g, gm_ln_b=gm_ln_b, gm_w_s=gm_w_s, gm_b_s=gm_b_s, gm_w_out=gm_w_out)
    M = dict(c_ctx=m_c_ctx, ada_w=m_ada_w, ada_b=m_ada_b, norm_g=m_norm_g, ffn_w_up=m_ffn_w_up, ffn_conv_w=m_ffn_conv_w, ffn_conv_b=m_ffn_conv_b, ffn_w_down=m_ffn_w_down, cm_w_in=m_cm_w_in, cm_b_in=m_cm_b_in, cm_dw_w=m_cm_dw_w, cm_dw_b=m_cm_dw_b, cm_ln_g=m_cm_ln_g, cm_ln_b=m_cm_ln_b, cm_w_out=m_cm_w_out, cm_b_out=m_cm_b_out, attn_w_qkv=m_attn_w_qkv, attn_sink=m_attn_sink, attn_w_o=m_attn_w_o, gm_w_in=m_gm_w_in, gm_b_in=m_gm_b_in, gm_ln_g=m_gm_ln_g, gm_ln_b=m_gm_ln_b, gm_w_s=m_gm_w_s, gm_b_s=m_gm_b_s, gm_w_out=m_gm_w_out)
    V = dict(c_ctx=v_c_ctx, ada_w=v_ada_w, ada_b=v_ada_b, norm_g=v_norm_g, ffn_w_up=v_ffn_w_up, ffn_conv_w=v_ffn_conv_w, ffn_conv_b=v_ffn_conv_b, ffn_w_down=v_ffn_w_down, cm_w_in=v_cm_w_in, cm_b_in=v_cm_b_in, cm_dw_w=v_cm_dw_w, cm_dw_b=v_cm_dw_b, cm_ln_g=v_cm_ln_g, cm_ln_b=v_cm_ln_b, cm_w_out=v_cm_w_out, cm_b_out=v_cm_b_out, attn_w_qkv=v_attn_w_qkv, attn_sink=v_attn_sink, attn_w_o=v_attn_w_o, gm_w_in=v_gm_w_in, gm_b_in=v_gm_b_in, gm_ln_g=v_gm_ln_g, gm_ln_b=v_gm_ln_b, gm_w_s=v_gm_w_s, gm_b_s=v_gm_b_s, gm_w_out=v_gm_w_out)
    me = 4 * lax.axis_index("x") + 2 * lax.axis_index("y") + lax.axis_index("c")
    small_shapes = [s for _, s in SMALL]

    small = _pack_rows([W[n] for n, _ in SMALL] + [c])
    layer_mats = [("cm_w_in", 0, "cm_w_out", 0), ("attn_w_qkv", 0, "attn_w_o", 0), ("gm_w_in", 0, "gm_w_out", 0),
                  ("cm_w_in", 1, "cm_w_out", 1)]
    local_bf16 = [[W[a][ja].astype(BF16), W[b][jb].astype(BF16), ffn_w_up[i].astype(BF16), ffn_w_down[i].astype(BF16)]
                  for i, (a, ja, b, jb) in enumerate(layer_mats)]
    gathered0 = _all_gather([small] + local_bf16[0], "gather_params0")
    small_g = gathered0[0]
    col_to_full = lambda g: g.transpose(1, 0, 2).reshape(g.shape[1], NDEV * g.shape[2])
    P = {}
    unpacked = jax.vmap(lambda r: tuple(_unpack_rows(r, small_shapes + [(D,)])))(small_g)
    for (n, _), g in zip(SMALL, unpacked[:-1]):
        if n == "ffn_conv_w":
            P[n] = [g[:, i] for i in range(DEPTH)]
        else:
            P[n] = _unshard_last(g)
    c_all = unpacked[-1]
    P["ffn_conv_b"] = [ffn_conv_b[i].reshape(NDEV, 1, FFN_BLK) for i in range(DEPTH)]
    P["attn_sink"], P["gm_w_s"], P["gm_b_s"] = attn_sink, gm_w_s, gm_b_s

    cond = jnp.concatenate([c_all, c_ctx[None], jnp.zeros((7, D), F32)])[None]
    scond = _rw_fwd(f_silu, [(cond, 0)], [], [(D, BF16)], name="ada_silu")[0]
    ada_bf = ada_w.astype(BF16)
    ncol = ada_w.shape[2]
    mod_loc = _mm(scond, ada_bf, "nn", name="ada_proj")
    mod_loc = mod_loc + lax.dynamic_slice_in_dim(ada_b, me * ncol, ncol, axis=1)[:, None, :]
    mod_g = _all_gather([mod_loc], "gather_mod")[0]
    mod_full = mod_g.transpose(1, 2, 0, 3).reshape(DEPTH, 16, 6, D)
    lat_mod = lax.dynamic_index_in_dim(mod_full, me, axis=1, keepdims=False)
    ctx_mod = mod_full[:, NDEV]

    gathers, exchanges, pending = {}, {}, {}
    col_to_parts = lambda g: g[0].reshape(g.shape[1], NDEV, g.shape[2] // NDEV).transpose(1, 0, 2)
    row_to_parts = lambda g: g.reshape(NDEV, -1, g.shape[-1])
    no_order = jnp.zeros((8, 128), F32)

    def layer_weights(i, h):
        if i == 0:
            mats = gathered0[1:]
            gathers[1] = _xfer_start("gather", local_bf16[1], mod_g, "gather_params1")
        else:
            mats = _xfer_wait(gathers[i], h)
            if i + 1 < DEPTH:
                gathers[i + 1] = _xfer_start("gather", local_bf16[i + 1], mats[0], f"gather_params{i + 1}")
        token = gathers[i + 1]["token"] if i + 1 < DEPTH else no_order
        a, ja, b, jb = layer_mats[i]
        pi = dict(P)
        pi[a] = col_to_full(mats[0]) if a.startswith(("attn", "gm")) else {ja: col_to_full(mats[0])}
        pi[b] = mats[1].reshape(-1, D) if b.startswith(("attn", "gm")) else {jb: mats[1].reshape(-1, D)}
        pi["ffn_w_up"], pi["ffn_w_down"] = {i: mats[2]}, {i: mats[3].reshape(FFN_PAIRS, FFN_BLK, D)}
        return _mixer_weights(i, pi), _ffn_weights(i, pi), token

    def grads_ready(tag, g):
        pending[tag] = g
        i = int(tag[1])
        col_name, row_name = {0: ("w_in", "w_out"), 1: ("w_qkv", "w_o"), 2: ("w_in", "w_out")}[i % 3]
        if tag == "f0":
            arrs = [g["up"], row_to_parts(g["down"])]
        elif tag == "m0":
            arrs = [col_to_parts(g[col_name]), row_to_parts(g[row_name])]
        elif tag[0] == "m":
            gf = pending[f"f{i}"]
            arrs = [col_to_parts(g[col_name]), row_to_parts(g[row_name]), gf["up"], row_to_parts(gf["down"])]
        else:
            return None
        exchanges[tag] = _xfer_start("scatter", arrs, no_order, "exchange_" + tag)
        return exchanges[tag]["token"]

    loss_part, grad_x, G, dmod = _local_step(x, ctx, loss_target, lat_mod, ctx_mod, P["norm_g"], layer_weights,
                                             grads_ready)
    loss = lax.psum(loss_part, AXES)
    recv = {tag: _xfer_wait(st, grad_x) for tag, st in exchanges.items()}

    dmod_g = _all_gather([dmod], "gather_dmod")[0]
    dm_cols = lax.dynamic_slice_in_dim(dmod_g, me * ncol, ncol, axis=3)
    dm_ext = dm_cols.transpose(2, 1, 0, 3).reshape(DEPTH, 16, ncol)
    cond_ext = jnp.concatenate([c_all, jnp.broadcast_to(c_ctx[None], (NDEV, D))])[None]
    scond_ext = _rw_fwd(f_silu, [(cond_ext, 0)], [], [(D, BF16)], name="ada_silu_ext")[0]
    g_ada_w = _mm(scond_ext, dm_ext, "tn", name="ada_proj_bw")
    dsil = _mm(dm_ext, ada_bf, "nt", reduce_blocks=True, name="ada_proj_bx")
    dcc = _rw_bwd(f_silu_rows, [(jnp.zeros((1, NDEV, D), F32), 0)], [(c_ctx.reshape(1, 1, D), "one")],
                  [dsil[:, NDEV:]], name="ada_silu_b", param_grad=[0])[0]

    out = {}

    def put(name, res):
        out[name] = res

    d_norm_g = jnp.stack([jnp.stack([G[f"m{i}"]["ng0"], G[f"m{i}"]["ng1"], G[f"f{i}"]["ng2"], G[f"f{i}"]["ng3"]])
                          for i in range(DEPTH)]).reshape(DEPTH, 4, D)
    small_full = dict(
        norm_g=d_norm_g,
        cm_b_in=jnp.stack([G["m0"]["b_in"], G["m3"]["b_in"]]).reshape(2, 2 * D),
        cm_dw_w=jnp.stack([G["m0"]["dw_w"][0], G["m3"]["dw_w"][0]]),
        cm_dw_b=jnp.stack([G["m0"]["dw_b"], G["m3"]["dw_b"]]).reshape(2, D),
        cm_ln_g=jnp.stack([G["m0"]["ln_g"], G["m3"]["ln_g"]]).reshape(2, D),
        cm_ln_b=jnp.stack([G["m0"]["ln_b"], G["m3"]["ln_b"]]).reshape(2, D),
        cm_b_out=jnp.stack([G["m0"]["b_out"], G["m3"]["b_out"]]).reshape(2, D),
        gm_b_in=G["m2"]["b_in"].reshape(1, 2 * GM_W),
        gm_ln_g=G["m2"]["ln_g"].reshape(1, GM_W), gm_ln_b=G["m2"]["ln_b"].reshape(1, GM_W))
    by_dest = []
    for n, _ in SMALL:
        if n == "ffn_conv_w":
            by_dest.append(jnp.stack([G[f"f{i}"]["cw"] for i in range(DEPTH)], axis=1))
        else:
            by_dest.append(_shard_last(small_full[n]))
    small_send = jax.vmap(lambda *vs: _pack_rows(list(vs)))(*by_dest)
    small_recv = _all_to_all([[small_send]], "exchange_small")[0]

    def shard3(a):
        return a.reshape(a.shape[0], -1, a.shape[-1])

    big_parts = dict(
        ffn_w_up=[recv["f0"][0]] + [recv[f"m{i}"][2] for i in (1, 2, 3)],
        ffn_w_down=[recv["f0"][1]] + [recv[f"m{i}"][3] for i in (1, 2, 3)],
        cm_w_in=[recv["m0"][0], recv["m3"][0]], cm_w_out=[recv["m0"][1], recv["m3"][1]],
        attn_w_qkv=[recv["m1"][0]], attn_w_o=[recv["m1"][1]], gm_w_in=[recv["m2"][0]], gm_w_out=[recv["m2"][1]])
    for n, parts in big_parts.items():
        shp = W[n].shape
        res = _adamw(parts, shard3(W[n]), shard3(M[n]), shard3(V[n]), "adamw_" + n)
        put(n, [r.reshape(shp) for r in res])

    small_local = lambda d_: _pack_rows([d_[n] for n, _ in SMALL])[None]
    res = _adamw(small_recv, small_local(W), small_local(M), small_local(V), "adamw_small")
    unp = [_unpack_rows(r[0], small_shapes) for r in res]
    for q, (n, _) in enumerate(SMALL):
        put(n, [unp[t][q] for t in range(4)])

    repl_names = ["c_ctx", "ffn_conv_b", "attn_sink", "gm_b_s", "gm_w_s"]
    repl_part = dict(
        c_ctx=dcc.reshape(D),
        ffn_conv_b=jnp.stack([G[f"f{i}"]["cb"].reshape(2 * 2816) for i in range(DEPTH)]),
        attn_sink=G["m1"]["sink"].reshape(1, N_Q),
        gm_b_s=G["m2"]["b_s"].reshape(1, GM_GROUPS, GM_CHUNK),
        gm_w_s=G["m2"]["w_s"].reshape(1, GM_GROUPS, GM_CHUNK, GM_CHUNK))
    repl_shapes = [W[n].shape for n in repl_names]
    repl_g = _all_gather([_pack_rows([repl_part[n] for n in repl_names], mult=256)], "gather_repl")[0]
    repl_local = lambda d_: _pack_rows([d_[n] for n in repl_names], mult=256)[None]
    res = _adamw(repl_g[None], repl_local(W), repl_local(M), repl_local(V), "adamw_repl")
    unp = [_unpack_rows(r[0], repl_shapes) for r in res]
    for q, n in enumerate(repl_names):
        put(n, [unp[t][q] for t in range(4)])

    put("ada_w", _adamw(g_ada_w[:, None], ada_w, m_ada_w, v_ada_w, "adamw_ada_w"))
    ada_b_parts = dmod_g.reshape(1, 2 * NDEV, DEPTH, 6 * D)
    res = _adamw(ada_b_parts, ada_b[None], m_ada_b[None], v_ada_b[None], "adamw_ada_b")
    put("ada_b", [r[0] for r in res])

    names = ["c_ctx", "ada_w", "ada_b", "norm_g", "ffn_w_up", "ffn_conv_w", "ffn_conv_b", "ffn_w_down", "cm_w_in",
             "cm_b_in", "cm_dw_w", "cm_dw_b", "cm_ln_g", "cm_ln_b", "cm_w_out", "cm_b_out", "attn_w_qkv",
             "attn_sink", "attn_w_o", "gm_w_in", "gm_b_in", "gm_ln_g", "gm_ln_b", "gm_w_s", "gm_b_s", "gm_w_out"]
    return (loss, grad_x, *[out[n][0] for n in names], *[out[n][1] for n in names],
            *[out[n][2] for n in names], *[out[n][3] for n in names])
```

```python
import functools

import jax
import jax.numpy as jnp
from jax import lax
from jax.experimental import pallas as pl
from jax.experimental.pallas import tpu as pltpu

F32, BF16 = jnp.float32, jnp.bfloat16
MESH = pl.DeviceIdType.MESH
AXES = ("x", "y", "c")
NDEV = 8

D = 1024
L = 2048
LC = 256
TA = L + LC
DEPTH = 4
EPS = 1e-6
HEAD_DIM = 64
N_Q, N_KV, Q_PER_KV = 16, 4, 4
ATTN_BLOCK = 128
GRID_W = 64
ROPE_BASE = 10000.0
GM_W = 2048
GM_CHUNK = 128
GM_GROUPS = 16
FFN_BLK = 704
CM_K, FFN_K = 31, 3

ADAM_LR, ADAM_B1, ADAM_B2, ADAM_EPS, ADAM_WD, ADAM_STEP = 0.001, 0.9, 0.999, 1e-08, 0.01, 10

VMEM_LIMIT_V7X = 56 * 1024 * 1024
ROW_TILE_ELEMS = 256 * 1024


def _cparams(sem=None):
    kw = dict(vmem_limit_bytes=VMEM_LIMIT_V7X)
    if sem is not None:
        kw["dimension_semantics"] = sem
    return pltpu.CompilerParams(**kw)


def _pick(n, cands):
    for c in cands:
        if n % c == 0:
            return c
    return n


def _as3(a):
    return a if a.ndim == 3 else a[None]


def _all_gather(arrs, name):
    n = len(arrs)

    def body(*refs):
        ins, outs = refs[:n], refs[n:2 * n]
        send_sems, recv_sems, local_sems = refs[2 * n:]
        x, y, c = lax.axis_index("x"), lax.axis_index("y"), lax.axis_index("c")
        me, sibling = (x, y, c), (x, y, 1 - c)
        chips = [(1 - x, y), (x, 1 - y), (1 - x, 1 - y)]

        def slot(a, p):
            return outs[a].at[4 * p[0] + 2 * p[1] + p[2]]

        def copy(a, k, block, to, src=None):
            return pltpu.make_async_remote_copy(
                src_ref=slot(a, block) if src is None else src, dst_ref=slot(a, block),
                send_sem=send_sems.at[a, k], recv_sem=recv_sems.at[a, k],
                device_id=to, device_id_type=MESH)

        mine = [pltpu.make_async_copy(ins[a], slot(a, me), local_sems.at[a]) for a in range(n)]
        for m in mine:
            m.start()
        first = []
        for a in range(n):
            first.append(copy(a, 0, me, sibling, src=ins[a]))
            first += [copy(a, 1 + j, me, (*chip, c), src=ins[a]) for j, chip in enumerate(chips)]
        for cp in first:
            cp.start()
        passed = []
        for j, chip in enumerate(chips):
            for a in range(n):
                copy(a, 1 + j, (*chip, c), me).wait_recv()
                p = copy(a, 4 + j, (*chip, c), sibling)
                p.start()
                passed.append(p)
        for a in range(n):
            copy(a, 0, sibling, me).wait_recv()
            for j, chip in enumerate(chips):
                copy(a, 4 + j, (*chip, 1 - c), me).wait_recv()
        for cp in first + passed:
            cp.wait_send()
        for m in mine:
            m.wait()

    any_spec = pl.BlockSpec(memory_space=pl.ANY)
    outs = pl.pallas_call(
        body, name=name,
        out_shape=[jax.ShapeDtypeStruct((NDEV,) + a.shape, a.dtype) for a in arrs],
        in_specs=[any_spec] * n, out_specs=[any_spec] * n,
        scratch_shapes=[pltpu.SemaphoreType.DMA((n, 7)), pltpu.SemaphoreType.DMA((n, 7)),
                        pltpu.SemaphoreType.DMA((n,))],
    )(*arrs)
    return list(outs)


def _all_to_all(groups, name):
    flat = [(gi, li, a) for gi, g in enumerate(groups) for li, a in enumerate(g)]
    n, ng = len(flat), len(groups)

    def body(*refs):
        ins, outs = refs[:n], refs[n:n + ng]
        send_sems, recv_sems, local_sems = refs[n + ng:]
        x, y, c = lax.axis_index("x"), lax.axis_index("y"), lax.axis_index("c")
        me = 4 * x + 2 * y + c
        copies = []
        for a, (gi, li, _) in enumerate(flat):
            loc = pltpu.make_async_copy(ins[a].at[me], outs[gi].at[li, me], local_sems.at[a])
            loc.start()
            copies.append(loc)
            for k in range(1, NDEV):
                px = 1 - x if (k >> 2) & 1 else x
                py = 1 - y if (k >> 1) & 1 else y
                pc = 1 - c if k & 1 else c
                cp = pltpu.make_async_remote_copy(
                    src_ref=ins[a].at[4 * px + 2 * py + pc], dst_ref=outs[gi].at[li, me],
                    send_sem=send_sems.at[a, k - 1], recv_sem=recv_sems.at[a, k - 1],
                    device_id=(px, py, pc), device_id_type=MESH)
                cp.start()
                copies.append(cp)
        for cp in copies:
            cp.wait()

    any_spec = pl.BlockSpec(memory_space=pl.ANY)
    outs = pl.pallas_call(
        body, name=name,
        out_shape=[jax.ShapeDtypeStruct((len(g),) + g[0].shape, g[0].dtype) for g in groups],
        in_specs=[any_spec] * n, out_specs=[any_spec] * ng,
        scratch_shapes=[pltpu.SemaphoreType.DMA((n, 7)), pltpu.SemaphoreType.DMA((n, 7)),
                        pltpu.SemaphoreType.DMA((n,))],
    )(*[a for _, _, a in flat])
    return list(outs)


HBM_SPEC = pl.BlockSpec(memory_space=pltpu.HBM)
SEM_SPEC = pl.BlockSpec(memory_space=pltpu.SEMAPHORE)
ANY_SPEC = pl.BlockSpec(memory_space=pl.ANY)
SPLIT_EFFECT = pltpu.SideEffectType.DATAFLOW_SIDE_EFFECTING


def _remote_copies(kind, ins, lands, send_sems, recv_sems):
    x, y, c = lax.axis_index("x"), lax.axis_index("y"), lax.axis_index("c")
    me = 4 * x + 2 * y + c
    out = []
    for a in range(len(ins)):
        for k in range(1, NDEV):
            px = 1 - x if (k >> 2) & 1 else x
            py = 1 - y if (k >> 1) & 1 else y
            pc = 1 - c if k & 1 else c
            src = ins[a] if kind == "gather" else ins[a].at[4 * px + 2 * py + pc]
            out.append(pltpu.make_async_remote_copy(
                src_ref=src, dst_ref=lands[a].at[me], send_sem=send_sems.at[a * (NDEV - 1) + k - 1],
                recv_sem=recv_sems.at[a * (NDEV - 1) + k - 1], device_id=(px, py, pc), device_id_type=MESH))
    return out


def _local_copies(kind, ins, lands, local_sems):
    me = 4 * lax.axis_index("x") + 2 * lax.axis_index("y") + lax.axis_index("c")
    return [pltpu.make_async_copy(ins[a] if kind == "gather" else ins[a].at[me], lands[a].at[me], local_sems.at[a])
            for a in range(len(ins))]


def _xfer_start(kind, arrs, after, name):
    n = len(arrs)
    lands = [lax.empty((NDEV,) + a.shape if kind == "gather" else a.shape, a.dtype) for a in arrs]

    def body(*refs):
        ins, lnd = refs[:n], refs[n:2 * n]
        send_sems, recv_sems, local_sems = refs[2 * n + 1:2 * n + 4]
        for cp in _remote_copies(kind, ins, lnd, send_sems, recv_sems) + _local_copies(kind, ins, lnd, local_sems):
            cp.start()
        refs[-1][...] = jnp.zeros_like(refs[-1])

    outs = pl.pallas_call(
        body, name=name,
        out_shape=(pltpu.SemaphoreType.DMA((n * (NDEV - 1),)), pltpu.SemaphoreType.DMA((n * (NDEV - 1),)),
                   pltpu.SemaphoreType.DMA((n,)),
                   *[pltpu.HBM(a.shape, a.dtype) for a in arrs + lands], jax.ShapeDtypeStruct((8, 128), F32)),
        in_specs=[HBM_SPEC] * (2 * n) + [ANY_SPEC],
        out_specs=(SEM_SPEC, SEM_SPEC, SEM_SPEC, *[HBM_SPEC] * (2 * n), pl.BlockSpec(memory_space=pltpu.VMEM)),
        input_output_aliases={a: 3 + a for a in range(2 * n)},
        compiler_params=pltpu.CompilerParams(has_side_effects=SPLIT_EFFECT),
    )(*[pltpu.with_memory_space_constraint(a, pltpu.HBM) for a in arrs + lands], after)
    return dict(kind=kind, n=n, sems=list(outs[:3]), bufs=list(outs[3:3 + 2 * n]), token=outs[-1], name=name)


def _xfer_wait(st, after):
    kind, n = st["kind"], st["n"]

    def body(*refs):
        ins, lnd = refs[:n], refs[n:2 * n]
        send_sems, recv_sems, local_sems = refs[2 * n:2 * n + 3]
        for cp in _remote_copies(kind, ins, lnd, send_sems, recv_sems):
            cp.wait_send()
            cp.wait_recv()
        for cp in _local_copies(kind, ins, lnd, local_sems):
            cp.wait()

    outs = pl.pallas_call(
        body, name=st["name"] + "_wait",
        out_shape=tuple(pltpu.HBM(b.shape, b.dtype) for b in st["bufs"]),
        in_specs=[HBM_SPEC] * (2 * n) + [SEM_SPEC] * 3 + [ANY_SPEC],
        out_specs=tuple([HBM_SPEC] * (2 * n)),
        input_output_aliases={a: a for a in range(2 * n)},
        compiler_params=pltpu.CompilerParams(has_side_effects=SPLIT_EFFECT),
    )(*st["bufs"], *st["sems"], after)
    return list(outs[n:])


def _mm(a, b, kind, *, name, out_dtype=F32, reduce_blocks=False):
    a, b = _as3(a), _as3(b)
    nba, nbb = a.shape[0], b.shape[0]
    nb = max(nba, nbb)
    assert nba in (1, nb) and nbb in (1, nb)
    if kind == "tn":
        t, m = a.shape[1:]
        n = b.shape[2]
        assert b.shape[1] == t and not reduce_blocks
        tk = _pick(t, (512, 768, 256))
        tm = m if m <= 1024 else _pick(m, (1024,))
        nred = t // tk
    else:
        m, k = a.shape[1:]
        n = b.shape[2] if kind == "nn" else b.shape[1]
        assert (b.shape[1] if kind == "nn" else b.shape[2]) == k
        tm = _pick(m, (512, 768, 256))
        nred = nb if reduce_blocks else 1
    tn = n if n <= 1024 else _pick(n, (512,))
    nbo = 1 if reduce_blocks else nb

    def blk(nbx, g, r):
        if nbx == 1:
            return 0
        return r if reduce_blocks else g

    if kind == "nn":
        a_spec = pl.BlockSpec((1, tm, k), lambda g, j, i, r: (blk(nba, g, r), i, 0))
        b_spec = pl.BlockSpec((1, k, tn), lambda g, j, i, r: (blk(nbb, g, r), 0, j))
        dims = (((1,), (0,)), ((), ()))
    elif kind == "nt":
        a_spec = pl.BlockSpec((1, tm, k), lambda g, j, i, r: (blk(nba, g, r), i, 0))
        b_spec = pl.BlockSpec((1, tn, k), lambda g, j, i, r: (blk(nbb, g, r), j, 0))
        dims = (((1,), (1,)), ((), ()))
    else:
        a_spec = pl.BlockSpec((1, tk, tm), lambda g, j, i, r: (blk(nba, g, r), r, i))
        b_spec = pl.BlockSpec((1, tk, tn), lambda g, j, i, r: (blk(nbb, g, r), r, j))
        dims = (((0,), (0,)), ((), ()))
    o_spec = pl.BlockSpec((1, tm, tn), lambda g, j, i, r: (g, i, j))

    def body(a_ref, b_ref, o_ref, *scratch):
        prod = lax.dot_general(a_ref[0].astype(BF16), b_ref[0].astype(BF16), dims,
                               preferred_element_type=F32)
        if nred == 1:
            o_ref[0] = prod.astype(o_ref.dtype)
        else:
            acc = scratch[0]
            r = pl.program_id(3)

            @pl.when(r == 0)
            def _():
                acc[...] = prod

            @pl.when(r > 0)
            def _():
                acc[...] += prod

            @pl.when(r == nred - 1)
            def _():
                o_ref[0] = acc[...].astype(o_ref.dtype)

    return pl.pallas_call(
        body, name=name,
        out_shape=jax.ShapeDtypeStruct((nbo, m, n), out_dtype),
        grid=(nbo, n // tn, m // tm, nred),
        in_specs=[a_spec, b_spec], out_specs=o_spec,
        scratch_shapes=[pltpu.VMEM((tm, tn), F32)] if nred > 1 else [],
        compiler_params=_cparams(("parallel", "parallel", "parallel", "arbitrary")),
    )(a, b)


def _row_tile(t, widths):
    tm = max(16, ROW_TILE_ELEMS // max(widths))
    tm = min(tm, 256)
    return t if t < tm else tm


def _sel_index(sel, g, i, tm):
    if sel == "one":
        return 0
    if sel == "seg":
        return (i * tm) // L
    return g + sel


def _row_spec(arr, off, tm):
    return pl.BlockSpec((1, tm, arr.shape[2]), lambda g, i: (g + off, i, 0))


def _par_spec(arr, sel, tm):
    return pl.BlockSpec((1, 1, arr.shape[2]), lambda g, i: (_sel_index(sel, g, i, tm), 0, 0))


def _norm_ops(ops):
    return [(o[0], o[1], o[2] if len(o) > 2 else 1) for o in ops]


def _split_cols(vals, nsplit):
    out = []
    for v, ns in zip(vals, nsplit):
        w = v.shape[1] // ns
        out += [v] if ns == 1 else [v[:, q * w:(q + 1) * w] for q in range(ns)]
    return out


def _join_cols(flat, nsplit):
    out, pos = [], 0
    for ns in nsplit:
        out.append(flat[pos] if ns == 1 else jnp.concatenate(flat[pos:pos + ns], axis=1))
        pos += ns
    return out


def _rw_fwd(fn, rows, params, outs, *, name, nblk=None):
    rows, params = _norm_ops(rows), _norm_ops(params)
    t = rows[0][0].shape[1]
    nblk = nblk or rows[0][0].shape[0]
    tm = _row_tile(t, [r.shape[2] for r, _, _ in rows] + [w for w, _ in outs])
    nr, npar = len(rows), len(params)
    nsplit = [ns for _, _, ns in rows + params]

    def body(*refs):
        vals = _split_cols([r[0].astype(F32) for r in refs[:nr + npar]], nsplit)
        res = fn(*vals)
        for o_ref, o in zip(refs[nr + npar:], res):
            o_ref[0] = o.astype(o_ref.dtype)

    res = pl.pallas_call(
        body, name=name,
        out_shape=[jax.ShapeDtypeStruct((nblk, t, w), dt) for w, dt in outs],
        grid=(nblk, t // tm),
        in_specs=[_row_spec(r, off, tm) for r, off, _ in rows] + [_par_spec(p, s, tm) for p, s, _ in params],
        out_specs=[pl.BlockSpec((1, tm, w), lambda g, i: (g, i, 0)) for w, _ in outs],
        compiler_params=_cparams(("parallel", "parallel")),
    )(*[r for r, _, _ in rows], *[p for p, _, _ in params])
    return list(res)


def _rw_bwd(fn, rows, params, cts, *, name, row_grad=(), param_grad=(), add=None, nblk=None):
    rows, params = _norm_ops(rows), _norm_ops(params)
    t = cts[0].shape[1]
    nblk = nblk or cts[0].shape[0]
    tm = _row_tile(t, [r.shape[2] for r, _, _ in rows] + [c.shape[2] for c in cts])
    ni = t // tm
    nr, npar, nct = len(rows), len(params), len(cts)
    nadd = 0 if add is None else 1
    n_in = nr + npar + nct + nadd
    nsplit = [ns for _, _, ns in rows + params]

    def body(*refs):
        prim = _split_cols([r[0].astype(F32) for r in refs[:nr + npar]], nsplit)
        ct = tuple(r[0].astype(F32) for r in refs[nr + npar:nr + npar + nct])
        _, vjp = jax.vjp(fn, *prim)
        grads = _join_cols(list(vjp(ct)), nsplit)
        out_refs = refs[n_in:]
        for q, (ri, _) in enumerate(row_grad):
            gr = grads[ri]
            if q == 0 and nadd:
                gr = gr + refs[n_in - 1][0].astype(F32)
            out_refs[q][0] = gr.astype(out_refs[q].dtype)
        g, i = pl.program_id(0), pl.program_id(1)
        step = g * ni + i
        pg, pi = (step - 1) // ni, (step - 1) % ni
        for q, pidx in enumerate(param_grad):
            o_ref = out_refs[len(row_grad) + q]
            sel = params[pidx][1]
            val = grads[nr + pidx]
            if sel == "one":
                first = step == 0
            else:
                first = (step == 0) | (_sel_index(sel, g, i, tm) != _sel_index(sel, pg, pi, tm))

            @pl.when(first)
            def _(o_ref=o_ref, val=val):
                o_ref[0] = val

            @pl.when(jnp.logical_not(first))
            def _(o_ref=o_ref, val=val):
                o_ref[0] += val

    in_arrays = [r for r, _, _ in rows] + [p for p, _, _ in params] + list(cts) + ([add] if nadd else [])
    in_specs = ([_row_spec(r, off, tm) for r, off, _ in rows] + [_par_spec(p, s, tm) for p, s, _ in params]
                + [_row_spec(c, 0, tm) for c in cts] + ([_row_spec(add, 0, tm)] if nadd else []))
    out_shape, out_specs = [], []
    for ri, dt in row_grad:
        w = rows[ri][0].shape[2]
        out_shape.append(jax.ShapeDtypeStruct((nblk, t, w), dt))
        out_specs.append(pl.BlockSpec((1, tm, w), lambda g, i: (g, i, 0)))
    for pidx in param_grad:
        p, sel, _ = params[pidx]
        out_shape.append(jax.ShapeDtypeStruct(p.shape, F32))
        out_specs.append(_par_spec(p, sel, tm))
    res = pl.pallas_call(
        body, name=name, out_shape=out_shape, grid=(nblk, ni),
        in_specs=in_specs, out_specs=out_specs,
        compiler_params=_cparams(("arbitrary", "arbitrary")),
    )(*in_arrays)
    return list(res)


def _sigmoid(x):
    return 1.0 / (1.0 + jnp.exp(-x))


def _rms(x, g):
    return x * lax.rsqrt(jnp.mean(x * x, axis=-1, keepdims=True) + EPS) * g


def _ln(x, g, b):
    mu = jnp.mean(x, axis=-1, keepdims=True)
    xc = x - mu
    var = jnp.mean(xc * xc, axis=-1, keepdims=True)
    return xc * lax.rsqrt(var + EPS) * g + b


def _gelu_tanh(x):
    return 0.5 * x * (1.0 + jnp.tanh(0.7978845608028654 * (x + 0.044715 * (x * x * x))))


def f_modnorm(h, g, sc, sh):
    return (_rms(h, g) * (1.0 + sc) + sh,)


def f_gate_rms(y, gate, g):
    return (gate * _rms(y, g),)


def f_gate_rms_bias(y, gate, g, b):
    return (gate * _rms(y + b, g),)


def f_resgate(h, y, gate, g):
    return (h + gate * _rms(y, g),)


def f_resgate_bias(h, y, gate, g, b):
    return (h + gate * _rms(y + b, g),)


def f_glu(pa, pg, ba, bg):
    return ((pa + ba) * _sigmoid(pg + bg),)


def f_lnsilu(z, g, b):
    t = _ln(z, g, b)
    return (t * _sigmoid(t),)


def f_gmlp_pre(pu, pv, bu, bv, g, bb):
    return _gelu_tanh(pu + bu), _ln(_gelu_tanh(pv + bv), g, bb)


def f_ffn_gate(zg, zv):
    return (zg * _sigmoid(zg) * zv,)


def f_silu(x):
    return (x * _sigmoid(x),)


def f_silu_rows(dummy, cc):
    return (cc * _sigmoid(cc) + 0.0 * dummy,)


def _rope(x_in, tables, neg_sin, out_dtype, name):
    w = x_in.shape[2]
    sign = -1.0 if neg_sin else 1.0

    def fn(x, cos, sin):
        cos = jnp.tile(cos, (1, w // 128))
        sin = jnp.tile(sin, (1, w // 128)) * sign
        lane = lax.broadcasted_iota(jnp.int32, x.shape, 1) & 31
        rot = jnp.where(lane < 16, -pltpu.roll(x, w - 16, 1), pltpu.roll(x, 16, 1))
        return (x * cos + rot * sin,)

    return _rw_fwd(fn, [(x_in, 0), (tables[0], 0), (tables[1], 0)], [], [(w, out_dtype)], name=name)[0]


CONV_TM = 256
CONV_RC = 32


def _conv_geometry(x, k):
    nb, t, w = x.shape
    halo = 16 if k > 17 else 8
    cb = _pick(w, (512,)) if w > 768 else w
    return nb, t, w, halo, cb, (k - 1) // 2


def _conv_in_specs(t, halo, cb):
    per = CONV_TM // halo
    last = t // halo - 1
    return [
        pl.BlockSpec((1, CONV_TM, cb), lambda g, jc, i: (g, i, jc)),
        pl.BlockSpec((1, halo, cb), lambda g, jc, i: (g, jnp.maximum(i * per - 1, 0), jc)),
        pl.BlockSpec((1, halo, cb), lambda g, jc, i: (g, jnp.minimum((i + 1) * per, last), jc)),
    ]


def _conv_fill(xp, x_ref, prev_ref, next_ref, halo, t):
    i = pl.program_id(2)
    seg_first = (i * CONV_TM == 0) | (i * CONV_TM == L)
    seg_last = ((i + 1) * CONV_TM == L) | ((i + 1) * CONV_TM == t)
    xp[0:halo, :] = jnp.where(seg_first, 0.0, prev_ref[0].astype(F32))
    xp[halo:halo + CONV_TM, :] = x_ref[0].astype(F32)
    xp[halo + CONV_TM:, :] = jnp.where(seg_last, 0.0, next_ref[0].astype(F32))


def _dwconv(x, w, b, *, name, out_dtype=F32):
    k = w.shape[1]
    nb, t, wd, halo, cb, half = _conv_geometry(x, k)
    base = halo - half

    def body(*refs):
        x_ref, prev_ref, next_ref, w_ref = refs[:4]
        b_ref = refs[4] if b is not None else None
        o_ref, xp = refs[-2], refs[-1]
        _conv_fill(xp, x_ref, prev_ref, next_ref, halo, t)
        for r0 in range(0, CONV_TM, CONV_RC):
            acc = jnp.zeros((CONV_RC, cb), F32)
            for kk in range(k):
                acc = acc + w_ref[0, kk:kk + 1, :] * xp[r0 + base + kk:r0 + base + kk + CONV_RC, :]
            if b_ref is not None:
                acc = acc + b_ref[0]
            o_ref[0, r0:r0 + CONV_RC, :] = acc.astype(o_ref.dtype)

    in_specs = _conv_in_specs(t, halo, cb) + [pl.BlockSpec((1, k, cb), lambda g, jc, i: (g, 0, jc))]
    args = [x, x, x, w]
    if b is not None:
        in_specs.append(pl.BlockSpec((1, 1, cb), lambda g, jc, i: (g, 0, jc)))
        args.append(b)
    return pl.pallas_call(
        body, name=name, out_shape=jax.ShapeDtypeStruct((nb, t, wd), out_dtype),
        grid=(nb, wd // cb, t // CONV_TM), in_specs=in_specs,
        out_specs=pl.BlockSpec((1, CONV_TM, cb), lambda g, jc, i: (g, i, jc)),
        scratch_shapes=[pltpu.VMEM((CONV_TM + 2 * halo, cb), F32)],
        compiler_params=_cparams(("parallel", "parallel", "parallel")),
    )(*args)


def _dwconv_wgrad(x, dy, k, *, name):
    nb, t, wd, halo, cb, half = _conv_geometry(x, k)
    base = halo - half

    def body(x_ref, prev_ref, next_ref, dy_ref, dw_ref, db_ref, xp):
        _conv_fill(xp, x_ref, prev_ref, next_ref, halo, t)
        i = pl.program_id(2)

        @pl.when(i == 0)
        def _():
            dw_ref[...] = jnp.zeros_like(dw_ref)
            db_ref[...] = jnp.zeros_like(db_ref)

        dyv = dy_ref[0].astype(F32)
        db_ref[0] += jnp.sum(dyv, axis=0, keepdims=True)
        for kk in range(k):
            dw_ref[0, kk:kk + 1, :] += jnp.sum(dyv * xp[base + kk:base + kk + CONV_TM, :], axis=0, keepdims=True)

    dw, db = pl.pallas_call(
        body, name=name,
        out_shape=[jax.ShapeDtypeStruct((nb, k, wd), F32), jax.ShapeDtypeStruct((nb, 1, wd), F32)],
        grid=(nb, wd // cb, t // CONV_TM),
        in_specs=_conv_in_specs(t, halo, cb) + [pl.BlockSpec((1, CONV_TM, cb), lambda g, jc, i: (g, i, jc))],
        out_specs=[pl.BlockSpec((1, k, cb), lambda g, jc, i: (g, 0, jc)),
                   pl.BlockSpec((1, 1, cb), lambda g, jc, i: (g, 0, jc))],
        scratch_shapes=[pltpu.VMEM((CONV_TM + 2 * halo, cb), F32)],
        compiler_params=_cparams(("parallel", "parallel", "arbitrary")),
    )(x, x, x, dy)
    return dw, db


ATTN_SCALE = HEAD_DIM ** -0.5
QROWS = Q_PER_KV * ATTN_BLOCK
NEG = -1e30


def _attn_scores(q, kw, kc, n):
    nt = (((1,), (1,)), ((), ()))
    s_w = lax.dot_general(q, kw, nt, preferred_element_type=F32) * ATTN_SCALE
    qi = lax.broadcasted_iota(jnp.int32, s_w.shape, 0) & (ATTN_BLOCK - 1)
    kj = lax.broadcasted_iota(jnp.int32, s_w.shape, 1)
    key_abs = (n - 1) * ATTN_BLOCK + kj
    ok = (jnp.abs(qi + ATTN_BLOCK - kj) <= ATTN_BLOCK) & (key_abs >= 0) & (key_abs < L)
    s_w = jnp.where(ok, s_w, NEG)
    s_c = lax.dot_general(q, kc, nt, preferred_element_type=F32) * ATTN_SCALE
    return s_w, s_c


def _sink_col(sink_ref, hk):
    return jnp.concatenate([jnp.full((ATTN_BLOCK, 1), sink_ref[hk * Q_PER_KV + g], F32) for g in range(Q_PER_KV)], axis=0)


def _attn_specs():
    qspec = pl.BlockSpec((Q_PER_KV, ATTN_BLOCK, HEAD_DIM), lambda hk, n: (hk, n, 0))
    kspec = pl.BlockSpec((1, L + 2 * ATTN_BLOCK, HEAD_DIM), lambda hk, n: (hk, 0, 0))
    cspec = pl.BlockSpec((1, LC, HEAD_DIM), lambda hk, n: (hk, 0, 0))
    lspec = pl.BlockSpec((Q_PER_KV, ATTN_BLOCK, 1), lambda hk, n: (hk, n, 0))
    sspec = pl.BlockSpec(memory_space=pltpu.SMEM)
    return qspec, kspec, cspec, lspec, sspec


def _attn_fwd(q, k, v, kc, vc, sink):
    qspec, kspec, cspec, lspec, sspec = _attn_specs()

    def body(q_ref, k_ref, v_ref, kc_ref, vc_ref, sink_ref, o_ref, lse_ref):
        hk, n = pl.program_id(0), pl.program_id(1)
        qv = q_ref[...].reshape(QROWS, HEAD_DIM)
        start = pl.multiple_of(n * ATTN_BLOCK, ATTN_BLOCK)
        kw = k_ref[0, pl.ds(start, 3 * ATTN_BLOCK), :]
        vw = v_ref[0, pl.ds(start, 3 * ATTN_BLOCK), :]
        s_w, s_c = _attn_scores(qv, kw, kc_ref[0], n)
        sk = _sink_col(sink_ref, hk)
        m = jnp.maximum(jnp.maximum(jnp.max(s_w, -1, keepdims=True), jnp.max(s_c, -1, keepdims=True)), sk)
        p_w, p_c = jnp.exp(s_w - m), jnp.exp(s_c - m)
        den = jnp.sum(p_w, -1, keepdims=True) + jnp.sum(p_c, -1, keepdims=True) + jnp.exp(sk - m)
        o = (jnp.dot(p_w.astype(BF16), vw, preferred_element_type=F32)
             + jnp.dot(p_c.astype(BF16), vc_ref[0], preferred_element_type=F32)) / den
        o_ref[...] = o.reshape(Q_PER_KV, ATTN_BLOCK, HEAD_DIM).astype(o_ref.dtype)
        lse_ref[...] = (m + jnp.log(den)).reshape(Q_PER_KV, ATTN_BLOCK, 1)

    return pl.pallas_call(
        body, name="attn_fwd",
        out_shape=[jax.ShapeDtypeStruct((N_Q, L, HEAD_DIM), BF16), jax.ShapeDtypeStruct((N_Q, L, 1), F32)],
        grid=(N_KV, L // ATTN_BLOCK),
        in_specs=[qspec, kspec, kspec, cspec, cspec, sspec], out_specs=[qspec, lspec],
        compiler_params=_cparams(("parallel", "parallel")),
    )(q, k, v, kc, vc, sink)


def _attn_bwd(q, k, v, kc, vc, sink, o, lse, do):
    qspec, kspec, cspec, lspec, sspec = _attn_specs()
    tn = (((0,), (0,)), ((), ()))
    nt = (((1,), (1,)), ((), ()))

    def body(q_ref, k_ref, v_ref, kc_ref, vc_ref, sink_ref, o_ref, lse_ref, do_ref,
             dq_ref, dk_ref, dv_ref, dkc_ref, dvc_ref, dsink_ref):
        hk, n = pl.program_id(0), pl.program_id(1)
        qv = q_ref[...].reshape(QROWS, HEAD_DIM)
        start = pl.multiple_of(n * ATTN_BLOCK, ATTN_BLOCK)
        win = pl.ds(start, 3 * ATTN_BLOCK)
        kw, vw = k_ref[0, win, :], v_ref[0, win, :]
        kcv, vcv = kc_ref[0], vc_ref[0]
        s_w, s_c = _attn_scores(qv, kw, kcv, n)
        lse_v = lse_ref[...].reshape(QROWS, 1)
        p_w, p_c = jnp.exp(s_w - lse_v), jnp.exp(s_c - lse_v)
        dov = do_ref[...].reshape(QROWS, HEAD_DIM).astype(F32)
        ov = o_ref[...].reshape(QROWS, HEAD_DIM).astype(F32)
        delta = jnp.sum(dov * ov, -1, keepdims=True)
        dob = dov.astype(BF16)
        dp_w = lax.dot_general(dob, vw, nt, preferred_element_type=F32)
        dp_c = lax.dot_general(dob, vcv, nt, preferred_element_type=F32)
        ds_w = (p_w * (dp_w - delta) * ATTN_SCALE).astype(BF16)
        ds_c = (p_c * (dp_c - delta) * ATTN_SCALE).astype(BF16)
        dq = jnp.dot(ds_w, kw, preferred_element_type=F32) + jnp.dot(ds_c, kcv, preferred_element_type=F32)
        dq_ref[...] = dq.reshape(Q_PER_KV, ATTN_BLOCK, HEAD_DIM)

        @pl.when(n == 0)
        def _():
            dk_ref[...] = jnp.zeros_like(dk_ref)
            dv_ref[...] = jnp.zeros_like(dv_ref)
            dkc_ref[...] = jnp.zeros_like(dkc_ref)
            dvc_ref[...] = jnp.zeros_like(dvc_ref)

        dk_ref[0, win, :] += lax.dot_general(ds_w, qv, tn, preferred_element_type=F32)
        dv_ref[0, win, :] += lax.dot_general(p_w.astype(BF16), dob, tn, preferred_element_type=F32)
        dkc_ref[0] += lax.dot_general(ds_c, qv, tn, preferred_element_type=F32)
        dvc_ref[0] += lax.dot_general(p_c.astype(BF16), dob, tn, preferred_element_type=F32)
        dsk = -jnp.exp(_sink_col(sink_ref, hk) - lse_v) * delta
        for g in range(Q_PER_KV):
            part = jnp.sum(dsk[g * ATTN_BLOCK:(g + 1) * ATTN_BLOCK])
            idx = hk * Q_PER_KV + g

            @pl.when(n == 0)
            def _(part=part, idx=idx):
                dsink_ref[idx] = part

            @pl.when(n > 0)
            def _(part=part, idx=idx):
                dsink_ref[idx] += part

    kshape = jax.ShapeDtypeStruct((N_KV, L + 2 * ATTN_BLOCK, HEAD_DIM), F32)
    cshape = jax.ShapeDtypeStruct((N_KV, LC, HEAD_DIM), F32)
    return pl.pallas_call(
        body, name="attn_bwd",
        out_shape=[jax.ShapeDtypeStruct((N_Q, L, HEAD_DIM), F32), kshape, kshape, cshape, cshape,
                   jax.ShapeDtypeStruct((N_Q,), F32)],
        grid=(N_KV, L // ATTN_BLOCK),
        in_specs=[qspec, kspec, kspec, cspec, cspec, sspec, qspec, lspec, qspec],
        out_specs=[qspec, kspec, kspec, cspec, cspec, sspec],
        compiler_params=_cparams(("arbitrary", "arbitrary")),
    )(q, k, v, kc, vc, sink, o, lse, do)


def _gm_specs():
    rspec = pl.BlockSpec((1, GM_CHUNK, GM_W), lambda n: (0, n, 0))
    wspec = pl.BlockSpec((GM_GROUPS, GM_CHUNK, GM_CHUNK), lambda n: (0, 0, 0))
    bspec = pl.BlockSpec((GM_GROUPS, GM_CHUNK, 1), lambda n: (0, 0, 0))
    return rspec, wspec, bspec


def _gm_spatial_fwd(u, v, ws, bs):
    rspec, wspec, bspec = _gm_specs()

    def body(u_ref, v_ref, ws_ref, bs_ref, o_ref):
        for g in range(GM_GROUPS):
            cols = slice(g * GM_CHUNK, (g + 1) * GM_CHUNK)
            s = jnp.dot(ws_ref[g], v_ref[0, :, cols], preferred_element_type=F32) + bs_ref[g]
            o_ref[0, :, cols] = (u_ref[0, :, cols] * s).astype(o_ref.dtype)

    return pl.pallas_call(
        body, name="gm_spatial_fwd", out_shape=jax.ShapeDtypeStruct((1, L, GM_W), BF16),
        grid=(L // GM_CHUNK,), in_specs=[rspec, rspec, wspec, bspec], out_specs=rspec,
        compiler_params=_cparams(("parallel",)),
    )(u, v, ws, bs)


def _gm_spatial_bwd(u, v, ws, bs, dus):
    rspec, wspec, bspec = _gm_specs()
    tn = (((0,), (0,)), ((), ()))
    nt = (((1,), (1,)), ((), ()))

    def body(u_ref, v_ref, ws_ref, bs_ref, d_ref, du_ref, dv_ref, dws_ref, dbs_ref):
        n = pl.program_id(0)

        @pl.when(n == 0)
        def _():
            dws_ref[...] = jnp.zeros_like(dws_ref)
            dbs_ref[...] = jnp.zeros_like(dbs_ref)

        for g in range(GM_GROUPS):
            cols = slice(g * GM_CHUNK, (g + 1) * GM_CHUNK)
            vb = v_ref[0, :, cols]
            s = jnp.dot(ws_ref[g], vb, preferred_element_type=F32) + bs_ref[g]
            d = d_ref[0, :, cols].astype(F32)
            du_ref[0, :, cols] = d * s
            ds = d * u_ref[0, :, cols]
            dsb = ds.astype(BF16)
            dv_ref[0, :, cols] = lax.dot_general(ws_ref[g], dsb, tn, preferred_element_type=F32)
            dws_ref[g] += lax.dot_general(dsb, vb, nt, preferred_element_type=F32)
            dbs_ref[g] += jnp.sum(ds, axis=1, keepdims=True)

    row = jax.ShapeDtypeStruct((1, L, GM_W), F32)
    return pl.pallas_call(
        body, name="gm_spatial_bwd",
        out_shape=[row, row, jax.ShapeDtypeStruct((GM_GROUPS, GM_CHUNK, GM_CHUNK), F32),
                   jax.ShapeDtypeStruct((GM_GROUPS, GM_CHUNK, 1), F32)],
        grid=(L // GM_CHUNK,), in_specs=[rspec, rspec, wspec, bspec, rspec],
        out_specs=[rspec, rspec, wspec, bspec],
        compiler_params=_cparams(("arbitrary",)),
    )(u, v, ws, bs, dus)


def _loss_head(h, target):
    tm = 256

    def body(h_ref, t_ref, dh_ref, loss_ref):
        d = h_ref[0] - t_ref[0]
        dh_ref[0] = d * (1.0 / D)

        @pl.when(pl.program_id(0) == 0)
        def _():
            loss_ref[...] = jnp.zeros_like(loss_ref)

        loss_ref[...] += jnp.sum(d * d) * (0.5 / D)

    spec = pl.BlockSpec((1, tm, D), lambda i: (0, i, 0))
    dh, loss = pl.pallas_call(
        body, name="loss_head",
        out_shape=[jax.ShapeDtypeStruct((1, L, D), F32), jax.ShapeDtypeStruct((8, 128), F32)],
        grid=(L // tm,), in_specs=[spec, spec],
        out_specs=[spec, pl.BlockSpec((8, 128), lambda i: (0, 0))],
        compiler_params=_cparams(("arbitrary",)),
    )(h, target)
    return dh, loss[0, 0]


def _adamw(parts, w, m, v, name):
    per_layer = isinstance(parts, (list, tuple))
    plist = list(parts) if per_layer else [parts]
    nl = len(plist) if per_layer else parts.shape[0]
    s, r, c = plist[0].shape[-3:]
    tr = r
    for cand in (512, 256, 128, 64, 32, 16):
        if r % cand == 0 and cand * c <= 131072:
            tr = cand
            break
    nr = r // tr
    npart = len(plist)
    c1 = 1.0 / (1.0 - ADAM_B1 ** ADAM_STEP)
    c2 = 1.0 / (1.0 - ADAM_B2 ** ADAM_STEP)

    def body(*refs):
        w_ref, m_ref, v_ref, g_ref, d_ref, nm_ref, nv_ref = refs[npart:]

        def update(read):
            g = read(0).astype(F32)
            for q in range(1, s):
                g = g + read(q).astype(F32)
            mn = ADAM_B1 * m_ref[0] + (1.0 - ADAM_B1) * g
            vn = ADAM_B2 * v_ref[0] + (1.0 - ADAM_B2) * (g * g)
            g_ref[0] = g
            nm_ref[0] = mn
            nv_ref[0] = vn
            d_ref[0] = -ADAM_LR * ((mn * c1) / (jnp.sqrt(vn * c2) + ADAM_EPS) + ADAM_WD * w_ref[0])

        if not per_layer:
            update(lambda q: refs[0][0, q])
        else:
            for l in range(nl):
                @pl.when(pl.program_id(0) == l)
                def _(l=l):
                    update(lambda q: refs[l][q])

    spec = pl.BlockSpec((1, tr, c), lambda li, i: (li, i, 0))
    shp = jax.ShapeDtypeStruct((nl, r, c), F32)
    if per_layer:
        pspecs = [pl.BlockSpec((s, tr, c), lambda li, i, l=l: (0, jnp.where(li == l, i, jnp.where(li > l, nr - 1, 0)), 0))
                  for l in range(nl)]
    else:
        pspecs = [pl.BlockSpec((1, s, tr, c), lambda li, i: (li, 0, i, 0))]
    return pl.pallas_call(
        body, name=name, out_shape=[shp] * 4, grid=(nl, nr),
        in_specs=pspecs + [spec, spec, spec], out_specs=[spec] * 4,
        compiler_params=_cparams(("arbitrary", "arbitrary")),
    )(*plist, w, m, v)


def _pack_rows(vecs, lanes=128, mult=8):
    flat = jnp.concatenate([v.reshape(-1) for v in vecs])
    n = flat.shape[0]
    rows = -(-n // (mult * lanes)) * mult
    return jnp.pad(flat, (0, rows * lanes - n)).reshape(rows, lanes)


def _unpack_rows(packed, shapes):
    flat = packed.reshape(-1)
    out, pos = [], 0
    for s in shapes:
        n = 1
        for d_ in s:
            n *= d_
        out.append(flat[pos:pos + n].reshape(s))
        pos += n
    return out


def _unshard_last(g):
    lead = g.shape[1:-1]
    return jnp.moveaxis(g, 0, -2).reshape(*lead, NDEV * g.shape[-1])


def _shard_last(full):
    lead, w = full.shape[:-1], full.shape[-1] // NDEV
    return jnp.moveaxis(full.reshape(*lead, NDEV, w), -2, 0)


def _rope_tables():
    rows = L // GRID_W
    row = jnp.repeat(jnp.arange(rows), GRID_W).astype(F32)
    col = jnp.tile(jnp.arange(GRID_W), rows).astype(F32)
    axis_dim = HEAD_DIM // 2
    inv_freq = ROPE_BASE ** (-jnp.arange(0, axis_dim, 2, dtype=F32) / axis_dim)
    ang_r, ang_c = row[:, None] * inv_freq[None, :], col[:, None] * inv_freq[None, :]
    ang = jnp.concatenate([ang_r, ang_r, ang_c, ang_c], axis=-1)
    ang = jnp.concatenate([ang, ang], axis=-1)[None]
    return jnp.cos(ang), jnp.sin(ang)


def _heads(x, nh):
    t = x.shape[1]
    return x.reshape(t, nh, HEAD_DIM).transpose(1, 0, 2)


def _unheads(x):
    nh, t, _ = x.shape
    return x.transpose(1, 0, 2).reshape(1, t, nh * HEAD_DIM)


FFN_HALO = 16
FFN_PAIRS = 4


def _ffn_tile(t):
    return 512 if t == L else 256


def _halo_specs(t, tm, block, index):
    per, last = tm // FFN_HALO, t // FFN_HALO - 1
    return [pl.BlockSpec(block(tm), lambda d, i: index(d, i)),
            pl.BlockSpec(block(FFN_HALO), lambda d, i: index(d, jnp.maximum(i * per - 1, 0))),
            pl.BlockSpec(block(FFN_HALO), lambda d, i: index(d, jnp.minimum((i + 1) * per, last)))]


def _seg_edges(i, tm, t):
    return (i * tm == 0) | (i * tm == L), ((i + 1) * tm == L) | ((i + 1) * tm == t)


def _conv3(buf, s, w, r0, n):
    return (w[0:1] * buf[s, r0 - 1:r0 - 1 + n, :] + w[1:2] * buf[s, r0:r0 + n, :]
            + w[2:3] * buf[s, r0 + 1:r0 + 1 + n, :])


def _ffn_core_fwd(a2, up, cw, cb, down, name):
    t = a2.shape[1]
    tm = _ffn_tile(t)
    h0 = FFN_HALO

    def body(a_ref, ap_ref, an_ref, up_ref, cw_ref, cb_ref, dn_ref, z_ref, f_ref, abuf, zbuf):
        d, i = pl.program_id(0), pl.program_id(1)
        seg_first, seg_last = _seg_edges(i, tm, t)
        abuf[0:h0, :] = ap_ref[0]
        abuf[h0:h0 + tm, :] = a_ref[0]
        abuf[h0 + tm:, :] = an_ref[0]
        for s in range(2):
            zbuf[s] = jnp.dot(abuf[...], up_ref[s, 0], preferred_element_type=F32)

        @pl.when(seg_first)
        def _():
            zbuf[:, 0:h0, :] = jnp.zeros((2, h0, FFN_BLK), F32)

        @pl.when(seg_last)
        def _():
            zbuf[:, h0 + tm:, :] = jnp.zeros((2, h0, FFN_BLK), F32)

        for s in range(2):
            z_ref[s, 0] = zbuf[s, h0:h0 + tm, :].astype(z_ref.dtype)
        zg = _conv3(zbuf, 0, cw_ref[0, 0], h0, tm) + cb_ref[0, 0]
        zv = _conv3(zbuf, 1, cw_ref[1, 0], h0, tm) + cb_ref[1, 0]
        u = (zg * _sigmoid(zg) * zv).astype(BF16)
        prod = jnp.dot(u, dn_ref[0], preferred_element_type=F32)
        rows = pl.ds(pl.multiple_of(i * tm, tm), tm)

        @pl.when(d == 0)
        def _():
            f_ref[0, rows, :] = prod

        @pl.when(d > 0)
        def _():
            f_ref[0, rows, :] += prod

    pair = lambda r, c: pl.BlockSpec((2, 1, r, c), lambda d, i: (0, d, 0, 0))
    return pl.pallas_call(
        body, name=name,
        out_shape=[jax.ShapeDtypeStruct((2, FFN_PAIRS, t, FFN_BLK), BF16), jax.ShapeDtypeStruct((1, t, D), F32)],
        grid=(FFN_PAIRS, t // tm),
        in_specs=_halo_specs(t, tm, lambda r: (1, r, D), lambda d, i: (0, i, 0))
        + [pair(D, FFN_BLK), pair(FFN_K, FFN_BLK), pair(1, FFN_BLK),
           pl.BlockSpec((1, FFN_BLK, D), lambda d, i: (d, 0, 0))],
        out_specs=[pl.BlockSpec((2, 1, tm, FFN_BLK), lambda d, i: (0, d, i, 0)),
                   pl.BlockSpec((1, t, D), lambda d, i: (0, 0, 0))],
        scratch_shapes=[pltpu.VMEM((tm + 2 * h0, D), BF16), pltpu.VMEM((2, tm + 2 * h0, FFN_BLK), F32)],
        compiler_params=_cparams(("arbitrary", "arbitrary")),
    )(a2, a2, a2, up, cw, cb, down)


def _ffn_core_bwd(df, z, cw, cb, down, name):
    t = df.shape[1]
    tm = _ffn_tile(t)
    h0 = FFN_HALO
    ni = t // tm
    w0, wn = h0 // 2, tm + h0
    tn = (((0,), (0,)), ((), ()))
    nt = (((1,), (1,)), ((), ()))

    def body(df_ref, dfp_ref, dfn_ref, z_ref, zp_ref, zn_ref, cw_ref, cb_ref, dn_ref,
             dz_ref, dcw_ref, dcb_ref, ddn_ref, dfbuf, zbuf, dzbuf, acc):
        d, i = pl.program_id(0), pl.program_id(1)
        seg_first, seg_last = _seg_edges(i, tm, t)
        dfbuf[0:h0, :] = dfp_ref[0]
        dfbuf[h0:h0 + tm, :] = df_ref[0]
        dfbuf[h0 + tm:, :] = dfn_ref[0]
        for s in range(2):
            zbuf[s, 0:h0, :] = zp_ref[s, 0].astype(F32)
            zbuf[s, h0:h0 + tm, :] = z_ref[s, 0].astype(F32)
            zbuf[s, h0 + tm:, :] = zn_ref[s, 0].astype(F32)

        @pl.when(seg_first)
        def _():
            zbuf[:, 0:h0, :] = jnp.zeros((2, h0, FFN_BLK), F32)

        @pl.when(seg_last)
        def _():
            zbuf[:, h0 + tm:, :] = jnp.zeros((2, h0, FFN_BLK), F32)

        du = lax.dot_general(dfbuf[...], dn_ref[0], nt, preferred_element_type=F32)[w0:w0 + wn]
        zg = _conv3(zbuf, 0, cw_ref[0, 0], w0, wn) + cb_ref[0, 0]
        zv = _conv3(zbuf, 1, cw_ref[1, 0], w0, wn) + cb_ref[1, 0]
        sg = _sigmoid(zg)
        silu = zg * sg
        dzbuf[0, w0:w0 + wn, :] = du * zv * (sg * (1.0 + zg * (1.0 - sg)))
        dzbuf[1, w0:w0 + wn, :] = du * silu

        @pl.when(seg_first)
        def _():
            dzbuf[:, w0:h0, :] = jnp.zeros((2, h0 - w0, FFN_BLK), F32)

        @pl.when(seg_last)
        def _():
            dzbuf[:, h0 + tm:w0 + wn, :] = jnp.zeros((2, w0, FFN_BLK), F32)

        @pl.when(i == 0)
        def _():
            dcw_ref[...] = jnp.zeros_like(dcw_ref)
            dcb_ref[...] = jnp.zeros_like(dcb_ref)

        for s in range(2):
            w = cw_ref[s, 0]
            dz = (w[2:3] * dzbuf[s, h0 - 1:h0 - 1 + tm, :] + w[1:2] * dzbuf[s, h0:h0 + tm, :]
                  + w[0:1] * dzbuf[s, h0 + 1:h0 + 1 + tm, :])
            dz_ref[s, 0] = dz.astype(dz_ref.dtype)
            dzc = dzbuf[s, h0:h0 + tm, :]
            dcb_ref[s, 0] += jnp.sum(dzc, axis=0, keepdims=True)
            for k in range(FFN_K):
                dcw_ref[s, 0, k:k + 1, :] += jnp.sum(dzc * zbuf[s, h0 - 1 + k:h0 - 1 + k + tm, :], axis=0, keepdims=True)
        u = (silu * zv)[h0 - w0:h0 - w0 + tm].astype(BF16)
        prod = lax.dot_general(u, dfbuf[h0:h0 + tm, :], tn, preferred_element_type=F32)

        @pl.when(i == 0)
        def _():
            acc[...] = prod

        @pl.when(i > 0)
        def _():
            acc[...] += prod

        @pl.when(i == ni - 1)
        def _():
            ddn_ref[0] = acc[...].astype(ddn_ref.dtype)

    pair = lambda r, c: pl.BlockSpec((2, 1, r, c), lambda d, i: (0, d, 0, 0))
    return pl.pallas_call(
        body, name=name,
        out_shape=[jax.ShapeDtypeStruct((2, FFN_PAIRS, t, FFN_BLK), BF16),
                   jax.ShapeDtypeStruct((2, FFN_PAIRS, FFN_K, FFN_BLK), F32),
                   jax.ShapeDtypeStruct((2, FFN_PAIRS, 1, FFN_BLK), F32),
                   jax.ShapeDtypeStruct((FFN_PAIRS, FFN_BLK, D), BF16)],
        grid=(FFN_PAIRS, ni),
        in_specs=_halo_specs(t, tm, lambda r: (1, r, D), lambda d, i: (0, i, 0))
        + _halo_specs(t, tm, lambda r: (2, 1, r, FFN_BLK), lambda d, i: (0, d, i, 0))
        + [pair(FFN_K, FFN_BLK), pair(1, FFN_BLK), pl.BlockSpec((1, FFN_BLK, D), lambda d, i: (d, 0, 0))],
        out_specs=[pl.BlockSpec((2, 1, tm, FFN_BLK), lambda d, i: (0, d, i, 0)), pair(FFN_K, FFN_BLK),
                   pair(1, FFN_BLK), pl.BlockSpec((1, FFN_BLK, D), lambda d, i: (d, 0, 0))],
        scratch_shapes=[pltpu.VMEM((tm + 2 * h0, D), BF16), pltpu.VMEM((2, tm + 2 * h0, FFN_BLK), F32),
                        pltpu.VMEM((2, tm + 2 * h0, FFN_BLK), F32), pltpu.VMEM((FFN_BLK, D), F32)],
        compiler_params=_cparams(("arbitrary", "arbitrary")),
    )(df, df, df, z, z, z, cw, cb, down)


def _ffn_fwd(i, h, mod, ng, wts):
    a2 = _rw_fwd(f_modnorm, [(h, 0)], [(ng[2], "one"), (mod["sc2"], "seg"), (mod["sh2"], "seg")],
                 [(D, BF16)], name=f"ffn{i}_norm")[0]
    z, f = _ffn_core_fwd(a2, wts["up"], wts["cw"], wts["cb"], wts["down"], f"ffn{i}_core")
    h2 = _rw_fwd(f_resgate, [(h, 0), (f, 0)], [(mod["g2"], "seg"), (ng[3], "one")], [(D, F32)],
                 name=f"ffn{i}_res")[0]
    return h2, (h, a2, z, f)


def _ffn_bwd(i, dh, res, mod, ng, wts):
    h, a2, z, f = res
    t = h.shape[1]
    df, dg2, dng3 = _rw_bwd(f_gate_rms, [(f, 0)], [(mod["g2"], "seg"), (ng[3], "one")], [dh],
                            name=f"ffn{i}_res_b", row_grad=[(0, BF16)], param_grad=[0, 1])
    dz, dcw, dcb, d_down = _ffn_core_bwd(df, z, wts["cw"], wts["cb"], wts["down"], f"ffn{i}_core_b")
    dz = dz.reshape(NDEV, t, FFN_BLK)
    dcw, dcb = dcw.reshape(NDEV, FFN_K, FFN_BLK), dcb.reshape(NDEV, 1, FFN_BLK)
    up8 = wts["up"].reshape(NDEV, D, FFN_BLK)
    d_up = _mm(a2, dz, "tn", out_dtype=BF16, name=f"ffn{i}_up_bw")
    da2 = _mm(dz, up8, "nt", reduce_blocks=True, name=f"ffn{i}_up_bx")
    dh_in, dng2, dsc2, dsh2 = _rw_bwd(
        f_modnorm, [(h, 0)], [(ng[2], "one"), (mod["sc2"], "seg"), (mod["sh2"], "seg")], [da2],
        name=f"ffn{i}_norm_b", row_grad=[(0, F32)], param_grad=[0, 1, 2], add=dh)
    grads = dict(up=d_up, down=d_down, cw=dcw, cb=dcb, ng2=dng2, ng3=dng3, sc2=dsc2, sh2=dsh2, g2=dg2)
    return dh_in, grads


def _mixer_norm_fwd(i, h, mod, ng):
    return _rw_fwd(f_modnorm, [(h, 0)], [(ng[0], "one"), (mod["sc1"], "seg"), (mod["sh1"], "seg")],
                   [(D, BF16)], name=f"mix{i}_norm")[0]


def _mixer_norm_bwd(i, h, mod, ng, da, dh):
    return _rw_bwd(f_modnorm, [(h, 0)], [(ng[0], "one"), (mod["sc1"], "seg"), (mod["sh1"], "seg")], [da],
                   name=f"mix{i}_norm_b", row_grad=[(0, F32)], param_grad=[0, 1, 2], add=dh)


def _conformer_fwd(i, h, mod, ng, wts):
    a = _mixer_norm_fwd(i, h, mod, ng)
    p = _mm(a, wts["w_in"], "nn", name=f"cm{i}_in")
    z = _rw_fwd(f_glu, [(p, 0, 2)], [(wts["b_in"], "one", 2)], [(D, F32)], name=f"cm{i}_glu")[0]
    zc = _dwconv(z, wts["dw_w"], wts["dw_b"], name=f"cm{i}_conv")
    r = _rw_fwd(f_lnsilu, [(zc, 0)], [(wts["ln_g"], "one"), (wts["ln_b"], "one")], [(D, BF16)],
                name=f"cm{i}_ln")[0]
    y = _mm(r, wts["w_out"], "nn", name=f"cm{i}_out")
    h2 = _rw_fwd(f_resgate_bias, [(h, 0), (y, 0)], [(mod["g1"], "seg"), (ng[1], "one"), (wts["b_out"], "one")],
                 [(D, F32)], name=f"cm{i}_res")[0]
    return h2, (h, a, p, z, zc, r, y)


def _conformer_bwd(i, dh, res, mod, ng, wts):
    h, a, p, z, zc, r, y = res
    dy, dg1, dng1, db_out = _rw_bwd(
        f_gate_rms_bias, [(y, 0)], [(mod["g1"], "seg"), (ng[1], "one"), (wts["b_out"], "one")], [dh],
        name=f"cm{i}_res_b", row_grad=[(0, BF16)], param_grad=[0, 1, 2])
    dr = _mm(dy, wts["w_out"], "nt", name=f"cm{i}_out_bx")
    d_w_out = _mm(r, dy, "tn", out_dtype=BF16, name=f"cm{i}_out_bw")
    dzc, dln_g, dln_b = _rw_bwd(f_lnsilu, [(zc, 0)], [(wts["ln_g"], "one"), (wts["ln_b"], "one")], [dr],
                                name=f"cm{i}_ln_b", row_grad=[(0, F32)], param_grad=[0, 1])
    ddw_w, ddw_b = _dwconv_wgrad(z, dzc, CM_K, name=f"cm{i}_conv_bw")
    dz = _dwconv(dzc, wts["dw_w"][:, ::-1, :], None, name=f"cm{i}_conv_bx")
    dp, db_in = _rw_bwd(f_glu, [(p, 0, 2)], [(wts["b_in"], "one", 2)], [dz], name=f"cm{i}_glu_b",
                        row_grad=[(0, BF16)], param_grad=[0])
    d_w_in = _mm(a, dp, "tn", out_dtype=BF16, name=f"cm{i}_in_bw")
    da = _mm(dp, wts["w_in"], "nt", name=f"cm{i}_in_bx")
    dh_in, dng0, dsc1, dsh1 = _mixer_norm_bwd(i, h, mod, ng, da, dh)
    grads = dict(w_in=d_w_in, w_out=d_w_out, b_in=db_in, dw_w=ddw_w, dw_b=ddw_b, ln_g=dln_g, ln_b=dln_b,
                 b_out=db_out, ng0=dng0, ng1=dng1, sc1=dsc1, sh1=dsh1, g1=dg1)
    return dh_in, grads


def _attention_fwd(i, h_all, mod, ng, wts, tables):
    a = _mixer_norm_fwd(i, h_all, mod, ng)
    qkv = _mm(a, wts["w_qkv"], "nn", name="attn_qkv")
    kv0 = N_Q * HEAD_DIM
    kv1 = kv0 + N_KV * HEAD_DIM
    q = _rope(qkv[:, :L, :kv0], tables, False, BF16, "attn_rope_q")
    k = _rope(qkv[:, :L, kv0:kv1], tables, False, BF16, "attn_rope_k")
    pad = ((0, 0), (ATTN_BLOCK, ATTN_BLOCK), (0, 0))
    q_h = _heads(q, N_Q)
    k_h = jnp.pad(_heads(k, N_KV), pad)
    v_h = jnp.pad(_heads(qkv[:, :L, kv1:].astype(BF16), N_KV), pad)
    kc_h = _heads(qkv[:, L:, kv0:kv1].astype(BF16), N_KV)
    vc_h = _heads(qkv[:, L:, kv1:].astype(BF16), N_KV)
    o_h, lse = _attn_fwd(q_h, k_h, v_h, kc_h, vc_h, wts["sink"])
    o = _unheads(o_h)
    y = _mm(o, wts["w_o"], "nn", name="attn_o")
    h_lat = h_all[:, :L]
    mod_lat = {k_: v_[:1] for k_, v_ in mod.items()}
    h2 = _rw_fwd(f_resgate, [(h_lat, 0), (y, 0)], [(mod_lat["g1"], "seg"), (ng[1], "one")], [(D, F32)],
                 name="attn_res")[0]
    return h2, (h_all, a, q_h, k_h, v_h, kc_h, vc_h, o_h, lse, o, y)


def _attention_bwd(i, dh, res, mod, ng, wts, tables):
    h_all, a, q_h, k_h, v_h, kc_h, vc_h, o_h, lse, o, y = res
    mod_lat = {k_: v_[:1] for k_, v_ in mod.items()}
    dy, dg1, dng1 = _rw_bwd(f_gate_rms, [(y, 0)], [(mod_lat["g1"], "seg"), (ng[1], "one")], [dh],
                            name="attn_res_b", row_grad=[(0, BF16)], param_grad=[0, 1])
    do = _mm(dy, wts["w_o"], "nt", name="attn_o_bx")
    d_w_o = _mm(o, dy, "tn", out_dtype=BF16, name="attn_o_bw")
    dq_h, dk_h, dv_h, dkc_h, dvc_h, dsink = _attn_bwd(q_h, k_h, v_h, kc_h, vc_h, wts["sink"], o_h, lse,
                                                        _heads(do, N_Q))
    dq = _rope(_unheads(dq_h), tables, True, BF16, "attn_rope_q_b")
    dk = _rope(_unheads(dk_h[:, ATTN_BLOCK:-ATTN_BLOCK]), tables, True, BF16, "attn_rope_k_b")
    dv = _unheads(dv_h[:, ATTN_BLOCK:-ATTN_BLOCK]).astype(BF16)
    d_lat = jnp.concatenate([dq, dk, dv], axis=2)
    d_ctx = jnp.concatenate([jnp.zeros((1, LC, N_Q * HEAD_DIM), BF16), _unheads(dkc_h).astype(BF16),
                             _unheads(dvc_h).astype(BF16)], axis=2)
    dqkv = jnp.concatenate([d_lat, d_ctx], axis=1)
    d_w_qkv = _mm(a, dqkv, "tn", out_dtype=BF16, name="attn_qkv_bw")
    da = _mm(dqkv, wts["w_qkv"], "nt", name="attn_qkv_bx")
    dh_res = jnp.concatenate([dh, jnp.zeros((1, LC, D), F32)], axis=1)
    dh_in, dng0, dsc1, dsh1 = _mixer_norm_bwd(i, h_all, mod, ng, da, dh_res)
    grads = dict(w_qkv=d_w_qkv, w_o=d_w_o, sink=dsink, ng0=dng0, ng1=dng1, sc1=dsc1, sh1=dsh1, g1=dg1)
    return dh_in, grads


def _gmlp_fwd(i, h, mod, ng, wts):
    a = _mixer_norm_fwd(i, h, mod, ng)
    p = _mm(a, wts["w_in"], "nn", name="gm_in")
    u, v = _rw_fwd(f_gmlp_pre, [(p, 0, 2)], [(wts["b_in"], "one", 2), (wts["ln_g"], "one"), (wts["ln_b"], "one")],
                   [(GM_W, F32), (GM_W, BF16)], name="gm_pre")
    us = _gm_spatial_fwd(u, v, wts["w_s"], wts["b_s"])
    y = _mm(us, wts["w_out"], "nn", name="gm_out")
    h2 = _rw_fwd(f_resgate, [(h, 0), (y, 0)], [(mod["g1"], "seg"), (ng[1], "one")], [(D, F32)],
                 name="gm_res")[0]
    return h2, (h, a, p, u, v, us, y)


def _gmlp_bwd(i, dh, res, mod, ng, wts):
    h, a, p, u, v, us, y = res
    dy, dg1, dng1 = _rw_bwd(f_gate_rms, [(y, 0)], [(mod["g1"], "seg"), (ng[1], "one")], [dh],
                            name="gm_res_b", row_grad=[(0, BF16)], param_grad=[0, 1])
    dus = _mm(dy, wts["w_out"], "nt", name="gm_out_bx")
    d_w_out = _mm(us, dy, "tn", out_dtype=BF16, name="gm_out_bw")
    du, dv, dws, dbs = _gm_spatial_bwd(u, v, wts["w_s"], wts["b_s"], dus)
    dp, db_in, dln_g, dln_b = _rw_bwd(
        f_gmlp_pre, [(p, 0, 2)], [(wts["b_in"], "one", 2), (wts["ln_g"], "one"), (wts["ln_b"], "one")], [du, dv],
        name="gm_pre_b", row_grad=[(0, BF16)], param_grad=[0, 1, 2])
    d_w_in = _mm(a, dp, "tn", out_dtype=BF16, name="gm_in_bw")
    da = _mm(dp, wts["w_in"], "nt", name="gm_in_bx")
    dh_in, dng0, dsc1, dsh1 = _mixer_norm_bwd(i, h, mod, ng, da, dh)
    grads = dict(w_in=d_w_in, w_out=d_w_out, b_in=db_in, ln_g=dln_g, ln_b=dln_b, w_s=dws, b_s=dbs,
                 ng0=dng0, ng1=dng1, sc1=dsc1, sh1=dsh1, g1=dg1)
    return dh_in, grads


MOD_NAMES = ("sh1", "sc1", "g1", "sh2", "sc2", "g2")
SMALL = (
    ("norm_g", (4, 4, 128)), ("ffn_conv_w", (4, 3, 704)), ("cm_b_in", (2, 256)), ("cm_dw_w", (2, 31, 128)),
    ("cm_dw_b", (2, 128)), ("cm_ln_g", (2, 128)), ("cm_ln_b", (2, 128)), ("cm_b_out", (2, 128)),
    ("gm_b_in", (1, 512)), ("gm_ln_g", (1, 256)), ("gm_ln_b", (1, 256)))


def _mixer_weights(i, P):
    if i % 3 == 0:
        j = i // 3
        return dict(w_in=P["cm_w_in"][j], w_out=P["cm_w_out"][j], b_in=P["cm_b_in"][j].reshape(1, 1, 2 * D),
                    dw_w=P["cm_dw_w"][j][None], dw_b=P["cm_dw_b"][j].reshape(1, 1, D),
                    ln_g=P["cm_ln_g"][j].reshape(1, 1, D), ln_b=P["cm_ln_b"][j].reshape(1, 1, D),
                    b_out=P["cm_b_out"][j].reshape(1, 1, D))
    if i % 3 == 1:
        return dict(w_qkv=P["attn_w_qkv"], w_o=P["attn_w_o"], sink=P["attn_sink"].reshape(N_Q))
    return dict(w_in=P["gm_w_in"], w_out=P["gm_w_out"], b_in=P["gm_b_in"].reshape(1, 1, 2 * GM_W),
                ln_g=P["gm_ln_g"].reshape(1, 1, GM_W), ln_b=P["gm_ln_b"].reshape(1, 1, GM_W),
                w_s=P["gm_w_s"].reshape(GM_GROUPS, GM_CHUNK, GM_CHUNK).astype(BF16),
                b_s=P["gm_b_s"].reshape(GM_GROUPS, GM_CHUNK, 1))


def _ffn_weights(i, P):
    return dict(up=P["ffn_w_up"][i].reshape(2, FFN_PAIRS, D, FFN_BLK), down=P["ffn_w_down"][i],
                cw=P["ffn_conv_w"][i].reshape(2, FFN_PAIRS, FFN_K, FFN_BLK),
                cb=P["ffn_conv_b"][i].reshape(2, FFN_PAIRS, 1, FFN_BLK))


def _local_step(x, ctx, target, lat_mod, ctx_mod, norm_g, layer_weights, grads_ready):
    tables = _rope_tables()
    ng = [[norm_g[i, j].reshape(1, 1, D) for j in range(4)] for i in range(DEPTH)]

    def mods(i, with_ctx, token):
        out = {}
        for j, nme in enumerate(MOD_NAMES):
            rows = [lat_mod[i, j]] + ([ctx_mod[i, j]] if with_ctx else [])
            out[nme] = jnp.stack(rows).reshape(len(rows), 1, D) + token[0, 0]
        return out

    def after(mod, token):
        return mod if token is None else {k_: v_ + token[0, 0] for k_, v_ in mod.items()}

    h_all = jnp.concatenate([x, ctx], axis=1)
    wm0, wf0, tok = layer_weights(0, h_all)
    m0 = mods(0, True, tok)
    h, r0m = _conformer_fwd(0, h_all, m0, ng[0], wm0)
    h, r0f = _ffn_fwd(0, h, m0, ng[0], wf0)
    wm1, wf1, tok = layer_weights(1, h)
    m1 = mods(1, True, tok)
    m1l = {k_: v_[:1] for k_, v_ in m1.items()}
    h, r1m = _attention_fwd(1, h, m1, ng[1], wm1, tables)
    h, r1f = _ffn_fwd(1, h, m1l, ng[1], wf1)
    wm2, wf2, tok = layer_weights(2, h)
    m2 = mods(2, False, tok)
    h, r2m = _gmlp_fwd(2, h, m2, ng[2], wm2)
    h, r2f = _ffn_fwd(2, h, m2, ng[2], wf2)
    wm3, wf3, tok = layer_weights(3, h)
    m3 = mods(3, False, tok)
    h, r3m = _conformer_fwd(3, h, m3, ng[3], wm3)
    h, r3f = _ffn_fwd(3, h, m3, ng[3], wf3)
    dh, loss = _loss_head(h, target)

    G = {}
    dh, G["f3"] = _ffn_bwd(3, dh, r3f, m3, ng[3], wf3)
    tok = grads_ready("f3", G["f3"])
    dh, G["m3"] = _conformer_bwd(3, dh, r3m, after(m3, tok), ng[3], wm3)
    tok = grads_ready("m3", G["m3"])
    dh, G["f2"] = _ffn_bwd(2, dh, r2f, after(m2, tok), ng[2], wf2)
    tok = grads_ready("f2", G["f2"])
    dh, G["m2"] = _gmlp_bwd(2, dh, r2m, after(m2, tok), ng[2], wm2)
    tok = grads_ready("m2", G["m2"])
    dh, G["f1"] = _ffn_bwd(1, dh, r1f, after(m1l, tok), ng[1], wf1)
    tok = grads_ready("f1", G["f1"])
    dh, G["m1"] = _attention_bwd(1, dh, r1m, after(m1, tok), ng[1], wm1, tables)
    tok = grads_ready("m1", G["m1"])
    dh, G["f0"] = _ffn_bwd(0, dh, r0f, after(m0, tok), ng[0], wf0)
    tok = grads_ready("f0", G["f0"])
    dh, G["m0"] = _conformer_bwd(0, dh, r0m, after(m0, tok), ng[0], wm0)
    grads_ready("m0", G["m0"])
    grad_x = dh[:, :L]

    zero = jnp.zeros((D,), F32)
    dmod = []
    for seg in range(2):
        per_layer = []
        for i in range(DEPTH):
            vals = []
            for nme in MOD_NAMES:
                src = G[("m" if nme.endswith("1") else "f") + str(i)][nme]
                vals.append(src[seg, 0] if src.shape[0] > seg else zero)
            per_layer.append(jnp.concatenate(vals))
        dmod.append(jnp.stack(per_layer))
    dmod = jnp.stack(dmod)
    return loss, grad_x, G, dmod


def kernel(x, c, ctx, c_ctx, ada_w, ada_b, norm_g, ffn_w_up, ffn_conv_w, ffn_conv_b, ffn_w_down, cm_w_in, cm_b_in, cm_dw_w, cm_dw_b, cm_ln_g, cm_ln_b, cm_w_out, cm_b_out, attn_w_qkv, attn_sink, attn_w_o, gm_w_in, gm_b_in, gm_ln_g, gm_ln_b, gm_w_s, gm_b_s, gm_w_out, loss_target, m_c_ctx, m_ada_w, m_ada_b, m_norm_g, m_ffn_w_up, m_ffn_conv_w, m_ffn_conv_b, m_ffn_w_down, m_cm_w_in, m_cm_b_in, m_cm_dw_w, m_cm_dw_b, m_cm_ln_g, m_cm_ln_b, m_cm_w_out, m_cm_b_out, m_attn_w_qkv, m_attn_sink, m_attn_w_o, m_gm_w_in, m_gm_b_in, m_gm_ln_g, m_gm_ln_b, m_gm_w_s, m_gm_b_s, m_gm_w_out, v_c_ctx, v_ada_w, v_ada_b, v_norm_g, v_ffn_w_up, v_ffn_conv_w, v_ffn_conv_b, v_ffn_w_down, v_cm_w_in, v_cm_b_in, v_cm_dw_w, v_cm_dw_b, v_cm_ln_g, v_cm_ln_b, v_cm_w_out, v_cm_b_out, v_attn_w_qkv, v_attn_sink, v_attn_w_o, v_gm_w_in, v_gm_b_in, v_gm_ln_g, v_gm_ln_b, v_gm_w_s, v_gm_b_s, v_gm_w_out):
    W = dict(c_ctx=c_ctx, ada_w=ada_w, ada_b=ada_b, norm_g=norm_g, ffn_w_up=ffn_w_up, ffn_conv_w=ffn_conv_w, ffn_conv_b=ffn_conv_b, ffn_w_down=ffn_w_down, cm_w_in=cm_w_in, cm_b_in=cm_b_in, cm_dw_w=cm_dw_w, cm_dw_b=cm_dw_b, cm_ln_g=cm_ln_g, cm_ln_b=cm_ln_b, cm_w_out=cm_w_out, cm_b_out=cm_b_out, attn_w_qkv=attn_w_qkv, attn_sink=attn_sink, attn_w_o=attn_w_o, gm_w_in=gm_w_in, gm_b_in=gm_b_in, gm_ln_g=gm_ln_g, gm_ln_b=gm_ln_b, gm_w_s=gm_w_s, gm_b_s=gm_b_s, gm_w_out=gm_w_out)
    M = dict(c_ctx=m_c_ctx, ada_w=m_ada_w, ada_b=m_ada_b, norm_g=m_norm_g, ffn_w_up=m_ffn_w_up, ffn_conv_w=m_ffn_conv_w, ffn_conv_b=m_ffn_conv_b, ffn_w_down=m_ffn_w_down, cm_w_in=m_cm_w_in, cm_b_in=m_cm_b_in, cm_dw_w=m_cm_dw_w, cm_dw_b=m_cm_dw_b, cm_ln_g=m_cm_ln_g, cm_ln_b=m_cm_ln_b, cm_w_out=m_cm_w_out, cm_b_out=m_cm_b_out, attn_w_qkv=m_attn_w_qkv, attn_sink=m_attn_sink, attn_w_o=m_attn_w_o, gm_w_in=m_gm_w_in, gm_b_in=m_gm_b_in, gm_ln_g=m_gm_ln_g, gm_ln_b=m_gm_ln_b, gm_w_s=m_gm_w_s, gm_b_s=m_gm_b_s, gm_w_out=m_gm_w_out)
    V = dict(c_ctx=v_c_ctx, ada_w=v_ada_w, ada_b=v_ada_b, norm_g=v_norm_g, ffn_w_up=v_ffn_w_up, ffn_conv_w=v_ffn_conv_w, ffn_conv_b=v_ffn_conv_b, ffn_w_down=v_ffn_w_down, cm_w_in=v_cm_w_in, cm_b_in=v_cm_b_in, cm_dw_w=v_cm_dw_w, cm_dw_b=v_cm_dw_b, cm_ln_g=v_cm_ln_g, cm_ln_b=v_cm_ln_b, cm_w_out=v_cm_w_out, cm_b_out=v_cm_b_out, attn_w_qkv=v_attn_w_qkv, attn_sink=v_attn_sink, attn_w_o=v_attn_w_o, gm_w_in=v_gm_w_in, gm_b_in=v_gm_b_in, gm_ln_g=v_gm_ln_g, gm_ln_b=v_gm_ln_b, gm_w_s=v_gm_w_s, gm_b_s=v_gm_b_s, gm_w_out=v_gm_w_out)
    me = 4 * lax.axis_index("x") + 2 * lax.axis_index("y") + lax.axis_index("c")
    small_shapes = [s for _, s in SMALL]

    small = _pack_rows([W[n] for n, _ in SMALL] + [c])
    layer_mats = [("cm_w_in", 0, "cm_w_out", 0), ("attn_w_qkv", 0, "attn_w_o", 0), ("gm_w_in", 0, "gm_w_out", 0),
                  ("cm_w_in", 1, "cm_w_out", 1)]
    local_bf16 = [[W[a][ja].astype(BF16), W[b][jb].astype(BF16), ffn_w_up[i].astype(BF16), ffn_w_down[i].astype(BF16)]
                  for i, (a, ja, b, jb) in enumerate(layer_mats)]
    gathered0 = _all_gather([small] + local_bf16[0], "gather_params0")
    small_g = gathered0[0]
    col_to_full = lambda g: g.transpose(1, 0, 2).reshape(g.shape[1], NDEV * g.shape[2])
    P = {}
    unpacked = jax.vmap(lambda r: tuple(_unpack_rows(r, small_shapes + [(D,)])))(small_g)
    for (n, _), g in zip(SMALL, unpacked[:-1]):
        if n == "ffn_conv_w":
            P[n] = [g[:, i] for i in range(DEPTH)]
        else:
            P[n] = _unshard_last(g)
    c_all = unpacked[-1]
    P["ffn_conv_b"] = [ffn_conv_b[i].reshape(NDEV, 1, FFN_BLK) for i in range(DEPTH)]
    P["attn_sink"], P["gm_w_s"], P["gm_b_s"] = attn_sink, gm_w_s, gm_b_s

    cond = jnp.concatenate([c_all, c_ctx[None], jnp.zeros((7, D), F32)])[None]
    scond = _rw_fwd(f_silu, [(cond, 0)], [], [(D, BF16)], name="ada_silu")[0]
    ada_bf = ada_w.astype(BF16)
    ncol = ada_w.shape[2]
    mod_loc = _mm(scond, ada_bf, "nn", name="ada_proj")
    mod_loc = mod_loc + lax.dynamic_slice_in_dim(ada_b, me * ncol, ncol, axis=1)[:, None, :]
    mod_g = _all_gather([mod_loc], "gather_mod")[0]
    mod_full = mod_g.transpose(1, 2, 0, 3).reshape(DEPTH, 16, 6, D)
    lat_mod = lax.dynamic_index_in_dim(mod_full, me, axis=1, keepdims=False)
    ctx_mod = mod_full[:, NDEV]

    gathers, exchanges, pending = {}, {}, {}
    col_to_parts = lambda g: g[0].reshape(g.shape[1], NDEV, g.shape[2] // NDEV).transpose(1, 0, 2)
    row_to_parts = lambda g: g.reshape(NDEV, -1, g.shape[-1])
    no_order = jnp.zeros((8, 128), F32)

    def layer_weights(i, h):
        if i == 0:
            mats = gathered0[1:]
            gathers[1] = _xfer_start("gather", local_bf16[1], mod_g, "gather_params1")
        else:
            mats = _xfer_wait(gathers[i], h)
            if i + 1 < DEPTH:
                gathers[i + 1] = _xfer_start("gather", local_bf16[i + 1], mats[0], f"gather_params{i + 1}")
        token = gathers[i + 1]["token"] if i + 1 < DEPTH else no_order
        a, ja, b, jb = layer_mats[i]
        pi = dict(P)
        pi[a] = col_to_full(mats[0]) if a.startswith(("attn", "gm")) else {ja: col_to_full(mats[0])}
        pi[b] = mats[1].reshape(-1, D) if b.startswith(("attn", "gm")) else {jb: mats[1].reshape(-1, D)}
        pi["ffn_w_up"], pi["ffn_w_down"] = {i: mats[2]}, {i: mats[3].reshape(FFN_PAIRS, FFN_BLK, D)}
        return _mixer_weights(i, pi), _ffn_weights(i, pi), token

    def grads_ready(tag, g):
        pending[tag] = g
        i = int(tag[1])
        col_name, row_name = {0: ("w_in", "w_out"), 1: ("w_qkv", "w_o"), 2: ("w_in", "w_out")}[i % 3]
        if tag == "f0":
            arrs = [g["up"], row_to_parts(g["down"])]
        elif tag == "m0":
            arrs = [col_to_parts(g[col_name]), row_to_parts(g[row_name])]
        elif tag[0] == "m":
            gf = pending[f"f{i}"]
            arrs = [col_to_parts(g[col_name]), row_to_parts(g[row_name]), gf["up"], row_to_parts(gf["down"])]
        else:
            return None
        exchanges[tag] = _xfer_start("scatter", arrs, no_order, "exchange_" + tag)
        return exchanges[tag]["token"]

    loss_part, grad_x, G, dmod = _local_step(x, ctx, loss_target, lat_mod, ctx_mod, P["norm_g"], layer_weights,
                                             grads_ready)
    recv = {tag: _xfer_wait(st, grad_x) for tag, st in exchanges.items()}

    dmod_g = _all_gather([dmod], "gather_dmod")[0]
    dm_cols = lax.dynamic_slice_in_dim(dmod_g, me * ncol, ncol, axis=3)
    dm_ext = dm_cols.transpose(2, 1, 0, 3).reshape(DEPTH, 16, ncol)
    cond_ext = jnp.concatenate([c_all, jnp.broadcast_to(c_ctx[None], (NDEV, D))])[None]
    scond_ext = _rw_fwd(f_silu, [(cond_ext, 0)], [], [(D, BF16)], name="ada_silu_ext")[0]
    g_ada_w = _mm(scond_ext, dm_ext, "tn", name="ada_proj_bw")
    dsil = _mm(dm_ext, ada_bf, "nt", reduce_blocks=True, name="ada_proj_bx")
    dcc = _rw_bwd(f_silu_rows, [(jnp.zeros((1, NDEV, D), F32), 0)], [(c_ctx.reshape(1, 1, D), "one")],
                  [dsil[:, NDEV:]], name="ada_silu_b", param_grad=[0])[0]

    out = {}

    def put(name, res):
        out[name] = res

    d_norm_g = jnp.stack([jnp.stack([G[f"m{i}"]["ng0"], G[f"m{i}"]["ng1"], G[f"f{i}"]["ng2"], G[f"f{i}"]["ng3"]])
                          for i in range(DEPTH)]).reshape(DEPTH, 4, D)
    small_full = dict(
        norm_g=d_norm_g,
        cm_b_in=jnp.stack([G["m0"]["b_in"], G["m3"]["b_in"]]).reshape(2, 2 * D),
        cm_dw_w=jnp.stack([G["m0"]["dw_w"][0], G["m3"]["dw_w"][0]]),
        cm_dw_b=jnp.stack([G["m0"]["dw_b"], G["m3"]["dw_b"]]).reshape(2, D),
        cm_ln_g=jnp.stack([G["m0"]["ln_g"], G["m3"]["ln_g"]]).reshape(2, D),
        cm_ln_b=jnp.stack([G["m0"]["ln_b"], G["m3"]["ln_b"]]).reshape(2, D),
        cm_b_out=jnp.stack([G["m0"]["b_out"], G["m3"]["b_out"]]).reshape(2, D),
        gm_b_in=G["m2"]["b_in"].reshape(1, 2 * GM_W),
        gm_ln_g=G["m2"]["ln_g"].reshape(1, GM_W), gm_ln_b=G["m2"]["ln_b"].reshape(1, GM_W))
    by_dest = []
    for n, _ in SMALL:
        if n == "ffn_conv_w":
            by_dest.append(jnp.stack([G[f"f{i}"]["cw"] for i in range(DEPTH)], axis=1))
        else:
            by_dest.append(_shard_last(small_full[n]))
    small_send = jax.vmap(lambda *vs: _pack_rows(list(vs)))(*by_dest)
    small_recv = _all_to_all([[small_send]], "exchange_small")[0]

    def shard3(a):
        return a.reshape(a.shape[0], -1, a.shape[-1])

    big_parts = dict(
        ffn_w_up=[recv["f0"][0]] + [recv[f"m{i}"][2] for i in (1, 2, 3)],
        ffn_w_down=[recv["f0"][1]] + [recv[f"m{i}"][3] for i in (1, 2, 3)],
        cm_w_in=[recv["m0"][0], recv["m3"][0]], cm_w_out=[recv["m0"][1], recv["m3"][1]],
        attn_w_qkv=[recv["m1"][0]], attn_w_o=[recv["m1"][1]], gm_w_in=[recv["m2"][0]], gm_w_out=[recv["m2"][1]])
    for n, parts in big_parts.items():
        shp = W[n].shape
        res = _adamw(parts, shard3(W[n]), shard3(M[n]), shard3(V[n]), "adamw_" + n)
        put(n, [r.reshape(shp) for r in res])

    small_local = lambda d_: _pack_rows([d_[n] for n, _ in SMALL])[None]
    res = _adamw(small_recv, small_local(W), small_local(M), small_local(V), "adamw_small")
    unp = [_unpack_rows(r[0], small_shapes) for r in res]
    for q, (n, _) in enumerate(SMALL):
        put(n, [unp[t][q] for t in range(4)])

    repl_names = ["c_ctx", "ffn_conv_b", "attn_sink", "gm_b_s", "gm_w_s"]
    repl_part = dict(
        c_ctx=dcc.reshape(D),
        ffn_conv_b=jnp.stack([G[f"f{i}"]["cb"].reshape(2 * 2816) for i in range(DEPTH)]),
        attn_sink=G["m1"]["sink"].reshape(1, N_Q),
        gm_b_s=G["m2"]["b_s"].reshape(1, GM_GROUPS, GM_CHUNK),
        gm_w_s=G["m2"]["w_s"].reshape(1, GM_GROUPS, GM_CHUNK, GM_CHUNK))
    repl_shapes = [W[n].shape for n in repl_names]
    repl_sent = _pack_rows([repl_part[n] for n in repl_names] + [loss_part.reshape(1)], mult=256)
    repl_g = _all_gather([repl_sent], "gather_repl")[0]
    loss = jnp.sum(repl_g.reshape(NDEV, -1)[:, sum(W[n].size for n in repl_names)])
    repl_local = lambda d_: _pack_rows([d_[n] for n in repl_names], mult=256)[None]
    res = _adamw(repl_g[None], repl_local(W), repl_local(M), repl_local(V), "adamw_repl")
    unp = [_unpack_rows(r[0], repl_shapes) for r in res]
    for q, n in enumerate(repl_names):
        put(n, [unp[t][q] for t in range(4)])

    put("ada_w", _adamw(g_ada_w[:, None], ada_w, m_ada_w, v_ada_w, "adamw_ada_w"))
    ada_b_parts = dmod_g.reshape(1, 2 * NDEV, DEPTH, 6 * D)
    res = _adamw(ada_b_parts, ada_b[None], m_ada_b[None], v_ada_b[None], "adamw_ada_b")
    put("ada_b", [r[0] for r in res])

    names = ["c_ctx", "ada_w", "ada_b", "norm_g", "ffn_w_up", "ffn_conv_w", "ffn_conv_b", "ffn_w_down", "cm_w_in",
             "cm_b_in", "cm_dw_w", "cm_dw_b", "cm_ln_g", "cm_ln_b", "cm_w_out", "cm_b_out", "attn_w_qkv",
             "attn_sink", "attn_w_o", "gm_w_in", "gm_b_in", "gm_ln_g", "gm_ln_b", "gm_w_s", "gm_b_s", "gm_w_out"]
    return (loss, grad_x, *[out[n][0] for n in names], *[out[n][1] for n in names],
            *[out[n][2] for n in names], *[out[n][3] for n in names])
```

```python
import functools

import jax
import jax.numpy as jnp
from jax import lax
from jax.experimental import pallas as pl
from jax.experimental.pallas import tpu as pltpu

F32, BF16 = jnp.float32, jnp.bfloat16
MESH = pl.DeviceIdType.MESH
AXES = ("x", "y", "c")
NDEV = 8

D = 1024
L = 2048
LC = 256
TA = L + LC
DEPTH = 4
EPS = 1e-6
HEAD_DIM = 64
N_Q, N_KV, Q_PER_KV = 16, 4, 4
ATTN_BLOCK = 128
GRID_W = 64
ROPE_BASE = 10000.0
GM_W = 2048
GM_CHUNK = 128
GM_GROUPS = 16
FFN_BLK = 704
CM_K, FFN_K = 31, 3

ADAM_LR, ADAM_B1, ADAM_B2, ADAM_EPS, ADAM_WD, ADAM_STEP = 0.001, 0.9, 0.999, 1e-08, 0.01, 10

VMEM_LIMIT_V7X = 56 * 1024 * 1024
ROW_TILE_ELEMS = 256 * 1024
MM_TILE_BYTES = 4 * 1024 * 1024


def _cparams(sem=None):
    kw = dict(vmem_limit_bytes=VMEM_LIMIT_V7X)
    if sem is not None:
        kw["dimension_semantics"] = sem
    return pltpu.CompilerParams(**kw)


def _pick(n, cands):
    for c in cands:
        if n % c == 0:
            return c
    return n


def _as3(a):
    return a if a.ndim == 3 else a[None]


def _all_gather(arrs, name):
    n = len(arrs)

    def body(*refs):
        ins, outs = refs[:n], refs[n:2 * n]
        send_sems, recv_sems, local_sems = refs[2 * n:]
        x, y, c = lax.axis_index("x"), lax.axis_index("y"), lax.axis_index("c")
        me, sibling = (x, y, c), (x, y, 1 - c)
        chips = [(1 - x, y), (x, 1 - y), (1 - x, 1 - y)]

        def slot(a, p):
            return outs[a].at[4 * p[0] + 2 * p[1] + p[2]]

        def copy(a, k, block, to, src=None):
            return pltpu.make_async_remote_copy(
                src_ref=slot(a, block) if src is None else src, dst_ref=slot(a, block),
                send_sem=send_sems.at[a, k], recv_sem=recv_sems.at[a, k],
                device_id=to, device_id_type=MESH)

        mine = [pltpu.make_async_copy(ins[a], slot(a, me), local_sems.at[a]) for a in range(n)]
        for m in mine:
            m.start()
        first = []
        for a in range(n):
            first.append(copy(a, 0, me, sibling, src=ins[a]))
            first += [copy(a, 1 + j, me, (*chip, c), src=ins[a]) for j, chip in enumerate(chips)]
        for cp in first:
            cp.start()
        passed = []
        for j, chip in enumerate(chips):
            for a in range(n):
                copy(a, 1 + j, (*chip, c), me).wait_recv()
                p = copy(a, 4 + j, (*chip, c), sibling)
                p.start()
                passed.append(p)
        for a in range(n):
            copy(a, 0, sibling, me).wait_recv()
            for j, chip in enumerate(chips):
                copy(a, 4 + j, (*chip, 1 - c), me).wait_recv()
        for cp in first + passed:
            cp.wait_send()
        for m in mine:
            m.wait()

    any_spec = pl.BlockSpec(memory_space=pl.ANY)
    outs = pl.pallas_call(
        body, name=name,
        out_shape=[jax.ShapeDtypeStruct((NDEV,) + a.shape, a.dtype) for a in arrs],
        in_specs=[any_spec] * n, out_specs=[any_spec] * n,
        scratch_shapes=[pltpu.SemaphoreType.DMA((n, 7)), pltpu.SemaphoreType.DMA((n, 7)),
                        pltpu.SemaphoreType.DMA((n,))],
    )(*arrs)
    return list(outs)


def _all_to_all(groups, name):
    flat = [(gi, li, a) for gi, g in enumerate(groups) for li, a in enumerate(g)]
    n, ng = len(flat), len(groups)

    def body(*refs):
        ins, outs = refs[:n], refs[n:n + ng]
        send_sems, recv_sems, local_sems = refs[n + ng:]
        x, y, c = lax.axis_index("x"), lax.axis_index("y"), lax.axis_index("c")
        me = 4 * x + 2 * y + c
        copies = []
        for a, (gi, li, _) in enumerate(flat):
            loc = pltpu.make_async_copy(ins[a].at[me], outs[gi].at[li, me], local_sems.at[a])
            loc.start()
            copies.append(loc)
            for k in range(1, NDEV):
                px = 1 - x if (k >> 2) & 1 else x
                py = 1 - y if (k >> 1) & 1 else y
                pc = 1 - c if k & 1 else c
                cp = pltpu.make_async_remote_copy(
                    src_ref=ins[a].at[4 * px + 2 * py + pc], dst_ref=outs[gi].at[li, me],
                    send_sem=send_sems.at[a, k - 1], recv_sem=recv_sems.at[a, k - 1],
                    device_id=(px, py, pc), device_id_type=MESH)
                cp.start()
                copies.append(cp)
        for cp in copies:
            cp.wait()

    any_spec = pl.BlockSpec(memory_space=pl.ANY)
    outs = pl.pallas_call(
        body, name=name,
        out_shape=[jax.ShapeDtypeStruct((len(g),) + g[0].shape, g[0].dtype) for g in groups],
        in_specs=[any_spec] * n, out_specs=[any_spec] * ng,
        scratch_shapes=[pltpu.SemaphoreType.DMA((n, 7)), pltpu.SemaphoreType.DMA((n, 7)),
                        pltpu.SemaphoreType.DMA((n,))],
    )(*[a for _, _, a in flat])
    return list(outs)


HBM_SPEC = pl.BlockSpec(memory_space=pltpu.HBM)
SEM_SPEC = pl.BlockSpec(memory_space=pltpu.SEMAPHORE)
ANY_SPEC = pl.BlockSpec(memory_space=pl.ANY)
SPLIT_EFFECT = pltpu.SideEffectType.DATAFLOW_SIDE_EFFECTING


def _remote_copies(kind, ins, lands, send_sems, recv_sems):
    x, y, c = lax.axis_index("x"), lax.axis_index("y"), lax.axis_index("c")
    me = 4 * x + 2 * y + c
    out = []
    for a in range(len(ins)):
        for k in range(1, NDEV):
            px = 1 - x if (k >> 2) & 1 else x
            py = 1 - y if (k >> 1) & 1 else y
            pc = 1 - c if k & 1 else c
            src = ins[a] if kind == "gather" else ins[a].at[4 * px + 2 * py + pc]
            out.append(pltpu.make_async_remote_copy(
                src_ref=src, dst_ref=lands[a].at[me], send_sem=send_sems.at[a * (NDEV - 1) + k - 1],
                recv_sem=recv_sems.at[a * (NDEV - 1) + k - 1], device_id=(px, py, pc), device_id_type=MESH))
    return out


def _local_copies(kind, ins, lands, local_sems):
    me = 4 * lax.axis_index("x") + 2 * lax.axis_index("y") + lax.axis_index("c")
    return [pltpu.make_async_copy(ins[a] if kind == "gather" else ins[a].at[me], lands[a].at[me], local_sems.at[a])
            for a in range(len(ins))]


def _xfer_start(kind, arrs, after, name):
    n = len(arrs)
    lands = [lax.empty((NDEV,) + a.shape if kind == "gather" else a.shape, a.dtype) for a in arrs]

    def body(*refs):
        ins, lnd = refs[:n], refs[n:2 * n]
        send_sems, recv_sems, local_sems = refs[2 * n + 1:2 * n + 4]
        for cp in _remote_copies(kind, ins, lnd, send_sems, recv_sems) + _local_copies(kind, ins, lnd, local_sems):
            cp.start()
        refs[-1][...] = jnp.zeros_like(refs[-1])

    outs = pl.pallas_call(
        body, name=name,
        out_shape=(pltpu.SemaphoreType.DMA((n * (NDEV - 1),)), pltpu.SemaphoreType.DMA((n * (NDEV - 1),)),
                   pltpu.SemaphoreType.DMA((n,)),
                   *[pltpu.HBM(a.shape, a.dtype) for a in arrs + lands], jax.ShapeDtypeStruct((8, 128), F32)),
        in_specs=[HBM_SPEC] * (2 * n) + [ANY_SPEC],
        out_specs=(SEM_SPEC, SEM_SPEC, SEM_SPEC, *[HBM_SPEC] * (2 * n), pl.BlockSpec(memory_space=pltpu.VMEM)),
        input_output_aliases={a: 3 + a for a in range(2 * n)},
        compiler_params=pltpu.CompilerParams(has_side_effects=SPLIT_EFFECT),
    )(*[pltpu.with_memory_space_constraint(a, pltpu.HBM) for a in arrs + lands], after)
    return dict(kind=kind, n=n, sems=list(outs[:3]), bufs=list(outs[3:3 + 2 * n]), token=outs[-1], name=name)


def _xfer_wait(st, after):
    kind, n = st["kind"], st["n"]

    def body(*refs):
        ins, lnd = refs[:n], refs[n:2 * n]
        send_sems, recv_sems, local_sems = refs[2 * n:2 * n + 3]
        for cp in _remote_copies(kind, ins, lnd, send_sems, recv_sems):
            cp.wait_send()
            cp.wait_recv()
        for cp in _local_copies(kind, ins, lnd, local_sems):
            cp.wait()

    outs = pl.pallas_call(
        body, name=st["name"] + "_wait",
        out_shape=tuple(pltpu.HBM(b.shape, b.dtype) for b in st["bufs"]),
        in_specs=[HBM_SPEC] * (2 * n) + [SEM_SPEC] * 3 + [ANY_SPEC],
        out_specs=tuple([HBM_SPEC] * (2 * n)),
        input_output_aliases={a: a for a in range(2 * n)},
        compiler_params=pltpu.CompilerParams(has_side_effects=SPLIT_EFFECT),
    )(*st["bufs"], *st["sems"], after)
    return list(outs[n:])


def _mm(a, b, kind, *, name, out_dtype=F32, reduce_blocks=False):
    a, b = _as3(a), _as3(b)
    nba, nbb = a.shape[0], b.shape[0]
    nb = max(nba, nbb)
    assert nba in (1, nb) and nbb in (1, nb)
    if kind == "tn":
        t, m = a.shape[1:]
        n = b.shape[2]
        assert b.shape[1] == t and not reduce_blocks
        tm = m if m <= 1024 else _pick(m, (1024,))
        tn = n if n <= 1024 else _pick(n, (1024, 768, 512))
        tk = t if t * (tm + tn) * 2 <= MM_TILE_BYTES * 3 else _pick(t, (512, 768, 256))
        nred = t // tk
    else:
        m, k = a.shape[1:]
        n = b.shape[2] if kind == "nn" else b.shape[1]
        assert (b.shape[1] if kind == "nn" else b.shape[2]) == k
        tn = n if (n <= 1024 or k * n * 2 <= 2 * MM_TILE_BYTES) else _pick(n, (1024, 768, 512))
        tm = m
        for cand in (1024, 768, 512, 256):
            if m % cand == 0 and cand * tn * 4 <= MM_TILE_BYTES and cand * k * 2 <= MM_TILE_BYTES:
                tm = cand
                break
        nred = nb if reduce_blocks else 1
    nbo = 1 if reduce_blocks else nb

    def blk(nbx, g, r):
        if nbx == 1:
            return 0
        return r if reduce_blocks else g

    if kind == "nn":
        a_spec = pl.BlockSpec((1, tm, k), lambda g, j, i, r: (blk(nba, g, r), i, 0))
        b_spec = pl.BlockSpec((1, k, tn), lambda g, j, i, r: (blk(nbb, g, r), 0, j))
        dims = (((1,), (0,)), ((), ()))
    elif kind == "nt":
        a_spec = pl.BlockSpec((1, tm, k), lambda g, j, i, r: (blk(nba, g, r), i, 0))
        b_spec = pl.BlockSpec((1, tn, k), lambda g, j, i, r: (blk(nbb, g, r), j, 0))
        dims = (((1,), (1,)), ((), ()))
    else:
        a_spec = pl.BlockSpec((1, tk, tm), lambda g, j, i, r: (blk(nba, g, r), r, i))
        b_spec = pl.BlockSpec((1, tk, tn), lambda g, j, i, r: (blk(nbb, g, r), r, j))
        dims = (((0,), (0,)), ((), ()))
    o_spec = pl.BlockSpec((1, tm, tn), lambda g, j, i, r: (g, i, j))

    def body(a_ref, b_ref, o_ref, *scratch):
        prod = lax.dot_general(a_ref[0].astype(BF16), b_ref[0].astype(BF16), dims,
                               preferred_element_type=F32)
        if nred == 1:
            o_ref[0] = prod.astype(o_ref.dtype)
        else:
            acc = scratch[0]
            r = pl.program_id(3)

            @pl.when(r == 0)
            def _():
                acc[...] = prod

            @pl.when(r > 0)
            def _():
                acc[...] += prod

            @pl.when(r == nred - 1)
            def _():
                o_ref[0] = acc[...].astype(o_ref.dtype)

    return pl.pallas_call(
        body, name=name,
        out_shape=jax.ShapeDtypeStruct((nbo, m, n), out_dtype),
        grid=(nbo, n // tn, m // tm, nred),
        in_specs=[a_spec, b_spec], out_specs=o_spec,
        scratch_shapes=[pltpu.VMEM((tm, tn), F32)] if nred > 1 else [],
        compiler_params=_cparams(("parallel", "parallel", "parallel", "arbitrary")),
    )(a, b)


def _row_tile(t, widths):
    tm = max(16, ROW_TILE_ELEMS // max(widths))
    tm = min(tm, 256)
    return t if t < tm else tm


def _sel_index(sel, g, i, tm):
    if sel == "one":
        return 0
    if sel == "seg":
        return (i * tm) // L
    return g + sel


def _row_spec(arr, off, tm):
    return pl.BlockSpec((1, tm, arr.shape[2]), lambda g, i: (g + off, i, 0))


def _par_spec(arr, sel, tm):
    return pl.BlockSpec((1, 1, arr.shape[2]), lambda g, i: (_sel_index(sel, g, i, tm), 0, 0))


def _norm_ops(ops):
    return [(o[0], o[1], o[2] if len(o) > 2 else 1) for o in ops]


def _split_cols(vals, nsplit):
    out = []
    for v, ns in zip(vals, nsplit):
        w = v.shape[1] // ns
        out += [v] if ns == 1 else [v[:, q * w:(q + 1) * w] for q in range(ns)]
    return out


def _join_cols(flat, nsplit):
    out, pos = [], 0
    for ns in nsplit:
        out.append(flat[pos] if ns == 1 else jnp.concatenate(flat[pos:pos + ns], axis=1))
        pos += ns
    return out


def _rw_fwd(fn, rows, params, outs, *, name, nblk=None):
    rows, params = _norm_ops(rows), _norm_ops(params)
    t = rows[0][0].shape[1]
    nblk = nblk or rows[0][0].shape[0]
    tm = _row_tile(t, [r.shape[2] for r, _, _ in rows] + [w for w, _ in outs])
    nr, npar = len(rows), len(params)
    nsplit = [ns for _, _, ns in rows + params]

    def body(*refs):
        vals = _split_cols([r[0].astype(F32) for r in refs[:nr + npar]], nsplit)
        res = fn(*vals)
        for o_ref, o in zip(refs[nr + npar:], res):
            o_ref[0] = o.astype(o_ref.dtype)

    res = pl.pallas_call(
        body, name=name,
        out_shape=[jax.ShapeDtypeStruct((nblk, t, w), dt) for w, dt in outs],
        grid=(nblk, t // tm),
        in_specs=[_row_spec(r, off, tm) for r, off, _ in rows] + [_par_spec(p, s, tm) for p, s, _ in params],
        out_specs=[pl.BlockSpec((1, tm, w), lambda g, i: (g, i, 0)) for w, _ in outs],
        compiler_params=_cparams(("parallel", "parallel")),
    )(*[r for r, _, _ in rows], *[p for p, _, _ in params])
    return list(res)


def _rw_bwd(fn, rows, params, cts, *, name, row_grad=(), param_grad=(), add=None, nblk=None):
    rows, params = _norm_ops(rows), _norm_ops(params)
    t = cts[0].shape[1]
    nblk = nblk or cts[0].shape[0]
    tm = _row_tile(t, [r.shape[2] for r, _, _ in rows] + [c.shape[2] for c in cts])
    ni = t // tm
    nr, npar, nct = len(rows), len(params), len(cts)
    nadd = 0 if add is None else 1
    n_in = nr + npar + nct + nadd
    nsplit = [ns for _, _, ns in rows + params]

    def body(*refs):
        prim = _split_cols([r[0].astype(F32) for r in refs[:nr + npar]], nsplit)
        ct = tuple(r[0].astype(F32) for r in refs[nr + npar:nr + npar + nct])
        _, vjp = jax.vjp(fn, *prim)
        grads = _join_cols(list(vjp(ct)), nsplit)
        out_refs = refs[n_in:]
        for q, (ri, _) in enumerate(row_grad):
            gr = grads[ri]
            if q == 0 and nadd:
                gr = gr + refs[n_in - 1][0].astype(F32)
            out_refs[q][0] = gr.astype(out_refs[q].dtype)
        g, i = pl.program_id(0), pl.program_id(1)
        step = g * ni + i
        pg, pi = (step - 1) // ni, (step - 1) % ni
        for q, pidx in enumerate(param_grad):
            o_ref = out_refs[len(row_grad) + q]
            sel = params[pidx][1]
            val = grads[nr + pidx]
            if sel == "one":
                first = step == 0
            else:
                first = (step == 0) | (_sel_index(sel, g, i, tm) != _sel_index(sel, pg, pi, tm))

            @pl.when(first)
            def _(o_ref=o_ref, val=val):
                o_ref[0] = val

            @pl.when(jnp.logical_not(first))
            def _(o_ref=o_ref, val=val):
                o_ref[0] += val

    in_arrays = [r for r, _, _ in rows] + [p for p, _, _ in params] + list(cts) + ([add] if nadd else [])
    in_specs = ([_row_spec(r, off, tm) for r, off, _ in rows] + [_par_spec(p, s, tm) for p, s, _ in params]
                + [_row_spec(c, 0, tm) for c in cts] + ([_row_spec(add, 0, tm)] if nadd else []))
    out_shape, out_specs = [], []
    for ri, dt in row_grad:
        w = rows[ri][0].shape[2]
        out_shape.append(jax.ShapeDtypeStruct((nblk, t, w), dt))
        out_specs.append(pl.BlockSpec((1, tm, w), lambda g, i: (g, i, 0)))
    for pidx in param_grad:
        p, sel, _ = params[pidx]
        out_shape.append(jax.ShapeDtypeStruct(p.shape, F32))
        out_specs.append(_par_spec(p, sel, tm))
    res = pl.pallas_call(
        body, name=name, out_shape=out_shape, grid=(nblk, ni),
        in_specs=in_specs, out_specs=out_specs,
        compiler_params=_cparams(("arbitrary", "arbitrary")),
    )(*in_arrays)
    return list(res)


def _sigmoid(x):
    return 1.0 / (1.0 + jnp.exp(-x))


def _rms(x, g):
    return x * lax.rsqrt(jnp.mean(x * x, axis=-1, keepdims=True) + EPS) * g


def _ln(x, g, b):
    mu = jnp.mean(x, axis=-1, keepdims=True)
    xc = x - mu
    var = jnp.mean(xc * xc, axis=-1, keepdims=True)
    return xc * lax.rsqrt(var + EPS) * g + b


def _gelu_tanh(x):
    return 0.5 * x * (1.0 + jnp.tanh(0.7978845608028654 * (x + 0.044715 * (x * x * x))))


def f_modnorm(h, g, sc, sh):
    return (_rms(h, g) * (1.0 + sc) + sh,)


def f_gate_rms(y, gate, g):
    return (gate * _rms(y, g),)


def f_gate_rms_bias(y, gate, g, b):
    return (gate * _rms(y + b, g),)


def f_resgate(h, y, gate, g):
    return (h + gate * _rms(y, g),)


def f_resgate_bias(h, y, gate, g, b):
    return (h + gate * _rms(y + b, g),)


def f_glu(pa, pg, ba, bg):
    return ((pa + ba) * _sigmoid(pg + bg),)


def f_lnsilu(z, g, b):
    t = _ln(z, g, b)
    return (t * _sigmoid(t),)


def f_gmlp_pre(pu, pv, bu, bv, g, bb):
    return _gelu_tanh(pu + bu), _ln(_gelu_tanh(pv + bv), g, bb)


def f_ffn_gate(zg, zv):
    return (zg * _sigmoid(zg) * zv,)


def f_silu(x):
    return (x * _sigmoid(x),)


def f_silu_rows(dummy, cc):
    return (cc * _sigmoid(cc) + 0.0 * dummy,)


def _rope(x_in, tables, neg_sin, out_dtype, name):
    w = x_in.shape[2]
    sign = -1.0 if neg_sin else 1.0

    def fn(x, cos, sin):
        cos = jnp.tile(cos, (1, w // 128))
        sin = jnp.tile(sin, (1, w // 128)) * sign
        lane = lax.broadcasted_iota(jnp.int32, x.shape, 1) & 31
        rot = jnp.where(lane < 16, -pltpu.roll(x, w - 16, 1), pltpu.roll(x, 16, 1))
        return (x * cos + rot * sin,)

    return _rw_fwd(fn, [(x_in, 0), (tables[0], 0), (tables[1], 0)], [], [(w, out_dtype)], name=name)[0]


CONV_TM = 256
CONV_RC = 32


def _conv_geometry(x, k):
    nb, t, w = x.shape
    halo = 16 if k > 17 else 8
    cb = _pick(w, (512,)) if w > 768 else w
    return nb, t, w, halo, cb, (k - 1) // 2


def _conv_in_specs(t, halo, cb):
    per = CONV_TM // halo
    last = t // halo - 1
    return [
        pl.BlockSpec((1, CONV_TM, cb), lambda g, jc, i: (g, i, jc)),
        pl.BlockSpec((1, halo, cb), lambda g, jc, i: (g, jnp.maximum(i * per - 1, 0), jc)),
        pl.BlockSpec((1, halo, cb), lambda g, jc, i: (g, jnp.minimum((i + 1) * per, last), jc)),
    ]


def _conv_fill(xp, x_ref, prev_ref, next_ref, halo, t):
    i = pl.program_id(2)
    seg_first = (i * CONV_TM == 0) | (i * CONV_TM == L)
    seg_last = ((i + 1) * CONV_TM == L) | ((i + 1) * CONV_TM == t)
    xp[0:halo, :] = jnp.where(seg_first, 0.0, prev_ref[0].astype(F32))
    xp[halo:halo + CONV_TM, :] = x_ref[0].astype(F32)
    xp[halo + CONV_TM:, :] = jnp.where(seg_last, 0.0, next_ref[0].astype(F32))


def _dwconv(x, w, b, *, name, out_dtype=F32):
    k = w.shape[1]
    nb, t, wd, halo, cb, half = _conv_geometry(x, k)
    base = halo - half

    def body(*refs):
        x_ref, prev_ref, next_ref, w_ref = refs[:4]
        b_ref = refs[4] if b is not None else None
        o_ref, xp = refs[-2], refs[-1]
        _conv_fill(xp, x_ref, prev_ref, next_ref, halo, t)
        for r0 in range(0, CONV_TM, CONV_RC):
            acc = jnp.zeros((CONV_RC, cb), F32)
            for kk in range(k):
                acc = acc + w_ref[0, kk:kk + 1, :] * xp[r0 + base + kk:r0 + base + kk + CONV_RC, :]
            if b_ref is not None:
                acc = acc + b_ref[0]
            o_ref[0, r0:r0 + CONV_RC, :] = acc.astype(o_ref.dtype)

    in_specs = _conv_in_specs(t, halo, cb) + [pl.BlockSpec((1, k, cb), lambda g, jc, i: (g, 0, jc))]
    args = [x, x, x, w]
    if b is not None:
        in_specs.append(pl.BlockSpec((1, 1, cb), lambda g, jc, i: (g, 0, jc)))
        args.append(b)
    return pl.pallas_call(
        body, name=name, out_shape=jax.ShapeDtypeStruct((nb, t, wd), out_dtype),
        grid=(nb, wd // cb, t // CONV_TM), in_specs=in_specs,
        out_specs=pl.BlockSpec((1, CONV_TM, cb), lambda g, jc, i: (g, i, jc)),
        scratch_shapes=[pltpu.VMEM((CONV_TM + 2 * halo, cb), F32)],
        compiler_params=_cparams(("parallel", "parallel", "parallel")),
    )(*args)


def _dwconv_wgrad(x, dy, k, *, name):
    nb, t, wd, halo, cb, half = _conv_geometry(x, k)
    base = halo - half

    def body(x_ref, prev_ref, next_ref, dy_ref, dw_ref, db_ref, xp):
        _conv_fill(xp, x_ref, prev_ref, next_ref, halo, t)
        i = pl.program_id(2)

        @pl.when(i == 0)
        def _():
            dw_ref[...] = jnp.zeros_like(dw_ref)
            db_ref[...] = jnp.zeros_like(db_ref)

        dyv = dy_ref[0].astype(F32)
        db_ref[0] += jnp.sum(dyv, axis=0, keepdims=True)
        for kk in range(k):
            dw_ref[0, kk:kk + 1, :] += jnp.sum(dyv * xp[base + kk:base + kk + CONV_TM, :], axis=0, keepdims=True)

    dw, db = pl.pallas_call(
        body, name=name,
        out_shape=[jax.ShapeDtypeStruct((nb, k, wd), F32), jax.ShapeDtypeStruct((nb, 1, wd), F32)],
        grid=(nb, wd // cb, t // CONV_TM),
        in_specs=_conv_in_specs(t, halo, cb) + [pl.BlockSpec((1, CONV_TM, cb), lambda g, jc, i: (g, i, jc))],
        out_specs=[pl.BlockSpec((1, k, cb), lambda g, jc, i: (g, 0, jc)),
                   pl.BlockSpec((1, 1, cb), lambda g, jc, i: (g, 0, jc))],
        scratch_shapes=[pltpu.VMEM((CONV_TM + 2 * halo, cb), F32)],
        compiler_params=_cparams(("parallel", "parallel", "arbitrary")),
    )(x, x, x, dy)
    return dw, db


ATTN_SCALE = HEAD_DIM ** -0.5
QROWS = Q_PER_KV * ATTN_BLOCK
NEG = -1e30


def _attn_scores(q, kw, kc, n):
    nt = (((1,), (1,)), ((), ()))
    s_w = lax.dot_general(q, kw, nt, preferred_element_type=F32) * ATTN_SCALE
    qi = lax.broadcasted_iota(jnp.int32, s_w.shape, 0) & (ATTN_BLOCK - 1)
    kj = lax.broadcasted_iota(jnp.int32, s_w.shape, 1)
    key_abs = (n - 1) * ATTN_BLOCK + kj
    ok = (jnp.abs(qi + ATTN_BLOCK - kj) <= ATTN_BLOCK) & (key_abs >= 0) & (key_abs < L)
    s_w = jnp.where(ok, s_w, NEG)
    s_c = lax.dot_general(q, kc, nt, preferred_element_type=F32) * ATTN_SCALE
    return s_w, s_c


def _sink_col(sink_ref, hk):
    return jnp.concatenate([jnp.full((ATTN_BLOCK, 1), sink_ref[hk * Q_PER_KV + g], F32) for g in range(Q_PER_KV)], axis=0)


def _attn_specs():
    qspec = pl.BlockSpec((Q_PER_KV, ATTN_BLOCK, HEAD_DIM), lambda hk, n: (hk, n, 0))
    kspec = pl.BlockSpec((1, L + 2 * ATTN_BLOCK, HEAD_DIM), lambda hk, n: (hk, 0, 0))
    cspec = pl.BlockSpec((1, LC, HEAD_DIM), lambda hk, n: (hk, 0, 0))
    lspec = pl.BlockSpec((Q_PER_KV, ATTN_BLOCK, 1), lambda hk, n: (hk, n, 0))
    sspec = pl.BlockSpec(memory_space=pltpu.SMEM)
    return qspec, kspec, cspec, lspec, sspec


def _attn_fwd(q, k, v, kc, vc, sink):
    qspec, kspec, cspec, lspec, sspec = _attn_specs()

    def body(q_ref, k_ref, v_ref, kc_ref, vc_ref, sink_ref, o_ref, lse_ref):
        hk, n = pl.program_id(0), pl.program_id(1)
        qv = q_ref[...].reshape(QROWS, HEAD_DIM)
        start = pl.multiple_of(n * ATTN_BLOCK, ATTN_BLOCK)
        kw = k_ref[0, pl.ds(start, 3 * ATTN_BLOCK), :]
        vw = v_ref[0, pl.ds(start, 3 * ATTN_BLOCK), :]
        s_w, s_c = _attn_scores(qv, kw, kc_ref[0], n)
        sk = _sink_col(sink_ref, hk)
        m = jnp.maximum(jnp.maximum(jnp.max(s_w, -1, keepdims=True), jnp.max(s_c, -1, keepdims=True)), sk)
        p_w, p_c = jnp.exp(s_w - m), jnp.exp(s_c - m)
        den = jnp.sum(p_w, -1, keepdims=True) + jnp.sum(p_c, -1, keepdims=True) + jnp.exp(sk - m)
        o = (jnp.dot(p_w.astype(BF16), vw, preferred_element_type=F32)
             + jnp.dot(p_c.astype(BF16), vc_ref[0], preferred_element_type=F32)) / den
        o_ref[...] = o.reshape(Q_PER_KV, ATTN_BLOCK, HEAD_DIM).astype(o_ref.dtype)
        lse_ref[...] = (m + jnp.log(den)).reshape(Q_PER_KV, ATTN_BLOCK, 1)

    return pl.pallas_call(
        body, name="attn_fwd",
        out_shape=[jax.ShapeDtypeStruct((N_Q, L, HEAD_DIM), BF16), jax.ShapeDtypeStruct((N_Q, L, 1), F32)],
        grid=(N_KV, L // ATTN_BLOCK),
        in_specs=[qspec, kspec, kspec, cspec, cspec, sspec], out_specs=[qspec, lspec],
        compiler_params=_cparams(("parallel", "parallel")),
    )(q, k, v, kc, vc, sink)


def _attn_bwd(q, k, v, kc, vc, sink, o, lse, do):
    qspec, kspec, cspec, lspec, sspec = _attn_specs()
    tn = (((0,), (0,)), ((), ()))
    nt = (((1,), (1,)), ((), ()))

    def body(q_ref, k_ref, v_ref, kc_ref, vc_ref, sink_ref, o_ref, lse_ref, do_ref,
             dq_ref, dk_ref, dv_ref, dkc_ref, dvc_ref, dsink_ref):
        hk, n = pl.program_id(0), pl.program_id(1)
        qv = q_ref[...].reshape(QROWS, HEAD_DIM)
        start = pl.multiple_of(n * ATTN_BLOCK, ATTN_BLOCK)
        win = pl.ds(start, 3 * ATTN_BLOCK)
        kw, vw = k_ref[0, win, :], v_ref[0, win, :]
        kcv, vcv = kc_ref[0], vc_ref[0]
        s_w, s_c = _attn_scores(qv, kw, kcv, n)
        lse_v = lse_ref[...].reshape(QROWS, 1)
        p_w, p_c = jnp.exp(s_w - lse_v), jnp.exp(s_c - lse_v)
        dov = do_ref[...].reshape(QROWS, HEAD_DIM).astype(F32)
        ov = o_ref[...].reshape(QROWS, HEAD_DIM).astype(F32)
        delta = jnp.sum(dov * ov, -1, keepdims=True)
        dob = dov.astype(BF16)
        dp_w = lax.dot_general(dob, vw, nt, preferred_element_type=F32)
        dp_c = lax.dot_general(dob, vcv, nt, preferred_element_type=F32)
        ds_w = (p_w * (dp_w - delta) * ATTN_SCALE).astype(BF16)
        ds_c = (p_c * (dp_c - delta) * ATTN_SCALE).astype(BF16)
        dq = jnp.dot(ds_w, kw, preferred_element_type=F32) + jnp.dot(ds_c, kcv, preferred_element_type=F32)
        dq_ref[...] = dq.reshape(Q_PER_KV, ATTN_BLOCK, HEAD_DIM)

        @pl.when(n == 0)
        def _():
            dk_ref[...] = jnp.zeros_like(dk_ref)
            dv_ref[...] = jnp.zeros_like(dv_ref)
            dkc_ref[...] = jnp.zeros_like(dkc_ref)
            dvc_ref[...] = jnp.zeros_like(dvc_ref)

        dk_ref[0, win, :] += lax.dot_general(ds_w, qv, tn, preferred_element_type=F32)
        dv_ref[0, win, :] += lax.dot_general(p_w.astype(BF16), dob, tn, preferred_element_type=F32)
        dkc_ref[0] += lax.dot_general(ds_c, qv, tn, preferred_element_type=F32)
        dvc_ref[0] += lax.dot_general(p_c.astype(BF16), dob, tn, preferred_element_type=F32)
        dsk = -jnp.exp(_sink_col(sink_ref, hk) - lse_v) * delta
        for g in range(Q_PER_KV):
            part = jnp.sum(dsk[g * ATTN_BLOCK:(g + 1) * ATTN_BLOCK])
            idx = hk * Q_PER_KV + g

            @pl.when(n == 0)
            def _(part=part, idx=idx):
                dsink_ref[idx] = part

            @pl.when(n > 0)
            def _(part=part, idx=idx):
                dsink_ref[idx] += part

    kshape = jax.ShapeDtypeStruct((N_KV, L + 2 * ATTN_BLOCK, HEAD_DIM), F32)
    cshape = jax.ShapeDtypeStruct((N_KV, LC, HEAD_DIM), F32)
    return pl.pallas_call(
        body, name="attn_bwd",
        out_shape=[jax.ShapeDtypeStruct((N_Q, L, HEAD_DIM), F32), kshape, kshape, cshape, cshape,
                   jax.ShapeDtypeStruct((N_Q,), F32)],
        grid=(N_KV, L // ATTN_BLOCK),
        in_specs=[qspec, kspec, kspec, cspec, cspec, sspec, qspec, lspec, qspec],
        out_specs=[qspec, kspec, kspec, cspec, cspec, sspec],
        compiler_params=_cparams(("arbitrary", "arbitrary")),
    )(q, k, v, kc, vc, sink, o, lse, do)


def _gm_specs():
    rspec = pl.BlockSpec((1, GM_CHUNK, GM_W), lambda n: (0, n, 0))
    wspec = pl.BlockSpec((GM_GROUPS, GM_CHUNK, GM_CHUNK), lambda n: (0, 0, 0))
    bspec = pl.BlockSpec((GM_GROUPS, GM_CHUNK, 1), lambda n: (0, 0, 0))
    return rspec, wspec, bspec


def _gm_spatial_fwd(u, v, ws, bs):
    rspec, wspec, bspec = _gm_specs()

    def body(u_ref, v_ref, ws_ref, bs_ref, o_ref):
        for g in range(GM_GROUPS):
            cols = slice(g * GM_CHUNK, (g + 1) * GM_CHUNK)
            s = jnp.dot(ws_ref[g], v_ref[0, :, cols], preferred_element_type=F32) + bs_ref[g]
            o_ref[0, :, cols] = (u_ref[0, :, cols] * s).astype(o_ref.dtype)

    return pl.pallas_call(
        body, name="gm_spatial_fwd", out_shape=jax.ShapeDtypeStruct((1, L, GM_W), BF16),
        grid=(L // GM_CHUNK,), in_specs=[rspec, rspec, wspec, bspec], out_specs=rspec,
        compiler_params=_cparams(("parallel",)),
    )(u, v, ws, bs)


def _gm_spatial_bwd(u, v, ws, bs, dus):
    rspec, wspec, bspec = _gm_specs()
    tn = (((0,), (0,)), ((), ()))
    nt = (((1,), (1,)), ((), ()))

    def body(u_ref, v_ref, ws_ref, bs_ref, d_ref, du_ref, dv_ref, dws_ref, dbs_ref):
        n = pl.program_id(0)

        @pl.when(n == 0)
        def _():
            dws_ref[...] = jnp.zeros_like(dws_ref)
            dbs_ref[...] = jnp.zeros_like(dbs_ref)

        for g in range(GM_GROUPS):
            cols = slice(g * GM_CHUNK, (g + 1) * GM_CHUNK)
            vb = v_ref[0, :, cols]
            s = jnp.dot(ws_ref[g], vb, preferred_element_type=F32) + bs_ref[g]
            d = d_ref[0, :, cols].astype(F32)
            du_ref[0, :, cols] = d * s
            ds = d * u_ref[0, :, cols]
            dsb = ds.astype(BF16)
            dv_ref[0, :, cols] = lax.dot_general(ws_ref[g], dsb, tn, preferred_element_type=F32)
            dws_ref[g] += lax.dot_general(dsb, vb, nt, preferred_element_type=F32)
            dbs_ref[g] += jnp.sum(ds, axis=1, keepdims=True)

    row = jax.ShapeDtypeStruct((1, L, GM_W), F32)
    return pl.pallas_call(
        body, name="gm_spatial_bwd",
        out_shape=[row, row, jax.ShapeDtypeStruct((GM_GROUPS, GM_CHUNK, GM_CHUNK), F32),
                   jax.ShapeDtypeStruct((GM_GROUPS, GM_CHUNK, 1), F32)],
        grid=(L // GM_CHUNK,), in_specs=[rspec, rspec, wspec, bspec, rspec],
        out_specs=[rspec, rspec, wspec, bspec],
        compiler_params=_cparams(("arbitrary",)),
    )(u, v, ws, bs, dus)


def _loss_head(h, target):
    tm = 256

    def body(h_ref, t_ref, dh_ref, loss_ref):
        d = h_ref[0] - t_ref[0]
        dh_ref[0] = d * (1.0 / D)

        @pl.when(pl.program_id(0) == 0)
        def _():
            loss_ref[...] = jnp.zeros_like(loss_ref)

        loss_ref[...] += jnp.sum(d * d) * (0.5 / D)

    spec = pl.BlockSpec((1, tm, D), lambda i: (0, i, 0))
    dh, loss = pl.pallas_call(
        body, name="loss_head",
        out_shape=[jax.ShapeDtypeStruct((1, L, D), F32), jax.ShapeDtypeStruct((8, 128), F32)],
        grid=(L // tm,), in_specs=[spec, spec],
        out_specs=[spec, pl.BlockSpec((8, 128), lambda i: (0, 0))],
        compiler_params=_cparams(("arbitrary",)),
    )(h, target)
    return dh, loss[0, 0]


def _adamw(parts, w, m, v, name):
    per_layer = isinstance(parts, (list, tuple))
    plist = list(parts) if per_layer else [parts]
    nl = len(plist) if per_layer else parts.shape[0]
    s, r, c = plist[0].shape[-3:]
    tr = r
    for cand in (512, 256, 128, 64, 32, 16):
        if r % cand == 0 and cand * c <= 131072:
            tr = cand
            break
    nr = r // tr
    npart = len(plist)
    c1 = 1.0 / (1.0 - ADAM_B1 ** ADAM_STEP)
    c2 = 1.0 / (1.0 - ADAM_B2 ** ADAM_STEP)

    def body(*refs):
        w_ref, m_ref, v_ref, g_ref, d_ref, nm_ref, nv_ref = refs[npart:]

        def update(read):
            g = read(0).astype(F32)
            for q in range(1, s):
                g = g + read(q).astype(F32)
            mn = ADAM_B1 * m_ref[0] + (1.0 - ADAM_B1) * g
            vn = ADAM_B2 * v_ref[0] + (1.0 - ADAM_B2) * (g * g)
            g_ref[0] = g
            nm_ref[0] = mn
            nv_ref[0] = vn
            d_ref[0] = -ADAM_LR * ((mn * c1) / (jnp.sqrt(vn * c2) + ADAM_EPS) + ADAM_WD * w_ref[0])

        if not per_layer:
            update(lambda q: refs[0][0, q])
        else:
            for l in range(nl):
                @pl.when(pl.program_id(0) == l)
                def _(l=l):
                    update(lambda q: refs[l][q])

    spec = pl.BlockSpec((1, tr, c), lambda li, i: (li, i, 0))
    shp = jax.ShapeDtypeStruct((nl, r, c), F32)
    if per_layer:
        pspecs = [pl.BlockSpec((s, tr, c), lambda li, i, l=l: (0, jnp.where(li == l, i, jnp.where(li > l, nr - 1, 0)), 0))
                  for l in range(nl)]
    else:
        pspecs = [pl.BlockSpec((1, s, tr, c), lambda li, i: (li, 0, i, 0))]
    return pl.pallas_call(
        body, name=name, out_shape=[shp] * 4, grid=(nl, nr),
        in_specs=pspecs + [spec, spec, spec], out_specs=[spec] * 4,
        compiler_params=_cparams(("arbitrary", "arbitrary")),
    )(*plist, w, m, v)


def _pack_rows(vecs, lanes=128, mult=8):
    flat = jnp.concatenate([v.reshape(-1) for v in vecs])
    n = flat.shape[0]
    rows = -(-n // (mult * lanes)) * mult
    return jnp.pad(flat, (0, rows * lanes - n)).reshape(rows, lanes)


def _unpack_rows(packed, shapes):
    flat = packed.reshape(-1)
    out, pos = [], 0
    for s in shapes:
        n = 1
        for d_ in s:
            n *= d_
        out.append(flat[pos:pos + n].reshape(s))
        pos += n
    return out


def _unshard_last(g):
    lead = g.shape[1:-1]
    return jnp.moveaxis(g, 0, -2).reshape(*lead, NDEV * g.shape[-1])


def _shard_last(full):
    lead, w = full.shape[:-1], full.shape[-1] // NDEV
    return jnp.moveaxis(full.reshape(*lead, NDEV, w), -2, 0)


def _rope_tables():
    rows = L // GRID_W
    row = jnp.repeat(jnp.arange(rows), GRID_W).astype(F32)
    col = jnp.tile(jnp.arange(GRID_W), rows).astype(F32)
    axis_dim = HEAD_DIM // 2
    inv_freq = ROPE_BASE ** (-jnp.arange(0, axis_dim, 2, dtype=F32) / axis_dim)
    ang_r, ang_c = row[:, None] * inv_freq[None, :], col[:, None] * inv_freq[None, :]
    ang = jnp.concatenate([ang_r, ang_r, ang_c, ang_c], axis=-1)
    ang = jnp.concatenate([ang, ang], axis=-1)[None]
    return jnp.cos(ang), jnp.sin(ang)


def _heads(x, nh):
    t = x.shape[1]
    return x.reshape(t, nh, HEAD_DIM).transpose(1, 0, 2)


def _unheads(x):
    nh, t, _ = x.shape
    return x.transpose(1, 0, 2).reshape(1, t, nh * HEAD_DIM)


FFN_HALO = 16
FFN_PAIRS = 4


def _ffn_tile(t):
    return 512 if t == L else 256


def _halo_specs(t, tm, block, index):
    per, last = tm // FFN_HALO, t // FFN_HALO - 1
    return [pl.BlockSpec(block(tm), lambda d, i: index(d, i)),
            pl.BlockSpec(block(FFN_HALO), lambda d, i: index(d, jnp.maximum(i * per - 1, 0))),
            pl.BlockSpec(block(FFN_HALO), lambda d, i: index(d, jnp.minimum((i + 1) * per, last)))]


def _seg_edges(i, tm, t):
    return (i * tm == 0) | (i * tm == L), ((i + 1) * tm == L) | ((i + 1) * tm == t)


FFN_RC = 8


def _sigmoid_t(x):
    return 0.5 * jnp.tanh(0.5 * x) + 0.5


class _RowShifts:
    def __init__(self, buf, s):
        self.buf, self.s, self.memo = buf, s, {}
        rows = lax.broadcasted_iota(jnp.int32, (FFN_RC, FFN_BLK), 0)
        self.first, self.last = rows == 0, rows == FFN_RC - 1

    def chunk(self, r):
        if r not in self.memo:
            self.memo[r] = self.buf[self.s, r:r + FFN_RC, :]
        return self.memo[r]

    def rot(self, r, by):
        if (r, by) not in self.memo:
            self.memo[(r, by)] = pltpu.roll(self.chunk(r), by, 0)
        return self.memo[(r, by)]

    def triple(self, r):
        before = jnp.where(self.first, self.rot(r - FFN_RC, 1), self.rot(r, 1))
        behind = jnp.where(self.last, self.rot(r + FFN_RC, FFN_RC - 1), self.rot(r, FFN_RC - 1))
        return before, self.chunk(r), behind


def _conv3_of(triple, cw_ref, s, flip=False):
    taps = [cw_ref[s, 0, k:k + 1, :] for k in ((2, 1, 0) if flip else (0, 1, 2))]
    return taps[0] * triple[0] + taps[1] * triple[1] + taps[2] * triple[2]


def _ffn_core_fwd(a2, up, cw, cb, down, name):
    t = a2.shape[1]
    tm = _ffn_tile(t)
    h0 = FFN_HALO

    def body(a_ref, ap_ref, an_ref, up_ref, cw_ref, cb_ref, dn_ref, z_ref, f_ref, abuf, zbuf, ubuf):
        d, i = pl.program_id(0), pl.program_id(1)
        seg_first, seg_last = _seg_edges(i, tm, t)
        abuf[0:h0, :] = ap_ref[0]
        abuf[h0:h0 + tm, :] = a_ref[0]
        abuf[h0 + tm:, :] = an_ref[0]
        for s in range(2):
            zbuf[s] = jnp.dot(abuf[...], up_ref[s, 0], preferred_element_type=F32)

        @pl.when(seg_first)
        def _():
            zbuf[:, 0:h0, :] = jnp.zeros((2, h0, FFN_BLK), F32)

        @pl.when(seg_last)
        def _():
            zbuf[:, h0 + tm:, :] = jnp.zeros((2, h0, FFN_BLK), F32)

        for s in range(2):
            z_ref[s, 0] = zbuf[s, h0:h0 + tm, :].astype(z_ref.dtype)

        zs = [_RowShifts(zbuf, 0), _RowShifts(zbuf, 1)]
        for r in range(h0, h0 + tm, FFN_RC):
            zg = _conv3_of(zs[0].triple(r), cw_ref, 0) + cb_ref[0, 0]
            zv = _conv3_of(zs[1].triple(r), cw_ref, 1) + cb_ref[1, 0]
            ubuf[r - h0:r - h0 + FFN_RC, :] = zg * _sigmoid_t(zg) * zv
        prod = jnp.dot(ubuf[...].astype(BF16), dn_ref[0], preferred_element_type=F32)
        rows = pl.ds(pl.multiple_of(i * tm, tm), tm)

        @pl.when(d == 0)
        def _():
            f_ref[0, rows, :] = prod

        @pl.when(d > 0)
        def _():
            f_ref[0, rows, :] += prod

    pair = lambda r, c: pl.BlockSpec((2, 1, r, c), lambda d, i: (0, d, 0, 0))
    return pl.pallas_call(
        body, name=name,
        out_shape=[jax.ShapeDtypeStruct((2, FFN_PAIRS, t, FFN_BLK), BF16), jax.ShapeDtypeStruct((1, t, D), F32)],
        grid=(FFN_PAIRS, t // tm),
        in_specs=_halo_specs(t, tm, lambda r: (1, r, D), lambda d, i: (0, i, 0))
        + [pair(D, FFN_BLK), pair(FFN_K, FFN_BLK), pair(1, FFN_BLK),
           pl.BlockSpec((1, FFN_BLK, D), lambda d, i: (d, 0, 0))],
        out_specs=[pl.BlockSpec((2, 1, tm, FFN_BLK), lambda d, i: (0, d, i, 0)),
                   pl.BlockSpec((1, t, D), lambda d, i: (0, 0, 0))],
        scratch_shapes=[pltpu.VMEM((tm + 2 * h0, D), BF16), pltpu.VMEM((2, tm + 2 * h0, FFN_BLK), F32),
                        pltpu.VMEM((tm, FFN_BLK), F32)],
        compiler_params=_cparams(("arbitrary", "arbitrary")),
    )(a2, a2, a2, up, cw, cb, down)


def _ffn_core_bwd(df, z, cw, cb, down, name):
    t = df.shape[1]
    tm = _ffn_tile(t)
    h0 = FFN_HALO
    ni = t // tm
    w0, wn = h0 // 2, tm + h0
    tn = (((0,), (0,)), ((), ()))
    nt = (((1,), (1,)), ((), ()))

    def body(df_ref, dfp_ref, dfn_ref, z_ref, zp_ref, zn_ref, cw_ref, cb_ref, dn_ref,
             dz_ref, dcw_ref, dcb_ref, ddn_ref, dfbuf, zbuf, dzbuf, acc, dubuf, ubuf, dzo):
        d, i = pl.program_id(0), pl.program_id(1)
        seg_first, seg_last = _seg_edges(i, tm, t)
        dfbuf[0:h0, :] = dfp_ref[0]
        dfbuf[h0:h0 + tm, :] = df_ref[0]
        dfbuf[h0 + tm:, :] = dfn_ref[0]
        for s in range(2):
            zbuf[s, 0:h0, :] = zp_ref[s, 0].astype(F32)
            zbuf[s, h0:h0 + tm, :] = z_ref[s, 0].astype(F32)
            zbuf[s, h0 + tm:, :] = zn_ref[s, 0].astype(F32)

        @pl.when(seg_first)
        def _():
            zbuf[:, 0:h0, :] = jnp.zeros((2, h0, FFN_BLK), F32)

        @pl.when(seg_last)
        def _():
            zbuf[:, h0 + tm:, :] = jnp.zeros((2, h0, FFN_BLK), F32)

        dubuf[...] = lax.dot_general(dfbuf[...], dn_ref[0], nt, preferred_element_type=F32)

        zs = [_RowShifts(zbuf, 0), _RowShifts(zbuf, 1)]
        sums = [[jnp.zeros((FFN_RC, FFN_BLK), F32)] * (FFN_K + 1) for _ in range(2)]
        for r in range(w0, w0 + wn, FFN_RC):
            tz = [zs[0].triple(r), zs[1].triple(r)]
            zg = _conv3_of(tz[0], cw_ref, 0) + cb_ref[0, 0]
            zv = _conv3_of(tz[1], cw_ref, 1) + cb_ref[1, 0]
            sg = _sigmoid_t(zg)
            silu = zg * sg
            du = dubuf[r:r + FFN_RC, :]
            dzc = [du * zv * (sg * (1.0 + zg * (1.0 - sg))), du * silu]
            dzbuf[0, r:r + FFN_RC, :] = dzc[0]
            dzbuf[1, r:r + FFN_RC, :] = dzc[1]
            ubuf[r:r + FFN_RC, :] = silu * zv
            if h0 <= r < h0 + tm:
                for s in range(2):
                    sums[s] = [sums[s][k] + dzc[s] * tz[s][k] for k in range(FFN_K)] + [sums[s][FFN_K] + dzc[s]]

        @pl.when(seg_first)
        def _():
            dzbuf[:, w0:h0, :] = jnp.zeros((2, h0 - w0, FFN_BLK), F32)

        @pl.when(seg_last)
        def _():
            dzbuf[:, h0 + tm:w0 + wn, :] = jnp.zeros((2, w0, FFN_BLK), F32)

        @pl.when(i == 0)
        def _():
            dcw_ref[...] = jnp.zeros_like(dcw_ref)
            dcb_ref[...] = jnp.zeros_like(dcb_ref)

        for s in range(2):
            dzs = _RowShifts(dzbuf, s)
            for r in range(h0, h0 + tm, FFN_RC):
                dzo[r - h0:r - h0 + FFN_RC, :] = _conv3_of(dzs.triple(r), cw_ref, s, flip=True)
            dz_ref[s, 0] = dzo[...].astype(dz_ref.dtype)
            for k in range(FFN_K):
                dcw_ref[s, 0, k:k + 1, :] += jnp.sum(sums[s][k], axis=0, keepdims=True)
            dcb_ref[s, 0] += jnp.sum(sums[s][FFN_K], axis=0, keepdims=True)
        prod = lax.dot_general(ubuf[h0:h0 + tm, :].astype(BF16), dfbuf[h0:h0 + tm, :], tn, preferred_element_type=F32)

        @pl.when(i == 0)
        def _():
            acc[...] = prod

        @pl.when(i > 0)
        def _():
            acc[...] += prod

        @pl.when(i == ni - 1)
        def _():
            ddn_ref[0] = acc[...].astype(ddn_ref.dtype)

    pair = lambda r, c: pl.BlockSpec((2, 1, r, c), lambda d, i: (0, d, 0, 0))
    return pl.pallas_call(
        body, name=name,
        out_shape=[jax.ShapeDtypeStruct((2, FFN_PAIRS, t, FFN_BLK), BF16),
                   jax.ShapeDtypeStruct((2, FFN_PAIRS, FFN_K, FFN_BLK), F32),
                   jax.ShapeDtypeStruct((2, FFN_PAIRS, 1, FFN_BLK), F32),
                   jax.ShapeDtypeStruct((FFN_PAIRS, FFN_BLK, D), BF16)],
        grid=(FFN_PAIRS, ni),
        in_specs=_halo_specs(t, tm, lambda r: (1, r, D), lambda d, i: (0, i, 0))
        + _halo_specs(t, tm, lambda r: (2, 1, r, FFN_BLK), lambda d, i: (0, d, i, 0))
        + [pair(FFN_K, FFN_BLK), pair(1, FFN_BLK), pl.BlockSpec((1, FFN_BLK, D), lambda d, i: (d, 0, 0))],
        out_specs=[pl.BlockSpec((2, 1, tm, FFN_BLK), lambda d, i: (0, d, i, 0)), pair(FFN_K, FFN_BLK),
                   pair(1, FFN_BLK), pl.BlockSpec((1, FFN_BLK, D), lambda d, i: (d, 0, 0))],
        scratch_shapes=[pltpu.VMEM((tm + 2 * h0, D), BF16), pltpu.VMEM((2, tm + 2 * h0, FFN_BLK), F32),
                        pltpu.VMEM((2, tm + 2 * h0, FFN_BLK), F32), pltpu.VMEM((FFN_BLK, D), F32),
                        pltpu.VMEM((tm + 2 * h0, FFN_BLK), F32), pltpu.VMEM((tm + 2 * h0, FFN_BLK), F32),
                        pltpu.VMEM((tm, FFN_BLK), F32)],
        compiler_params=_cparams(("arbitrary", "arbitrary")),
    )(df, df, df, z, z, z, cw, cb, down)


def _ffn_fwd(i, h, mod, ng, wts):
    a2 = _rw_fwd(f_modnorm, [(h, 0)], [(ng[2], "one"), (mod["sc2"], "seg"), (mod["sh2"], "seg")],
                 [(D, BF16)], name=f"ffn{i}_norm")[0]
    z, f = _ffn_core_fwd(a2, wts["up"], wts["cw"], wts["cb"], wts["down"], f"ffn{i}_core")
    h2 = _rw_fwd(f_resgate, [(h, 0), (f, 0)], [(mod["g2"], "seg"), (ng[3], "one")], [(D, F32)],
                 name=f"ffn{i}_res")[0]
    return h2, (h, a2, z, f)


def _ffn_bwd(i, dh, res, mod, ng, wts):
    h, a2, z, f = res
    t = h.shape[1]
    df, dg2, dng3 = _rw_bwd(f_gate_rms, [(f, 0)], [(mod["g2"], "seg"), (ng[3], "one")], [dh],
                            name=f"ffn{i}_res_b", row_grad=[(0, BF16)], param_grad=[0, 1])
    dz, dcw, dcb, d_down = _ffn_core_bwd(df, z, wts["cw"], wts["cb"], wts["down"], f"ffn{i}_core_b")
    dz = dz.reshape(NDEV, t, FFN_BLK)
    dcw, dcb = dcw.reshape(NDEV, FFN_K, FFN_BLK), dcb.reshape(NDEV, 1, FFN_BLK)
    up8 = wts["up"].reshape(NDEV, D, FFN_BLK)
    d_up = _mm(dz, a2, "tn", out_dtype=BF16, name=f"ffn{i}_up_bw")
    da2 = _mm(dz, up8, "nt", reduce_blocks=True, name=f"ffn{i}_up_bx")
    dh_in, dng2, dsc2, dsh2 = _rw_bwd(
        f_modnorm, [(h, 0)], [(ng[2], "one"), (mod["sc2"], "seg"), (mod["sh2"], "seg")], [da2],
        name=f"ffn{i}_norm_b", row_grad=[(0, F32)], param_grad=[0, 1, 2], add=dh)
    grads = dict(up=d_up, down=d_down, cw=dcw, cb=dcb, ng2=dng2, ng3=dng3, sc2=dsc2, sh2=dsh2, g2=dg2)
    return dh_in, grads


def _mixer_norm_fwd(i, h, mod, ng):
    return _rw_fwd(f_modnorm, [(h, 0)], [(ng[0], "one"), (mod["sc1"], "seg"), (mod["sh1"], "seg")],
                   [(D, BF16)], name=f"mix{i}_norm")[0]


def _mixer_norm_bwd(i, h, mod, ng, da, dh):
    return _rw_bwd(f_modnorm, [(h, 0)], [(ng[0], "one"), (mod["sc1"], "seg"), (mod["sh1"], "seg")], [da],
                   name=f"mix{i}_norm_b", row_grad=[(0, F32)], param_grad=[0, 1, 2], add=dh)


def _conformer_fwd(i, h, mod, ng, wts):
    a = _mixer_norm_fwd(i, h, mod, ng)
    p = _mm(a, wts["w_in"], "nn", name=f"cm{i}_in")
    z = _rw_fwd(f_glu, [(p, 0, 2)], [(wts["b_in"], "one", 2)], [(D, F32)], name=f"cm{i}_glu")[0]
    zc = _dwconv(z, wts["dw_w"], wts["dw_b"], name=f"cm{i}_conv")
    r = _rw_fwd(f_lnsilu, [(zc, 0)], [(wts["ln_g"], "one"), (wts["ln_b"], "one")], [(D, BF16)],
                name=f"cm{i}_ln")[0]
    y = _mm(r, wts["w_out"], "nn", name=f"cm{i}_out")
    h2 = _rw_fwd(f_resgate_bias, [(h, 0), (y, 0)], [(mod["g1"], "seg"), (ng[1], "one"), (wts["b_out"], "one")],
                 [(D, F32)], name=f"cm{i}_res")[0]
    return h2, (h, a, p, z, zc, r, y)


def _conformer_bwd(i, dh, res, mod, ng, wts):
    h, a, p, z, zc, r, y = res
    dy, dg1, dng1, db_out = _rw_bwd(
        f_gate_rms_bias, [(y, 0)], [(mod["g1"], "seg"), (ng[1], "one"), (wts["b_out"], "one")], [dh],
        name=f"cm{i}_res_b", row_grad=[(0, BF16)], param_grad=[0, 1, 2])
    dr = _mm(dy, wts["w_out"], "nt", name=f"cm{i}_out_bx")
    d_w_out = _mm(r, dy, "tn", out_dtype=BF16, name=f"cm{i}_out_bw")
    dzc, dln_g, dln_b = _rw_bwd(f_lnsilu, [(zc, 0)], [(wts["ln_g"], "one"), (wts["ln_b"], "one")], [dr],
                                name=f"cm{i}_ln_b", row_grad=[(0, F32)], param_grad=[0, 1])
    ddw_w, ddw_b = _dwconv_wgrad(z, dzc, CM_K, name=f"cm{i}_conv_bw")
    dz = _dwconv(dzc, wts["dw_w"][:, ::-1, :], None, name=f"cm{i}_conv_bx")
    dp, db_in = _rw_bwd(f_glu, [(p, 0, 2)], [(wts["b_in"], "one", 2)], [dz], name=f"cm{i}_glu_b",
                        row_grad=[(0, BF16)], param_grad=[0])
    d_w_in = _mm(a, dp, "tn", out_dtype=BF16, name=f"cm{i}_in_bw")
    da = _mm(dp, wts["w_in"], "nt", name=f"cm{i}_in_bx")
    dh_in, dng0, dsc1, dsh1 = _mixer_norm_bwd(i, h, mod, ng, da, dh)
    grads = dict(w_in=d_w_in, w_out=d_w_out, b_in=db_in, dw_w=ddw_w, dw_b=ddw_b, ln_g=dln_g, ln_b=dln_b,
                 b_out=db_out, ng0=dng0, ng1=dng1, sc1=dsc1, sh1=dsh1, g1=dg1)
    return dh_in, grads


def _attention_fwd(i, h_all, mod, ng, wts, tables):
    a = _mixer_norm_fwd(i, h_all, mod, ng)
    qkv = _mm(a, wts["w_qkv"], "nn", name="attn_qkv")
    kv0 = N_Q * HEAD_DIM
    kv1 = kv0 + N_KV * HEAD_DIM
    q = _rope(qkv[:, :L, :kv0], tables, False, BF16, "attn_rope_q")
    k = _rope(qkv[:, :L, kv0:kv1], tables, False, BF16, "attn_rope_k")
    pad = ((0, 0), (ATTN_BLOCK, ATTN_BLOCK), (0, 0))
    q_h = _heads(q, N_Q)
    k_h = jnp.pad(_heads(k, N_KV), pad)
    v_h = jnp.pad(_heads(qkv[:, :L, kv1:].astype(BF16), N_KV), pad)
    kc_h = _heads(qkv[:, L:, kv0:kv1].astype(BF16), N_KV)
    vc_h = _heads(qkv[:, L:, kv1:].astype(BF16), N_KV)
    o_h, lse = _attn_fwd(q_h, k_h, v_h, kc_h, vc_h, wts["sink"])
    o = _unheads(o_h)
    y = _mm(o, wts["w_o"], "nn", name="attn_o")
    h_lat = h_all[:, :L]
    mod_lat = {k_: v_[:1] for k_, v_ in mod.items()}
    h2 = _rw_fwd(f_resgate, [(h_lat, 0), (y, 0)], [(mod_lat["g1"], "seg"), (ng[1], "one")], [(D, F32)],
                 name="attn_res")[0]
    return h2, (h_all, a, q_h, k_h, v_h, kc_h, vc_h, o_h, lse, o, y)


def _attention_bwd(i, dh, res, mod, ng, wts, tables):
    h_all, a, q_h, k_h, v_h, kc_h, vc_h, o_h, lse, o, y = res
    mod_lat = {k_: v_[:1] for k_, v_ in mod.items()}
    dy, dg1, dng1 = _rw_bwd(f_gate_rms, [(y, 0)], [(mod_lat["g1"], "seg"), (ng[1], "one")], [dh],
                            name="attn_res_b", row_grad=[(0, BF16)], param_grad=[0, 1])
    do = _mm(dy, wts["w_o"], "nt", name="attn_o_bx")
    d_w_o = _mm(o, dy, "tn", out_dtype=BF16, name="attn_o_bw")
    dq_h, dk_h, dv_h, dkc_h, dvc_h, dsink = _attn_bwd(q_h, k_h, v_h, kc_h, vc_h, wts["sink"], o_h, lse,
                                                        _heads(do, N_Q))
    dq = _rope(_unheads(dq_h), tables, True, BF16, "attn_rope_q_b")
    dk = _rope(_unheads(dk_h[:, ATTN_BLOCK:-ATTN_BLOCK]), tables, True, BF16, "attn_rope_k_b")
    dv = _unheads(dv_h[:, ATTN_BLOCK:-ATTN_BLOCK]).astype(BF16)
    d_lat = jnp.concatenate([dq, dk, dv], axis=2)
    d_ctx = jnp.concatenate([jnp.zeros((1, LC, N_Q * HEAD_DIM), BF16), _unheads(dkc_h).astype(BF16),
                             _unheads(dvc_h).astype(BF16)], axis=2)
    dqkv = jnp.concatenate([d_lat, d_ctx], axis=1)
    d_w_qkv = _mm(a, dqkv, "tn", out_dtype=BF16, name="attn_qkv_bw")
    da = _mm(dqkv, wts["w_qkv"], "nt", name="attn_qkv_bx")
    dh_res = jnp.concatenate([dh, jnp.zeros((1, LC, D), F32)], axis=1)
    dh_in, dng0, dsc1, dsh1 = _mixer_norm_bwd(i, h_all, mod, ng, da, dh_res)
    grads = dict(w_qkv=d_w_qkv, w_o=d_w_o, sink=dsink, ng0=dng0, ng1=dng1, sc1=dsc1, sh1=dsh1, g1=dg1)
    return dh_in, grads


def _gmlp_fwd(i, h, mod, ng, wts):
    a = _mixer_norm_fwd(i, h, mod, ng)
    p = _mm(a, wts["w_in"], "nn", name="gm_in")
    u, v = _rw_fwd(f_gmlp_pre, [(p, 0, 2)], [(wts["b_in"], "one", 2), (wts["ln_g"], "one"), (wts["ln_b"], "one")],
                   [(GM_W, F32), (GM_W, BF16)], name="gm_pre")
    us = _gm_spatial_fwd(u, v, wts["w_s"], wts["b_s"])
    y = _mm(us, wts["w_out"], "nn", name="gm_out")
    h2 = _rw_fwd(f_resgate, [(h, 0), (y, 0)], [(mod["g1"], "seg"), (ng[1], "one")], [(D, F32)],
                 name="gm_res")[0]
    return h2, (h, a, p, u, v, us, y)


def _gmlp_bwd(i, dh, res, mod, ng, wts):
    h, a, p, u, v, us, y = res
    dy, dg1, dng1 = _rw_bwd(f_gate_rms, [(y, 0)], [(mod["g1"], "seg"), (ng[1], "one")], [dh],
                            name="gm_res_b", row_grad=[(0, BF16)], param_grad=[0, 1])
    dus = _mm(dy, wts["w_out"], "nt", name="gm_out_bx")
    d_w_out = _mm(us, dy, "tn", out_dtype=BF16, name="gm_out_bw")
    du, dv, dws, dbs = _gm_spatial_bwd(u, v, wts["w_s"], wts["b_s"], dus)
    dp, db_in, dln_g, dln_b = _rw_bwd(
        f_gmlp_pre, [(p, 0, 2)], [(wts["b_in"], "one", 2), (wts["ln_g"], "one"), (wts["ln_b"], "one")], [du, dv],
        name="gm_pre_b", row_grad=[(0, BF16)], param_grad=[0, 1, 2])
    d_w_in = _mm(a, dp, "tn", out_dtype=BF16, name="gm_in_bw")
    da = _mm(dp, wts["w_in"], "nt", name="gm_in_bx")
    dh_in, dng0, dsc1, dsh1 = _mixer_norm_bwd(i, h, mod, ng, da, dh)
    grads = dict(w_in=d_w_in, w_out=d_w_out, b_in=db_in, ln_g=dln_g, ln_b=dln_b, w_s=dws, b_s=dbs,
                 ng0=dng0, ng1=dng1, sc1=dsc1, sh1=dsh1, g1=dg1)
    return dh_in, grads


MOD_NAMES = ("sh1", "sc1", "g1", "sh2", "sc2", "g2")
SMALL = (
    ("norm_g", (4, 4, 128)), ("ffn_conv_w", (4, 3, 704)), ("cm_b_in", (2, 256)), ("cm_dw_w", (2, 31, 128)),
    ("cm_dw_b", (2, 128)), ("cm_ln_g", (2, 128)), ("cm_ln_b", (2, 128)), ("cm_b_out", (2, 128)),
    ("gm_b_in", (1, 512)), ("gm_ln_g", (1, 256)), ("gm_ln_b", (1, 256)))


def _mixer_weights(i, P):
    if i % 3 == 0:
        j = i // 3
        return dict(w_in=P["cm_w_in"][j], w_out=P["cm_w_out"][j], b_in=P["cm_b_in"][j].reshape(1, 1, 2 * D),
                    dw_w=P["cm_dw_w"][j][None], dw_b=P["cm_dw_b"][j].reshape(1, 1, D),
                    ln_g=P["cm_ln_g"][j].reshape(1, 1, D), ln_b=P["cm_ln_b"][j].reshape(1, 1, D),
                    b_out=P["cm_b_out"][j].reshape(1, 1, D))
    if i % 3 == 1:
        return dict(w_qkv=P["attn_w_qkv"], w_o=P["attn_w_o"], sink=P["attn_sink"].reshape(N_Q))
    return dict(w_in=P["gm_w_in"], w_out=P["gm_w_out"], b_in=P["gm_b_in"].reshape(1, 1, 2 * GM_W),
                ln_g=P["gm_ln_g"].reshape(1, 1, GM_W), ln_b=P["gm_ln_b"].reshape(1, 1, GM_W),
                w_s=P["gm_w_s"].reshape(GM_GROUPS, GM_CHUNK, GM_CHUNK).astype(BF16),
                b_s=P["gm_b_s"].reshape(GM_GROUPS, GM_CHUNK, 1))


def _ffn_weights(i, P):
    return dict(up=P["ffn_w_up"][i].reshape(2, FFN_PAIRS, D, FFN_BLK), down=P["ffn_w_down"][i],
                cw=P["ffn_conv_w"][i].reshape(2, FFN_PAIRS, FFN_K, FFN_BLK),
                cb=P["ffn_conv_b"][i].reshape(2, FFN_PAIRS, 1, FFN_BLK))


def _local_step(x, ctx, target, lat_mod, ctx_mod, norm_g, layer_weights, grads_ready):
    tables = _rope_tables()
    ng = [[norm_g[i, j].reshape(1, 1, D) for j in range(4)] for i in range(DEPTH)]

    def mods(i, with_ctx, token):
        out = {}
        for j, nme in enumerate(MOD_NAMES):
            rows = [lat_mod[i, j]] + ([ctx_mod[i, j]] if with_ctx else [])
            out[nme] = jnp.stack(rows).reshape(len(rows), 1, D) + token[0, 0]
        return out

    def after(mod, token):
        return mod if token is None else {k_: v_ + token[0, 0] for k_, v_ in mod.items()}

    h_all = jnp.concatenate([x, ctx], axis=1)
    wm0, wf0, tok = layer_weights(0, h_all)
    m0 = mods(0, True, tok)
    h, r0m = _conformer_fwd(0, h_all, m0, ng[0], wm0)
    h, r0f = _ffn_fwd(0, h, m0, ng[0], wf0)
    wm1, wf1, tok = layer_weights(1, h)
    m1 = mods(1, True, tok)
    m1l = {k_: v_[:1] for k_, v_ in m1.items()}
    h, r1m = _attention_fwd(1, h, m1, ng[1], wm1, tables)
    h, r1f = _ffn_fwd(1, h, m1l, ng[1], wf1)
    wm2, wf2, tok = layer_weights(2, h)
    m2 = mods(2, False, tok)
    h, r2m = _gmlp_fwd(2, h, m2, ng[2], wm2)
    h, r2f = _ffn_fwd(2, h, m2, ng[2], wf2)
    wm3, wf3, tok = layer_weights(3, h)
    m3 = mods(3, False, tok)
    h, r3m = _conformer_fwd(3, h, m3, ng[3], wm3)
    h, r3f = _ffn_fwd(3, h, m3, ng[3], wf3)
    dh, loss = _loss_head(h, target)

    G = {}
    dh, G["f3"] = _ffn_bwd(3, dh, r3f, m3, ng[3], wf3)
    tok = grads_ready("f3", G["f3"])
    dh, G["m3"] = _conformer_bwd(3, dh, r3m, after(m3, tok), ng[3], wm3)
    tok = grads_ready("m3", G["m3"])
    dh, G["f2"] = _ffn_bwd(2, dh, r2f, after(m2, tok), ng[2], wf2)
    tok = grads_ready("f2", G["f2"])
    dh, G["m2"] = _gmlp_bwd(2, dh, r2m, after(m2, tok), ng[2], wm2)
    tok = grads_ready("m2", G["m2"])
    dh, G["f1"] = _ffn_bwd(1, dh, r1f, after(m1l, tok), ng[1], wf1)
    tok = grads_ready("f1", G["f1"])
    dh, G["m1"] = _attention_bwd(1, dh, r1m, after(m1, tok), ng[1], wm1, tables)
    tok = grads_ready("m1", G["m1"])
    dh, G["f0"] = _ffn_bwd(0, dh, r0f, after(m0, tok), ng[0], wf0)
    tok = grads_ready("f0", G["f0"])
    dh, G["m0"] = _conformer_bwd(0, dh, r0m, after(m0, tok), ng[0], wm0)
    grads_ready("m0", G["m0"])
    grad_x = dh[:, :L]

    zero = jnp.zeros((D,), F32)
    dmod = []
    for seg in range(2):
        per_layer = []
        for i in range(DEPTH):
            vals = []
            for nme in MOD_NAMES:
                src = G[("m" if nme.endswith("1") else "f") + str(i)][nme]
                vals.append(src[seg, 0] if src.shape[0] > seg else zero)
            per_layer.append(jnp.concatenate(vals))
        dmod.append(jnp.stack(per_layer))
    dmod = jnp.stack(dmod)
    return loss, grad_x, G, dmod


def kernel(x, c, ctx, c_ctx, ada_w, ada_b, norm_g, ffn_w_up, ffn_conv_w, ffn_conv_b, ffn_w_down, cm_w_in, cm_b_in, cm_dw_w, cm_dw_b, cm_ln_g, cm_ln_b, cm_w_out, cm_b_out, attn_w_qkv, attn_sink, attn_w_o, gm_w_in, gm_b_in, gm_ln_g, gm_ln_b, gm_w_s, gm_b_s, gm_w_out, loss_target, m_c_ctx, m_ada_w, m_ada_b, m_norm_g, m_ffn_w_up, m_ffn_conv_w, m_ffn_conv_b, m_ffn_w_down, m_cm_w_in, m_cm_b_in, m_cm_dw_w, m_cm_dw_b, m_cm_ln_g, m_cm_ln_b, m_cm_w_out, m_cm_b_out, m_attn_w_qkv, m_attn_sink, m_attn_w_o, m_gm_w_in, m_gm_b_in, m_gm_ln_g, m_gm_ln_b, m_gm_w_s, m_gm_b_s, m_gm_w_out, v_c_ctx, v_ada_w, v_ada_b, v_norm_g, v_ffn_w_up, v_ffn_conv_w, v_ffn_conv_b, v_ffn_w_down, v_cm_w_in, v_cm_b_in, v_cm_dw_w, v_cm_dw_b, v_cm_ln_g, v_cm_ln_b, v_cm_w_out, v_cm_b_out, v_attn_w_qkv, v_attn_sink, v_attn_w_o, v_gm_w_in, v_gm_b_in, v_gm_ln_g, v_gm_ln_b, v_gm_w_s, v_gm_b_s, v_gm_w_out):
    W = dict(c_ctx=c_ctx, ada_w=ada_w, ada_b=ada_b, norm_g=norm_g, ffn_w_up=ffn_w_up, ffn_conv_w=ffn_conv_w, ffn_conv_b=ffn_conv_b, ffn_w_down=ffn_w_down, cm_w_in=cm_w_in, cm_b_in=cm_b_in, cm_dw_w=cm_dw_w, cm_dw_b=cm_dw_b, cm_ln_g=cm_ln_g, cm_ln_b=cm_ln_b, cm_w_out=cm_w_out, cm_b_out=cm_b_out, attn_w_qkv=attn_w_qkv, attn_sink=attn_sink, attn_w_o=attn_w_o, gm_w_in=gm_w_in, gm_b_in=gm_b_in, gm_ln_g=gm_ln_g, gm_ln_b=gm_ln_b, gm_w_s=gm_w_s, gm_b_s=gm_b_s, gm_w_out=gm_w_out)
    M = dict(c_ctx=m_c_ctx, ada_w=m_ada_w, ada_b=m_ada_b, norm_g=m_norm_g, ffn_w_up=m_ffn_w_up, ffn_conv_w=m_ffn_conv_w, ffn_conv_b=m_ffn_conv_b, ffn_w_down=m_ffn_w_down, cm_w_in=m_cm_w_in, cm_b_in=m_cm_b_in, cm_dw_w=m_cm_dw_w, cm_dw_b=m_cm_dw_b, cm_ln_g=m_cm_ln_g, cm_ln_b=m_cm_ln_b, cm_w_out=m_cm_w_out, cm_b_out=m_cm_b_out, attn_w_qkv=m_attn_w_qkv, attn_sink=m_attn_sink, attn_w_o=m_attn_w_o, gm_w_in=m_gm_w_in, gm_b_in=m_gm_b_in, gm_ln_g=m_gm_ln_g, gm_ln_b=m_gm_ln_b, gm_w_s=m_gm_w_s, gm_b_s=m_gm_b_s, gm_w_out=m_gm_w_out)
    V = dict(c_ctx=v_c_ctx, ada_w=v_ada_w, ada_b=v_ada_b, norm_g=v_norm_g, ffn_w_up=v_ffn_w_up, ffn_conv_w=v_ffn_conv_w, ffn_conv_b=v_ffn_conv_b, ffn_w_down=v_ffn_w_down, cm_w_in=v_cm_w_in, cm_b_in=v_cm_b_in, cm_dw_w=v_cm_dw_w, cm_dw_b=v_cm_dw_b, cm_ln_g=v_cm_ln_g, cm_ln_b=v_cm_ln_b, cm_w_out=v_cm_w_out, cm_b_out=v_cm_b_out, attn_w_qkv=v_attn_w_qkv, attn_sink=v_attn_sink, attn_w_o=v_attn_w_o, gm_w_in=v_gm_w_in, gm_b_in=v_gm_b_in, gm_ln_g=v_gm_ln_g, gm_ln_b=v_gm_ln_b, gm_w_s=v_gm_w_s, gm_b_s=v_gm_b_s, gm_w_out=v_gm_w_out)
    me = 4 * lax.axis_index("x") + 2 * lax.axis_index("y") + lax.axis_index("c")
    small_shapes = [s for _, s in SMALL]

    small = _pack_rows([W[n] for n, _ in SMALL] + [c])
    layer_mats = [("cm_w_in", 0, "cm_w_out", 0), ("attn_w_qkv", 0, "attn_w_o", 0), ("gm_w_in", 0, "gm_w_out", 0),
                  ("cm_w_in", 1, "cm_w_out", 1)]
    local_bf16 = [[W[a][ja].astype(BF16), W[b][jb].astype(BF16), ffn_w_up[i].astype(BF16), ffn_w_down[i].astype(BF16)]
                  for i, (a, ja, b, jb) in enumerate(layer_mats)]
    gathered0 = _all_gather([small] + local_bf16[0], "gather_params0")
    small_g = gathered0[0]
    col_to_full = lambda g: g.transpose(1, 0, 2).reshape(g.shape[1], NDEV * g.shape[2])
    P = {}
    unpacked = jax.vmap(lambda r: tuple(_unpack_rows(r, small_shapes + [(D,)])))(small_g)
    for (n, _), g in zip(SMALL, unpacked[:-1]):
        if n == "ffn_conv_w":
            P[n] = [g[:, i] for i in range(DEPTH)]
        else:
            P[n] = _unshard_last(g)
    c_all = unpacked[-1]
    P["ffn_conv_b"] = [ffn_conv_b[i].reshape(NDEV, 1, FFN_BLK) for i in range(DEPTH)]
    P["attn_sink"], P["gm_w_s"], P["gm_b_s"] = attn_sink, gm_w_s, gm_b_s

    cond = jnp.concatenate([c_all, c_ctx[None], jnp.zeros((7, D), F32)])[None]
    scond = _rw_fwd(f_silu, [(cond, 0)], [], [(D, BF16)], name="ada_silu")[0]
    ada_bf = ada_w.astype(BF16)
    ncol = ada_w.shape[2]
    mod_loc = _mm(scond, ada_bf, "nn", name="ada_proj")
    mod_loc = mod_loc + lax.dynamic_slice_in_dim(ada_b, me * ncol, ncol, axis=1)[:, None, :]
    mod_g = _all_gather([mod_loc], "gather_mod")[0]
    mod_full = mod_g.transpose(1, 2, 0, 3).reshape(DEPTH, 16, 6, D)
    lat_mod = lax.dynamic_index_in_dim(mod_full, me, axis=1, keepdims=False)
    ctx_mod = mod_full[:, NDEV]

    gathers, exchanges, pending = {}, {}, {}
    col_to_parts = lambda g: g[0].reshape(g.shape[1], NDEV, g.shape[2] // NDEV).transpose(1, 0, 2)
    row_to_parts = lambda g: g.reshape(NDEV, -1, g.shape[-1])
    no_order = jnp.zeros((8, 128), F32)

    def layer_weights(i, h):
        if i == 0:
            mats = gathered0[1:]
            gathers[1] = _xfer_start("gather", local_bf16[1], mod_g, "gather_params1")
        else:
            mats = _xfer_wait(gathers[i], h)
            if i + 1 < DEPTH:
                gathers[i + 1] = _xfer_start("gather", local_bf16[i + 1], mats[0], f"gather_params{i + 1}")
        token = gathers[i + 1]["token"] if i + 1 < DEPTH else no_order
        a, ja, b, jb = layer_mats[i]
        pi = dict(P)
        pi[a] = col_to_full(mats[0]) if a.startswith(("attn", "gm")) else {ja: col_to_full(mats[0])}
        pi[b] = mats[1].reshape(-1, D) if b.startswith(("attn", "gm")) else {jb: mats[1].reshape(-1, D)}
        pi["ffn_w_up"], pi["ffn_w_down"] = {i: mats[2]}, {i: mats[3].reshape(FFN_PAIRS, FFN_BLK, D)}
        return _mixer_weights(i, pi), _ffn_weights(i, pi), token

    def grads_ready(tag, g):
        pending[tag] = g
        i = int(tag[1])
        col_name, row_name = {0: ("w_in", "w_out"), 1: ("w_qkv", "w_o"), 2: ("w_in", "w_out")}[i % 3]
        if tag == "f0":
            arrs = [g["up"], row_to_parts(g["down"])]
        elif tag == "m0":
            arrs = [col_to_parts(g[col_name]), row_to_parts(g[row_name])]
        elif tag[0] == "m":
            gf = pending[f"f{i}"]
            arrs = [col_to_parts(g[col_name]), row_to_parts(g[row_name]), gf["up"], row_to_parts(gf["down"])]
        else:
            return None
        exchanges[tag] = _xfer_start("scatter", arrs, no_order, "exchange_" + tag)
        return exchanges[tag]["token"]

    loss_part, grad_x, G, dmod = _local_step(x, ctx, loss_target, lat_mod, ctx_mod, P["norm_g"], layer_weights,
                                             grads_ready)
    recv = {tag: _xfer_wait(exchanges[tag], grad_x) for tag in ("m3", "m2", "m1")}

    dmod_g = _all_gather([dmod], "gather_dmod")[0]
    dm_cols = lax.dynamic_slice_in_dim(dmod_g, me * ncol, ncol, axis=3)
    dm_ext = dm_cols.transpose(2, 1, 0, 3).reshape(DEPTH, 16, ncol)
    cond_ext = jnp.concatenate([c_all, jnp.broadcast_to(c_ctx[None], (NDEV, D))])[None]
    scond_ext = _rw_fwd(f_silu, [(cond_ext, 0)], [], [(D, BF16)], name="ada_silu_ext")[0]
    g_ada_w = _mm(scond_ext, dm_ext, "tn", name="ada_proj_bw")
    dsil = _mm(dm_ext, ada_bf, "nt", reduce_blocks=True, name="ada_proj_bx")
    dcc = _rw_bwd(f_silu_rows, [(jnp.zeros((1, NDEV, D), F32), 0)], [(c_ctx.reshape(1, 1, D), "one")],
                  [dsil[:, NDEV:]], name="ada_silu_b", param_grad=[0])[0]

    out = {}

    def put(name, res):
        out[name] = res

    d_norm_g = jnp.stack([jnp.stack([G[f"m{i}"]["ng0"], G[f"m{i}"]["ng1"], G[f"f{i}"]["ng2"], G[f"f{i}"]["ng3"]])
                          for i in range(DEPTH)]).reshape(DEPTH, 4, D)
    small_full = dict(
        norm_g=d_norm_g,
        cm_b_in=jnp.stack([G["m0"]["b_in"], G["m3"]["b_in"]]).reshape(2, 2 * D),
        cm_dw_w=jnp.stack([G["m0"]["dw_w"][0], G["m3"]["dw_w"][0]]),
        cm_dw_b=jnp.stack([G["m0"]["dw_b"], G["m3"]["dw_b"]]).reshape(2, D),
        cm_ln_g=jnp.stack([G["m0"]["ln_g"], G["m3"]["ln_g"]]).reshape(2, D),
        cm_ln_b=jnp.stack([G["m0"]["ln_b"], G["m3"]["ln_b"]]).reshape(2, D),
        cm_b_out=jnp.stack([G["m0"]["b_out"], G["m3"]["b_out"]]).reshape(2, D),
        gm_b_in=G["m2"]["b_in"].reshape(1, 2 * GM_W),
        gm_ln_g=G["m2"]["ln_g"].reshape(1, GM_W), gm_ln_b=G["m2"]["ln_b"].reshape(1, GM_W))
    by_dest = []
    for n, _ in SMALL:
        if n == "ffn_conv_w":
            by_dest.append(jnp.stack([G[f"f{i}"]["cw"] for i in range(DEPTH)], axis=1))
        else:
            by_dest.append(_shard_last(small_full[n]))
    small_send = jax.vmap(lambda *vs: _pack_rows(list(vs)))(*by_dest)
    small_recv = _all_to_all([[small_send]], "exchange_small")[0]

    def shard3(a):
        return a.reshape(a.shape[0], -1, a.shape[-1])

    small_local = lambda d_: _pack_rows([d_[n] for n, _ in SMALL])[None]
    res = _adamw(small_recv, small_local(W), small_local(M), small_local(V), "adamw_small")
    unp = [_unpack_rows(r[0], small_shapes) for r in res]
    for q, (n, _) in enumerate(SMALL):
        put(n, [unp[t][q] for t in range(4)])

    repl_names = ["c_ctx", "ffn_conv_b", "attn_sink", "gm_b_s", "gm_w_s"]
    repl_part = dict(
        c_ctx=dcc.reshape(D),
        ffn_conv_b=jnp.stack([G[f"f{i}"]["cb"].reshape(2 * 2816) for i in range(DEPTH)]),
        attn_sink=G["m1"]["sink"].reshape(1, N_Q),
        gm_b_s=G["m2"]["b_s"].reshape(1, GM_GROUPS, GM_CHUNK),
        gm_w_s=G["m2"]["w_s"].reshape(1, GM_GROUPS, GM_CHUNK, GM_CHUNK))
    repl_shapes = [W[n].shape for n in repl_names]
    repl_sent = _pack_rows([repl_part[n] for n in repl_names] + [loss_part.reshape(1)], mult=256)
    repl_g = _all_gather([repl_sent], "gather_repl")[0]
    loss = jnp.sum(repl_g.reshape(NDEV, -1)[:, sum(W[n].size for n in repl_names)])
    repl_local = lambda d_: _pack_rows([d_[n] for n in repl_names], mult=256)[None]
    res = _adamw(repl_g[None], repl_local(W), repl_local(M), repl_local(V), "adamw_repl")
    unp = [_unpack_rows(r[0], repl_shapes) for r in res]
    for q, n in enumerate(repl_names):
        put(n, [unp[t][q] for t in range(4)])

    recv.update({tag: _xfer_wait(exchanges[tag], repl_g) for tag in ("f0", "m0")})
    big_parts = dict(
        ffn_w_up=[recv["f0"][0]] + [recv[f"m{i}"][2] for i in (1, 2, 3)],
        ffn_w_down=[recv["f0"][1]] + [recv[f"m{i}"][3] for i in (1, 2, 3)],
        cm_w_in=[recv["m0"][0], recv["m3"][0]], cm_w_out=[recv["m0"][1], recv["m3"][1]],
        attn_w_qkv=[recv["m1"][0]], attn_w_o=[recv["m1"][1]], gm_w_in=[recv["m2"][0]], gm_w_out=[recv["m2"][1]])
    for n, parts in big_parts.items():
        turn = (lambda a: jnp.swapaxes(a, 1, 2)) if n == "ffn_w_up" else shard3
        res = _adamw(parts, turn(W[n]), turn(M[n]), turn(V[n]), "adamw_" + n)
        put(n, [(turn(r) if n == "ffn_w_up" else r).reshape(W[n].shape) for r in res])

    put("ada_w", _adamw(g_ada_w[:, None], ada_w, m_ada_w, v_ada_w, "adamw_ada_w"))
    ada_b_parts = dmod_g.reshape(1, 2 * NDEV, DEPTH, 6 * D)
    res = _adamw(ada_b_parts, ada_b[None], m_ada_b[None], v_ada_b[None], "adamw_ada_b")
    put("ada_b", [r[0] for r in res])

    names = ["c_ctx", "ada_w", "ada_b", "norm_g", "ffn_w_up", "ffn_conv_w", "ffn_conv_b", "ffn_w_down", "cm_w_in",
             "cm_b_in", "cm_dw_w", "cm_dw_b", "cm_ln_g", "cm_ln_b", "cm_w_out", "cm_b_out", "attn_w_qkv",
             "attn_sink", "attn_w_o", "gm_w_in", "gm_b_in", "gm_ln_g", "gm_ln_b", "gm_w_s", "gm_b_s", "gm_w_out"]
    return (loss, grad_x, *[out[n][0] for n in names], *[out[n][1] for n in names],
            *[out[n][2] for n in names], *[out[n][3] for n in names])
```

```python
import functools

import jax
import jax.numpy as jnp
from jax import lax
from jax.experimental import pallas as pl
from jax.experimental.pallas import tpu as pltpu

F32, BF16 = jnp.float32, jnp.bfloat16
MESH = pl.DeviceIdType.MESH
AXES = ("x", "y", "c")
NDEV = 8

D = 1024
L = 2048
LC = 256
TA = L + LC
DEPTH = 4
EPS = 1e-6
HEAD_DIM = 64
N_Q, N_KV, Q_PER_KV = 16, 4, 4
ATTN_BLOCK = 128
GRID_W = 64
ROPE_BASE = 10000.0
GM_W = 2048
GM_CHUNK = 128
GM_GROUPS = 16
FFN_BLK = 704
CM_K, FFN_K = 31, 3

ADAM_LR, ADAM_B1, ADAM_B2, ADAM_EPS, ADAM_WD, ADAM_STEP = 0.001, 0.9, 0.999, 1e-08, 0.01, 10

VMEM_LIMIT_V7X = 58 * 1024 * 1024
ROW_TILE_ELEMS = 256 * 1024
MM_TILE_BYTES = 4 * 1024 * 1024


def _cparams(sem=None):
    kw = dict(vmem_limit_bytes=VMEM_LIMIT_V7X)
    if sem is not None:
        kw["dimension_semantics"] = sem
    return pltpu.CompilerParams(**kw)


def _pick(n, cands):
    for c in cands:
        if n % c == 0:
            return c
    return n


def _as3(a):
    return a if a.ndim == 3 else a[None]


def _all_gather(arrs, name):
    n = len(arrs)

    def body(*refs):
        ins, outs = refs[:n], refs[n:2 * n]
        send_sems, recv_sems, local_sems = refs[2 * n:]
        x, y, c = lax.axis_index("x"), lax.axis_index("y"), lax.axis_index("c")
        me, sibling = (x, y, c), (x, y, 1 - c)
        chips = [(1 - x, y), (x, 1 - y), (1 - x, 1 - y)]

        def slot(a, p):
            return outs[a].at[4 * p[0] + 2 * p[1] + p[2]]

        def copy(a, k, block, to, src=None):
            return pltpu.make_async_remote_copy(
                src_ref=slot(a, block) if src is None else src, dst_ref=slot(a, block),
                send_sem=send_sems.at[a, k], recv_sem=recv_sems.at[a, k],
                device_id=to, device_id_type=MESH)

        mine = [pltpu.make_async_copy(ins[a], slot(a, me), local_sems.at[a]) for a in range(n)]
        for m in mine:
            m.start()
        first = []
        for a in range(n):
            first.append(copy(a, 0, me, sibling, src=ins[a]))
            first += [copy(a, 1 + j, me, (*chip, c), src=ins[a]) for j, chip in enumerate(chips)]
        for cp in first:
            cp.start()
        passed = []
        for j, chip in enumerate(chips):
            for a in range(n):
                copy(a, 1 + j, (*chip, c), me).wait_recv()
                p = copy(a, 4 + j, (*chip, c), sibling)
                p.start()
                passed.append(p)
        for a in range(n):
            copy(a, 0, sibling, me).wait_recv()
            for j, chip in enumerate(chips):
                copy(a, 4 + j, (*chip, 1 - c), me).wait_recv()
        for cp in first + passed:
            cp.wait_send()
        for m in mine:
            m.wait()

    any_spec = pl.BlockSpec(memory_space=pl.ANY)
    outs = pl.pallas_call(
        body, name=name,
        out_shape=[jax.ShapeDtypeStruct((NDEV,) + a.shape, a.dtype) for a in arrs],
        in_specs=[any_spec] * n, out_specs=[any_spec] * n,
        scratch_shapes=[pltpu.SemaphoreType.DMA((n, 7)), pltpu.SemaphoreType.DMA((n, 7)),
                        pltpu.SemaphoreType.DMA((n,))],
    )(*arrs)
    return list(outs)


def _all_to_all(groups, name):
    flat = [(gi, li, a) for gi, g in enumerate(groups) for li, a in enumerate(g)]
    n, ng = len(flat), len(groups)

    def body(*refs):
        ins, outs = refs[:n], refs[n:n + ng]
        send_sems, recv_sems, local_sems = refs[n + ng:]
        x, y, c = lax.axis_index("x"), lax.axis_index("y"), lax.axis_index("c")
        me = 4 * x + 2 * y + c
        copies = []
        for a, (gi, li, _) in enumerate(flat):
            loc = pltpu.make_async_copy(ins[a].at[me], outs[gi].at[li, me], local_sems.at[a])
            loc.start()
            copies.append(loc)
            for k in range(1, NDEV):
                px = 1 - x if (k >> 2) & 1 else x
                py = 1 - y if (k >> 1) & 1 else y
                pc = 1 - c if k & 1 else c
                cp = pltpu.make_async_remote_copy(
                    src_ref=ins[a].at[4 * px + 2 * py + pc], dst_ref=outs[gi].at[li, me],
                    send_sem=send_sems.at[a, k - 1], recv_sem=recv_sems.at[a, k - 1],
                    device_id=(px, py, pc), device_id_type=MESH)
                cp.start()
                copies.append(cp)
        for cp in copies:
            cp.wait()

    any_spec = pl.BlockSpec(memory_space=pl.ANY)
    outs = pl.pallas_call(
        body, name=name,
        out_shape=[jax.ShapeDtypeStruct((len(g),) + g[0].shape, g[0].dtype) for g in groups],
        in_specs=[any_spec] * n, out_specs=[any_spec] * ng,
        scratch_shapes=[pltpu.SemaphoreType.DMA((n, 7)), pltpu.SemaphoreType.DMA((n, 7)),
                        pltpu.SemaphoreType.DMA((n,))],
    )(*[a for _, _, a in flat])
    return list(outs)


HBM_SPEC = pl.BlockSpec(memory_space=pltpu.HBM)
SEM_SPEC = pl.BlockSpec(memory_space=pltpu.SEMAPHORE)
ANY_SPEC = pl.BlockSpec(memory_space=pl.ANY)
SPLIT_EFFECT = pltpu.SideEffectType.DATAFLOW_SIDE_EFFECTING


def _remote_copies(kind, ins, lands, send_sems, recv_sems):
    x, y, c = lax.axis_index("x"), lax.axis_index("y"), lax.axis_index("c")
    me = 4 * x + 2 * y + c
    out = []
    for a in range(len(ins)):
        for k in range(1, NDEV):
            px = 1 - x if (k >> 2) & 1 else x
            py = 1 - y if (k >> 1) & 1 else y
            pc = 1 - c if k & 1 else c
            src = ins[a] if kind == "gather" else ins[a].at[4 * px + 2 * py + pc]
            out.append(pltpu.make_async_remote_copy(
                src_ref=src, dst_ref=lands[a].at[me], send_sem=send_sems.at[a * (NDEV - 1) + k - 1],
                recv_sem=recv_sems.at[a * (NDEV - 1) + k - 1], device_id=(px, py, pc), device_id_type=MESH))
    return out


def _local_copies(kind, ins, lands, local_sems):
    me = 4 * lax.axis_index("x") + 2 * lax.axis_index("y") + lax.axis_index("c")
    return [pltpu.make_async_copy(ins[a] if kind == "gather" else ins[a].at[me], lands[a].at[me], local_sems.at[a])
            for a in range(len(ins))]


def _xfer_start(kind, arrs, after, name):
    n = len(arrs)
    lands = [lax.empty((NDEV,) + a.shape if kind == "gather" else a.shape, a.dtype) for a in arrs]

    def body(*refs):
        ins, lnd = refs[:n], refs[n:2 * n]
        send_sems, recv_sems, local_sems = refs[2 * n + 1:2 * n + 4]
        for cp in _remote_copies(kind, ins, lnd, send_sems, recv_sems) + _local_copies(kind, ins, lnd, local_sems):
            cp.start()
        refs[-1][...] = jnp.zeros_like(refs[-1])

    outs = pl.pallas_call(
        body, name=name,
        out_shape=(pltpu.SemaphoreType.DMA((n * (NDEV - 1),)), pltpu.SemaphoreType.DMA((n * (NDEV - 1),)),
                   pltpu.SemaphoreType.DMA((n,)),
                   *[pltpu.HBM(a.shape, a.dtype) for a in arrs + lands], jax.ShapeDtypeStruct((8, 128), F32)),
        in_specs=[HBM_SPEC] * (2 * n) + [ANY_SPEC],
        out_specs=(SEM_SPEC, SEM_SPEC, SEM_SPEC, *[HBM_SPEC] * (2 * n), pl.BlockSpec(memory_space=pltpu.VMEM)),
        input_output_aliases={a: 3 + a for a in range(2 * n)},
        compiler_params=pltpu.CompilerParams(has_side_effects=SPLIT_EFFECT),
    )(*[pltpu.with_memory_space_constraint(a, pltpu.HBM) for a in arrs + lands], after)
    return dict(kind=kind, n=n, sems=list(outs[:3]), bufs=list(outs[3:3 + 2 * n]), token=outs[-1], name=name)


def _xfer_wait(st, after):
    kind, n = st["kind"], st["n"]

    def body(*refs):
        ins, lnd = refs[:n], refs[n:2 * n]
        send_sems, recv_sems, local_sems = refs[2 * n:2 * n + 3]
        for cp in _remote_copies(kind, ins, lnd, send_sems, recv_sems):
            cp.wait_send()
            cp.wait_recv()
        for cp in _local_copies(kind, ins, lnd, local_sems):
            cp.wait()

    outs = pl.pallas_call(
        body, name=st["name"] + "_wait",
        out_shape=tuple(pltpu.HBM(b.shape, b.dtype) for b in st["bufs"]),
        in_specs=[HBM_SPEC] * (2 * n) + [SEM_SPEC] * 3 + [ANY_SPEC],
        out_specs=tuple([HBM_SPEC] * (2 * n)),
        input_output_aliases={a: a for a in range(2 * n)},
        compiler_params=pltpu.CompilerParams(has_side_effects=SPLIT_EFFECT),
    )(*st["bufs"], *st["sems"], after)
    return list(outs[n:])


def _mm(a, b, kind, *, name, out_dtype=F32, reduce_blocks=False):
    a, b = _as3(a), _as3(b)
    nba, nbb = a.shape[0], b.shape[0]
    nb = max(nba, nbb)
    assert nba in (1, nb) and nbb in (1, nb)
    if kind == "tn":
        t, m = a.shape[1:]
        n = b.shape[2]
        assert b.shape[1] == t and not reduce_blocks
        tm = m if m <= 1024 else _pick(m, (1024,))
        tn = n if n <= 1024 else _pick(n, (1024, 768, 512))
        tk = t if t * (tm + tn) * 2 <= MM_TILE_BYTES * 3 else _pick(t, (512, 768, 256))
        nred = t // tk
    else:
        m, k = a.shape[1:]
        n = b.shape[2] if kind == "nn" else b.shape[1]
        assert (b.shape[1] if kind == "nn" else b.shape[2]) == k
        tn = n if (n <= 1024 or k * n * 2 <= 2 * MM_TILE_BYTES) else _pick(n, (1024, 768, 512))
        tm = m
        for cand in (1024, 768, 512, 256):
            if m % cand == 0 and cand * tn * 4 <= MM_TILE_BYTES and cand * k * 2 <= MM_TILE_BYTES:
                tm = cand
                break
        nred = nb if reduce_blocks else 1
    nbo = 1 if reduce_blocks else nb

    def blk(nbx, g, r):
        if nbx == 1:
            return 0
        return r if reduce_blocks else g

    if kind == "nn":
        a_spec = pl.BlockSpec((1, tm, k), lambda g, j, i, r: (blk(nba, g, r), i, 0))
        b_spec = pl.BlockSpec((1, k, tn), lambda g, j, i, r: (blk(nbb, g, r), 0, j))
        dims = (((1,), (0,)), ((), ()))
    elif kind == "nt":
        a_spec = pl.BlockSpec((1, tm, k), lambda g, j, i, r: (blk(nba, g, r), i, 0))
        b_spec = pl.BlockSpec((1, tn, k), lambda g, j, i, r: (blk(nbb, g, r), j, 0))
        dims = (((1,), (1,)), ((), ()))
    else:
        a_spec = pl.BlockSpec((1, tk, tm), lambda g, j, i, r: (blk(nba, g, r), r, i))
        b_spec = pl.BlockSpec((1, tk, tn), lambda g, j, i, r: (blk(nbb, g, r), r, j))
        dims = (((0,), (0,)), ((), ()))
    o_spec = pl.BlockSpec((1, tm, tn), lambda g, j, i, r: (g, i, j))

    def body(a_ref, b_ref, o_ref, *scratch):
        prod = lax.dot_general(a_ref[0].astype(BF16), b_ref[0].astype(BF16), dims,
                               preferred_element_type=F32)
        if nred == 1:
            o_ref[0] = prod.astype(o_ref.dtype)
        else:
            acc = scratch[0]
            r = pl.program_id(3)

            @pl.when(r == 0)
            def _():
                acc[...] = prod

            @pl.when(r > 0)
            def _():
                acc[...] += prod

            @pl.when(r == nred - 1)
            def _():
                o_ref[0] = acc[...].astype(o_ref.dtype)

    return pl.pallas_call(
        body, name=name,
        out_shape=jax.ShapeDtypeStruct((nbo, m, n), out_dtype),
        grid=(nbo, n // tn, m // tm, nred),
        in_specs=[a_spec, b_spec], out_specs=o_spec,
        scratch_shapes=[pltpu.VMEM((tm, tn), F32)] if nred > 1 else [],
        compiler_params=_cparams(("parallel", "parallel", "parallel", "arbitrary")),
    )(a, b)


def _row_tile(t, widths):
    tm = max(16, ROW_TILE_ELEMS // max(widths))
    tm = min(tm, 256)
    return t if t < tm else tm


def _sel_index(sel, g, i, tm):
    if sel == "one":
        return 0
    if sel == "seg":
        return (i * tm) // L
    return g + sel


def _row_spec(arr, off, tm):
    return pl.BlockSpec((1, tm, arr.shape[2]), lambda g, i: (g + off, i, 0))


def _par_spec(arr, sel, tm):
    return pl.BlockSpec((1, 1, arr.shape[2]), lambda g, i: (_sel_index(sel, g, i, tm), 0, 0))


def _norm_ops(ops):
    return [(o[0], o[1], o[2] if len(o) > 2 else 1) for o in ops]


def _split_cols(vals, nsplit):
    out = []
    for v, ns in zip(vals, nsplit):
        w = v.shape[1] // ns
        out += [v] if ns == 1 else [v[:, q * w:(q + 1) * w] for q in range(ns)]
    return out


def _join_cols(flat, nsplit):
    out, pos = [], 0
    for ns in nsplit:
        out.append(flat[pos] if ns == 1 else jnp.concatenate(flat[pos:pos + ns], axis=1))
        pos += ns
    return out


def _rw_fwd(fn, rows, params, outs, *, name, nblk=None):
    rows, params = _norm_ops(rows), _norm_ops(params)
    t = rows[0][0].shape[1]
    nblk = nblk or rows[0][0].shape[0]
    tm = _row_tile(t, [r.shape[2] for r, _, _ in rows] + [w for w, _ in outs])
    nr, npar = len(rows), len(params)
    nsplit = [ns for _, _, ns in rows + params]

    def body(*refs):
        vals = _split_cols([r[0].astype(F32) for r in refs[:nr + npar]], nsplit)
        res = fn(*vals)
        for o_ref, o in zip(refs[nr + npar:], res):
            o_ref[0] = o.astype(o_ref.dtype)

    res = pl.pallas_call(
        body, name=name,
        out_shape=[jax.ShapeDtypeStruct((nblk, t, w), dt) for w, dt in outs],
        grid=(nblk, t // tm),
        in_specs=[_row_spec(r, off, tm) for r, off, _ in rows] + [_par_spec(p, s, tm) for p, s, _ in params],
        out_specs=[pl.BlockSpec((1, tm, w), lambda g, i: (g, i, 0)) for w, _ in outs],
        compiler_params=_cparams(("parallel", "parallel")),
    )(*[r for r, _, _ in rows], *[p for p, _, _ in params])
    return list(res)


def _rw_bwd(fn, rows, params, cts, *, name, row_grad=(), param_grad=(), add=None, nblk=None):
    rows, params = _norm_ops(rows), _norm_ops(params)
    t = cts[0].shape[1]
    nblk = nblk or cts[0].shape[0]
    tm = _row_tile(t, [r.shape[2] for r, _, _ in rows] + [c.shape[2] for c in cts])
    ni = t // tm
    nr, npar, nct = len(rows), len(params), len(cts)
    nadd = 0 if add is None else 1
    n_in = nr + npar + nct + nadd
    nsplit = [ns for _, _, ns in rows + params]

    def body(*refs):
        prim = _split_cols([r[0].astype(F32) for r in refs[:nr + npar]], nsplit)
        ct = tuple(r[0].astype(F32) for r in refs[nr + npar:nr + npar + nct])
        _, vjp = jax.vjp(fn, *prim)
        grads = _join_cols(list(vjp(ct)), nsplit)
        out_refs = refs[n_in:]
        for q, (ri, _) in enumerate(row_grad):
            gr = grads[ri]
            if q == 0 and nadd:
                gr = gr + refs[n_in - 1][0].astype(F32)
            out_refs[q][0] = gr.astype(out_refs[q].dtype)
        g, i = pl.program_id(0), pl.program_id(1)
        step = g * ni + i
        pg, pi = (step - 1) // ni, (step - 1) % ni
        for q, pidx in enumerate(param_grad):
            o_ref = out_refs[len(row_grad) + q]
            sel = params[pidx][1]
            val = grads[nr + pidx]
            if sel == "one":
                first = step == 0
            else:
                first = (step == 0) | (_sel_index(sel, g, i, tm) != _sel_index(sel, pg, pi, tm))

            @pl.when(first)
            def _(o_ref=o_ref, val=val):
                o_ref[0] = val

            @pl.when(jnp.logical_not(first))
            def _(o_ref=o_ref, val=val):
                o_ref[0] += val

    in_arrays = [r for r, _, _ in rows] + [p for p, _, _ in params] + list(cts) + ([add] if nadd else [])
    in_specs = ([_row_spec(r, off, tm) for r, off, _ in rows] + [_par_spec(p, s, tm) for p, s, _ in params]
                + [_row_spec(c, 0, tm) for c in cts] + ([_row_spec(add, 0, tm)] if nadd else []))
    out_shape, out_specs = [], []
    for ri, dt in row_grad:
        w = rows[ri][0].shape[2]
        out_shape.append(jax.ShapeDtypeStruct((nblk, t, w), dt))
        out_specs.append(pl.BlockSpec((1, tm, w), lambda g, i: (g, i, 0)))
    for pidx in param_grad:
        p, sel, _ = params[pidx]
        out_shape.append(jax.ShapeDtypeStruct(p.shape, F32))
        out_specs.append(_par_spec(p, sel, tm))
    res = pl.pallas_call(
        body, name=name, out_shape=out_shape, grid=(nblk, ni),
        in_specs=in_specs, out_specs=out_specs,
        compiler_params=_cparams(("arbitrary", "arbitrary")),
    )(*in_arrays)
    return list(res)


def _sigmoid(x):
    return 1.0 / (1.0 + jnp.exp(-x))


def _rms(x, g):
    return x * lax.rsqrt(jnp.mean(x * x, axis=-1, keepdims=True) + EPS) * g


def _ln(x, g, b):
    mu = jnp.mean(x, axis=-1, keepdims=True)
    xc = x - mu
    var = jnp.mean(xc * xc, axis=-1, keepdims=True)
    return xc * lax.rsqrt(var + EPS) * g + b


def _gelu_tanh(x):
    return 0.5 * x * (1.0 + jnp.tanh(0.7978845608028654 * (x + 0.044715 * (x * x * x))))


def f_modnorm(h, g, sc, sh):
    return (_rms(h, g) * (1.0 + sc) + sh,)


def f_gate_rms(y, gate, g):
    return (gate * _rms(y, g),)


def f_gate_rms_bias(y, gate, g, b):
    return (gate * _rms(y + b, g),)


def f_resgate(h, y, gate, g):
    return (h + gate * _rms(y, g),)


def f_resgate_bias(h, y, gate, g, b):
    return (h + gate * _rms(y + b, g),)


def f_glu(pa, pg, ba, bg):
    return ((pa + ba) * _sigmoid(pg + bg),)


def f_lnsilu(z, g, b):
    t = _ln(z, g, b)
    return (t * _sigmoid(t),)


def f_gmlp_pre(pu, pv, bu, bv, g, bb):
    return _gelu_tanh(pu + bu), _ln(_gelu_tanh(pv + bv), g, bb)


def f_ffn_gate(zg, zv):
    return (zg * _sigmoid(zg) * zv,)


def f_silu(x):
    return (x * _sigmoid(x),)


def f_silu_rows(dummy, cc):
    return (cc * _sigmoid(cc) + 0.0 * dummy,)


def _rope(x_in, tables, neg_sin, out_dtype, name):
    w = x_in.shape[2]
    sign = -1.0 if neg_sin else 1.0

    def fn(x, cos, sin):
        cos = jnp.tile(cos, (1, w // 128))
        sin = jnp.tile(sin, (1, w // 128)) * sign
        lane = lax.broadcasted_iota(jnp.int32, x.shape, 1) & 31
        rot = jnp.where(lane < 16, -pltpu.roll(x, w - 16, 1), pltpu.roll(x, 16, 1))
        return (x * cos + rot * sin,)

    return _rw_fwd(fn, [(x_in, 0), (tables[0], 0), (tables[1], 0)], [], [(w, out_dtype)], name=name)[0]


CONV_TM = 256
CONV_RC = 32


def _conv_geometry(x, k):
    nb, t, w = x.shape
    halo = 16 if k > 17 else 8
    cb = _pick(w, (512,)) if w > 768 else w
    return nb, t, w, halo, cb, (k - 1) // 2


def _conv_in_specs(t, halo, cb):
    per = CONV_TM // halo
    last = t // halo - 1
    return [
        pl.BlockSpec((1, CONV_TM, cb), lambda g, jc, i: (g, i, jc)),
        pl.BlockSpec((1, halo, cb), lambda g, jc, i: (g, jnp.maximum(i * per - 1, 0), jc)),
        pl.BlockSpec((1, halo, cb), lambda g, jc, i: (g, jnp.minimum((i + 1) * per, last), jc)),
    ]


def _conv_fill(xp, x_ref, prev_ref, next_ref, halo, t):
    i = pl.program_id(2)
    seg_first = (i * CONV_TM == 0) | (i * CONV_TM == L)
    seg_last = ((i + 1) * CONV_TM == L) | ((i + 1) * CONV_TM == t)
    xp[0:halo, :] = jnp.where(seg_first, 0.0, prev_ref[0].astype(F32))
    xp[halo:halo + CONV_TM, :] = x_ref[0].astype(F32)
    xp[halo + CONV_TM:, :] = jnp.where(seg_last, 0.0, next_ref[0].astype(F32))


def _dwconv(x, w, b, *, name, out_dtype=F32):
    k = w.shape[1]
    nb, t, wd, halo, cb, half = _conv_geometry(x, k)
    base = halo - half

    def body(*refs):
        x_ref, prev_ref, next_ref, w_ref = refs[:4]
        b_ref = refs[4] if b is not None else None
        o_ref, xp = refs[-2], refs[-1]
        _conv_fill(xp, x_ref, prev_ref, next_ref, halo, t)
        for r0 in range(0, CONV_TM, CONV_RC):
            acc = jnp.zeros((CONV_RC, cb), F32)
            for kk in range(k):
                acc = acc + w_ref[0, kk:kk + 1, :] * xp[r0 + base + kk:r0 + base + kk + CONV_RC, :]
            if b_ref is not None:
                acc = acc + b_ref[0]
            o_ref[0, r0:r0 + CONV_RC, :] = acc.astype(o_ref.dtype)

    in_specs = _conv_in_specs(t, halo, cb) + [pl.BlockSpec((1, k, cb), lambda g, jc, i: (g, 0, jc))]
    args = [x, x, x, w]
    if b is not None:
        in_specs.append(pl.BlockSpec((1, 1, cb), lambda g, jc, i: (g, 0, jc)))
        args.append(b)
    return pl.pallas_call(
        body, name=name, out_shape=jax.ShapeDtypeStruct((nb, t, wd), out_dtype),
        grid=(nb, wd // cb, t // CONV_TM), in_specs=in_specs,
        out_specs=pl.BlockSpec((1, CONV_TM, cb), lambda g, jc, i: (g, i, jc)),
        scratch_shapes=[pltpu.VMEM((CONV_TM + 2 * halo, cb), F32)],
        compiler_params=_cparams(("parallel", "parallel", "parallel")),
    )(*args)


def _dwconv_wgrad(x, dy, k, *, name):
    nb, t, wd, halo, cb, half = _conv_geometry(x, k)
    base = halo - half

    def body(x_ref, prev_ref, next_ref, dy_ref, dw_ref, db_ref, xp):
        _conv_fill(xp, x_ref, prev_ref, next_ref, halo, t)
        i = pl.program_id(2)

        @pl.when(i == 0)
        def _():
            dw_ref[...] = jnp.zeros_like(dw_ref)
            db_ref[...] = jnp.zeros_like(db_ref)

        dyv = dy_ref[0].astype(F32)
        db_ref[0] += jnp.sum(dyv, axis=0, keepdims=True)
        for kk in range(k):
            dw_ref[0, kk:kk + 1, :] += jnp.sum(dyv * xp[base + kk:base + kk + CONV_TM, :], axis=0, keepdims=True)

    dw, db = pl.pallas_call(
        body, name=name,
        out_shape=[jax.ShapeDtypeStruct((nb, k, wd), F32), jax.ShapeDtypeStruct((nb, 1, wd), F32)],
        grid=(nb, wd // cb, t // CONV_TM),
        in_specs=_conv_in_specs(t, halo, cb) + [pl.BlockSpec((1, CONV_TM, cb), lambda g, jc, i: (g, i, jc))],
        out_specs=[pl.BlockSpec((1, k, cb), lambda g, jc, i: (g, 0, jc)),
                   pl.BlockSpec((1, 1, cb), lambda g, jc, i: (g, 0, jc))],
        scratch_shapes=[pltpu.VMEM((CONV_TM + 2 * halo, cb), F32)],
        compiler_params=_cparams(("parallel", "parallel", "arbitrary")),
    )(x, x, x, dy)
    return dw, db


ATTN_SCALE = HEAD_DIM ** -0.5
QROWS = Q_PER_KV * ATTN_BLOCK
NEG = -1e30


def _attn_scores(q, kw, kc, n):
    nt = (((1,), (1,)), ((), ()))
    s_w = lax.dot_general(q, kw, nt, preferred_element_type=F32) * ATTN_SCALE
    qi = lax.broadcasted_iota(jnp.int32, s_w.shape, 0) & (ATTN_BLOCK - 1)
    kj = lax.broadcasted_iota(jnp.int32, s_w.shape, 1)
    key_abs = (n - 1) * ATTN_BLOCK + kj
    ok = (jnp.abs(qi + ATTN_BLOCK - kj) <= ATTN_BLOCK) & (key_abs >= 0) & (key_abs < L)
    s_w = jnp.where(ok, s_w, NEG)
    s_c = lax.dot_general(q, kc, nt, preferred_element_type=F32) * ATTN_SCALE
    return s_w, s_c


def _sink_col(sink_ref, hk):
    return jnp.concatenate([jnp.full((ATTN_BLOCK, 1), sink_ref[hk * Q_PER_KV + g], F32) for g in range(Q_PER_KV)], axis=0)


def _attn_specs():
    qspec = pl.BlockSpec((Q_PER_KV, ATTN_BLOCK, HEAD_DIM), lambda hk, n: (hk, n, 0))
    kspec = pl.BlockSpec((1, L + 2 * ATTN_BLOCK, HEAD_DIM), lambda hk, n: (hk, 0, 0))
    cspec = pl.BlockSpec((1, LC, HEAD_DIM), lambda hk, n: (hk, 0, 0))
    lspec = pl.BlockSpec((Q_PER_KV, ATTN_BLOCK, 1), lambda hk, n: (hk, n, 0))
    sspec = pl.BlockSpec(memory_space=pltpu.SMEM)
    return qspec, kspec, cspec, lspec, sspec


def _attn_fwd(q, k, v, kc, vc, sink):
    qspec, kspec, cspec, lspec, sspec = _attn_specs()

    def body(q_ref, k_ref, v_ref, kc_ref, vc_ref, sink_ref, o_ref, lse_ref):
        hk, n = pl.program_id(0), pl.program_id(1)
        qv = q_ref[...].reshape(QROWS, HEAD_DIM)
        start = pl.multiple_of(n * ATTN_BLOCK, ATTN_BLOCK)
        kw = k_ref[0, pl.ds(start, 3 * ATTN_BLOCK), :]
        vw = v_ref[0, pl.ds(start, 3 * ATTN_BLOCK), :]
        s_w, s_c = _attn_scores(qv, kw, kc_ref[0], n)
        sk = _sink_col(sink_ref, hk)
        m = jnp.maximum(jnp.maximum(jnp.max(s_w, -1, keepdims=True), jnp.max(s_c, -1, keepdims=True)), sk)
        p_w, p_c = jnp.exp(s_w - m), jnp.exp(s_c - m)
        den = jnp.sum(p_w, -1, keepdims=True) + jnp.sum(p_c, -1, keepdims=True) + jnp.exp(sk - m)
        o = (jnp.dot(p_w.astype(BF16), vw, preferred_element_type=F32)
             + jnp.dot(p_c.astype(BF16), vc_ref[0], preferred_element_type=F32)) / den
        o_ref[...] = o.reshape(Q_PER_KV, ATTN_BLOCK, HEAD_DIM).astype(o_ref.dtype)
        lse_ref[...] = (m + jnp.log(den)).reshape(Q_PER_KV, ATTN_BLOCK, 1)

    return pl.pallas_call(
        body, name="attn_fwd",
        out_shape=[jax.ShapeDtypeStruct((N_Q, L, HEAD_DIM), BF16), jax.ShapeDtypeStruct((N_Q, L, 1), F32)],
        grid=(N_KV, L // ATTN_BLOCK),
        in_specs=[qspec, kspec, kspec, cspec, cspec, sspec], out_specs=[qspec, lspec],
        compiler_params=_cparams(("parallel", "parallel")),
    )(q, k, v, kc, vc, sink)


def _attn_bwd(q, k, v, kc, vc, sink, o, lse, do):
    qspec, kspec, cspec, lspec, sspec = _attn_specs()
    tn = (((0,), (0,)), ((), ()))
    nt = (((1,), (1,)), ((), ()))

    def body(q_ref, k_ref, v_ref, kc_ref, vc_ref, sink_ref, o_ref, lse_ref, do_ref,
             dq_ref, dk_ref, dv_ref, dkc_ref, dvc_ref, dsink_ref):
        hk, n = pl.program_id(0), pl.program_id(1)
        qv = q_ref[...].reshape(QROWS, HEAD_DIM)
        start = pl.multiple_of(n * ATTN_BLOCK, ATTN_BLOCK)
        win = pl.ds(start, 3 * ATTN_BLOCK)
        kw, vw = k_ref[0, win, :], v_ref[0, win, :]
        kcv, vcv = kc_ref[0], vc_ref[0]
        s_w, s_c = _attn_scores(qv, kw, kcv, n)
        lse_v = lse_ref[...].reshape(QROWS, 1)
        p_w, p_c = jnp.exp(s_w - lse_v), jnp.exp(s_c - lse_v)
        dov = do_ref[...].reshape(QROWS, HEAD_DIM).astype(F32)
        ov = o_ref[...].reshape(QROWS, HEAD_DIM).astype(F32)
        delta = jnp.sum(dov * ov, -1, keepdims=True)
        dob = dov.astype(BF16)
        dp_w = lax.dot_general(dob, vw, nt, preferred_element_type=F32)
        dp_c = lax.dot_general(dob, vcv, nt, preferred_element_type=F32)
        ds_w = (p_w * (dp_w - delta) * ATTN_SCALE).astype(BF16)
        ds_c = (p_c * (dp_c - delta) * ATTN_SCALE).astype(BF16)
        dq = jnp.dot(ds_w, kw, preferred_element_type=F32) + jnp.dot(ds_c, kcv, preferred_element_type=F32)
        dq_ref[...] = dq.reshape(Q_PER_KV, ATTN_BLOCK, HEAD_DIM)

        @pl.when(n == 0)
        def _():
            dk_ref[...] = jnp.zeros_like(dk_ref)
            dv_ref[...] = jnp.zeros_like(dv_ref)
            dkc_ref[...] = jnp.zeros_like(dkc_ref)
            dvc_ref[...] = jnp.zeros_like(dvc_ref)

        dk_ref[0, win, :] += lax.dot_general(ds_w, qv, tn, preferred_element_type=F32)
        dv_ref[0, win, :] += lax.dot_general(p_w.astype(BF16), dob, tn, preferred_element_type=F32)
        dkc_ref[0] += lax.dot_general(ds_c, qv, tn, preferred_element_type=F32)
        dvc_ref[0] += lax.dot_general(p_c.astype(BF16), dob, tn, preferred_element_type=F32)
        dsk = -jnp.exp(_sink_col(sink_ref, hk) - lse_v) * delta
        for g in range(Q_PER_KV):
            part = jnp.sum(dsk[g * ATTN_BLOCK:(g + 1) * ATTN_BLOCK])
            idx = hk * Q_PER_KV + g

            @pl.when(n == 0)
            def _(part=part, idx=idx):
                dsink_ref[idx] = part

            @pl.when(n > 0)
            def _(part=part, idx=idx):
                dsink_ref[idx] += part

    kshape = jax.ShapeDtypeStruct((N_KV, L + 2 * ATTN_BLOCK, HEAD_DIM), F32)
    cshape = jax.ShapeDtypeStruct((N_KV, LC, HEAD_DIM), F32)
    return pl.pallas_call(
        body, name="attn_bwd",
        out_shape=[jax.ShapeDtypeStruct((N_Q, L, HEAD_DIM), F32), kshape, kshape, cshape, cshape,
                   jax.ShapeDtypeStruct((N_Q,), F32)],
        grid=(N_KV, L // ATTN_BLOCK),
        in_specs=[qspec, kspec, kspec, cspec, cspec, sspec, qspec, lspec, qspec],
        out_specs=[qspec, kspec, kspec, cspec, cspec, sspec],
        compiler_params=_cparams(("arbitrary", "arbitrary")),
    )(q, k, v, kc, vc, sink, o, lse, do)


def _gm_specs():
    rspec = pl.BlockSpec((1, GM_CHUNK, GM_W), lambda n: (0, n, 0))
    wspec = pl.BlockSpec((GM_GROUPS, GM_CHUNK, GM_CHUNK), lambda n: (0, 0, 0))
    bspec = pl.BlockSpec((GM_GROUPS, GM_CHUNK, 1), lambda n: (0, 0, 0))
    return rspec, wspec, bspec


def _gm_spatial_fwd(u, v, ws, bs):
    rspec, wspec, bspec = _gm_specs()

    def body(u_ref, v_ref, ws_ref, bs_ref, o_ref):
        for g in range(GM_GROUPS):
            cols = slice(g * GM_CHUNK, (g + 1) * GM_CHUNK)
            s = jnp.dot(ws_ref[g], v_ref[0, :, cols], preferred_element_type=F32) + bs_ref[g]
            o_ref[0, :, cols] = (u_ref[0, :, cols] * s).astype(o_ref.dtype)

    return pl.pallas_call(
        body, name="gm_spatial_fwd", out_shape=jax.ShapeDtypeStruct((1, L, GM_W), BF16),
        grid=(L // GM_CHUNK,), in_specs=[rspec, rspec, wspec, bspec], out_specs=rspec,
        compiler_params=_cparams(("parallel",)),
    )(u, v, ws, bs)


def _gm_spatial_bwd(u, v, ws, bs, dus):
    rspec, wspec, bspec = _gm_specs()
    tn = (((0,), (0,)), ((), ()))
    nt = (((1,), (1,)), ((), ()))

    def body(u_ref, v_ref, ws_ref, bs_ref, d_ref, du_ref, dv_ref, dws_ref, dbs_ref):
        n = pl.program_id(0)

        @pl.when(n == 0)
        def _():
            dws_ref[...] = jnp.zeros_like(dws_ref)
            dbs_ref[...] = jnp.zeros_like(dbs_ref)

        for g in range(GM_GROUPS):
            cols = slice(g * GM_CHUNK, (g + 1) * GM_CHUNK)
            vb = v_ref[0, :, cols]
            s = jnp.dot(ws_ref[g], vb, preferred_element_type=F32) + bs_ref[g]
            d = d_ref[0, :, cols].astype(F32)
            du_ref[0, :, cols] = d * s
            ds = d * u_ref[0, :, cols]
            dsb = ds.astype(BF16)
            dv_ref[0, :, cols] = lax.dot_general(ws_ref[g], dsb, tn, preferred_element_type=F32)
            dws_ref[g] += lax.dot_general(dsb, vb, nt, preferred_element_type=F32)
            dbs_ref[g] += jnp.sum(ds, axis=1, keepdims=True)

    row = jax.ShapeDtypeStruct((1, L, GM_W), F32)
    return pl.pallas_call(
        body, name="gm_spatial_bwd",
        out_shape=[row, row, jax.ShapeDtypeStruct((GM_GROUPS, GM_CHUNK, GM_CHUNK), F32),
                   jax.ShapeDtypeStruct((GM_GROUPS, GM_CHUNK, 1), F32)],
        grid=(L // GM_CHUNK,), in_specs=[rspec, rspec, wspec, bspec, rspec],
        out_specs=[rspec, rspec, wspec, bspec],
        compiler_params=_cparams(("arbitrary",)),
    )(u, v, ws, bs, dus)


def _loss_head(h, target):
    tm = 256

    def body(h_ref, t_ref, dh_ref, loss_ref):
        d = h_ref[0] - t_ref[0]
        dh_ref[0] = d * (1.0 / D)

        @pl.when(pl.program_id(0) == 0)
        def _():
            loss_ref[...] = jnp.zeros_like(loss_ref)

        loss_ref[...] += jnp.sum(d * d) * (0.5 / D)

    spec = pl.BlockSpec((1, tm, D), lambda i: (0, i, 0))
    dh, loss = pl.pallas_call(
        body, name="loss_head",
        out_shape=[jax.ShapeDtypeStruct((1, L, D), F32), jax.ShapeDtypeStruct((8, 128), F32)],
        grid=(L // tm,), in_specs=[spec, spec],
        out_specs=[spec, pl.BlockSpec((8, 128), lambda i: (0, 0))],
        compiler_params=_cparams(("arbitrary",)),
    )(h, target)
    return dh, loss[0, 0]


def _adamw(parts, w, m, v, name):
    per_layer = isinstance(parts, (list, tuple))
    plist = list(parts) if per_layer else [parts]
    nl = len(plist) if per_layer else parts.shape[0]
    s, r, c = plist[0].shape[-3:]
    tr = r
    for cand in (512, 256, 128, 64, 32, 16):
        if r % cand == 0 and cand * c <= 131072:
            tr = cand
            break
    nr = r // tr
    npart = len(plist)
    c1 = 1.0 / (1.0 - ADAM_B1 ** ADAM_STEP)
    c2 = 1.0 / (1.0 - ADAM_B2 ** ADAM_STEP)

    def body(*refs):
        w_ref, m_ref, v_ref, g_ref, d_ref, nm_ref, nv_ref = refs[npart:]

        def update(read):
            g = read(0).astype(F32)
            for q in range(1, s):
                g = g + read(q).astype(F32)
            mn = ADAM_B1 * m_ref[0] + (1.0 - ADAM_B1) * g
            vn = ADAM_B2 * v_ref[0] + (1.0 - ADAM_B2) * (g * g)
            g_ref[0] = g
            nm_ref[0] = mn
            nv_ref[0] = vn
            d_ref[0] = -ADAM_LR * ((mn * c1) / (jnp.sqrt(vn * c2) + ADAM_EPS) + ADAM_WD * w_ref[0])

        if not per_layer:
            update(lambda q: refs[0][0, q])
        else:
            for l in range(nl):
                @pl.when(pl.program_id(0) == l)
                def _(l=l):
                    update(lambda q: refs[l][q])

    spec = pl.BlockSpec((1, tr, c), lambda li, i: (li, i, 0))
    shp = jax.ShapeDtypeStruct((nl, r, c), F32)
    if per_layer:
        pspecs = [pl.BlockSpec((s, tr, c), lambda li, i, l=l: (0, jnp.where(li == l, i, jnp.where(li > l, nr - 1, 0)), 0))
                  for l in range(nl)]
    else:
        pspecs = [pl.BlockSpec((1, s, tr, c), lambda li, i: (li, 0, i, 0))]
    return pl.pallas_call(
        body, name=name, out_shape=[shp] * 4, grid=(nl, nr),
        in_specs=pspecs + [spec, spec, spec], out_specs=[spec] * 4,
        compiler_params=_cparams(("arbitrary", "arbitrary")),
    )(*plist, w, m, v)


def _pack_rows(vecs, lanes=128, mult=8):
    flat = jnp.concatenate([v.reshape(-1) for v in vecs])
    n = flat.shape[0]
    rows = -(-n // (mult * lanes)) * mult
    return jnp.pad(flat, (0, rows * lanes - n)).reshape(rows, lanes)


def _unpack_rows(packed, shapes):
    flat = packed.reshape(-1)
    out, pos = [], 0
    for s in shapes:
        n = 1
        for d_ in s:
            n *= d_
        out.append(flat[pos:pos + n].reshape(s))
        pos += n
    return out


def _unshard_last(g):
    lead = g.shape[1:-1]
    return jnp.moveaxis(g, 0, -2).reshape(*lead, NDEV * g.shape[-1])


def _shard_last(full):
    lead, w = full.shape[:-1], full.shape[-1] // NDEV
    return jnp.moveaxis(full.reshape(*lead, NDEV, w), -2, 0)


def _rope_tables():
    rows = L // GRID_W
    row = jnp.repeat(jnp.arange(rows), GRID_W).astype(F32)
    col = jnp.tile(jnp.arange(GRID_W), rows).astype(F32)
    axis_dim = HEAD_DIM // 2
    inv_freq = ROPE_BASE ** (-jnp.arange(0, axis_dim, 2, dtype=F32) / axis_dim)
    ang_r, ang_c = row[:, None] * inv_freq[None, :], col[:, None] * inv_freq[None, :]
    ang = jnp.concatenate([ang_r, ang_r, ang_c, ang_c], axis=-1)
    ang = jnp.concatenate([ang, ang], axis=-1)[None]
    return jnp.cos(ang), jnp.sin(ang)


def _heads(x, nh):
    t = x.shape[1]
    return x.reshape(t, nh, HEAD_DIM).transpose(1, 0, 2)


def _unheads(x):
    nh, t, _ = x.shape
    return x.transpose(1, 0, 2).reshape(1, t, nh * HEAD_DIM)


FFN_HALO = 16
FFN_PAIRS = 4


def _ffn_tile(t):
    return 512 if t == L else 256


def _halo_specs(t, tm, block, index):
    per, last = tm // FFN_HALO, t // FFN_HALO - 1
    return [pl.BlockSpec(block(tm), lambda d, i: index(d, i)),
            pl.BlockSpec(block(FFN_HALO), lambda d, i: index(d, jnp.maximum(i * per - 1, 0))),
            pl.BlockSpec(block(FFN_HALO), lambda d, i: index(d, jnp.minimum((i + 1) * per, last)))]


def _seg_edges(i, tm, t):
    return (i * tm == 0) | (i * tm == L), ((i + 1) * tm == L) | ((i + 1) * tm == t)


FFN_RC = 8


def _sigmoid_t(x):
    return 0.5 * jnp.tanh(0.5 * x) + 0.5


class _RowShifts:
    def __init__(self, buf, s):
        self.buf, self.s, self.memo = buf, s, {}
        rows = lax.broadcasted_iota(jnp.int32, (FFN_RC, FFN_BLK), 0)
        self.first, self.last = rows == 0, rows == FFN_RC - 1

    def chunk(self, r):
        if r not in self.memo:
            self.memo[r] = self.buf[self.s, r:r + FFN_RC, :]
        return self.memo[r]

    def rot(self, r, by):
        if (r, by) not in self.memo:
            self.memo[(r, by)] = pltpu.roll(self.chunk(r), by, 0)
        return self.memo[(r, by)]

    def triple(self, r):
        before = jnp.where(self.first, self.rot(r - FFN_RC, 1), self.rot(r, 1))
        behind = jnp.where(self.last, self.rot(r + FFN_RC, FFN_RC - 1), self.rot(r, FFN_RC - 1))
        return before, self.chunk(r), behind


def _conv3_of(triple, cw_ref, s, flip=False):
    taps = [cw_ref[s, 0, k:k + 1, :] for k in ((2, 1, 0) if flip else (0, 1, 2))]
    return taps[0] * triple[0] + taps[1] * triple[1] + taps[2] * triple[2]


def _ffn_core_fwd(a2, up, cw, cb, down, name):
    t = a2.shape[1]
    tm = _ffn_tile(t)
    h0 = FFN_HALO

    def body(a_ref, ap_ref, an_ref, up_ref, cw_ref, cb_ref, dn_ref, z_ref, f_ref, abuf, zbuf, ubuf):
        d, i = pl.program_id(0), pl.program_id(1)
        seg_first, seg_last = _seg_edges(i, tm, t)
        abuf[0:h0, :] = ap_ref[0]
        abuf[h0:h0 + tm, :] = a_ref[0]
        abuf[h0 + tm:, :] = an_ref[0]
        for s in range(2):
            zbuf[s] = jnp.dot(abuf[...], up_ref[s, 0], preferred_element_type=F32)

        @pl.when(seg_first)
        def _():
            zbuf[:, 0:h0, :] = jnp.zeros((2, h0, FFN_BLK), F32)

        @pl.when(seg_last)
        def _():
            zbuf[:, h0 + tm:, :] = jnp.zeros((2, h0, FFN_BLK), F32)

        for s in range(2):
            z_ref[s, 0] = zbuf[s, h0:h0 + tm, :].astype(z_ref.dtype)

        zs = [_RowShifts(zbuf, 0), _RowShifts(zbuf, 1)]
        for r in range(h0, h0 + tm, FFN_RC):
            zg = _conv3_of(zs[0].triple(r), cw_ref, 0) + cb_ref[0, 0]
            zv = _conv3_of(zs[1].triple(r), cw_ref, 1) + cb_ref[1, 0]
            ubuf[r - h0:r - h0 + FFN_RC, :] = zg * _sigmoid_t(zg) * zv
        prod = jnp.dot(ubuf[...].astype(BF16), dn_ref[0], preferred_element_type=F32)
        rows = pl.ds(pl.multiple_of(i * tm, tm), tm)

        @pl.when(d == 0)
        def _():
            f_ref[0, rows, :] = prod

        @pl.when(d > 0)
        def _():
            f_ref[0, rows, :] += prod

    pair = lambda r, c: pl.BlockSpec((2, 1, r, c), lambda d, i: (0, d, 0, 0))
    return pl.pallas_call(
        body, name=name,
        out_shape=[jax.ShapeDtypeStruct((2, FFN_PAIRS, t, FFN_BLK), BF16), jax.ShapeDtypeStruct((1, t, D), F32)],
        grid=(FFN_PAIRS, t // tm),
        in_specs=_halo_specs(t, tm, lambda r: (1, r, D), lambda d, i: (0, i, 0))
        + [pair(D, FFN_BLK), pair(FFN_K, FFN_BLK), pair(1, FFN_BLK),
           pl.BlockSpec((1, FFN_BLK, D), lambda d, i: (d, 0, 0))],
        out_specs=[pl.BlockSpec((2, 1, tm, FFN_BLK), lambda d, i: (0, d, i, 0)),
                   pl.BlockSpec((1, t, D), lambda d, i: (0, 0, 0))],
        scratch_shapes=[pltpu.VMEM((tm + 2 * h0, D), BF16), pltpu.VMEM((2, tm + 2 * h0, FFN_BLK), F32),
                        pltpu.VMEM((tm, FFN_BLK), F32)],
        compiler_params=_cparams(("arbitrary", "arbitrary")),
    )(a2, a2, a2, up, cw, cb, down)


def _ffn_core_bwd(df, z, cw, cb, down, up, a2, name):
    t = df.shape[1]
    tm = _ffn_tile(t)
    h0 = FFN_HALO
    ni = t // tm
    w0, wn = h0 // 2, tm + h0
    tn = (((0,), (0,)), ((), ()))
    nt = (((1,), (1,)), ((), ()))

    def body(df_ref, dfp_ref, dfn_ref, z_ref, zp_ref, zn_ref, cw_ref, cb_ref, dn_ref, up_ref, a2_ref,
             dcw_ref, dcb_ref, ddn_ref, da_ref, dup_ref, dfbuf, zbuf, dzbuf, acc, dubuf, dzo, acc_up):
        d, i = pl.program_id(0), pl.program_id(1)
        seg_first, seg_last = _seg_edges(i, tm, t)
        dfbuf[0:h0, :] = dfp_ref[0]
        dfbuf[h0:h0 + tm, :] = df_ref[0]
        dfbuf[h0 + tm:, :] = dfn_ref[0]
        for s in range(2):
            zbuf[s, 0:h0, :] = zp_ref[s, 0].astype(F32)
            zbuf[s, h0:h0 + tm, :] = z_ref[s, 0].astype(F32)
            zbuf[s, h0 + tm:, :] = zn_ref[s, 0].astype(F32)

        @pl.when(seg_first)
        def _():
            zbuf[:, 0:h0, :] = jnp.zeros((2, h0, FFN_BLK), F32)

        @pl.when(seg_last)
        def _():
            zbuf[:, h0 + tm:, :] = jnp.zeros((2, h0, FFN_BLK), F32)

        dubuf[...] = lax.dot_general(dfbuf[...], dn_ref[0], nt, preferred_element_type=F32)

        zs = [_RowShifts(zbuf, 0), _RowShifts(zbuf, 1)]
        sums = [[jnp.zeros((FFN_RC, FFN_BLK), F32)] * (FFN_K + 1) for _ in range(2)]
        for r in range(w0, w0 + wn, FFN_RC):
            tz = [zs[0].triple(r), zs[1].triple(r)]
            zg = _conv3_of(tz[0], cw_ref, 0) + cb_ref[0, 0]
            zv = _conv3_of(tz[1], cw_ref, 1) + cb_ref[1, 0]
            sg = _sigmoid_t(zg)
            silu = zg * sg
            du = dubuf[r:r + FFN_RC, :]
            dzc = [du * zv * (sg * (1.0 + zg * (1.0 - sg))), du * silu]
            dzbuf[0, r:r + FFN_RC, :] = dzc[0]
            dzbuf[1, r:r + FFN_RC, :] = dzc[1]
            dubuf[r:r + FFN_RC, :] = silu * zv
            if h0 <= r < h0 + tm:
                for s in range(2):
                    sums[s] = [sums[s][k] + dzc[s] * tz[s][k] for k in range(FFN_K)] + [sums[s][FFN_K] + dzc[s]]

        @pl.when(seg_first)
        def _():
            dzbuf[:, w0:h0, :] = jnp.zeros((2, h0 - w0, FFN_BLK), F32)

        @pl.when(seg_last)
        def _():
            dzbuf[:, h0 + tm:w0 + wn, :] = jnp.zeros((2, w0, FFN_BLK), F32)

        @pl.when(i == 0)
        def _():
            dcw_ref[...] = jnp.zeros_like(dcw_ref)
            dcb_ref[...] = jnp.zeros_like(dcb_ref)

        da = None
        for s in range(2):
            dzs = _RowShifts(dzbuf, s)
            for r in range(h0, h0 + tm, FFN_RC):
                dzo[r - h0:r - h0 + FFN_RC, :] = _conv3_of(dzs.triple(r), cw_ref, s, flip=True)
            dzb = dzo[...].astype(BF16)
            part = lax.dot_general(dzb, up_ref[s, 0], nt, preferred_element_type=F32)
            da = part if da is None else da + part
            gup = lax.dot_general(dzb, a2_ref[0], tn, preferred_element_type=F32)

            @pl.when(i == 0)
            def _(s=s, gup=gup):
                acc_up[s] = gup

            @pl.when(i > 0)
            def _(s=s, gup=gup):
                acc_up[s] += gup
            for k in range(FFN_K):
                dcw_ref[s, 0, k:k + 1, :] += jnp.sum(sums[s][k], axis=0, keepdims=True)
            dcb_ref[s, 0] += jnp.sum(sums[s][FFN_K], axis=0, keepdims=True)
        rows = pl.ds(pl.multiple_of(i * tm, tm), tm)

        @pl.when(d == 0)
        def _():
            da_ref[0, rows, :] = da

        @pl.when(d > 0)
        def _():
            da_ref[0, rows, :] += da

        prod = lax.dot_general(dubuf[h0:h0 + tm, :].astype(BF16), dfbuf[h0:h0 + tm, :], tn, preferred_element_type=F32)

        @pl.when(i == 0)
        def _():
            acc[...] = prod

        @pl.when(i > 0)
        def _():
            acc[...] += prod

        @pl.when(i == ni - 1)
        def _():
            ddn_ref[0] = acc[...].astype(ddn_ref.dtype)
            dup_ref[:, 0] = acc_up[...].astype(dup_ref.dtype)

    pair = lambda r, c: pl.BlockSpec((2, 1, r, c), lambda d, i: (0, d, 0, 0))
    return pl.pallas_call(
        body, name=name,
        out_shape=[jax.ShapeDtypeStruct((2, FFN_PAIRS, FFN_K, FFN_BLK), F32),
                   jax.ShapeDtypeStruct((2, FFN_PAIRS, 1, FFN_BLK), F32),
                   jax.ShapeDtypeStruct((FFN_PAIRS, FFN_BLK, D), BF16), jax.ShapeDtypeStruct((1, t, D), F32),
                   jax.ShapeDtypeStruct((2, FFN_PAIRS, FFN_BLK, D), BF16)],
        grid=(FFN_PAIRS, ni),
        in_specs=_halo_specs(t, tm, lambda r: (1, r, D), lambda d, i: (0, i, 0))
        + _halo_specs(t, tm, lambda r: (2, 1, r, FFN_BLK), lambda d, i: (0, d, i, 0))
        + [pair(FFN_K, FFN_BLK), pair(1, FFN_BLK), pl.BlockSpec((1, FFN_BLK, D), lambda d, i: (d, 0, 0)),
           pair(D, FFN_BLK), pl.BlockSpec((1, tm, D), lambda d, i: (0, i, 0))],
        out_specs=[pair(FFN_K, FFN_BLK),
                   pair(1, FFN_BLK), pl.BlockSpec((1, FFN_BLK, D), lambda d, i: (d, 0, 0)),
                   pl.BlockSpec((1, t, D), lambda d, i: (0, 0, 0)), pair(FFN_BLK, D)],
        scratch_shapes=[pltpu.VMEM((tm + 2 * h0, D), BF16), pltpu.VMEM((2, tm + 2 * h0, FFN_BLK), F32),
                        pltpu.VMEM((2, tm + 2 * h0, FFN_BLK), F32), pltpu.VMEM((FFN_BLK, D), F32),
                        pltpu.VMEM((tm + 2 * h0, FFN_BLK), F32),
                        pltpu.VMEM((tm, FFN_BLK), F32), pltpu.VMEM((2, FFN_BLK, D), F32)],
        compiler_params=_cparams(("arbitrary", "arbitrary")),
    )(df, df, df, z, z, z, cw, cb, down, up, a2)


def _ffn_fwd(i, h, mod, ng, wts):
    a2 = _rw_fwd(f_modnorm, [(h, 0)], [(ng[2], "one"), (mod["sc2"], "seg"), (mod["sh2"], "seg")],
                 [(D, BF16)], name=f"ffn{i}_norm")[0]
    z, f = _ffn_core_fwd(a2, wts["up"], wts["cw"], wts["cb"], wts["down"], f"ffn{i}_core")
    h2 = _rw_fwd(f_resgate, [(h, 0), (f, 0)], [(mod["g2"], "seg"), (ng[3], "one")], [(D, F32)],
                 name=f"ffn{i}_res")[0]
    return h2, (h, a2, z, f)


def _ffn_bwd(i, dh, res, mod, ng, wts):
    h, a2, z, f = res
    t = h.shape[1]
    df, dg2, dng3 = _rw_bwd(f_gate_rms, [(f, 0)], [(mod["g2"], "seg"), (ng[3], "one")], [dh],
                            name=f"ffn{i}_res_b", row_grad=[(0, BF16)], param_grad=[0, 1])
    dcw, dcb, d_down, da2, d_up = _ffn_core_bwd(df, z, wts["cw"], wts["cb"], wts["down"], wts["up"], a2,
                                                f"ffn{i}_core_b")
    dcw, dcb = dcw.reshape(NDEV, FFN_K, FFN_BLK), dcb.reshape(NDEV, 1, FFN_BLK)
    d_up = d_up.reshape(NDEV, FFN_BLK, D)
    dh_in, dng2, dsc2, dsh2 = _rw_bwd(
        f_modnorm, [(h, 0)], [(ng[2], "one"), (mod["sc2"], "seg"), (mod["sh2"], "seg")], [da2],
        name=f"ffn{i}_norm_b", row_grad=[(0, F32)], param_grad=[0, 1, 2], add=dh)
    grads = dict(up=d_up, down=d_down, cw=dcw, cb=dcb, ng2=dng2, ng3=dng3, sc2=dsc2, sh2=dsh2, g2=dg2)
    return dh_in, grads


def _mixer_norm_fwd(i, h, mod, ng):
    return _rw_fwd(f_modnorm, [(h, 0)], [(ng[0], "one"), (mod["sc1"], "seg"), (mod["sh1"], "seg")],
                   [(D, BF16)], name=f"mix{i}_norm")[0]


def _mixer_norm_bwd(i, h, mod, ng, da, dh):
    return _rw_bwd(f_modnorm, [(h, 0)], [(ng[0], "one"), (mod["sc1"], "seg"), (mod["sh1"], "seg")], [da],
                   name=f"mix{i}_norm_b", row_grad=[(0, F32)], param_grad=[0, 1, 2], add=dh)


def _conformer_fwd(i, h, mod, ng, wts):
    a = _mixer_norm_fwd(i, h, mod, ng)
    p = _mm(a, wts["w_in"], "nn", name=f"cm{i}_in")
    z = _rw_fwd(f_glu, [(p, 0, 2)], [(wts["b_in"], "one", 2)], [(D, F32)], name=f"cm{i}_glu")[0]
    zc = _dwconv(z, wts["dw_w"], wts["dw_b"], name=f"cm{i}_conv")
    r = _rw_fwd(f_lnsilu, [(zc, 0)], [(wts["ln_g"], "one"), (wts["ln_b"], "one")], [(D, BF16)],
                name=f"cm{i}_ln")[0]
    y = _mm(r, wts["w_out"], "nn", name=f"cm{i}_out")
    h2 = _rw_fwd(f_resgate_bias, [(h, 0), (y, 0)], [(mod["g1"], "seg"), (ng[1], "one"), (wts["b_out"], "one")],
                 [(D, F32)], name=f"cm{i}_res")[0]
    return h2, (h, a, p, z, zc, r, y)


def _conformer_bwd(i, dh, res, mod, ng, wts):
    h, a, p, z, zc, r, y = res
    dy, dg1, dng1, db_out = _rw_bwd(
        f_gate_rms_bias, [(y, 0)], [(mod["g1"], "seg"), (ng[1], "one"), (wts["b_out"], "one")], [dh],
        name=f"cm{i}_res_b", row_grad=[(0, BF16)], param_grad=[0, 1, 2])
    dr = _mm(dy, wts["w_out"], "nt", name=f"cm{i}_out_bx")
    d_w_out = _mm(r, dy, "tn", out_dtype=BF16, name=f"cm{i}_out_bw")
    dzc, dln_g, dln_b = _rw_bwd(f_lnsilu, [(zc, 0)], [(wts["ln_g"], "one"), (wts["ln_b"], "one")], [dr],
                                name=f"cm{i}_ln_b", row_grad=[(0, F32)], param_grad=[0, 1])
    ddw_w, ddw_b = _dwconv_wgrad(z, dzc, CM_K, name=f"cm{i}_conv_bw")
    dz = _dwconv(dzc, wts["dw_w"][:, ::-1, :], None, name=f"cm{i}_conv_bx")
    dp, db_in = _rw_bwd(f_glu, [(p, 0, 2)], [(wts["b_in"], "one", 2)], [dz], name=f"cm{i}_glu_b",
                        row_grad=[(0, BF16)], param_grad=[0])
    d_w_in = _mm(a, dp, "tn", out_dtype=BF16, name=f"cm{i}_in_bw")
    da = _mm(dp, wts["w_in"], "nt", name=f"cm{i}_in_bx")
    dh_in, dng0, dsc1, dsh1 = _mixer_norm_bwd(i, h, mod, ng, da, dh)
    grads = dict(w_in=d_w_in, w_out=d_w_out, b_in=db_in, dw_w=ddw_w, dw_b=ddw_b, ln_g=dln_g, ln_b=dln_b,
                 b_out=db_out, ng0=dng0, ng1=dng1, sc1=dsc1, sh1=dsh1, g1=dg1)
    return dh_in, grads


def _attention_fwd(i, h_all, mod, ng, wts, tables):
    a = _mixer_norm_fwd(i, h_all, mod, ng)
    qkv = _mm(a, wts["w_qkv"], "nn", name="attn_qkv")
    kv0 = N_Q * HEAD_DIM
    kv1 = kv0 + N_KV * HEAD_DIM
    q = _rope(qkv[:, :L, :kv0], tables, False, BF16, "attn_rope_q")
    k = _rope(qkv[:, :L, kv0:kv1], tables, False, BF16, "attn_rope_k")
    pad = ((0, 0), (ATTN_BLOCK, ATTN_BLOCK), (0, 0))
    q_h = _heads(q, N_Q)
    k_h = jnp.pad(_heads(k, N_KV), pad)
    v_h = jnp.pad(_heads(qkv[:, :L, kv1:].astype(BF16), N_KV), pad)
    kc_h = _heads(qkv[:, L:, kv0:kv1].astype(BF16), N_KV)
    vc_h = _heads(qkv[:, L:, kv1:].astype(BF16), N_KV)
    o_h, lse = _attn_fwd(q_h, k_h, v_h, kc_h, vc_h, wts["sink"])
    o = _unheads(o_h)
    y = _mm(o, wts["w_o"], "nn", name="attn_o")
    h_lat = h_all[:, :L]
    mod_lat = {k_: v_[:1] for k_, v_ in mod.items()}
    h2 = _rw_fwd(f_resgate, [(h_lat, 0), (y, 0)], [(mod_lat["g1"], "seg"), (ng[1], "one")], [(D, F32)],
                 name="attn_res")[0]
    return h2, (h_all, a, q_h, k_h, v_h, kc_h, vc_h, o_h, lse, o, y)


def _attention_bwd(i, dh, res, mod, ng, wts, tables):
    h_all, a, q_h, k_h, v_h, kc_h, vc_h, o_h, lse, o, y = res
    mod_lat = {k_: v_[:1] for k_, v_ in mod.items()}
    dy, dg1, dng1 = _rw_bwd(f_gate_rms, [(y, 0)], [(mod_lat["g1"], "seg"), (ng[1], "one")], [dh],
                            name="attn_res_b", row_grad=[(0, BF16)], param_grad=[0, 1])
    do = _mm(dy, wts["w_o"], "nt", name="attn_o_bx")
    d_w_o = _mm(o, dy, "tn", out_dtype=BF16, name="attn_o_bw")
    dq_h, dk_h, dv_h, dkc_h, dvc_h, dsink = _attn_bwd(q_h, k_h, v_h, kc_h, vc_h, wts["sink"], o_h, lse,
                                                        _heads(do, N_Q))
    dq = _rope(_unheads(dq_h), tables, True, BF16, "attn_rope_q_b")
    dk = _rope(_unheads(dk_h[:, ATTN_BLOCK:-ATTN_BLOCK]), tables, True, BF16, "attn_rope_k_b")
    dv = _unheads(dv_h[:, ATTN_BLOCK:-ATTN_BLOCK]).astype(BF16)
    d_lat = jnp.concatenate([dq, dk, dv], axis=2)
    d_ctx = jnp.concatenate([jnp.zeros((1, LC, N_Q * HEAD_DIM), BF16), _unheads(dkc_h).astype(BF16),
                             _unheads(dvc_h).astype(BF16)], axis=2)
    dqkv = jnp.concatenate([d_lat, d_ctx], axis=1)
    d_w_qkv = _mm(a, dqkv, "tn", out_dtype=BF16, name="attn_qkv_bw")
    da = _mm(dqkv, wts["w_qkv"], "nt", name="attn_qkv_bx")
    dh_res = jnp.concatenate([dh, jnp.zeros((1, LC, D), F32)], axis=1)
    dh_in, dng0, dsc1, dsh1 = _mixer_norm_bwd(i, h_all, mod, ng, da, dh_res)
    grads = dict(w_qkv=d_w_qkv, w_o=d_w_o, sink=dsink, ng0=dng0, ng1=dng1, sc1=dsc1, sh1=dsh1, g1=dg1)
    return dh_in, grads


def _gmlp_fwd(i, h, mod, ng, wts):
    a = _mixer_norm_fwd(i, h, mod, ng)
    p = _mm(a, wts["w_in"], "nn", name="gm_in")
    u, v = _rw_fwd(f_gmlp_pre, [(p, 0, 2)], [(wts["b_in"], "one", 2), (wts["ln_g"], "one"), (wts["ln_b"], "one")],
                   [(GM_W, F32), (GM_W, BF16)], name="gm_pre")
    us = _gm_spatial_fwd(u, v, wts["w_s"], wts["b_s"])
    y = _mm(us, wts["w_out"], "nn", name="gm_out")
    h2 = _rw_fwd(f_resgate, [(h, 0), (y, 0)], [(mod["g1"], "seg"), (ng[1], "one")], [(D, F32)],
                 name="gm_res")[0]
    return h2, (h, a, p, u, v, us, y)


def _gmlp_bwd(i, dh, res, mod, ng, wts):
    h, a, p, u, v, us, y = res
    dy, dg1, dng1 = _rw_bwd(f_gate_rms, [(y, 0)], [(mod["g1"], "seg"), (ng[1], "one")], [dh],
                            name="gm_res_b", row_grad=[(0, BF16)], param_grad=[0, 1])
    dus = _mm(dy, wts["w_out"], "nt", name="gm_out_bx")
    d_w_out = _mm(us, dy, "tn", out_dtype=BF16, name="gm_out_bw")
    du, dv, dws, dbs = _gm_spatial_bwd(u, v, wts["w_s"], wts["b_s"], dus)
    dp, db_in, dln_g, dln_b = _rw_bwd(
        f_gmlp_pre, [(p, 0, 2)], [(wts["b_in"], "one", 2), (wts["ln_g"], "one"), (wts["ln_b"], "one")], [du, dv],
        name="gm_pre_b", row_grad=[(0, BF16)], param_grad=[0, 1, 2])
    d_w_in = _mm(a, dp, "tn", out_dtype=BF16, name="gm_in_bw")
    da = _mm(dp, wts["w_in"], "nt", name="gm_in_bx")
    dh_in, dng0, dsc1, dsh1 = _mixer_norm_bwd(i, h, mod, ng, da, dh)
    grads = dict(w_in=d_w_in, w_out=d_w_out, b_in=db_in, ln_g=dln_g, ln_b=dln_b, w_s=dws, b_s=dbs,
                 ng0=dng0, ng1=dng1, sc1=dsc1, sh1=dsh1, g1=dg1)
    return dh_in, grads


MOD_NAMES = ("sh1", "sc1", "g1", "sh2", "sc2", "g2")
SMALL = (
    ("norm_g", (4, 4, 128)), ("ffn_conv_w", (4, 3, 704)), ("cm_b_in", (2, 256)), ("cm_dw_w", (2, 31, 128)),
    ("cm_dw_b", (2, 128)), ("cm_ln_g", (2, 128)), ("cm_ln_b", (2, 128)), ("cm_b_out", (2, 128)),
    ("gm_b_in", (1, 512)), ("gm_ln_g", (1, 256)), ("gm_ln_b", (1, 256)))


def _mixer_weights(i, P):
    if i % 3 == 0:
        j = i // 3
        return dict(w_in=P["cm_w_in"][j], w_out=P["cm_w_out"][j], b_in=P["cm_b_in"][j].reshape(1, 1, 2 * D),
                    dw_w=P["cm_dw_w"][j][None], dw_b=P["cm_dw_b"][j].reshape(1, 1, D),
                    ln_g=P["cm_ln_g"][j].reshape(1, 1, D), ln_b=P["cm_ln_b"][j].reshape(1, 1, D),
                    b_out=P["cm_b_out"][j].reshape(1, 1, D))
    if i % 3 == 1:
        return dict(w_qkv=P["attn_w_qkv"], w_o=P["attn_w_o"], sink=P["attn_sink"].reshape(N_Q))
    return dict(w_in=P["gm_w_in"], w_out=P["gm_w_out"], b_in=P["gm_b_in"].reshape(1, 1, 2 * GM_W),
                ln_g=P["gm_ln_g"].reshape(1, 1, GM_W), ln_b=P["gm_ln_b"].reshape(1, 1, GM_W),
                w_s=P["gm_w_s"].reshape(GM_GROUPS, GM_CHUNK, GM_CHUNK).astype(BF16),
                b_s=P["gm_b_s"].reshape(GM_GROUPS, GM_CHUNK, 1))


def _ffn_weights(i, P):
    return dict(up=P["ffn_w_up"][i].reshape(2, FFN_PAIRS, D, FFN_BLK), down=P["ffn_w_down"][i],
                cw=P["ffn_conv_w"][i].reshape(2, FFN_PAIRS, FFN_K, FFN_BLK),
                cb=P["ffn_conv_b"][i].reshape(2, FFN_PAIRS, 1, FFN_BLK))


def _local_step(x, ctx, target, lat_mod, ctx_mod, norm_g, layer_weights, grads_ready):
    tables = _rope_tables()
    ng = [[norm_g[i, j].reshape(1, 1, D) for j in range(4)] for i in range(DEPTH)]

    def mods(i, with_ctx, token):
        out = {}
        for j, nme in enumerate(MOD_NAMES):
            rows = [lat_mod[i, j]] + ([ctx_mod[i, j]] if with_ctx else [])
            out[nme] = jnp.stack(rows).reshape(len(rows), 1, D) + token[0, 0]
        return out

    def after(mod, token):
        return mod if token is None else {k_: v_ + token[0, 0] for k_, v_ in mod.items()}

    h_all = jnp.concatenate([x, ctx], axis=1)
    wm0, wf0, tok = layer_weights(0, h_all)
    m0 = mods(0, True, tok)
    h, r0m = _conformer_fwd(0, h_all, m0, ng[0], wm0)
    h, r0f = _ffn_fwd(0, h, m0, ng[0], wf0)
    wm1, wf1, tok = layer_weights(1, h)
    m1 = mods(1, True, tok)
    m1l = {k_: v_[:1] for k_, v_ in m1.items()}
    h, r1m = _attention_fwd(1, h, m1, ng[1], wm1, tables)
    h, r1f = _ffn_fwd(1, h, m1l, ng[1], wf1)
    wm2, wf2, tok = layer_weights(2, h)
    m2 = mods(2, False, tok)
    h, r2m = _gmlp_fwd(2, h, m2, ng[2], wm2)
    h, r2f = _ffn_fwd(2, h, m2, ng[2], wf2)
    wm3, wf3, tok = layer_weights(3, h)
    m3 = mods(3, False, tok)
    h, r3m = _conformer_fwd(3, h, m3, ng[3], wm3)
    h, r3f = _ffn_fwd(3, h, m3, ng[3], wf3)
    dh, loss = _loss_head(h, target)

    G = {}
    dh, G["f3"] = _ffn_bwd(3, dh, r3f, m3, ng[3], wf3)
    tok = grads_ready("f3", G["f3"])
    dh, G["m3"] = _conformer_bwd(3, dh, r3m, after(m3, tok), ng[3], wm3)
    tok = grads_ready("m3", G["m3"])
    dh, G["f2"] = _ffn_bwd(2, dh, r2f, after(m2, tok), ng[2], wf2)
    tok = grads_ready("f2", G["f2"])
    dh, G["m2"] = _gmlp_bwd(2, dh, r2m, after(m2, tok), ng[2], wm2)
    tok = grads_ready("m2", G["m2"])
    dh, G["f1"] = _ffn_bwd(1, dh, r1f, after(m1l, tok), ng[1], wf1)
    tok = grads_ready("f1", G["f1"])
    dh, G["m1"] = _attention_bwd(1, dh, r1m, after(m1, tok), ng[1], wm1, tables)
    tok = grads_ready("m1", G["m1"])
    dh, G["f0"] = _ffn_bwd(0, dh, r0f, after(m0, tok), ng[0], wf0)
    tok = grads_ready("f0", G["f0"])
    dh, G["m0"] = _conformer_bwd(0, dh, r0m, after(m0, tok), ng[0], wm0)
    grads_ready("m0", G["m0"])
    grad_x = dh[:, :L]

    zero = jnp.zeros((D,), F32)
    dmod = []
    for seg in range(2):
        per_layer = []
        for i in range(DEPTH):
            vals = []
            for nme in MOD_NAMES:
                src = G[("m" if nme.endswith("1") else "f") + str(i)][nme]
                vals.append(src[seg, 0] if src.shape[0] > seg else zero)
            per_layer.append(jnp.concatenate(vals))
        dmod.append(jnp.stack(per_layer))
    dmod = jnp.stack(dmod)
    return loss, grad_x, G, dmod


def kernel(x, c, ctx, c_ctx, ada_w, ada_b, norm_g, ffn_w_up, ffn_conv_w, ffn_conv_b, ffn_w_down, cm_w_in, cm_b_in, cm_dw_w, cm_dw_b, cm_ln_g, cm_ln_b, cm_w_out, cm_b_out, attn_w_qkv, attn_sink, attn_w_o, gm_w_in, gm_b_in, gm_ln_g, gm_ln_b, gm_w_s, gm_b_s, gm_w_out, loss_target, m_c_ctx, m_ada_w, m_ada_b, m_norm_g, m_ffn_w_up, m_ffn_conv_w, m_ffn_conv_b, m_ffn_w_down, m_cm_w_in, m_cm_b_in, m_cm_dw_w, m_cm_dw_b, m_cm_ln_g, m_cm_ln_b, m_cm_w_out, m_cm_b_out, m_attn_w_qkv, m_attn_sink, m_attn_w_o, m_gm_w_in, m_gm_b_in, m_gm_ln_g, m_gm_ln_b, m_gm_w_s, m_gm_b_s, m_gm_w_out, v_c_ctx, v_ada_w, v_ada_b, v_norm_g, v_ffn_w_up, v_ffn_conv_w, v_ffn_conv_b, v_ffn_w_down, v_cm_w_in, v_cm_b_in, v_cm_dw_w, v_cm_dw_b, v_cm_ln_g, v_cm_ln_b, v_cm_w_out, v_cm_b_out, v_attn_w_qkv, v_attn_sink, v_attn_w_o, v_gm_w_in, v_gm_b_in, v_gm_ln_g, v_gm_ln_b, v_gm_w_s, v_gm_b_s, v_gm_w_out):
    W = dict(c_ctx=c_ctx, ada_w=ada_w, ada_b=ada_b, norm_g=norm_g, ffn_w_up=ffn_w_up, ffn_conv_w=ffn_conv_w, ffn_conv_b=ffn_conv_b, ffn_w_down=ffn_w_down, cm_w_in=cm_w_in, cm_b_in=cm_b_in, cm_dw_w=cm_dw_w, cm_dw_b=cm_dw_b, cm_ln_g=cm_ln_g, cm_ln_b=cm_ln_b, cm_w_out=cm_w_out, cm_b_out=cm_b_out, attn_w_qkv=attn_w_qkv, attn_sink=attn_sink, attn_w_o=attn_w_o, gm_w_in=gm_w_in, gm_b_in=gm_b_in, gm_ln_g=gm_ln_g, gm_ln_b=gm_ln_b, gm_w_s=gm_w_s, gm_b_s=gm_b_s, gm_w_out=gm_w_out)
    M = dict(c_ctx=m_c_ctx, ada_w=m_ada_w, ada_b=m_ada_b, norm_g=m_norm_g, ffn_w_up=m_ffn_w_up, ffn_conv_w=m_ffn_conv_w, ffn_conv_b=m_ffn_conv_b, ffn_w_down=m_ffn_w_down, cm_w_in=m_cm_w_in, cm_b_in=m_cm_b_in, cm_dw_w=m_cm_dw_w, cm_dw_b=m_cm_dw_b, cm_ln_g=m_cm_ln_g, cm_ln_b=m_cm_ln_b, cm_w_out=m_cm_w_out, cm_b_out=m_cm_b_out, attn_w_qkv=m_attn_w_qkv, attn_sink=m_attn_sink, attn_w_o=m_attn_w_o, gm_w_in=m_gm_w_in, gm_b_in=m_gm_b_in, gm_ln_g=m_gm_ln_g, gm_ln_b=m_gm_ln_b, gm_w_s=m_gm_w_s, gm_b_s=m_gm_b_s, gm_w_out=m_gm_w_out)
    V = dict(c_ctx=v_c_ctx, ada_w=v_ada_w, ada_b=v_ada_b, norm_g=v_norm_g, ffn_w_up=v_ffn_w_up, ffn_conv_w=v_ffn_conv_w, ffn_conv_b=v_ffn_conv_b, ffn_w_down=v_ffn_w_down, cm_w_in=v_cm_w_in, cm_b_in=v_cm_b_in, cm_dw_w=v_cm_dw_w, cm_dw_b=v_cm_dw_b, cm_ln_g=v_cm_ln_g, cm_ln_b=v_cm_ln_b, cm_w_out=v_cm_w_out, cm_b_out=v_cm_b_out, attn_w_qkv=v_attn_w_qkv, attn_sink=v_attn_sink, attn_w_o=v_attn_w_o, gm_w_in=v_gm_w_in, gm_b_in=v_gm_b_in, gm_ln_g=v_gm_ln_g, gm_ln_b=v_gm_ln_b, gm_w_s=v_gm_w_s, gm_b_s=v_gm_b_s, gm_w_out=v_gm_w_out)
    me = 4 * lax.axis_index("x") + 2 * lax.axis_index("y") + lax.axis_index("c")
    small_shapes = [s for _, s in SMALL]

    small = _pack_rows([W[n] for n, _ in SMALL] + [c])
    layer_mats = [("cm_w_in", 0, "cm_w_out", 0), ("attn_w_qkv", 0, "attn_w_o", 0), ("gm_w_in", 0, "gm_w_out", 0),
                  ("cm_w_in", 1, "cm_w_out", 1)]
    local_bf16 = [[W[a][ja].astype(BF16), W[b][jb].astype(BF16), ffn_w_up[i].astype(BF16), ffn_w_down[i].astype(BF16)]
                  for i, (a, ja, b, jb) in enumerate(layer_mats)]
    gathered0 = _all_gather([small] + local_bf16[0], "gather_params0")
    small_g = gathered0[0]
    col_to_full = lambda g: g.transpose(1, 0, 2).reshape(g.shape[1], NDEV * g.shape[2])
    P = {}
    unpacked = jax.vmap(lambda r: tuple(_unpack_rows(r, small_shapes + [(D,)])))(small_g)
    for (n, _), g in zip(SMALL, unpacked[:-1]):
        if n == "ffn_conv_w":
            P[n] = [g[:, i] for i in range(DEPTH)]
        else:
            P[n] = _unshard_last(g)
    c_all = unpacked[-1]
    P["ffn_conv_b"] = [ffn_conv_b[i].reshape(NDEV, 1, FFN_BLK) for i in range(DEPTH)]
    P["attn_sink"], P["gm_w_s"], P["gm_b_s"] = attn_sink, gm_w_s, gm_b_s

    cond = jnp.concatenate([c_all, c_ctx[None], jnp.zeros((7, D), F32)])[None]
    scond = _rw_fwd(f_silu, [(cond, 0)], [], [(D, BF16)], name="ada_silu")[0]
    ada_bf = ada_w.astype(BF16)
    ncol = ada_w.shape[2]
    mod_loc = _mm(scond, ada_bf, "nn", name="ada_proj")
    mod_loc = mod_loc + lax.dynamic_slice_in_dim(ada_b, me * ncol, ncol, axis=1)[:, None, :]
    mod_g = _all_gather([mod_loc], "gather_mod")[0]
    mod_full = mod_g.transpose(1, 2, 0, 3).reshape(DEPTH, 16, 6, D)
    lat_mod = lax.dynamic_index_in_dim(mod_full, me, axis=1, keepdims=False)
    ctx_mod = mod_full[:, NDEV]

    gathers, exchanges, pending = {}, {}, {}
    col_to_parts = lambda g: g[0].reshape(g.shape[1], NDEV, g.shape[2] // NDEV).transpose(1, 0, 2)
    row_to_parts = lambda g: g.reshape(NDEV, -1, g.shape[-1])
    no_order = jnp.zeros((8, 128), F32)

    def layer_weights(i, h):
        if i == 0:
            mats = gathered0[1:]
            gathers[1] = _xfer_start("gather", local_bf16[1], mod_g, "gather_params1")
        else:
            mats = _xfer_wait(gathers[i], h)
            if i + 1 < DEPTH:
                gathers[i + 1] = _xfer_start("gather", local_bf16[i + 1], mats[0], f"gather_params{i + 1}")
        token = gathers[i + 1]["token"] if i + 1 < DEPTH else no_order
        a, ja, b, jb = layer_mats[i]
        pi = dict(P)
        pi[a] = col_to_full(mats[0]) if a.startswith(("attn", "gm")) else {ja: col_to_full(mats[0])}
        pi[b] = mats[1].reshape(-1, D) if b.startswith(("attn", "gm")) else {jb: mats[1].reshape(-1, D)}
        pi["ffn_w_up"], pi["ffn_w_down"] = {i: mats[2]}, {i: mats[3].reshape(FFN_PAIRS, FFN_BLK, D)}
        return _mixer_weights(i, pi), _ffn_weights(i, pi), token

    def grads_ready(tag, g):
        pending[tag] = g
        i = int(tag[1])
        col_name, row_name = {0: ("w_in", "w_out"), 1: ("w_qkv", "w_o"), 2: ("w_in", "w_out")}[i % 3]
        if tag == "f0":
            arrs = [g["up"], row_to_parts(g["down"])]
        elif tag == "m0":
            arrs = [col_to_parts(g[col_name]), row_to_parts(g[row_name])]
        elif tag[0] == "m":
            gf = pending[f"f{i}"]
            arrs = [col_to_parts(g[col_name]), row_to_parts(g[row_name]), gf["up"], row_to_parts(gf["down"])]
        else:
            return None
        exchanges[tag] = _xfer_start("scatter", arrs, no_order, "exchange_" + tag)
        return exchanges[tag]["token"]

    loss_part, grad_x, G, dmod = _local_step(x, ctx, loss_target, lat_mod, ctx_mod, P["norm_g"], layer_weights,
                                             grads_ready)
    recv = {tag: _xfer_wait(exchanges[tag], grad_x) for tag in ("m3", "m2", "m1")}

    dmod_g = _all_gather([dmod], "gather_dmod")[0]
    dm_cols = lax.dynamic_slice_in_dim(dmod_g, me * ncol, ncol, axis=3)
    dm_ext = dm_cols.transpose(2, 1, 0, 3).reshape(DEPTH, 16, ncol)
    cond_ext = jnp.concatenate([c_all, jnp.broadcast_to(c_ctx[None], (NDEV, D))])[None]
    scond_ext = _rw_fwd(f_silu, [(cond_ext, 0)], [], [(D, BF16)], name="ada_silu_ext")[0]
    g_ada_w = _mm(scond_ext, dm_ext, "tn", name="ada_proj_bw")
    dsil = _mm(dm_ext, ada_bf, "nt", reduce_blocks=True, name="ada_proj_bx")
    dcc = _rw_bwd(f_silu_rows, [(jnp.zeros((1, NDEV, D), F32), 0)], [(c_ctx.reshape(1, 1, D), "one")],
                  [dsil[:, NDEV:]], name="ada_silu_b", param_grad=[0])[0]

    out = {}

    def put(name, res):
        out[name] = res

    d_norm_g = jnp.stack([jnp.stack([G[f"m{i}"]["ng0"], G[f"m{i}"]["ng1"], G[f"f{i}"]["ng2"], G[f"f{i}"]["ng3"]])
                          for i in range(DEPTH)]).reshape(DEPTH, 4, D)
    small_full = dict(
        norm_g=d_norm_g,
        cm_b_in=jnp.stack([G["m0"]["b_in"], G["m3"]["b_in"]]).reshape(2, 2 * D),
        cm_dw_w=jnp.stack([G["m0"]["dw_w"][0], G["m3"]["dw_w"][0]]),
        cm_dw_b=jnp.stack([G["m0"]["dw_b"], G["m3"]["dw_b"]]).reshape(2, D),
        cm_ln_g=jnp.stack([G["m0"]["ln_g"], G["m3"]["ln_g"]]).reshape(2, D),
        cm_ln_b=jnp.stack([G["m0"]["ln_b"], G["m3"]["ln_b"]]).reshape(2, D),
        cm_b_out=jnp.stack([G["m0"]["b_out"], G["m3"]["b_out"]]).reshape(2, D),
        gm_b_in=G["m2"]["b_in"].reshape(1, 2 * GM_W),
        gm_ln_g=G["m2"]["ln_g"].reshape(1, GM_W), gm_ln_b=G["m2"]["ln_b"].reshape(1, GM_W))
    by_dest = []
    for n, _ in SMALL:
        if n == "ffn_conv_w":
            by_dest.append(jnp.stack([G[f"f{i}"]["cw"] for i in range(DEPTH)], axis=1))
        else:
            by_dest.append(_shard_last(small_full[n]))
    small_send = jax.vmap(lambda *vs: _pack_rows(list(vs)))(*by_dest)
    small_recv = _all_to_all([[small_send]], "exchange_small")[0]

    def shard3(a):
        return a.reshape(a.shape[0], -1, a.shape[-1])

    small_local = lambda d_: _pack_rows([d_[n] for n, _ in SMALL])[None]
    res = _adamw(small_recv, small_local(W), small_local(M), small_local(V), "adamw_small")
    unp = [_unpack_rows(r[0], small_shapes) for r in res]
    for q, (n, _) in enumerate(SMALL):
        put(n, [unp[t][q] for t in range(4)])

    repl_names = ["c_ctx", "ffn_conv_b", "attn_sink", "gm_b_s", "gm_w_s"]
    repl_part = dict(
        c_ctx=dcc.reshape(D),
        ffn_conv_b=jnp.stack([G[f"f{i}"]["cb"].reshape(2 * 2816) for i in range(DEPTH)]),
        attn_sink=G["m1"]["sink"].reshape(1, N_Q),
        gm_b_s=G["m2"]["b_s"].reshape(1, GM_GROUPS, GM_CHUNK),
        gm_w_s=G["m2"]["w_s"].reshape(1, GM_GROUPS, GM_CHUNK, GM_CHUNK))
    repl_shapes = [W[n].shape for n in repl_names]
    repl_sent = _pack_rows([repl_part[n] for n in repl_names] + [loss_part.reshape(1)], mult=256)
    repl_g = _all_gather([repl_sent], "gather_repl")[0]
    loss = jnp.sum(repl_g.reshape(NDEV, -1)[:, sum(W[n].size for n in repl_names)])
    repl_local = lambda d_: _pack_rows([d_[n] for n in repl_names], mult=256)[None]
    res = _adamw(repl_g[None], repl_local(W), repl_local(M), repl_local(V), "adamw_repl")
    unp = [_unpack_rows(r[0], repl_shapes) for r in res]
    for q, n in enumerate(repl_names):
        put(n, [unp[t][q] for t in range(4)])

    def update_big(n, parts):
        turn = (lambda a: jnp.swapaxes(a, 1, 2)) if n == "ffn_w_up" else shard3
        res = _adamw(parts, turn(W[n]), turn(M[n]), turn(V[n]), "adamw_" + n)
        put(n, [(turn(r) if n == "ffn_w_up" else r).reshape(W[n].shape) for r in res])

    put("ada_w", _adamw(g_ada_w[:, None], ada_w, m_ada_w, v_ada_w, "adamw_ada_w"))
    ada_b_parts = dmod_g.reshape(1, 2 * NDEV, DEPTH, 6 * D)
    res = _adamw(ada_b_parts, ada_b[None], m_ada_b[None], v_ada_b[None], "adamw_ada_b")
    put("ada_b", [r[0] for r in res])
    early = dict(attn_w_qkv=[recv["m1"][0]], attn_w_o=[recv["m1"][1]], gm_w_in=[recv["m2"][0]],
                 gm_w_out=[recv["m2"][1]])
    for n, parts in early.items():
        update_big(n, parts)
    done_first = sum(out[n][1].reshape(-1)[:1024] for n in ["ada_w", "ada_b", "gm_w_in", "gm_w_out", "attn_w_qkv"])
    recv.update({tag: _xfer_wait(exchanges[tag], done_first) for tag in ("f0", "m0")})
    late = dict(
        ffn_w_up=[recv["f0"][0]] + [recv[f"m{i}"][2] for i in (1, 2, 3)],
        ffn_w_down=[recv["f0"][1]] + [recv[f"m{i}"][3] for i in (1, 2, 3)],
        cm_w_in=[recv["m0"][0], recv["m3"][0]], cm_w_out=[recv["m0"][1], recv["m3"][1]])
    for n, parts in late.items():
        update_big(n, parts)

    names = ["c_ctx", "ada_w", "ada_b", "norm_g", "ffn_w_up", "ffn_conv_w", "ffn_conv_b", "ffn_w_down", "cm_w_in",
             "cm_b_in", "cm_dw_w", "cm_dw_b", "cm_ln_g", "cm_ln_b", "cm_w_out", "cm_b_out", "attn_w_qkv",
             "attn_sink", "attn_w_o", "gm_w_in", "gm_b_in", "gm_ln_g", "gm_ln_b", "gm_w_s", "gm_b_s", "gm_w_out"]
    return (loss, grad_x, *[out[n][0] for n in names], *[out[n][1] for n in names],
            *[out[n][2] for n in names], *[out[n][3] for n in names])
```

```python
import functools

import jax
import jax.numpy as jnp
from jax import lax
from jax.experimental import pallas as pl
from jax.experimental.pallas import tpu as pltpu

F32, BF16 = jnp.float32, jnp.bfloat16
MESH = pl.DeviceIdType.MESH
AXES = ("x", "y", "c")
NDEV = 8

D = 1024
L = 2048
LC = 256
TA = L + LC
DEPTH = 4
EPS = 1e-6
HEAD_DIM = 64
N_Q, N_KV, Q_PER_KV = 16, 4, 4
ATTN_BLOCK = 128
GRID_W = 64
ROPE_BASE = 10000.0
GM_W = 2048
GM_CHUNK = 128
GM_GROUPS = 16
FFN_BLK = 704
CM_K, FFN_K = 31, 3

ADAM_LR, ADAM_B1, ADAM_B2, ADAM_EPS, ADAM_WD, ADAM_STEP = 0.001, 0.9, 0.999, 1e-08, 0.01, 10

VMEM_LIMIT_V7X = 58 * 1024 * 1024
ROW_TILE_ELEMS = 256 * 1024
MM_TILE_BYTES = 4 * 1024 * 1024


def _cparams(sem=None):
    kw = dict(vmem_limit_bytes=VMEM_LIMIT_V7X)
    if sem is not None:
        kw["dimension_semantics"] = sem
    return pltpu.CompilerParams(**kw)


def _pick(n, cands):
    for c in cands:
        if n % c == 0:
            return c
    return n


def _as3(a):
    return a if a.ndim == 3 else a[None]


def _all_gather(arrs, name):
    n = len(arrs)

    def body(*refs):
        ins, outs = refs[:n], refs[n:2 * n]
        send_sems, recv_sems, local_sems = refs[2 * n:]
        x, y, c = lax.axis_index("x"), lax.axis_index("y"), lax.axis_index("c")
        me, sibling = (x, y, c), (x, y, 1 - c)
        chips = [(1 - x, y), (x, 1 - y), (1 - x, 1 - y)]

        def slot(a, p):
            return outs[a].at[4 * p[0] + 2 * p[1] + p[2]]

        def copy(a, k, block, to, src=None):
            return pltpu.make_async_remote_copy(
                src_ref=slot(a, block) if src is None else src, dst_ref=slot(a, block),
                send_sem=send_sems.at[a, k], recv_sem=recv_sems.at[a, k],
                device_id=to, device_id_type=MESH)

        mine = [pltpu.make_async_copy(ins[a], slot(a, me), local_sems.at[a]) for a in range(n)]
        for m in mine:
            m.start()
        first = []
        for a in range(n):
            first.append(copy(a, 0, me, sibling, src=ins[a]))
            first += [copy(a, 1 + j, me, (*chip, c), src=ins[a]) for j, chip in enumerate(chips)]
        for cp in first:
            cp.start()
        passed = []
        for j, chip in enumerate(chips):
            for a in range(n):
                copy(a, 1 + j, (*chip, c), me).wait_recv()
                p = copy(a, 4 + j, (*chip, c), sibling)
                p.start()
                passed.append(p)
        for a in range(n):
            copy(a, 0, sibling, me).wait_recv()
            for j, chip in enumerate(chips):
                copy(a, 4 + j, (*chip, 1 - c), me).wait_recv()
        for cp in first + passed:
            cp.wait_send()
        for m in mine:
            m.wait()

    any_spec = pl.BlockSpec(memory_space=pl.ANY)
    outs = pl.pallas_call(
        body, name=name,
        out_shape=[jax.ShapeDtypeStruct((NDEV,) + a.shape, a.dtype) for a in arrs],
        in_specs=[any_spec] * n, out_specs=[any_spec] * n,
        scratch_shapes=[pltpu.SemaphoreType.DMA((n, 7)), pltpu.SemaphoreType.DMA((n, 7)),
                        pltpu.SemaphoreType.DMA((n,))],
    )(*arrs)
    return list(outs)


def _all_to_all(groups, name):
    flat = [(gi, li, a) for gi, g in enumerate(groups) for li, a in enumerate(g)]
    n, ng = len(flat), len(groups)

    def body(*refs):
        ins, outs = refs[:n], refs[n:n + ng]
        send_sems, recv_sems, local_sems = refs[n + ng:]
        x, y, c = lax.axis_index("x"), lax.axis_index("y"), lax.axis_index("c")
        me = 4 * x + 2 * y + c
        copies = []
        for a, (gi, li, _) in enumerate(flat):
            loc = pltpu.make_async_copy(ins[a].at[me], outs[gi].at[li, me], local_sems.at[a])
            loc.start()
            copies.append(loc)
            for k in range(1, NDEV):
                px = 1 - x if (k >> 2) & 1 else x
                py = 1 - y if (k >> 1) & 1 else y
                pc = 1 - c if k & 1 else c
                cp = pltpu.make_async_remote_copy(
                    src_ref=ins[a].at[4 * px + 2 * py + pc], dst_ref=outs[gi].at[li, me],
                    send_sem=send_sems.at[a, k - 1], recv_sem=recv_sems.at[a, k - 1],
                    device_id=(px, py, pc), device_id_type=MESH)
                cp.start()
                copies.append(cp)
        for cp in copies:
            cp.wait()

    any_spec = pl.BlockSpec(memory_space=pl.ANY)
    outs = pl.pallas_call(
        body, name=name,
        out_shape=[jax.ShapeDtypeStruct((len(g),) + g[0].shape, g[0].dtype) for g in groups],
        in_specs=[any_spec] * n, out_specs=[any_spec] * ng,
        scratch_shapes=[pltpu.SemaphoreType.DMA((n, 7)), pltpu.SemaphoreType.DMA((n, 7)),
                        pltpu.SemaphoreType.DMA((n,))],
    )(*[a for _, _, a in flat])
    return list(outs)


HBM_SPEC = pl.BlockSpec(memory_space=pltpu.HBM)
SEM_SPEC = pl.BlockSpec(memory_space=pltpu.SEMAPHORE)
ANY_SPEC = pl.BlockSpec(memory_space=pl.ANY)
SPLIT_EFFECT = pltpu.SideEffectType.DATAFLOW_SIDE_EFFECTING


def _remote_copies(kind, ins, lands, send_sems, recv_sems):
    x, y, c = lax.axis_index("x"), lax.axis_index("y"), lax.axis_index("c")
    me = 4 * x + 2 * y + c
    out = []
    for a in range(len(ins)):
        for k in range(1, NDEV):
            px = 1 - x if (k >> 2) & 1 else x
            py = 1 - y if (k >> 1) & 1 else y
            pc = 1 - c if k & 1 else c
            src = ins[a] if kind == "gather" else ins[a].at[4 * px + 2 * py + pc]
            out.append(pltpu.make_async_remote_copy(
                src_ref=src, dst_ref=lands[a].at[me], send_sem=send_sems.at[a * (NDEV - 1) + k - 1],
                recv_sem=recv_sems.at[a * (NDEV - 1) + k - 1], device_id=(px, py, pc), device_id_type=MESH))
    return out


def _local_copies(kind, ins, lands, local_sems):
    me = 4 * lax.axis_index("x") + 2 * lax.axis_index("y") + lax.axis_index("c")
    return [pltpu.make_async_copy(ins[a] if kind == "gather" else ins[a].at[me], lands[a].at[me], local_sems.at[a])
            for a in range(len(ins))]


def _xfer_start(kind, arrs, after, name):
    n = len(arrs)
    lands = [lax.empty((NDEV,) + a.shape if kind == "gather" else a.shape, a.dtype) for a in arrs]

    def body(*refs):
        ins, lnd = refs[:n], refs[n:2 * n]
        send_sems, recv_sems, local_sems = refs[2 * n + 1:2 * n + 4]
        for cp in _remote_copies(kind, ins, lnd, send_sems, recv_sems) + _local_copies(kind, ins, lnd, local_sems):
            cp.start()
        refs[-1][...] = jnp.zeros_like(refs[-1])

    outs = pl.pallas_call(
        body, name=name,
        out_shape=(pltpu.SemaphoreType.DMA((n * (NDEV - 1),)), pltpu.SemaphoreType.DMA((n * (NDEV - 1),)),
                   pltpu.SemaphoreType.DMA((n,)),
                   *[pltpu.HBM(a.shape, a.dtype) for a in arrs + lands], jax.ShapeDtypeStruct((8, 128), F32)),
        in_specs=[HBM_SPEC] * (2 * n) + [ANY_SPEC],
        out_specs=(SEM_SPEC, SEM_SPEC, SEM_SPEC, *[HBM_SPEC] * (2 * n), pl.BlockSpec(memory_space=pltpu.VMEM)),
        input_output_aliases={a: 3 + a for a in range(2 * n)},
        compiler_params=pltpu.CompilerParams(has_side_effects=SPLIT_EFFECT),
    )(*[pltpu.with_memory_space_constraint(a, pltpu.HBM) for a in arrs + lands], after)
    return dict(kind=kind, n=n, sems=list(outs[:3]), bufs=list(outs[3:3 + 2 * n]), token=outs[-1], name=name)


def _xfer_wait(st, after):
    kind, n = st["kind"], st["n"]

    def body(*refs):
        ins, lnd = refs[:n], refs[n:2 * n]
        send_sems, recv_sems, local_sems = refs[2 * n:2 * n + 3]
        for cp in _remote_copies(kind, ins, lnd, send_sems, recv_sems):
            cp.wait_send()
            cp.wait_recv()
        for cp in _local_copies(kind, ins, lnd, local_sems):
            cp.wait()

    outs = pl.pallas_call(
        body, name=st["name"] + "_wait",
        out_shape=tuple(pltpu.HBM(b.shape, b.dtype) for b in st["bufs"]),
        in_specs=[HBM_SPEC] * (2 * n) + [SEM_SPEC] * 3 + [ANY_SPEC],
        out_specs=tuple([HBM_SPEC] * (2 * n)),
        input_output_aliases={a: a for a in range(2 * n)},
        compiler_params=pltpu.CompilerParams(has_side_effects=SPLIT_EFFECT),
    )(*st["bufs"], *st["sems"], after)
    return list(outs[n:])


def _mm(a, b, kind, *, name, out_dtype=F32, reduce_blocks=False):
    a, b = _as3(a), _as3(b)
    nba, nbb = a.shape[0], b.shape[0]
    nb = max(nba, nbb)
    assert nba in (1, nb) and nbb in (1, nb)
    if kind == "tn":
        t, m = a.shape[1:]
        n = b.shape[2]
        assert b.shape[1] == t and not reduce_blocks
        tm = m if m <= 1024 else _pick(m, (1024,))
        tn = n if n <= 1024 else _pick(n, (1024, 768, 512))
        tk = t if t * (tm + tn) * 2 <= MM_TILE_BYTES * 3 else _pick(t, (512, 768, 256))
        nred = t // tk
    else:
        m, k = a.shape[1:]
        n = b.shape[2] if kind == "nn" else b.shape[1]
        assert (b.shape[1] if kind == "nn" else b.shape[2]) == k
        tn = n if (n <= 1024 or k * n * 2 <= 2 * MM_TILE_BYTES) else _pick(n, (1024, 768, 512))
        tm = m
        for cand in (1024, 768, 512, 256):
            if m % cand == 0 and cand * tn * 4 <= MM_TILE_BYTES and cand * k * 2 <= MM_TILE_BYTES:
                tm = cand
                break
        nred = nb if reduce_blocks else 1
    nbo = 1 if reduce_blocks else nb

    def blk(nbx, g, r):
        if nbx == 1:
            return 0
        return r if reduce_blocks else g

    if kind == "nn":
        a_spec = pl.BlockSpec((1, tm, k), lambda g, j, i, r: (blk(nba, g, r), i, 0))
        b_spec = pl.BlockSpec((1, k, tn), lambda g, j, i, r: (blk(nbb, g, r), 0, j))
        dims = (((1,), (0,)), ((), ()))
    elif kind == "nt":
        a_spec = pl.BlockSpec((1, tm, k), lambda g, j, i, r: (blk(nba, g, r), i, 0))
        b_spec = pl.BlockSpec((1, tn, k), lambda g, j, i, r: (blk(nbb, g, r), j, 0))
        dims = (((1,), (1,)), ((), ()))
    else:
        a_spec = pl.BlockSpec((1, tk, tm), lambda g, j, i, r: (blk(nba, g, r), r, i))
        b_spec = pl.BlockSpec((1, tk, tn), lambda g, j, i, r: (blk(nbb, g, r), r, j))
        dims = (((0,), (0,)), ((), ()))
    o_spec = pl.BlockSpec((1, tm, tn), lambda g, j, i, r: (g, i, j))

    def body(a_ref, b_ref, o_ref, *scratch):
        prod = lax.dot_general(a_ref[0].astype(BF16), b_ref[0].astype(BF16), dims,
                               preferred_element_type=F32)
        if nred == 1:
            o_ref[0] = prod.astype(o_ref.dtype)
        else:
            acc = scratch[0]
            r = pl.program_id(3)

            @pl.when(r == 0)
            def _():
                acc[...] = prod

            @pl.when(r > 0)
            def _():
                acc[...] += prod

            @pl.when(r == nred - 1)
            def _():
                o_ref[0] = acc[...].astype(o_ref.dtype)

    return pl.pallas_call(
        body, name=name,
        out_shape=jax.ShapeDtypeStruct((nbo, m, n), out_dtype),
        grid=(nbo, n // tn, m // tm, nred),
        in_specs=[a_spec, b_spec], out_specs=o_spec,
        scratch_shapes=[pltpu.VMEM((tm, tn), F32)] if nred > 1 else [],
        compiler_params=_cparams(("parallel", "parallel", "parallel", "arbitrary")),
    )(a, b)


def _row_tile(t, widths):
    tm = max(16, ROW_TILE_ELEMS // max(widths))
    tm = min(tm, 256)
    return t if t < tm else tm


def _sel_index(sel, g, i, tm):
    if sel == "one":
        return 0
    if sel == "seg":
        return (i * tm) // L
    return g + sel


def _row_spec(arr, off, tm):
    return pl.BlockSpec((1, tm, arr.shape[2]), lambda g, i: (g + off, i, 0))


def _par_spec(arr, sel, tm):
    return pl.BlockSpec((1, 1, arr.shape[2]), lambda g, i: (_sel_index(sel, g, i, tm), 0, 0))


def _norm_ops(ops):
    return [(o[0], o[1], o[2] if len(o) > 2 else 1) for o in ops]


def _split_cols(vals, nsplit):
    out = []
    for v, ns in zip(vals, nsplit):
        w = v.shape[1] // ns
        out += [v] if ns == 1 else [v[:, q * w:(q + 1) * w] for q in range(ns)]
    return out


def _join_cols(flat, nsplit):
    out, pos = [], 0
    for ns in nsplit:
        out.append(flat[pos] if ns == 1 else jnp.concatenate(flat[pos:pos + ns], axis=1))
        pos += ns
    return out


def _rw_fwd(fn, rows, params, outs, *, name, nblk=None):
    rows, params = _norm_ops(rows), _norm_ops(params)
    t = rows[0][0].shape[1]
    nblk = nblk or rows[0][0].shape[0]
    tm = _row_tile(t, [r.shape[2] for r, _, _ in rows] + [w for w, _ in outs])
    nr, npar = len(rows), len(params)
    nsplit = [ns for _, _, ns in rows + params]

    def body(*refs):
        vals = _split_cols([r[0].astype(F32) for r in refs[:nr + npar]], nsplit)
        res = fn(*vals)
        for o_ref, o in zip(refs[nr + npar:], res):
            o_ref[0] = o.astype(o_ref.dtype)

    res = pl.pallas_call(
        body, name=name,
        out_shape=[jax.ShapeDtypeStruct((nblk, t, w), dt) for w, dt in outs],
        grid=(nblk, t // tm),
        in_specs=[_row_spec(r, off, tm) for r, off, _ in rows] + [_par_spec(p, s, tm) for p, s, _ in params],
        out_specs=[pl.BlockSpec((1, tm, w), lambda g, i: (g, i, 0)) for w, _ in outs],
        compiler_params=_cparams(("parallel", "parallel")),
    )(*[r for r, _, _ in rows], *[p for p, _, _ in params])
    return list(res)


def _rw_bwd(fn, rows, params, cts, *, name, row_grad=(), param_grad=(), add=None, nblk=None):
    rows, params = _norm_ops(rows), _norm_ops(params)
    t = cts[0].shape[1]
    nblk = nblk or cts[0].shape[0]
    tm = _row_tile(t, [r.shape[2] for r, _, _ in rows] + [c.shape[2] for c in cts])
    ni = t // tm
    nr, npar, nct = len(rows), len(params), len(cts)
    nadd = 0 if add is None else 1
    n_in = nr + npar + nct + nadd
    nsplit = [ns for _, _, ns in rows + params]

    def body(*refs):
        prim = _split_cols([r[0].astype(F32) for r in refs[:nr + npar]], nsplit)
        ct = tuple(r[0].astype(F32) for r in refs[nr + npar:nr + npar + nct])
        _, vjp = jax.vjp(fn, *prim)
        grads = _join_cols(list(vjp(ct)), nsplit)
        out_refs = refs[n_in:]
        for q, (ri, _) in enumerate(row_grad):
            gr = grads[ri]
            if q == 0 and nadd:
                gr = gr + refs[n_in - 1][0].astype(F32)
            out_refs[q][0] = gr.astype(out_refs[q].dtype)
        g, i = pl.program_id(0), pl.program_id(1)
        step = g * ni + i
        pg, pi = (step - 1) // ni, (step - 1) % ni
        for q, pidx in enumerate(param_grad):
            o_ref = out_refs[len(row_grad) + q]
            sel = params[pidx][1]
            val = grads[nr + pidx]
            if sel == "one":
                first = step == 0
            else:
                first = (step == 0) | (_sel_index(sel, g, i, tm) != _sel_index(sel, pg, pi, tm))

            @pl.when(first)
            def _(o_ref=o_ref, val=val):
                o_ref[0] = val

            @pl.when(jnp.logical_not(first))
            def _(o_ref=o_ref, val=val):
                o_ref[0] += val

    in_arrays = [r for r, _, _ in rows] + [p for p, _, _ in params] + list(cts) + ([add] if nadd else [])
    in_specs = ([_row_spec(r, off, tm) for r, off, _ in rows] + [_par_spec(p, s, tm) for p, s, _ in params]
                + [_row_spec(c, 0, tm) for c in cts] + ([_row_spec(add, 0, tm)] if nadd else []))
    out_shape, out_specs = [], []
    for ri, dt in row_grad:
        w = rows[ri][0].shape[2]
        out_shape.append(jax.ShapeDtypeStruct((nblk, t, w), dt))
        out_specs.append(pl.BlockSpec((1, tm, w), lambda g, i: (g, i, 0)))
    for pidx in param_grad:
        p, sel, _ = params[pidx]
        out_shape.append(jax.ShapeDtypeStruct(p.shape, F32))
        out_specs.append(_par_spec(p, sel, tm))
    res = pl.pallas_call(
        body, name=name, out_shape=out_shape, grid=(nblk, ni),
        in_specs=in_specs, out_specs=out_specs,
        compiler_params=_cparams(("arbitrary", "arbitrary")),
    )(*in_arrays)
    return list(res)


def _sigmoid(x):
    return 1.0 / (1.0 + jnp.exp(-x))


def _rms(x, g):
    return x * lax.rsqrt(jnp.mean(x * x, axis=-1, keepdims=True) + EPS) * g


def _ln(x, g, b):
    mu = jnp.mean(x, axis=-1, keepdims=True)
    xc = x - mu
    var = jnp.mean(xc * xc, axis=-1, keepdims=True)
    return xc * lax.rsqrt(var + EPS) * g + b


def _gelu_tanh(x):
    return 0.5 * x * (1.0 + jnp.tanh(0.7978845608028654 * (x + 0.044715 * (x * x * x))))


def f_modnorm(h, g, sc, sh):
    return (_rms(h, g) * (1.0 + sc) + sh,)


def f_gate_rms(y, gate, g):
    return (gate * _rms(y, g),)


def f_gate_rms_bias(y, gate, g, b):
    return (gate * _rms(y + b, g),)


def f_resgate(h, y, gate, g):
    return (h + gate * _rms(y, g),)


def f_resgate_bias(h, y, gate, g, b):
    return (h + gate * _rms(y + b, g),)


def f_glu(pa, pg, ba, bg):
    return ((pa + ba) * _sigmoid(pg + bg),)


def f_lnsilu(z, g, b):
    t = _ln(z, g, b)
    return (t * _sigmoid(t),)


def f_gmlp_pre(pu, pv, bu, bv, g, bb):
    return _gelu_tanh(pu + bu), _ln(_gelu_tanh(pv + bv), g, bb)


def f_ffn_gate(zg, zv):
    return (zg * _sigmoid(zg) * zv,)


def f_silu(x):
    return (x * _sigmoid(x),)


def f_silu_rows(dummy, cc):
    return (cc * _sigmoid(cc) + 0.0 * dummy,)


def _rope(x_in, tables, neg_sin, out_dtype, name):
    w = x_in.shape[2]
    sign = -1.0 if neg_sin else 1.0

    def fn(x, cos, sin):
        cos = jnp.tile(cos, (1, w // 128))
        sin = jnp.tile(sin, (1, w // 128)) * sign
        lane = lax.broadcasted_iota(jnp.int32, x.shape, 1) & 31
        rot = jnp.where(lane < 16, -pltpu.roll(x, w - 16, 1), pltpu.roll(x, 16, 1))
        return (x * cos + rot * sin,)

    return _rw_fwd(fn, [(x_in, 0), (tables[0], 0), (tables[1], 0)], [], [(w, out_dtype)], name=name)[0]


CONV_TM = 256
CONV_RC = 8


class _ShiftedRows:
    def __init__(self, xp, cb):
        self.xp, self.memo = xp, {}
        self.row = lax.broadcasted_iota(jnp.int32, (CONV_RC, cb), 0)

    def _get(self, key, make):
        if key not in self.memo:
            self.memo[key] = make()
        return self.memo[key]

    def chunk(self, a):
        return self._get(("c", a), lambda: self.xp[a:a + CONV_RC, :])

    def rot(self, a, j):
        return self._get(("r", a, j), lambda: pltpu.roll(self.chunk(a), CONV_RC - j, 0))

    def at(self, a):
        q, j = divmod(a, CONV_RC)
        if j == 0:
            return self.chunk(a)
        low = self._get(("m", j), lambda: self.row < CONV_RC - j)
        return jnp.where(low, self.rot(q * CONV_RC, j), self.rot(q * CONV_RC + CONV_RC, j))


def _conv_geometry(x, k):
    nb, t, w = x.shape
    halo = 16 if k > 17 else 8
    cb = _pick(w, (512,)) if w > 768 else w
    return nb, t, w, halo, cb, (k - 1) // 2


def _conv_in_specs(t, halo, cb):
    per = CONV_TM // halo
    last = t // halo - 1
    return [
        pl.BlockSpec((1, CONV_TM, cb), lambda g, jc, i: (g, i, jc)),
        pl.BlockSpec((1, halo, cb), lambda g, jc, i: (g, jnp.maximum(i * per - 1, 0), jc)),
        pl.BlockSpec((1, halo, cb), lambda g, jc, i: (g, jnp.minimum((i + 1) * per, last), jc)),
    ]


def _conv_fill(xp, x_ref, prev_ref, next_ref, halo, t):
    i = pl.program_id(2)
    seg_first = (i * CONV_TM == 0) | (i * CONV_TM == L)
    seg_last = ((i + 1) * CONV_TM == L) | ((i + 1) * CONV_TM == t)
    xp[0:halo, :] = jnp.where(seg_first, 0.0, prev_ref[0].astype(F32))
    xp[halo:halo + CONV_TM, :] = x_ref[0].astype(F32)
    xp[halo + CONV_TM:, :] = jnp.where(seg_last, 0.0, next_ref[0].astype(F32))


def _dwconv(x, w, b, *, name, out_dtype=F32):
    k = w.shape[1]
    nb, t, wd, halo, cb, half = _conv_geometry(x, k)
    base = halo - half

    def body(*refs):
        x_ref, prev_ref, next_ref, w_ref = refs[:4]
        b_ref = refs[4] if b is not None else None
        o_ref, xp = refs[-2], refs[-1]
        _conv_fill(xp, x_ref, prev_ref, next_ref, halo, t)
        rows = _ShiftedRows(xp, cb)
        for o in range(0, CONV_TM, CONV_RC):
            acc = None
            for kk in range(k):
                term = w_ref[0, kk:kk + 1, :] * rows.at(o + base + kk)
                acc = term if acc is None else acc + term
            if b_ref is not None:
                acc = acc + b_ref[0]
            o_ref[0, o:o + CONV_RC, :] = acc.astype(o_ref.dtype)

    in_specs = _conv_in_specs(t, halo, cb) + [pl.BlockSpec((1, k, cb), lambda g, jc, i: (g, 0, jc))]
    args = [x, x, x, w]
    if b is not None:
        in_specs.append(pl.BlockSpec((1, 1, cb), lambda g, jc, i: (g, 0, jc)))
        args.append(b)
    return pl.pallas_call(
        body, name=name, out_shape=jax.ShapeDtypeStruct((nb, t, wd), out_dtype),
        grid=(nb, wd // cb, t // CONV_TM), in_specs=in_specs,
        out_specs=pl.BlockSpec((1, CONV_TM, cb), lambda g, jc, i: (g, i, jc)),
        scratch_shapes=[pltpu.VMEM((CONV_TM + 2 * halo, cb), F32)],
        compiler_params=_cparams(("parallel", "parallel", "parallel")),
    )(*args)


def _dwconv_wgrad(x, dy, k, *, name):
    nb, t, wd, halo, cb, half = _conv_geometry(x, k)
    base = halo - half

    def body(x_ref, prev_ref, next_ref, dy_ref, dw_ref, db_ref, xp):
        _conv_fill(xp, x_ref, prev_ref, next_ref, halo, t)
        i = pl.program_id(2)

        @pl.when(i == 0)
        def _():
            dw_ref[...] = jnp.zeros_like(dw_ref)
            db_ref[...] = jnp.zeros_like(db_ref)

        rows = _ShiftedRows(xp, cb)
        dys = [dy_ref[0, o:o + CONV_RC, :].astype(F32) for o in range(0, CONV_TM, CONV_RC)]
        db_ref[0] += jnp.sum(sum(dys[1:], dys[0]), axis=0, keepdims=True)
        for kk in range(k):
            acc = None
            for ci, dyc in enumerate(dys):
                term = dyc * rows.at(ci * CONV_RC + base + kk)
                acc = term if acc is None else acc + term
            dw_ref[0, kk:kk + 1, :] += jnp.sum(acc, axis=0, keepdims=True)

    dw, db = pl.pallas_call(
        body, name=name,
        out_shape=[jax.ShapeDtypeStruct((nb, k, wd), F32), jax.ShapeDtypeStruct((nb, 1, wd), F32)],
        grid=(nb, wd // cb, t // CONV_TM),
        in_specs=_conv_in_specs(t, halo, cb) + [pl.BlockSpec((1, CONV_TM, cb), lambda g, jc, i: (g, i, jc))],
        out_specs=[pl.BlockSpec((1, k, cb), lambda g, jc, i: (g, 0, jc)),
                   pl.BlockSpec((1, 1, cb), lambda g, jc, i: (g, 0, jc))],
        scratch_shapes=[pltpu.VMEM((CONV_TM + 2 * halo, cb), F32)],
        compiler_params=_cparams(("parallel", "parallel", "arbitrary")),
    )(x, x, x, dy)
    return dw, db


ATTN_SCALE = HEAD_DIM ** -0.5
QROWS = Q_PER_KV * ATTN_BLOCK
NEG = -1e30


def _attn_scores(q, kw, kc, n):
    nt = (((1,), (1,)), ((), ()))
    s_w = lax.dot_general(q, kw, nt, preferred_element_type=F32) * ATTN_SCALE
    qi = lax.broadcasted_iota(jnp.int32, s_w.shape, 0) & (ATTN_BLOCK - 1)
    kj = lax.broadcasted_iota(jnp.int32, s_w.shape, 1)
    key_abs = (n - 1) * ATTN_BLOCK + kj
    ok = (jnp.abs(qi + ATTN_BLOCK - kj) <= ATTN_BLOCK) & (key_abs >= 0) & (key_abs < L)
    s_w = jnp.where(ok, s_w, NEG)
    s_c = lax.dot_general(q, kc, nt, preferred_element_type=F32) * ATTN_SCALE
    return s_w, s_c


def _sink_col(sink_ref, hk):
    return jnp.concatenate([jnp.full((ATTN_BLOCK, 1), sink_ref[hk * Q_PER_KV + g], F32) for g in range(Q_PER_KV)], axis=0)


def _attn_specs():
    qspec = pl.BlockSpec((Q_PER_KV, ATTN_BLOCK, HEAD_DIM), lambda hk, n: (hk, n, 0))
    kspec = pl.BlockSpec((1, L + 2 * ATTN_BLOCK, HEAD_DIM), lambda hk, n: (hk, 0, 0))
    cspec = pl.BlockSpec((1, LC, HEAD_DIM), lambda hk, n: (hk, 0, 0))
    lspec = pl.BlockSpec((Q_PER_KV, ATTN_BLOCK, 1), lambda hk, n: (hk, n, 0))
    sspec = pl.BlockSpec(memory_space=pltpu.SMEM)
    return qspec, kspec, cspec, lspec, sspec


def _attn_fwd(q, k, v, kc, vc, sink):
    qspec, kspec, cspec, lspec, sspec = _attn_specs()

    def body(q_ref, k_ref, v_ref, kc_ref, vc_ref, sink_ref, o_ref, lse_ref):
        hk, n = pl.program_id(0), pl.program_id(1)
        qv = q_ref[...].reshape(QROWS, HEAD_DIM)
        start = pl.multiple_of(n * ATTN_BLOCK, ATTN_BLOCK)
        kw = k_ref[0, pl.ds(start, 3 * ATTN_BLOCK), :]
        vw = v_ref[0, pl.ds(start, 3 * ATTN_BLOCK), :]
        s_w, s_c = _attn_scores(qv, kw, kc_ref[0], n)
        sk = _sink_col(sink_ref, hk)
        m = jnp.maximum(jnp.maximum(jnp.max(s_w, -1, keepdims=True), jnp.max(s_c, -1, keepdims=True)), sk)
        p_w, p_c = jnp.exp(s_w - m), jnp.exp(s_c - m)
        den = jnp.sum(p_w, -1, keepdims=True) + jnp.sum(p_c, -1, keepdims=True) + jnp.exp(sk - m)
        o = (jnp.dot(p_w.astype(BF16), vw, preferred_element_type=F32)
             + jnp.dot(p_c.astype(BF16), vc_ref[0], preferred_element_type=F32)) / den
        o_ref[...] = o.reshape(Q_PER_KV, ATTN_BLOCK, HEAD_DIM).astype(o_ref.dtype)
        lse_ref[...] = (m + jnp.log(den)).reshape(Q_PER_KV, ATTN_BLOCK, 1)

    return pl.pallas_call(
        body, name="attn_fwd",
        out_shape=[jax.ShapeDtypeStruct((N_Q, L, HEAD_DIM), BF16), jax.ShapeDtypeStruct((N_Q, L, 1), F32)],
        grid=(N_KV, L // ATTN_BLOCK),
        in_specs=[qspec, kspec, kspec, cspec, cspec, sspec], out_specs=[qspec, lspec],
        compiler_params=_cparams(("parallel", "parallel")),
    )(q, k, v, kc, vc, sink)


def _attn_bwd(q, k, v, kc, vc, sink, o, lse, do):
    qspec, kspec, cspec, lspec, sspec = _attn_specs()
    tn = (((0,), (0,)), ((), ()))
    nt = (((1,), (1,)), ((), ()))

    def body(q_ref, k_ref, v_ref, kc_ref, vc_ref, sink_ref, o_ref, lse_ref, do_ref,
             dq_ref, dk_ref, dv_ref, dkc_ref, dvc_ref, dsink_ref):
        hk, n = pl.program_id(0), pl.program_id(1)
        qv = q_ref[...].reshape(QROWS, HEAD_DIM)
        start = pl.multiple_of(n * ATTN_BLOCK, ATTN_BLOCK)
        win = pl.ds(start, 3 * ATTN_BLOCK)
        kw, vw = k_ref[0, win, :], v_ref[0, win, :]
        kcv, vcv = kc_ref[0], vc_ref[0]
        s_w, s_c = _attn_scores(qv, kw, kcv, n)
        lse_v = lse_ref[...].reshape(QROWS, 1)
        p_w, p_c = jnp.exp(s_w - lse_v), jnp.exp(s_c - lse_v)
        dov = do_ref[...].reshape(QROWS, HEAD_DIM).astype(F32)
        ov = o_ref[...].reshape(QROWS, HEAD_DIM).astype(F32)
        delta = jnp.sum(dov * ov, -1, keepdims=True)
        dob = dov.astype(BF16)
        dp_w = lax.dot_general(dob, vw, nt, preferred_element_type=F32)
        dp_c = lax.dot_general(dob, vcv, nt, preferred_element_type=F32)
        ds_w = (p_w * (dp_w - delta) * ATTN_SCALE).astype(BF16)
        ds_c = (p_c * (dp_c - delta) * ATTN_SCALE).astype(BF16)
        dq = jnp.dot(ds_w, kw, preferred_element_type=F32) + jnp.dot(ds_c, kcv, preferred_element_type=F32)
        dq_ref[...] = dq.reshape(Q_PER_KV, ATTN_BLOCK, HEAD_DIM)

        @pl.when(n == 0)
        def _():
            dk_ref[...] = jnp.zeros_like(dk_ref)
            dv_ref[...] = jnp.zeros_like(dv_ref)
            dkc_ref[...] = jnp.zeros_like(dkc_ref)
            dvc_ref[...] = jnp.zeros_like(dvc_ref)

        dk_ref[0, win, :] += lax.dot_general(ds_w, qv, tn, preferred_element_type=F32)
        dv_ref[0, win, :] += lax.dot_general(p_w.astype(BF16), dob, tn, preferred_element_type=F32)
        dkc_ref[0] += lax.dot_general(ds_c, qv, tn, preferred_element_type=F32)
        dvc_ref[0] += lax.dot_general(p_c.astype(BF16), dob, tn, preferred_element_type=F32)
        dsk = -jnp.exp(_sink_col(sink_ref, hk) - lse_v) * delta
        for g in range(Q_PER_KV):
            part = jnp.sum(dsk[g * ATTN_BLOCK:(g + 1) * ATTN_BLOCK])
            idx = hk * Q_PER_KV + g

            @pl.when(n == 0)
            def _(part=part, idx=idx):
                dsink_ref[idx] = part

            @pl.when(n > 0)
            def _(part=part, idx=idx):
                dsink_ref[idx] += part

    kshape = jax.ShapeDtypeStruct((N_KV, L + 2 * ATTN_BLOCK, HEAD_DIM), F32)
    cshape = jax.ShapeDtypeStruct((N_KV, LC, HEAD_DIM), F32)
    return pl.pallas_call(
        body, name="attn_bwd",
        out_shape=[jax.ShapeDtypeStruct((N_Q, L, HEAD_DIM), F32), kshape, kshape, cshape, cshape,
                   jax.ShapeDtypeStruct((N_Q,), F32)],
        grid=(N_KV, L // ATTN_BLOCK),
        in_specs=[qspec, kspec, kspec, cspec, cspec, sspec, qspec, lspec, qspec],
        out_specs=[qspec, kspec, kspec, cspec, cspec, sspec],
        compiler_params=_cparams(("arbitrary", "arbitrary")),
    )(q, k, v, kc, vc, sink, o, lse, do)


def _gm_specs():
    rspec = pl.BlockSpec((1, GM_CHUNK, GM_W), lambda n: (0, n, 0))
    wspec = pl.BlockSpec((GM_GROUPS, GM_CHUNK, GM_CHUNK), lambda n: (0, 0, 0))
    bspec = pl.BlockSpec((GM_GROUPS, GM_CHUNK, 1), lambda n: (0, 0, 0))
    return rspec, wspec, bspec


def _gm_spatial_fwd(u, v, ws, bs):
    rspec, wspec, bspec = _gm_specs()

    def body(u_ref, v_ref, ws_ref, bs_ref, o_ref):
        for g in range(GM_GROUPS):
            cols = slice(g * GM_CHUNK, (g + 1) * GM_CHUNK)
            s = jnp.dot(ws_ref[g], v_ref[0, :, cols], preferred_element_type=F32) + bs_ref[g]
            o_ref[0, :, cols] = (u_ref[0, :, cols] * s).astype(o_ref.dtype)

    return pl.pallas_call(
        body, name="gm_spatial_fwd", out_shape=jax.ShapeDtypeStruct((1, L, GM_W), BF16),
        grid=(L // GM_CHUNK,), in_specs=[rspec, rspec, wspec, bspec], out_specs=rspec,
        compiler_params=_cparams(("parallel",)),
    )(u, v, ws, bs)


def _gm_spatial_bwd(u, v, ws, bs, dus):
    rspec, wspec, bspec = _gm_specs()
    tn = (((0,), (0,)), ((), ()))
    nt = (((1,), (1,)), ((), ()))

    def body(u_ref, v_ref, ws_ref, bs_ref, d_ref, du_ref, dv_ref, dws_ref, dbs_ref):
        n = pl.program_id(0)

        @pl.when(n == 0)
        def _():
            dws_ref[...] = jnp.zeros_like(dws_ref)
            dbs_ref[...] = jnp.zeros_like(dbs_ref)

        for g in range(GM_GROUPS):
            cols = slice(g * GM_CHUNK, (g + 1) * GM_CHUNK)
            vb = v_ref[0, :, cols]
            s = jnp.dot(ws_ref[g], vb, preferred_element_type=F32) + bs_ref[g]
            d = d_ref[0, :, cols].astype(F32)
            du_ref[0, :, cols] = d * s
            ds = d * u_ref[0, :, cols]
            dsb = ds.astype(BF16)
            dv_ref[0, :, cols] = lax.dot_general(ws_ref[g], dsb, tn, preferred_element_type=F32)
            dws_ref[g] += lax.dot_general(dsb, vb, nt, preferred_element_type=F32)
            dbs_ref[g] += jnp.sum(ds, axis=1, keepdims=True)

    row = jax.ShapeDtypeStruct((1, L, GM_W), F32)
    return pl.pallas_call(
        body, name="gm_spatial_bwd",
        out_shape=[row, row, jax.ShapeDtypeStruct((GM_GROUPS, GM_CHUNK, GM_CHUNK), F32),
                   jax.ShapeDtypeStruct((GM_GROUPS, GM_CHUNK, 1), F32)],
        grid=(L // GM_CHUNK,), in_specs=[rspec, rspec, wspec, bspec, rspec],
        out_specs=[rspec, rspec, wspec, bspec],
        compiler_params=_cparams(("arbitrary",)),
    )(u, v, ws, bs, dus)


def _loss_head(h, target):
    tm = 256

    def body(h_ref, t_ref, dh_ref, loss_ref):
        d = h_ref[0] - t_ref[0]
        dh_ref[0] = d * (1.0 / D)

        @pl.when(pl.program_id(0) == 0)
        def _():
            loss_ref[...] = jnp.zeros_like(loss_ref)

        loss_ref[...] += jnp.sum(d * d) * (0.5 / D)

    spec = pl.BlockSpec((1, tm, D), lambda i: (0, i, 0))
    dh, loss = pl.pallas_call(
        body, name="loss_head",
        out_shape=[jax.ShapeDtypeStruct((1, L, D), F32), jax.ShapeDtypeStruct((8, 128), F32)],
        grid=(L // tm,), in_specs=[spec, spec],
        out_specs=[spec, pl.BlockSpec((8, 128), lambda i: (0, 0))],
        compiler_params=_cparams(("arbitrary",)),
    )(h, target)
    return dh, loss[0, 0]


def _adamw(parts, w, m, v, name):
    per_layer = isinstance(parts, (list, tuple))
    plist = list(parts) if per_layer else [parts]
    nl = len(plist) if per_layer else parts.shape[0]
    s, r, c = plist[0].shape[-3:]
    tr = r
    for cand in (512, 256, 128, 64, 32, 16):
        if r % cand == 0 and cand * c <= 131072:
            tr = cand
            break
    nr = r // tr
    npart = len(plist)
    c1 = 1.0 / (1.0 - ADAM_B1 ** ADAM_STEP)
    c2 = 1.0 / (1.0 - ADAM_B2 ** ADAM_STEP)

    def body(*refs):
        w_ref, m_ref, v_ref, g_ref, d_ref, nm_ref, nv_ref = refs[npart:]

        def update(read):
            g = read(0).astype(F32)
            for q in range(1, s):
                g = g + read(q).astype(F32)
            mn = ADAM_B1 * m_ref[0] + (1.0 - ADAM_B1) * g
            vn = ADAM_B2 * v_ref[0] + (1.0 - ADAM_B2) * (g * g)
            g_ref[0] = g
            nm_ref[0] = mn
            nv_ref[0] = vn
            d_ref[0] = -ADAM_LR * ((mn * c1) / (jnp.sqrt(vn * c2) + ADAM_EPS) + ADAM_WD * w_ref[0])

        if not per_layer:
            update(lambda q: refs[0][0, q])
        else:
            for l in range(nl):
                @pl.when(pl.program_id(0) == l)
                def _(l=l):
                    update(lambda q: refs[l][q])

    spec = pl.BlockSpec((1, tr, c), lambda li, i: (li, i, 0))
    shp = jax.ShapeDtypeStruct((nl, r, c), F32)
    if per_layer:
        pspecs = [pl.BlockSpec((s, tr, c), lambda li, i, l=l: (0, jnp.where(li == l, i, jnp.where(li > l, nr - 1, 0)), 0))
                  for l in range(nl)]
    else:
        pspecs = [pl.BlockSpec((1, s, tr, c), lambda li, i: (li, 0, i, 0))]
    return pl.pallas_call(
        body, name=name, out_shape=[shp] * 4, grid=(nl, nr),
        in_specs=pspecs + [spec, spec, spec], out_specs=[spec] * 4,
        compiler_params=_cparams(("arbitrary", "arbitrary")),
    )(*plist, w, m, v)


def _pack_rows(vecs, lanes=128, mult=8):
    flat = jnp.concatenate([v.reshape(-1) for v in vecs])
    n = flat.shape[0]
    rows = -(-n // (mult * lanes)) * mult
    return jnp.pad(flat, (0, rows * lanes - n)).reshape(rows, lanes)


def _unpack_rows(packed, shapes):
    flat = packed.reshape(-1)
    out, pos = [], 0
    for s in shapes:
        n = 1
        for d_ in s:
            n *= d_
        out.append(flat[pos:pos + n].reshape(s))
        pos += n
    return out


def _unshard_last(g):
    lead = g.shape[1:-1]
    return jnp.moveaxis(g, 0, -2).reshape(*lead, NDEV * g.shape[-1])


def _shard_last(full):
    lead, w = full.shape[:-1], full.shape[-1] // NDEV
    return jnp.moveaxis(full.reshape(*lead, NDEV, w), -2, 0)


def _rope_tables():
    rows = L // GRID_W
    row = jnp.repeat(jnp.arange(rows), GRID_W).astype(F32)
    col = jnp.tile(jnp.arange(GRID_W), rows).astype(F32)
    axis_dim = HEAD_DIM // 2
    inv_freq = ROPE_BASE ** (-jnp.arange(0, axis_dim, 2, dtype=F32) / axis_dim)
    ang_r, ang_c = row[:, None] * inv_freq[None, :], col[:, None] * inv_freq[None, :]
    ang = jnp.concatenate([ang_r, ang_r, ang_c, ang_c], axis=-1)
    ang = jnp.concatenate([ang, ang], axis=-1)[None]
    return jnp.cos(ang), jnp.sin(ang)


def _heads(x, nh):
    t = x.shape[1]
    return x.reshape(t, nh, HEAD_DIM).transpose(1, 0, 2)


def _unheads(x):
    nh, t, _ = x.shape
    return x.transpose(1, 0, 2).reshape(1, t, nh * HEAD_DIM)


FFN_HALO = 16
FFN_PAIRS = 4


def _ffn_tile(t):
    return 512 if t == L else 256


def _halo_specs(t, tm, block, index):
    per, last = tm // FFN_HALO, t // FFN_HALO - 1
    return [pl.BlockSpec(block(tm), lambda d, i: index(d, i)),
            pl.BlockSpec(block(FFN_HALO), lambda d, i: index(d, jnp.maximum(i * per - 1, 0))),
            pl.BlockSpec(block(FFN_HALO), lambda d, i: index(d, jnp.minimum((i + 1) * per, last)))]


def _seg_edges(i, tm, t):
    return (i * tm == 0) | (i * tm == L), ((i + 1) * tm == L) | ((i + 1) * tm == t)


FFN_RC = 8


def _sigmoid_t(x):
    return 0.5 * jnp.tanh(0.5 * x) + 0.5


class _RowShifts:
    def __init__(self, buf, s):
        self.buf, self.s, self.memo = buf, s, {}
        rows = lax.broadcasted_iota(jnp.int32, (FFN_RC, FFN_BLK), 0)
        self.first, self.last = rows == 0, rows == FFN_RC - 1

    def chunk(self, r):
        if r not in self.memo:
            self.memo[r] = self.buf[self.s, r:r + FFN_RC, :]
        return self.memo[r]

    def rot(self, r, by):
        if (r, by) not in self.memo:
            self.memo[(r, by)] = pltpu.roll(self.chunk(r), by, 0)
        return self.memo[(r, by)]

    def triple(self, r):
        before = jnp.where(self.first, self.rot(r - FFN_RC, 1), self.rot(r, 1))
        behind = jnp.where(self.last, self.rot(r + FFN_RC, FFN_RC - 1), self.rot(r, FFN_RC - 1))
        return before, self.chunk(r), behind


def _conv3_of(triple, cw_ref, s, flip=False):
    taps = [cw_ref[s, 0, k:k + 1, :] for k in ((2, 1, 0) if flip else (0, 1, 2))]
    return taps[0] * triple[0] + taps[1] * triple[1] + taps[2] * triple[2]


def _ffn_core_fwd(a2, up, cw, cb, down, name):
    t = a2.shape[1]
    tm = _ffn_tile(t)
    h0 = FFN_HALO

    def body(a_ref, ap_ref, an_ref, up_ref, cw_ref, cb_ref, dn_ref, z_ref, f_ref, abuf, zbuf, ubuf):
        d, i = pl.program_id(0), pl.program_id(1)
        seg_first, seg_last = _seg_edges(i, tm, t)
        abuf[0:h0, :] = ap_ref[0]
        abuf[h0:h0 + tm, :] = a_ref[0]
        abuf[h0 + tm:, :] = an_ref[0]
        for s in range(2):
            zbuf[s] = jnp.dot(abuf[...], up_ref[s, 0], preferred_element_type=F32)

        @pl.when(seg_first)
        def _():
            zbuf[:, 0:h0, :] = jnp.zeros((2, h0, FFN_BLK), F32)

        @pl.when(seg_last)
        def _():
            zbuf[:, h0 + tm:, :] = jnp.zeros((2, h0, FFN_BLK), F32)

        for s in range(2):
            z_ref[s, 0] = zbuf[s, h0:h0 + tm, :].astype(z_ref.dtype)

        zs = [_RowShifts(zbuf, 0), _RowShifts(zbuf, 1)]
        for r in range(h0, h0 + tm, FFN_RC):
            zg = _conv3_of(zs[0].triple(r), cw_ref, 0) + cb_ref[0, 0]
            zv = _conv3_of(zs[1].triple(r), cw_ref, 1) + cb_ref[1, 0]
            ubuf[r - h0:r - h0 + FFN_RC, :] = zg * _sigmoid_t(zg) * zv
        prod = jnp.dot(ubuf[...].astype(BF16), dn_ref[0], preferred_element_type=F32)
        rows = pl.ds(pl.multiple_of(i * tm, tm), tm)

        @pl.when(d == 0)
        def _():
            f_ref[0, rows, :] = prod

        @pl.when(d > 0)
        def _():
            f_ref[0, rows, :] += prod

    pair = lambda r, c: pl.BlockSpec((2, 1, r, c), lambda d, i: (0, d, 0, 0))
    return pl.pallas_call(
        body, name=name,
        out_shape=[jax.ShapeDtypeStruct((2, FFN_PAIRS, t, FFN_BLK), BF16), jax.ShapeDtypeStruct((1, t, D), F32)],
        grid=(FFN_PAIRS, t // tm),
        in_specs=_halo_specs(t, tm, lambda r: (1, r, D), lambda d, i: (0, i, 0))
        + [pair(D, FFN_BLK), pair(FFN_K, FFN_BLK), pair(1, FFN_BLK),
           pl.BlockSpec((1, FFN_BLK, D), lambda d, i: (d, 0, 0))],
        out_specs=[pl.BlockSpec((2, 1, tm, FFN_BLK), lambda d, i: (0, d, i, 0)),
                   pl.BlockSpec((1, t, D), lambda d, i: (0, 0, 0))],
        scratch_shapes=[pltpu.VMEM((tm + 2 * h0, D), BF16), pltpu.VMEM((2, tm + 2 * h0, FFN_BLK), F32),
                        pltpu.VMEM((tm, FFN_BLK), F32)],
        compiler_params=_cparams(("arbitrary", "arbitrary")),
    )(a2, a2, a2, up, cw, cb, down)


def _ffn_core_bwd(df, z, cw, cb, down, up, a2, name):
    t = df.shape[1]
    tm = _ffn_tile(t)
    h0 = FFN_HALO
    ni = t // tm
    w0, wn = h0 // 2, tm + h0
    tn = (((0,), (0,)), ((), ()))
    nt = (((1,), (1,)), ((), ()))

    def body(df_ref, dfp_ref, dfn_ref, z_ref, zp_ref, zn_ref, cw_ref, cb_ref, dn_ref, up_ref, a2_ref,
             dcw_ref, dcb_ref, ddn_ref, da_ref, dup_ref, dfbuf, zbuf, dzbuf, acc, dubuf, dzo, acc_up):
        d, i = pl.program_id(0), pl.program_id(1)
        seg_first, seg_last = _seg_edges(i, tm, t)
        dfbuf[0:h0, :] = dfp_ref[0]
        dfbuf[h0:h0 + tm, :] = df_ref[0]
        dfbuf[h0 + tm:, :] = dfn_ref[0]
        for s in range(2):
            zbuf[s, 0:h0, :] = zp_ref[s, 0].astype(F32)
            zbuf[s, h0:h0 + tm, :] = z_ref[s, 0].astype(F32)
            zbuf[s, h0 + tm:, :] = zn_ref[s, 0].astype(F32)

        @pl.when(seg_first)
        def _():
            zbuf[:, 0:h0, :] = jnp.zeros((2, h0, FFN_BLK), F32)

        @pl.when(seg_last)
        def _():
            zbuf[:, h0 + tm:, :] = jnp.zeros((2, h0, FFN_BLK), F32)

        dubuf[...] = lax.dot_general(dfbuf[...], dn_ref[0], nt, preferred_element_type=F32)

        zs = [_RowShifts(zbuf, 0), _RowShifts(zbuf, 1)]
        sums = [[jnp.zeros((FFN_RC, FFN_BLK), F32)] * (FFN_K + 1) for _ in range(2)]
        for r in range(w0, w0 + wn, FFN_RC):
            tz = [zs[0].triple(r), zs[1].triple(r)]
            zg = _conv3_of(tz[0], cw_ref, 0) + cb_ref[0, 0]
            zv = _conv3_of(tz[1], cw_ref, 1) + cb_ref[1, 0]
            sg = _sigmoid_t(zg)
            silu = zg * sg
            du = dubuf[r:r + FFN_RC, :]
            dzc = [du * zv * (sg * (1.0 + zg * (1.0 - sg))), du * silu]
            dzbuf[0, r:r + FFN_RC, :] = dzc[0]
            dzbuf[1, r:r + FFN_RC, :] = dzc[1]
            dubuf[r:r + FFN_RC, :] = silu * zv
            if h0 <= r < h0 + tm:
                for s in range(2):
                    sums[s] = [sums[s][k] + dzc[s] * tz[s][k] for k in range(FFN_K)] + [sums[s][FFN_K] + dzc[s]]

        @pl.when(seg_first)
        def _():
            dzbuf[:, w0:h0, :] = jnp.zeros((2, h0 - w0, FFN_BLK), F32)

        @pl.when(seg_last)
        def _():
            dzbuf[:, h0 + tm:w0 + wn, :] = jnp.zeros((2, w0, FFN_BLK), F32)

        @pl.when(i == 0)
        def _():
            dcw_ref[...] = jnp.zeros_like(dcw_ref)
            dcb_ref[...] = jnp.zeros_like(dcb_ref)

        da = None
        for s in range(2):
            dzs = _RowShifts(dzbuf, s)
            for r in range(h0, h0 + tm, FFN_RC):
                dzo[r - h0:r - h0 + FFN_RC, :] = _conv3_of(dzs.triple(r), cw_ref, s, flip=True)
            dzb = dzo[...].astype(BF16)
            part = lax.dot_general(dzb, up_ref[s, 0], nt, preferred_element_type=F32)
            da = part if da is None else da + part
            gup = lax.dot_general(dzb, a2_ref[0], tn, preferred_element_type=F32)

            @pl.when(i == 0)
            def _(s=s, gup=gup):
                acc_up[s] = gup

            @pl.when(i > 0)
            def _(s=s, gup=gup):
                acc_up[s] += gup
            for k in range(FFN_K):
                dcw_ref[s, 0, k:k + 1, :] += jnp.sum(sums[s][k], axis=0, keepdims=True)
            dcb_ref[s, 0] += jnp.sum(sums[s][FFN_K], axis=0, keepdims=True)
        rows = pl.ds(pl.multiple_of(i * tm, tm), tm)

        @pl.when(d == 0)
        def _():
            da_ref[0, rows, :] = da

        @pl.when(d > 0)
        def _():
            da_ref[0, rows, :] += da

        prod = lax.dot_general(dubuf[h0:h0 + tm, :].astype(BF16), dfbuf[h0:h0 + tm, :], tn, preferred_element_type=F32)

        @pl.when(i == 0)
        def _():
            acc[...] = prod

        @pl.when(i > 0)
        def _():
            acc[...] += prod

        @pl.when(i == ni - 1)
        def _():
            ddn_ref[0] = acc[...].astype(ddn_ref.dtype)
            dup_ref[:, 0] = acc_up[...].astype(dup_ref.dtype)

    pair = lambda r, c: pl.BlockSpec((2, 1, r, c), lambda d, i: (0, d, 0, 0))
    return pl.pallas_call(
        body, name=name,
        out_shape=[jax.ShapeDtypeStruct((2, FFN_PAIRS, FFN_K, FFN_BLK), F32),
                   jax.ShapeDtypeStruct((2, FFN_PAIRS, 1, FFN_BLK), F32),
                   jax.ShapeDtypeStruct((FFN_PAIRS, FFN_BLK, D), BF16), jax.ShapeDtypeStruct((1, t, D), F32),
                   jax.ShapeDtypeStruct((2, FFN_PAIRS, FFN_BLK, D), BF16)],
        grid=(FFN_PAIRS, ni),
        in_specs=_halo_specs(t, tm, lambda r: (1, r, D), lambda d, i: (0, i, 0))
        + _halo_specs(t, tm, lambda r: (2, 1, r, FFN_BLK), lambda d, i: (0, d, i, 0))
        + [pair(FFN_K, FFN_BLK), pair(1, FFN_BLK), pl.BlockSpec((1, FFN_BLK, D), lambda d, i: (d, 0, 0)),
           pair(D, FFN_BLK), pl.BlockSpec((1, tm, D), lambda d, i: (0, i, 0))],
        out_specs=[pair(FFN_K, FFN_BLK),
                   pair(1, FFN_BLK), pl.BlockSpec((1, FFN_BLK, D), lambda d, i: (d, 0, 0)),
                   pl.BlockSpec((1, t, D), lambda d, i: (0, 0, 0)), pair(FFN_BLK, D)],
        scratch_shapes=[pltpu.VMEM((tm + 2 * h0, D), BF16), pltpu.VMEM((2, tm + 2 * h0, FFN_BLK), F32),
                        pltpu.VMEM((2, tm + 2 * h0, FFN_BLK), F32), pltpu.VMEM((FFN_BLK, D), F32),
                        pltpu.VMEM((tm + 2 * h0, FFN_BLK), F32),
                        pltpu.VMEM((tm, FFN_BLK), F32), pltpu.VMEM((2, FFN_BLK, D), F32)],
        compiler_params=_cparams(("arbitrary", "arbitrary")),
    )(df, df, df, z, z, z, cw, cb, down, up, a2)


def _ffn_fwd(i, h, mod, ng, wts):
    a2 = _rw_fwd(f_modnorm, [(h, 0)], [(ng[2], "one"), (mod["sc2"], "seg"), (mod["sh2"], "seg")],
                 [(D, BF16)], name=f"ffn{i}_norm")[0]
    z, f = _ffn_core_fwd(a2, wts["up"], wts["cw"], wts["cb"], wts["down"], f"ffn{i}_core")
    h2 = _rw_fwd(f_resgate, [(h, 0), (f, 0)], [(mod["g2"], "seg"), (ng[3], "one")], [(D, F32)],
                 name=f"ffn{i}_res")[0]
    return h2, (h, a2, z, f)


def _ffn_bwd(i, dh, res, mod, ng, wts):
    h, a2, z, f = res
    t = h.shape[1]
    df, dg2, dng3 = _rw_bwd(f_gate_rms, [(f, 0)], [(mod["g2"], "seg"), (ng[3], "one")], [dh],
                            name=f"ffn{i}_res_b", row_grad=[(0, BF16)], param_grad=[0, 1])
    dcw, dcb, d_down, da2, d_up = _ffn_core_bwd(df, z, wts["cw"], wts["cb"], wts["down"], wts["up"], a2,
                                                f"ffn{i}_core_b")
    dcw, dcb = dcw.reshape(NDEV, FFN_K, FFN_BLK), dcb.reshape(NDEV, 1, FFN_BLK)
    d_up = d_up.reshape(NDEV, FFN_BLK, D)
    dh_in, dng2, dsc2, dsh2 = _rw_bwd(
        f_modnorm, [(h, 0)], [(ng[2], "one"), (mod["sc2"], "seg"), (mod["sh2"], "seg")], [da2],
        name=f"ffn{i}_norm_b", row_grad=[(0, F32)], param_grad=[0, 1, 2], add=dh)
    grads = dict(up=d_up, down=d_down, cw=dcw, cb=dcb, ng2=dng2, ng3=dng3, sc2=dsc2, sh2=dsh2, g2=dg2)
    return dh_in, grads


def _mixer_norm_fwd(i, h, mod, ng):
    return _rw_fwd(f_modnorm, [(h, 0)], [(ng[0], "one"), (mod["sc1"], "seg"), (mod["sh1"], "seg")],
                   [(D, BF16)], name=f"mix{i}_norm")[0]


def _mixer_norm_bwd(i, h, mod, ng, da, dh):
    return _rw_bwd(f_modnorm, [(h, 0)], [(ng[0], "one"), (mod["sc1"], "seg"), (mod["sh1"], "seg")], [da],
                   name=f"mix{i}_norm_b", row_grad=[(0, F32)], param_grad=[0, 1, 2], add=dh)


def _conformer_fwd(i, h, mod, ng, wts):
    a = _mixer_norm_fwd(i, h, mod, ng)
    p = _mm(a, wts["w_in"], "nn", out_dtype=BF16, name=f"cm{i}_in")
    z = _rw_fwd(f_glu, [(p, 0, 2)], [(wts["b_in"], "one", 2)], [(D, F32)], name=f"cm{i}_glu")[0]
    zc = _dwconv(z, wts["dw_w"], wts["dw_b"], name=f"cm{i}_conv")
    r = _rw_fwd(f_lnsilu, [(zc, 0)], [(wts["ln_g"], "one"), (wts["ln_b"], "one")], [(D, BF16)],
                name=f"cm{i}_ln")[0]
    y = _mm(r, wts["w_out"], "nn", name=f"cm{i}_out")
    h2 = _rw_fwd(f_resgate_bias, [(h, 0), (y, 0)], [(mod["g1"], "seg"), (ng[1], "one"), (wts["b_out"], "one")],
                 [(D, F32)], name=f"cm{i}_res")[0]
    return h2, (h, a, p, z, zc, r, y)


def _conformer_bwd(i, dh, res, mod, ng, wts):
    h, a, p, z, zc, r, y = res
    dy, dg1, dng1, db_out = _rw_bwd(
        f_gate_rms_bias, [(y, 0)], [(mod["g1"], "seg"), (ng[1], "one"), (wts["b_out"], "one")], [dh],
        name=f"cm{i}_res_b", row_grad=[(0, BF16)], param_grad=[0, 1, 2])
    dr = _mm(dy, wts["w_out"], "nt", name=f"cm{i}_out_bx")
    d_w_out = _mm(r, dy, "tn", out_dtype=BF16, name=f"cm{i}_out_bw")
    dzc, dln_g, dln_b = _rw_bwd(f_lnsilu, [(zc, 0)], [(wts["ln_g"], "one"), (wts["ln_b"], "one")], [dr],
                                name=f"cm{i}_ln_b", row_grad=[(0, F32)], param_grad=[0, 1])
    ddw_w, ddw_b = _dwconv_wgrad(z, dzc, CM_K, name=f"cm{i}_conv_bw")
    dz = _dwconv(dzc, wts["dw_w"][:, ::-1, :], None, name=f"cm{i}_conv_bx")
    dp, db_in = _rw_bwd(f_glu, [(p, 0, 2)], [(wts["b_in"], "one", 2)], [dz], name=f"cm{i}_glu_b",
                        row_grad=[(0, BF16)], param_grad=[0])
    d_w_in = _mm(a, dp, "tn", out_dtype=BF16, name=f"cm{i}_in_bw")
    da = _mm(dp, wts["w_in"], "nt", name=f"cm{i}_in_bx")
    dh_in, dng0, dsc1, dsh1 = _mixer_norm_bwd(i, h, mod, ng, da, dh)
    grads = dict(w_in=d_w_in, w_out=d_w_out, b_in=db_in, dw_w=ddw_w, dw_b=ddw_b, ln_g=dln_g, ln_b=dln_b,
                 b_out=db_out, ng0=dng0, ng1=dng1, sc1=dsc1, sh1=dsh1, g1=dg1)
    return dh_in, grads


def _attention_fwd(i, h_all, mod, ng, wts, tables):
    a = _mixer_norm_fwd(i, h_all, mod, ng)
    qkv = _mm(a, wts["w_qkv"], "nn", name="attn_qkv")
    kv0 = N_Q * HEAD_DIM
    kv1 = kv0 + N_KV * HEAD_DIM
    q = _rope(qkv[:, :L, :kv0], tables, False, BF16, "attn_rope_q")
    k = _rope(qkv[:, :L, kv0:kv1], tables, False, BF16, "attn_rope_k")
    pad = ((0, 0), (ATTN_BLOCK, ATTN_BLOCK), (0, 0))
    q_h = _heads(q, N_Q)
    k_h = jnp.pad(_heads(k, N_KV), pad)
    v_h = jnp.pad(_heads(qkv[:, :L, kv1:].astype(BF16), N_KV), pad)
    kc_h = _heads(qkv[:, L:, kv0:kv1].astype(BF16), N_KV)
    vc_h = _heads(qkv[:, L:, kv1:].astype(BF16), N_KV)
    o_h, lse = _attn_fwd(q_h, k_h, v_h, kc_h, vc_h, wts["sink"])
    o = _unheads(o_h)
    y = _mm(o, wts["w_o"], "nn", name="attn_o")
    h_lat = h_all[:, :L]
    mod_lat = {k_: v_[:1] for k_, v_ in mod.items()}
    h2 = _rw_fwd(f_resgate, [(h_lat, 0), (y, 0)], [(mod_lat["g1"], "seg"), (ng[1], "one")], [(D, F32)],
                 name="attn_res")[0]
    return h2, (h_all, a, q_h, k_h, v_h, kc_h, vc_h, o_h, lse, o, y)


def _attention_bwd(i, dh, res, mod, ng, wts, tables):
    h_all, a, q_h, k_h, v_h, kc_h, vc_h, o_h, lse, o, y = res
    mod_lat = {k_: v_[:1] for k_, v_ in mod.items()}
    dy, dg1, dng1 = _rw_bwd(f_gate_rms, [(y, 0)], [(mod_lat["g1"], "seg"), (ng[1], "one")], [dh],
                            name="attn_res_b", row_grad=[(0, BF16)], param_grad=[0, 1])
    do = _mm(dy, wts["w_o"], "nt", name="attn_o_bx")
    d_w_o = _mm(o, dy, "tn", out_dtype=BF16, name="attn_o_bw")
    dq_h, dk_h, dv_h, dkc_h, dvc_h, dsink = _attn_bwd(q_h, k_h, v_h, kc_h, vc_h, wts["sink"], o_h, lse,
                                                        _heads(do, N_Q))
    dq = _rope(_unheads(dq_h), tables, True, BF16, "attn_rope_q_b")
    dk = _rope(_unheads(dk_h[:, ATTN_BLOCK:-ATTN_BLOCK]), tables, True, BF16, "attn_rope_k_b")
    dv = _unheads(dv_h[:, ATTN_BLOCK:-ATTN_BLOCK]).astype(BF16)
    d_lat = jnp.concatenate([dq, dk, dv], axis=2)
    d_ctx = jnp.concatenate([jnp.zeros((1, LC, N_Q * HEAD_DIM), BF16), _unheads(dkc_h).astype(BF16),
                             _unheads(dvc_h).astype(BF16)], axis=2)
    dqkv = jnp.concatenate([d_lat, d_ctx], axis=1)
    d_w_qkv = _mm(a, dqkv, "tn", out_dtype=BF16, name="attn_qkv_bw")
    da = _mm(dqkv, wts["w_qkv"], "nt", name="attn_qkv_bx")
    dh_res = jnp.concatenate([dh, jnp.zeros((1, LC, D), F32)], axis=1)
    dh_in, dng0, dsc1, dsh1 = _mixer_norm_bwd(i, h_all, mod, ng, da, dh_res)
    grads = dict(w_qkv=d_w_qkv, w_o=d_w_o, sink=dsink, ng0=dng0, ng1=dng1, sc1=dsc1, sh1=dsh1, g1=dg1)
    return dh_in, grads


def _gmlp_fwd(i, h, mod, ng, wts):
    a = _mixer_norm_fwd(i, h, mod, ng)
    p = _mm(a, wts["w_in"], "nn", out_dtype=BF16, name="gm_in")
    u, v = _rw_fwd(f_gmlp_pre, [(p, 0, 2)], [(wts["b_in"], "one", 2), (wts["ln_g"], "one"), (wts["ln_b"], "one")],
                   [(GM_W, F32), (GM_W, BF16)], name="gm_pre")
    us = _gm_spatial_fwd(u, v, wts["w_s"], wts["b_s"])
    y = _mm(us, wts["w_out"], "nn", name="gm_out")
    h2 = _rw_fwd(f_resgate, [(h, 0), (y, 0)], [(mod["g1"], "seg"), (ng[1], "one")], [(D, F32)],
                 name="gm_res")[0]
    return h2, (h, a, p, u, v, us, y)


def _gmlp_bwd(i, dh, res, mod, ng, wts):
    h, a, p, u, v, us, y = res
    dy, dg1, dng1 = _rw_bwd(f_gate_rms, [(y, 0)], [(mod["g1"], "seg"), (ng[1], "one")], [dh],
                            name="gm_res_b", row_grad=[(0, BF16)], param_grad=[0, 1])
    dus = _mm(dy, wts["w_out"], "nt", name="gm_out_bx")
    d_w_out = _mm(us, dy, "tn", out_dtype=BF16, name="gm_out_bw")
    du, dv, dws, dbs = _gm_spatial_bwd(u, v, wts["w_s"], wts["b_s"], dus)
    dp, db_in, dln_g, dln_b = _rw_bwd(
        f_gmlp_pre, [(p, 0, 2)], [(wts["b_in"], "one", 2), (wts["ln_g"], "one"), (wts["ln_b"], "one")], [du, dv],
        name="gm_pre_b", row_grad=[(0, BF16)], param_grad=[0, 1, 2])
    d_w_in = _mm(a, dp, "tn", out_dtype=BF16, name="gm_in_bw")
    da = _mm(dp, wts["w_in"], "nt", name="gm_in_bx")
    dh_in, dng0, dsc1, dsh1 = _mixer_norm_bwd(i, h, mod, ng, da, dh)
    grads = dict(w_in=d_w_in, w_out=d_w_out, b_in=db_in, ln_g=dln_g, ln_b=dln_b, w_s=dws, b_s=dbs,
                 ng0=dng0, ng1=dng1, sc1=dsc1, sh1=dsh1, g1=dg1)
    return dh_in, grads


MOD_NAMES = ("sh1", "sc1", "g1", "sh2", "sc2", "g2")
SMALL = (
    ("norm_g", (4, 4, 128)), ("ffn_conv_w", (4, 3, 704)), ("cm_b_in", (2, 256)), ("cm_dw_w", (2, 31, 128)),
    ("cm_dw_b", (2, 128)), ("cm_ln_g", (2, 128)), ("cm_ln_b", (2, 128)), ("cm_b_out", (2, 128)),
    ("gm_b_in", (1, 512)), ("gm_ln_g", (1, 256)), ("gm_ln_b", (1, 256)))


def _mixer_weights(i, P):
    if i % 3 == 0:
        j = i // 3
        return dict(w_in=P["cm_w_in"][j], w_out=P["cm_w_out"][j], b_in=P["cm_b_in"][j].reshape(1, 1, 2 * D),
                    dw_w=P["cm_dw_w"][j][None], dw_b=P["cm_dw_b"][j].reshape(1, 1, D),
                    ln_g=P["cm_ln_g"][j].reshape(1, 1, D), ln_b=P["cm_ln_b"][j].reshape(1, 1, D),
                    b_out=P["cm_b_out"][j].reshape(1, 1, D))
    if i % 3 == 1:
        return dict(w_qkv=P["attn_w_qkv"], w_o=P["attn_w_o"], sink=P["attn_sink"].reshape(N_Q))
    return dict(w_in=P["gm_w_in"], w_out=P["gm_w_out"], b_in=P["gm_b_in"].reshape(1, 1, 2 * GM_W),
                ln_g=P["gm_ln_g"].reshape(1, 1, GM_W), ln_b=P["gm_ln_b"].reshape(1, 1, GM_W),
                w_s=P["gm_w_s"].reshape(GM_GROUPS, GM_CHUNK, GM_CHUNK).astype(BF16),
                b_s=P["gm_b_s"].reshape(GM_GROUPS, GM_CHUNK, 1))


def _ffn_weights(i, P):
    return dict(up=P["ffn_w_up"][i].reshape(2, FFN_PAIRS, D, FFN_BLK), down=P["ffn_w_down"][i],
                cw=P["ffn_conv_w"][i].reshape(2, FFN_PAIRS, FFN_K, FFN_BLK),
                cb=P["ffn_conv_b"][i].reshape(2, FFN_PAIRS, 1, FFN_BLK))


def _local_step(x, ctx, target, lat_mod, ctx_mod, norm_g, layer_weights, grads_ready):
    tables = _rope_tables()
    ng = [[norm_g[i, j].reshape(1, 1, D) for j in range(4)] for i in range(DEPTH)]

    def mods(i, with_ctx, token):
        out = {}
        for j, nme in enumerate(MOD_NAMES):
            rows = [lat_mod[i, j]] + ([ctx_mod[i, j]] if with_ctx else [])
            out[nme] = jnp.stack(rows).reshape(len(rows), 1, D) + token[0, 0]
        return out

    def after(mod, token):
        return mod if token is None else {k_: v_ + token[0, 0] for k_, v_ in mod.items()}

    h_all = jnp.concatenate([x, ctx], axis=1)
    wm0, wf0, tok = layer_weights(0, h_all)
    m0 = mods(0, True, tok)
    h, r0m = _conformer_fwd(0, h_all, m0, ng[0], wm0)
    h, r0f = _ffn_fwd(0, h, m0, ng[0], wf0)
    wm1, wf1, tok = layer_weights(1, h)
    m1 = mods(1, True, tok)
    m1l = {k_: v_[:1] for k_, v_ in m1.items()}
    h, r1m = _attention_fwd(1, h, m1, ng[1], wm1, tables)
    h, r1f = _ffn_fwd(1, h, m1l, ng[1], wf1)
    wm2, wf2, tok = layer_weights(2, h)
    m2 = mods(2, False, tok)
    h, r2m = _gmlp_fwd(2, h, m2, ng[2], wm2)
    h, r2f = _ffn_fwd(2, h, m2, ng[2], wf2)
    wm3, wf3, tok = layer_weights(3, h)
    m3 = mods(3, False, tok)
    h, r3m = _conformer_fwd(3, h, m3, ng[3], wm3)
    h, r3f = _ffn_fwd(3, h, m3, ng[3], wf3)
    dh, loss = _loss_head(h, target)

    G = {}
    dh, G["f3"] = _ffn_bwd(3, dh, r3f, m3, ng[3], wf3)
    tok = grads_ready("f3", G["f3"])
    dh, G["m3"] = _conformer_bwd(3, dh, r3m, after(m3, tok), ng[3], wm3)
    tok = grads_ready("m3", G["m3"])
    dh, G["f2"] = _ffn_bwd(2, dh, r2f, after(m2, tok), ng[2], wf2)
    tok = grads_ready("f2", G["f2"])
    dh, G["m2"] = _gmlp_bwd(2, dh, r2m, after(m2, tok), ng[2], wm2)
    tok = grads_ready("m2", G["m2"])
    dh, G["f1"] = _ffn_bwd(1, dh, r1f, after(m1l, tok), ng[1], wf1)
    tok = grads_ready("f1", G["f1"])
    dh, G["m1"] = _attention_bwd(1, dh, r1m, after(m1, tok), ng[1], wm1, tables)
    tok = grads_ready("m1", G["m1"])
    dh, G["f0"] = _ffn_bwd(0, dh, r0f, after(m0, tok), ng[0], wf0)
    tok = grads_ready("f0", G["f0"])
    dh, G["m0"] = _conformer_bwd(0, dh, r0m, after(m0, tok), ng[0], wm0)
    grads_ready("m0", G["m0"])
    grad_x = dh[:, :L]

    zero = jnp.zeros((D,), F32)
    dmod = []
    for seg in range(2):
        per_layer = []
        for i in range(DEPTH):
            vals = []
            for nme in MOD_NAMES:
                src = G[("m" if nme.endswith("1") else "f") + str(i)][nme]
                vals.append(src[seg, 0] if src.shape[0] > seg else zero)
            per_layer.append(jnp.concatenate(vals))
        dmod.append(jnp.stack(per_layer))
    dmod = jnp.stack(dmod)
    return loss, grad_x, G, dmod


def kernel(x, c, ctx, c_ctx, ada_w, ada_b, norm_g, ffn_w_up, ffn_conv_w, ffn_conv_b, ffn_w_down, cm_w_in, cm_b_in, cm_dw_w, cm_dw_b, cm_ln_g, cm_ln_b, cm_w_out, cm_b_out, attn_w_qkv, attn_sink, attn_w_o, gm_w_in, gm_b_in, gm_ln_g, gm_ln_b, gm_w_s, gm_b_s, gm_w_out, loss_target, m_c_ctx, m_ada_w, m_ada_b, m_norm_g, m_ffn_w_up, m_ffn_conv_w, m_ffn_conv_b, m_ffn_w_down, m_cm_w_in, m_cm_b_in, m_cm_dw_w, m_cm_dw_b, m_cm_ln_g, m_cm_ln_b, m_cm_w_out, m_cm_b_out, m_attn_w_qkv, m_attn_sink, m_attn_w_o, m_gm_w_in, m_gm_b_in, m_gm_ln_g, m_gm_ln_b, m_gm_w_s, m_gm_b_s, m_gm_w_out, v_c_ctx, v_ada_w, v_ada_b, v_norm_g, v_ffn_w_up, v_ffn_conv_w, v_ffn_conv_b, v_ffn_w_down, v_cm_w_in, v_cm_b_in, v_cm_dw_w, v_cm_dw_b, v_cm_ln_g, v_cm_ln_b, v_cm_w_out, v_cm_b_out, v_attn_w_qkv, v_attn_sink, v_attn_w_o, v_gm_w_in, v_gm_b_in, v_gm_ln_g, v_gm_ln_b, v_gm_w_s, v_gm_b_s, v_gm_w_out):
    W = dict(c_ctx=c_ctx, ada_w=ada_w, ada_b=ada_b, norm_g=norm_g, ffn_w_up=ffn_w_up, ffn_conv_w=ffn_conv_w, ffn_conv_b=ffn_conv_b, ffn_w_down=ffn_w_down, cm_w_in=cm_w_in, cm_b_in=cm_b_in, cm_dw_w=cm_dw_w, cm_dw_b=cm_dw_b, cm_ln_g=cm_ln_g, cm_ln_b=cm_ln_b, cm_w_out=cm_w_out, cm_b_out=cm_b_out, attn_w_qkv=attn_w_qkv, attn_sink=attn_sink, attn_w_o=attn_w_o, gm_w_in=gm_w_in, gm_b_in=gm_b_in, gm_ln_g=gm_ln_g, gm_ln_b=gm_ln_b, gm_w_s=gm_w_s, gm_b_s=gm_b_s, gm_w_out=gm_w_out)
    M = dict(c_ctx=m_c_ctx, ada_w=m_ada_w, ada_b=m_ada_b, norm_g=m_norm_g, ffn_w_up=m_ffn_w_up, ffn_conv_w=m_ffn_conv_w, ffn_conv_b=m_ffn_conv_b, ffn_w_down=m_ffn_w_down, cm_w_in=m_cm_w_in, cm_b_in=m_cm_b_in, cm_dw_w=m_cm_dw_w, cm_dw_b=m_cm_dw_b, cm_ln_g=m_cm_ln_g, cm_ln_b=m_cm_ln_b, cm_w_out=m_cm_w_out, cm_b_out=m_cm_b_out, attn_w_qkv=m_attn_w_qkv, attn_sink=m_attn_sink, attn_w_o=m_attn_w_o, gm_w_in=m_gm_w_in, gm_b_in=m_gm_b_in, gm_ln_g=m_gm_ln_g, gm_ln_b=m_gm_ln_b, gm_w_s=m_gm_w_s, gm_b_s=m_gm_b_s, gm_w_out=m_gm_w_out)
    V = dict(c_ctx=v_c_ctx, ada_w=v_ada_w, ada_b=v_ada_b, norm_g=v_norm_g, ffn_w_up=v_ffn_w_up, ffn_conv_w=v_ffn_conv_w, ffn_conv_b=v_ffn_conv_b, ffn_w_down=v_ffn_w_down, cm_w_in=v_cm_w_in, cm_b_in=v_cm_b_in, cm_dw_w=v_cm_dw_w, cm_dw_b=v_cm_dw_b, cm_ln_g=v_cm_ln_g, cm_ln_b=v_cm_ln_b, cm_w_out=v_cm_w_out, cm_b_out=v_cm_b_out, attn_w_qkv=v_attn_w_qkv, attn_sink=v_attn_sink, attn_w_o=v_attn_w_o, gm_w_in=v_gm_w_in, gm_b_in=v_gm_b_in, gm_ln_g=v_gm_ln_g, gm_ln_b=v_gm_ln_b, gm_w_s=v_gm_w_s, gm_b_s=v_gm_b_s, gm_w_out=v_gm_w_out)
    me = 4 * lax.axis_index("x") + 2 * lax.axis_index("y") + lax.axis_index("c")
    small_shapes = [s for _, s in SMALL]

    small = _pack_rows([W[n] for n, _ in SMALL] + [c])
    layer_mats = [("cm_w_in", 0, "cm_w_out", 0), ("attn_w_qkv", 0, "attn_w_o", 0), ("gm_w_in", 0, "gm_w_out", 0),
                  ("cm_w_in", 1, "cm_w_out", 1)]
    local_bf16 = [[W[a][ja].astype(BF16), W[b][jb].astype(BF16), ffn_w_up[i].astype(BF16), ffn_w_down[i].astype(BF16)]
                  for i, (a, ja, b, jb) in enumerate(layer_mats)]
    gathered0 = _all_gather([small] + local_bf16[0], "gather_params0")
    small_g = gathered0[0]
    col_to_full = lambda g: g.transpose(1, 0, 2).reshape(g.shape[1], NDEV * g.shape[2])
    P = {}
    unpacked = jax.vmap(lambda r: tuple(_unpack_rows(r, small_shapes + [(D,)])))(small_g)
    for (n, _), g in zip(SMALL, unpacked[:-1]):
        if n == "ffn_conv_w":
            P[n] = [g[:, i] for i in range(DEPTH)]
        else:
            P[n] = _unshard_last(g)
    c_all = unpacked[-1]
    P["ffn_conv_b"] = [ffn_conv_b[i].reshape(NDEV, 1, FFN_BLK) for i in range(DEPTH)]
    P["attn_sink"], P["gm_w_s"], P["gm_b_s"] = attn_sink, gm_w_s, gm_b_s

    cond = jnp.concatenate([c_all, c_ctx[None], jnp.zeros((7, D), F32)])[None]
    scond = _rw_fwd(f_silu, [(cond, 0)], [], [(D, BF16)], name="ada_silu")[0]
    ada_bf = ada_w.astype(BF16)
    ncol = ada_w.shape[2]
    mod_loc = _mm(scond, ada_bf, "nn", name="ada_proj")
    mod_loc = mod_loc + lax.dynamic_slice_in_dim(ada_b, me * ncol, ncol, axis=1)[:, None, :]
    mod_g = _all_gather([mod_loc], "gather_mod")[0]
    mod_full = mod_g.transpose(1, 2, 0, 3).reshape(DEPTH, 16, 6, D)
    lat_mod = lax.dynamic_index_in_dim(mod_full, me, axis=1, keepdims=False)
    ctx_mod = mod_full[:, NDEV]

    gathers, exchanges, pending = {}, {}, {}
    col_to_parts = lambda g: g[0].reshape(g.shape[1], NDEV, g.shape[2] // NDEV).transpose(1, 0, 2)
    row_to_parts = lambda g: g.reshape(NDEV, -1, g.shape[-1])
    no_order = jnp.zeros((8, 128), F32)

    def layer_weights(i, h):
        if i == 0:
            mats = gathered0[1:]
            gathers[1] = _xfer_start("gather", local_bf16[1], mod_g, "gather_params1")
        else:
            mats = _xfer_wait(gathers[i], h)
            if i + 1 < DEPTH:
                gathers[i + 1] = _xfer_start("gather", local_bf16[i + 1], mats[0], f"gather_params{i + 1}")
        token = gathers[i + 1]["token"] if i + 1 < DEPTH else no_order
        a, ja, b, jb = layer_mats[i]
        pi = dict(P)
        pi[a] = col_to_full(mats[0]) if a.startswith(("attn", "gm")) else {ja: col_to_full(mats[0])}
        pi[b] = mats[1].reshape(-1, D) if b.startswith(("attn", "gm")) else {jb: mats[1].reshape(-1, D)}
        pi["ffn_w_up"], pi["ffn_w_down"] = {i: mats[2]}, {i: mats[3].reshape(FFN_PAIRS, FFN_BLK, D)}
        return _mixer_weights(i, pi), _ffn_weights(i, pi), token

    def grads_ready(tag, g):
        pending[tag] = g
        i = int(tag[1])
        col_name, row_name = {0: ("w_in", "w_out"), 1: ("w_qkv", "w_o"), 2: ("w_in", "w_out")}[i % 3]
        if tag == "f0":
            arrs = [g["up"], row_to_parts(g["down"])]
        elif tag == "m0":
            arrs = [col_to_parts(g[col_name]), row_to_parts(g[row_name])]
        elif tag[0] == "m":
            gf = pending[f"f{i}"]
            arrs = [col_to_parts(g[col_name]), row_to_parts(g[row_name]), gf["up"], row_to_parts(gf["down"])]
        else:
            return None
        exchanges[tag] = _xfer_start("scatter", arrs, no_order, "exchange_" + tag)
        if tag == "m2":
            gathers["w_s"] = _xfer_start("gather", [g["w_s"].reshape(GM_GROUPS * GM_CHUNK, GM_CHUNK)],
                                         exchanges[tag]["token"], "gather_gm_w_s")
            return gathers["w_s"]["token"]
        return exchanges[tag]["token"]

    loss_part, grad_x, G, dmod = _local_step(x, ctx, loss_target, lat_mod, ctx_mod, P["norm_g"], layer_weights,
                                             grads_ready)
    recv = {tag: _xfer_wait(exchanges[tag], grad_x) for tag in ("m3", "m2", "m1")}

    dmod_g = _all_gather([dmod], "gather_dmod")[0]
    dm_cols = lax.dynamic_slice_in_dim(dmod_g, me * ncol, ncol, axis=3)
    dm_ext = dm_cols.transpose(2, 1, 0, 3).reshape(DEPTH, 16, ncol)
    cond_ext = jnp.concatenate([c_all, jnp.broadcast_to(c_ctx[None], (NDEV, D))])[None]
    scond_ext = _rw_fwd(f_silu, [(cond_ext, 0)], [], [(D, BF16)], name="ada_silu_ext")[0]
    g_ada_w = _mm(scond_ext, dm_ext, "tn", name="ada_proj_bw")
    dsil = _mm(dm_ext, ada_bf, "nt", reduce_blocks=True, name="ada_proj_bx")
    dcc = _rw_bwd(f_silu_rows, [(jnp.zeros((1, NDEV, D), F32), 0)], [(c_ctx.reshape(1, 1, D), "one")],
                  [dsil[:, NDEV:]], name="ada_silu_b", param_grad=[0])[0]

    out = {}

    def put(name, res):
        out[name] = res

    d_norm_g = jnp.stack([jnp.stack([G[f"m{i}"]["ng0"], G[f"m{i}"]["ng1"], G[f"f{i}"]["ng2"], G[f"f{i}"]["ng3"]])
                          for i in range(DEPTH)]).reshape(DEPTH, 4, D)
    small_full = dict(
        norm_g=d_norm_g,
        cm_b_in=jnp.stack([G["m0"]["b_in"], G["m3"]["b_in"]]).reshape(2, 2 * D),
        cm_dw_w=jnp.stack([G["m0"]["dw_w"][0], G["m3"]["dw_w"][0]]),
        cm_dw_b=jnp.stack([G["m0"]["dw_b"], G["m3"]["dw_b"]]).reshape(2, D),
        cm_ln_g=jnp.stack([G["m0"]["ln_g"], G["m3"]["ln_g"]]).reshape(2, D),
        cm_ln_b=jnp.stack([G["m0"]["ln_b"], G["m3"]["ln_b"]]).reshape(2, D),
        cm_b_out=jnp.stack([G["m0"]["b_out"], G["m3"]["b_out"]]).reshape(2, D),
        gm_b_in=G["m2"]["b_in"].reshape(1, 2 * GM_W),
        gm_ln_g=G["m2"]["ln_g"].reshape(1, GM_W), gm_ln_b=G["m2"]["ln_b"].reshape(1, GM_W))
    by_dest = []
    for n, _ in SMALL:
        if n == "ffn_conv_w":
            by_dest.append(jnp.stack([G[f"f{i}"]["cw"] for i in range(DEPTH)], axis=1))
        else:
            by_dest.append(_shard_last(small_full[n]))
    small_send = jax.vmap(lambda *vs: _pack_rows(list(vs)))(*by_dest)
    small_recv = _all_to_all([[small_send]], "exchange_small")[0]

    def shard3(a):
        return a.reshape(a.shape[0], -1, a.shape[-1])

    small_local = lambda d_: _pack_rows([d_[n] for n, _ in SMALL])[None]
    res = _adamw(small_recv, small_local(W), small_local(M), small_local(V), "adamw_small")
    unp = [_unpack_rows(r[0], small_shapes) for r in res]
    for q, (n, _) in enumerate(SMALL):
        put(n, [unp[t][q] for t in range(4)])

    repl_names = ["c_ctx", "ffn_conv_b", "attn_sink", "gm_b_s"]
    repl_part = dict(
        c_ctx=dcc.reshape(D),
        ffn_conv_b=jnp.stack([G[f"f{i}"]["cb"].reshape(2 * 2816) for i in range(DEPTH)]),
        attn_sink=G["m1"]["sink"].reshape(1, N_Q),
        gm_b_s=G["m2"]["b_s"].reshape(1, GM_GROUPS, GM_CHUNK))
    w_s_parts = _xfer_wait(gathers["w_s"], grad_x)[0]
    flat_s = lambda a: a.reshape(1, GM_GROUPS * GM_CHUNK, GM_CHUNK)
    res = _adamw(w_s_parts[None], flat_s(gm_w_s), flat_s(m_gm_w_s), flat_s(v_gm_w_s), "adamw_gm_w_s")
    put("gm_w_s", [r.reshape(gm_w_s.shape) for r in res])
    repl_shapes = [W[n].shape for n in repl_names]
    repl_sent = _pack_rows([repl_part[n] for n in repl_names] + [loss_part.reshape(1)], mult=256)
    repl_g = _all_gather([repl_sent], "gather_repl")[0]
    loss = jnp.sum(repl_g.reshape(NDEV, -1)[:, sum(W[n].size for n in repl_names)])
    repl_local = lambda d_: _pack_rows([d_[n] for n in repl_names], mult=256)[None]
    res = _adamw(repl_g[None], repl_local(W), repl_local(M), repl_local(V), "adamw_repl")
    unp = [_unpack_rows(r[0], repl_shapes) for r in res]
    for q, n in enumerate(repl_names):
        put(n, [unp[t][q] for t in range(4)])

    def update_big(n, parts):
        turn = (lambda a: jnp.swapaxes(a, 1, 2)) if n == "ffn_w_up" else shard3
        res = _adamw(parts, turn(W[n]), turn(M[n]), turn(V[n]), "adamw_" + n)
        put(n, [(turn(r) if n == "ffn_w_up" else r).reshape(W[n].shape) for r in res])

    put("ada_w", _adamw(g_ada_w[:, None], ada_w, m_ada_w, v_ada_w, "adamw_ada_w"))
    ada_b_parts = dmod_g.reshape(1, 2 * NDEV, DEPTH, 6 * D)
    res = _adamw(ada_b_parts, ada_b[None], m_ada_b[None], v_ada_b[None], "adamw_ada_b")
    put("ada_b", [r[0] for r in res])
    early = dict(attn_w_qkv=[recv["m1"][0]], attn_w_o=[recv["m1"][1]], gm_w_in=[recv["m2"][0]],
                 gm_w_out=[recv["m2"][1]])
    for n, parts in early.items():
        update_big(n, parts)
    done_first = sum(out[n][1].reshape(-1)[:1024] for n in ["ada_w", "ada_b", "gm_w_in", "gm_w_out", "attn_w_qkv"])
    recv.update({tag: _xfer_wait(exchanges[tag], done_first) for tag in ("f0", "m0")})
    late = dict(
        ffn_w_up=[recv["f0"][0]] + [recv[f"m{i}"][2] for i in (1, 2, 3)],
        ffn_w_down=[recv["f0"][1]] + [recv[f"m{i}"][3] for i in (1, 2, 3)],
        cm_w_in=[recv["m0"][0], recv["m3"][0]], cm_w_out=[recv["m0"][1], recv["m3"][1]])
    for n, parts in late.items():
        update_big(n, parts)

    names = ["c_ctx", "ada_w", "ada_b", "norm_g", "ffn_w_up", "ffn_conv_w", "ffn_conv_b", "ffn_w_down", "cm_w_in",
             "cm_b_in", "cm_dw_w", "cm_dw_b", "cm_ln_g", "cm_ln_b", "cm_w_out", "cm_b_out", "attn_w_qkv",
             "attn_sink", "attn_w_o", "gm_w_in", "gm_b_in", "gm_ln_g", "gm_ln_b", "gm_w_s", "gm_b_s", "gm_w_out"]
    return (loss, grad_x, *[out[n][0] for n in names], *[out[n][1] for n in names],
            *[out[n][2] for n in names], *[out[n][3] for n in names])
```

```python
import functools

import numpy as np
import jax
import jax.numpy as jnp
from jax import lax
from jax.experimental import pallas as pl
from jax.experimental.pallas import tpu as pltpu

F32, BF16 = jnp.float32, jnp.bfloat16
MESH = pl.DeviceIdType.MESH
AXES = ("x", "y", "c")
NDEV = 8

D = 1024
L = 2048
LC = 256
TA = L + LC
DEPTH = 4
EPS = 1e-6
HEAD_DIM = 64
N_Q, N_KV, Q_PER_KV = 16, 4, 4
ATTN_BLOCK = 128
GRID_W = 64
ROPE_BASE = 10000.0
GM_W = 2048
GM_CHUNK = 128
GM_GROUPS = 16
FFN_BLK = 704
CM_K, FFN_K = 31, 3

ADAM_LR, ADAM_B1, ADAM_B2, ADAM_EPS, ADAM_WD, ADAM_STEP = 0.001, 0.9, 0.999, 1e-08, 0.01, 10

VMEM_LIMIT_V7X = 58 * 1024 * 1024
ROW_TILE_ELEMS = 256 * 1024
MM_TILE_BYTES = 4 * 1024 * 1024


def _cparams(sem=None):
    kw = dict(vmem_limit_bytes=VMEM_LIMIT_V7X)
    if sem is not None:
        kw["dimension_semantics"] = sem
    return pltpu.CompilerParams(**kw)


def _pick(n, cands):
    for c in cands:
        if n % c == 0:
            return c
    return n


def _as3(a):
    return a if a.ndim == 3 else a[None]


def _all_gather(arrs, name):
    n = len(arrs)

    def body(*refs):
        ins, outs = refs[:n], refs[n:2 * n]
        send_sems, recv_sems, local_sems = refs[2 * n:]
        x, y, c = lax.axis_index("x"), lax.axis_index("y"), lax.axis_index("c")
        me, sibling = (x, y, c), (x, y, 1 - c)
        chips = [(1 - x, y), (x, 1 - y), (1 - x, 1 - y)]

        def slot(a, p):
            return outs[a].at[4 * p[0] + 2 * p[1] + p[2]]

        def copy(a, k, block, to, src=None):
            return pltpu.make_async_remote_copy(
                src_ref=slot(a, block) if src is None else src, dst_ref=slot(a, block),
                send_sem=send_sems.at[a, k], recv_sem=recv_sems.at[a, k],
                device_id=to, device_id_type=MESH)

        mine = [pltpu.make_async_copy(ins[a], slot(a, me), local_sems.at[a]) for a in range(n)]
        for m in mine:
            m.start()
        first = []
        for a in range(n):
            first.append(copy(a, 0, me, sibling, src=ins[a]))
            first += [copy(a, 1 + j, me, (*chip, c), src=ins[a]) for j, chip in enumerate(chips)]
        for cp in first:
            cp.start()
        passed = []
        for j, chip in enumerate(chips):
            for a in range(n):
                copy(a, 1 + j, (*chip, c), me).wait_recv()
                p = copy(a, 4 + j, (*chip, c), sibling)
                p.start()
                passed.append(p)
        for a in range(n):
            copy(a, 0, sibling, me).wait_recv()
            for j, chip in enumerate(chips):
                copy(a, 4 + j, (*chip, 1 - c), me).wait_recv()
        for cp in first + passed:
            cp.wait_send()
        for m in mine:
            m.wait()

    any_spec = pl.BlockSpec(memory_space=pl.ANY)
    outs = pl.pallas_call(
        body, name=name,
        out_shape=[jax.ShapeDtypeStruct((NDEV,) + a.shape, a.dtype) for a in arrs],
        in_specs=[any_spec] * n, out_specs=[any_spec] * n,
        scratch_shapes=[pltpu.SemaphoreType.DMA((n, 7)), pltpu.SemaphoreType.DMA((n, 7)),
                        pltpu.SemaphoreType.DMA((n,))],
    )(*arrs)
    return list(outs)


def _all_to_all(groups, name):
    flat = [(gi, li, a) for gi, g in enumerate(groups) for li, a in enumerate(g)]
    n, ng = len(flat), len(groups)

    def body(*refs):
        ins, outs = refs[:n], refs[n:n + ng]
        send_sems, recv_sems, local_sems = refs[n + ng:]
        x, y, c = lax.axis_index("x"), lax.axis_index("y"), lax.axis_index("c")
        me = 4 * x + 2 * y + c
        copies = []
        for a, (gi, li, _) in enumerate(flat):
            loc = pltpu.make_async_copy(ins[a].at[me], outs[gi].at[li, me], local_sems.at[a])
            loc.start()
            copies.append(loc)
            for k in range(1, NDEV):
                px = 1 - x if (k >> 2) & 1 else x
                py = 1 - y if (k >> 1) & 1 else y
                pc = 1 - c if k & 1 else c
                cp = pltpu.make_async_remote_copy(
                    src_ref=ins[a].at[4 * px + 2 * py + pc], dst_ref=outs[gi].at[li, me],
                    send_sem=send_sems.at[a, k - 1], recv_sem=recv_sems.at[a, k - 1],
                    device_id=(px, py, pc), device_id_type=MESH)
                cp.start()
                copies.append(cp)
        for cp in copies:
            cp.wait()

    any_spec = pl.BlockSpec(memory_space=pl.ANY)
    outs = pl.pallas_call(
        body, name=name,
        out_shape=[jax.ShapeDtypeStruct((len(g),) + g[0].shape, g[0].dtype) for g in groups],
        in_specs=[any_spec] * n, out_specs=[any_spec] * ng,
        scratch_shapes=[pltpu.SemaphoreType.DMA((n, 7)), pltpu.SemaphoreType.DMA((n, 7)),
                        pltpu.SemaphoreType.DMA((n,))],
    )(*[a for _, _, a in flat])
    return list(outs)


HBM_SPEC = pl.BlockSpec(memory_space=pltpu.HBM)
SEM_SPEC = pl.BlockSpec(memory_space=pltpu.SEMAPHORE)
ANY_SPEC = pl.BlockSpec(memory_space=pl.ANY)
SPLIT_EFFECT = pltpu.SideEffectType.DATAFLOW_SIDE_EFFECTING


def _remote_copies(kind, ins, lands, send_sems, recv_sems):
    x, y, c = lax.axis_index("x"), lax.axis_index("y"), lax.axis_index("c")
    me = 4 * x + 2 * y + c
    out = []
    for a in range(len(ins)):
        for k in range(1, NDEV):
            px = 1 - x if (k >> 2) & 1 else x
            py = 1 - y if (k >> 1) & 1 else y
            pc = 1 - c if k & 1 else c
            src = ins[a] if kind == "gather" else ins[a].at[4 * px + 2 * py + pc]
            out.append(pltpu.make_async_remote_copy(
                src_ref=src, dst_ref=lands[a].at[me], send_sem=send_sems.at[a * (NDEV - 1) + k - 1],
                recv_sem=recv_sems.at[a * (NDEV - 1) + k - 1], device_id=(px, py, pc), device_id_type=MESH))
    return out


def _local_copies(kind, ins, lands, local_sems):
    me = 4 * lax.axis_index("x") + 2 * lax.axis_index("y") + lax.axis_index("c")
    return [pltpu.make_async_copy(ins[a] if kind == "gather" else ins[a].at[me], lands[a].at[me], local_sems.at[a])
            for a in range(len(ins))]


def _xfer_start(kind, arrs, after, name):
    n = len(arrs)
    lands = [lax.empty((NDEV,) + a.shape if kind == "gather" else a.shape, a.dtype) for a in arrs]

    def body(*refs):
        ins, lnd = refs[:n], refs[n:2 * n]
        send_sems, recv_sems, local_sems = refs[2 * n + 1:2 * n + 4]
        for cp in _remote_copies(kind, ins, lnd, send_sems, recv_sems) + _local_copies(kind, ins, lnd, local_sems):
            cp.start()
        refs[-1][...] = jnp.zeros_like(refs[-1])

    outs = pl.pallas_call(
        body, name=name,
        out_shape=(pltpu.SemaphoreType.DMA((n * (NDEV - 1),)), pltpu.SemaphoreType.DMA((n * (NDEV - 1),)),
                   pltpu.SemaphoreType.DMA((n,)),
                   *[pltpu.HBM(a.shape, a.dtype) for a in arrs + lands], jax.ShapeDtypeStruct((8, 128), F32)),
        in_specs=[HBM_SPEC] * (2 * n) + [ANY_SPEC],
        out_specs=(SEM_SPEC, SEM_SPEC, SEM_SPEC, *[HBM_SPEC] * (2 * n), pl.BlockSpec(memory_space=pltpu.VMEM)),
        input_output_aliases={a: 3 + a for a in range(2 * n)},
        compiler_params=pltpu.CompilerParams(has_side_effects=SPLIT_EFFECT),
    )(*[pltpu.with_memory_space_constraint(a, pltpu.HBM) for a in arrs + lands], after)
    return dict(kind=kind, n=n, sems=list(outs[:3]), bufs=list(outs[3:3 + 2 * n]), token=outs[-1], name=name)


def _xfer_wait(st, after):
    kind, n = st["kind"], st["n"]

    def body(*refs):
        ins, lnd = refs[:n], refs[n:2 * n]
        send_sems, recv_sems, local_sems = refs[2 * n:2 * n + 3]
        for cp in _remote_copies(kind, ins, lnd, send_sems, recv_sems):
            cp.wait_send()
            cp.wait_recv()
        for cp in _local_copies(kind, ins, lnd, local_sems):
            cp.wait()

    outs = pl.pallas_call(
        body, name=st["name"] + "_wait",
        out_shape=tuple(pltpu.HBM(b.shape, b.dtype) for b in st["bufs"]),
        in_specs=[HBM_SPEC] * (2 * n) + [SEM_SPEC] * 3 + [ANY_SPEC],
        out_specs=tuple([HBM_SPEC] * (2 * n)),
        input_output_aliases={a: a for a in range(2 * n)},
        compiler_params=pltpu.CompilerParams(has_side_effects=SPLIT_EFFECT),
    )(*st["bufs"], *st["sems"], after)
    return list(outs[n:])


def _mm(a, b, kind, *, name, out_dtype=F32, reduce_blocks=False):
    a, b = _as3(a), _as3(b)
    nba, nbb = a.shape[0], b.shape[0]
    nb = max(nba, nbb)
    assert nba in (1, nb) and nbb in (1, nb)
    if kind == "tn":
        t, m = a.shape[1:]
        n = b.shape[2]
        assert b.shape[1] == t and not reduce_blocks
        tm = m if m <= 1024 else _pick(m, (1024,))
        tn = n if n <= 1024 else _pick(n, (1024, 768, 512))
        tk = t if t * (tm + tn) * 2 <= MM_TILE_BYTES * 3 else _pick(t, (512, 768, 256))
        nred = t // tk
    else:
        m, k = a.shape[1:]
        n = b.shape[2] if kind == "nn" else b.shape[1]
        assert (b.shape[1] if kind == "nn" else b.shape[2]) == k
        tn = n if (n <= 1024 or k * n * 2 <= 2 * MM_TILE_BYTES) else _pick(n, (1024, 768, 512))
        tm = m
        for cand in (1024, 768, 512, 256):
            if m % cand == 0 and cand * tn * 4 <= MM_TILE_BYTES and cand * k * 2 <= MM_TILE_BYTES:
                tm = cand
                break
        nred = nb if reduce_blocks else 1
    nbo = 1 if reduce_blocks else nb

    def blk(nbx, g, r):
        if nbx == 1:
            return 0
        return r if reduce_blocks else g

    if kind == "nn":
        a_spec = pl.BlockSpec((1, tm, k), lambda g, j, i, r: (blk(nba, g, r), i, 0))
        b_spec = pl.BlockSpec((1, k, tn), lambda g, j, i, r: (blk(nbb, g, r), 0, j))
        dims = (((1,), (0,)), ((), ()))
    elif kind == "nt":
        a_spec = pl.BlockSpec((1, tm, k), lambda g, j, i, r: (blk(nba, g, r), i, 0))
        b_spec = pl.BlockSpec((1, tn, k), lambda g, j, i, r: (blk(nbb, g, r), j, 0))
        dims = (((1,), (1,)), ((), ()))
    else:
        a_spec = pl.BlockSpec((1, tk, tm), lambda g, j, i, r: (blk(nba, g, r), r, i))
        b_spec = pl.BlockSpec((1, tk, tn), lambda g, j, i, r: (blk(nbb, g, r), r, j))
        dims = (((0,), (0,)), ((), ()))
    o_spec = pl.BlockSpec((1, tm, tn), lambda g, j, i, r: (g, i, j))

    def body(a_ref, b_ref, o_ref, *scratch):
        prod = lax.dot_general(a_ref[0].astype(BF16), b_ref[0].astype(BF16), dims,
                               preferred_element_type=F32)
        if nred == 1:
            o_ref[0] = prod.astype(o_ref.dtype)
        else:
            acc = scratch[0]
            r = pl.program_id(3)

            @pl.when(r == 0)
            def _():
                acc[...] = prod

            @pl.when(r > 0)
            def _():
                acc[...] += prod

            @pl.when(r == nred - 1)
            def _():
                o_ref[0] = acc[...].astype(o_ref.dtype)

    return pl.pallas_call(
        body, name=name,
        out_shape=jax.ShapeDtypeStruct((nbo, m, n), out_dtype),
        grid=(nbo, n // tn, m // tm, nred),
        in_specs=[a_spec, b_spec], out_specs=o_spec,
        scratch_shapes=[pltpu.VMEM((tm, tn), F32)] if nred > 1 else [],
        compiler_params=_cparams(("parallel", "parallel", "parallel", "arbitrary")),
    )(a, b)


def _row_tile(t, widths):
    tm = max(16, ROW_TILE_ELEMS // max(widths))
    tm = min(tm, 256)
    return t if t < tm else tm


def _sel_index(sel, g, i, tm):
    if sel == "one":
        return 0
    if sel == "seg":
        return (i * tm) // L
    return g + sel


def _row_spec(arr, off, tm):
    return pl.BlockSpec((1, tm, arr.shape[2]), lambda g, i: (g + off, i, 0))


def _par_spec(arr, sel, tm):
    return pl.BlockSpec((1, 1, arr.shape[2]), lambda g, i: (_sel_index(sel, g, i, tm), 0, 0))


def _norm_ops(ops):
    return [(o[0], o[1], o[2] if len(o) > 2 else 1) for o in ops]


def _split_cols(vals, nsplit):
    out = []
    for v, ns in zip(vals, nsplit):
        w = v.shape[1] // ns
        out += [v] if ns == 1 else [v[:, q * w:(q + 1) * w] for q in range(ns)]
    return out


def _join_cols(flat, nsplit):
    out, pos = [], 0
    for ns in nsplit:
        out.append(flat[pos] if ns == 1 else jnp.concatenate(flat[pos:pos + ns], axis=1))
        pos += ns
    return out


def _rw_fwd(fn, rows, params, outs, *, name, nblk=None):
    rows, params = _norm_ops(rows), _norm_ops(params)
    t = rows[0][0].shape[1]
    nblk = nblk or rows[0][0].shape[0]
    tm = _row_tile(t, [r.shape[2] for r, _, _ in rows] + [w for w, _ in outs])
    nr, npar = len(rows), len(params)
    nsplit = [ns for _, _, ns in rows + params]

    def body(*refs):
        vals = _split_cols([r[0].astype(F32) for r in refs[:nr + npar]], nsplit)
        res = fn(*vals)
        for o_ref, o in zip(refs[nr + npar:], res):
            o_ref[0] = o.astype(o_ref.dtype)

    res = pl.pallas_call(
        body, name=name,
        out_shape=[jax.ShapeDtypeStruct((nblk, t, w), dt) for w, dt in outs],
        grid=(nblk, t // tm),
        in_specs=[_row_spec(r, off, tm) for r, off, _ in rows] + [_par_spec(p, s, tm) for p, s, _ in params],
        out_specs=[pl.BlockSpec((1, tm, w), lambda g, i: (g, i, 0)) for w, _ in outs],
        compiler_params=_cparams(("parallel", "parallel")),
    )(*[r for r, _, _ in rows], *[p for p, _, _ in params])
    return list(res)


def _rw_bwd(fn, rows, params, cts, *, name, row_grad=(), param_grad=(), add=None, nblk=None):
    rows, params = _norm_ops(rows), _norm_ops(params)
    t = cts[0].shape[1]
    nblk = nblk or cts[0].shape[0]
    tm = _row_tile(t, [r.shape[2] for r, _, _ in rows] + [c.shape[2] for c in cts])
    ni = t // tm
    nr, npar, nct = len(rows), len(params), len(cts)
    nadd = 0 if add is None else 1
    n_in = nr + npar + nct + nadd
    nsplit = [ns for _, _, ns in rows + params]

    def body(*refs):
        prim = _split_cols([r[0].astype(F32) for r in refs[:nr + npar]], nsplit)
        ct = tuple(r[0].astype(F32) for r in refs[nr + npar:nr + npar + nct])
        _, vjp = jax.vjp(fn, *prim)
        grads = _join_cols(list(vjp(ct)), nsplit)
        out_refs = refs[n_in:]
        for q, (ri, _) in enumerate(row_grad):
            gr = grads[ri]
            if q == 0 and nadd:
                gr = gr + refs[n_in - 1][0].astype(F32)
            out_refs[q][0] = gr.astype(out_refs[q].dtype)
        g, i = pl.program_id(0), pl.program_id(1)
        step = g * ni + i
        pg, pi = (step - 1) // ni, (step - 1) % ni
        for q, pidx in enumerate(param_grad):
            o_ref = out_refs[len(row_grad) + q]
            sel = params[pidx][1]
            val = grads[nr + pidx]
            if sel == "one":
                first = step == 0
            else:
                first = (step == 0) | (_sel_index(sel, g, i, tm) != _sel_index(sel, pg, pi, tm))

            @pl.when(first)
            def _(o_ref=o_ref, val=val):
                o_ref[0] = val

            @pl.when(jnp.logical_not(first))
            def _(o_ref=o_ref, val=val):
                o_ref[0] += val

    in_arrays = [r for r, _, _ in rows] + [p for p, _, _ in params] + list(cts) + ([add] if nadd else [])
    in_specs = ([_row_spec(r, off, tm) for r, off, _ in rows] + [_par_spec(p, s, tm) for p, s, _ in params]
                + [_row_spec(c, 0, tm) for c in cts] + ([_row_spec(add, 0, tm)] if nadd else []))
    out_shape, out_specs = [], []
    for ri, dt in row_grad:
        w = rows[ri][0].shape[2]
        out_shape.append(jax.ShapeDtypeStruct((nblk, t, w), dt))
        out_specs.append(pl.BlockSpec((1, tm, w), lambda g, i: (g, i, 0)))
    for pidx in param_grad:
        p, sel, _ = params[pidx]
        out_shape.append(jax.ShapeDtypeStruct(p.shape, F32))
        out_specs.append(_par_spec(p, sel, tm))
    res = pl.pallas_call(
        body, name=name, out_shape=out_shape, grid=(nblk, ni),
        in_specs=in_specs, out_specs=out_specs,
        compiler_params=_cparams(("arbitrary", "arbitrary")),
    )(*in_arrays)
    return list(res)


def _sigmoid(x):
    return 1.0 / (1.0 + jnp.exp(-x))


def _rms(x, g):
    return x * lax.rsqrt(jnp.mean(x * x, axis=-1, keepdims=True) + EPS) * g


def _ln(x, g, b):
    mu = jnp.mean(x, axis=-1, keepdims=True)
    xc = x - mu
    var = jnp.mean(xc * xc, axis=-1, keepdims=True)
    return xc * lax.rsqrt(var + EPS) * g + b


def _gelu_tanh(x):
    return 0.5 * x * (1.0 + jnp.tanh(0.7978845608028654 * (x + 0.044715 * (x * x * x))))


def f_modnorm(h, g, sc, sh):
    return (_rms(h, g) * (1.0 + sc) + sh,)


def f_gate_rms(y, gate, g):
    return (gate * _rms(y, g),)


def f_gate_rms_bias(y, gate, g, b):
    return (gate * _rms(y + b, g),)


def f_resgate(h, y, gate, g):
    return (h + gate * _rms(y, g),)


def f_resgate_bias(h, y, gate, g, b):
    return (h + gate * _rms(y + b, g),)


def f_glu(pa, pg, ba, bg):
    return ((pa + ba) * _sigmoid(pg + bg),)


def f_lnsilu(z, g, b):
    t = _ln(z, g, b)
    return (t * _sigmoid(t),)


def f_gmlp_pre(pu, pv, bu, bv, g, bb):
    return _gelu_tanh(pu + bu), _ln(_gelu_tanh(pv + bv), g, bb)


def f_ffn_gate(zg, zv):
    return (zg * _sigmoid(zg) * zv,)


def f_silu(x):
    return (x * _sigmoid(x),)


def f_silu_rows(dummy, cc):
    return (cc * _sigmoid(cc) + 0.0 * dummy,)


def _rope(x_in, tables, neg_sin, out_dtype, name, cols=None):
    w, ci = cols if cols else (x_in.shape[2], 0)
    sign = -1.0 if neg_sin else 1.0
    tm = 256

    def body(x_ref, cos_ref, sin_ref, o_ref):
        x = x_ref[0].astype(F32)
        cos = jnp.tile(cos_ref[0], (1, w // 128))
        sin = jnp.tile(sin_ref[0], (1, w // 128)) * sign
        lane = lax.broadcasted_iota(jnp.int32, x.shape, 1) & 31
        rot = jnp.where(lane < 16, -pltpu.roll(x, w - 16, 1), pltpu.roll(x, 16, 1))
        o_ref[0] = (x * cos + rot * sin).astype(o_ref.dtype)

    tspec = pl.BlockSpec((1, tm, 128), lambda i: (0, i, 0))
    return pl.pallas_call(
        body, name=name, out_shape=jax.ShapeDtypeStruct((1, L, w), out_dtype), grid=(L // tm,),
        in_specs=[pl.BlockSpec((1, tm, w), lambda i: (0, i, ci)), tspec, tspec],
        out_specs=pl.BlockSpec((1, tm, w), lambda i: (0, i, 0)),
        compiler_params=_cparams(("parallel",)),
    )(x_in, tables[0], tables[1])


CONV_TM = 256
CONV_RC = 8


class _ShiftedRows:
    def __init__(self, xp, cb):
        self.xp, self.memo = xp, {}
        self.row = lax.broadcasted_iota(jnp.int32, (CONV_RC, cb), 0)

    def _get(self, key, make):
        if key not in self.memo:
            self.memo[key] = make()
        return self.memo[key]

    def chunk(self, a):
        return self._get(("c", a), lambda: self.xp[a:a + CONV_RC, :])

    def rot(self, a, j):
        return self._get(("r", a, j), lambda: pltpu.roll(self.chunk(a), CONV_RC - j, 0))

    def at(self, a):
        q, j = divmod(a, CONV_RC)
        if j == 0:
            return self.chunk(a)
        low = self._get(("m", j), lambda: self.row < CONV_RC - j)
        return jnp.where(low, self.rot(q * CONV_RC, j), self.rot(q * CONV_RC + CONV_RC, j))


def _conv_geometry(x, k):
    nb, t, w = x.shape
    halo = 16 if k > 17 else 8
    cb = _pick(w, (512,)) if w > 768 else w
    return nb, t, w, halo, cb, (k - 1) // 2


def _conv_in_specs(t, halo, cb):
    per = CONV_TM // halo
    last = t // halo - 1
    return [
        pl.BlockSpec((1, CONV_TM, cb), lambda g, jc, i: (g, i, jc)),
        pl.BlockSpec((1, halo, cb), lambda g, jc, i: (g, jnp.maximum(i * per - 1, 0), jc)),
        pl.BlockSpec((1, halo, cb), lambda g, jc, i: (g, jnp.minimum((i + 1) * per, last), jc)),
    ]


def _conv_fill(xp, x_ref, prev_ref, next_ref, halo, t):
    i = pl.program_id(2)
    seg_first = (i * CONV_TM == 0) | (i * CONV_TM == L)
    seg_last = ((i + 1) * CONV_TM == L) | ((i + 1) * CONV_TM == t)
    xp[0:halo, :] = jnp.where(seg_first, 0.0, prev_ref[0].astype(F32))
    xp[halo:halo + CONV_TM, :] = x_ref[0].astype(F32)
    xp[halo + CONV_TM:, :] = jnp.where(seg_last, 0.0, next_ref[0].astype(F32))


def _dwconv(x, w, b, *, name, out_dtype=F32):
    k = w.shape[1]
    nb, t, wd, halo, cb, half = _conv_geometry(x, k)
    base = halo - half

    def body(*refs):
        x_ref, prev_ref, next_ref, w_ref = refs[:4]
        b_ref = refs[4] if b is not None else None
        o_ref, xp = refs[-2], refs[-1]
        _conv_fill(xp, x_ref, prev_ref, next_ref, halo, t)
        rows = _ShiftedRows(xp, cb)
        for o in range(0, CONV_TM, CONV_RC):
            acc = None
            for kk in range(k):
                term = w_ref[0, kk:kk + 1, :] * rows.at(o + base + kk)
                acc = term if acc is None else acc + term
            if b_ref is not None:
                acc = acc + b_ref[0]
            o_ref[0, o:o + CONV_RC, :] = acc.astype(o_ref.dtype)

    in_specs = _conv_in_specs(t, halo, cb) + [pl.BlockSpec((1, k, cb), lambda g, jc, i: (g, 0, jc))]
    args = [x, x, x, w]
    if b is not None:
        in_specs.append(pl.BlockSpec((1, 1, cb), lambda g, jc, i: (g, 0, jc)))
        args.append(b)
    return pl.pallas_call(
        body, name=name, out_shape=jax.ShapeDtypeStruct((nb, t, wd), out_dtype),
        grid=(nb, wd // cb, t // CONV_TM), in_specs=in_specs,
        out_specs=pl.BlockSpec((1, CONV_TM, cb), lambda g, jc, i: (g, i, jc)),
        scratch_shapes=[pltpu.VMEM((CONV_TM + 2 * halo, cb), F32)],
        compiler_params=_cparams(("parallel", "parallel", "parallel")),
    )(*args)


def _dwconv_wgrad(x, dy, k, *, name):
    nb, t, wd, halo, cb, half = _conv_geometry(x, k)
    base = halo - half

    def body(x_ref, prev_ref, next_ref, dy_ref, dw_ref, db_ref, xp):
        _conv_fill(xp, x_ref, prev_ref, next_ref, halo, t)
        i = pl.program_id(2)

        @pl.when(i == 0)
        def _():
            dw_ref[...] = jnp.zeros_like(dw_ref)
            db_ref[...] = jnp.zeros_like(db_ref)

        rows = _ShiftedRows(xp, cb)
        dys = [dy_ref[0, o:o + CONV_RC, :].astype(F32) for o in range(0, CONV_TM, CONV_RC)]
        db_ref[0] += jnp.sum(sum(dys[1:], dys[0]), axis=0, keepdims=True)
        for kk in range(k):
            acc = None
            for ci, dyc in enumerate(dys):
                term = dyc * rows.at(ci * CONV_RC + base + kk)
                acc = term if acc is None else acc + term
            dw_ref[0, kk:kk + 1, :] += jnp.sum(acc, axis=0, keepdims=True)

    dw, db = pl.pallas_call(
        body, name=name,
        out_shape=[jax.ShapeDtypeStruct((nb, k, wd), F32), jax.ShapeDtypeStruct((nb, 1, wd), F32)],
        grid=(nb, wd // cb, t // CONV_TM),
        in_specs=_conv_in_specs(t, halo, cb) + [pl.BlockSpec((1, CONV_TM, cb), lambda g, jc, i: (g, i, jc))],
        out_specs=[pl.BlockSpec((1, k, cb), lambda g, jc, i: (g, 0, jc)),
                   pl.BlockSpec((1, 1, cb), lambda g, jc, i: (g, 0, jc))],
        scratch_shapes=[pltpu.VMEM((CONV_TM + 2 * halo, cb), F32)],
        compiler_params=_cparams(("parallel", "parallel", "arbitrary")),
    )(x, x, x, dy)
    return dw, db


ATTN_SCALE = HEAD_DIM ** -0.5
QROWS = Q_PER_KV * ATTN_BLOCK
NEG = -1e30


def _attn_bias():
    qi = np.arange(ATTN_BLOCK)[:, None] + ATTN_BLOCK
    kj = np.arange(3 * ATTN_BLOCK)[None, :]
    near = np.abs(qi - kj) <= ATTN_BLOCK
    valid = np.stack([kj >= ATTN_BLOCK, kj >= 0, kj < 2 * ATTN_BLOCK])
    return jnp.asarray(np.where(near[None] & valid, 0.0, NEG), F32)


def _attn_scores(q, kw, kc, bias):
    nt = (((1,), (1,)), ((), ()))
    s_w = lax.dot_general(q, kw, nt, preferred_element_type=F32) * ATTN_SCALE
    s_w = (s_w.reshape(Q_PER_KV, ATTN_BLOCK, 3 * ATTN_BLOCK) + bias[None]).reshape(QROWS, 3 * ATTN_BLOCK)
    s_c = lax.dot_general(q, kc, nt, preferred_element_type=F32) * ATTN_SCALE
    return s_w, s_c


def _sink_col(sink_ref, hk):
    return jnp.concatenate([jnp.full((ATTN_BLOCK, 1), sink_ref[hk * Q_PER_KV + g], F32) for g in range(Q_PER_KV)], axis=0)


def _attn_specs():
    qspec = pl.BlockSpec((Q_PER_KV, ATTN_BLOCK, HEAD_DIM), lambda hk, n: (hk, n, 0))
    kspec = pl.BlockSpec((1, L + 2 * ATTN_BLOCK, HEAD_DIM), lambda hk, n: (hk, 0, 0))
    cspec = pl.BlockSpec((1, LC, HEAD_DIM), lambda hk, n: (hk, 0, 0))
    lspec = pl.BlockSpec((Q_PER_KV, ATTN_BLOCK, 1), lambda hk, n: (hk, n, 0))
    sspec = pl.BlockSpec(memory_space=pltpu.SMEM)
    last = L // ATTN_BLOCK - 1
    bspec = pl.BlockSpec((1, ATTN_BLOCK, 3 * ATTN_BLOCK),
                         lambda hk, n: (jnp.where(n == 0, 0, jnp.where(n == last, 2, 1)), 0, 0))
    return qspec, kspec, cspec, lspec, sspec, bspec


def _attn_fwd(q, k, v, kc, vc, sink):
    qspec, kspec, cspec, lspec, sspec, bspec = _attn_specs()

    def body(bias_ref, q_ref, k_ref, v_ref, kc_ref, vc_ref, sink_ref, o_ref, lse_ref):
        hk, n = pl.program_id(0), pl.program_id(1)
        qv = q_ref[...].reshape(QROWS, HEAD_DIM)
        start = pl.multiple_of(n * ATTN_BLOCK, ATTN_BLOCK)
        kw = k_ref[0, pl.ds(start, 3 * ATTN_BLOCK), :]
        vw = v_ref[0, pl.ds(start, 3 * ATTN_BLOCK), :]
        s_w, s_c = _attn_scores(qv, kw, kc_ref[0], bias_ref[0])
        sk = _sink_col(sink_ref, hk)
        m = jnp.maximum(jnp.maximum(jnp.max(s_w, -1, keepdims=True), jnp.max(s_c, -1, keepdims=True)), sk)
        p_w, p_c = jnp.exp(s_w - m), jnp.exp(s_c - m)
        den = jnp.sum(p_w, -1, keepdims=True) + jnp.sum(p_c, -1, keepdims=True) + jnp.exp(sk - m)
        o = (jnp.dot(p_w.astype(BF16), vw, preferred_element_type=F32)
             + jnp.dot(p_c.astype(BF16), vc_ref[0], preferred_element_type=F32)) / den
        o_ref[...] = o.reshape(Q_PER_KV, ATTN_BLOCK, HEAD_DIM).astype(o_ref.dtype)
        lse_ref[...] = (m + jnp.log(den)).reshape(Q_PER_KV, ATTN_BLOCK, 1)

    return pl.pallas_call(
        body, name="attn_fwd",
        out_shape=[jax.ShapeDtypeStruct((N_Q, L, HEAD_DIM), BF16), jax.ShapeDtypeStruct((N_Q, L, 1), F32)],
        grid=(N_KV, L // ATTN_BLOCK),
        in_specs=[bspec, qspec, kspec, kspec, cspec, cspec, sspec], out_specs=[qspec, lspec],
        compiler_params=_cparams(("parallel", "parallel")),
    )(_attn_bias(), q, k, v, kc, vc, sink)


def _attn_bwd(q, k, v, kc, vc, sink, o, lse, do):
    qspec, kspec, cspec, lspec, sspec, bspec = _attn_specs()
    tn = (((0,), (0,)), ((), ()))
    nt = (((1,), (1,)), ((), ()))

    def body(bias_ref, q_ref, k_ref, v_ref, kc_ref, vc_ref, sink_ref, o_ref, lse_ref, do_ref,
             dq_ref, dk_ref, dv_ref, dkc_ref, dvc_ref, dsink_ref):
        hk, n = pl.program_id(0), pl.program_id(1)
        qv = q_ref[...].reshape(QROWS, HEAD_DIM)
        start = pl.multiple_of(n * ATTN_BLOCK, ATTN_BLOCK)
        win = pl.ds(start, 3 * ATTN_BLOCK)
        kw, vw = k_ref[0, win, :], v_ref[0, win, :]
        kcv, vcv = kc_ref[0], vc_ref[0]
        s_w, s_c = _attn_scores(qv, kw, kcv, bias_ref[0])
        lse_v = lse_ref[...].reshape(QROWS, 1)
        p_w, p_c = jnp.exp(s_w - lse_v), jnp.exp(s_c - lse_v)
        dov = do_ref[...].reshape(QROWS, HEAD_DIM).astype(F32)
        ov = o_ref[...].reshape(QROWS, HEAD_DIM).astype(F32)
        delta = jnp.sum(dov * ov, -1, keepdims=True)
        dob = dov.astype(BF16)
        dp_w = lax.dot_general(dob, vw, nt, preferred_element_type=F32)
        dp_c = lax.dot_general(dob, vcv, nt, preferred_element_type=F32)
        ds_w = (p_w * (dp_w - delta) * ATTN_SCALE).astype(BF16)
        ds_c = (p_c * (dp_c - delta) * ATTN_SCALE).astype(BF16)
        dq = jnp.dot(ds_w, kw, preferred_element_type=F32) + jnp.dot(ds_c, kcv, preferred_element_type=F32)
        dq_ref[...] = dq.reshape(Q_PER_KV, ATTN_BLOCK, HEAD_DIM)

        @pl.when(n == 0)
        def _():
            dk_ref[...] = jnp.zeros_like(dk_ref)
            dv_ref[...] = jnp.zeros_like(dv_ref)
            dkc_ref[...] = jnp.zeros_like(dkc_ref)
            dvc_ref[...] = jnp.zeros_like(dvc_ref)

        dk_ref[0, win, :] += lax.dot_general(ds_w, qv, tn, preferred_element_type=F32)
        dv_ref[0, win, :] += lax.dot_general(p_w.astype(BF16), dob, tn, preferred_element_type=F32)
        dkc_ref[0] += lax.dot_general(ds_c, qv, tn, preferred_element_type=F32)
        dvc_ref[0] += lax.dot_general(p_c.astype(BF16), dob, tn, preferred_element_type=F32)
        dsk = -jnp.exp(_sink_col(sink_ref, hk) - lse_v) * delta
        for g in range(Q_PER_KV):
            part = jnp.sum(dsk[g * ATTN_BLOCK:(g + 1) * ATTN_BLOCK])
            idx = hk * Q_PER_KV + g

            @pl.when(n == 0)
            def _(part=part, idx=idx):
                dsink_ref[idx] = part

            @pl.when(n > 0)
            def _(part=part, idx=idx):
                dsink_ref[idx] += part

    kshape = jax.ShapeDtypeStruct((N_KV, L + 2 * ATTN_BLOCK, HEAD_DIM), F32)
    cshape = jax.ShapeDtypeStruct((N_KV, LC, HEAD_DIM), F32)
    return pl.pallas_call(
        body, name="attn_bwd",
        out_shape=[jax.ShapeDtypeStruct((N_Q, L, HEAD_DIM), F32), kshape, kshape, cshape, cshape,
                   jax.ShapeDtypeStruct((N_Q,), F32)],
        grid=(N_KV, L // ATTN_BLOCK),
        in_specs=[bspec, qspec, kspec, kspec, cspec, cspec, sspec, qspec, lspec, qspec],
        out_specs=[qspec, kspec, kspec, cspec, cspec, sspec],
        compiler_params=_cparams(("arbitrary", "arbitrary")),
    )(_attn_bias(), q, k, v, kc, vc, sink, o, lse, do)


def _gm_specs():
    rspec = pl.BlockSpec((1, GM_CHUNK, GM_W), lambda n: (0, n, 0))
    wspec = pl.BlockSpec((GM_GROUPS, GM_CHUNK, GM_CHUNK), lambda n: (0, 0, 0))
    bspec = pl.BlockSpec((GM_GROUPS, GM_CHUNK, 1), lambda n: (0, 0, 0))
    return rspec, wspec, bspec


def _gm_spatial_fwd(u, v, ws, bs):
    rspec, wspec, bspec = _gm_specs()

    def body(u_ref, v_ref, ws_ref, bs_ref, o_ref):
        for g in range(GM_GROUPS):
            cols = slice(g * GM_CHUNK, (g + 1) * GM_CHUNK)
            s = jnp.dot(ws_ref[g], v_ref[0, :, cols], preferred_element_type=F32) + bs_ref[g]
            o_ref[0, :, cols] = (u_ref[0, :, cols] * s).astype(o_ref.dtype)

    return pl.pallas_call(
        body, name="gm_spatial_fwd", out_shape=jax.ShapeDtypeStruct((1, L, GM_W), BF16),
        grid=(L // GM_CHUNK,), in_specs=[rspec, rspec, wspec, bspec], out_specs=rspec,
        compiler_params=_cparams(("parallel",)),
    )(u, v, ws, bs)


def _gm_spatial_bwd(u, v, ws, bs, dus):
    rspec, wspec, bspec = _gm_specs()
    tn = (((0,), (0,)), ((), ()))
    nt = (((1,), (1,)), ((), ()))

    def body(u_ref, v_ref, ws_ref, bs_ref, d_ref, du_ref, dv_ref, dws_ref, dbs_ref):
        n = pl.program_id(0)

        @pl.when(n == 0)
        def _():
            dws_ref[...] = jnp.zeros_like(dws_ref)
            dbs_ref[...] = jnp.zeros_like(dbs_ref)

        for g in range(GM_GROUPS):
            cols = slice(g * GM_CHUNK, (g + 1) * GM_CHUNK)
            vb = v_ref[0, :, cols]
            s = jnp.dot(ws_ref[g], vb, preferred_element_type=F32) + bs_ref[g]
            d = d_ref[0, :, cols].astype(F32)
            du_ref[0, :, cols] = d * s
            ds = d * u_ref[0, :, cols]
            dsb = ds.astype(BF16)
            dv_ref[0, :, cols] = lax.dot_general(ws_ref[g], dsb, tn, preferred_element_type=F32)
            dws_ref[g] += lax.dot_general(dsb, vb, nt, preferred_element_type=F32)
            dbs_ref[g] += jnp.sum(ds, axis=1, keepdims=True)

    row = jax.ShapeDtypeStruct((1, L, GM_W), F32)
    return pl.pallas_call(
        body, name="gm_spatial_bwd",
        out_shape=[row, row, jax.ShapeDtypeStruct((GM_GROUPS, GM_CHUNK, GM_CHUNK), F32),
                   jax.ShapeDtypeStruct((GM_GROUPS, GM_CHUNK, 1), F32)],
        grid=(L // GM_CHUNK,), in_specs=[rspec, rspec, wspec, bspec, rspec],
        out_specs=[rspec, rspec, wspec, bspec],
        compiler_params=_cparams(("arbitrary",)),
    )(u, v, ws, bs, dus)


def _loss_head(h, target):
    tm = 256

    def body(h_ref, t_ref, dh_ref, loss_ref):
        d = h_ref[0] - t_ref[0]
        dh_ref[0] = d * (1.0 / D)

        @pl.when(pl.program_id(0) == 0)
        def _():
            loss_ref[...] = jnp.zeros_like(loss_ref)

        loss_ref[...] += jnp.sum(d * d) * (0.5 / D)

    spec = pl.BlockSpec((1, tm, D), lambda i: (0, i, 0))
    dh, loss = pl.pallas_call(
        body, name="loss_head",
        out_shape=[jax.ShapeDtypeStruct((1, L, D), F32), jax.ShapeDtypeStruct((8, 128), F32)],
        grid=(L // tm,), in_specs=[spec, spec],
        out_specs=[spec, pl.BlockSpec((8, 128), lambda i: (0, 0))],
        compiler_params=_cparams(("arbitrary",)),
    )(h, target)
    return dh, loss[0, 0]


def _adamw(parts, w, m, v, name):
    per_layer = isinstance(parts, (list, tuple))
    plist = list(parts) if per_layer else [parts]
    nl = len(plist) if per_layer else parts.shape[0]
    s, r, c = plist[0].shape[-3:]
    tr = r
    for cand in (512, 256, 128, 64, 32, 16):
        if r % cand == 0 and cand * c <= 131072:
            tr = cand
            break
    nr = r // tr
    npart = len(plist)
    c1 = 1.0 / (1.0 - ADAM_B1 ** ADAM_STEP)
    c2 = 1.0 / (1.0 - ADAM_B2 ** ADAM_STEP)

    def body(*refs):
        w_ref, m_ref, v_ref, g_ref, d_ref, nm_ref, nv_ref = refs[npart:]

        def update(read):
            g = read(0).astype(F32)
            for q in range(1, s):
                g = g + read(q).astype(F32)
            mn = ADAM_B1 * m_ref[0] + (1.0 - ADAM_B1) * g
            vn = ADAM_B2 * v_ref[0] + (1.0 - ADAM_B2) * (g * g)
            g_ref[0] = g
            nm_ref[0] = mn
            nv_ref[0] = vn
            d_ref[0] = -ADAM_LR * ((mn * c1) / (jnp.sqrt(vn * c2) + ADAM_EPS) + ADAM_WD * w_ref[0])

        if not per_layer:
            update(lambda q: refs[0][0, q])
        else:
            for l in range(nl):
                @pl.when(pl.program_id(0) == l)
                def _(l=l):
                    update(lambda q: refs[l][q])

    spec = pl.BlockSpec((1, tr, c), lambda li, i: (li, i, 0))
    shp = jax.ShapeDtypeStruct((nl, r, c), F32)
    if per_layer:
        pspecs = [pl.BlockSpec((s, tr, c), lambda li, i, l=l: (0, jnp.where(li == l, i, jnp.where(li > l, nr - 1, 0)), 0))
                  for l in range(nl)]
    else:
        pspecs = [pl.BlockSpec((1, s, tr, c), lambda li, i: (li, 0, i, 0))]
    return pl.pallas_call(
        body, name=name, out_shape=[shp] * 4, grid=(nl, nr),
        in_specs=pspecs + [spec, spec, spec], out_specs=[spec] * 4,
        compiler_params=_cparams(("arbitrary", "arbitrary")),
    )(*plist, w, m, v)


def _pack_rows(vecs, lanes=128, mult=8):
    flat = jnp.concatenate([v.reshape(-1) for v in vecs])
    n = flat.shape[0]
    rows = -(-n // (mult * lanes)) * mult
    return jnp.pad(flat, (0, rows * lanes - n)).reshape(rows, lanes)


def _unpack_rows(packed, shapes):
    flat = packed.reshape(-1)
    out, pos = [], 0
    for s in shapes:
        n = 1
        for d_ in s:
            n *= d_
        out.append(flat[pos:pos + n].reshape(s))
        pos += n
    return out


def _unshard_last(g):
    lead = g.shape[1:-1]
    return jnp.moveaxis(g, 0, -2).reshape(*lead, NDEV * g.shape[-1])


def _shard_last(full):
    lead, w = full.shape[:-1], full.shape[-1] // NDEV
    return jnp.moveaxis(full.reshape(*lead, NDEV, w), -2, 0)


def _rope_tables():
    rows = L // GRID_W
    row = jnp.repeat(jnp.arange(rows), GRID_W).astype(F32)
    col = jnp.tile(jnp.arange(GRID_W), rows).astype(F32)
    axis_dim = HEAD_DIM // 2
    inv_freq = ROPE_BASE ** (-jnp.arange(0, axis_dim, 2, dtype=F32) / axis_dim)
    ang_r, ang_c = row[:, None] * inv_freq[None, :], col[:, None] * inv_freq[None, :]
    ang = jnp.concatenate([ang_r, ang_r, ang_c, ang_c], axis=-1)
    ang = jnp.concatenate([ang, ang], axis=-1)[None]
    return jnp.cos(ang), jnp.sin(ang)


def _heads(x, nh):
    t = x.shape[1]
    return x.reshape(t, nh, HEAD_DIM).transpose(1, 0, 2)


def _unheads(x):
    nh, t, _ = x.shape
    return x.transpose(1, 0, 2).reshape(1, t, nh * HEAD_DIM)


FFN_HALO = 16
FFN_PAIRS = 4


def _ffn_tile(t):
    return 512 if t == L else 256


def _halo_specs(t, tm, block, index):
    per, last = tm // FFN_HALO, t // FFN_HALO - 1
    return [pl.BlockSpec(block(tm), lambda d, i: index(d, i)),
            pl.BlockSpec(block(FFN_HALO), lambda d, i: index(d, jnp.maximum(i * per - 1, 0))),
            pl.BlockSpec(block(FFN_HALO), lambda d, i: index(d, jnp.minimum((i + 1) * per, last)))]


def _seg_edges(i, tm, t):
    return (i * tm == 0) | (i * tm == L), ((i + 1) * tm == L) | ((i + 1) * tm == t)


FFN_RC = 8


def _sigmoid_t(x):
    return 0.5 * jnp.tanh(0.5 * x) + 0.5


class _RowShifts:
    def __init__(self, buf, s):
        self.buf, self.s, self.memo = buf, s, {}
        rows = lax.broadcasted_iota(jnp.int32, (FFN_RC, FFN_BLK), 0)
        self.first, self.last = rows == 0, rows == FFN_RC - 1

    def chunk(self, r):
        if r not in self.memo:
            self.memo[r] = self.buf[self.s, r:r + FFN_RC, :]
        return self.memo[r]

    def rot(self, r, by):
        if (r, by) not in self.memo:
            self.memo[(r, by)] = pltpu.roll(self.chunk(r), by, 0)
        return self.memo[(r, by)]

    def triple(self, r):
        before = jnp.where(self.first, self.rot(r - FFN_RC, 1), self.rot(r, 1))
        behind = jnp.where(self.last, self.rot(r + FFN_RC, FFN_RC - 1), self.rot(r, FFN_RC - 1))
        return before, self.chunk(r), behind


def _conv3_of(triple, cw_ref, s, flip=False):
    taps = [cw_ref[s, 0, k:k + 1, :] for k in ((2, 1, 0) if flip else (0, 1, 2))]
    return taps[0] * triple[0] + taps[1] * triple[1] + taps[2] * triple[2]


def _ffn_core_fwd(a2, up, cw, cb, down, name):
    t = a2.shape[1]
    tm = _ffn_tile(t)
    h0 = FFN_HALO

    def body(a_ref, ap_ref, an_ref, up_ref, cw_ref, cb_ref, dn_ref, z_ref, f_ref, abuf, zbuf, ubuf):
        d, i = pl.program_id(0), pl.program_id(1)
        seg_first, seg_last = _seg_edges(i, tm, t)
        abuf[0:h0, :] = ap_ref[0]
        abuf[h0:h0 + tm, :] = a_ref[0]
        abuf[h0 + tm:, :] = an_ref[0]
        for s in range(2):
            zbuf[s] = jnp.dot(abuf[...], up_ref[s, 0], preferred_element_type=F32)

        @pl.when(seg_first)
        def _():
            zbuf[:, 0:h0, :] = jnp.zeros((2, h0, FFN_BLK), F32)

        @pl.when(seg_last)
        def _():
            zbuf[:, h0 + tm:, :] = jnp.zeros((2, h0, FFN_BLK), F32)

        for s in range(2):
            z_ref[s, 0] = zbuf[s, h0:h0 + tm, :].astype(z_ref.dtype)

        zs = [_RowShifts(zbuf, 0), _RowShifts(zbuf, 1)]
        for r in range(h0, h0 + tm, FFN_RC):
            zg = _conv3_of(zs[0].triple(r), cw_ref, 0) + cb_ref[0, 0]
            zv = _conv3_of(zs[1].triple(r), cw_ref, 1) + cb_ref[1, 0]
            ubuf[r - h0:r - h0 + FFN_RC, :] = zg * _sigmoid_t(zg) * zv
        prod = jnp.dot(ubuf[...].astype(BF16), dn_ref[0], preferred_element_type=F32)
        rows = pl.ds(pl.multiple_of(i * tm, tm), tm)

        @pl.when(d == 0)
        def _():
            f_ref[0, rows, :] = prod

        @pl.when(d > 0)
        def _():
            f_ref[0, rows, :] += prod

    pair = lambda r, c: pl.BlockSpec((2, 1, r, c), lambda d, i: (0, d, 0, 0))
    return pl.pallas_call(
        body, name=name,
        out_shape=[jax.ShapeDtypeStruct((2, FFN_PAIRS, t, FFN_BLK), BF16), jax.ShapeDtypeStruct((1, t, D), F32)],
        grid=(FFN_PAIRS, t // tm),
        in_specs=_halo_specs(t, tm, lambda r: (1, r, D), lambda d, i: (0, i, 0))
        + [pair(D, FFN_BLK), pair(FFN_K, FFN_BLK), pair(1, FFN_BLK),
           pl.BlockSpec((1, FFN_BLK, D), lambda d, i: (d, 0, 0))],
        out_specs=[pl.BlockSpec((2, 1, tm, FFN_BLK), lambda d, i: (0, d, i, 0)),
                   pl.BlockSpec((1, t, D), lambda d, i: (0, 0, 0))],
        scratch_shapes=[pltpu.VMEM((tm + 2 * h0, D), BF16), pltpu.VMEM((2, tm + 2 * h0, FFN_BLK), F32),
                        pltpu.VMEM((tm, FFN_BLK), F32)],
        compiler_params=_cparams(("arbitrary", "arbitrary")),
    )(a2, a2, a2, up, cw, cb, down)


def _ffn_core_bwd(df, z, cw, cb, down, up, a2, name):
    t = df.shape[1]
    tm = _ffn_tile(t)
    h0 = FFN_HALO
    ni = t // tm
    w0, wn = h0 // 2, tm + h0
    tn = (((0,), (0,)), ((), ()))
    nt = (((1,), (1,)), ((), ()))

    def body(df_ref, dfp_ref, dfn_ref, z_ref, zp_ref, zn_ref, cw_ref, cb_ref, dn_ref, up_ref, a2_ref,
             dcw_ref, dcb_ref, ddn_ref, da_ref, dup_ref, dfbuf, zbuf, dzbuf, acc, dubuf, dzo, acc_up):
        d, i = pl.program_id(0), pl.program_id(1)
        seg_first, seg_last = _seg_edges(i, tm, t)
        dfbuf[0:h0, :] = dfp_ref[0]
        dfbuf[h0:h0 + tm, :] = df_ref[0]
        dfbuf[h0 + tm:, :] = dfn_ref[0]
        for s in range(2):
            zbuf[s, 0:h0, :] = zp_ref[s, 0].astype(F32)
            zbuf[s, h0:h0 + tm, :] = z_ref[s, 0].astype(F32)
            zbuf[s, h0 + tm:, :] = zn_ref[s, 0].astype(F32)

        @pl.when(seg_first)
        def _():
            zbuf[:, 0:h0, :] = jnp.zeros((2, h0, FFN_BLK), F32)

        @pl.when(seg_last)
        def _():
            zbuf[:, h0 + tm:, :] = jnp.zeros((2, h0, FFN_BLK), F32)

        dubuf[...] = lax.dot_general(dfbuf[...], dn_ref[0], nt, preferred_element_type=F32)

        zs = [_RowShifts(zbuf, 0), _RowShifts(zbuf, 1)]
        sums = [[jnp.zeros((FFN_RC, FFN_BLK), F32)] * (FFN_K + 1) for _ in range(2)]
        for r in range(w0, w0 + wn, FFN_RC):
            tz = [zs[0].triple(r), zs[1].triple(r)]
            zg = _conv3_of(tz[0], cw_ref, 0) + cb_ref[0, 0]
            zv = _conv3_of(tz[1], cw_ref, 1) + cb_ref[1, 0]
            sg = _sigmoid_t(zg)
            silu = zg * sg
            du = dubuf[r:r + FFN_RC, :]
            dzc = [du * zv * (sg * (1.0 + zg * (1.0 - sg))), du * silu]
            dzbuf[0, r:r + FFN_RC, :] = dzc[0]
            dzbuf[1, r:r + FFN_RC, :] = dzc[1]
            dubuf[r:r + FFN_RC, :] = silu * zv
            if h0 <= r < h0 + tm:
                for s in range(2):
                    sums[s] = [sums[s][k] + dzc[s] * tz[s][k] for k in range(FFN_K)] + [sums[s][FFN_K] + dzc[s]]

        @pl.when(seg_first)
        def _():
            dzbuf[:, w0:h0, :] = jnp.zeros((2, h0 - w0, FFN_BLK), F32)

        @pl.when(seg_last)
        def _():
            dzbuf[:, h0 + tm:w0 + wn, :] = jnp.zeros((2, w0, FFN_BLK), F32)

        @pl.when(i == 0)
        def _():
            dcw_ref[...] = jnp.zeros_like(dcw_ref)
            dcb_ref[...] = jnp.zeros_like(dcb_ref)

        da = None
        for s in range(2):
            dzs = _RowShifts(dzbuf, s)
            for r in range(h0, h0 + tm, FFN_RC):
                dzo[r - h0:r - h0 + FFN_RC, :] = _conv3_of(dzs.triple(r), cw_ref, s, flip=True)
            dzb = dzo[...].astype(BF16)
            part = lax.dot_general(dzb, up_ref[s, 0], nt, preferred_element_type=F32)
            da = part if da is None else da + part
            gup = lax.dot_general(dzb, a2_ref[0], tn, preferred_element_type=F32)

            @pl.when(i == 0)
            def _(s=s, gup=gup):
                acc_up[s] = gup

            @pl.when(i > 0)
            def _(s=s, gup=gup):
                acc_up[s] += gup
            for k in range(FFN_K):
                dcw_ref[s, 0, k:k + 1, :] += jnp.sum(sums[s][k], axis=0, keepdims=True)
            dcb_ref[s, 0] += jnp.sum(sums[s][FFN_K], axis=0, keepdims=True)
        rows = pl.ds(pl.multiple_of(i * tm, tm), tm)

        @pl.when(d == 0)
        def _():
            da_ref[0, rows, :] = da

        @pl.when(d > 0)
        def _():
            da_ref[0, rows, :] += da

        prod = lax.dot_general(dubuf[h0:h0 + tm, :].astype(BF16), dfbuf[h0:h0 + tm, :], tn, preferred_element_type=F32)

        @pl.when(i == 0)
        def _():
            acc[...] = prod

        @pl.when(i > 0)
        def _():
            acc[...] += prod

        @pl.when(i == ni - 1)
        def _():
            ddn_ref[0] = acc[...].astype(ddn_ref.dtype)
            dup_ref[:, 0] = acc_up[...].astype(dup_ref.dtype)

    pair = lambda r, c: pl.BlockSpec((2, 1, r, c), lambda d, i: (0, d, 0, 0))
    return pl.pallas_call(
        body, name=name,
        out_shape=[jax.ShapeDtypeStruct((2, FFN_PAIRS, FFN_K, FFN_BLK), F32),
                   jax.ShapeDtypeStruct((2, FFN_PAIRS, 1, FFN_BLK), F32),
                   jax.ShapeDtypeStruct((FFN_PAIRS, FFN_BLK, D), BF16), jax.ShapeDtypeStruct((1, t, D), F32),
                   jax.ShapeDtypeStruct((2, FFN_PAIRS, FFN_BLK, D), BF16)],
        grid=(FFN_PAIRS, ni),
        in_specs=_halo_specs(t, tm, lambda r: (1, r, D), lambda d, i: (0, i, 0))
        + _halo_specs(t, tm, lambda r: (2, 1, r, FFN_BLK), lambda d, i: (0, d, i, 0))
        + [pair(FFN_K, FFN_BLK), pair(1, FFN_BLK), pl.BlockSpec((1, FFN_BLK, D), lambda d, i: (d, 0, 0)),
           pair(D, FFN_BLK), pl.BlockSpec((1, tm, D), lambda d, i: (0, i, 0))],
        out_specs=[pair(FFN_K, FFN_BLK),
                   pair(1, FFN_BLK), pl.BlockSpec((1, FFN_BLK, D), lambda d, i: (d, 0, 0)),
                   pl.BlockSpec((1, t, D), lambda d, i: (0, 0, 0)), pair(FFN_BLK, D)],
        scratch_shapes=[pltpu.VMEM((tm + 2 * h0, D), BF16), pltpu.VMEM((2, tm + 2 * h0, FFN_BLK), F32),
                        pltpu.VMEM((2, tm + 2 * h0, FFN_BLK), F32), pltpu.VMEM((FFN_BLK, D), F32),
                        pltpu.VMEM((tm + 2 * h0, FFN_BLK), F32),
                        pltpu.VMEM((tm, FFN_BLK), F32), pltpu.VMEM((2, FFN_BLK, D), F32)],
        compiler_params=_cparams(("arbitrary", "arbitrary")),
    )(df, df, df, z, z, z, cw, cb, down, up, a2)


def _ffn_fwd(i, h, mod, ng, wts):
    a2 = _rw_fwd(f_modnorm, [(h, 0)], [(ng[2], "one"), (mod["sc2"], "seg"), (mod["sh2"], "seg")],
                 [(D, BF16)], name=f"ffn{i}_norm")[0]
    z, f = _ffn_core_fwd(a2, wts["up"], wts["cw"], wts["cb"], wts["down"], f"ffn{i}_core")
    h2 = _rw_fwd(f_resgate, [(h, 0), (f, 0)], [(mod["g2"], "seg"), (ng[3], "one")], [(D, F32)],
                 name=f"ffn{i}_res")[0]
    return h2, (h, a2, z, f)


def _ffn_bwd(i, dh, res, mod, ng, wts):
    h, a2, z, f = res
    t = h.shape[1]
    df, dg2, dng3 = _rw_bwd(f_gate_rms, [(f, 0)], [(mod["g2"], "seg"), (ng[3], "one")], [dh],
                            name=f"ffn{i}_res_b", row_grad=[(0, BF16)], param_grad=[0, 1])
    dcw, dcb, d_down, da2, d_up = _ffn_core_bwd(df, z, wts["cw"], wts["cb"], wts["down"], wts["up"], a2,
                                                f"ffn{i}_core_b")
    dcw, dcb = dcw.reshape(NDEV, FFN_K, FFN_BLK), dcb.reshape(NDEV, 1, FFN_BLK)
    d_up = d_up.reshape(NDEV, FFN_BLK, D)
    dh_in, dng2, dsc2, dsh2 = _rw_bwd(
        f_modnorm, [(h, 0)], [(ng[2], "one"), (mod["sc2"], "seg"), (mod["sh2"], "seg")], [da2],
        name=f"ffn{i}_norm_b", row_grad=[(0, F32)], param_grad=[0, 1, 2], add=dh)
    grads = dict(up=d_up, down=d_down, cw=dcw, cb=dcb, ng2=dng2, ng3=dng3, sc2=dsc2, sh2=dsh2, g2=dg2)
    return dh_in, grads


def _mixer_norm_fwd(i, h, mod, ng):
    return _rw_fwd(f_modnorm, [(h, 0)], [(ng[0], "one"), (mod["sc1"], "seg"), (mod["sh1"], "seg")],
                   [(D, BF16)], name=f"mix{i}_norm")[0]


def _mixer_norm_bwd(i, h, mod, ng, da, dh):
    return _rw_bwd(f_modnorm, [(h, 0)], [(ng[0], "one"), (mod["sc1"], "seg"), (mod["sh1"], "seg")], [da],
                   name=f"mix{i}_norm_b", row_grad=[(0, F32)], param_grad=[0, 1, 2], add=dh)


def _conformer_fwd(i, h, mod, ng, wts):
    a = _mixer_norm_fwd(i, h, mod, ng)
    p = _mm(a, wts["w_in"], "nn", out_dtype=BF16, name=f"cm{i}_in")
    z = _rw_fwd(f_glu, [(p, 0, 2)], [(wts["b_in"], "one", 2)], [(D, F32)], name=f"cm{i}_glu")[0]
    zc = _dwconv(z, wts["dw_w"], wts["dw_b"], name=f"cm{i}_conv")
    r = _rw_fwd(f_lnsilu, [(zc, 0)], [(wts["ln_g"], "one"), (wts["ln_b"], "one")], [(D, BF16)],
                name=f"cm{i}_ln")[0]
    y = _mm(r, wts["w_out"], "nn", name=f"cm{i}_out")
    h2 = _rw_fwd(f_resgate_bias, [(h, 0), (y, 0)], [(mod["g1"], "seg"), (ng[1], "one"), (wts["b_out"], "one")],
                 [(D, F32)], name=f"cm{i}_res")[0]
    return h2, (h, a, p, z, zc, r, y)


def _conformer_bwd(i, dh, res, mod, ng, wts):
    h, a, p, z, zc, r, y = res
    dy, dg1, dng1, db_out = _rw_bwd(
        f_gate_rms_bias, [(y, 0)], [(mod["g1"], "seg"), (ng[1], "one"), (wts["b_out"], "one")], [dh],
        name=f"cm{i}_res_b", row_grad=[(0, BF16)], param_grad=[0, 1, 2])
    dr = _mm(dy, wts["w_out"], "nt", name=f"cm{i}_out_bx")
    d_w_out = _mm(r, dy, "tn", out_dtype=BF16, name=f"cm{i}_out_bw")
    dzc, dln_g, dln_b = _rw_bwd(f_lnsilu, [(zc, 0)], [(wts["ln_g"], "one"), (wts["ln_b"], "one")], [dr],
                                name=f"cm{i}_ln_b", row_grad=[(0, F32)], param_grad=[0, 1])
    ddw_w, ddw_b = _dwconv_wgrad(z, dzc, CM_K, name=f"cm{i}_conv_bw")
    dz = _dwconv(dzc, wts["dw_w"][:, ::-1, :], None, name=f"cm{i}_conv_bx")
    dp, db_in = _rw_bwd(f_glu, [(p, 0, 2)], [(wts["b_in"], "one", 2)], [dz], name=f"cm{i}_glu_b",
                        row_grad=[(0, BF16)], param_grad=[0])
    d_w_in = _mm(a, dp, "tn", out_dtype=BF16, name=f"cm{i}_in_bw")
    da = _mm(dp, wts["w_in"], "nt", name=f"cm{i}_in_bx")
    dh_in, dng0, dsc1, dsh1 = _mixer_norm_bwd(i, h, mod, ng, da, dh)
    grads = dict(w_in=d_w_in, w_out=d_w_out, b_in=db_in, dw_w=ddw_w, dw_b=ddw_b, ln_g=dln_g, ln_b=dln_b,
                 b_out=db_out, ng0=dng0, ng1=dng1, sc1=dsc1, sh1=dsh1, g1=dg1)
    return dh_in, grads


def _attention_fwd(i, h_all, mod, ng, wts, tables):
    a = _mixer_norm_fwd(i, h_all, mod, ng)
    qkv = _mm(a, wts["w_qkv"], "nn", name="attn_qkv")
    kv0 = N_Q * HEAD_DIM
    kv1 = kv0 + N_KV * HEAD_DIM
    q = _rope(qkv, tables, False, BF16, "attn_rope_q", cols=(kv0, 0))
    k = _rope(qkv, tables, False, BF16, "attn_rope_k", cols=(kv1 - kv0, kv0 // (kv1 - kv0)))
    pad = ((0, 0), (ATTN_BLOCK, ATTN_BLOCK), (0, 0))
    q_h = _heads(q, N_Q)
    k_h = jnp.pad(_heads(k, N_KV), pad)
    v_h = jnp.pad(_heads(qkv[:, :L, kv1:].astype(BF16), N_KV), pad)
    kc_h = _heads(qkv[:, L:, kv0:kv1].astype(BF16), N_KV)
    vc_h = _heads(qkv[:, L:, kv1:].astype(BF16), N_KV)
    o_h, lse = _attn_fwd(q_h, k_h, v_h, kc_h, vc_h, wts["sink"])
    o = _unheads(o_h)
    y = _mm(o, wts["w_o"], "nn", name="attn_o")
    h_lat = h_all[:, :L]
    mod_lat = {k_: v_[:1] for k_, v_ in mod.items()}
    h2 = _rw_fwd(f_resgate, [(h_lat, 0), (y, 0)], [(mod_lat["g1"], "seg"), (ng[1], "one")], [(D, F32)],
                 name="attn_res")[0]
    return h2, (h_all, a, q_h, k_h, v_h, kc_h, vc_h, o_h, lse, o, y)


def _attention_bwd(i, dh, res, mod, ng, wts, tables):
    h_all, a, q_h, k_h, v_h, kc_h, vc_h, o_h, lse, o, y = res
    mod_lat = {k_: v_[:1] for k_, v_ in mod.items()}
    dy, dg1, dng1 = _rw_bwd(f_gate_rms, [(y, 0)], [(mod_lat["g1"], "seg"), (ng[1], "one")], [dh],
                            name="attn_res_b", row_grad=[(0, BF16)], param_grad=[0, 1])
    do = _mm(dy, wts["w_o"], "nt", name="attn_o_bx")
    d_w_o = _mm(o, dy, "tn", out_dtype=BF16, name="attn_o_bw")
    dq_h, dk_h, dv_h, dkc_h, dvc_h, dsink = _attn_bwd(q_h, k_h, v_h, kc_h, vc_h, wts["sink"], o_h, lse,
                                                        _heads(do, N_Q))
    dq = _rope(_unheads(dq_h), tables, True, BF16, "attn_rope_q_b")
    dk = _rope(_unheads(dk_h[:, ATTN_BLOCK:-ATTN_BLOCK]), tables, True, BF16, "attn_rope_k_b")
    dv = _unheads(dv_h[:, ATTN_BLOCK:-ATTN_BLOCK]).astype(BF16)
    d_lat = jnp.concatenate([dq, dk, dv], axis=2)
    d_ctx = jnp.concatenate([jnp.zeros((1, LC, N_Q * HEAD_DIM), BF16), _unheads(dkc_h).astype(BF16),
                             _unheads(dvc_h).astype(BF16)], axis=2)
    dqkv = jnp.concatenate([d_lat, d_ctx], axis=1)
    d_w_qkv = _mm(a, dqkv, "tn", out_dtype=BF16, name="attn_qkv_bw")
    da = _mm(dqkv, wts["w_qkv"], "nt", name="attn_qkv_bx")
    dh_res = jnp.concatenate([dh, jnp.zeros((1, LC, D), F32)], axis=1)
    dh_in, dng0, dsc1, dsh1 = _mixer_norm_bwd(i, h_all, mod, ng, da, dh_res)
    grads = dict(w_qkv=d_w_qkv, w_o=d_w_o, sink=dsink, ng0=dng0, ng1=dng1, sc1=dsc1, sh1=dsh1, g1=dg1)
    return dh_in, grads


def _gmlp_fwd(i, h, mod, ng, wts):
    a = _mixer_norm_fwd(i, h, mod, ng)
    p = _mm(a, wts["w_in"], "nn", out_dtype=BF16, name="gm_in")
    u, v = _rw_fwd(f_gmlp_pre, [(p, 0, 2)], [(wts["b_in"], "one", 2), (wts["ln_g"], "one"), (wts["ln_b"], "one")],
                   [(GM_W, F32), (GM_W, BF16)], name="gm_pre")
    us = _gm_spatial_fwd(u, v, wts["w_s"], wts["b_s"])
    y = _mm(us, wts["w_out"], "nn", name="gm_out")
    h2 = _rw_fwd(f_resgate, [(h, 0), (y, 0)], [(mod["g1"], "seg"), (ng[1], "one")], [(D, F32)],
                 name="gm_res")[0]
    return h2, (h, a, p, u, v, us, y)


def _gmlp_bwd(i, dh, res, mod, ng, wts):
    h, a, p, u, v, us, y = res
    dy, dg1, dng1 = _rw_bwd(f_gate_rms, [(y, 0)], [(mod["g1"], "seg"), (ng[1], "one")], [dh],
                            name="gm_res_b", row_grad=[(0, BF16)], param_grad=[0, 1])
    dus = _mm(dy, wts["w_out"], "nt", name="gm_out_bx")
    d_w_out = _mm(us, dy, "tn", out_dtype=BF16, name="gm_out_bw")
    du, dv, dws, dbs = _gm_spatial_bwd(u, v, wts["w_s"], wts["b_s"], dus)
    dp, db_in, dln_g, dln_b = _rw_bwd(
        f_gmlp_pre, [(p, 0, 2)], [(wts["b_in"], "one", 2), (wts["ln_g"], "one"), (wts["ln_b"], "one")], [du, dv],
        name="gm_pre_b", row_grad=[(0, BF16)], param_grad=[0, 1, 2])
    d_w_in = _mm(a, dp, "tn", out_dtype=BF16, name="gm_in_bw")
    da = _mm(dp, wts["w_in"], "nt", name="gm_in_bx")
    dh_in, dng0, dsc1, dsh1 = _mixer_norm_bwd(i, h, mod, ng, da, dh)
    grads = dict(w_in=d_w_in, w_out=d_w_out, b_in=db_in, ln_g=dln_g, ln_b=dln_b, w_s=dws, b_s=dbs,
                 ng0=dng0, ng1=dng1, sc1=dsc1, sh1=dsh1, g1=dg1)
    return dh_in, grads


MOD_NAMES = ("sh1", "sc1", "g1", "sh2", "sc2", "g2")
SMALL = (
    ("norm_g", (4, 4, 128)), ("ffn_conv_w", (4, 3, 704)), ("cm_b_in", (2, 256)), ("cm_dw_w", (2, 31, 128)),
    ("cm_dw_b", (2, 128)), ("cm_ln_g", (2, 128)), ("cm_ln_b", (2, 128)), ("cm_b_out", (2, 128)),
    ("gm_b_in", (1, 512)), ("gm_ln_g", (1, 256)), ("gm_ln_b", (1, 256)))


def _mixer_weights(i, P):
    if i % 3 == 0:
        j = i // 3
        return dict(w_in=P["cm_w_in"][j], w_out=P["cm_w_out"][j], b_in=P["cm_b_in"][j].reshape(1, 1, 2 * D),
                    dw_w=P["cm_dw_w"][j][None], dw_b=P["cm_dw_b"][j].reshape(1, 1, D),
                    ln_g=P["cm_ln_g"][j].reshape(1, 1, D), ln_b=P["cm_ln_b"][j].reshape(1, 1, D),
                    b_out=P["cm_b_out"][j].reshape(1, 1, D))
    if i % 3 == 1:
        return dict(w_qkv=P["attn_w_qkv"], w_o=P["attn_w_o"], sink=P["attn_sink"].reshape(N_Q))
    return dict(w_in=P["gm_w_in"], w_out=P["gm_w_out"], b_in=P["gm_b_in"].reshape(1, 1, 2 * GM_W),
                ln_g=P["gm_ln_g"].reshape(1, 1, GM_W), ln_b=P["gm_ln_b"].reshape(1, 1, GM_W),
                w_s=P["gm_w_s"].reshape(GM_GROUPS, GM_CHUNK, GM_CHUNK).astype(BF16),
                b_s=P["gm_b_s"].reshape(GM_GROUPS, GM_CHUNK, 1))


def _ffn_weights(i, P):
    return dict(up=P["ffn_w_up"][i].reshape(2, FFN_PAIRS, D, FFN_BLK), down=P["ffn_w_down"][i],
                cw=P["ffn_conv_w"][i].reshape(2, FFN_PAIRS, FFN_K, FFN_BLK),
                cb=P["ffn_conv_b"][i].reshape(2, FFN_PAIRS, 1, FFN_BLK))


def _local_step(x, ctx, target, lat_mod, ctx_mod, norm_g, layer_weights, grads_ready):
    tables = _rope_tables()
    ng = [[norm_g[i, j].reshape(1, 1, D) for j in range(4)] for i in range(DEPTH)]

    def mods(i, with_ctx, token):
        out = {}
        for j, nme in enumerate(MOD_NAMES):
            rows = [lat_mod[i, j]] + ([ctx_mod[i, j]] if with_ctx else [])
            out[nme] = jnp.stack(rows).reshape(len(rows), 1, D) + token[0, 0]
        return out

    def after(mod, token):
        return mod if token is None else {k_: v_ + token[0, 0] for k_, v_ in mod.items()}

    h_all = jnp.concatenate([x, ctx], axis=1)
    wm0, wf0, tok = layer_weights(0, h_all)
    m0 = mods(0, True, tok)
    h, r0m = _conformer_fwd(0, h_all, m0, ng[0], wm0)
    wf0 = wf0(h) if callable(wf0) else wf0
    h, r0f = _ffn_fwd(0, h, m0, ng[0], wf0)
    wm1, wf1, tok = layer_weights(1, h)
    m1 = mods(1, True, tok)
    m1l = {k_: v_[:1] for k_, v_ in m1.items()}
    h, r1m = _attention_fwd(1, h, m1, ng[1], wm1, tables)
    wf1 = wf1(h) if callable(wf1) else wf1
    h, r1f = _ffn_fwd(1, h, m1l, ng[1], wf1)
    wm2, wf2, tok = layer_weights(2, h)
    m2 = mods(2, False, tok)
    h, r2m = _gmlp_fwd(2, h, m2, ng[2], wm2)
    wf2 = wf2(h) if callable(wf2) else wf2
    h, r2f = _ffn_fwd(2, h, m2, ng[2], wf2)
    wm3, wf3, tok = layer_weights(3, h)
    m3 = mods(3, False, tok)
    h, r3m = _conformer_fwd(3, h, m3, ng[3], wm3)
    wf3 = wf3(h) if callable(wf3) else wf3
    h, r3f = _ffn_fwd(3, h, m3, ng[3], wf3)
    dh, loss = _loss_head(h, target)

    G = {}
    dh, G["f3"] = _ffn_bwd(3, dh, r3f, m3, ng[3], wf3)
    tok = grads_ready("f3", G["f3"])
    dh, G["m3"] = _conformer_bwd(3, dh, r3m, after(m3, tok), ng[3], wm3)
    tok = grads_ready("m3", G["m3"])
    dh, G["f2"] = _ffn_bwd(2, dh, r2f, after(m2, tok), ng[2], wf2)
    tok = grads_ready("f2", G["f2"])
    dh, G["m2"] = _gmlp_bwd(2, dh, r2m, after(m2, tok), ng[2], wm2)
    tok = grads_ready("m2", G["m2"])
    dh, G["f1"] = _ffn_bwd(1, dh, r1f, after(m1l, tok), ng[1], wf1)
    tok = grads_ready("f1", G["f1"])
    dh, G["m1"] = _attention_bwd(1, dh, r1m, after(m1, tok), ng[1], wm1, tables)
    tok = grads_ready("m1", G["m1"])
    dh, G["f0"] = _ffn_bwd(0, dh, r0f, after(m0, tok), ng[0], wf0)
    tok = grads_ready("f0", G["f0"])
    dh, G["m0"] = _conformer_bwd(0, dh, r0m, after(m0, tok), ng[0], wm0)
    grads_ready("m0", G["m0"])
    grad_x = dh[:, :L]

    zero = jnp.zeros((D,), F32)
    dmod = []
    for seg in range(2):
        per_layer = []
        for i in range(DEPTH):
            vals = []
            for nme in MOD_NAMES:
                src = G[("m" if nme.endswith("1") else "f") + str(i)][nme]
                vals.append(src[seg, 0] if src.shape[0] > seg else zero)
            per_layer.append(jnp.concatenate(vals))
        dmod.append(jnp.stack(per_layer))
    dmod = jnp.stack(dmod)
    return loss, grad_x, G, dmod


def kernel(x, c, ctx, c_ctx, ada_w, ada_b, norm_g, ffn_w_up, ffn_conv_w, ffn_conv_b, ffn_w_down, cm_w_in, cm_b_in, cm_dw_w, cm_dw_b, cm_ln_g, cm_ln_b, cm_w_out, cm_b_out, attn_w_qkv, attn_sink, attn_w_o, gm_w_in, gm_b_in, gm_ln_g, gm_ln_b, gm_w_s, gm_b_s, gm_w_out, loss_target, m_c_ctx, m_ada_w, m_ada_b, m_norm_g, m_ffn_w_up, m_ffn_conv_w, m_ffn_conv_b, m_ffn_w_down, m_cm_w_in, m_cm_b_in, m_cm_dw_w, m_cm_dw_b, m_cm_ln_g, m_cm_ln_b, m_cm_w_out, m_cm_b_out, m_attn_w_qkv, m_attn_sink, m_attn_w_o, m_gm_w_in, m_gm_b_in, m_gm_ln_g, m_gm_ln_b, m_gm_w_s, m_gm_b_s, m_gm_w_out, v_c_ctx, v_ada_w, v_ada_b, v_norm_g, v_ffn_w_up, v_ffn_conv_w, v_ffn_conv_b, v_ffn_w_down, v_cm_w_in, v_cm_b_in, v_cm_dw_w, v_cm_dw_b, v_cm_ln_g, v_cm_ln_b, v_cm_w_out, v_cm_b_out, v_attn_w_qkv, v_attn_sink, v_attn_w_o, v_gm_w_in, v_gm_b_in, v_gm_ln_g, v_gm_ln_b, v_gm_w_s, v_gm_b_s, v_gm_w_out):
    W = dict(c_ctx=c_ctx, ada_w=ada_w, ada_b=ada_b, norm_g=norm_g, ffn_w_up=ffn_w_up, ffn_conv_w=ffn_conv_w, ffn_conv_b=ffn_conv_b, ffn_w_down=ffn_w_down, cm_w_in=cm_w_in, cm_b_in=cm_b_in, cm_dw_w=cm_dw_w, cm_dw_b=cm_dw_b, cm_ln_g=cm_ln_g, cm_ln_b=cm_ln_b, cm_w_out=cm_w_out, cm_b_out=cm_b_out, attn_w_qkv=attn_w_qkv, attn_sink=attn_sink, attn_w_o=attn_w_o, gm_w_in=gm_w_in, gm_b_in=gm_b_in, gm_ln_g=gm_ln_g, gm_ln_b=gm_ln_b, gm_w_s=gm_w_s, gm_b_s=gm_b_s, gm_w_out=gm_w_out)
    M = dict(c_ctx=m_c_ctx, ada_w=m_ada_w, ada_b=m_ada_b, norm_g=m_norm_g, ffn_w_up=m_ffn_w_up, ffn_conv_w=m_ffn_conv_w, ffn_conv_b=m_ffn_conv_b, ffn_w_down=m_ffn_w_down, cm_w_in=m_cm_w_in, cm_b_in=m_cm_b_in, cm_dw_w=m_cm_dw_w, cm_dw_b=m_cm_dw_b, cm_ln_g=m_cm_ln_g, cm_ln_b=m_cm_ln_b, cm_w_out=m_cm_w_out, cm_b_out=m_cm_b_out, attn_w_qkv=m_attn_w_qkv, attn_sink=m_attn_sink, attn_w_o=m_attn_w_o, gm_w_in=m_gm_w_in, gm_b_in=m_gm_b_in, gm_ln_g=m_gm_ln_g, gm_ln_b=m_gm_ln_b, gm_w_s=m_gm_w_s, gm_b_s=m_gm_b_s, gm_w_out=m_gm_w_out)
    V = dict(c_ctx=v_c_ctx, ada_w=v_ada_w, ada_b=v_ada_b, norm_g=v_norm_g, ffn_w_up=v_ffn_w_up, ffn_conv_w=v_ffn_conv_w, ffn_conv_b=v_ffn_conv_b, ffn_w_down=v_ffn_w_down, cm_w_in=v_cm_w_in, cm_b_in=v_cm_b_in, cm_dw_w=v_cm_dw_w, cm_dw_b=v_cm_dw_b, cm_ln_g=v_cm_ln_g, cm_ln_b=v_cm_ln_b, cm_w_out=v_cm_w_out, cm_b_out=v_cm_b_out, attn_w_qkv=v_attn_w_qkv, attn_sink=v_attn_sink, attn_w_o=v_attn_w_o, gm_w_in=v_gm_w_in, gm_b_in=v_gm_b_in, gm_ln_g=v_gm_ln_g, gm_ln_b=v_gm_ln_b, gm_w_s=v_gm_w_s, gm_b_s=v_gm_b_s, gm_w_out=v_gm_w_out)
    me = 4 * lax.axis_index("x") + 2 * lax.axis_index("y") + lax.axis_index("c")
    small_shapes = [s for _, s in SMALL]

    small = _pack_rows([W[n] for n, _ in SMALL] + [c])
    layer_mats = [("cm_w_in", 0, "cm_w_out", 0), ("attn_w_qkv", 0, "attn_w_o", 0), ("gm_w_in", 0, "gm_w_out", 0),
                  ("cm_w_in", 1, "cm_w_out", 1)]
    local_bf16 = [[W[a][ja].astype(BF16), W[b][jb].astype(BF16), ffn_w_up[i].astype(BF16), ffn_w_down[i].astype(BF16)]
                  for i, (a, ja, b, jb) in enumerate(layer_mats)]
    gathered0 = _all_gather([small] + local_bf16[0][:2], "gather_params0")
    small_g = gathered0[0]
    col_to_full = lambda g: g.transpose(1, 0, 2).reshape(g.shape[1], NDEV * g.shape[2])
    P = {}
    unpacked = jax.vmap(lambda r: tuple(_unpack_rows(r, small_shapes + [(D,)])))(small_g)
    for (n, _), g in zip(SMALL, unpacked[:-1]):
        if n == "ffn_conv_w":
            P[n] = [g[:, i] for i in range(DEPTH)]
        else:
            P[n] = _unshard_last(g)
    c_all = unpacked[-1]
    P["ffn_conv_b"] = [ffn_conv_b[i].reshape(NDEV, 1, FFN_BLK) for i in range(DEPTH)]
    P["attn_sink"], P["gm_w_s"], P["gm_b_s"] = attn_sink, gm_w_s, gm_b_s

    cond = jnp.concatenate([c_all, c_ctx[None], jnp.zeros((7, D), F32)])[None]
    scond = _rw_fwd(f_silu, [(cond, 0)], [], [(D, BF16)], name="ada_silu")[0]
    ada_bf = ada_w.astype(BF16)
    ncol = ada_w.shape[2]
    mod_loc = _mm(scond, ada_bf, "nn", name="ada_proj")
    mod_loc = mod_loc + lax.dynamic_slice_in_dim(ada_b, me * ncol, ncol, axis=1)[:, None, :]
    mod_g = _all_gather([mod_loc], "gather_mod")[0]
    mod_full = mod_g.transpose(1, 2, 0, 3).reshape(DEPTH, 16, 6, D)
    lat_mod = lax.dynamic_index_in_dim(mod_full, me, axis=1, keepdims=False)
    ctx_mod = mod_full[:, NDEV]

    gathers, exchanges, pending = {}, {}, {}
    col_to_parts = lambda g: g[0].reshape(g.shape[1], NDEV, g.shape[2] // NDEV).transpose(1, 0, 2)
    row_to_parts = lambda g: g.reshape(NDEV, -1, g.shape[-1])
    no_order = jnp.zeros((8, 128), F32)

    def layer_weights(i, h):
        if i == 0:
            gathers["f0"] = _xfer_start("gather", local_bf16[0][2:], mod_g, "gather_params_f0")
            gathers[1] = _xfer_start("gather", local_bf16[1], gathers["f0"]["token"], "gather_params1")
            mats = gathered0[1:]
        else:
            mats = _xfer_wait(gathers[i], h)
            if i + 1 < DEPTH:
                gathers[i + 1] = _xfer_start("gather", local_bf16[i + 1], mats[0], f"gather_params{i + 1}")
        token = gathers[i + 1]["token"] if i + 1 < DEPTH else no_order
        a, ja, b, jb = layer_mats[i]
        pi = dict(P)
        pi[a] = col_to_full(mats[0]) if a.startswith(("attn", "gm")) else {ja: col_to_full(mats[0])}
        pi[b] = mats[1].reshape(-1, D) if b.startswith(("attn", "gm")) else {jb: mats[1].reshape(-1, D)}
        def ffn_weights(h_mid):
            up, down = mats[2:] if i else _xfer_wait(gathers["f0"], h_mid)
            pi["ffn_w_up"], pi["ffn_w_down"] = {i: up}, {i: down.reshape(FFN_PAIRS, FFN_BLK, D)}
            return _ffn_weights(i, pi)

        return _mixer_weights(i, pi), ffn_weights, token

    def grads_ready(tag, g):
        pending[tag] = g
        i = int(tag[1])
        col_name, row_name = {0: ("w_in", "w_out"), 1: ("w_qkv", "w_o"), 2: ("w_in", "w_out")}[i % 3]
        if tag == "f0":
            arrs = [g["up"], row_to_parts(g["down"])]
        elif tag == "m0":
            arrs = [col_to_parts(g[col_name]), row_to_parts(g[row_name])]
        elif tag[0] == "m":
            gf = pending[f"f{i}"]
            arrs = [col_to_parts(g[col_name]), row_to_parts(g[row_name]), gf["up"], row_to_parts(gf["down"])]
        else:
            return None
        exchanges[tag] = _xfer_start("scatter", arrs, no_order, "exchange_" + tag)
        if tag == "m2":
            gathers["w_s"] = _xfer_start("gather", [g["w_s"].reshape(GM_GROUPS * GM_CHUNK, GM_CHUNK)],
                                         exchanges[tag]["token"], "gather_gm_w_s")
            return gathers["w_s"]["token"]
        return exchanges[tag]["token"]

    loss_part, grad_x, G, dmod = _local_step(x, ctx, loss_target, lat_mod, ctx_mod, P["norm_g"], layer_weights,
                                             grads_ready)
    recv = {tag: _xfer_wait(exchanges[tag], grad_x) for tag in ("m3", "m2", "m1")}

    dmod_g = _all_gather([dmod], "gather_dmod")[0]
    dm_cols = lax.dynamic_slice_in_dim(dmod_g, me * ncol, ncol, axis=3)
    dm_ext = dm_cols.transpose(2, 1, 0, 3).reshape(DEPTH, 16, ncol)
    cond_ext = jnp.concatenate([c_all, jnp.broadcast_to(c_ctx[None], (NDEV, D))])[None]
    scond_ext = _rw_fwd(f_silu, [(cond_ext, 0)], [], [(D, BF16)], name="ada_silu_ext")[0]
    g_ada_w = _mm(scond_ext, dm_ext, "tn", name="ada_proj_bw")
    dsil = _mm(dm_ext, ada_bf, "nt", reduce_blocks=True, name="ada_proj_bx")
    dcc = _rw_bwd(f_silu_rows, [(jnp.zeros((1, NDEV, D), F32), 0)], [(c_ctx.reshape(1, 1, D), "one")],
                  [dsil[:, NDEV:]], name="ada_silu_b", param_grad=[0])[0]

    out = {}

    def put(name, res):
        out[name] = res

    d_norm_g = jnp.stack([jnp.stack([G[f"m{i}"]["ng0"], G[f"m{i}"]["ng1"], G[f"f{i}"]["ng2"], G[f"f{i}"]["ng3"]])
                          for i in range(DEPTH)]).reshape(DEPTH, 4, D)
    small_full = dict(
        norm_g=d_norm_g,
        cm_b_in=jnp.stack([G["m0"]["b_in"], G["m3"]["b_in"]]).reshape(2, 2 * D),
        cm_dw_w=jnp.stack([G["m0"]["dw_w"][0], G["m3"]["dw_w"][0]]),
        cm_dw_b=jnp.stack([G["m0"]["dw_b"], G["m3"]["dw_b"]]).reshape(2, D),
        cm_ln_g=jnp.stack([G["m0"]["ln_g"], G["m3"]["ln_g"]]).reshape(2, D),
        cm_ln_b=jnp.stack([G["m0"]["ln_b"], G["m3"]["ln_b"]]).reshape(2, D),
        cm_b_out=jnp.stack([G["m0"]["b_out"], G["m3"]["b_out"]]).reshape(2, D),
        gm_b_in=G["m2"]["b_in"].reshape(1, 2 * GM_W),
        gm_ln_g=G["m2"]["ln_g"].reshape(1, GM_W), gm_ln_b=G["m2"]["ln_b"].reshape(1, GM_W))
    by_dest = []
    for n, _ in SMALL:
        if n == "ffn_conv_w":
            by_dest.append(jnp.stack([G[f"f{i}"]["cw"] for i in range(DEPTH)], axis=1))
        else:
            by_dest.append(_shard_last(small_full[n]))
    small_send = jax.vmap(lambda *vs: _pack_rows(list(vs)))(*by_dest)
    small_recv = _all_to_all([[small_send]], "exchange_small")[0]

    def shard3(a):
        return a.reshape(a.shape[0], -1, a.shape[-1])

    small_local = lambda d_: _pack_rows([d_[n] for n, _ in SMALL])[None]
    res = _adamw(small_recv, small_local(W), small_local(M), small_local(V), "adamw_small")
    unp = [_unpack_rows(r[0], small_shapes) for r in res]
    for q, (n, _) in enumerate(SMALL):
        put(n, [unp[t][q] for t in range(4)])

    repl_names = ["c_ctx", "ffn_conv_b", "attn_sink", "gm_b_s"]
    repl_part = dict(
        c_ctx=dcc.reshape(D),
        ffn_conv_b=jnp.stack([G[f"f{i}"]["cb"].reshape(2 * 2816) for i in range(DEPTH)]),
        attn_sink=G["m1"]["sink"].reshape(1, N_Q),
        gm_b_s=G["m2"]["b_s"].reshape(1, GM_GROUPS, GM_CHUNK))
    w_s_parts = _xfer_wait(gathers["w_s"], grad_x)[0]
    flat_s = lambda a: a.reshape(1, GM_GROUPS * GM_CHUNK, GM_CHUNK)
    res = _adamw(w_s_parts[None], flat_s(gm_w_s), flat_s(m_gm_w_s), flat_s(v_gm_w_s), "adamw_gm_w_s")
    put("gm_w_s", [r.reshape(gm_w_s.shape) for r in res])
    repl_shapes = [W[n].shape for n in repl_names]
    repl_sent = _pack_rows([repl_part[n] for n in repl_names] + [loss_part.reshape(1)], mult=256)
    repl_g = _all_gather([repl_sent], "gather_repl")[0]
    loss = jnp.sum(repl_g.reshape(NDEV, -1)[:, sum(W[n].size for n in repl_names)])
    repl_local = lambda d_: _pack_rows([d_[n] for n in repl_names], mult=256)[None]
    res = _adamw(repl_g[None], repl_local(W), repl_local(M), repl_local(V), "adamw_repl")
    unp = [_unpack_rows(r[0], repl_shapes) for r in res]
    for q, n in enumerate(repl_names):
        put(n, [unp[t][q] for t in range(4)])

    def update_big(n, parts):
        turn = (lambda a: jnp.swapaxes(a, 1, 2)) if n == "ffn_w_up" else shard3
        res = _adamw(parts, turn(W[n]), turn(M[n]), turn(V[n]), "adamw_" + n)
        put(n, [(turn(r) if n == "ffn_w_up" else r).reshape(W[n].shape) for r in res])

    put("ada_w", _adamw(g_ada_w[:, None], ada_w, m_ada_w, v_ada_w, "adamw_ada_w"))
    ada_b_parts = dmod_g.reshape(1, 2 * NDEV, DEPTH, 6 * D)
    res = _adamw(ada_b_parts, ada_b[None], m_ada_b[None], v_ada_b[None], "adamw_ada_b")
    put("ada_b", [r[0] for r in res])
    early = dict(attn_w_qkv=[recv["m1"][0]], attn_w_o=[recv["m1"][1]], gm_w_in=[recv["m2"][0]],
                 gm_w_out=[recv["m2"][1]])
    for n, parts in early.items():
        update_big(n, parts)
    done_first = sum(out[n][1].reshape(-1)[:1024] for n in ["ada_w", "ada_b", "gm_w_in", "gm_w_out", "attn_w_qkv"])
    recv.update({tag: _xfer_wait(exchanges[tag], done_first) for tag in ("f0", "m0")})
    late = dict(
        ffn_w_up=[recv["f0"][0]] + [recv[f"m{i}"][2] for i in (1, 2, 3)],
        ffn_w_down=[recv["f0"][1]] + [recv[f"m{i}"][3] for i in (1, 2, 3)],
        cm_w_in=[recv["m0"][0], recv["m3"][0]], cm_w_out=[recv["m0"][1], recv["m3"][1]])
    for n, parts in late.items():
        update_big(n, parts)

    names = ["c_ctx", "ada_w", "ada_b", "norm_g", "ffn_w_up", "ffn_conv_w", "ffn_conv_b", "ffn_w_down", "cm_w_in",
             "cm_b_in", "cm_dw_w", "cm_dw_b", "cm_ln_g", "cm_ln_b", "cm_w_out", "cm_b_out", "attn_w_qkv",
             "attn_sink", "attn_w_o", "gm_w_in", "gm_b_in", "gm_ln_g", "gm_ln_b", "gm_w_s", "gm_b_s", "gm_w_out"]
    return (loss, grad_x, *[out[n][0] for n in names], *[out[n][1] for n in names],
            *[out[n][2] for n in names], *[out[n][3] for n in names])
```

```python
import functools

import numpy as np
import jax
import jax.numpy as jnp
from jax import lax
from jax.experimental import pallas as pl
from jax.experimental.pallas import tpu as pltpu

F32, BF16 = jnp.float32, jnp.bfloat16
MESH = pl.DeviceIdType.MESH
AXES = ("x", "y", "c")
NDEV = 8

D = 1024
L = 2048
LC = 256
TA = L + LC
DEPTH = 4
EPS = 1e-6
HEAD_DIM = 64
N_Q, N_KV, Q_PER_KV = 16, 4, 4
ATTN_BLOCK = 128
GRID_W = 64
ROPE_BASE = 10000.0
GM_W = 2048
GM_CHUNK = 128
GM_GROUPS = 16
FFN_BLK = 704
CM_K, FFN_K = 31, 3

ADAM_LR, ADAM_B1, ADAM_B2, ADAM_EPS, ADAM_WD, ADAM_STEP = 0.001, 0.9, 0.999, 1e-08, 0.01, 10

VMEM_LIMIT_V7X = 58 * 1024 * 1024
ROW_TILE_ELEMS = 256 * 1024
MM_TILE_BYTES = 4 * 1024 * 1024


def _cparams(sem=None):
    kw = dict(vmem_limit_bytes=VMEM_LIMIT_V7X)
    if sem is not None:
        kw["dimension_semantics"] = sem
    return pltpu.CompilerParams(**kw)


def _pick(n, cands):
    for c in cands:
        if n % c == 0:
            return c
    return n


def _as3(a):
    return a if a.ndim == 3 else a[None]


def _all_gather(arrs, name):
    n = len(arrs)

    def body(*refs):
        ins, outs = refs[:n], refs[n:2 * n]
        send_sems, recv_sems, local_sems = refs[2 * n:]
        x, y, c = lax.axis_index("x"), lax.axis_index("y"), lax.axis_index("c")
        me, sibling = (x, y, c), (x, y, 1 - c)
        chips = [(1 - x, y), (x, 1 - y), (1 - x, 1 - y)]

        def slot(a, p):
            return outs[a].at[4 * p[0] + 2 * p[1] + p[2]]

        def copy(a, k, block, to, src=None):
            return pltpu.make_async_remote_copy(
                src_ref=slot(a, block) if src is None else src, dst_ref=slot(a, block),
                send_sem=send_sems.at[a, k], recv_sem=recv_sems.at[a, k],
                device_id=to, device_id_type=MESH)

        mine = [pltpu.make_async_copy(ins[a], slot(a, me), local_sems.at[a]) for a in range(n)]
        for m in mine:
            m.start()
        first = []
        for a in range(n):
            first.append(copy(a, 0, me, sibling, src=ins[a]))
            first += [copy(a, 1 + j, me, (*chip, c), src=ins[a]) for j, chip in enumerate(chips)]
        for cp in first:
            cp.start()
        passed = []
        for j, chip in enumerate(chips):
            for a in range(n):
                copy(a, 1 + j, (*chip, c), me).wait_recv()
                p = copy(a, 4 + j, (*chip, c), sibling)
                p.start()
                passed.append(p)
        for a in range(n):
            copy(a, 0, sibling, me).wait_recv()
            for j, chip in enumerate(chips):
                copy(a, 4 + j, (*chip, 1 - c), me).wait_recv()
        for cp in first + passed:
            cp.wait_send()
        for m in mine:
            m.wait()

    any_spec = pl.BlockSpec(memory_space=pl.ANY)
    outs = pl.pallas_call(
        body, name=name,
        out_shape=[jax.ShapeDtypeStruct((NDEV,) + a.shape, a.dtype) for a in arrs],
        in_specs=[any_spec] * n, out_specs=[any_spec] * n,
        scratch_shapes=[pltpu.SemaphoreType.DMA((n, 7)), pltpu.SemaphoreType.DMA((n, 7)),
                        pltpu.SemaphoreType.DMA((n,))],
    )(*arrs)
    return list(outs)


def _all_to_all(groups, name):
    flat = [(gi, li, a) for gi, g in enumerate(groups) for li, a in enumerate(g)]
    n, ng = len(flat), len(groups)

    def body(*refs):
        ins, outs = refs[:n], refs[n:n + ng]
        send_sems, recv_sems, local_sems = refs[n + ng:]
        x, y, c = lax.axis_index("x"), lax.axis_index("y"), lax.axis_index("c")
        me = 4 * x + 2 * y + c
        copies = []
        for a, (gi, li, _) in enumerate(flat):
            loc = pltpu.make_async_copy(ins[a].at[me], outs[gi].at[li, me], local_sems.at[a])
            loc.start()
            copies.append(loc)
            for k in range(1, NDEV):
                px = 1 - x if (k >> 2) & 1 else x
                py = 1 - y if (k >> 1) & 1 else y
                pc = 1 - c if k & 1 else c
                cp = pltpu.make_async_remote_copy(
                    src_ref=ins[a].at[4 * px + 2 * py + pc], dst_ref=outs[gi].at[li, me],
                    send_sem=send_sems.at[a, k - 1], recv_sem=recv_sems.at[a, k - 1],
                    device_id=(px, py, pc), device_id_type=MESH)
                cp.start()
                copies.append(cp)
        for cp in copies:
            cp.wait()

    any_spec = pl.BlockSpec(memory_space=pl.ANY)
    outs = pl.pallas_call(
        body, name=name,
        out_shape=[jax.ShapeDtypeStruct((len(g),) + g[0].shape, g[0].dtype) for g in groups],
        in_specs=[any_spec] * n, out_specs=[any_spec] * ng,
        scratch_shapes=[pltpu.SemaphoreType.DMA((n, 7)), pltpu.SemaphoreType.DMA((n, 7)),
                        pltpu.SemaphoreType.DMA((n,))],
    )(*[a for _, _, a in flat])
    return list(outs)


HBM_SPEC = pl.BlockSpec(memory_space=pltpu.HBM)
SEM_SPEC = pl.BlockSpec(memory_space=pltpu.SEMAPHORE)
ANY_SPEC = pl.BlockSpec(memory_space=pl.ANY)
SPLIT_EFFECT = pltpu.SideEffectType.DATAFLOW_SIDE_EFFECTING


def _remote_copies(kind, ins, lands, send_sems, recv_sems):
    x, y, c = lax.axis_index("x"), lax.axis_index("y"), lax.axis_index("c")
    me = 4 * x + 2 * y + c
    out = []
    for a in range(len(ins)):
        for k in range(1, NDEV):
            px = 1 - x if (k >> 2) & 1 else x
            py = 1 - y if (k >> 1) & 1 else y
            pc = 1 - c if k & 1 else c
            src = ins[a] if kind == "gather" else ins[a].at[4 * px + 2 * py + pc]
            out.append(pltpu.make_async_remote_copy(
                src_ref=src, dst_ref=lands[a].at[me], send_sem=send_sems.at[a * (NDEV - 1) + k - 1],
                recv_sem=recv_sems.at[a * (NDEV - 1) + k - 1], device_id=(px, py, pc), device_id_type=MESH))
    return out


def _local_copies(kind, ins, lands, local_sems):
    me = 4 * lax.axis_index("x") + 2 * lax.axis_index("y") + lax.axis_index("c")
    return [pltpu.make_async_copy(ins[a] if kind == "gather" else ins[a].at[me], lands[a].at[me], local_sems.at[a])
            for a in range(len(ins))]


def _xfer_start(kind, arrs, after, name):
    n = len(arrs)
    lands = [lax.empty((NDEV,) + a.shape if kind == "gather" else a.shape, a.dtype) for a in arrs]

    def body(*refs):
        ins, lnd = refs[:n], refs[n:2 * n]
        send_sems, recv_sems, local_sems = refs[2 * n + 1:2 * n + 4]
        for cp in _remote_copies(kind, ins, lnd, send_sems, recv_sems) + _local_copies(kind, ins, lnd, local_sems):
            cp.start()
        refs[-1][...] = jnp.zeros_like(refs[-1])

    outs = pl.pallas_call(
        body, name=name,
        out_shape=(pltpu.SemaphoreType.DMA((n * (NDEV - 1),)), pltpu.SemaphoreType.DMA((n * (NDEV - 1),)),
                   pltpu.SemaphoreType.DMA((n,)),
                   *[pltpu.HBM(a.shape, a.dtype) for a in arrs + lands], jax.ShapeDtypeStruct((8, 128), F32)),
        in_specs=[HBM_SPEC] * (2 * n) + [ANY_SPEC],
        out_specs=(SEM_SPEC, SEM_SPEC, SEM_SPEC, *[HBM_SPEC] * (2 * n), pl.BlockSpec(memory_space=pltpu.VMEM)),
        input_output_aliases={a: 3 + a for a in range(2 * n)},
        compiler_params=pltpu.CompilerParams(has_side_effects=SPLIT_EFFECT),
    )(*[pltpu.with_memory_space_constraint(a, pltpu.HBM) for a in arrs + lands], after)
    return dict(kind=kind, n=n, sems=list(outs[:3]), bufs=list(outs[3:3 + 2 * n]), token=outs[-1], name=name)


def _xfer_wait(st, after):
    kind, n = st["kind"], st["n"]

    def body(*refs):
        ins, lnd = refs[:n], refs[n:2 * n]
        send_sems, recv_sems, local_sems = refs[2 * n:2 * n + 3]
        for cp in _remote_copies(kind, ins, lnd, send_sems, recv_sems):
            cp.wait_send()
            cp.wait_recv()
        for cp in _local_copies(kind, ins, lnd, local_sems):
            cp.wait()

    outs = pl.pallas_call(
        body, name=st["name"] + "_wait",
        out_shape=tuple(pltpu.HBM(b.shape, b.dtype) for b in st["bufs"]),
        in_specs=[HBM_SPEC] * (2 * n) + [SEM_SPEC] * 3 + [ANY_SPEC],
        out_specs=tuple([HBM_SPEC] * (2 * n)),
        input_output_aliases={a: a for a in range(2 * n)},
        compiler_params=pltpu.CompilerParams(has_side_effects=SPLIT_EFFECT),
    )(*st["bufs"], *st["sems"], after)
    return list(outs[n:])


def _mm(a, b, kind, *, name, out_dtype=F32, reduce_blocks=False):
    a, b = _as3(a), _as3(b)
    nba, nbb = a.shape[0], b.shape[0]
    nb = max(nba, nbb)
    assert nba in (1, nb) and nbb in (1, nb)
    if kind == "tn":
        t, m = a.shape[1:]
        n = b.shape[2]
        assert b.shape[1] == t and not reduce_blocks
        tm = m if m <= 1024 else _pick(m, (1024,))
        tn = n if n <= 1024 else _pick(n, (1024, 768, 512))
        tk = t if t * (tm + tn) * 2 <= MM_TILE_BYTES * 3 else _pick(t, (512, 768, 256))
        nred = t // tk
    else:
        m, k = a.shape[1:]
        n = b.shape[2] if kind == "nn" else b.shape[1]
        assert (b.shape[1] if kind == "nn" else b.shape[2]) == k
        tn = n if (n <= 1024 or k * n * 2 <= 2 * MM_TILE_BYTES) else _pick(n, (1024, 768, 512))
        tm = m
        for cand in (1024, 768, 512, 256):
            if m % cand == 0 and cand * tn * 4 <= MM_TILE_BYTES and cand * k * 2 <= MM_TILE_BYTES:
                tm = cand
                break
        nred = nb if reduce_blocks else 1
    nbo = 1 if reduce_blocks else nb

    def blk(nbx, g, r):
        if nbx == 1:
            return 0
        return r if reduce_blocks else g

    if kind == "nn":
        a_spec = pl.BlockSpec((1, tm, k), lambda g, j, i, r: (blk(nba, g, r), i, 0))
        b_spec = pl.BlockSpec((1, k, tn), lambda g, j, i, r: (blk(nbb, g, r), 0, j))
        dims = (((1,), (0,)), ((), ()))
    elif kind == "nt":
        a_spec = pl.BlockSpec((1, tm, k), lambda g, j, i, r: (blk(nba, g, r), i, 0))
        b_spec = pl.BlockSpec((1, tn, k), lambda g, j, i, r: (blk(nbb, g, r), j, 0))
        dims = (((1,), (1,)), ((), ()))
    else:
        a_spec = pl.BlockSpec((1, tk, tm), lambda g, j, i, r: (blk(nba, g, r), r, i))
        b_spec = pl.BlockSpec((1, tk, tn), lambda g, j, i, r: (blk(nbb, g, r), r, j))
        dims = (((0,), (0,)), ((), ()))
    o_spec = pl.BlockSpec((1, tm, tn), lambda g, j, i, r: (g, i, j))

    def body(a_ref, b_ref, o_ref, *scratch):
        prod = lax.dot_general(a_ref[0].astype(BF16), b_ref[0].astype(BF16), dims,
                               preferred_element_type=F32)
        if nred == 1:
            o_ref[0] = prod.astype(o_ref.dtype)
        else:
            acc = scratch[0]
            r = pl.program_id(3)

            @pl.when(r == 0)
            def _():
                acc[...] = prod

            @pl.when(r > 0)
            def _():
                acc[...] += prod

            @pl.when(r == nred - 1)
            def _():
                o_ref[0] = acc[...].astype(o_ref.dtype)

    return pl.pallas_call(
        body, name=name,
        out_shape=jax.ShapeDtypeStruct((nbo, m, n), out_dtype),
        grid=(nbo, n // tn, m // tm, nred),
        in_specs=[a_spec, b_spec], out_specs=o_spec,
        scratch_shapes=[pltpu.VMEM((tm, tn), F32)] if nred > 1 else [],
        compiler_params=_cparams(("parallel", "parallel", "parallel", "arbitrary")),
    )(a, b)


def _row_tile(t, widths):
    tm = max(16, ROW_TILE_ELEMS // max(widths))
    tm = min(tm, 256)
    return t if t < tm else tm


def _sel_index(sel, g, i, tm):
    if sel == "one":
        return 0
    if sel == "seg":
        return (i * tm) // L
    return g + sel


def _row_spec(arr, off, tm):
    return pl.BlockSpec((1, tm, arr.shape[2]), lambda g, i: (g + off, i, 0))


def _par_spec(arr, sel, tm):
    return pl.BlockSpec((1, 1, arr.shape[2]), lambda g, i: (_sel_index(sel, g, i, tm), 0, 0))


def _norm_ops(ops):
    return [(o[0], o[1], o[2] if len(o) > 2 else 1) for o in ops]


def _split_cols(vals, nsplit):
    out = []
    for v, ns in zip(vals, nsplit):
        w = v.shape[1] // ns
        out += [v] if ns == 1 else [v[:, q * w:(q + 1) * w] for q in range(ns)]
    return out


def _join_cols(flat, nsplit):
    out, pos = [], 0
    for ns in nsplit:
        out.append(flat[pos] if ns == 1 else jnp.concatenate(flat[pos:pos + ns], axis=1))
        pos += ns
    return out


def _rw_fwd(fn, rows, params, outs, *, name, nblk=None):
    rows, params = _norm_ops(rows), _norm_ops(params)
    t = rows[0][0].shape[1]
    nblk = nblk or rows[0][0].shape[0]
    tm = _row_tile(t, [r.shape[2] for r, _, _ in rows] + [w for w, _ in outs])
    nr, npar = len(rows), len(params)
    nsplit = [ns for _, _, ns in rows + params]

    def body(*refs):
        vals = _split_cols([r[0].astype(F32) for r in refs[:nr + npar]], nsplit)
        res = fn(*vals)
        for o_ref, o in zip(refs[nr + npar:], res):
            o_ref[0] = o.astype(o_ref.dtype)

    res = pl.pallas_call(
        body, name=name,
        out_shape=[jax.ShapeDtypeStruct((nblk, t, w), dt) for w, dt in outs],
        grid=(nblk, t // tm),
        in_specs=[_row_spec(r, off, tm) for r, off, _ in rows] + [_par_spec(p, s, tm) for p, s, _ in params],
        out_specs=[pl.BlockSpec((1, tm, w), lambda g, i: (g, i, 0)) for w, _ in outs],
        compiler_params=_cparams(("parallel", "parallel")),
    )(*[r for r, _, _ in rows], *[p for p, _, _ in params])
    return list(res)


def _rw_bwd(fn, rows, params, cts, *, name, row_grad=(), param_grad=(), add=None, nblk=None):
    rows, params = _norm_ops(rows), _norm_ops(params)
    t = cts[0].shape[1]
    nblk = nblk or cts[0].shape[0]
    tm = _row_tile(t, [r.shape[2] for r, _, _ in rows] + [c.shape[2] for c in cts])
    ni = t // tm
    nr, npar, nct = len(rows), len(params), len(cts)
    nadd = 0 if add is None else 1
    n_in = nr + npar + nct + nadd
    nsplit = [ns for _, _, ns in rows + params]

    def body(*refs):
        prim = _split_cols([r[0].astype(F32) for r in refs[:nr + npar]], nsplit)
        ct = tuple(r[0].astype(F32) for r in refs[nr + npar:nr + npar + nct])
        _, vjp = jax.vjp(fn, *prim)
        grads = _join_cols(list(vjp(ct)), nsplit)
        out_refs = refs[n_in:]
        for q, (ri, _) in enumerate(row_grad):
            gr = grads[ri]
            if q == 0 and nadd:
                gr = gr + refs[n_in - 1][0].astype(F32)
            out_refs[q][0] = gr.astype(out_refs[q].dtype)
        g, i = pl.program_id(0), pl.program_id(1)
        step = g * ni + i
        pg, pi = (step - 1) // ni, (step - 1) % ni
        for q, pidx in enumerate(param_grad):
            o_ref = out_refs[len(row_grad) + q]
            sel = params[pidx][1]
            val = grads[nr + pidx]
            if sel == "one":
                first = step == 0
            else:
                first = (step == 0) | (_sel_index(sel, g, i, tm) != _sel_index(sel, pg, pi, tm))

            @pl.when(first)
            def _(o_ref=o_ref, val=val):
                o_ref[0] = val

            @pl.when(jnp.logical_not(first))
            def _(o_ref=o_ref, val=val):
                o_ref[0] += val

    in_arrays = [r for r, _, _ in rows] + [p for p, _, _ in params] + list(cts) + ([add] if nadd else [])
    in_specs = ([_row_spec(r, off, tm) for r, off, _ in rows] + [_par_spec(p, s, tm) for p, s, _ in params]
                + [_row_spec(c, 0, tm) for c in cts] + ([_row_spec(add, 0, tm)] if nadd else []))
    out_shape, out_specs = [], []
    for ri, dt in row_grad:
        w = rows[ri][0].shape[2]
        out_shape.append(jax.ShapeDtypeStruct((nblk, t, w), dt))
        out_specs.append(pl.BlockSpec((1, tm, w), lambda g, i: (g, i, 0)))
    for pidx in param_grad:
        p, sel, _ = params[pidx]
        out_shape.append(jax.ShapeDtypeStruct(p.shape, F32))
        out_specs.append(_par_spec(p, sel, tm))
    res = pl.pallas_call(
        body, name=name, out_shape=out_shape, grid=(nblk, ni),
        in_specs=in_specs, out_specs=out_specs,
        compiler_params=_cparams(("arbitrary", "arbitrary")),
    )(*in_arrays)
    return list(res)


def _sigmoid(x):
    return 1.0 / (1.0 + jnp.exp(-x))


def _rms(x, g):
    return x * lax.rsqrt(jnp.mean(x * x, axis=-1, keepdims=True) + EPS) * g


def _ln(x, g, b):
    mu = jnp.mean(x, axis=-1, keepdims=True)
    xc = x - mu
    var = jnp.mean(xc * xc, axis=-1, keepdims=True)
    return xc * lax.rsqrt(var + EPS) * g + b


def _gelu_tanh(x):
    return 0.5 * x * (1.0 + jnp.tanh(0.7978845608028654 * (x + 0.044715 * (x * x * x))))


def f_modnorm(h, g, sc, sh):
    return (_rms(h, g) * (1.0 + sc) + sh,)


def f_gate_rms(y, gate, g):
    return (gate * _rms(y, g),)


def f_gate_rms_bias(y, gate, g, b):
    return (gate * _rms(y + b, g),)


def f_resgate(h, y, gate, g):
    return (h + gate * _rms(y, g),)


def f_resgate_bias(h, y, gate, g, b):
    return (h + gate * _rms(y + b, g),)


def f_glu(pa, pg, ba, bg):
    return ((pa + ba) * _sigmoid(pg + bg),)


def f_lnsilu(z, g, b):
    t = _ln(z, g, b)
    return (t * _sigmoid(t),)


def f_gmlp_pre(pu, pv, bu, bv, g, bb):
    return _gelu_tanh(pu + bu), _ln(_gelu_tanh(pv + bv), g, bb)


def f_ffn_gate(zg, zv):
    return (zg * _sigmoid(zg) * zv,)


def f_silu(x):
    return (x * _sigmoid(x),)


def f_silu_rows(dummy, cc):
    return (cc * _sigmoid(cc) + 0.0 * dummy,)


def _rope(x_in, tables, neg_sin, out_dtype, name, cols=None):
    w, ci = cols if cols else (x_in.shape[2], 0)
    sign = -1.0 if neg_sin else 1.0
    tm = 256

    def body(x_ref, cos_ref, sin_ref, o_ref):
        x = x_ref[0].astype(F32)
        cos = jnp.tile(cos_ref[0], (1, w // 128))
        sin = jnp.tile(sin_ref[0], (1, w // 128)) * sign
        lane = lax.broadcasted_iota(jnp.int32, x.shape, 1) & 31
        rot = jnp.where(lane < 16, -pltpu.roll(x, w - 16, 1), pltpu.roll(x, 16, 1))
        o_ref[0] = (x * cos + rot * sin).astype(o_ref.dtype)

    tspec = pl.BlockSpec((1, tm, 128), lambda i: (0, i, 0))
    return pl.pallas_call(
        body, name=name, out_shape=jax.ShapeDtypeStruct((1, L, w), out_dtype), grid=(L // tm,),
        in_specs=[pl.BlockSpec((1, tm, w), lambda i: (0, i, ci)), tspec, tspec],
        out_specs=pl.BlockSpec((1, tm, w), lambda i: (0, i, 0)),
        compiler_params=_cparams(("parallel",)),
    )(x_in, tables[0], tables[1])


CONV_TM = 256
CONV_RC = 8


class _ShiftedRows:
    def __init__(self, xp, cb):
        self.xp, self.memo = xp, {}
        self.row = lax.broadcasted_iota(jnp.int32, (CONV_RC, cb), 0)

    def _get(self, key, make):
        if key not in self.memo:
            self.memo[key] = make()
        return self.memo[key]

    def chunk(self, a):
        return self._get(("c", a), lambda: self.xp[a:a + CONV_RC, :])

    def rot(self, a, j):
        return self._get(("r", a, j), lambda: pltpu.roll(self.chunk(a), CONV_RC - j, 0))

    def at(self, a):
        q, j = divmod(a, CONV_RC)
        if j == 0:
            return self.chunk(a)
        low = self._get(("m", j), lambda: self.row < CONV_RC - j)
        return jnp.where(low, self.rot(q * CONV_RC, j), self.rot(q * CONV_RC + CONV_RC, j))


def _conv_geometry(x, k):
    nb, t, w = x.shape
    halo = 16 if k > 17 else 8
    cb = _pick(w, (512,)) if w > 768 else w
    return nb, t, w, halo, cb, (k - 1) // 2


def _conv_in_specs(t, halo, cb):
    per = CONV_TM // halo
    last = t // halo - 1
    return [
        pl.BlockSpec((1, CONV_TM, cb), lambda g, jc, i: (g, i, jc)),
        pl.BlockSpec((1, halo, cb), lambda g, jc, i: (g, jnp.maximum(i * per - 1, 0), jc)),
        pl.BlockSpec((1, halo, cb), lambda g, jc, i: (g, jnp.minimum((i + 1) * per, last), jc)),
    ]


def _conv_fill(xp, x_ref, prev_ref, next_ref, halo, t):
    i = pl.program_id(2)
    seg_first = (i * CONV_TM == 0) | (i * CONV_TM == L)
    seg_last = ((i + 1) * CONV_TM == L) | ((i + 1) * CONV_TM == t)
    xp[0:halo, :] = jnp.where(seg_first, 0.0, prev_ref[0].astype(F32))
    xp[halo:halo + CONV_TM, :] = x_ref[0].astype(F32)
    xp[halo + CONV_TM:, :] = jnp.where(seg_last, 0.0, next_ref[0].astype(F32))


def _dwconv(x, w, b, *, name, out_dtype=F32):
    k = w.shape[1]
    nb, t, wd, halo, cb, half = _conv_geometry(x, k)
    base = halo - half

    def body(*refs):
        x_ref, prev_ref, next_ref, w_ref = refs[:4]
        b_ref = refs[4] if b is not None else None
        o_ref, xp = refs[-2], refs[-1]
        _conv_fill(xp, x_ref, prev_ref, next_ref, halo, t)
        rows = _ShiftedRows(xp, cb)
        for o in range(0, CONV_TM, CONV_RC):
            acc = None
            for kk in range(k):
                term = w_ref[0, kk:kk + 1, :] * rows.at(o + base + kk)
                acc = term if acc is None else acc + term
            if b_ref is not None:
                acc = acc + b_ref[0]
            o_ref[0, o:o + CONV_RC, :] = acc.astype(o_ref.dtype)

    in_specs = _conv_in_specs(t, halo, cb) + [pl.BlockSpec((1, k, cb), lambda g, jc, i: (g, 0, jc))]
    args = [x, x, x, w]
    if b is not None:
        in_specs.append(pl.BlockSpec((1, 1, cb), lambda g, jc, i: (g, 0, jc)))
        args.append(b)
    return pl.pallas_call(
        body, name=name, out_shape=jax.ShapeDtypeStruct((nb, t, wd), out_dtype),
        grid=(nb, wd // cb, t // CONV_TM), in_specs=in_specs,
        out_specs=pl.BlockSpec((1, CONV_TM, cb), lambda g, jc, i: (g, i, jc)),
        scratch_shapes=[pltpu.VMEM((CONV_TM + 2 * halo, cb), F32)],
        compiler_params=_cparams(("parallel", "parallel", "parallel")),
    )(*args)


def _dwconv_wgrad(x, dy, k, *, name):
    nb, t, wd, halo, cb, half = _conv_geometry(x, k)
    base = halo - half

    def body(x_ref, prev_ref, next_ref, dy_ref, dw_ref, db_ref, xp):
        _conv_fill(xp, x_ref, prev_ref, next_ref, halo, t)
        i = pl.program_id(2)

        @pl.when(i == 0)
        def _():
            dw_ref[...] = jnp.zeros_like(dw_ref)
            db_ref[...] = jnp.zeros_like(db_ref)

        rows = _ShiftedRows(xp, cb)
        dys = [dy_ref[0, o:o + CONV_RC, :].astype(F32) for o in range(0, CONV_TM, CONV_RC)]
        db_ref[0] += jnp.sum(sum(dys[1:], dys[0]), axis=0, keepdims=True)
        for kk in range(k):
            acc = None
            for ci, dyc in enumerate(dys):
                term = dyc * rows.at(ci * CONV_RC + base + kk)
                acc = term if acc is None else acc + term
            dw_ref[0, kk:kk + 1, :] += jnp.sum(acc, axis=0, keepdims=True)

    dw, db = pl.pallas_call(
        body, name=name,
        out_shape=[jax.ShapeDtypeStruct((nb, k, wd), F32), jax.ShapeDtypeStruct((nb, 1, wd), F32)],
        grid=(nb, wd // cb, t // CONV_TM),
        in_specs=_conv_in_specs(t, halo, cb) + [pl.BlockSpec((1, CONV_TM, cb), lambda g, jc, i: (g, i, jc))],
        out_specs=[pl.BlockSpec((1, k, cb), lambda g, jc, i: (g, 0, jc)),
                   pl.BlockSpec((1, 1, cb), lambda g, jc, i: (g, 0, jc))],
        scratch_shapes=[pltpu.VMEM((CONV_TM + 2 * halo, cb), F32)],
        compiler_params=_cparams(("parallel", "parallel", "arbitrary")),
    )(x, x, x, dy)
    return dw, db


ATTN_SCALE = HEAD_DIM ** -0.5
QROWS = Q_PER_KV * ATTN_BLOCK
NEG = -1e30


def _attn_bias():
    qi = np.arange(ATTN_BLOCK)[:, None] + ATTN_BLOCK
    kj = np.arange(3 * ATTN_BLOCK)[None, :]
    near = np.abs(qi - kj) <= ATTN_BLOCK
    valid = np.stack([kj >= ATTN_BLOCK, kj >= 0, kj < 2 * ATTN_BLOCK])
    return jnp.asarray(np.where(near[None] & valid, 0.0, NEG), F32)


def _attn_scores(q, kw, kc, bias):
    nt = (((1,), (1,)), ((), ()))
    s_w = lax.dot_general(q, kw, nt, preferred_element_type=F32) * ATTN_SCALE
    s_w = (s_w.reshape(Q_PER_KV, ATTN_BLOCK, 3 * ATTN_BLOCK) + bias[None]).reshape(QROWS, 3 * ATTN_BLOCK)
    s_c = lax.dot_general(q, kc, nt, preferred_element_type=F32) * ATTN_SCALE
    return s_w, s_c


def _sink_col(sink_ref, hk):
    return jnp.concatenate([jnp.full((ATTN_BLOCK, 1), sink_ref[hk * Q_PER_KV + g], F32) for g in range(Q_PER_KV)], axis=0)


def _attn_specs():
    qspec = pl.BlockSpec((Q_PER_KV, ATTN_BLOCK, HEAD_DIM), lambda hk, n: (hk, n, 0))
    kspec = pl.BlockSpec((1, L + 2 * ATTN_BLOCK, HEAD_DIM), lambda hk, n: (hk, 0, 0))
    cspec = pl.BlockSpec((1, LC, HEAD_DIM), lambda hk, n: (hk, 0, 0))
    lspec = pl.BlockSpec((Q_PER_KV, ATTN_BLOCK, 1), lambda hk, n: (hk, n, 0))
    sspec = pl.BlockSpec(memory_space=pltpu.SMEM)
    last = L // ATTN_BLOCK - 1
    bspec = pl.BlockSpec((1, ATTN_BLOCK, 3 * ATTN_BLOCK),
                         lambda hk, n: (jnp.where(n == 0, 0, jnp.where(n == last, 2, 1)), 0, 0))
    return qspec, kspec, cspec, lspec, sspec, bspec


def _attn_fwd(q, k, v, kc, vc, sink):
    qspec, kspec, cspec, lspec, sspec, bspec = _attn_specs()

    def body(bias_ref, q_ref, k_ref, v_ref, kc_ref, vc_ref, sink_ref, o_ref, lse_ref):
        hk, n = pl.program_id(0), pl.program_id(1)
        qv = q_ref[...].reshape(QROWS, HEAD_DIM)
        start = pl.multiple_of(n * ATTN_BLOCK, ATTN_BLOCK)
        kw = k_ref[0, pl.ds(start, 3 * ATTN_BLOCK), :]
        vw = v_ref[0, pl.ds(start, 3 * ATTN_BLOCK), :]
        s_w, s_c = _attn_scores(qv, kw, kc_ref[0], bias_ref[0])
        sk = _sink_col(sink_ref, hk)
        m = jnp.maximum(jnp.maximum(jnp.max(s_w, -1, keepdims=True), jnp.max(s_c, -1, keepdims=True)), sk)
        p_w, p_c = jnp.exp(s_w - m), jnp.exp(s_c - m)
        den = jnp.sum(p_w, -1, keepdims=True) + jnp.sum(p_c, -1, keepdims=True) + jnp.exp(sk - m)
        o = (jnp.dot(p_w.astype(BF16), vw, preferred_element_type=F32)
             + jnp.dot(p_c.astype(BF16), vc_ref[0], preferred_element_type=F32)) / den
        o_ref[...] = o.reshape(Q_PER_KV, ATTN_BLOCK, HEAD_DIM).astype(o_ref.dtype)
        lse_ref[...] = (m + jnp.log(den)).reshape(Q_PER_KV, ATTN_BLOCK, 1)

    return pl.pallas_call(
        body, name="attn_fwd",
        out_shape=[jax.ShapeDtypeStruct((N_Q, L, HEAD_DIM), BF16), jax.ShapeDtypeStruct((N_Q, L, 1), F32)],
        grid=(N_KV, L // ATTN_BLOCK),
        in_specs=[bspec, qspec, kspec, kspec, cspec, cspec, sspec], out_specs=[qspec, lspec],
        compiler_params=_cparams(("parallel", "parallel")),
    )(_attn_bias(), q, k, v, kc, vc, sink)


def _attn_bwd(q, k, v, kc, vc, sink, o, lse, do):
    qspec, kspec, cspec, lspec, sspec, bspec = _attn_specs()
    tn = (((0,), (0,)), ((), ()))
    nt = (((1,), (1,)), ((), ()))

    def body(bias_ref, q_ref, k_ref, v_ref, kc_ref, vc_ref, sink_ref, o_ref, lse_ref, do_ref,
             dq_ref, dk_ref, dv_ref, dkc_ref, dvc_ref, dsink_ref):
        hk, n = pl.program_id(0), pl.program_id(1)
        qv = q_ref[...].reshape(QROWS, HEAD_DIM)
        start = pl.multiple_of(n * ATTN_BLOCK, ATTN_BLOCK)
        win = pl.ds(start, 3 * ATTN_BLOCK)
        kw, vw = k_ref[0, win, :], v_ref[0, win, :]
        kcv, vcv = kc_ref[0], vc_ref[0]
        s_w, s_c = _attn_scores(qv, kw, kcv, bias_ref[0])
        lse_v = lse_ref[...].reshape(QROWS, 1)
        p_w, p_c = jnp.exp(s_w - lse_v), jnp.exp(s_c - lse_v)
        dov = do_ref[...].reshape(QROWS, HEAD_DIM).astype(F32)
        ov = o_ref[...].reshape(QROWS, HEAD_DIM).astype(F32)
        delta = jnp.sum(dov * ov, -1, keepdims=True)
        dob = dov.astype(BF16)
        dp_w = lax.dot_general(dob, vw, nt, preferred_element_type=F32)
        dp_c = lax.dot_general(dob, vcv, nt, preferred_element_type=F32)
        ds_w = (p_w * (dp_w - delta) * ATTN_SCALE).astype(BF16)
        ds_c = (p_c * (dp_c - delta) * ATTN_SCALE).astype(BF16)
        dq = jnp.dot(ds_w, kw, preferred_element_type=F32) + jnp.dot(ds_c, kcv, preferred_element_type=F32)
        dq_ref[...] = dq.reshape(Q_PER_KV, ATTN_BLOCK, HEAD_DIM)

        @pl.when(n == 0)
        def _():
            dk_ref[...] = jnp.zeros_like(dk_ref)
            dv_ref[...] = jnp.zeros_like(dv_ref)
            dkc_ref[...] = jnp.zeros_like(dkc_ref)
            dvc_ref[...] = jnp.zeros_like(dvc_ref)

        dk_ref[0, win, :] += lax.dot_general(ds_w, qv, tn, preferred_element_type=F32)
        dv_ref[0, win, :] += lax.dot_general(p_w.astype(BF16), dob, tn, preferred_element_type=F32)
        dkc_ref[0] += lax.dot_general(ds_c, qv, tn, preferred_element_type=F32)
        dvc_ref[0] += lax.dot_general(p_c.astype(BF16), dob, tn, preferred_element_type=F32)
        dsk = -jnp.exp(_sink_col(sink_ref, hk) - lse_v) * delta
        for g in range(Q_PER_KV):
            part = jnp.sum(dsk[g * ATTN_BLOCK:(g + 1) * ATTN_BLOCK])
            idx = hk * Q_PER_KV + g

            @pl.when(n == 0)
            def _(part=part, idx=idx):
                dsink_ref[idx] = part

            @pl.when(n > 0)
            def _(part=part, idx=idx):
                dsink_ref[idx] += part

    kshape = jax.ShapeDtypeStruct((N_KV, L + 2 * ATTN_BLOCK, HEAD_DIM), F32)
    cshape = jax.ShapeDtypeStruct((N_KV, LC, HEAD_DIM), F32)
    return pl.pallas_call(
        body, name="attn_bwd",
        out_shape=[jax.ShapeDtypeStruct((N_Q, L, HEAD_DIM), F32), kshape, kshape, cshape, cshape,
                   jax.ShapeDtypeStruct((N_Q,), F32)],
        grid=(N_KV, L // ATTN_BLOCK),
        in_specs=[bspec, qspec, kspec, kspec, cspec, cspec, sspec, qspec, lspec, qspec],
        out_specs=[qspec, kspec, kspec, cspec, cspec, sspec],
        compiler_params=_cparams(("arbitrary", "arbitrary")),
    )(_attn_bias(), q, k, v, kc, vc, sink, o, lse, do)


def _gm_specs():
    rspec = pl.BlockSpec((1, GM_CHUNK, GM_W), lambda n: (0, n, 0))
    wspec = pl.BlockSpec((GM_GROUPS, GM_CHUNK, GM_CHUNK), lambda n: (0, 0, 0))
    bspec = pl.BlockSpec((GM_GROUPS, GM_CHUNK, 1), lambda n: (0, 0, 0))
    return rspec, wspec, bspec


def _gm_spatial_fwd(u, v, ws, bs):
    rspec, wspec, bspec = _gm_specs()

    def body(u_ref, v_ref, ws_ref, bs_ref, o_ref):
        for g in range(GM_GROUPS):
            cols = slice(g * GM_CHUNK, (g + 1) * GM_CHUNK)
            s = jnp.dot(ws_ref[g], v_ref[0, :, cols], preferred_element_type=F32) + bs_ref[g]
            o_ref[0, :, cols] = (u_ref[0, :, cols] * s).astype(o_ref.dtype)

    return pl.pallas_call(
        body, name="gm_spatial_fwd", out_shape=jax.ShapeDtypeStruct((1, L, GM_W), BF16),
        grid=(L // GM_CHUNK,), in_specs=[rspec, rspec, wspec, bspec], out_specs=rspec,
        compiler_params=_cparams(("parallel",)),
    )(u, v, ws, bs)


def _gm_spatial_bwd(u, v, ws, bs, dus):
    rspec, wspec, bspec = _gm_specs()
    tn = (((0,), (0,)), ((), ()))
    nt = (((1,), (1,)), ((), ()))

    def body(u_ref, v_ref, ws_ref, bs_ref, d_ref, du_ref, dv_ref, dws_ref, dbs_ref):
        n = pl.program_id(0)

        @pl.when(n == 0)
        def _():
            dws_ref[...] = jnp.zeros_like(dws_ref)
            dbs_ref[...] = jnp.zeros_like(dbs_ref)

        for g in range(GM_GROUPS):
            cols = slice(g * GM_CHUNK, (g + 1) * GM_CHUNK)
            vb = v_ref[0, :, cols]
            s = jnp.dot(ws_ref[g], vb, preferred_element_type=F32) + bs_ref[g]
            d = d_ref[0, :, cols].astype(F32)
            du_ref[0, :, cols] = d * s
            ds = d * u_ref[0, :, cols]
            dsb = ds.astype(BF16)
            dv_ref[0, :, cols] = lax.dot_general(ws_ref[g], dsb, tn, preferred_element_type=F32)
            dws_ref[g] += lax.dot_general(dsb, vb, nt, preferred_element_type=F32)
            dbs_ref[g] += jnp.sum(ds, axis=1, keepdims=True)

    row = jax.ShapeDtypeStruct((1, L, GM_W), F32)
    return pl.pallas_call(
        body, name="gm_spatial_bwd",
        out_shape=[row, row, jax.ShapeDtypeStruct((GM_GROUPS, GM_CHUNK, GM_CHUNK), F32),
                   jax.ShapeDtypeStruct((GM_GROUPS, GM_CHUNK, 1), F32)],
        grid=(L // GM_CHUNK,), in_specs=[rspec, rspec, wspec, bspec, rspec],
        out_specs=[rspec, rspec, wspec, bspec],
        compiler_params=_cparams(("arbitrary",)),
    )(u, v, ws, bs, dus)


def _loss_head(h, target):
    tm = 256

    def body(h_ref, t_ref, dh_ref, loss_ref):
        d = h_ref[0] - t_ref[0]
        dh_ref[0] = d * (1.0 / D)

        @pl.when(pl.program_id(0) == 0)
        def _():
            loss_ref[...] = jnp.zeros_like(loss_ref)

        loss_ref[...] += jnp.sum(d * d) * (0.5 / D)

    spec = pl.BlockSpec((1, tm, D), lambda i: (0, i, 0))
    dh, loss = pl.pallas_call(
        body, name="loss_head",
        out_shape=[jax.ShapeDtypeStruct((1, L, D), F32), jax.ShapeDtypeStruct((8, 128), F32)],
        grid=(L // tm,), in_specs=[spec, spec],
        out_specs=[spec, pl.BlockSpec((8, 128), lambda i: (0, 0))],
        compiler_params=_cparams(("arbitrary",)),
    )(h, target)
    return dh, loss[0, 0]


def _adamw(parts, w, m, v, name):
    per_layer = isinstance(parts, (list, tuple))
    plist = list(parts) if per_layer else [parts]
    nl = len(plist) if per_layer else parts.shape[0]
    s, r, c = plist[0].shape[-3:]
    tr = r
    for cand in (512, 256, 128, 64, 32, 16):
        if r % cand == 0 and cand * c <= 131072:
            tr = cand
            break
    nr = r // tr
    npart = len(plist)
    c1 = 1.0 / (1.0 - ADAM_B1 ** ADAM_STEP)
    c2 = 1.0 / (1.0 - ADAM_B2 ** ADAM_STEP)

    def body(*refs):
        w_ref, m_ref, v_ref, g_ref, d_ref, nm_ref, nv_ref = refs[npart:]

        def update(read):
            g = read(0).astype(F32)
            for q in range(1, s):
                g = g + read(q).astype(F32)
            mn = ADAM_B1 * m_ref[0] + (1.0 - ADAM_B1) * g
            vn = ADAM_B2 * v_ref[0] + (1.0 - ADAM_B2) * (g * g)
            g_ref[0] = g
            nm_ref[0] = mn
            nv_ref[0] = vn
            d_ref[0] = -ADAM_LR * ((mn * c1) / (jnp.sqrt(vn * c2) + ADAM_EPS) + ADAM_WD * w_ref[0])

        if not per_layer:
            update(lambda q: refs[0][0, q])
        else:
            for l in range(nl):
                @pl.when(pl.program_id(0) == l)
                def _(l=l):
                    update(lambda q: refs[l][q])

    spec = pl.BlockSpec((1, tr, c), lambda li, i: (li, i, 0))
    shp = jax.ShapeDtypeStruct((nl, r, c), F32)
    if per_layer:
        pspecs = [pl.BlockSpec((s, tr, c), lambda li, i, l=l: (0, jnp.where(li == l, i, jnp.where(li > l, nr - 1, 0)), 0))
                  for l in range(nl)]
    else:
        pspecs = [pl.BlockSpec((1, s, tr, c), lambda li, i: (li, 0, i, 0))]
    return pl.pallas_call(
        body, name=name, out_shape=[shp] * 4, grid=(nl, nr),
        in_specs=pspecs + [spec, spec, spec], out_specs=[spec] * 4,
        compiler_params=_cparams(("arbitrary", "arbitrary")),
    )(*plist, w, m, v)


def _pack_rows(vecs, lanes=128, mult=8):
    flat = jnp.concatenate([v.reshape(-1) for v in vecs])
    n = flat.shape[0]
    rows = -(-n // (mult * lanes)) * mult
    return jnp.pad(flat, (0, rows * lanes - n)).reshape(rows, lanes)


def _unpack_rows(packed, shapes):
    flat = packed.reshape(-1)
    out, pos = [], 0
    for s in shapes:
        n = 1
        for d_ in s:
            n *= d_
        out.append(flat[pos:pos + n].reshape(s))
        pos += n
    return out


def _unshard_last(g):
    lead = g.shape[1:-1]
    return jnp.moveaxis(g, 0, -2).reshape(*lead, NDEV * g.shape[-1])


def _shard_last(full):
    lead, w = full.shape[:-1], full.shape[-1] // NDEV
    return jnp.moveaxis(full.reshape(*lead, NDEV, w), -2, 0)


def _rope_tables():
    rows = L // GRID_W
    row = jnp.repeat(jnp.arange(rows), GRID_W).astype(F32)
    col = jnp.tile(jnp.arange(GRID_W), rows).astype(F32)
    axis_dim = HEAD_DIM // 2
    inv_freq = ROPE_BASE ** (-jnp.arange(0, axis_dim, 2, dtype=F32) / axis_dim)
    ang_r, ang_c = row[:, None] * inv_freq[None, :], col[:, None] * inv_freq[None, :]
    ang = jnp.concatenate([ang_r, ang_r, ang_c, ang_c], axis=-1)
    ang = jnp.concatenate([ang, ang], axis=-1)[None]
    return jnp.cos(ang), jnp.sin(ang)


def _heads(x, nh):
    t = x.shape[1]
    return x.reshape(t, nh, HEAD_DIM).transpose(1, 0, 2)


def _unheads(x):
    nh, t, _ = x.shape
    return x.transpose(1, 0, 2).reshape(1, t, nh * HEAD_DIM)


FFN_HALO = 16
FFN_PAIRS = 4


def _ffn_tile(t):
    return 512 if t == L else 256


def _halo_specs(t, tm, block, index):
    per, last = tm // FFN_HALO, t // FFN_HALO - 1
    return [pl.BlockSpec(block(tm), lambda d, i: index(d, i)),
            pl.BlockSpec(block(FFN_HALO), lambda d, i: index(d, jnp.maximum(i * per - 1, 0))),
            pl.BlockSpec(block(FFN_HALO), lambda d, i: index(d, jnp.minimum((i + 1) * per, last)))]


def _seg_edges(i, tm, t):
    return (i * tm == 0) | (i * tm == L), ((i + 1) * tm == L) | ((i + 1) * tm == t)


FFN_RC = 8


def _sigmoid_t(x):
    return 0.5 * jnp.tanh(0.5 * x) + 0.5


class _RowShifts:
    def __init__(self, buf, s):
        self.buf, self.s, self.memo = buf, s, {}
        rows = lax.broadcasted_iota(jnp.int32, (FFN_RC, FFN_BLK), 0)
        self.first, self.last = rows == 0, rows == FFN_RC - 1

    def chunk(self, r):
        if r not in self.memo:
            self.memo[r] = self.buf[self.s, r:r + FFN_RC, :]
        return self.memo[r]

    def rot(self, r, by):
        if (r, by) not in self.memo:
            self.memo[(r, by)] = pltpu.roll(self.chunk(r), by, 0)
        return self.memo[(r, by)]

    def triple(self, r):
        before = jnp.where(self.first, self.rot(r - FFN_RC, 1), self.rot(r, 1))
        behind = jnp.where(self.last, self.rot(r + FFN_RC, FFN_RC - 1), self.rot(r, FFN_RC - 1))
        return before, self.chunk(r), behind


def _conv3_of(triple, cw_ref, s, flip=False):
    taps = [cw_ref[s, 0, k:k + 1, :] for k in ((2, 1, 0) if flip else (0, 1, 2))]
    return taps[0] * triple[0] + taps[1] * triple[1] + taps[2] * triple[2]


def _ffn_core_fwd(a2, up, cw, cb, down, name):
    t = a2.shape[1]
    tm = _ffn_tile(t)
    h0 = FFN_HALO

    def body(a_ref, ap_ref, an_ref, up_ref, cw_ref, cb_ref, dn_ref, z_ref, f_ref, abuf, zbuf, ubuf):
        d, i = pl.program_id(0), pl.program_id(1)
        seg_first, seg_last = _seg_edges(i, tm, t)
        abuf[0:h0, :] = ap_ref[0]
        abuf[h0:h0 + tm, :] = a_ref[0]
        abuf[h0 + tm:, :] = an_ref[0]
        for s in range(2):
            zbuf[s] = jnp.dot(abuf[...], up_ref[s, 0], preferred_element_type=F32)

        @pl.when(seg_first)
        def _():
            zbuf[:, 0:h0, :] = jnp.zeros((2, h0, FFN_BLK), F32)

        @pl.when(seg_last)
        def _():
            zbuf[:, h0 + tm:, :] = jnp.zeros((2, h0, FFN_BLK), F32)

        for s in range(2):
            z_ref[s, 0] = zbuf[s, h0:h0 + tm, :].astype(z_ref.dtype)

        zs = [_RowShifts(zbuf, 0), _RowShifts(zbuf, 1)]
        for r in range(h0, h0 + tm, FFN_RC):
            zg = _conv3_of(zs[0].triple(r), cw_ref, 0) + cb_ref[0, 0]
            zv = _conv3_of(zs[1].triple(r), cw_ref, 1) + cb_ref[1, 0]
            ubuf[r - h0:r - h0 + FFN_RC, :] = zg * _sigmoid_t(zg) * zv
        prod = jnp.dot(ubuf[...].astype(BF16), dn_ref[0], preferred_element_type=F32)
        rows = pl.ds(pl.multiple_of(i * tm, tm), tm)

        @pl.when(d == 0)
        def _():
            f_ref[0, rows, :] = prod

        @pl.when(d > 0)
        def _():
            f_ref[0, rows, :] += prod

    pair = lambda r, c: pl.BlockSpec((2, 1, r, c), lambda d, i: (0, d, 0, 0))
    return pl.pallas_call(
        body, name=name,
        out_shape=[jax.ShapeDtypeStruct((2, FFN_PAIRS, t, FFN_BLK), BF16), jax.ShapeDtypeStruct((1, t, D), F32)],
        grid=(FFN_PAIRS, t // tm),
        in_specs=_halo_specs(t, tm, lambda r: (1, r, D), lambda d, i: (0, i, 0))
        + [pair(D, FFN_BLK), pair(FFN_K, FFN_BLK), pair(1, FFN_BLK),
           pl.BlockSpec((1, FFN_BLK, D), lambda d, i: (d, 0, 0))],
        out_specs=[pl.BlockSpec((2, 1, tm, FFN_BLK), lambda d, i: (0, d, i, 0)),
                   pl.BlockSpec((1, t, D), lambda d, i: (0, 0, 0))],
        scratch_shapes=[pltpu.VMEM((tm + 2 * h0, D), BF16), pltpu.VMEM((2, tm + 2 * h0, FFN_BLK), F32),
                        pltpu.VMEM((tm, FFN_BLK), F32)],
        compiler_params=_cparams(("arbitrary", "arbitrary")),
    )(a2, a2, a2, up, cw, cb, down)


def _ffn_core_bwd(df, z, cw, cb, down, up, a2, name):
    t = df.shape[1]
    tm = _ffn_tile(t)
    h0 = FFN_HALO
    ni = t // tm
    w0, wn = h0 // 2, tm + h0
    tn = (((0,), (0,)), ((), ()))
    nt = (((1,), (1,)), ((), ()))

    def body(df_ref, dfp_ref, dfn_ref, z_ref, zp_ref, zn_ref, cw_ref, cb_ref, dn_ref, up_ref, a2_ref,
             dcw_ref, dcb_ref, ddn_ref, da_ref, dup_ref, dfbuf, zbuf, dzbuf, acc, dubuf, dzo, acc_up):
        d, i = pl.program_id(0), pl.program_id(1)
        seg_first, seg_last = _seg_edges(i, tm, t)
        dfbuf[0:h0, :] = dfp_ref[0]
        dfbuf[h0:h0 + tm, :] = df_ref[0]
        dfbuf[h0 + tm:, :] = dfn_ref[0]
        for s in range(2):
            zbuf[s, 0:h0, :] = zp_ref[s, 0].astype(F32)
            zbuf[s, h0:h0 + tm, :] = z_ref[s, 0].astype(F32)
            zbuf[s, h0 + tm:, :] = zn_ref[s, 0].astype(F32)

        @pl.when(seg_first)
        def _():
            zbuf[:, 0:h0, :] = jnp.zeros((2, h0, FFN_BLK), F32)

        @pl.when(seg_last)
        def _():
            zbuf[:, h0 + tm:, :] = jnp.zeros((2, h0, FFN_BLK), F32)

        dubuf[...] = lax.dot_general(dfbuf[...], dn_ref[0], nt, preferred_element_type=F32)

        zs = [_RowShifts(zbuf, 0), _RowShifts(zbuf, 1)]
        sums = [[jnp.zeros((FFN_RC, FFN_BLK), F32)] * (FFN_K + 1) for _ in range(2)]
        for r in range(w0, w0 + wn, FFN_RC):
            tz = [zs[0].triple(r), zs[1].triple(r)]
            zg = _conv3_of(tz[0], cw_ref, 0) + cb_ref[0, 0]
            zv = _conv3_of(tz[1], cw_ref, 1) + cb_ref[1, 0]
            sg = _sigmoid_t(zg)
            silu = zg * sg
            du = dubuf[r:r + FFN_RC, :]
            dzc = [du * zv * (sg * (1.0 + zg * (1.0 - sg))), du * silu]
            dzbuf[0, r:r + FFN_RC, :] = dzc[0]
            dzbuf[1, r:r + FFN_RC, :] = dzc[1]
            dubuf[r:r + FFN_RC, :] = silu * zv
            if h0 <= r < h0 + tm:
                for s in range(2):
                    sums[s] = [sums[s][k] + dzc[s] * tz[s][k] for k in range(FFN_K)] + [sums[s][FFN_K] + dzc[s]]

        @pl.when(seg_first)
        def _():
            dzbuf[:, w0:h0, :] = jnp.zeros((2, h0 - w0, FFN_BLK), F32)

        @pl.when(seg_last)
        def _():
            dzbuf[:, h0 + tm:w0 + wn, :] = jnp.zeros((2, w0, FFN_BLK), F32)

        @pl.when(i == 0)
        def _():
            dcw_ref[...] = jnp.zeros_like(dcw_ref)
            dcb_ref[...] = jnp.zeros_like(dcb_ref)

        da = None
        for s in range(2):
            dzs = _RowShifts(dzbuf, s)
            for r in range(h0, h0 + tm, FFN_RC):
                dzo[r - h0:r - h0 + FFN_RC, :] = _conv3_of(dzs.triple(r), cw_ref, s, flip=True)
            dzb = dzo[...].astype(BF16)
            part = lax.dot_general(dzb, up_ref[s, 0], nt, preferred_element_type=F32)
            da = part if da is None else da + part
            gup = lax.dot_general(dzb, a2_ref[0], tn, preferred_element_type=F32)

            @pl.when(i == 0)
            def _(s=s, gup=gup):
                acc_up[s] = gup

            @pl.when(i > 0)
            def _(s=s, gup=gup):
                acc_up[s] += gup
            for k in range(FFN_K):
                dcw_ref[s, 0, k:k + 1, :] += jnp.sum(sums[s][k], axis=0, keepdims=True)
            dcb_ref[s, 0] += jnp.sum(sums[s][FFN_K], axis=0, keepdims=True)
        rows = pl.ds(pl.multiple_of(i * tm, tm), tm)

        @pl.when(d == 0)
        def _():
            da_ref[0, rows, :] = da

        @pl.when(d > 0)
        def _():
            da_ref[0, rows, :] += da

        prod = lax.dot_general(dubuf[h0:h0 + tm, :].astype(BF16), dfbuf[h0:h0 + tm, :], tn, preferred_element_type=F32)

        @pl.when(i == 0)
        def _():
            acc[...] = prod

        @pl.when(i > 0)
        def _():
            acc[...] += prod

        @pl.when(i == ni - 1)
        def _():
            ddn_ref[0] = acc[...].astype(ddn_ref.dtype)
            dup_ref[:, 0] = acc_up[...].astype(dup_ref.dtype)

    pair = lambda r, c: pl.BlockSpec((2, 1, r, c), lambda d, i: (0, d, 0, 0))
    return pl.pallas_call(
        body, name=name,
        out_shape=[jax.ShapeDtypeStruct((2, FFN_PAIRS, FFN_K, FFN_BLK), F32),
                   jax.ShapeDtypeStruct((2, FFN_PAIRS, 1, FFN_BLK), F32),
                   jax.ShapeDtypeStruct((FFN_PAIRS, FFN_BLK, D), BF16), jax.ShapeDtypeStruct((1, t, D), F32),
                   jax.ShapeDtypeStruct((2, FFN_PAIRS, FFN_BLK, D), BF16)],
        grid=(FFN_PAIRS, ni),
        in_specs=_halo_specs(t, tm, lambda r: (1, r, D), lambda d, i: (0, i, 0))
        + _halo_specs(t, tm, lambda r: (2, 1, r, FFN_BLK), lambda d, i: (0, d, i, 0))
        + [pair(FFN_K, FFN_BLK), pair(1, FFN_BLK), pl.BlockSpec((1, FFN_BLK, D), lambda d, i: (d, 0, 0)),
           pair(D, FFN_BLK), pl.BlockSpec((1, tm, D), lambda d, i: (0, i, 0))],
        out_specs=[pair(FFN_K, FFN_BLK),
                   pair(1, FFN_BLK), pl.BlockSpec((1, FFN_BLK, D), lambda d, i: (d, 0, 0)),
                   pl.BlockSpec((1, t, D), lambda d, i: (0, 0, 0)), pair(FFN_BLK, D)],
        scratch_shapes=[pltpu.VMEM((tm + 2 * h0, D), BF16), pltpu.VMEM((2, tm + 2 * h0, FFN_BLK), F32),
                        pltpu.VMEM((2, tm + 2 * h0, FFN_BLK), F32), pltpu.VMEM((FFN_BLK, D), F32),
                        pltpu.VMEM((tm + 2 * h0, FFN_BLK), F32),
                        pltpu.VMEM((tm, FFN_BLK), F32), pltpu.VMEM((2, FFN_BLK, D), F32)],
        compiler_params=_cparams(("arbitrary", "arbitrary")),
    )(df, df, df, z, z, z, cw, cb, down, up, a2)


def _ffn_fwd(i, h, mod, ng, wts):
    a2 = _rw_fwd(f_modnorm, [(h, 0)], [(ng[2], "one"), (mod["sc2"], "seg"), (mod["sh2"], "seg")],
                 [(D, BF16)], name=f"ffn{i}_norm")[0]
    z, f = _ffn_core_fwd(a2, wts["up"], wts["cw"], wts["cb"], wts["down"], f"ffn{i}_core")
    h2 = _rw_fwd(f_resgate, [(h, 0), (f, 0)], [(mod["g2"], "seg"), (ng[3], "one")], [(D, F32)],
                 name=f"ffn{i}_res")[0]
    return h2, (h, a2, z, f)


def _ffn_bwd(i, dh, res, mod, ng, wts):
    h, a2, z, f = res
    t = h.shape[1]
    df, dg2, dng3 = _rw_bwd(f_gate_rms, [(f, 0)], [(mod["g2"], "seg"), (ng[3], "one")], [dh],
                            name=f"ffn{i}_res_b", row_grad=[(0, BF16)], param_grad=[0, 1])
    dcw, dcb, d_down, da2, d_up = _ffn_core_bwd(df, z, wts["cw"], wts["cb"], wts["down"], wts["up"], a2,
                                                f"ffn{i}_core_b")
    dcw, dcb = dcw.reshape(NDEV, FFN_K, FFN_BLK), dcb.reshape(NDEV, 1, FFN_BLK)
    d_up = d_up.reshape(NDEV, FFN_BLK, D)
    dh_in, dng2, dsc2, dsh2 = _rw_bwd(
        f_modnorm, [(h, 0)], [(ng[2], "one"), (mod["sc2"], "seg"), (mod["sh2"], "seg")], [da2],
        name=f"ffn{i}_norm_b", row_grad=[(0, F32)], param_grad=[0, 1, 2], add=dh)
    grads = dict(up=d_up, down=d_down, cw=dcw, cb=dcb, ng2=dng2, ng3=dng3, sc2=dsc2, sh2=dsh2, g2=dg2)
    return dh_in, grads


def _mixer_norm_fwd(i, h, mod, ng):
    return _rw_fwd(f_modnorm, [(h, 0)], [(ng[0], "one"), (mod["sc1"], "seg"), (mod["sh1"], "seg")],
                   [(D, BF16)], name=f"mix{i}_norm")[0]


def _mixer_norm_bwd(i, h, mod, ng, da, dh):
    return _rw_bwd(f_modnorm, [(h, 0)], [(ng[0], "one"), (mod["sc1"], "seg"), (mod["sh1"], "seg")], [da],
                   name=f"mix{i}_norm_b", row_grad=[(0, F32)], param_grad=[0, 1, 2], add=dh)


def _conformer_fwd(i, h, mod, ng, wts):
    a = _mixer_norm_fwd(i, h, mod, ng)
    p = _mm(a, wts["w_in"], "nn", out_dtype=BF16, name=f"cm{i}_in")
    z = _rw_fwd(f_glu, [(p, 0, 2)], [(wts["b_in"], "one", 2)], [(D, F32)], name=f"cm{i}_glu")[0]
    zc = _dwconv(z, wts["dw_w"], wts["dw_b"], name=f"cm{i}_conv")
    r = _rw_fwd(f_lnsilu, [(zc, 0)], [(wts["ln_g"], "one"), (wts["ln_b"], "one")], [(D, BF16)],
                name=f"cm{i}_ln")[0]
    y = _mm(r, wts["w_out"], "nn", name=f"cm{i}_out")
    h2 = _rw_fwd(f_resgate_bias, [(h, 0), (y, 0)], [(mod["g1"], "seg"), (ng[1], "one"), (wts["b_out"], "one")],
                 [(D, F32)], name=f"cm{i}_res")[0]
    return h2, (h, a, p, z, zc, r, y)


def _conformer_bwd(i, dh, res, mod, ng, wts):
    h, a, p, z, zc, r, y = res
    dy, dg1, dng1, db_out = _rw_bwd(
        f_gate_rms_bias, [(y, 0)], [(mod["g1"], "seg"), (ng[1], "one"), (wts["b_out"], "one")], [dh],
        name=f"cm{i}_res_b", row_grad=[(0, BF16)], param_grad=[0, 1, 2])
    dr = _mm(dy, wts["w_out"], "nt", name=f"cm{i}_out_bx")
    d_w_out = _mm(r, dy, "tn", out_dtype=BF16, name=f"cm{i}_out_bw")
    dzc, dln_g, dln_b = _rw_bwd(f_lnsilu, [(zc, 0)], [(wts["ln_g"], "one"), (wts["ln_b"], "one")], [dr],
                                name=f"cm{i}_ln_b", row_grad=[(0, F32)], param_grad=[0, 1])
    ddw_w, ddw_b = _dwconv_wgrad(z, dzc, CM_K, name=f"cm{i}_conv_bw")
    dz = _dwconv(dzc, wts["dw_w"][:, ::-1, :], None, name=f"cm{i}_conv_bx")
    dp, db_in = _rw_bwd(f_glu, [(p, 0, 2)], [(wts["b_in"], "one", 2)], [dz], name=f"cm{i}_glu_b",
                        row_grad=[(0, BF16)], param_grad=[0])
    d_w_in = _mm(a, dp, "tn", out_dtype=BF16, name=f"cm{i}_in_bw")
    da = _mm(dp, wts["w_in"], "nt", name=f"cm{i}_in_bx")
    dh_in, dng0, dsc1, dsh1 = _mixer_norm_bwd(i, h, mod, ng, da, dh)
    grads = dict(w_in=d_w_in, w_out=d_w_out, b_in=db_in, dw_w=ddw_w, dw_b=ddw_b, ln_g=dln_g, ln_b=dln_b,
                 b_out=db_out, ng0=dng0, ng1=dng1, sc1=dsc1, sh1=dsh1, g1=dg1)
    return dh_in, grads


def _attention_fwd(i, h_all, mod, ng, wts, tables):
    a = _mixer_norm_fwd(i, h_all, mod, ng)
    qkv = _mm(a, wts["w_qkv"], "nn", name="attn_qkv")
    kv0 = N_Q * HEAD_DIM
    kv1 = kv0 + N_KV * HEAD_DIM
    q = _rope(qkv, tables, False, BF16, "attn_rope_q", cols=(kv0, 0))
    k = _rope(qkv, tables, False, BF16, "attn_rope_k", cols=(kv1 - kv0, kv0 // (kv1 - kv0)))
    pad = ((0, 0), (ATTN_BLOCK, ATTN_BLOCK), (0, 0))
    q_h = _heads(q, N_Q)
    k_h = jnp.pad(_heads(k, N_KV), pad)
    v_h = jnp.pad(_heads(qkv[:, :L, kv1:].astype(BF16), N_KV), pad)
    kc_h = _heads(qkv[:, L:, kv0:kv1].astype(BF16), N_KV)
    vc_h = _heads(qkv[:, L:, kv1:].astype(BF16), N_KV)
    o_h, lse = _attn_fwd(q_h, k_h, v_h, kc_h, vc_h, wts["sink"])
    o = _unheads(o_h)
    y = _mm(o, wts["w_o"], "nn", name="attn_o")
    h_lat = h_all[:, :L]
    mod_lat = {k_: v_[:1] for k_, v_ in mod.items()}
    h2 = _rw_fwd(f_resgate, [(h_lat, 0), (y, 0)], [(mod_lat["g1"], "seg"), (ng[1], "one")], [(D, F32)],
                 name="attn_res")[0]
    return h2, (h_all, a, q_h, k_h, v_h, kc_h, vc_h, o_h, lse, o, y)


def _attention_bwd(i, dh, res, mod, ng, wts, tables):
    h_all, a, q_h, k_h, v_h, kc_h, vc_h, o_h, lse, o, y = res
    mod_lat = {k_: v_[:1] for k_, v_ in mod.items()}
    dy, dg1, dng1 = _rw_bwd(f_gate_rms, [(y, 0)], [(mod_lat["g1"], "seg"), (ng[1], "one")], [dh],
                            name="attn_res_b", row_grad=[(0, BF16)], param_grad=[0, 1])
    do = _mm(dy, wts["w_o"], "nt", name="attn_o_bx")
    d_w_o = _mm(o, dy, "tn", out_dtype=BF16, name="attn_o_bw")
    dq_h, dk_h, dv_h, dkc_h, dvc_h, dsink = _attn_bwd(q_h, k_h, v_h, kc_h, vc_h, wts["sink"], o_h, lse,
                                                        _heads(do, N_Q))
    dq = _rope(_unheads(dq_h), tables, True, BF16, "attn_rope_q_b")
    dk = _rope(_unheads(dk_h[:, ATTN_BLOCK:-ATTN_BLOCK]), tables, True, BF16, "attn_rope_k_b")
    dv = _unheads(dv_h[:, ATTN_BLOCK:-ATTN_BLOCK]).astype(BF16)
    d_lat = jnp.concatenate([dq, dk, dv], axis=2)
    d_ctx = jnp.concatenate([jnp.zeros((1, LC, N_Q * HEAD_DIM), BF16), _unheads(dkc_h).astype(BF16),
                             _unheads(dvc_h).astype(BF16)], axis=2)
    dqkv = jnp.concatenate([d_lat, d_ctx], axis=1)
    d_w_qkv = _mm(a, dqkv, "tn", out_dtype=BF16, name="attn_qkv_bw")
    da = _mm(dqkv, wts["w_qkv"], "nt", name="attn_qkv_bx")
    dh_res = jnp.concatenate([dh, jnp.zeros((1, LC, D), F32)], axis=1)
    dh_in, dng0, dsc1, dsh1 = _mixer_norm_bwd(i, h_all, mod, ng, da, dh_res)
    grads = dict(w_qkv=d_w_qkv, w_o=d_w_o, sink=dsink, ng0=dng0, ng1=dng1, sc1=dsc1, sh1=dsh1, g1=dg1)
    return dh_in, grads


def _gmlp_fwd(i, h, mod, ng, wts):
    a = _mixer_norm_fwd(i, h, mod, ng)
    p = _mm(a, wts["w_in"], "nn", out_dtype=BF16, name="gm_in")
    u, v = _rw_fwd(f_gmlp_pre, [(p, 0, 2)], [(wts["b_in"], "one", 2), (wts["ln_g"], "one"), (wts["ln_b"], "one")],
                   [(GM_W, F32), (GM_W, BF16)], name="gm_pre")
    us = _gm_spatial_fwd(u, v, wts["w_s"], wts["b_s"])
    y = _mm(us, wts["w_out"], "nn", name="gm_out")
    h2 = _rw_fwd(f_resgate, [(h, 0), (y, 0)], [(mod["g1"], "seg"), (ng[1], "one")], [(D, F32)],
                 name="gm_res")[0]
    return h2, (h, a, p, u, v, us, y)


def _gmlp_bwd(i, dh, res, mod, ng, wts):
    h, a, p, u, v, us, y = res
    dy, dg1, dng1 = _rw_bwd(f_gate_rms, [(y, 0)], [(mod["g1"], "seg"), (ng[1], "one")], [dh],
                            name="gm_res_b", row_grad=[(0, BF16)], param_grad=[0, 1])
    dus = _mm(dy, wts["w_out"], "nt", name="gm_out_bx")
    d_w_out = _mm(us, dy, "tn", out_dtype=BF16, name="gm_out_bw")
    du, dv, dws, dbs = _gm_spatial_bwd(u, v, wts["w_s"], wts["b_s"], dus)
    dp, db_in, dln_g, dln_b = _rw_bwd(
        f_gmlp_pre, [(p, 0, 2)], [(wts["b_in"], "one", 2), (wts["ln_g"], "one"), (wts["ln_b"], "one")], [du, dv],
        name="gm_pre_b", row_grad=[(0, BF16)], param_grad=[0, 1, 2])
    d_w_in = _mm(a, dp, "tn", out_dtype=BF16, name="gm_in_bw")
    da = _mm(dp, wts["w_in"], "nt", name="gm_in_bx")
    dh_in, dng0, dsc1, dsh1 = _mixer_norm_bwd(i, h, mod, ng, da, dh)
    grads = dict(w_in=d_w_in, w_out=d_w_out, b_in=db_in, ln_g=dln_g, ln_b=dln_b, w_s=dws, b_s=dbs,
                 ng0=dng0, ng1=dng1, sc1=dsc1, sh1=dsh1, g1=dg1)
    return dh_in, grads


MOD_NAMES = ("sh1", "sc1", "g1", "sh2", "sc2", "g2")
SMALL = (
    ("norm_g", (4, 4, 128)), ("ffn_conv_w", (4, 3, 704)), ("cm_b_in", (2, 256)), ("cm_dw_w", (2, 31, 128)),
    ("cm_dw_b", (2, 128)), ("cm_ln_g", (2, 128)), ("cm_ln_b", (2, 128)), ("cm_b_out", (2, 128)),
    ("gm_b_in", (1, 512)), ("gm_ln_g", (1, 256)), ("gm_ln_b", (1, 256)))


def _mixer_weights(i, P):
    if i % 3 == 0:
        j = i // 3
        return dict(w_in=P["cm_w_in"][j], w_out=P["cm_w_out"][j], b_in=P["cm_b_in"][j].reshape(1, 1, 2 * D),
                    dw_w=P["cm_dw_w"][j][None], dw_b=P["cm_dw_b"][j].reshape(1, 1, D),
                    ln_g=P["cm_ln_g"][j].reshape(1, 1, D), ln_b=P["cm_ln_b"][j].reshape(1, 1, D),
                    b_out=P["cm_b_out"][j].reshape(1, 1, D))
    if i % 3 == 1:
        return dict(w_qkv=P["attn_w_qkv"], w_o=P["attn_w_o"], sink=P["attn_sink"].reshape(N_Q))
    return dict(w_in=P["gm_w_in"], w_out=P["gm_w_out"], b_in=P["gm_b_in"].reshape(1, 1, 2 * GM_W),
                ln_g=P["gm_ln_g"].reshape(1, 1, GM_W), ln_b=P["gm_ln_b"].reshape(1, 1, GM_W),
                w_s=P["gm_w_s"].reshape(GM_GROUPS, GM_CHUNK, GM_CHUNK).astype(BF16),
                b_s=P["gm_b_s"].reshape(GM_GROUPS, GM_CHUNK, 1))


def _ffn_weights(i, P):
    return dict(up=P["ffn_w_up"][i].reshape(2, FFN_PAIRS, D, FFN_BLK), down=P["ffn_w_down"][i],
                cw=P["ffn_conv_w"][i].reshape(2, FFN_PAIRS, FFN_K, FFN_BLK),
                cb=P["ffn_conv_b"][i].reshape(2, FFN_PAIRS, 1, FFN_BLK))


def _local_step(x, ctx, target, lat_mod, ctx_mod, norm_g, layer_weights, grads_ready):
    tables = _rope_tables()
    ng = [[norm_g[i, j].reshape(1, 1, D) for j in range(4)] for i in range(DEPTH)]

    def mods(i, with_ctx, token):
        out = {}
        for j, nme in enumerate(MOD_NAMES):
            rows = [lat_mod[i, j]] + ([ctx_mod[i, j]] if with_ctx else [])
            out[nme] = jnp.stack(rows).reshape(len(rows), 1, D) + token[0, 0]
        return out

    def after(mod, token):
        return mod if token is None else {k_: v_ + token[0, 0] for k_, v_ in mod.items()}

    h_all = jnp.concatenate([x, ctx], axis=1)
    wm0, wf0, tok = layer_weights(0, h_all)
    m0 = mods(0, True, tok)
    h, r0m = _conformer_fwd(0, h_all, m0, ng[0], wm0)
    wf0 = wf0(h) if callable(wf0) else wf0
    h, r0f = _ffn_fwd(0, h, m0, ng[0], wf0)
    wm1, wf1, tok = layer_weights(1, h)
    m1 = mods(1, True, tok)
    m1l = {k_: v_[:1] for k_, v_ in m1.items()}
    h, r1m = _attention_fwd(1, h, m1, ng[1], wm1, tables)
    wf1 = wf1(h) if callable(wf1) else wf1
    h, r1f = _ffn_fwd(1, h, m1l, ng[1], wf1)
    wm2, wf2, tok = layer_weights(2, h)
    m2 = mods(2, False, tok)
    h, r2m = _gmlp_fwd(2, h, m2, ng[2], wm2)
    wf2 = wf2(h) if callable(wf2) else wf2
    h, r2f = _ffn_fwd(2, h, m2, ng[2], wf2)
    wm3, wf3, tok = layer_weights(3, h)
    m3 = mods(3, False, tok)
    h, r3m = _conformer_fwd(3, h, m3, ng[3], wm3)
    wf3 = wf3(h) if callable(wf3) else wf3
    h, r3f = _ffn_fwd(3, h, m3, ng[3], wf3)
    dh, loss = _loss_head(h, target)

    G = {}
    dh, G["f3"] = _ffn_bwd(3, dh, r3f, m3, ng[3], wf3)
    tok = grads_ready("f3", G["f3"])
    dh, G["m3"] = _conformer_bwd(3, dh, r3m, after(m3, tok), ng[3], wm3)
    tok = grads_ready("m3", G["m3"])
    dh, G["f2"] = _ffn_bwd(2, dh, r2f, after(m2, tok), ng[2], wf2)
    tok = grads_ready("f2", G["f2"])
    dh, G["m2"] = _gmlp_bwd(2, dh, r2m, after(m2, tok), ng[2], wm2)
    tok = grads_ready("m2", G["m2"])
    dh, G["f1"] = _ffn_bwd(1, dh, r1f, after(m1l, tok), ng[1], wf1)
    tok = grads_ready("f1", G["f1"])
    dh, G["m1"] = _attention_bwd(1, dh, r1m, after(m1, tok), ng[1], wm1, tables)
    tok = grads_ready("m1", G["m1"])
    dh, G["f0"] = _ffn_bwd(0, dh, r0f, after(m0, tok), ng[0], wf0)
    tok = grads_ready("f0", G["f0"])
    dh, G["m0"] = _conformer_bwd(0, dh, r0m, after(m0, tok), ng[0], wm0)
    grads_ready("m0", G["m0"])
    grad_x = dh[:, :L]

    zero = jnp.zeros((D,), F32)
    dmod = []
    for seg in range(2):
        per_layer = []
        for i in range(DEPTH):
            vals = []
            for nme in MOD_NAMES:
                src = G[("m" if nme.endswith("1") else "f") + str(i)][nme]
                vals.append(src[seg, 0] if src.shape[0] > seg else zero)
            per_layer.append(jnp.concatenate(vals))
        dmod.append(jnp.stack(per_layer))
    dmod = jnp.stack(dmod)
    return loss, grad_x, G, dmod


def kernel(x, c, ctx, c_ctx, ada_w, ada_b, norm_g, ffn_w_up, ffn_conv_w, ffn_conv_b, ffn_w_down, cm_w_in, cm_b_in, cm_dw_w, cm_dw_b, cm_ln_g, cm_ln_b, cm_w_out, cm_b_out, attn_w_qkv, attn_sink, attn_w_o, gm_w_in, gm_b_in, gm_ln_g, gm_ln_b, gm_w_s, gm_b_s, gm_w_out, loss_target, m_c_ctx, m_ada_w, m_ada_b, m_norm_g, m_ffn_w_up, m_ffn_conv_w, m_ffn_conv_b, m_ffn_w_down, m_cm_w_in, m_cm_b_in, m_cm_dw_w, m_cm_dw_b, m_cm_ln_g, m_cm_ln_b, m_cm_w_out, m_cm_b_out, m_attn_w_qkv, m_attn_sink, m_attn_w_o, m_gm_w_in, m_gm_b_in, m_gm_ln_g, m_gm_ln_b, m_gm_w_s, m_gm_b_s, m_gm_w_out, v_c_ctx, v_ada_w, v_ada_b, v_norm_g, v_ffn_w_up, v_ffn_conv_w, v_ffn_conv_b, v_ffn_w_down, v_cm_w_in, v_cm_b_in, v_cm_dw_w, v_cm_dw_b, v_cm_ln_g, v_cm_ln_b, v_cm_w_out, v_cm_b_out, v_attn_w_qkv, v_attn_sink, v_attn_w_o, v_gm_w_in, v_gm_b_in, v_gm_ln_g, v_gm_ln_b, v_gm_w_s, v_gm_b_s, v_gm_w_out):
    W = dict(c_ctx=c_ctx, ada_w=ada_w, ada_b=ada_b, norm_g=norm_g, ffn_w_up=ffn_w_up, ffn_conv_w=ffn_conv_w, ffn_conv_b=ffn_conv_b, ffn_w_down=ffn_w_down, cm_w_in=cm_w_in, cm_b_in=cm_b_in, cm_dw_w=cm_dw_w, cm_dw_b=cm_dw_b, cm_ln_g=cm_ln_g, cm_ln_b=cm_ln_b, cm_w_out=cm_w_out, cm_b_out=cm_b_out, attn_w_qkv=attn_w_qkv, attn_sink=attn_sink, attn_w_o=attn_w_o, gm_w_in=gm_w_in, gm_b_in=gm_b_in, gm_ln_g=gm_ln_g, gm_ln_b=gm_ln_b, gm_w_s=gm_w_s, gm_b_s=gm_b_s, gm_w_out=gm_w_out)
    M = dict(c_ctx=m_c_ctx, ada_w=m_ada_w, ada_b=m_ada_b, norm_g=m_norm_g, ffn_w_up=m_ffn_w_up, ffn_conv_w=m_ffn_conv_w, ffn_conv_b=m_ffn_conv_b, ffn_w_down=m_ffn_w_down, cm_w_in=m_cm_w_in, cm_b_in=m_cm_b_in, cm_dw_w=m_cm_dw_w, cm_dw_b=m_cm_dw_b, cm_ln_g=m_cm_ln_g, cm_ln_b=m_cm_ln_b, cm_w_out=m_cm_w_out, cm_b_out=m_cm_b_out, attn_w_qkv=m_attn_w_qkv, attn_sink=m_attn_sink, attn_w_o=m_attn_w_o, gm_w_in=m_gm_w_in, gm_b_in=m_gm_b_in, gm_ln_g=m_gm_ln_g, gm_ln_b=m_gm_ln_b, gm_w_s=m_gm_w_s, gm_b_s=m_gm_b_s, gm_w_out=m_gm_w_out)
    V = dict(c_ctx=v_c_ctx, ada_w=v_ada_w, ada_b=v_ada_b, norm_g=v_norm_g, ffn_w_up=v_ffn_w_up, ffn_conv_w=v_ffn_conv_w, ffn_conv_b=v_ffn_conv_b, ffn_w_down=v_ffn_w_down, cm_w_in=v_cm_w_in, cm_b_in=v_cm_b_in, cm_dw_w=v_cm_dw_w, cm_dw_b=v_cm_dw_b, cm_ln_g=v_cm_ln_g, cm_ln_b=v_cm_ln_b, cm_w_out=v_cm_w_out, cm_b_out=v_cm_b_out, attn_w_qkv=v_attn_w_qkv, attn_sink=v_attn_sink, attn_w_o=v_attn_w_o, gm_w_in=v_gm_w_in, gm_b_in=v_gm_b_in, gm_ln_g=v_gm_ln_g, gm_ln_b=v_gm_ln_b, gm_w_s=v_gm_w_s, gm_b_s=v_gm_b_s, gm_w_out=v_gm_w_out)
    me = 4 * lax.axis_index("x") + 2 * lax.axis_index("y") + lax.axis_index("c")
    small_shapes = [s for _, s in SMALL]

    small = _pack_rows([W[n] for n, _ in SMALL] + [c])
    layer_mats = [("cm_w_in", 0, "cm_w_out", 0), ("attn_w_qkv", 0, "attn_w_o", 0), ("gm_w_in", 0, "gm_w_out", 0),
                  ("cm_w_in", 1, "cm_w_out", 1)]
    local_bf16 = [[W[a][ja].astype(BF16), W[b][jb].astype(BF16), ffn_w_up[i].astype(BF16), ffn_w_down[i].astype(BF16)]
                  for i, (a, ja, b, jb) in enumerate(layer_mats)]
    gathered0 = _all_gather([small] + local_bf16[0], "gather_params0")
    small_g = gathered0[0]
    col_to_full = lambda g: g.transpose(1, 0, 2).reshape(g.shape[1], NDEV * g.shape[2])
    P = {}
    unpacked = jax.vmap(lambda r: tuple(_unpack_rows(r, small_shapes + [(D,)])))(small_g)
    for (n, _), g in zip(SMALL, unpacked[:-1]):
        if n == "ffn_conv_w":
            P[n] = [g[:, i] for i in range(DEPTH)]
        else:
            P[n] = _unshard_last(g)
    c_all = unpacked[-1]
    P["ffn_conv_b"] = [ffn_conv_b[i].reshape(NDEV, 1, FFN_BLK) for i in range(DEPTH)]
    P["attn_sink"], P["gm_w_s"], P["gm_b_s"] = attn_sink, gm_w_s, gm_b_s

    cond = jnp.concatenate([c_all, c_ctx[None], jnp.zeros((7, D), F32)])[None]
    scond = _rw_fwd(f_silu, [(cond, 0)], [], [(D, BF16)], name="ada_silu")[0]
    ada_bf = ada_w.astype(BF16)
    ncol = ada_w.shape[2]
    mod_loc = _mm(scond, ada_bf, "nn", name="ada_proj")
    mod_loc = mod_loc + lax.dynamic_slice_in_dim(ada_b, me * ncol, ncol, axis=1)[:, None, :]
    mod_g = _all_gather([mod_loc], "gather_mod")[0]
    mod_full = mod_g.transpose(1, 2, 0, 3).reshape(DEPTH, 16, 6, D)
    lat_mod = lax.dynamic_index_in_dim(mod_full, me, axis=1, keepdims=False)
    ctx_mod = mod_full[:, NDEV]

    gathers, exchanges, pending = {}, {}, {}
    col_to_parts = lambda g: g[0].reshape(g.shape[1], NDEV, g.shape[2] // NDEV).transpose(1, 0, 2)
    row_to_parts = lambda g: g.reshape(NDEV, -1, g.shape[-1])
    no_order = jnp.zeros((8, 128), F32)

    def layer_weights(i, h):
        if i == 0:
            mats = gathered0[1:]
            gathers[1] = _xfer_start("gather", local_bf16[1], mod_g, "gather_params1")
        else:
            mats = _xfer_wait(gathers[i], h)
            if i + 1 < DEPTH:
                gathers[i + 1] = _xfer_start("gather", local_bf16[i + 1], mats[0], f"gather_params{i + 1}")
        token = gathers[i + 1]["token"] if i + 1 < DEPTH else no_order
        a, ja, b, jb = layer_mats[i]
        pi = dict(P)
        pi[a] = col_to_full(mats[0]) if a.startswith(("attn", "gm")) else {ja: col_to_full(mats[0])}
        pi[b] = mats[1].reshape(-1, D) if b.startswith(("attn", "gm")) else {jb: mats[1].reshape(-1, D)}
        pi["ffn_w_up"], pi["ffn_w_down"] = {i: mats[2]}, {i: mats[3].reshape(FFN_PAIRS, FFN_BLK, D)}
        return _mixer_weights(i, pi), _ffn_weights(i, pi), token

    def grads_ready(tag, g):
        pending[tag] = g
        i = int(tag[1])
        col_name, row_name = {0: ("w_in", "w_out"), 1: ("w_qkv", "w_o"), 2: ("w_in", "w_out")}[i % 3]
        if tag == "f0":
            arrs = [g["up"], row_to_parts(g["down"])]
        elif tag == "m0":
            arrs = [col_to_parts(g[col_name]), row_to_parts(g[row_name])]
        elif tag[0] == "m":
            gf = pending[f"f{i}"]
            arrs = [col_to_parts(g[col_name]), row_to_parts(g[row_name]), gf["up"], row_to_parts(gf["down"])]
        else:
            return None
        exchanges[tag] = _xfer_start("scatter", arrs, no_order, "exchange_" + tag)
        if tag == "m2":
            gathers["w_s"] = _xfer_start("gather", [g["w_s"].reshape(GM_GROUPS * GM_CHUNK, GM_CHUNK)],
                                         exchanges[tag]["token"], "gather_gm_w_s")
            return gathers["w_s"]["token"]
        return exchanges[tag]["token"]

    loss_part, grad_x, G, dmod = _local_step(x, ctx, loss_target, lat_mod, ctx_mod, P["norm_g"], layer_weights,
                                             grads_ready)
    recv = {tag: _xfer_wait(exchanges[tag], grad_x) for tag in ("m3", "m2", "m1")}

    dmod_g = _all_gather([dmod], "gather_dmod")[0]
    dm_cols = lax.dynamic_slice_in_dim(dmod_g, me * ncol, ncol, axis=3)
    dm_ext = dm_cols.transpose(2, 1, 0, 3).reshape(DEPTH, 16, ncol)
    cond_ext = jnp.concatenate([c_all, jnp.broadcast_to(c_ctx[None], (NDEV, D))])[None]
    scond_ext = _rw_fwd(f_silu, [(cond_ext, 0)], [], [(D, BF16)], name="ada_silu_ext")[0]
    g_ada_w = _mm(scond_ext, dm_ext, "tn", name="ada_proj_bw")
    dsil = _mm(dm_ext, ada_bf, "nt", reduce_blocks=True, name="ada_proj_bx")
    dcc = _rw_bwd(f_silu_rows, [(jnp.zeros((1, NDEV, D), F32), 0)], [(c_ctx.reshape(1, 1, D), "one")],
                  [dsil[:, NDEV:]], name="ada_silu_b", param_grad=[0])[0]

    out = {}

    def put(name, res):
        out[name] = res

    d_norm_g = jnp.stack([jnp.stack([G[f"m{i}"]["ng0"], G[f"m{i}"]["ng1"], G[f"f{i}"]["ng2"], G[f"f{i}"]["ng3"]])
                          for i in range(DEPTH)]).reshape(DEPTH, 4, D)
    small_full = dict(
        norm_g=d_norm_g,
        cm_b_in=jnp.stack([G["m0"]["b_in"], G["m3"]["b_in"]]).reshape(2, 2 * D),
        cm_dw_w=jnp.stack([G["m0"]["dw_w"][0], G["m3"]["dw_w"][0]]),
        cm_dw_b=jnp.stack([G["m0"]["dw_b"], G["m3"]["dw_b"]]).reshape(2, D),
        cm_ln_g=jnp.stack([G["m0"]["ln_g"], G["m3"]["ln_g"]]).reshape(2, D),
        cm_ln_b=jnp.stack([G["m0"]["ln_b"], G["m3"]["ln_b"]]).reshape(2, D),
        cm_b_out=jnp.stack([G["m0"]["b_out"], G["m3"]["b_out"]]).reshape(2, D),
        gm_b_in=G["m2"]["b_in"].reshape(1, 2 * GM_W),
        gm_ln_g=G["m2"]["ln_g"].reshape(1, GM_W), gm_ln_b=G["m2"]["ln_b"].reshape(1, GM_W))
    by_dest = []
    for n, _ in SMALL:
        if n == "ffn_conv_w":
            by_dest.append(jnp.stack([G[f"f{i}"]["cw"] for i in range(DEPTH)], axis=1))
        else:
            by_dest.append(_shard_last(small_full[n]))
    small_send = jax.vmap(lambda *vs: _pack_rows(list(vs)))(*by_dest)
    small_recv = _all_to_all([[small_send]], "exchange_small")[0]

    def shard3(a):
        return a.reshape(a.shape[0], -1, a.shape[-1])

    small_local = lambda d_: _pack_rows([d_[n] for n, _ in SMALL])[None]
    res = _adamw(small_recv, small_local(W), small_local(M), small_local(V), "adamw_small")
    unp = [_unpack_rows(r[0], small_shapes) for r in res]
    for q, (n, _) in enumerate(SMALL):
        put(n, [unp[t][q] for t in range(4)])

    repl_names = ["c_ctx", "ffn_conv_b", "attn_sink", "gm_b_s"]
    repl_part = dict(
        c_ctx=dcc.reshape(D),
        ffn_conv_b=jnp.stack([G[f"f{i}"]["cb"].reshape(2 * 2816) for i in range(DEPTH)]),
        attn_sink=G["m1"]["sink"].reshape(1, N_Q),
        gm_b_s=G["m2"]["b_s"].reshape(1, GM_GROUPS, GM_CHUNK))
    w_s_parts = _xfer_wait(gathers["w_s"], grad_x)[0]
    flat_s = lambda a: a.reshape(1, GM_GROUPS * GM_CHUNK, GM_CHUNK)
    res = _adamw(w_s_parts[None], flat_s(gm_w_s), flat_s(m_gm_w_s), flat_s(v_gm_w_s), "adamw_gm_w_s")
    put("gm_w_s", [r.reshape(gm_w_s.shape) for r in res])
    repl_shapes = [W[n].shape for n in repl_names]
    repl_sent = _pack_rows([repl_part[n] for n in repl_names] + [loss_part.reshape(1)], mult=256)
    repl_g = _all_gather([repl_sent], "gather_repl")[0]
    loss = jnp.sum(repl_g.reshape(NDEV, -1)[:, sum(W[n].size for n in repl_names)])
    repl_local = lambda d_: _pack_rows([d_[n] for n in repl_names], mult=256)[None]
    res = _adamw(repl_g[None], repl_local(W), repl_local(M), repl_local(V), "adamw_repl")
    unp = [_unpack_rows(r[0], repl_shapes) for r in res]
    for q, n in enumerate(repl_names):
        put(n, [unp[t][q] for t in range(4)])

    def update_big(n, parts):
        turn = (lambda a: jnp.swapaxes(a, 1, 2)) if n == "ffn_w_up" else shard3
        res = _adamw(parts, turn(W[n]), turn(M[n]), turn(V[n]), "adamw_" + n)
        put(n, [(turn(r) if n == "ffn_w_up" else r).reshape(W[n].shape) for r in res])

    put("ada_w", _adamw(g_ada_w[:, None], ada_w, m_ada_w, v_ada_w, "adamw_ada_w"))
    ada_b_parts = dmod_g.reshape(1, 2 * NDEV, DEPTH, 6 * D)
    res = _adamw(ada_b_parts, ada_b[None], m_ada_b[None], v_ada_b[None], "adamw_ada_b")
    put("ada_b", [r[0] for r in res])
    early = dict(attn_w_qkv=[recv["m1"][0]], attn_w_o=[recv["m1"][1]], gm_w_in=[recv["m2"][0]],
                 gm_w_out=[recv["m2"][1]])
    for n, parts in early.items():
        update_big(n, parts)
    done_first = sum(out[n][1].reshape(-1)[:1024] for n in ["ada_w", "ada_b", "gm_w_in", "gm_w_out", "attn_w_qkv"])
    recv.update({tag: _xfer_wait(exchanges[tag], done_first) for tag in ("f0", "m0")})
    late = dict(
        ffn_w_up=[recv["f0"][0]] + [recv[f"m{i}"][2] for i in (1, 2, 3)],
        ffn_w_down=[recv["f0"][1]] + [recv[f"m{i}"][3] for i in (1, 2, 3)],
        cm_w_in=[recv["m0"][0], recv["m3"][0]], cm_w_out=[recv["m0"][1], recv["m3"][1]])
    for n, parts in late.items():
        update_big(n, parts)

    names = ["c_ctx", "ada_w", "ada_b", "norm_g", "ffn_w_up", "ffn_conv_w", "ffn_conv_b", "ffn_w_down", "cm_w_in",
             "cm_b_in", "cm_dw_w", "cm_dw_b", "cm_ln_g", "cm_ln_b", "cm_w_out", "cm_b_out", "attn_w_qkv",
             "attn_sink", "attn_w_o", "gm_w_in", "gm_b_in", "gm_ln_g", "gm_ln_b", "gm_w_s", "gm_b_s", "gm_w_out"]
    return (loss, grad_x, *[out[n][0] for n in names], *[out[n][1] for n in names],
            *[out[n][2] for n in names], *[out[n][3] for n in names])
```

```python
import functools

import numpy as np
import jax
import jax.numpy as jnp
from jax import lax
from jax.experimental import pallas as pl
from jax.experimental.pallas import tpu as pltpu

F32, BF16 = jnp.float32, jnp.bfloat16
MESH = pl.DeviceIdType.MESH
AXES = ("x", "y", "c")
NDEV = 8

D = 1024
L = 2048
LC = 256
TA = L + LC
DEPTH = 4
EPS = 1e-6
HEAD_DIM = 64
N_Q, N_KV, Q_PER_KV = 16, 4, 4
ATTN_BLOCK = 128
GRID_W = 64
ROPE_BASE = 10000.0
GM_W = 2048
GM_CHUNK = 128
GM_GROUPS = 16
FFN_BLK = 704
CM_K, FFN_K = 31, 3

ADAM_LR, ADAM_B1, ADAM_B2, ADAM_EPS, ADAM_WD, ADAM_STEP = 0.001, 0.9, 0.999, 1e-08, 0.01, 10

VMEM_LIMIT_V7X = 58 * 1024 * 1024
ROW_TILE_ELEMS = 512 * 1024
MM_TILE_BYTES = 4 * 1024 * 1024


def _cparams(sem=None):
    kw = dict(vmem_limit_bytes=VMEM_LIMIT_V7X)
    if sem is not None:
        kw["dimension_semantics"] = sem
    return pltpu.CompilerParams(**kw)


def _pick(n, cands):
    for c in cands:
        if n % c == 0:
            return c
    return n


def _as3(a):
    return a if a.ndim == 3 else a[None]


def _all_gather(arrs, name):
    n = len(arrs)

    def body(*refs):
        ins, outs = refs[:n], refs[n:2 * n]
        send_sems, recv_sems, local_sems = refs[2 * n:]
        x, y, c = lax.axis_index("x"), lax.axis_index("y"), lax.axis_index("c")
        me, sibling = (x, y, c), (x, y, 1 - c)
        chips = [(1 - x, y), (x, 1 - y), (1 - x, 1 - y)]

        def slot(a, p):
            return outs[a].at[4 * p[0] + 2 * p[1] + p[2]]

        def copy(a, k, block, to, src=None):
            return pltpu.make_async_remote_copy(
                src_ref=slot(a, block) if src is None else src, dst_ref=slot(a, block),
                send_sem=send_sems.at[a, k], recv_sem=recv_sems.at[a, k],
                device_id=to, device_id_type=MESH)

        mine = [pltpu.make_async_copy(ins[a], slot(a, me), local_sems.at[a]) for a in range(n)]
        for m in mine:
            m.start()
        first = []
        for a in range(n):
            first.append(copy(a, 0, me, sibling, src=ins[a]))
            first += [copy(a, 1 + j, me, (*chip, c), src=ins[a]) for j, chip in enumerate(chips)]
        for cp in first:
            cp.start()
        passed = []
        for j, chip in enumerate(chips):
            for a in range(n):
                copy(a, 1 + j, (*chip, c), me).wait_recv()
                p = copy(a, 4 + j, (*chip, c), sibling)
                p.start()
                passed.append(p)
        for a in range(n):
            copy(a, 0, sibling, me).wait_recv()
            for j, chip in enumerate(chips):
                copy(a, 4 + j, (*chip, 1 - c), me).wait_recv()
        for cp in first + passed:
            cp.wait_send()
        for m in mine:
            m.wait()

    any_spec = pl.BlockSpec(memory_space=pl.ANY)
    outs = pl.pallas_call(
        body, name=name,
        out_shape=[jax.ShapeDtypeStruct((NDEV,) + a.shape, a.dtype) for a in arrs],
        in_specs=[any_spec] * n, out_specs=[any_spec] * n,
        scratch_shapes=[pltpu.SemaphoreType.DMA((n, 7)), pltpu.SemaphoreType.DMA((n, 7)),
                        pltpu.SemaphoreType.DMA((n,))],
    )(*arrs)
    return list(outs)


def _all_to_all(groups, name):
    flat = [(gi, li, a) for gi, g in enumerate(groups) for li, a in enumerate(g)]
    n, ng = len(flat), len(groups)

    def body(*refs):
        ins, outs = refs[:n], refs[n:n + ng]
        send_sems, recv_sems, local_sems = refs[n + ng:]
        x, y, c = lax.axis_index("x"), lax.axis_index("y"), lax.axis_index("c")
        me = 4 * x + 2 * y + c
        copies = []
        for a, (gi, li, _) in enumerate(flat):
            loc = pltpu.make_async_copy(ins[a].at[me], outs[gi].at[li, me], local_sems.at[a])
            loc.start()
            copies.append(loc)
            for k in range(1, NDEV):
                px = 1 - x if (k >> 2) & 1 else x
                py = 1 - y if (k >> 1) & 1 else y
                pc = 1 - c if k & 1 else c
                cp = pltpu.make_async_remote_copy(
                    src_ref=ins[a].at[4 * px + 2 * py + pc], dst_ref=outs[gi].at[li, me],
                    send_sem=send_sems.at[a, k - 1], recv_sem=recv_sems.at[a, k - 1],
                    device_id=(px, py, pc), device_id_type=MESH)
                cp.start()
                copies.append(cp)
        for cp in copies:
            cp.wait()

    any_spec = pl.BlockSpec(memory_space=pl.ANY)
    outs = pl.pallas_call(
        body, name=name,
        out_shape=[jax.ShapeDtypeStruct((len(g),) + g[0].shape, g[0].dtype) for g in groups],
        in_specs=[any_spec] * n, out_specs=[any_spec] * ng,
        scratch_shapes=[pltpu.SemaphoreType.DMA((n, 7)), pltpu.SemaphoreType.DMA((n, 7)),
                        pltpu.SemaphoreType.DMA((n,))],
    )(*[a for _, _, a in flat])
    return list(outs)


HBM_SPEC = pl.BlockSpec(memory_space=pltpu.HBM)
SEM_SPEC = pl.BlockSpec(memory_space=pltpu.SEMAPHORE)
ANY_SPEC = pl.BlockSpec(memory_space=pl.ANY)
SPLIT_EFFECT = pltpu.SideEffectType.DATAFLOW_SIDE_EFFECTING


def _remote_copies(kind, ins, lands, send_sems, recv_sems):
    x, y, c = lax.axis_index("x"), lax.axis_index("y"), lax.axis_index("c")
    me = 4 * x + 2 * y + c
    out = []
    for a in range(len(ins)):
        for k in range(1, NDEV):
            px = 1 - x if (k >> 2) & 1 else x
            py = 1 - y if (k >> 1) & 1 else y
            pc = 1 - c if k & 1 else c
            src = ins[a] if kind == "gather" else ins[a].at[4 * px + 2 * py + pc]
            out.append(pltpu.make_async_remote_copy(
                src_ref=src, dst_ref=lands[a].at[me], send_sem=send_sems.at[a * (NDEV - 1) + k - 1],
                recv_sem=recv_sems.at[a * (NDEV - 1) + k - 1], device_id=(px, py, pc), device_id_type=MESH))
    return out


def _local_copies(kind, ins, lands, local_sems):
    me = 4 * lax.axis_index("x") + 2 * lax.axis_index("y") + lax.axis_index("c")
    return [pltpu.make_async_copy(ins[a] if kind == "gather" else ins[a].at[me], lands[a].at[me], local_sems.at[a])
            for a in range(len(ins))]


def _xfer_start(kind, arrs, after, name):
    n = len(arrs)
    lands = [lax.empty((NDEV,) + a.shape if kind == "gather" else a.shape, a.dtype) for a in arrs]

    def body(*refs):
        ins, lnd = refs[:n], refs[n:2 * n]
        send_sems, recv_sems, local_sems = refs[2 * n + 1:2 * n + 4]
        for cp in _remote_copies(kind, ins, lnd, send_sems, recv_sems) + _local_copies(kind, ins, lnd, local_sems):
            cp.start()
        refs[-1][...] = jnp.zeros_like(refs[-1])

    outs = pl.pallas_call(
        body, name=name,
        out_shape=(pltpu.SemaphoreType.DMA((n * (NDEV - 1),)), pltpu.SemaphoreType.DMA((n * (NDEV - 1),)),
                   pltpu.SemaphoreType.DMA((n,)),
                   *[pltpu.HBM(a.shape, a.dtype) for a in arrs + lands], jax.ShapeDtypeStruct((8, 128), F32)),
        in_specs=[HBM_SPEC] * (2 * n) + [ANY_SPEC],
        out_specs=(SEM_SPEC, SEM_SPEC, SEM_SPEC, *[HBM_SPEC] * (2 * n), pl.BlockSpec(memory_space=pltpu.VMEM)),
        input_output_aliases={a: 3 + a for a in range(2 * n)},
        compiler_params=pltpu.CompilerParams(has_side_effects=SPLIT_EFFECT),
    )(*[pltpu.with_memory_space_constraint(a, pltpu.HBM) for a in arrs + lands], after)
    return dict(kind=kind, n=n, sems=list(outs[:3]), bufs=list(outs[3:3 + 2 * n]), token=outs[-1], name=name)


def _xfer_wait(st, after):
    kind, n = st["kind"], st["n"]

    def body(*refs):
        ins, lnd = refs[:n], refs[n:2 * n]
        send_sems, recv_sems, local_sems = refs[2 * n:2 * n + 3]
        for cp in _remote_copies(kind, ins, lnd, send_sems, recv_sems):
            cp.wait_send()
            cp.wait_recv()
        for cp in _local_copies(kind, ins, lnd, local_sems):
            cp.wait()

    outs = pl.pallas_call(
        body, name=st["name"] + "_wait",
        out_shape=tuple(pltpu.HBM(b.shape, b.dtype) for b in st["bufs"]),
        in_specs=[HBM_SPEC] * (2 * n) + [SEM_SPEC] * 3 + [ANY_SPEC],
        out_specs=tuple([HBM_SPEC] * (2 * n)),
        input_output_aliases={a: a for a in range(2 * n)},
        compiler_params=pltpu.CompilerParams(has_side_effects=SPLIT_EFFECT),
    )(*st["bufs"], *st["sems"], after)
    return list(outs[n:])


def _mm(a, b, kind, *, name, out_dtype=F32, reduce_blocks=False):
    a, b = _as3(a), _as3(b)
    nba, nbb = a.shape[0], b.shape[0]
    nb = max(nba, nbb)
    assert nba in (1, nb) and nbb in (1, nb)
    if kind == "tn":
        t, m = a.shape[1:]
        n = b.shape[2]
        assert b.shape[1] == t and not reduce_blocks
        tm = m if m <= 1024 else _pick(m, (1024,))
        tn = n if n <= 1024 else _pick(n, (1024, 768, 512))
        tk = t if t * (tm + tn) * 2 <= MM_TILE_BYTES * 3 else _pick(t, (512, 768, 256))
        nred = t // tk
    else:
        m, k = a.shape[1:]
        n = b.shape[2] if kind == "nn" else b.shape[1]
        assert (b.shape[1] if kind == "nn" else b.shape[2]) == k
        tn = n if (n <= 1024 or k * n * 2 <= 2 * MM_TILE_BYTES) else _pick(n, (1024, 768, 512))
        tm = m
        for cand in (1024, 768, 512, 256):
            if m % cand == 0 and cand * tn * 4 <= MM_TILE_BYTES and cand * k * 2 <= MM_TILE_BYTES:
                tm = cand
                break
        nred = nb if reduce_blocks else 1
    nbo = 1 if reduce_blocks else nb

    def blk(nbx, g, r):
        if nbx == 1:
            return 0
        return r if reduce_blocks else g

    if kind == "nn":
        a_spec = pl.BlockSpec((1, tm, k), lambda g, j, i, r: (blk(nba, g, r), i, 0))
        b_spec = pl.BlockSpec((1, k, tn), lambda g, j, i, r: (blk(nbb, g, r), 0, j))
        dims = (((1,), (0,)), ((), ()))
    elif kind == "nt":
        a_spec = pl.BlockSpec((1, tm, k), lambda g, j, i, r: (blk(nba, g, r), i, 0))
        b_spec = pl.BlockSpec((1, tn, k), lambda g, j, i, r: (blk(nbb, g, r), j, 0))
        dims = (((1,), (1,)), ((), ()))
    else:
        a_spec = pl.BlockSpec((1, tk, tm), lambda g, j, i, r: (blk(nba, g, r), r, i))
        b_spec = pl.BlockSpec((1, tk, tn), lambda g, j, i, r: (blk(nbb, g, r), r, j))
        dims = (((0,), (0,)), ((), ()))
    o_spec = pl.BlockSpec((1, tm, tn), lambda g, j, i, r: (g, i, j))

    def body(a_ref, b_ref, o_ref, *scratch):
        prod = lax.dot_general(a_ref[0].astype(BF16), b_ref[0].astype(BF16), dims,
                               preferred_element_type=F32)
        if nred == 1:
            o_ref[0] = prod.astype(o_ref.dtype)
        else:
            acc = scratch[0]
            r = pl.program_id(3)

            @pl.when(r == 0)
            def _():
                acc[...] = prod

            @pl.when(r > 0)
            def _():
                acc[...] += prod

            @pl.when(r == nred - 1)
            def _():
                o_ref[0] = acc[...].astype(o_ref.dtype)

    return pl.pallas_call(
        body, name=name,
        out_shape=jax.ShapeDtypeStruct((nbo, m, n), out_dtype),
        grid=(nbo, n // tn, m // tm, nred),
        in_specs=[a_spec, b_spec], out_specs=o_spec,
        scratch_shapes=[pltpu.VMEM((tm, tn), F32)] if nred > 1 else [],
        compiler_params=_cparams(("parallel", "parallel", "parallel", "arbitrary")),
    )(a, b)


def _row_tile(t, widths):
    tm = min(max(16, ROW_TILE_ELEMS // max(widths)), 512)
    while t % tm and tm > 16:
        tm //= 2
    return t if t < tm else tm


def _sel_index(sel, g, i, tm):
    if sel == "one":
        return 0
    if sel == "seg":
        return (i * tm) // L
    return g + sel


def _row_spec(arr, off, tm):
    return pl.BlockSpec((1, tm, arr.shape[2]), lambda g, i: (g + off, i, 0))


def _par_spec(arr, sel, tm):
    return pl.BlockSpec((1, 1, arr.shape[2]), lambda g, i: (_sel_index(sel, g, i, tm), 0, 0))


def _norm_ops(ops):
    return [(o[0], o[1], o[2] if len(o) > 2 else 1) for o in ops]


def _split_cols(vals, nsplit):
    out = []
    for v, ns in zip(vals, nsplit):
        w = v.shape[1] // ns
        out += [v] if ns == 1 else [v[:, q * w:(q + 1) * w] for q in range(ns)]
    return out


def _join_cols(flat, nsplit):
    out, pos = [], 0
    for ns in nsplit:
        out.append(flat[pos] if ns == 1 else jnp.concatenate(flat[pos:pos + ns], axis=1))
        pos += ns
    return out


def _rw_fwd(fn, rows, params, outs, *, name, nblk=None):
    rows, params = _norm_ops(rows), _norm_ops(params)
    t = rows[0][0].shape[1]
    nblk = nblk or rows[0][0].shape[0]
    tm = _row_tile(t, [r.shape[2] for r, _, _ in rows] + [w for w, _ in outs])
    nr, npar = len(rows), len(params)
    nsplit = [ns for _, _, ns in rows + params]

    def body(*refs):
        vals = _split_cols([r[0].astype(F32) for r in refs[:nr + npar]], nsplit)
        res = fn(*vals)
        for o_ref, o in zip(refs[nr + npar:], res):
            o_ref[0] = o.astype(o_ref.dtype)

    res = pl.pallas_call(
        body, name=name,
        out_shape=[jax.ShapeDtypeStruct((nblk, t, w), dt) for w, dt in outs],
        grid=(nblk, t // tm),
        in_specs=[_row_spec(r, off, tm) for r, off, _ in rows] + [_par_spec(p, s, tm) for p, s, _ in params],
        out_specs=[pl.BlockSpec((1, tm, w), lambda g, i: (g, i, 0)) for w, _ in outs],
        compiler_params=_cparams(("parallel", "parallel")),
    )(*[r for r, _, _ in rows], *[p for p, _, _ in params])
    return list(res)


def _rw_bwd(fn, rows, params, cts, *, name, row_grad=(), param_grad=(), add=None, nblk=None):
    rows, params = _norm_ops(rows), _norm_ops(params)
    t = cts[0].shape[1]
    nblk = nblk or cts[0].shape[0]
    tm = _row_tile(t, [r.shape[2] for r, _, _ in rows] + [c.shape[2] for c in cts])
    ni = t // tm
    nr, npar, nct = len(rows), len(params), len(cts)
    nadd = 0 if add is None else 1
    n_in = nr + npar + nct + nadd
    nsplit = [ns for _, _, ns in rows + params]

    def body(*refs):
        prim = _split_cols([r[0].astype(F32) for r in refs[:nr + npar]], nsplit)
        ct = tuple(r[0].astype(F32) for r in refs[nr + npar:nr + npar + nct])
        _, vjp = jax.vjp(fn, *prim)
        grads = _join_cols(list(vjp(ct)), nsplit)
        out_refs = refs[n_in:]
        for q, (ri, _) in enumerate(row_grad):
            gr = grads[ri]
            if q == 0 and nadd:
                gr = gr + refs[n_in - 1][0].astype(F32)
            out_refs[q][0] = gr.astype(out_refs[q].dtype)
        g, i = pl.program_id(0), pl.program_id(1)
        step = g * ni + i
        pg, pi = (step - 1) // ni, (step - 1) % ni
        for q, pidx in enumerate(param_grad):
            o_ref = out_refs[len(row_grad) + q]
            sel = params[pidx][1]
            val = grads[nr + pidx]
            if sel == "one":
                first = step == 0
            else:
                first = (step == 0) | (_sel_index(sel, g, i, tm) != _sel_index(sel, pg, pi, tm))

            @pl.when(first)
            def _(o_ref=o_ref, val=val):
                o_ref[0] = val

            @pl.when(jnp.logical_not(first))
            def _(o_ref=o_ref, val=val):
                o_ref[0] += val

    in_arrays = [r for r, _, _ in rows] + [p for p, _, _ in params] + list(cts) + ([add] if nadd else [])
    in_specs = ([_row_spec(r, off, tm) for r, off, _ in rows] + [_par_spec(p, s, tm) for p, s, _ in params]
                + [_row_spec(c, 0, tm) for c in cts] + ([_row_spec(add, 0, tm)] if nadd else []))
    out_shape, out_specs = [], []
    for ri, dt in row_grad:
        w = rows[ri][0].shape[2]
        out_shape.append(jax.ShapeDtypeStruct((nblk, t, w), dt))
        out_specs.append(pl.BlockSpec((1, tm, w), lambda g, i: (g, i, 0)))
    for pidx in param_grad:
        p, sel, _ = params[pidx]
        out_shape.append(jax.ShapeDtypeStruct(p.shape, F32))
        out_specs.append(_par_spec(p, sel, tm))
    res = pl.pallas_call(
        body, name=name, out_shape=out_shape, grid=(nblk, ni),
        in_specs=in_specs, out_specs=out_specs,
        compiler_params=_cparams(("arbitrary", "arbitrary")),
    )(*in_arrays)
    return list(res)


def _sigmoid(x):
    return 1.0 / (1.0 + jnp.exp(-x))


def _rms(x, g):
    return x * lax.rsqrt(jnp.mean(x * x, axis=-1, keepdims=True) + EPS) * g


def _ln(x, g, b):
    mu = jnp.mean(x, axis=-1, keepdims=True)
    xc = x - mu
    var = jnp.mean(xc * xc, axis=-1, keepdims=True)
    return xc * lax.rsqrt(var + EPS) * g + b


def _gelu_tanh(x):
    return 0.5 * x * (1.0 + jnp.tanh(0.7978845608028654 * (x + 0.044715 * (x * x * x))))


def f_modnorm(h, g, sc, sh):
    return (_rms(h, g) * (1.0 + sc) + sh,)


def f_gate_rms(y, gate, g):
    return (gate * _rms(y, g),)


def f_gate_rms_bias(y, gate, g, b):
    return (gate * _rms(y + b, g),)


def f_resgate(h, y, gate, g):
    return (h + gate * _rms(y, g),)


def f_resgate_bias(h, y, gate, g, b):
    return (h + gate * _rms(y + b, g),)


def f_glu(pa, pg, ba, bg):
    return ((pa + ba) * _sigmoid(pg + bg),)


def f_lnsilu(z, g, b):
    t = _ln(z, g, b)
    return (t * _sigmoid(t),)


def f_gmlp_pre(pu, pv, bu, bv, g, bb):
    return _gelu_tanh(pu + bu), _ln(_gelu_tanh(pv + bv), g, bb)


def f_ffn_gate(zg, zv):
    return (zg * _sigmoid(zg) * zv,)


def f_silu(x):
    return (x * _sigmoid(x),)


def f_silu_rows(dummy, cc):
    return (cc * _sigmoid(cc) + 0.0 * dummy,)


def _rope(x_in, tables, neg_sin, out_dtype, name, cols=None):
    w, ci = cols if cols else (x_in.shape[2], 0)
    sign = -1.0 if neg_sin else 1.0
    tm = 256

    def body(x_ref, cos_ref, sin_ref, o_ref):
        x = x_ref[0].astype(F32)
        cos = jnp.tile(cos_ref[0], (1, w // 128))
        sin = jnp.tile(sin_ref[0], (1, w // 128)) * sign
        lane = lax.broadcasted_iota(jnp.int32, x.shape, 1) & 31
        rot = jnp.where(lane < 16, -pltpu.roll(x, w - 16, 1), pltpu.roll(x, 16, 1))
        o_ref[0] = (x * cos + rot * sin).astype(o_ref.dtype)

    tspec = pl.BlockSpec((1, tm, 128), lambda i: (0, i, 0))
    return pl.pallas_call(
        body, name=name, out_shape=jax.ShapeDtypeStruct((1, L, w), out_dtype), grid=(L // tm,),
        in_specs=[pl.BlockSpec((1, tm, w), lambda i: (0, i, ci)), tspec, tspec],
        out_specs=pl.BlockSpec((1, tm, w), lambda i: (0, i, 0)),
        compiler_params=_cparams(("parallel",)),
    )(x_in, tables[0], tables[1])


CONV_TM = 256
CONV_RC = 8


class _ShiftedRows:
    def __init__(self, xp, cb):
        self.xp, self.memo = xp, {}
        self.row = lax.broadcasted_iota(jnp.int32, (CONV_RC, cb), 0)

    def _get(self, key, make):
        if key not in self.memo:
            self.memo[key] = make()
        return self.memo[key]

    def chunk(self, a):
        return self._get(("c", a), lambda: self.xp[a:a + CONV_RC, :])

    def rot(self, a, j):
        return self._get(("r", a, j), lambda: pltpu.roll(self.chunk(a), CONV_RC - j, 0))

    def at(self, a):
        q, j = divmod(a, CONV_RC)
        if j == 0:
            return self.chunk(a)
        low = self._get(("m", j), lambda: self.row < CONV_RC - j)
        return jnp.where(low, self.rot(q * CONV_RC, j), self.rot(q * CONV_RC + CONV_RC, j))


def _conv_geometry(x, k):
    nb, t, w = x.shape
    halo = 16 if k > 17 else 8
    cb = _pick(w, (512,)) if w > 768 else w
    return nb, t, w, halo, cb, (k - 1) // 2


def _conv_in_specs(t, halo, cb):
    per = CONV_TM // halo
    last = t // halo - 1
    return [
        pl.BlockSpec((1, CONV_TM, cb), lambda g, jc, i: (g, i, jc)),
        pl.BlockSpec((1, halo, cb), lambda g, jc, i: (g, jnp.maximum(i * per - 1, 0), jc)),
        pl.BlockSpec((1, halo, cb), lambda g, jc, i: (g, jnp.minimum((i + 1) * per, last), jc)),
    ]


def _conv_fill(xp, x_ref, prev_ref, next_ref, halo, t):
    i = pl.program_id(2)
    seg_first = (i * CONV_TM == 0) | (i * CONV_TM == L)
    seg_last = ((i + 1) * CONV_TM == L) | ((i + 1) * CONV_TM == t)
    xp[0:halo, :] = jnp.where(seg_first, 0.0, prev_ref[0].astype(F32))
    xp[halo:halo + CONV_TM, :] = x_ref[0].astype(F32)
    xp[halo + CONV_TM:, :] = jnp.where(seg_last, 0.0, next_ref[0].astype(F32))


def _dwconv(x, w, b, *, name, out_dtype=F32):
    k = w.shape[1]
    nb, t, wd, halo, cb, half = _conv_geometry(x, k)
    base = halo - half

    def body(*refs):
        x_ref, prev_ref, next_ref, w_ref = refs[:4]
        b_ref = refs[4] if b is not None else None
        o_ref, xp = refs[-2], refs[-1]
        _conv_fill(xp, x_ref, prev_ref, next_ref, halo, t)
        rows = _ShiftedRows(xp, cb)
        for o in range(0, CONV_TM, CONV_RC):
            acc = None
            for kk in range(k):
                term = w_ref[0, kk:kk + 1, :] * rows.at(o + base + kk)
                acc = term if acc is None else acc + term
            if b_ref is not None:
                acc = acc + b_ref[0]
            o_ref[0, o:o + CONV_RC, :] = acc.astype(o_ref.dtype)

    in_specs = _conv_in_specs(t, halo, cb) + [pl.BlockSpec((1, k, cb), lambda g, jc, i: (g, 0, jc))]
    args = [x, x, x, w]
    if b is not None:
        in_specs.append(pl.BlockSpec((1, 1, cb), lambda g, jc, i: (g, 0, jc)))
        args.append(b)
    return pl.pallas_call(
        body, name=name, out_shape=jax.ShapeDtypeStruct((nb, t, wd), out_dtype),
        grid=(nb, wd // cb, t // CONV_TM), in_specs=in_specs,
        out_specs=pl.BlockSpec((1, CONV_TM, cb), lambda g, jc, i: (g, i, jc)),
        scratch_shapes=[pltpu.VMEM((CONV_TM + 2 * halo, cb), F32)],
        compiler_params=_cparams(("parallel", "parallel", "parallel")),
    )(*args)


def _dwconv_wgrad(x, dy, k, *, name):
    nb, t, wd, halo, cb, half = _conv_geometry(x, k)
    base = halo - half

    def body(x_ref, prev_ref, next_ref, dy_ref, dw_ref, db_ref, xp):
        _conv_fill(xp, x_ref, prev_ref, next_ref, halo, t)
        i = pl.program_id(2)

        @pl.when(i == 0)
        def _():
            dw_ref[...] = jnp.zeros_like(dw_ref)
            db_ref[...] = jnp.zeros_like(db_ref)

        rows = _ShiftedRows(xp, cb)
        dys = [dy_ref[0, o:o + CONV_RC, :].astype(F32) for o in range(0, CONV_TM, CONV_RC)]
        db_ref[0] += jnp.sum(sum(dys[1:], dys[0]), axis=0, keepdims=True)
        for kk in range(k):
            acc = None
            for ci, dyc in enumerate(dys):
                term = dyc * rows.at(ci * CONV_RC + base + kk)
                acc = term if acc is None else acc + term
            dw_ref[0, kk:kk + 1, :] += jnp.sum(acc, axis=0, keepdims=True)

    dw, db = pl.pallas_call(
        body, name=name,
        out_shape=[jax.ShapeDtypeStruct((nb, k, wd), F32), jax.ShapeDtypeStruct((nb, 1, wd), F32)],
        grid=(nb, wd // cb, t // CONV_TM),
        in_specs=_conv_in_specs(t, halo, cb) + [pl.BlockSpec((1, CONV_TM, cb), lambda g, jc, i: (g, i, jc))],
        out_specs=[pl.BlockSpec((1, k, cb), lambda g, jc, i: (g, 0, jc)),
                   pl.BlockSpec((1, 1, cb), lambda g, jc, i: (g, 0, jc))],
        scratch_shapes=[pltpu.VMEM((CONV_TM + 2 * halo, cb), F32)],
        compiler_params=_cparams(("parallel", "parallel", "arbitrary")),
    )(x, x, x, dy)
    return dw, db


ATTN_SCALE = HEAD_DIM ** -0.5
QROWS = Q_PER_KV * ATTN_BLOCK
NEG = -1e30


def _attn_bias():
    qi = np.arange(ATTN_BLOCK)[:, None] + ATTN_BLOCK
    kj = np.arange(3 * ATTN_BLOCK)[None, :]
    near = np.abs(qi - kj) <= ATTN_BLOCK
    valid = np.stack([kj >= ATTN_BLOCK, kj >= 0, kj < 2 * ATTN_BLOCK])
    return jnp.asarray(np.where(near[None] & valid, 0.0, NEG), F32)


def _attn_scores(q, kw, kc, bias):
    nt = (((1,), (1,)), ((), ()))
    s_w = lax.dot_general(q, kw, nt, preferred_element_type=F32) * ATTN_SCALE
    s_w = (s_w.reshape(Q_PER_KV, ATTN_BLOCK, 3 * ATTN_BLOCK) + bias[None]).reshape(QROWS, 3 * ATTN_BLOCK)
    s_c = lax.dot_general(q, kc, nt, preferred_element_type=F32) * ATTN_SCALE
    return s_w, s_c


def _sink_col(sink_ref, hk):
    return jnp.concatenate([jnp.full((ATTN_BLOCK, 1), sink_ref[hk * Q_PER_KV + g], F32) for g in range(Q_PER_KV)], axis=0)


def _attn_specs():
    qspec = pl.BlockSpec((Q_PER_KV, ATTN_BLOCK, HEAD_DIM), lambda hk, n: (hk, n, 0))
    kspec = pl.BlockSpec((1, L + 2 * ATTN_BLOCK, HEAD_DIM), lambda hk, n: (hk, 0, 0))
    cspec = pl.BlockSpec((1, LC, HEAD_DIM), lambda hk, n: (hk, 0, 0))
    lspec = pl.BlockSpec((Q_PER_KV, ATTN_BLOCK, 1), lambda hk, n: (hk, n, 0))
    sspec = pl.BlockSpec(memory_space=pltpu.SMEM)
    last = L // ATTN_BLOCK - 1
    bspec = pl.BlockSpec((1, ATTN_BLOCK, 3 * ATTN_BLOCK),
                         lambda hk, n: (jnp.where(n == 0, 0, jnp.where(n == last, 2, 1)), 0, 0))
    return qspec, kspec, cspec, lspec, sspec, bspec


def _attn_fwd(q, k, v, kc, vc, sink):
    qspec, kspec, cspec, lspec, sspec, bspec = _attn_specs()

    def body(bias_ref, q_ref, k_ref, v_ref, kc_ref, vc_ref, sink_ref, o_ref, lse_ref):
        hk, n = pl.program_id(0), pl.program_id(1)
        qv = q_ref[...].reshape(QROWS, HEAD_DIM)
        start = pl.multiple_of(n * ATTN_BLOCK, ATTN_BLOCK)
        kw = k_ref[0, pl.ds(start, 3 * ATTN_BLOCK), :]
        vw = v_ref[0, pl.ds(start, 3 * ATTN_BLOCK), :]
        s_w, s_c = _attn_scores(qv, kw, kc_ref[0], bias_ref[0])
        sk = _sink_col(sink_ref, hk)
        m = jnp.maximum(jnp.maximum(jnp.max(s_w, -1, keepdims=True), jnp.max(s_c, -1, keepdims=True)), sk)
        p_w, p_c = jnp.exp(s_w - m), jnp.exp(s_c - m)
        den = jnp.sum(p_w, -1, keepdims=True) + jnp.sum(p_c, -1, keepdims=True) + jnp.exp(sk - m)
        o = (jnp.dot(p_w.astype(BF16), vw, preferred_element_type=F32)
             + jnp.dot(p_c.astype(BF16), vc_ref[0], preferred_element_type=F32)) / den
        o_ref[...] = o.reshape(Q_PER_KV, ATTN_BLOCK, HEAD_DIM).astype(o_ref.dtype)
        lse_ref[...] = (m + jnp.log(den)).reshape(Q_PER_KV, ATTN_BLOCK, 1)

    return pl.pallas_call(
        body, name="attn_fwd",
        out_shape=[jax.ShapeDtypeStruct((N_Q, L, HEAD_DIM), BF16), jax.ShapeDtypeStruct((N_Q, L, 1), F32)],
        grid=(N_KV, L // ATTN_BLOCK),
        in_specs=[bspec, qspec, kspec, kspec, cspec, cspec, sspec], out_specs=[qspec, lspec],
        compiler_params=_cparams(("parallel", "parallel")),
    )(_attn_bias(), q, k, v, kc, vc, sink)


def _attn_bwd(q, k, v, kc, vc, sink, o, lse, do):
    qspec, kspec, cspec, lspec, sspec, bspec = _attn_specs()
    tn = (((0,), (0,)), ((), ()))
    nt = (((1,), (1,)), ((), ()))

    def body(bias_ref, q_ref, k_ref, v_ref, kc_ref, vc_ref, sink_ref, o_ref, lse_ref, do_ref,
             dq_ref, dk_ref, dv_ref, dkc_ref, dvc_ref, dsink_ref):
        hk, n = pl.program_id(0), pl.program_id(1)
        qv = q_ref[...].reshape(QROWS, HEAD_DIM)
        start = pl.multiple_of(n * ATTN_BLOCK, ATTN_BLOCK)
        win = pl.ds(start, 3 * ATTN_BLOCK)
        kw, vw = k_ref[0, win, :], v_ref[0, win, :]
        kcv, vcv = kc_ref[0], vc_ref[0]
        s_w, s_c = _attn_scores(qv, kw, kcv, bias_ref[0])
        lse_v = lse_ref[...].reshape(QROWS, 1)
        p_w, p_c = jnp.exp(s_w - lse_v), jnp.exp(s_c - lse_v)
        dov = do_ref[...].reshape(QROWS, HEAD_DIM).astype(F32)
        ov = o_ref[...].reshape(QROWS, HEAD_DIM).astype(F32)
        delta = jnp.sum(dov * ov, -1, keepdims=True)
        dob = dov.astype(BF16)
        dp_w = lax.dot_general(dob, vw, nt, preferred_element_type=F32)
        dp_c = lax.dot_general(dob, vcv, nt, preferred_element_type=F32)
        ds_w = (p_w * (dp_w - delta) * ATTN_SCALE).astype(BF16)
        ds_c = (p_c * (dp_c - delta) * ATTN_SCALE).astype(BF16)
        dq = jnp.dot(ds_w, kw, preferred_element_type=F32) + jnp.dot(ds_c, kcv, preferred_element_type=F32)
        dq_ref[...] = dq.reshape(Q_PER_KV, ATTN_BLOCK, HEAD_DIM)

        @pl.when(n == 0)
        def _():
            dk_ref[...] = jnp.zeros_like(dk_ref)
            dv_ref[...] = jnp.zeros_like(dv_ref)
            dkc_ref[...] = jnp.zeros_like(dkc_ref)
            dvc_ref[...] = jnp.zeros_like(dvc_ref)

        dk_ref[0, win, :] += lax.dot_general(ds_w, qv, tn, preferred_element_type=F32)
        dv_ref[0, win, :] += lax.dot_general(p_w.astype(BF16), dob, tn, preferred_element_type=F32)
        dkc_ref[0] += lax.dot_general(ds_c, qv, tn, preferred_element_type=F32)
        dvc_ref[0] += lax.dot_general(p_c.astype(BF16), dob, tn, preferred_element_type=F32)
        dsk = -jnp.exp(_sink_col(sink_ref, hk) - lse_v) * delta
        for g in range(Q_PER_KV):
            part = jnp.sum(dsk[g * ATTN_BLOCK:(g + 1) * ATTN_BLOCK])
            idx = hk * Q_PER_KV + g

            @pl.when(n == 0)
            def _(part=part, idx=idx):
                dsink_ref[idx] = part

            @pl.when(n > 0)
            def _(part=part, idx=idx):
                dsink_ref[idx] += part

    kshape = jax.ShapeDtypeStruct((N_KV, L + 2 * ATTN_BLOCK, HEAD_DIM), F32)
    cshape = jax.ShapeDtypeStruct((N_KV, LC, HEAD_DIM), F32)
    return pl.pallas_call(
        body, name="attn_bwd",
        out_shape=[jax.ShapeDtypeStruct((N_Q, L, HEAD_DIM), F32), kshape, kshape, cshape, cshape,
                   jax.ShapeDtypeStruct((N_Q,), F32)],
        grid=(N_KV, L // ATTN_BLOCK),
        in_specs=[bspec, qspec, kspec, kspec, cspec, cspec, sspec, qspec, lspec, qspec],
        out_specs=[qspec, kspec, kspec, cspec, cspec, sspec],
        compiler_params=_cparams(("arbitrary", "arbitrary")),
    )(_attn_bias(), q, k, v, kc, vc, sink, o, lse, do)


def _gm_specs():
    rspec = pl.BlockSpec((1, GM_CHUNK, GM_W), lambda n: (0, n, 0))
    wspec = pl.BlockSpec((GM_GROUPS, GM_CHUNK, GM_CHUNK), lambda n: (0, 0, 0))
    bspec = pl.BlockSpec((GM_GROUPS, GM_CHUNK, 1), lambda n: (0, 0, 0))
    return rspec, wspec, bspec


def _gm_spatial_fwd(u, v, ws, bs):
    rspec, wspec, bspec = _gm_specs()

    def body(u_ref, v_ref, ws_ref, bs_ref, o_ref):
        for g in range(GM_GROUPS):
            cols = slice(g * GM_CHUNK, (g + 1) * GM_CHUNK)
            s = jnp.dot(ws_ref[g], v_ref[0, :, cols], preferred_element_type=F32) + bs_ref[g]
            o_ref[0, :, cols] = (u_ref[0, :, cols] * s).astype(o_ref.dtype)

    return pl.pallas_call(
        body, name="gm_spatial_fwd", out_shape=jax.ShapeDtypeStruct((1, L, GM_W), BF16),
        grid=(L // GM_CHUNK,), in_specs=[rspec, rspec, wspec, bspec], out_specs=rspec,
        compiler_params=_cparams(("parallel",)),
    )(u, v, ws, bs)


def _gm_spatial_bwd(u, v, ws, bs, dus):
    rspec, wspec, bspec = _gm_specs()
    tn = (((0,), (0,)), ((), ()))
    nt = (((1,), (1,)), ((), ()))

    def body(u_ref, v_ref, ws_ref, bs_ref, d_ref, du_ref, dv_ref, dws_ref, dbs_ref):
        n = pl.program_id(0)

        @pl.when(n == 0)
        def _():
            dws_ref[...] = jnp.zeros_like(dws_ref)
            dbs_ref[...] = jnp.zeros_like(dbs_ref)

        for g in range(GM_GROUPS):
            cols = slice(g * GM_CHUNK, (g + 1) * GM_CHUNK)
            vb = v_ref[0, :, cols]
            s = jnp.dot(ws_ref[g], vb, preferred_element_type=F32) + bs_ref[g]
            d = d_ref[0, :, cols].astype(F32)
            du_ref[0, :, cols] = d * s
            ds = d * u_ref[0, :, cols]
            dsb = ds.astype(BF16)
            dv_ref[0, :, cols] = lax.dot_general(ws_ref[g], dsb, tn, preferred_element_type=F32)
            dws_ref[g] += lax.dot_general(dsb, vb, nt, preferred_element_type=F32)
            dbs_ref[g] += jnp.sum(ds, axis=1, keepdims=True)

    row = jax.ShapeDtypeStruct((1, L, GM_W), F32)
    return pl.pallas_call(
        body, name="gm_spatial_bwd",
        out_shape=[row, row, jax.ShapeDtypeStruct((GM_GROUPS, GM_CHUNK, GM_CHUNK), F32),
                   jax.ShapeDtypeStruct((GM_GROUPS, GM_CHUNK, 1), F32)],
        grid=(L // GM_CHUNK,), in_specs=[rspec, rspec, wspec, bspec, rspec],
        out_specs=[rspec, rspec, wspec, bspec],
        compiler_params=_cparams(("arbitrary",)),
    )(u, v, ws, bs, dus)


def _loss_head(h, target):
    tm = 256

    def body(h_ref, t_ref, dh_ref, loss_ref):
        d = h_ref[0] - t_ref[0]
        dh_ref[0] = d * (1.0 / D)

        @pl.when(pl.program_id(0) == 0)
        def _():
            loss_ref[...] = jnp.zeros_like(loss_ref)

        loss_ref[...] += jnp.sum(d * d) * (0.5 / D)

    spec = pl.BlockSpec((1, tm, D), lambda i: (0, i, 0))
    dh, loss = pl.pallas_call(
        body, name="loss_head",
        out_shape=[jax.ShapeDtypeStruct((1, L, D), F32), jax.ShapeDtypeStruct((8, 128), F32)],
        grid=(L // tm,), in_specs=[spec, spec],
        out_specs=[spec, pl.BlockSpec((8, 128), lambda i: (0, 0))],
        compiler_params=_cparams(("arbitrary",)),
    )(h, target)
    return dh, loss[0, 0]


def _adamw(parts, w, m, v, name):
    per_layer = isinstance(parts, (list, tuple))
    plist = list(parts) if per_layer else [parts]
    nl = len(plist) if per_layer else parts.shape[0]
    s, r, c = plist[0].shape[-3:]
    tr = r
    for cand in (512, 256, 128, 64, 32, 16):
        if r % cand == 0 and cand * c <= 131072:
            tr = cand
            break
    nr = r // tr
    npart = len(plist)
    c1 = 1.0 / (1.0 - ADAM_B1 ** ADAM_STEP)
    c2 = 1.0 / (1.0 - ADAM_B2 ** ADAM_STEP)

    def body(*refs):
        w_ref, m_ref, v_ref, g_ref, d_ref, nm_ref, nv_ref = refs[npart:]

        def update(read):
            g = read(0).astype(F32)
            for q in range(1, s):
                g = g + read(q).astype(F32)
            mn = ADAM_B1 * m_ref[0] + (1.0 - ADAM_B1) * g
            vn = ADAM_B2 * v_ref[0] + (1.0 - ADAM_B2) * (g * g)
            g_ref[0] = g
            nm_ref[0] = mn
            nv_ref[0] = vn
            d_ref[0] = -ADAM_LR * ((mn * c1) / (jnp.sqrt(vn * c2) + ADAM_EPS) + ADAM_WD * w_ref[0])

        if not per_layer:
            update(lambda q: refs[0][0, q])
        else:
            for l in range(nl):
                @pl.when(pl.program_id(0) == l)
                def _(l=l):
                    update(lambda q: refs[l][q])

    spec = pl.BlockSpec((1, tr, c), lambda li, i: (li, i, 0))
    shp = jax.ShapeDtypeStruct((nl, r, c), F32)
    if per_layer:
        pspecs = [pl.BlockSpec((s, tr, c), lambda li, i, l=l: (0, jnp.where(li == l, i, jnp.where(li > l, nr - 1, 0)), 0))
                  for l in range(nl)]
    else:
        pspecs = [pl.BlockSpec((1, s, tr, c), lambda li, i: (li, 0, i, 0))]
    return pl.pallas_call(
        body, name=name, out_shape=[shp] * 4, grid=(nl, nr),
        in_specs=pspecs + [spec, spec, spec], out_specs=[spec] * 4,
        compiler_params=_cparams(("arbitrary", "arbitrary")),
    )(*plist, w, m, v)


def _pack_rows(vecs, lanes=128, mult=8):
    flat = jnp.concatenate([v.reshape(-1) for v in vecs])
    n = flat.shape[0]
    rows = -(-n // (mult * lanes)) * mult
    return jnp.pad(flat, (0, rows * lanes - n)).reshape(rows, lanes)


def _unpack_rows(packed, shapes):
    flat = packed.reshape(-1)
    out, pos = [], 0
    for s in shapes:
        n = 1
        for d_ in s:
            n *= d_
        out.append(flat[pos:pos + n].reshape(s))
        pos += n
    return out


def _unshard_last(g):
    lead = g.shape[1:-1]
    return jnp.moveaxis(g, 0, -2).reshape(*lead, NDEV * g.shape[-1])


def _shard_last(full):
    lead, w = full.shape[:-1], full.shape[-1] // NDEV
    return jnp.moveaxis(full.reshape(*lead, NDEV, w), -2, 0)


def _rope_tables():
    rows = L // GRID_W
    row = jnp.repeat(jnp.arange(rows), GRID_W).astype(F32)
    col = jnp.tile(jnp.arange(GRID_W), rows).astype(F32)
    axis_dim = HEAD_DIM // 2
    inv_freq = ROPE_BASE ** (-jnp.arange(0, axis_dim, 2, dtype=F32) / axis_dim)
    ang_r, ang_c = row[:, None] * inv_freq[None, :], col[:, None] * inv_freq[None, :]
    ang = jnp.concatenate([ang_r, ang_r, ang_c, ang_c], axis=-1)
    ang = jnp.concatenate([ang, ang], axis=-1)[None]
    return jnp.cos(ang), jnp.sin(ang)


def _heads(x, nh):
    t = x.shape[1]
    return x.reshape(t, nh, HEAD_DIM).transpose(1, 0, 2)


def _unheads(x):
    nh, t, _ = x.shape
    return x.transpose(1, 0, 2).reshape(1, t, nh * HEAD_DIM)


FFN_HALO = 16
FFN_PAIRS = 4


def _ffn_tile(t):
    return 512 if t == L else 256


def _halo_specs(t, tm, block, index):
    per, last = tm // FFN_HALO, t // FFN_HALO - 1
    return [pl.BlockSpec(block(tm), lambda d, i: index(d, i)),
            pl.BlockSpec(block(FFN_HALO), lambda d, i: index(d, jnp.maximum(i * per - 1, 0))),
            pl.BlockSpec(block(FFN_HALO), lambda d, i: index(d, jnp.minimum((i + 1) * per, last)))]


def _seg_edges(i, tm, t):
    return (i * tm == 0) | (i * tm == L), ((i + 1) * tm == L) | ((i + 1) * tm == t)


FFN_RC = 8


def _sigmoid_t(x):
    return 0.5 * jnp.tanh(0.5 * x) + 0.5


class _RowShifts:
    def __init__(self, buf, s):
        self.buf, self.s, self.memo = buf, s, {}
        rows = lax.broadcasted_iota(jnp.int32, (FFN_RC, FFN_BLK), 0)
        self.first, self.last = rows == 0, rows == FFN_RC - 1

    def chunk(self, r):
        if r not in self.memo:
            self.memo[r] = self.buf[self.s, r:r + FFN_RC, :]
        return self.memo[r]

    def rot(self, r, by):
        if (r, by) not in self.memo:
            self.memo[(r, by)] = pltpu.roll(self.chunk(r), by, 0)
        return self.memo[(r, by)]

    def triple(self, r):
        before = jnp.where(self.first, self.rot(r - FFN_RC, 1), self.rot(r, 1))
        behind = jnp.where(self.last, self.rot(r + FFN_RC, FFN_RC - 1), self.rot(r, FFN_RC - 1))
        return before, self.chunk(r), behind


def _conv3_of(triple, cw_ref, s, flip=False):
    taps = [cw_ref[s, 0, k:k + 1, :] for k in ((2, 1, 0) if flip else (0, 1, 2))]
    return taps[0] * triple[0] + taps[1] * triple[1] + taps[2] * triple[2]


def _ffn_core_fwd(a2, up, cw, cb, down, name):
    t = a2.shape[1]
    tm = _ffn_tile(t)
    h0 = FFN_HALO

    def body(a_ref, ap_ref, an_ref, up_ref, cw_ref, cb_ref, dn_ref, z_ref, f_ref, abuf, zbuf, ubuf):
        d, i = pl.program_id(0), pl.program_id(1)
        seg_first, seg_last = _seg_edges(i, tm, t)
        abuf[0:h0, :] = ap_ref[0]
        abuf[h0:h0 + tm, :] = a_ref[0]
        abuf[h0 + tm:, :] = an_ref[0]
        for s in range(2):
            zbuf[s] = jnp.dot(abuf[...], up_ref[s, 0], preferred_element_type=F32)

        @pl.when(seg_first)
        def _():
            zbuf[:, 0:h0, :] = jnp.zeros((2, h0, FFN_BLK), F32)

        @pl.when(seg_last)
        def _():
            zbuf[:, h0 + tm:, :] = jnp.zeros((2, h0, FFN_BLK), F32)

        for s in range(2):
            z_ref[s, 0] = zbuf[s, h0:h0 + tm, :].astype(z_ref.dtype)

        zs = [_RowShifts(zbuf, 0), _RowShifts(zbuf, 1)]
        for r in range(h0, h0 + tm, FFN_RC):
            zg = _conv3_of(zs[0].triple(r), cw_ref, 0) + cb_ref[0, 0]
            zv = _conv3_of(zs[1].triple(r), cw_ref, 1) + cb_ref[1, 0]
            ubuf[r - h0:r - h0 + FFN_RC, :] = zg * _sigmoid_t(zg) * zv
        prod = jnp.dot(ubuf[...].astype(BF16), dn_ref[0], preferred_element_type=F32)
        rows = pl.ds(pl.multiple_of(i * tm, tm), tm)

        @pl.when(d == 0)
        def _():
            f_ref[0, rows, :] = prod

        @pl.when(d > 0)
        def _():
            f_ref[0, rows, :] += prod

    pair = lambda r, c: pl.BlockSpec((2, 1, r, c), lambda d, i: (0, d, 0, 0))
    return pl.pallas_call(
        body, name=name,
        out_shape=[jax.ShapeDtypeStruct((2, FFN_PAIRS, t, FFN_BLK), BF16), jax.ShapeDtypeStruct((1, t, D), F32)],
        grid=(FFN_PAIRS, t // tm),
        in_specs=_halo_specs(t, tm, lambda r: (1, r, D), lambda d, i: (0, i, 0))
        + [pair(D, FFN_BLK), pair(FFN_K, FFN_BLK), pair(1, FFN_BLK),
           pl.BlockSpec((1, FFN_BLK, D), lambda d, i: (d, 0, 0))],
        out_specs=[pl.BlockSpec((2, 1, tm, FFN_BLK), lambda d, i: (0, d, i, 0)),
                   pl.BlockSpec((1, t, D), lambda d, i: (0, 0, 0))],
        scratch_shapes=[pltpu.VMEM((tm + 2 * h0, D), BF16), pltpu.VMEM((2, tm + 2 * h0, FFN_BLK), F32),
                        pltpu.VMEM((tm, FFN_BLK), F32)],
        compiler_params=_cparams(("arbitrary", "arbitrary")),
    )(a2, a2, a2, up, cw, cb, down)


def _ffn_core_bwd(df, z, cw, cb, down, up, a2, name):
    t = df.shape[1]
    tm = _ffn_tile(t)
    h0 = FFN_HALO
    ni = t // tm
    w0, wn = h0 // 2, tm + h0
    tn = (((0,), (0,)), ((), ()))
    nt = (((1,), (1,)), ((), ()))

    def body(df_ref, dfp_ref, dfn_ref, z_ref, zp_ref, zn_ref, cw_ref, cb_ref, dn_ref, up_ref, a2_ref,
             dcw_ref, dcb_ref, ddn_ref, da_ref, dup_ref, dfbuf, zbuf, dzbuf, acc, dubuf, dzo, acc_up):
        d, i = pl.program_id(0), pl.program_id(1)
        seg_first, seg_last = _seg_edges(i, tm, t)
        dfbuf[0:h0, :] = dfp_ref[0]
        dfbuf[h0:h0 + tm, :] = df_ref[0]
        dfbuf[h0 + tm:, :] = dfn_ref[0]
        for s in range(2):
            zbuf[s, 0:h0, :] = zp_ref[s, 0].astype(F32)
            zbuf[s, h0:h0 + tm, :] = z_ref[s, 0].astype(F32)
            zbuf[s, h0 + tm:, :] = zn_ref[s, 0].astype(F32)

        @pl.when(seg_first)
        def _():
            zbuf[:, 0:h0, :] = jnp.zeros((2, h0, FFN_BLK), F32)

        @pl.when(seg_last)
        def _():
            zbuf[:, h0 + tm:, :] = jnp.zeros((2, h0, FFN_BLK), F32)

        dubuf[...] = lax.dot_general(dfbuf[...], dn_ref[0], nt, preferred_element_type=F32)

        zs = [_RowShifts(zbuf, 0), _RowShifts(zbuf, 1)]
        sums = [[jnp.zeros((FFN_RC, FFN_BLK), F32)] * (FFN_K + 1) for _ in range(2)]
        for r in range(w0, w0 + wn, FFN_RC):
            tz = [zs[0].triple(r), zs[1].triple(r)]
            zg = _conv3_of(tz[0], cw_ref, 0) + cb_ref[0, 0]
            zv = _conv3_of(tz[1], cw_ref, 1) + cb_ref[1, 0]
            sg = _sigmoid_t(zg)
            silu = zg * sg
            du = dubuf[r:r + FFN_RC, :]
            dzc = [du * zv * (sg * (1.0 + zg * (1.0 - sg))), du * silu]
            dzbuf[0, r:r + FFN_RC, :] = dzc[0]
            dzbuf[1, r:r + FFN_RC, :] = dzc[1]
            dubuf[r:r + FFN_RC, :] = silu * zv
            if h0 <= r < h0 + tm:
                for s in range(2):
                    sums[s] = [sums[s][k] + dzc[s] * tz[s][k] for k in range(FFN_K)] + [sums[s][FFN_K] + dzc[s]]

        @pl.when(seg_first)
        def _():
            dzbuf[:, w0:h0, :] = jnp.zeros((2, h0 - w0, FFN_BLK), F32)

        @pl.when(seg_last)
        def _():
            dzbuf[:, h0 + tm:w0 + wn, :] = jnp.zeros((2, w0, FFN_BLK), F32)

        @pl.when(i == 0)
        def _():
            dcw_ref[...] = jnp.zeros_like(dcw_ref)
            dcb_ref[...] = jnp.zeros_like(dcb_ref)

        da = None
        for s in range(2):
            dzs = _RowShifts(dzbuf, s)
            for r in range(h0, h0 + tm, FFN_RC):
                dzo[r - h0:r - h0 + FFN_RC, :] = _conv3_of(dzs.triple(r), cw_ref, s, flip=True)
            dzb = dzo[...].astype(BF16)
            part = lax.dot_general(dzb, up_ref[s, 0], nt, preferred_element_type=F32)
            da = part if da is None else da + part
            gup = lax.dot_general(dzb, a2_ref[0], tn, preferred_element_type=F32)

            @pl.when(i == 0)
            def _(s=s, gup=gup):
                acc_up[s] = gup

            @pl.when(i > 0)
            def _(s=s, gup=gup):
                acc_up[s] += gup
            for k in range(FFN_K):
                dcw_ref[s, 0, k:k + 1, :] += jnp.sum(sums[s][k], axis=0, keepdims=True)
            dcb_ref[s, 0] += jnp.sum(sums[s][FFN_K], axis=0, keepdims=True)
        rows = pl.ds(pl.multiple_of(i * tm, tm), tm)

        @pl.when(d == 0)
        def _():
            da_ref[0, rows, :] = da

        @pl.when(d > 0)
        def _():
            da_ref[0, rows, :] += da

        prod = lax.dot_general(dubuf[h0:h0 + tm, :].astype(BF16), dfbuf[h0:h0 + tm, :], tn, preferred_element_type=F32)

        @pl.when(i == 0)
        def _():
            acc[...] = prod

        @pl.when(i > 0)
        def _():
            acc[...] += prod

        @pl.when(i == ni - 1)
        def _():
            ddn_ref[0] = acc[...].astype(ddn_ref.dtype)
            dup_ref[:, 0] = acc_up[...].astype(dup_ref.dtype)

    pair = lambda r, c: pl.BlockSpec((2, 1, r, c), lambda d, i: (0, d, 0, 0))
    return pl.pallas_call(
        body, name=name,
        out_shape=[jax.ShapeDtypeStruct((2, FFN_PAIRS, FFN_K, FFN_BLK), F32),
                   jax.ShapeDtypeStruct((2, FFN_PAIRS, 1, FFN_BLK), F32),
                   jax.ShapeDtypeStruct((FFN_PAIRS, FFN_BLK, D), BF16), jax.ShapeDtypeStruct((1, t, D), F32),
                   jax.ShapeDtypeStruct((2, FFN_PAIRS, FFN_BLK, D), BF16)],
        grid=(FFN_PAIRS, ni),
        in_specs=_halo_specs(t, tm, lambda r: (1, r, D), lambda d, i: (0, i, 0))
        + _halo_specs(t, tm, lambda r: (2, 1, r, FFN_BLK), lambda d, i: (0, d, i, 0))
        + [pair(FFN_K, FFN_BLK), pair(1, FFN_BLK), pl.BlockSpec((1, FFN_BLK, D), lambda d, i: (d, 0, 0)),
           pair(D, FFN_BLK), pl.BlockSpec((1, tm, D), lambda d, i: (0, i, 0))],
        out_specs=[pair(FFN_K, FFN_BLK),
                   pair(1, FFN_BLK), pl.BlockSpec((1, FFN_BLK, D), lambda d, i: (d, 0, 0)),
                   pl.BlockSpec((1, t, D), lambda d, i: (0, 0, 0)), pair(FFN_BLK, D)],
        scratch_shapes=[pltpu.VMEM((tm + 2 * h0, D), BF16), pltpu.VMEM((2, tm + 2 * h0, FFN_BLK), F32),
                        pltpu.VMEM((2, tm + 2 * h0, FFN_BLK), F32), pltpu.VMEM((FFN_BLK, D), F32),
                        pltpu.VMEM((tm + 2 * h0, FFN_BLK), F32),
                        pltpu.VMEM((tm, FFN_BLK), F32), pltpu.VMEM((2, FFN_BLK, D), F32)],
        compiler_params=_cparams(("arbitrary", "arbitrary")),
    )(df, df, df, z, z, z, cw, cb, down, up, a2)


def _ffn_fwd(i, h, mod, ng, wts):
    a2 = _rw_fwd(f_modnorm, [(h, 0)], [(ng[2], "one"), (mod["sc2"], "seg"), (mod["sh2"], "seg")],
                 [(D, BF16)], name=f"ffn{i}_norm")[0]
    z, f = _ffn_core_fwd(a2, wts["up"], wts["cw"], wts["cb"], wts["down"], f"ffn{i}_core")
    h2 = _rw_fwd(f_resgate, [(h, 0), (f, 0)], [(mod["g2"], "seg"), (ng[3], "one")], [(D, F32)],
                 name=f"ffn{i}_res")[0]
    return h2, (h, a2, z, f)


def _ffn_bwd(i, dh, res, mod, ng, wts):
    h, a2, z, f = res
    t = h.shape[1]
    df, dg2, dng3 = _rw_bwd(f_gate_rms, [(f, 0)], [(mod["g2"], "seg"), (ng[3], "one")], [dh],
                            name=f"ffn{i}_res_b", row_grad=[(0, BF16)], param_grad=[0, 1])
    dcw, dcb, d_down, da2, d_up = _ffn_core_bwd(df, z, wts["cw"], wts["cb"], wts["down"], wts["up"], a2,
                                                f"ffn{i}_core_b")
    dcw, dcb = dcw.reshape(NDEV, FFN_K, FFN_BLK), dcb.reshape(NDEV, 1, FFN_BLK)
    d_up = d_up.reshape(NDEV, FFN_BLK, D)
    dh_in, dng2, dsc2, dsh2 = _rw_bwd(
        f_modnorm, [(h, 0)], [(ng[2], "one"), (mod["sc2"], "seg"), (mod["sh2"], "seg")], [da2],
        name=f"ffn{i}_norm_b", row_grad=[(0, F32)], param_grad=[0, 1, 2], add=dh)
    grads = dict(up=d_up, down=d_down, cw=dcw, cb=dcb, ng2=dng2, ng3=dng3, sc2=dsc2, sh2=dsh2, g2=dg2)
    return dh_in, grads


def _mixer_norm_fwd(i, h, mod, ng):
    return _rw_fwd(f_modnorm, [(h, 0)], [(ng[0], "one"), (mod["sc1"], "seg"), (mod["sh1"], "seg")],
                   [(D, BF16)], name=f"mix{i}_norm")[0]


def _mixer_norm_bwd(i, h, mod, ng, da, dh):
    return _rw_bwd(f_modnorm, [(h, 0)], [(ng[0], "one"), (mod["sc1"], "seg"), (mod["sh1"], "seg")], [da],
                   name=f"mix{i}_norm_b", row_grad=[(0, F32)], param_grad=[0, 1, 2], add=dh)


def _conformer_fwd(i, h, mod, ng, wts):
    a = _mixer_norm_fwd(i, h, mod, ng)
    p = _mm(a, wts["w_in"], "nn", out_dtype=BF16, name=f"cm{i}_in")
    z = _rw_fwd(f_glu, [(p, 0, 2)], [(wts["b_in"], "one", 2)], [(D, F32)], name=f"cm{i}_glu")[0]
    zc = _dwconv(z, wts["dw_w"], wts["dw_b"], name=f"cm{i}_conv")
    r = _rw_fwd(f_lnsilu, [(zc, 0)], [(wts["ln_g"], "one"), (wts["ln_b"], "one")], [(D, BF16)],
                name=f"cm{i}_ln")[0]
    y = _mm(r, wts["w_out"], "nn", name=f"cm{i}_out")
    h2 = _rw_fwd(f_resgate_bias, [(h, 0), (y, 0)], [(mod["g1"], "seg"), (ng[1], "one"), (wts["b_out"], "one")],
                 [(D, F32)], name=f"cm{i}_res")[0]
    return h2, (h, a, p, z, zc, r, y)


def _conformer_bwd(i, dh, res, mod, ng, wts):
    h, a, p, z, zc, r, y = res
    dy, dg1, dng1, db_out = _rw_bwd(
        f_gate_rms_bias, [(y, 0)], [(mod["g1"], "seg"), (ng[1], "one"), (wts["b_out"], "one")], [dh],
        name=f"cm{i}_res_b", row_grad=[(0, BF16)], param_grad=[0, 1, 2])
    dr = _mm(dy, wts["w_out"], "nt", name=f"cm{i}_out_bx")
    d_w_out = _mm(r, dy, "tn", out_dtype=BF16, name=f"cm{i}_out_bw")
    dzc, dln_g, dln_b = _rw_bwd(f_lnsilu, [(zc, 0)], [(wts["ln_g"], "one"), (wts["ln_b"], "one")], [dr],
                                name=f"cm{i}_ln_b", row_grad=[(0, F32)], param_grad=[0, 1])
    ddw_w, ddw_b = _dwconv_wgrad(z, dzc, CM_K, name=f"cm{i}_conv_bw")
    dz = _dwconv(dzc, wts["dw_w"][:, ::-1, :], None, name=f"cm{i}_conv_bx")
    dp, db_in = _rw_bwd(f_glu, [(p, 0, 2)], [(wts["b_in"], "one", 2)], [dz], name=f"cm{i}_glu_b",
                        row_grad=[(0, BF16)], param_grad=[0])
    d_w_in = _mm(a, dp, "tn", out_dtype=BF16, name=f"cm{i}_in_bw")
    da = _mm(dp, wts["w_in"], "nt", name=f"cm{i}_in_bx")
    dh_in, dng0, dsc1, dsh1 = _mixer_norm_bwd(i, h, mod, ng, da, dh)
    grads = dict(w_in=d_w_in, w_out=d_w_out, b_in=db_in, dw_w=ddw_w, dw_b=ddw_b, ln_g=dln_g, ln_b=dln_b,
                 b_out=db_out, ng0=dng0, ng1=dng1, sc1=dsc1, sh1=dsh1, g1=dg1)
    return dh_in, grads


def _attention_fwd(i, h_all, mod, ng, wts, tables):
    a = _mixer_norm_fwd(i, h_all, mod, ng)
    qkv = _mm(a, wts["w_qkv"], "nn", name="attn_qkv")
    kv0 = N_Q * HEAD_DIM
    kv1 = kv0 + N_KV * HEAD_DIM
    q = _rope(qkv, tables, False, BF16, "attn_rope_q", cols=(kv0, 0))
    k = _rope(qkv, tables, False, BF16, "attn_rope_k", cols=(kv1 - kv0, kv0 // (kv1 - kv0)))
    pad = ((0, 0), (ATTN_BLOCK, ATTN_BLOCK), (0, 0))
    q_h = _heads(q, N_Q)
    k_h = jnp.pad(_heads(k, N_KV), pad)
    v_h = jnp.pad(_heads(qkv[:, :L, kv1:].astype(BF16), N_KV), pad)
    kc_h = _heads(qkv[:, L:, kv0:kv1].astype(BF16), N_KV)
    vc_h = _heads(qkv[:, L:, kv1:].astype(BF16), N_KV)
    o_h, lse = _attn_fwd(q_h, k_h, v_h, kc_h, vc_h, wts["sink"])
    o = _unheads(o_h)
    y = _mm(o, wts["w_o"], "nn", name="attn_o")
    h_lat = h_all[:, :L]
    mod_lat = {k_: v_[:1] for k_, v_ in mod.items()}
    h2 = _rw_fwd(f_resgate, [(h_lat, 0), (y, 0)], [(mod_lat["g1"], "seg"), (ng[1], "one")], [(D, F32)],
                 name="attn_res")[0]
    return h2, (h_all, a, q_h, k_h, v_h, kc_h, vc_h, o_h, lse, o, y)


def _attention_bwd(i, dh, res, mod, ng, wts, tables):
    h_all, a, q_h, k_h, v_h, kc_h, vc_h, o_h, lse, o, y = res
    mod_lat = {k_: v_[:1] for k_, v_ in mod.items()}
    dy, dg1, dng1 = _rw_bwd(f_gate_rms, [(y, 0)], [(mod_lat["g1"], "seg"), (ng[1], "one")], [dh],
                            name="attn_res_b", row_grad=[(0, BF16)], param_grad=[0, 1])
    do = _mm(dy, wts["w_o"], "nt", name="attn_o_bx")
    d_w_o = _mm(o, dy, "tn", out_dtype=BF16, name="attn_o_bw")
    dq_h, dk_h, dv_h, dkc_h, dvc_h, dsink = _attn_bwd(q_h, k_h, v_h, kc_h, vc_h, wts["sink"], o_h, lse,
                                                        _heads(do, N_Q))
    dq = _rope(_unheads(dq_h), tables, True, BF16, "attn_rope_q_b")
    dk = _rope(_unheads(dk_h[:, ATTN_BLOCK:-ATTN_BLOCK]), tables, True, BF16, "attn_rope_k_b")
    dv = _unheads(dv_h[:, ATTN_BLOCK:-ATTN_BLOCK]).astype(BF16)
    d_lat = jnp.concatenate([dq, dk, dv], axis=2)
    d_ctx = jnp.concatenate([jnp.zeros((1, LC, N_Q * HEAD_DIM), BF16), _unheads(dkc_h).astype(BF16),
                             _unheads(dvc_h).astype(BF16)], axis=2)
    dqkv = jnp.concatenate([d_lat, d_ctx], axis=1)
    d_w_qkv = _mm(a, dqkv, "tn", out_dtype=BF16, name="attn_qkv_bw")
    da = _mm(dqkv, wts["w_qkv"], "nt", name="attn_qkv_bx")
    dh_res = jnp.concatenate([dh, jnp.zeros((1, LC, D), F32)], axis=1)
    dh_in, dng0, dsc1, dsh1 = _mixer_norm_bwd(i, h_all, mod, ng, da, dh_res)
    grads = dict(w_qkv=d_w_qkv, w_o=d_w_o, sink=dsink, ng0=dng0, ng1=dng1, sc1=dsc1, sh1=dsh1, g1=dg1)
    return dh_in, grads


def _gmlp_fwd(i, h, mod, ng, wts):
    a = _mixer_norm_fwd(i, h, mod, ng)
    p = _mm(a, wts["w_in"], "nn", out_dtype=BF16, name="gm_in")
    u, v = _rw_fwd(f_gmlp_pre, [(p, 0, 2)], [(wts["b_in"], "one", 2), (wts["ln_g"], "one"), (wts["ln_b"], "one")],
                   [(GM_W, F32), (GM_W, BF16)], name="gm_pre")
    us = _gm_spatial_fwd(u, v, wts["w_s"], wts["b_s"])
    y = _mm(us, wts["w_out"], "nn", name="gm_out")
    h2 = _rw_fwd(f_resgate, [(h, 0), (y, 0)], [(mod["g1"], "seg"), (ng[1], "one")], [(D, F32)],
                 name="gm_res")[0]
    return h2, (h, a, p, u, v, us, y)


def _gmlp_bwd(i, dh, res, mod, ng, wts):
    h, a, p, u, v, us, y = res
    dy, dg1, dng1 = _rw_bwd(f_gate_rms, [(y, 0)], [(mod["g1"], "seg"), (ng[1], "one")], [dh],
                            name="gm_res_b", row_grad=[(0, BF16)], param_grad=[0, 1])
    dus = _mm(dy, wts["w_out"], "nt", name="gm_out_bx")
    d_w_out = _mm(us, dy, "tn", out_dtype=BF16, name="gm_out_bw")
    du, dv, dws, dbs = _gm_spatial_bwd(u, v, wts["w_s"], wts["b_s"], dus)
    dp, db_in, dln_g, dln_b = _rw_bwd(
        f_gmlp_pre, [(p, 0, 2)], [(wts["b_in"], "one", 2), (wts["ln_g"], "one"), (wts["ln_b"], "one")], [du, dv],
        name="gm_pre_b", row_grad=[(0, BF16)], param_grad=[0, 1, 2])
    d_w_in = _mm(a, dp, "tn", out_dtype=BF16, name="gm_in_bw")
    da = _mm(dp, wts["w_in"], "nt", name="gm_in_bx")
    dh_in, dng0, dsc1, dsh1 = _mixer_norm_bwd(i, h, mod, ng, da, dh)
    grads = dict(w_in=d_w_in, w_out=d_w_out, b_in=db_in, ln_g=dln_g, ln_b=dln_b, w_s=dws, b_s=dbs,
                 ng0=dng0, ng1=dng1, sc1=dsc1, sh1=dsh1, g1=dg1)
    return dh_in, grads


MOD_NAMES = ("sh1", "sc1", "g1", "sh2", "sc2", "g2")
SMALL = (
    ("norm_g", (4, 4, 128)), ("ffn_conv_w", (4, 3, 704)), ("cm_b_in", (2, 256)), ("cm_dw_w", (2, 31, 128)),
    ("cm_dw_b", (2, 128)), ("cm_ln_g", (2, 128)), ("cm_ln_b", (2, 128)), ("cm_b_out", (2, 128)),
    ("gm_b_in", (1, 512)), ("gm_ln_g", (1, 256)), ("gm_ln_b", (1, 256)))


def _mixer_weights(i, P):
    if i % 3 == 0:
        j = i // 3
        return dict(w_in=P["cm_w_in"][j], w_out=P["cm_w_out"][j], b_in=P["cm_b_in"][j].reshape(1, 1, 2 * D),
                    dw_w=P["cm_dw_w"][j][None], dw_b=P["cm_dw_b"][j].reshape(1, 1, D),
                    ln_g=P["cm_ln_g"][j].reshape(1, 1, D), ln_b=P["cm_ln_b"][j].reshape(1, 1, D),
                    b_out=P["cm_b_out"][j].reshape(1, 1, D))
    if i % 3 == 1:
        return dict(w_qkv=P["attn_w_qkv"], w_o=P["attn_w_o"], sink=P["attn_sink"].reshape(N_Q))
    return dict(w_in=P["gm_w_in"], w_out=P["gm_w_out"], b_in=P["gm_b_in"].reshape(1, 1, 2 * GM_W),
                ln_g=P["gm_ln_g"].reshape(1, 1, GM_W), ln_b=P["gm_ln_b"].reshape(1, 1, GM_W),
                w_s=P["gm_w_s"].reshape(GM_GROUPS, GM_CHUNK, GM_CHUNK).astype(BF16),
                b_s=P["gm_b_s"].reshape(GM_GROUPS, GM_CHUNK, 1))


def _ffn_weights(i, P):
    return dict(up=P["ffn_w_up"][i].reshape(2, FFN_PAIRS, D, FFN_BLK), down=P["ffn_w_down"][i],
                cw=P["ffn_conv_w"][i].reshape(2, FFN_PAIRS, FFN_K, FFN_BLK),
                cb=P["ffn_conv_b"][i].reshape(2, FFN_PAIRS, 1, FFN_BLK))


def _local_step(x, ctx, target, lat_mod, ctx_mod, norm_g, layer_weights, grads_ready):
    tables = _rope_tables()
    ng = [[norm_g[i, j].reshape(1, 1, D) for j in range(4)] for i in range(DEPTH)]

    def mods(i, with_ctx, token):
        out = {}
        for j, nme in enumerate(MOD_NAMES):
            rows = [lat_mod[i, j]] + ([ctx_mod[i, j]] if with_ctx else [])
            out[nme] = jnp.stack(rows).reshape(len(rows), 1, D) + token[0, 0]
        return out

    def after(mod, token):
        return mod if token is None else {k_: v_ + token[0, 0] for k_, v_ in mod.items()}

    h_all = jnp.concatenate([x, ctx], axis=1)
    wm0, wf0, tok = layer_weights(0, h_all)
    m0 = mods(0, True, tok)
    h, r0m = _conformer_fwd(0, h_all, m0, ng[0], wm0)
    wf0 = wf0(h) if callable(wf0) else wf0
    h, r0f = _ffn_fwd(0, h, m0, ng[0], wf0)
    wm1, wf1, tok = layer_weights(1, h)
    m1 = mods(1, True, tok)
    m1l = {k_: v_[:1] for k_, v_ in m1.items()}
    h, r1m = _attention_fwd(1, h, m1, ng[1], wm1, tables)
    wf1 = wf1(h) if callable(wf1) else wf1
    h, r1f = _ffn_fwd(1, h, m1l, ng[1], wf1)
    wm2, wf2, tok = layer_weights(2, h)
    m2 = mods(2, False, tok)
    h, r2m = _gmlp_fwd(2, h, m2, ng[2], wm2)
    wf2 = wf2(h) if callable(wf2) else wf2
    h, r2f = _ffn_fwd(2, h, m2, ng[2], wf2)
    wm3, wf3, tok = layer_weights(3, h)
    m3 = mods(3, False, tok)
    h, r3m = _conformer_fwd(3, h, m3, ng[3], wm3)
    wf3 = wf3(h) if callable(wf3) else wf3
    h, r3f = _ffn_fwd(3, h, m3, ng[3], wf3)
    dh, loss = _loss_head(h, target)

    G = {}
    dh, G["f3"] = _ffn_bwd(3, dh, r3f, m3, ng[3], wf3)
    tok = grads_ready("f3", G["f3"])
    dh, G["m3"] = _conformer_bwd(3, dh, r3m, after(m3, tok), ng[3], wm3)
    tok = grads_ready("m3", G["m3"])
    dh, G["f2"] = _ffn_bwd(2, dh, r2f, after(m2, tok), ng[2], wf2)
    tok = grads_ready("f2", G["f2"])
    dh, G["m2"] = _gmlp_bwd(2, dh, r2m, after(m2, tok), ng[2], wm2)
    tok = grads_ready("m2", G["m2"])
    dh, G["f1"] = _ffn_bwd(1, dh, r1f, after(m1l, tok), ng[1], wf1)
    tok = grads_ready("f1", G["f1"])
    dh, G["m1"] = _attention_bwd(1, dh, r1m, after(m1, tok), ng[1], wm1, tables)
    tok = grads_ready("m1", G["m1"])
    dh, G["f0"] = _ffn_bwd(0, dh, r0f, after(m0, tok), ng[0], wf0)
    tok = grads_ready("f0", G["f0"])
    dh, G["m0"] = _conformer_bwd(0, dh, r0m, after(m0, tok), ng[0], wm0)
    grads_ready("m0", G["m0"])
    grad_x = dh[:, :L]

    zero = jnp.zeros((D,), F32)
    dmod = []
    for seg in range(2):
        per_layer = []
        for i in range(DEPTH):
            vals = []
            for nme in MOD_NAMES:
                src = G[("m" if nme.endswith("1") else "f") + str(i)][nme]
                vals.append(src[seg, 0] if src.shape[0] > seg else zero)
            per_layer.append(jnp.concatenate(vals))
        dmod.append(jnp.stack(per_layer))
    dmod = jnp.stack(dmod)
    return loss, grad_x, G, dmod


def kernel(x, c, ctx, c_ctx, ada_w, ada_b, norm_g, ffn_w_up, ffn_conv_w, ffn_conv_b, ffn_w_down, cm_w_in, cm_b_in, cm_dw_w, cm_dw_b, cm_ln_g, cm_ln_b, cm_w_out, cm_b_out, attn_w_qkv, attn_sink, attn_w_o, gm_w_in, gm_b_in, gm_ln_g, gm_ln_b, gm_w_s, gm_b_s, gm_w_out, loss_target, m_c_ctx, m_ada_w, m_ada_b, m_norm_g, m_ffn_w_up, m_ffn_conv_w, m_ffn_conv_b, m_ffn_w_down, m_cm_w_in, m_cm_b_in, m_cm_dw_w, m_cm_dw_b, m_cm_ln_g, m_cm_ln_b, m_cm_w_out, m_cm_b_out, m_attn_w_qkv, m_attn_sink, m_attn_w_o, m_gm_w_in, m_gm_b_in, m_gm_ln_g, m_gm_ln_b, m_gm_w_s, m_gm_b_s, m_gm_w_out, v_c_ctx, v_ada_w, v_ada_b, v_norm_g, v_ffn_w_up, v_ffn_conv_w, v_ffn_conv_b, v_ffn_w_down, v_cm_w_in, v_cm_b_in, v_cm_dw_w, v_cm_dw_b, v_cm_ln_g, v_cm_ln_b, v_cm_w_out, v_cm_b_out, v_attn_w_qkv, v_attn_sink, v_attn_w_o, v_gm_w_in, v_gm_b_in, v_gm_ln_g, v_gm_ln_b, v_gm_w_s, v_gm_b_s, v_gm_w_out):
    W = dict(c_ctx=c_ctx, ada_w=ada_w, ada_b=ada_b, norm_g=norm_g, ffn_w_up=ffn_w_up, ffn_conv_w=ffn_conv_w, ffn_conv_b=ffn_conv_b, ffn_w_down=ffn_w_down, cm_w_in=cm_w_in, cm_b_in=cm_b_in, cm_dw_w=cm_dw_w, cm_dw_b=cm_dw_b, cm_ln_g=cm_ln_g, cm_ln_b=cm_ln_b, cm_w_out=cm_w_out, cm_b_out=cm_b_out, attn_w_qkv=attn_w_qkv, attn_sink=attn_sink, attn_w_o=attn_w_o, gm_w_in=gm_w_in, gm_b_in=gm_b_in, gm_ln_g=gm_ln_g, gm_ln_b=gm_ln_b, gm_w_s=gm_w_s, gm_b_s=gm_b_s, gm_w_out=gm_w_out)
    M = dict(c_ctx=m_c_ctx, ada_w=m_ada_w, ada_b=m_ada_b, norm_g=m_norm_g, ffn_w_up=m_ffn_w_up, ffn_conv_w=m_ffn_conv_w, ffn_conv_b=m_ffn_conv_b, ffn_w_down=m_ffn_w_down, cm_w_in=m_cm_w_in, cm_b_in=m_cm_b_in, cm_dw_w=m_cm_dw_w, cm_dw_b=m_cm_dw_b, cm_ln_g=m_cm_ln_g, cm_ln_b=m_cm_ln_b, cm_w_out=m_cm_w_out, cm_b_out=m_cm_b_out, attn_w_qkv=m_attn_w_qkv, attn_sink=m_attn_sink, attn_w_o=m_attn_w_o, gm_w_in=m_gm_w_in, gm_b_in=m_gm_b_in, gm_ln_g=m_gm_ln_g, gm_ln_b=m_gm_ln_b, gm_w_s=m_gm_w_s, gm_b_s=m_gm_b_s, gm_w_out=m_gm_w_out)
    V = dict(c_ctx=v_c_ctx, ada_w=v_ada_w, ada_b=v_ada_b, norm_g=v_norm_g, ffn_w_up=v_ffn_w_up, ffn_conv_w=v_ffn_conv_w, ffn_conv_b=v_ffn_conv_b, ffn_w_down=v_ffn_w_down, cm_w_in=v_cm_w_in, cm_b_in=v_cm_b_in, cm_dw_w=v_cm_dw_w, cm_dw_b=v_cm_dw_b, cm_ln_g=v_cm_ln_g, cm_ln_b=v_cm_ln_b, cm_w_out=v_cm_w_out, cm_b_out=v_cm_b_out, attn_w_qkv=v_attn_w_qkv, attn_sink=v_attn_sink, attn_w_o=v_attn_w_o, gm_w_in=v_gm_w_in, gm_b_in=v_gm_b_in, gm_ln_g=v_gm_ln_g, gm_ln_b=v_gm_ln_b, gm_w_s=v_gm_w_s, gm_b_s=v_gm_b_s, gm_w_out=v_gm_w_out)
    me = 4 * lax.axis_index("x") + 2 * lax.axis_index("y") + lax.axis_index("c")
    small_shapes = [s for _, s in SMALL]

    small = _pack_rows([W[n] for n, _ in SMALL] + [c])
    layer_mats = [("cm_w_in", 0, "cm_w_out", 0), ("attn_w_qkv", 0, "attn_w_o", 0), ("gm_w_in", 0, "gm_w_out", 0),
                  ("cm_w_in", 1, "cm_w_out", 1)]
    local_bf16 = [[W[a][ja].astype(BF16), W[b][jb].astype(BF16), ffn_w_up[i].astype(BF16), ffn_w_down[i].astype(BF16)]
                  for i, (a, ja, b, jb) in enumerate(layer_mats)]
    gathered0 = _all_gather([small] + local_bf16[0], "gather_params0")
    small_g = gathered0[0]
    col_to_full = lambda g: g.transpose(1, 0, 2).reshape(g.shape[1], NDEV * g.shape[2])
    P = {}
    unpacked = jax.vmap(lambda r: tuple(_unpack_rows(r, small_shapes + [(D,)])))(small_g)
    for (n, _), g in zip(SMALL, unpacked[:-1]):
        if n == "ffn_conv_w":
            P[n] = [g[:, i] for i in range(DEPTH)]
        else:
            P[n] = _unshard_last(g)
    c_all = unpacked[-1]
    P["ffn_conv_b"] = [ffn_conv_b[i].reshape(NDEV, 1, FFN_BLK) for i in range(DEPTH)]
    P["attn_sink"], P["gm_w_s"], P["gm_b_s"] = attn_sink, gm_w_s, gm_b_s

    cond = jnp.concatenate([c_all, c_ctx[None], jnp.zeros((7, D), F32)])[None]
    scond = _rw_fwd(f_silu, [(cond, 0)], [], [(D, BF16)], name="ada_silu")[0]
    ada_bf = ada_w.astype(BF16)
    ncol = ada_w.shape[2]
    mod_loc = _mm(scond, ada_bf, "nn", name="ada_proj")
    mod_loc = mod_loc + lax.dynamic_slice_in_dim(ada_b, me * ncol, ncol, axis=1)[:, None, :]
    mod_g = _all_gather([mod_loc], "gather_mod")[0]
    mod_full = mod_g.transpose(1, 2, 0, 3).reshape(DEPTH, 16, 6, D)
    lat_mod = lax.dynamic_index_in_dim(mod_full, me, axis=1, keepdims=False)
    ctx_mod = mod_full[:, NDEV]

    gathers, exchanges, pending = {}, {}, {}
    col_to_parts = lambda g: g[0].reshape(g.shape[1], NDEV, g.shape[2] // NDEV).transpose(1, 0, 2)
    row_to_parts = lambda g: g.reshape(NDEV, -1, g.shape[-1])
    no_order = jnp.zeros((8, 128), F32)

    def layer_weights(i, h):
        if i == 0:
            mats = gathered0[1:]
            gathers[1] = _xfer_start("gather", local_bf16[1], mod_g, "gather_params1")
        else:
            mats = _xfer_wait(gathers[i], h)
            if i + 1 < DEPTH:
                gathers[i + 1] = _xfer_start("gather", local_bf16[i + 1], mats[0], f"gather_params{i + 1}")
        token = gathers[i + 1]["token"] if i + 1 < DEPTH else no_order
        a, ja, b, jb = layer_mats[i]
        pi = dict(P)
        pi[a] = col_to_full(mats[0]) if a.startswith(("attn", "gm")) else {ja: col_to_full(mats[0])}
        pi[b] = mats[1].reshape(-1, D) if b.startswith(("attn", "gm")) else {jb: mats[1].reshape(-1, D)}
        pi["ffn_w_up"], pi["ffn_w_down"] = {i: mats[2]}, {i: mats[3].reshape(FFN_PAIRS, FFN_BLK, D)}
        return _mixer_weights(i, pi), _ffn_weights(i, pi), token

    def grads_ready(tag, g):
        pending[tag] = g
        i = int(tag[1])
        col_name, row_name = {0: ("w_in", "w_out"), 1: ("w_qkv", "w_o"), 2: ("w_in", "w_out")}[i % 3]
        if tag == "f0":
            arrs = [g["up"], row_to_parts(g["down"])]
        elif tag == "m0":
            arrs = [col_to_parts(g[col_name]), row_to_parts(g[row_name])]
        elif tag[0] == "m":
            gf = pending[f"f{i}"]
            arrs = [col_to_parts(g[col_name]), row_to_parts(g[row_name]), gf["up"], row_to_parts(gf["down"])]
        else:
            return None
        exchanges[tag] = _xfer_start("scatter", arrs, no_order, "exchange_" + tag)
        if tag == "m2":
            gathers["w_s"] = _xfer_start("gather", [g["w_s"].reshape(GM_GROUPS * GM_CHUNK, GM_CHUNK)],
                                         exchanges[tag]["token"], "gather_gm_w_s")
            return gathers["w_s"]["token"]
        return exchanges[tag]["token"]

    loss_part, grad_x, G, dmod = _local_step(x, ctx, loss_target, lat_mod, ctx_mod, P["norm_g"], layer_weights,
                                             grads_ready)
    recv = {tag: _xfer_wait(exchanges[tag], grad_x) for tag in ("m3", "m2", "m1")}

    dmod_g = _all_gather([dmod], "gather_dmod")[0]
    dm_cols = lax.dynamic_slice_in_dim(dmod_g, me * ncol, ncol, axis=3)
    dm_ext = dm_cols.transpose(2, 1, 0, 3).reshape(DEPTH, 16, ncol)
    cond_ext = jnp.concatenate([c_all, jnp.broadcast_to(c_ctx[None], (NDEV, D))])[None]
    scond_ext = _rw_fwd(f_silu, [(cond_ext, 0)], [], [(D, BF16)], name="ada_silu_ext")[0]
    g_ada_w = _mm(scond_ext, dm_ext, "tn", name="ada_proj_bw")
    dsil = _mm(dm_ext, ada_bf, "nt", reduce_blocks=True, name="ada_proj_bx")
    dcc = _rw_bwd(f_silu_rows, [(jnp.zeros((1, NDEV, D), F32), 0)], [(c_ctx.reshape(1, 1, D), "one")],
                  [dsil[:, NDEV:]], name="ada_silu_b", param_grad=[0])[0]

    out = {}

    def put(name, res):
        out[name] = res

    d_norm_g = jnp.stack([jnp.stack([G[f"m{i}"]["ng0"], G[f"m{i}"]["ng1"], G[f"f{i}"]["ng2"], G[f"f{i}"]["ng3"]])
                          for i in range(DEPTH)]).reshape(DEPTH, 4, D)
    small_full = dict(
        norm_g=d_norm_g,
        cm_b_in=jnp.stack([G["m0"]["b_in"], G["m3"]["b_in"]]).reshape(2, 2 * D),
        cm_dw_w=jnp.stack([G["m0"]["dw_w"][0], G["m3"]["dw_w"][0]]),
        cm_dw_b=jnp.stack([G["m0"]["dw_b"], G["m3"]["dw_b"]]).reshape(2, D),
        cm_ln_g=jnp.stack([G["m0"]["ln_g"], G["m3"]["ln_g"]]).reshape(2, D),
        cm_ln_b=jnp.stack([G["m0"]["ln_b"], G["m3"]["ln_b"]]).reshape(2, D),
        cm_b_out=jnp.stack([G["m0"]["b_out"], G["m3"]["b_out"]]).reshape(2, D),
        gm_b_in=G["m2"]["b_in"].reshape(1, 2 * GM_W),
        gm_ln_g=G["m2"]["ln_g"].reshape(1, GM_W), gm_ln_b=G["m2"]["ln_b"].reshape(1, GM_W))
    by_dest = []
    for n, _ in SMALL:
        if n == "ffn_conv_w":
            by_dest.append(jnp.stack([G[f"f{i}"]["cw"] for i in range(DEPTH)], axis=1))
        else:
            by_dest.append(_shard_last(small_full[n]))
    small_send = jax.vmap(lambda *vs: _pack_rows(list(vs)))(*by_dest)
    small_recv = _all_to_all([[small_send]], "exchange_small")[0]

    def shard3(a):
        return a.reshape(a.shape[0], -1, a.shape[-1])

    small_local = lambda d_: _pack_rows([d_[n] for n, _ in SMALL])[None]
    res = _adamw(small_recv, small_local(W), small_local(M), small_local(V), "adamw_small")
    unp = [_unpack_rows(r[0], small_shapes) for r in res]
    for q, (n, _) in enumerate(SMALL):
        put(n, [unp[t][q] for t in range(4)])

    repl_names = ["c_ctx", "ffn_conv_b", "attn_sink", "gm_b_s"]
    repl_part = dict(
        c_ctx=dcc.reshape(D),
        ffn_conv_b=jnp.stack([G[f"f{i}"]["cb"].reshape(2 * 2816) for i in range(DEPTH)]),
        attn_sink=G["m1"]["sink"].reshape(1, N_Q),
        gm_b_s=G["m2"]["b_s"].reshape(1, GM_GROUPS, GM_CHUNK))
    w_s_parts = _xfer_wait(gathers["w_s"], grad_x)[0]
    flat_s = lambda a: a.reshape(1, GM_GROUPS * GM_CHUNK, GM_CHUNK)
    res = _adamw(w_s_parts[None], flat_s(gm_w_s), flat_s(m_gm_w_s), flat_s(v_gm_w_s), "adamw_gm_w_s")
    put("gm_w_s", [r.reshape(gm_w_s.shape) for r in res])
    repl_shapes = [W[n].shape for n in repl_names]
    repl_sent = _pack_rows([repl_part[n] for n in repl_names] + [loss_part.reshape(1)], mult=256)
    repl_g = _all_gather([repl_sent], "gather_repl")[0]
    loss = jnp.sum(repl_g.reshape(NDEV, -1)[:, sum(W[n].size for n in repl_names)])
    repl_local = lambda d_: _pack_rows([d_[n] for n in repl_names], mult=256)[None]
    res = _adamw(repl_g[None], repl_local(W), repl_local(M), repl_local(V), "adamw_repl")
    unp = [_unpack_rows(r[0], repl_shapes) for r in res]
    for q, n in enumerate(repl_names):
        put(n, [unp[t][q] for t in range(4)])

    def update_big(n, parts):
        turn = (lambda a: jnp.swapaxes(a, 1, 2)) if n == "ffn_w_up" else shard3
        res = _adamw(parts, turn(W[n]), turn(M[n]), turn(V[n]), "adamw_" + n)
        put(n, [(turn(r) if n == "ffn_w_up" else r).reshape(W[n].shape) for r in res])

    put("ada_w", _adamw(g_ada_w[:, None], ada_w, m_ada_w, v_ada_w, "adamw_ada_w"))
    ada_b_parts = dmod_g.reshape(1, 2 * NDEV, DEPTH, 6 * D)
    res = _adamw(ada_b_parts, ada_b[None], m_ada_b[None], v_ada_b[None], "adamw_ada_b")
    put("ada_b", [r[0] for r in res])
    early = dict(attn_w_qkv=[recv["m1"][0]], attn_w_o=[recv["m1"][1]], gm_w_in=[recv["m2"][0]],
                 gm_w_out=[recv["m2"][1]])
    for n, parts in early.items():
        update_big(n, parts)
    done_first = sum(out[n][1].reshape(-1)[:1024] for n in ["ada_w", "ada_b", "gm_w_in", "gm_w_out", "attn_w_qkv"])
    recv.update({tag: _xfer_wait(exchanges[tag], done_first) for tag in ("f0", "m0")})
    late = dict(
        ffn_w_up=[recv["f0"][0]] + [recv[f"m{i}"][2] for i in (1, 2, 3)],
        ffn_w_down=[recv["f0"][1]] + [recv[f"m{i}"][3] for i in (1, 2, 3)],
        cm_w_in=[recv["m0"][0], recv["m3"][0]], cm_w_out=[recv["m0"][1], recv["m3"][1]])
    for n, parts in late.items():
        update_big(n, parts)

    names = ["c_ctx", "ada_w", "ada_b", "norm_g", "ffn_w_up", "ffn_conv_w", "ffn_conv_b", "ffn_w_down", "cm_w_in",
             "cm_b_in", "cm_dw_w", "cm_dw_b", "cm_ln_g", "cm_ln_b", "cm_w_out", "cm_b_out", "attn_w_qkv",
             "attn_sink", "attn_w_o", "gm_w_in", "gm_b_in", "gm_ln_g", "gm_ln_b", "gm_w_s", "gm_b_s", "gm_w_out"]
    return (loss, grad_x, *[out[n][0] for n in names], *[out[n][1] for n in names],
            *[out[n][2] for n in names], *[out[n][3] for n in names])
```

```python
import functools

import numpy as np
import jax
import jax.numpy as jnp
from jax import lax
from jax.experimental import pallas as pl
from jax.experimental.pallas import tpu as pltpu

F32, BF16 = jnp.float32, jnp.bfloat16
MESH = pl.DeviceIdType.MESH
AXES = ("x", "y", "c")
NDEV = 8

D = 1024
L = 2048
LC = 256
TA = L + LC
DEPTH = 4
EPS = 1e-6
HEAD_DIM = 64
N_Q, N_KV, Q_PER_KV = 16, 4, 4
ATTN_BLOCK = 128
GRID_W = 64
ROPE_BASE = 10000.0
GM_W = 2048
GM_CHUNK = 128
GM_GROUPS = 16
FFN_BLK = 704
CM_K, FFN_K = 31, 3

ADAM_LR, ADAM_B1, ADAM_B2, ADAM_EPS, ADAM_WD, ADAM_STEP = 0.001, 0.9, 0.999, 1e-08, 0.01, 10

VMEM_LIMIT_V7X = 58 * 1024 * 1024
ROW_TILE_ELEMS = 512 * 1024
MM_TILE_BYTES = 4 * 1024 * 1024


def _cparams(sem=None):
    kw = dict(vmem_limit_bytes=VMEM_LIMIT_V7X)
    if sem is not None:
        kw["dimension_semantics"] = sem
    return pltpu.CompilerParams(**kw)


def _pick(n, cands):
    for c in cands:
        if n % c == 0:
            return c
    return n


def _as3(a):
    return a if a.ndim == 3 else a[None]


def _all_gather(arrs, name):
    n = len(arrs)

    def body(*refs):
        ins, outs = refs[:n], refs[n:2 * n]
        send_sems, recv_sems, local_sems = refs[2 * n:]
        x, y, c = lax.axis_index("x"), lax.axis_index("y"), lax.axis_index("c")
        me, sibling = (x, y, c), (x, y, 1 - c)
        chips = [(1 - x, y), (x, 1 - y), (1 - x, 1 - y)]

        def slot(a, p):
            return outs[a].at[4 * p[0] + 2 * p[1] + p[2]]

        def copy(a, k, block, to, src=None):
            return pltpu.make_async_remote_copy(
                src_ref=slot(a, block) if src is None else src, dst_ref=slot(a, block),
                send_sem=send_sems.at[a, k], recv_sem=recv_sems.at[a, k],
                device_id=to, device_id_type=MESH)

        mine = [pltpu.make_async_copy(ins[a], slot(a, me), local_sems.at[a]) for a in range(n)]
        for m in mine:
            m.start()
        first = []
        for a in range(n):
            first.append(copy(a, 0, me, sibling, src=ins[a]))
            first += [copy(a, 1 + j, me, (*chip, c), src=ins[a]) for j, chip in enumerate(chips)]
        for cp in first:
            cp.start()
        passed = []
        for j, chip in enumerate(chips):
            for a in range(n):
                copy(a, 1 + j, (*chip, c), me).wait_recv()
                p = copy(a, 4 + j, (*chip, c), sibling)
                p.start()
                passed.append(p)
        for a in range(n):
            copy(a, 0, sibling, me).wait_recv()
            for j, chip in enumerate(chips):
                copy(a, 4 + j, (*chip, 1 - c), me).wait_recv()
        for cp in first + passed:
            cp.wait_send()
        for m in mine:
            m.wait()

    any_spec = pl.BlockSpec(memory_space=pl.ANY)
    outs = pl.pallas_call(
        body, name=name,
        out_shape=[jax.ShapeDtypeStruct((NDEV,) + a.shape, a.dtype) for a in arrs],
        in_specs=[any_spec] * n, out_specs=[any_spec] * n,
        scratch_shapes=[pltpu.SemaphoreType.DMA((n, 7)), pltpu.SemaphoreType.DMA((n, 7)),
                        pltpu.SemaphoreType.DMA((n,))],
    )(*arrs)
    return list(outs)


def _all_to_all(groups, name):
    flat = [(gi, li, a) for gi, g in enumerate(groups) for li, a in enumerate(g)]
    n, ng = len(flat), len(groups)

    def body(*refs):
        ins, outs = refs[:n], refs[n:n + ng]
        send_sems, recv_sems, local_sems = refs[n + ng:]
        x, y, c = lax.axis_index("x"), lax.axis_index("y"), lax.axis_index("c")
        me = 4 * x + 2 * y + c
        copies = []
        for a, (gi, li, _) in enumerate(flat):
            loc = pltpu.make_async_copy(ins[a].at[me], outs[gi].at[li, me], local_sems.at[a])
            loc.start()
            copies.append(loc)
            for k in range(1, NDEV):
                px = 1 - x if (k >> 2) & 1 else x
                py = 1 - y if (k >> 1) & 1 else y
                pc = 1 - c if k & 1 else c
                cp = pltpu.make_async_remote_copy(
                    src_ref=ins[a].at[4 * px + 2 * py + pc], dst_ref=outs[gi].at[li, me],
                    send_sem=send_sems.at[a, k - 1], recv_sem=recv_sems.at[a, k - 1],
                    device_id=(px, py, pc), device_id_type=MESH)
                cp.start()
                copies.append(cp)
        for cp in copies:
            cp.wait()

    any_spec = pl.BlockSpec(memory_space=pl.ANY)
    outs = pl.pallas_call(
        body, name=name,
        out_shape=[jax.ShapeDtypeStruct((len(g),) + g[0].shape, g[0].dtype) for g in groups],
        in_specs=[any_spec] * n, out_specs=[any_spec] * ng,
        scratch_shapes=[pltpu.SemaphoreType.DMA((n, 7)), pltpu.SemaphoreType.DMA((n, 7)),
                        pltpu.SemaphoreType.DMA((n,))],
    )(*[a for _, _, a in flat])
    return list(outs)


HBM_SPEC = pl.BlockSpec(memory_space=pltpu.HBM)
SEM_SPEC = pl.BlockSpec(memory_space=pltpu.SEMAPHORE)
ANY_SPEC = pl.BlockSpec(memory_space=pl.ANY)
SPLIT_EFFECT = pltpu.SideEffectType.DATAFLOW_SIDE_EFFECTING


def _remote_copies(kind, ins, lands, send_sems, recv_sems):
    x, y, c = lax.axis_index("x"), lax.axis_index("y"), lax.axis_index("c")
    me = 4 * x + 2 * y + c
    out = []
    for a in range(len(ins)):
        for k in range(1, NDEV):
            px = 1 - x if (k >> 2) & 1 else x
            py = 1 - y if (k >> 1) & 1 else y
            pc = 1 - c if k & 1 else c
            src = ins[a] if kind == "gather" else ins[a].at[4 * px + 2 * py + pc]
            out.append(pltpu.make_async_remote_copy(
                src_ref=src, dst_ref=lands[a].at[me], send_sem=send_sems.at[a * (NDEV - 1) + k - 1],
                recv_sem=recv_sems.at[a * (NDEV - 1) + k - 1], device_id=(px, py, pc), device_id_type=MESH))
    return out


def _local_copies(kind, ins, lands, local_sems):
    me = 4 * lax.axis_index("x") + 2 * lax.axis_index("y") + lax.axis_index("c")
    return [pltpu.make_async_copy(ins[a] if kind == "gather" else ins[a].at[me], lands[a].at[me], local_sems.at[a])
            for a in range(len(ins))]


def _xfer_start(kind, arrs, after, name):
    n = len(arrs)
    lands = [lax.empty((NDEV,) + a.shape if kind == "gather" else a.shape, a.dtype) for a in arrs]

    def body(*refs):
        ins, lnd = refs[:n], refs[n:2 * n]
        send_sems, recv_sems, local_sems = refs[2 * n + 1:2 * n + 4]
        for cp in _remote_copies(kind, ins, lnd, send_sems, recv_sems) + _local_copies(kind, ins, lnd, local_sems):
            cp.start()
        refs[-1][...] = jnp.zeros_like(refs[-1])

    outs = pl.pallas_call(
        body, name=name,
        out_shape=(pltpu.SemaphoreType.DMA((n * (NDEV - 1),)), pltpu.SemaphoreType.DMA((n * (NDEV - 1),)),
                   pltpu.SemaphoreType.DMA((n,)),
                   *[pltpu.HBM(a.shape, a.dtype) for a in arrs + lands], jax.ShapeDtypeStruct((8, 128), F32)),
        in_specs=[HBM_SPEC] * (2 * n) + [ANY_SPEC],
        out_specs=(SEM_SPEC, SEM_SPEC, SEM_SPEC, *[HBM_SPEC] * (2 * n), pl.BlockSpec(memory_space=pltpu.VMEM)),
        input_output_aliases={a: 3 + a for a in range(2 * n)},
        compiler_params=pltpu.CompilerParams(has_side_effects=SPLIT_EFFECT),
    )(*[pltpu.with_memory_space_constraint(a, pltpu.HBM) for a in arrs + lands], after)
    return dict(kind=kind, n=n, sems=list(outs[:3]), bufs=list(outs[3:3 + 2 * n]), token=outs[-1], name=name)


def _xfer_wait(st, after):
    kind, n = st["kind"], st["n"]

    def body(*refs):
        ins, lnd = refs[:n], refs[n:2 * n]
        send_sems, recv_sems, local_sems = refs[2 * n:2 * n + 3]
        for cp in _remote_copies(kind, ins, lnd, send_sems, recv_sems):
            cp.wait_send()
            cp.wait_recv()
        for cp in _local_copies(kind, ins, lnd, local_sems):
            cp.wait()

    outs = pl.pallas_call(
        body, name=st["name"] + "_wait",
        out_shape=tuple(pltpu.HBM(b.shape, b.dtype) for b in st["bufs"]),
        in_specs=[HBM_SPEC] * (2 * n) + [SEM_SPEC] * 3 + [ANY_SPEC],
        out_specs=tuple([HBM_SPEC] * (2 * n)),
        input_output_aliases={a: a for a in range(2 * n)},
        compiler_params=pltpu.CompilerParams(has_side_effects=SPLIT_EFFECT),
    )(*st["bufs"], *st["sems"], after)
    return list(outs[n:])


def _mm(a, b, kind, *, name, out_dtype=F32, reduce_blocks=False):
    a, b = _as3(a), _as3(b)
    nba, nbb = a.shape[0], b.shape[0]
    nb = max(nba, nbb)
    assert nba in (1, nb) and nbb in (1, nb)
    if kind == "tn":
        t, m = a.shape[1:]
        n = b.shape[2]
        assert b.shape[1] == t and not reduce_blocks
        tm = m if m <= 1024 else _pick(m, (1024,))
        tn = n if n <= 1024 else _pick(n, (1024, 768, 512))
        tk = t if t * (tm + tn) * 2 <= MM_TILE_BYTES * 3 else _pick(t, (512, 768, 256))
        nred = t // tk
    else:
        m, k = a.shape[1:]
        n = b.shape[2] if kind == "nn" else b.shape[1]
        assert (b.shape[1] if kind == "nn" else b.shape[2]) == k
        tn = n if (n <= 1024 or k * n * 2 <= 2 * MM_TILE_BYTES) else _pick(n, (1024, 768, 512))
        tm = m
        for cand in (1024, 768, 512, 256):
            if m % cand == 0 and cand * tn * 4 <= MM_TILE_BYTES and cand * k * 2 <= MM_TILE_BYTES:
                tm = cand
                break
        nred = nb if reduce_blocks else 1
    nbo = 1 if reduce_blocks else nb

    def blk(nbx, g, r):
        if nbx == 1:
            return 0
        return r if reduce_blocks else g

    if kind == "nn":
        a_spec = pl.BlockSpec((1, tm, k), lambda g, j, i, r: (blk(nba, g, r), i, 0))
        b_spec = pl.BlockSpec((1, k, tn), lambda g, j, i, r: (blk(nbb, g, r), 0, j))
        dims = (((1,), (0,)), ((), ()))
    elif kind == "nt":
        a_spec = pl.BlockSpec((1, tm, k), lambda g, j, i, r: (blk(nba, g, r), i, 0))
        b_spec = pl.BlockSpec((1, tn, k), lambda g, j, i, r: (blk(nbb, g, r), j, 0))
        dims = (((1,), (1,)), ((), ()))
    else:
        a_spec = pl.BlockSpec((1, tk, tm), lambda g, j, i, r: (blk(nba, g, r), r, i))
        b_spec = pl.BlockSpec((1, tk, tn), lambda g, j, i, r: (blk(nbb, g, r), r, j))
        dims = (((0,), (0,)), ((), ()))
    o_spec = pl.BlockSpec((1, tm, tn), lambda g, j, i, r: (g, i, j))

    def body(a_ref, b_ref, o_ref, *scratch):
        prod = lax.dot_general(a_ref[0].astype(BF16), b_ref[0].astype(BF16), dims,
                               preferred_element_type=F32)
        if nred == 1:
            o_ref[0] = prod.astype(o_ref.dtype)
        else:
            acc = scratch[0]
            r = pl.program_id(3)

            @pl.when(r == 0)
            def _():
                acc[...] = prod

            @pl.when(r > 0)
            def _():
                acc[...] += prod

            @pl.when(r == nred - 1)
            def _():
                o_ref[0] = acc[...].astype(o_ref.dtype)

    return pl.pallas_call(
        body, name=name,
        out_shape=jax.ShapeDtypeStruct((nbo, m, n), out_dtype),
        grid=(nbo, n // tn, m // tm, nred),
        in_specs=[a_spec, b_spec], out_specs=o_spec,
        scratch_shapes=[pltpu.VMEM((tm, tn), F32)] if nred > 1 else [],
        compiler_params=_cparams(("parallel", "parallel", "parallel", "arbitrary")),
    )(a, b)


def _row_tile(t, widths):
    tm = min(max(16, ROW_TILE_ELEMS // max(widths)), 512)
    while t % tm and tm > 16:
        tm //= 2
    return t if t < tm else tm


def _sel_index(sel, g, i, tm):
    if sel == "one":
        return 0
    if sel == "seg":
        return (i * tm) // L
    return g + sel


def _row_spec(arr, off, tm):
    return pl.BlockSpec((1, tm, arr.shape[2]), lambda g, i: (g + off, i, 0))


def _par_spec(arr, sel, tm):
    return pl.BlockSpec((1, 1, arr.shape[2]), lambda g, i: (_sel_index(sel, g, i, tm), 0, 0))


def _norm_ops(ops):
    return [(o[0], o[1], o[2] if len(o) > 2 else 1) for o in ops]


def _split_cols(vals, nsplit):
    out = []
    for v, ns in zip(vals, nsplit):
        w = v.shape[1] // ns
        out += [v] if ns == 1 else [v[:, q * w:(q + 1) * w] for q in range(ns)]
    return out


def _join_cols(flat, nsplit):
    out, pos = [], 0
    for ns in nsplit:
        out.append(flat[pos] if ns == 1 else jnp.concatenate(flat[pos:pos + ns], axis=1))
        pos += ns
    return out


def _rw_fwd(fn, rows, params, outs, *, name, nblk=None):
    rows, params = _norm_ops(rows), _norm_ops(params)
    t = rows[0][0].shape[1]
    nblk = nblk or rows[0][0].shape[0]
    tm = _row_tile(t, [r.shape[2] for r, _, _ in rows] + [w for w, _ in outs])
    nr, npar = len(rows), len(params)
    nsplit = [ns for _, _, ns in rows + params]

    def body(*refs):
        vals = _split_cols([r[0].astype(F32) for r in refs[:nr + npar]], nsplit)
        res = fn(*vals)
        for o_ref, o in zip(refs[nr + npar:], res):
            o_ref[0] = o.astype(o_ref.dtype)

    res = pl.pallas_call(
        body, name=name,
        out_shape=[jax.ShapeDtypeStruct((nblk, t, w), dt) for w, dt in outs],
        grid=(nblk, t // tm),
        in_specs=[_row_spec(r, off, tm) for r, off, _ in rows] + [_par_spec(p, s, tm) for p, s, _ in params],
        out_specs=[pl.BlockSpec((1, tm, w), lambda g, i: (g, i, 0)) for w, _ in outs],
        compiler_params=_cparams(("parallel", "parallel")),
    )(*[r for r, _, _ in rows], *[p for p, _, _ in params])
    return list(res)


def _rw_bwd(fn, rows, params, cts, *, name, row_grad=(), param_grad=(), add=None, nblk=None):
    rows, params = _norm_ops(rows), _norm_ops(params)
    t = cts[0].shape[1]
    nblk = nblk or cts[0].shape[0]
    tm = _row_tile(t, [r.shape[2] for r, _, _ in rows] + [c.shape[2] for c in cts])
    ni = t // tm
    nr, npar, nct = len(rows), len(params), len(cts)
    nadd = 0 if add is None else 1
    n_in = nr + npar + nct + nadd
    nsplit = [ns for _, _, ns in rows + params]

    def body(*refs):
        prim = _split_cols([r[0].astype(F32) for r in refs[:nr + npar]], nsplit)
        ct = tuple(r[0].astype(F32) for r in refs[nr + npar:nr + npar + nct])
        _, vjp = jax.vjp(fn, *prim)
        grads = _join_cols(list(vjp(ct)), nsplit)
        out_refs = refs[n_in:]
        for q, (ri, _) in enumerate(row_grad):
            gr = grads[ri]
            if q == 0 and nadd:
                gr = gr + refs[n_in - 1][0].astype(F32)
            out_refs[q][0] = gr.astype(out_refs[q].dtype)
        g, i = pl.program_id(0), pl.program_id(1)
        step = g * ni + i
        pg, pi = (step - 1) // ni, (step - 1) % ni
        for q, pidx in enumerate(param_grad):
            o_ref = out_refs[len(row_grad) + q]
            sel = params[pidx][1]
            val = grads[nr + pidx]
            if sel == "one":
                first = step == 0
            else:
                first = (step == 0) | (_sel_index(sel, g, i, tm) != _sel_index(sel, pg, pi, tm))

            @pl.when(first)
            def _(o_ref=o_ref, val=val):
                o_ref[0] = val

            @pl.when(jnp.logical_not(first))
            def _(o_ref=o_ref, val=val):
                o_ref[0] += val

    in_arrays = [r for r, _, _ in rows] + [p for p, _, _ in params] + list(cts) + ([add] if nadd else [])
    in_specs = ([_row_spec(r, off, tm) for r, off, _ in rows] + [_par_spec(p, s, tm) for p, s, _ in params]
                + [_row_spec(c, 0, tm) for c in cts] + ([_row_spec(add, 0, tm)] if nadd else []))
    out_shape, out_specs = [], []
    for ri, dt in row_grad:
        w = rows[ri][0].shape[2]
        out_shape.append(jax.ShapeDtypeStruct((nblk, t, w), dt))
        out_specs.append(pl.BlockSpec((1, tm, w), lambda g, i: (g, i, 0)))
    for pidx in param_grad:
        p, sel, _ = params[pidx]
        out_shape.append(jax.ShapeDtypeStruct(p.shape, F32))
        out_specs.append(_par_spec(p, sel, tm))
    res = pl.pallas_call(
        body, name=name, out_shape=out_shape, grid=(nblk, ni),
        in_specs=in_specs, out_specs=out_specs,
        compiler_params=_cparams(("arbitrary", "arbitrary")),
    )(*in_arrays)
    return list(res)


def _sigmoid(x):
    return 1.0 / (1.0 + jnp.exp(-x))


def _rms(x, g):
    return x * lax.rsqrt(jnp.mean(x * x, axis=-1, keepdims=True) + EPS) * g


def _ln(x, g, b):
    mu = jnp.mean(x, axis=-1, keepdims=True)
    xc = x - mu
    var = jnp.mean(xc * xc, axis=-1, keepdims=True)
    return xc * lax.rsqrt(var + EPS) * g + b


def _gelu_tanh(x):
    return 0.5 * x * (1.0 + jnp.tanh(0.7978845608028654 * (x + 0.044715 * (x * x * x))))


def f_modnorm(h, g, sc, sh):
    return (_rms(h, g) * (1.0 + sc) + sh,)


def f_gate_rms(y, gate, g):
    return (gate * _rms(y, g),)


def f_gate_rms_bias(y, gate, g, b):
    return (gate * _rms(y + b, g),)


def f_resgate(h, y, gate, g):
    return (h + gate * _rms(y, g),)


def f_resgate_bias(h, y, gate, g, b):
    return (h + gate * _rms(y + b, g),)


def f_glu(pa, pg, ba, bg):
    return ((pa + ba) * _sigmoid(pg + bg),)


def f_lnsilu(z, g, b):
    t = _ln(z, g, b)
    return (t * _sigmoid(t),)


def f_gmlp_pre(pu, pv, bu, bv, g, bb):
    return _gelu_tanh(pu + bu), _ln(_gelu_tanh(pv + bv), g, bb)


def f_ffn_gate(zg, zv):
    return (zg * _sigmoid(zg) * zv,)


def f_silu(x):
    return (x * _sigmoid(x),)


def f_silu_rows(dummy, cc):
    return (cc * _sigmoid(cc) + 0.0 * dummy,)


def _rope(x_in, tables, neg_sin, out_dtype, name, cols=None):
    w, ci = cols if cols else (x_in.shape[2], 0)
    sign = -1.0 if neg_sin else 1.0
    tm = 256

    def body(x_ref, cos_ref, sin_ref, o_ref):
        x = x_ref[0].astype(F32)
        cos = jnp.tile(cos_ref[0], (1, w // 128))
        sin = jnp.tile(sin_ref[0], (1, w // 128)) * sign
        lane = lax.broadcasted_iota(jnp.int32, x.shape, 1) & 31
        rot = jnp.where(lane < 16, -pltpu.roll(x, w - 16, 1), pltpu.roll(x, 16, 1))
        o_ref[0] = (x * cos + rot * sin).astype(o_ref.dtype)

    tspec = pl.BlockSpec((1, tm, 128), lambda i: (0, i, 0))
    return pl.pallas_call(
        body, name=name, out_shape=jax.ShapeDtypeStruct((1, L, w), out_dtype), grid=(L // tm,),
        in_specs=[pl.BlockSpec((1, tm, w), lambda i: (0, i, ci)), tspec, tspec],
        out_specs=pl.BlockSpec((1, tm, w), lambda i: (0, i, 0)),
        compiler_params=_cparams(("parallel",)),
    )(x_in, tables[0], tables[1])


CONV_TM = 256
CONV_RC = 8


class _ShiftedRows:
    def __init__(self, xp, cb):
        self.xp, self.memo = xp, {}
        self.row = lax.broadcasted_iota(jnp.int32, (CONV_RC, cb), 0)

    def _get(self, key, make):
        if key not in self.memo:
            self.memo[key] = make()
        return self.memo[key]

    def chunk(self, a):
        return self._get(("c", a), lambda: self.xp[a:a + CONV_RC, :])

    def rot(self, a, j):
        return self._get(("r", a, j), lambda: pltpu.roll(self.chunk(a), CONV_RC - j, 0))

    def at(self, a):
        q, j = divmod(a, CONV_RC)
        if j == 0:
            return self.chunk(a)
        low = self._get(("m", j), lambda: self.row < CONV_RC - j)
        return jnp.where(low, self.rot(q * CONV_RC, j), self.rot(q * CONV_RC + CONV_RC, j))


def _conv_geometry(x, k):
    nb, t, w = x.shape
    halo = 16 if k > 17 else 8
    cb = _pick(w, (512,)) if w > 768 else w
    return nb, t, w, halo, cb, (k - 1) // 2


def _conv_in_specs(t, halo, cb):
    per = CONV_TM // halo
    last = t // halo - 1
    return [
        pl.BlockSpec((1, CONV_TM, cb), lambda g, jc, i: (g, i, jc)),
        pl.BlockSpec((1, halo, cb), lambda g, jc, i: (g, jnp.maximum(i * per - 1, 0), jc)),
        pl.BlockSpec((1, halo, cb), lambda g, jc, i: (g, jnp.minimum((i + 1) * per, last), jc)),
    ]


def _conv_fill(xp, x_ref, prev_ref, next_ref, halo, t):
    i = pl.program_id(2)
    seg_first = (i * CONV_TM == 0) | (i * CONV_TM == L)
    seg_last = ((i + 1) * CONV_TM == L) | ((i + 1) * CONV_TM == t)
    xp[0:halo, :] = jnp.where(seg_first, 0.0, prev_ref[0].astype(F32))
    xp[halo:halo + CONV_TM, :] = x_ref[0].astype(F32)
    xp[halo + CONV_TM:, :] = jnp.where(seg_last, 0.0, next_ref[0].astype(F32))


def _dwconv(x, w, b, *, name, out_dtype=F32):
    k = w.shape[1]
    nb, t, wd, halo, cb, half = _conv_geometry(x, k)
    base = halo - half

    def body(*refs):
        x_ref, prev_ref, next_ref, w_ref = refs[:4]
        b_ref = refs[4] if b is not None else None
        o_ref, xp = refs[-2], refs[-1]
        _conv_fill(xp, x_ref, prev_ref, next_ref, halo, t)
        rows = _ShiftedRows(xp, cb)
        for o in range(0, CONV_TM, CONV_RC):
            acc = None
            for kk in range(k):
                term = w_ref[0, kk:kk + 1, :] * rows.at(o + base + kk)
                acc = term if acc is None else acc + term
            if b_ref is not None:
                acc = acc + b_ref[0]
            o_ref[0, o:o + CONV_RC, :] = acc.astype(o_ref.dtype)

    in_specs = _conv_in_specs(t, halo, cb) + [pl.BlockSpec((1, k, cb), lambda g, jc, i: (g, 0, jc))]
    args = [x, x, x, w]
    if b is not None:
        in_specs.append(pl.BlockSpec((1, 1, cb), lambda g, jc, i: (g, 0, jc)))
        args.append(b)
    return pl.pallas_call(
        body, name=name, out_shape=jax.ShapeDtypeStruct((nb, t, wd), out_dtype),
        grid=(nb, wd // cb, t // CONV_TM), in_specs=in_specs,
        out_specs=pl.BlockSpec((1, CONV_TM, cb), lambda g, jc, i: (g, i, jc)),
        scratch_shapes=[pltpu.VMEM((CONV_TM + 2 * halo, cb), F32)],
        compiler_params=_cparams(("parallel", "parallel", "parallel")),
    )(*args)


def _dwconv_wgrad(x, dy, k, *, name):
    nb, t, wd, halo, cb, half = _conv_geometry(x, k)
    base = halo - half

    def body(x_ref, prev_ref, next_ref, dy_ref, dw_ref, db_ref, xp):
        _conv_fill(xp, x_ref, prev_ref, next_ref, halo, t)
        i = pl.program_id(2)

        @pl.when(i == 0)
        def _():
            dw_ref[...] = jnp.zeros_like(dw_ref)
            db_ref[...] = jnp.zeros_like(db_ref)

        rows = _ShiftedRows(xp, cb)
        dys = [dy_ref[0, o:o + CONV_RC, :].astype(F32) for o in range(0, CONV_TM, CONV_RC)]
        db_ref[0] += jnp.sum(sum(dys[1:], dys[0]), axis=0, keepdims=True)
        for kk in range(k):
            acc = None
            for ci, dyc in enumerate(dys):
                term = dyc * rows.at(ci * CONV_RC + base + kk)
                acc = term if acc is None else acc + term
            dw_ref[0, kk:kk + 1, :] += jnp.sum(acc, axis=0, keepdims=True)

    dw, db = pl.pallas_call(
        body, name=name,
        out_shape=[jax.ShapeDtypeStruct((nb, k, wd), F32), jax.ShapeDtypeStruct((nb, 1, wd), F32)],
        grid=(nb, wd // cb, t // CONV_TM),
        in_specs=_conv_in_specs(t, halo, cb) + [pl.BlockSpec((1, CONV_TM, cb), lambda g, jc, i: (g, i, jc))],
        out_specs=[pl.BlockSpec((1, k, cb), lambda g, jc, i: (g, 0, jc)),
                   pl.BlockSpec((1, 1, cb), lambda g, jc, i: (g, 0, jc))],
        scratch_shapes=[pltpu.VMEM((CONV_TM + 2 * halo, cb), F32)],
        compiler_params=_cparams(("parallel", "parallel", "arbitrary")),
    )(x, x, x, dy)
    return dw, db


ATTN_SCALE = HEAD_DIM ** -0.5
QROWS = Q_PER_KV * ATTN_BLOCK
NEG = -1e30


def _attn_bias():
    qi = np.arange(ATTN_BLOCK)[:, None] + ATTN_BLOCK
    kj = np.arange(3 * ATTN_BLOCK)[None, :]
    near = np.abs(qi - kj) <= ATTN_BLOCK
    valid = np.stack([kj >= ATTN_BLOCK, kj >= 0, kj < 2 * ATTN_BLOCK])
    return jnp.asarray(np.where(near[None] & valid, 0.0, NEG), F32)


def _attn_scores(q, kw, kc, bias):
    nt = (((1,), (1,)), ((), ()))
    s_w = lax.dot_general(q, kw, nt, preferred_element_type=F32) * ATTN_SCALE
    s_w = (s_w.reshape(Q_PER_KV, ATTN_BLOCK, 3 * ATTN_BLOCK) + bias[None]).reshape(QROWS, 3 * ATTN_BLOCK)
    s_c = lax.dot_general(q, kc, nt, preferred_element_type=F32) * ATTN_SCALE
    return s_w, s_c


def _sink_col(sink_ref, hk):
    return jnp.concatenate([jnp.full((ATTN_BLOCK, 1), sink_ref[hk * Q_PER_KV + g], F32) for g in range(Q_PER_KV)], axis=0)


def _attn_specs():
    qspec = pl.BlockSpec((Q_PER_KV, ATTN_BLOCK, HEAD_DIM), lambda hk, n: (hk, n, 0))
    kspec = pl.BlockSpec((1, L + 2 * ATTN_BLOCK, HEAD_DIM), lambda hk, n: (hk, 0, 0))
    cspec = pl.BlockSpec((1, LC, HEAD_DIM), lambda hk, n: (hk, 0, 0))
    lspec = pl.BlockSpec((Q_PER_KV, ATTN_BLOCK, 1), lambda hk, n: (hk, n, 0))
    sspec = pl.BlockSpec(memory_space=pltpu.SMEM)
    last = L // ATTN_BLOCK - 1
    bspec = pl.BlockSpec((1, ATTN_BLOCK, 3 * ATTN_BLOCK),
                         lambda hk, n: (jnp.where(n == 0, 0, jnp.where(n == last, 2, 1)), 0, 0))
    return qspec, kspec, cspec, lspec, sspec, bspec


def _attn_fwd(q, k, v, kc, vc, sink):
    qspec, kspec, cspec, lspec, sspec, bspec = _attn_specs()

    def body(bias_ref, q_ref, k_ref, v_ref, kc_ref, vc_ref, sink_ref, o_ref, lse_ref):
        hk, n = pl.program_id(0), pl.program_id(1)
        qv = q_ref[...].reshape(QROWS, HEAD_DIM)
        start = pl.multiple_of(n * ATTN_BLOCK, ATTN_BLOCK)
        kw = k_ref[0, pl.ds(start, 3 * ATTN_BLOCK), :]
        vw = v_ref[0, pl.ds(start, 3 * ATTN_BLOCK), :]
        s_w, s_c = _attn_scores(qv, kw, kc_ref[0], bias_ref[0])
        sk = _sink_col(sink_ref, hk)
        m = jnp.maximum(jnp.maximum(jnp.max(s_w, -1, keepdims=True), jnp.max(s_c, -1, keepdims=True)), sk)
        p_w, p_c = jnp.exp(s_w - m), jnp.exp(s_c - m)
        den = jnp.sum(p_w, -1, keepdims=True) + jnp.sum(p_c, -1, keepdims=True) + jnp.exp(sk - m)
        o = (jnp.dot(p_w.astype(BF16), vw, preferred_element_type=F32)
             + jnp.dot(p_c.astype(BF16), vc_ref[0], preferred_element_type=F32)) / den
        o_ref[...] = o.reshape(Q_PER_KV, ATTN_BLOCK, HEAD_DIM).astype(o_ref.dtype)
        lse_ref[...] = (m + jnp.log(den)).reshape(Q_PER_KV, ATTN_BLOCK, 1)

    return pl.pallas_call(
        body, name="attn_fwd",
        out_shape=[jax.ShapeDtypeStruct((N_Q, L, HEAD_DIM), BF16), jax.ShapeDtypeStruct((N_Q, L, 1), F32)],
        grid=(N_KV, L // ATTN_BLOCK),
        in_specs=[bspec, qspec, kspec, kspec, cspec, cspec, sspec], out_specs=[qspec, lspec],
        compiler_params=_cparams(("parallel", "parallel")),
    )(_attn_bias(), q, k, v, kc, vc, sink)


def _attn_bwd(q, k, v, kc, vc, sink, o, lse, do):
    qspec, kspec, cspec, lspec, sspec, bspec = _attn_specs()
    tn = (((0,), (0,)), ((), ()))
    nt = (((1,), (1,)), ((), ()))

    def body(bias_ref, q_ref, k_ref, v_ref, kc_ref, vc_ref, sink_ref, o_ref, lse_ref, do_ref,
             dq_ref, dk_ref, dv_ref, dkc_ref, dvc_ref, dsink_ref):
        hk, n = pl.program_id(0), pl.program_id(1)
        qv = q_ref[...].reshape(QROWS, HEAD_DIM)
        start = pl.multiple_of(n * ATTN_BLOCK, ATTN_BLOCK)
        win = pl.ds(start, 3 * ATTN_BLOCK)
        kw, vw = k_ref[0, win, :], v_ref[0, win, :]
        kcv, vcv = kc_ref[0], vc_ref[0]
        s_w, s_c = _attn_scores(qv, kw, kcv, bias_ref[0])
        lse_v = lse_ref[...].reshape(QROWS, 1)
        p_w, p_c = jnp.exp(s_w - lse_v), jnp.exp(s_c - lse_v)
        dov = do_ref[...].reshape(QROWS, HEAD_DIM).astype(F32)
        ov = o_ref[...].reshape(QROWS, HEAD_DIM).astype(F32)
        delta = jnp.sum(dov * ov, -1, keepdims=True)
        dob = dov.astype(BF16)
        dp_w = lax.dot_general(dob, vw, nt, preferred_element_type=F32)
        dp_c = lax.dot_general(dob, vcv, nt, preferred_element_type=F32)
        ds_w = (p_w * (dp_w - delta) * ATTN_SCALE).astype(BF16)
        ds_c = (p_c * (dp_c - delta) * ATTN_SCALE).astype(BF16)
        dq = jnp.dot(ds_w, kw, preferred_element_type=F32) + jnp.dot(ds_c, kcv, preferred_element_type=F32)
        dq_ref[...] = dq.reshape(Q_PER_KV, ATTN_BLOCK, HEAD_DIM)

        @pl.when(n == 0)
        def _():
            dk_ref[...] = jnp.zeros_like(dk_ref)
            dv_ref[...] = jnp.zeros_like(dv_ref)
            dkc_ref[...] = jnp.zeros_like(dkc_ref)
            dvc_ref[...] = jnp.zeros_like(dvc_ref)

        dk_ref[0, win, :] += lax.dot_general(ds_w, qv, tn, preferred_element_type=F32)
        dv_ref[0, win, :] += lax.dot_general(p_w.astype(BF16), dob, tn, preferred_element_type=F32)
        dkc_ref[0] += lax.dot_general(ds_c, qv, tn, preferred_element_type=F32)
        dvc_ref[0] += lax.dot_general(p_c.astype(BF16), dob, tn, preferred_element_type=F32)
        dsk = -jnp.exp(_sink_col(sink_ref, hk) - lse_v) * delta
        for g in range(Q_PER_KV):
            part = jnp.sum(dsk[g * ATTN_BLOCK:(g + 1) * ATTN_BLOCK])
            idx = hk * Q_PER_KV + g

            @pl.when(n == 0)
            def _(part=part, idx=idx):
                dsink_ref[idx] = part

            @pl.when(n > 0)
            def _(part=part, idx=idx):
                dsink_ref[idx] += part

    kshape = jax.ShapeDtypeStruct((N_KV, L + 2 * ATTN_BLOCK, HEAD_DIM), F32)
    cshape = jax.ShapeDtypeStruct((N_KV, LC, HEAD_DIM), F32)
    return pl.pallas_call(
        body, name="attn_bwd",
        out_shape=[jax.ShapeDtypeStruct((N_Q, L, HEAD_DIM), F32), kshape, kshape, cshape, cshape,
                   jax.ShapeDtypeStruct((N_Q,), F32)],
        grid=(N_KV, L // ATTN_BLOCK),
        in_specs=[bspec, qspec, kspec, kspec, cspec, cspec, sspec, qspec, lspec, qspec],
        out_specs=[qspec, kspec, kspec, cspec, cspec, sspec],
        compiler_params=_cparams(("arbitrary", "arbitrary")),
    )(_attn_bias(), q, k, v, kc, vc, sink, o, lse, do)


def _gm_specs():
    rspec = pl.BlockSpec((1, GM_CHUNK, GM_W), lambda n: (0, n, 0))
    wspec = pl.BlockSpec((GM_GROUPS, GM_CHUNK, GM_CHUNK), lambda n: (0, 0, 0))
    bspec = pl.BlockSpec((GM_GROUPS, GM_CHUNK, 1), lambda n: (0, 0, 0))
    return rspec, wspec, bspec


def _gm_spatial_fwd(u, v, ws, bs):
    rspec, wspec, bspec = _gm_specs()

    def body(u_ref, v_ref, ws_ref, bs_ref, o_ref):
        for g in range(GM_GROUPS):
            cols = slice(g * GM_CHUNK, (g + 1) * GM_CHUNK)
            s = jnp.dot(ws_ref[g], v_ref[0, :, cols], preferred_element_type=F32) + bs_ref[g]
            o_ref[0, :, cols] = (u_ref[0, :, cols] * s).astype(o_ref.dtype)

    return pl.pallas_call(
        body, name="gm_spatial_fwd", out_shape=jax.ShapeDtypeStruct((1, L, GM_W), BF16),
        grid=(L // GM_CHUNK,), in_specs=[rspec, rspec, wspec, bspec], out_specs=rspec,
        compiler_params=_cparams(("parallel",)),
    )(u, v, ws, bs)


def _gm_spatial_bwd(u, v, ws, bs, dus):
    rspec, wspec, bspec = _gm_specs()
    tn = (((0,), (0,)), ((), ()))
    nt = (((1,), (1,)), ((), ()))

    def body(u_ref, v_ref, ws_ref, bs_ref, d_ref, du_ref, dv_ref, dws_ref, dbs_ref):
        n = pl.program_id(0)

        @pl.when(n == 0)
        def _():
            dws_ref[...] = jnp.zeros_like(dws_ref)
            dbs_ref[...] = jnp.zeros_like(dbs_ref)

        for g in range(GM_GROUPS):
            cols = slice(g * GM_CHUNK, (g + 1) * GM_CHUNK)
            vb = v_ref[0, :, cols]
            s = jnp.dot(ws_ref[g], vb, preferred_element_type=F32) + bs_ref[g]
            d = d_ref[0, :, cols].astype(F32)
            du_ref[0, :, cols] = d * s
            ds = d * u_ref[0, :, cols]
            dsb = ds.astype(BF16)
            dv_ref[0, :, cols] = lax.dot_general(ws_ref[g], dsb, tn, preferred_element_type=F32)
            dws_ref[g] += lax.dot_general(dsb, vb, nt, preferred_element_type=F32)
            dbs_ref[g] += jnp.sum(ds, axis=1, keepdims=True)

    row = jax.ShapeDtypeStruct((1, L, GM_W), F32)
    return pl.pallas_call(
        body, name="gm_spatial_bwd",
        out_shape=[row, row, jax.ShapeDtypeStruct((GM_GROUPS, GM_CHUNK, GM_CHUNK), F32),
                   jax.ShapeDtypeStruct((GM_GROUPS, GM_CHUNK, 1), F32)],
        grid=(L // GM_CHUNK,), in_specs=[rspec, rspec, wspec, bspec, rspec],
        out_specs=[rspec, rspec, wspec, bspec],
        compiler_params=_cparams(("arbitrary",)),
    )(u, v, ws, bs, dus)


def _loss_head(h, target):
    tm = 256

    def body(h_ref, t_ref, dh_ref, loss_ref):
        d = h_ref[0] - t_ref[0]
        dh_ref[0] = d * (1.0 / D)

        @pl.when(pl.program_id(0) == 0)
        def _():
            loss_ref[...] = jnp.zeros_like(loss_ref)

        loss_ref[...] += jnp.sum(d * d) * (0.5 / D)

    spec = pl.BlockSpec((1, tm, D), lambda i: (0, i, 0))
    dh, loss = pl.pallas_call(
        body, name="loss_head",
        out_shape=[jax.ShapeDtypeStruct((1, L, D), F32), jax.ShapeDtypeStruct((8, 128), F32)],
        grid=(L // tm,), in_specs=[spec, spec],
        out_specs=[spec, pl.BlockSpec((8, 128), lambda i: (0, 0))],
        compiler_params=_cparams(("arbitrary",)),
    )(h, target)
    return dh, loss[0, 0]


def _adamw(parts, w, m, v, name):
    per_layer = isinstance(parts, (list, tuple))
    plist = list(parts) if per_layer else [parts]
    nl = len(plist) if per_layer else parts.shape[0]
    s, r, c = plist[0].shape[-3:]
    tr = r
    for cand in (512, 256, 128, 64, 32, 16):
        if r % cand == 0 and cand * c <= 131072:
            tr = cand
            break
    nr = r // tr
    npart = len(plist)
    c1 = 1.0 / (1.0 - ADAM_B1 ** ADAM_STEP)
    c2 = 1.0 / (1.0 - ADAM_B2 ** ADAM_STEP)

    def body(*refs):
        w_ref, m_ref, v_ref, g_ref, d_ref, nm_ref, nv_ref = refs[npart:]

        def update(read):
            g = read(0).astype(F32)
            for q in range(1, s):
                g = g + read(q).astype(F32)
            mn = ADAM_B1 * m_ref[0] + (1.0 - ADAM_B1) * g
            vn = ADAM_B2 * v_ref[0] + (1.0 - ADAM_B2) * (g * g)
            g_ref[0] = g
            nm_ref[0] = mn
            nv_ref[0] = vn
            d_ref[0] = -ADAM_LR * ((mn * c1) / (jnp.sqrt(vn * c2) + ADAM_EPS) + ADAM_WD * w_ref[0])

        if not per_layer:
            update(lambda q: refs[0][0, q])
        else:
            for l in range(nl):
                @pl.when(pl.program_id(0) == l)
                def _(l=l):
                    update(lambda q: refs[l][q])

    spec = pl.BlockSpec((1, tr, c), lambda li, i: (li, i, 0))
    shp = jax.ShapeDtypeStruct((nl, r, c), F32)
    if per_layer:
        pspecs = [pl.BlockSpec((s, tr, c), lambda li, i, l=l: (0, jnp.where(li == l, i, jnp.where(li > l, nr - 1, 0)), 0))
                  for l in range(nl)]
    else:
        pspecs = [pl.BlockSpec((1, s, tr, c), lambda li, i: (li, 0, i, 0))]
    return pl.pallas_call(
        body, name=name, out_shape=[shp] * 4, grid=(nl, nr),
        in_specs=pspecs + [spec, spec, spec], out_specs=[spec] * 4,
        compiler_params=_cparams(("arbitrary", "arbitrary")),
    )(*plist, w, m, v)


def _pack_rows(vecs, lanes=128, mult=8):
    flat = jnp.concatenate([v.reshape(-1) for v in vecs])
    n = flat.shape[0]
    rows = -(-n // (mult * lanes)) * mult
    return jnp.pad(flat, (0, rows * lanes - n)).reshape(rows, lanes)


def _unpack_rows(packed, shapes):
    flat = packed.reshape(-1)
    out, pos = [], 0
    for s in shapes:
        n = 1
        for d_ in s:
            n *= d_
        out.append(flat[pos:pos + n].reshape(s))
        pos += n
    return out


def _unshard_last(g):
    lead = g.shape[1:-1]
    return jnp.moveaxis(g, 0, -2).reshape(*lead, NDEV * g.shape[-1])


def _shard_last(full):
    lead, w = full.shape[:-1], full.shape[-1] // NDEV
    return jnp.moveaxis(full.reshape(*lead, NDEV, w), -2, 0)


def _rope_tables():
    rows = L // GRID_W
    row = jnp.repeat(jnp.arange(rows), GRID_W).astype(F32)
    col = jnp.tile(jnp.arange(GRID_W), rows).astype(F32)
    axis_dim = HEAD_DIM // 2
    inv_freq = ROPE_BASE ** (-jnp.arange(0, axis_dim, 2, dtype=F32) / axis_dim)
    ang_r, ang_c = row[:, None] * inv_freq[None, :], col[:, None] * inv_freq[None, :]
    ang = jnp.concatenate([ang_r, ang_r, ang_c, ang_c], axis=-1)
    ang = jnp.concatenate([ang, ang], axis=-1)[None]
    return jnp.cos(ang), jnp.sin(ang)


def _heads(x, nh):
    t = x.shape[1]
    return x.reshape(t, nh, HEAD_DIM).transpose(1, 0, 2)


def _unheads(x):
    nh, t, _ = x.shape
    return x.transpose(1, 0, 2).reshape(1, t, nh * HEAD_DIM)


FFN_HALO = 16
FFN_PAIRS = 4


def _ffn_tile(t):
    return 512 if t == L else 256


def _halo_specs(t, tm, block, index):
    per, last = tm // FFN_HALO, t // FFN_HALO - 1
    return [pl.BlockSpec(block(tm), lambda d, i: index(d, i)),
            pl.BlockSpec(block(FFN_HALO), lambda d, i: index(d, jnp.maximum(i * per - 1, 0))),
            pl.BlockSpec(block(FFN_HALO), lambda d, i: index(d, jnp.minimum((i + 1) * per, last)))]


def _seg_edges(i, tm, t):
    return (i * tm == 0) | (i * tm == L), ((i + 1) * tm == L) | ((i + 1) * tm == t)


FFN_RC = 8


def _sigmoid_t(x):
    return 0.5 * jnp.tanh(0.5 * x) + 0.5


class _RowShifts:
    def __init__(self, buf, s):
        self.buf, self.s, self.memo = buf, s, {}
        rows = lax.broadcasted_iota(jnp.int32, (FFN_RC, FFN_BLK), 0)
        self.first, self.last = rows == 0, rows == FFN_RC - 1

    def chunk(self, r):
        if r not in self.memo:
            self.memo[r] = self.buf[self.s, r:r + FFN_RC, :]
        return self.memo[r]

    def rot(self, r, by):
        if (r, by) not in self.memo:
            self.memo[(r, by)] = pltpu.roll(self.chunk(r), by, 0)
        return self.memo[(r, by)]

    def triple(self, r):
        before = jnp.where(self.first, self.rot(r - FFN_RC, 1), self.rot(r, 1))
        behind = jnp.where(self.last, self.rot(r + FFN_RC, FFN_RC - 1), self.rot(r, FFN_RC - 1))
        return before, self.chunk(r), behind


def _conv3_of(triple, cw_ref, s, flip=False):
    taps = [cw_ref[s, 0, k:k + 1, :] for k in ((2, 1, 0) if flip else (0, 1, 2))]
    return taps[0] * triple[0] + taps[1] * triple[1] + taps[2] * triple[2]


def _ffn_core_fwd(a2, up, cw, cb, down, name):
    t = a2.shape[1]
    tm = _ffn_tile(t)
    h0 = FFN_HALO

    def body(a_ref, ap_ref, an_ref, up_ref, cw_ref, cb_ref, dn_ref, z_ref, f_ref, abuf, zbuf, ubuf):
        d, i = pl.program_id(0), pl.program_id(1)
        seg_first, seg_last = _seg_edges(i, tm, t)
        abuf[0:h0, :] = ap_ref[0]
        abuf[h0:h0 + tm, :] = a_ref[0]
        abuf[h0 + tm:, :] = an_ref[0]
        hm, split = tm // 2, tm // 2 + 2 * h0
        keep_first = jnp.where(seg_first, 0.0, 1.0)
        keep_last = jnp.where(seg_last, 0.0, 1.0)
        for s in range(2):
            zbuf[s, 0:split, :] = jnp.dot(abuf[0:split, :], up_ref[s, 0], preferred_element_type=F32)
            zbuf[s, 0:h0, :] = zbuf[s, 0:h0, :] * keep_first
        for s in range(2):
            zbuf[s, split:, :] = jnp.dot(abuf[split:, :], up_ref[s, 0], preferred_element_type=F32)
            zbuf[s, h0 + tm:, :] = zbuf[s, h0 + tm:, :] * keep_last
        zs = [_RowShifts(zbuf, 0), _RowShifts(zbuf, 1)]
        prods = []
        for half in range(2):
            lo = h0 + half * hm
            for s in range(2):
                z_ref[s, 0, half * hm:(half + 1) * hm, :] = zbuf[s, lo:lo + hm, :].astype(z_ref.dtype)
            for r in range(lo, lo + hm, FFN_RC):
                zg = _conv3_of(zs[0].triple(r), cw_ref, 0) + cb_ref[0, 0]
                zv = _conv3_of(zs[1].triple(r), cw_ref, 1) + cb_ref[1, 0]
                ubuf[r - h0:r - h0 + FFN_RC, :] = zg * _sigmoid_t(zg) * zv
            prods.append(jnp.dot(ubuf[half * hm:(half + 1) * hm, :].astype(BF16), dn_ref[0],
                                 preferred_element_type=F32))
        prod = jnp.concatenate(prods, axis=0)
        rows = pl.ds(pl.multiple_of(i * tm, tm), tm)

        @pl.when(d == 0)
        def _():
            f_ref[0, rows, :] = prod

        @pl.when(d > 0)
        def _():
            f_ref[0, rows, :] += prod

    pair = lambda r, c: pl.BlockSpec((2, 1, r, c), lambda d, i: (0, d, 0, 0))
    return pl.pallas_call(
        body, name=name,
        out_shape=[jax.ShapeDtypeStruct((2, FFN_PAIRS, t, FFN_BLK), BF16), jax.ShapeDtypeStruct((1, t, D), F32)],
        grid=(FFN_PAIRS, t // tm),
        in_specs=_halo_specs(t, tm, lambda r: (1, r, D), lambda d, i: (0, i, 0))
        + [pair(D, FFN_BLK), pair(FFN_K, FFN_BLK), pair(1, FFN_BLK),
           pl.BlockSpec((1, FFN_BLK, D), lambda d, i: (d, 0, 0))],
        out_specs=[pl.BlockSpec((2, 1, tm, FFN_BLK), lambda d, i: (0, d, i, 0)),
                   pl.BlockSpec((1, t, D), lambda d, i: (0, 0, 0))],
        scratch_shapes=[pltpu.VMEM((tm + 2 * h0, D), BF16), pltpu.VMEM((2, tm + 2 * h0, FFN_BLK), F32),
                        pltpu.VMEM((tm, FFN_BLK), F32)],
        compiler_params=_cparams(("arbitrary", "arbitrary")),
    )(a2, a2, a2, up, cw, cb, down)


def _ffn_core_bwd(df, z, cw, cb, down, up, a2, name):
    t = df.shape[1]
    tm = _ffn_tile(t)
    h0 = FFN_HALO
    ni = t // tm
    w0, wn = h0 // 2, tm + h0
    tn = (((0,), (0,)), ((), ()))
    nt = (((1,), (1,)), ((), ()))

    def body(df_ref, dfp_ref, dfn_ref, z_ref, zp_ref, zn_ref, cw_ref, cb_ref, dn_ref, up_ref, a2_ref,
             dcw_ref, dcb_ref, ddn_ref, da_ref, dup_ref, dfbuf, zbuf, dzbuf, acc, dubuf, dzo, acc_up):
        d, i = pl.program_id(0), pl.program_id(1)
        seg_first, seg_last = _seg_edges(i, tm, t)
        dfbuf[0:h0, :] = dfp_ref[0]
        dfbuf[h0:h0 + tm, :] = df_ref[0]
        dfbuf[h0 + tm:, :] = dfn_ref[0]
        for s in range(2):
            zbuf[s, 0:h0, :] = zp_ref[s, 0].astype(F32)
            zbuf[s, h0:h0 + tm, :] = z_ref[s, 0].astype(F32)
            zbuf[s, h0 + tm:, :] = zn_ref[s, 0].astype(F32)

        @pl.when(seg_first)
        def _():
            zbuf[:, 0:h0, :] = jnp.zeros((2, h0, FFN_BLK), F32)

        @pl.when(seg_last)
        def _():
            zbuf[:, h0 + tm:, :] = jnp.zeros((2, h0, FFN_BLK), F32)

        dubuf[...] = lax.dot_general(dfbuf[...], dn_ref[0], nt, preferred_element_type=F32)

        zs = [_RowShifts(zbuf, 0), _RowShifts(zbuf, 1)]
        sums = [[jnp.zeros((FFN_RC, FFN_BLK), F32)] * (FFN_K + 1) for _ in range(2)]
        for r in range(w0, w0 + wn, FFN_RC):
            tz = [zs[0].triple(r), zs[1].triple(r)]
            zg = _conv3_of(tz[0], cw_ref, 0) + cb_ref[0, 0]
            zv = _conv3_of(tz[1], cw_ref, 1) + cb_ref[1, 0]
            sg = _sigmoid_t(zg)
            silu = zg * sg
            du = dubuf[r:r + FFN_RC, :]
            dzc = [du * zv * (sg * (1.0 + zg * (1.0 - sg))), du * silu]
            dzbuf[0, r:r + FFN_RC, :] = dzc[0]
            dzbuf[1, r:r + FFN_RC, :] = dzc[1]
            dubuf[r:r + FFN_RC, :] = silu * zv
            if h0 <= r < h0 + tm:
                for s in range(2):
                    sums[s] = [sums[s][k] + dzc[s] * tz[s][k] for k in range(FFN_K)] + [sums[s][FFN_K] + dzc[s]]

        @pl.when(seg_first)
        def _():
            dzbuf[:, w0:h0, :] = jnp.zeros((2, h0 - w0, FFN_BLK), F32)

        @pl.when(seg_last)
        def _():
            dzbuf[:, h0 + tm:w0 + wn, :] = jnp.zeros((2, w0, FFN_BLK), F32)

        @pl.when(i == 0)
        def _():
            dcw_ref[...] = jnp.zeros_like(dcw_ref)
            dcb_ref[...] = jnp.zeros_like(dcb_ref)

        da = None
        for s in range(2):
            dzs = _RowShifts(dzbuf, s)
            for r in range(h0, h0 + tm, FFN_RC):
                dzo[r - h0:r - h0 + FFN_RC, :] = _conv3_of(dzs.triple(r), cw_ref, s, flip=True)
            dzb = dzo[...].astype(BF16)
            part = lax.dot_general(dzb, up_ref[s, 0], nt, preferred_element_type=F32)
            da = part if da is None else da + part
            gup = lax.dot_general(dzb, a2_ref[0], tn, preferred_element_type=F32)

            @pl.when(i == 0)
            def _(s=s, gup=gup):
                acc_up[s] = gup

            @pl.when(i > 0)
            def _(s=s, gup=gup):
                acc_up[s] += gup
            for k in range(FFN_K):
                dcw_ref[s, 0, k:k + 1, :] += jnp.sum(sums[s][k], axis=0, keepdims=True)
            dcb_ref[s, 0] += jnp.sum(sums[s][FFN_K], axis=0, keepdims=True)
        rows = pl.ds(pl.multiple_of(i * tm, tm), tm)

        @pl.when(d == 0)
        def _():
            da_ref[0, rows, :] = da

        @pl.when(d > 0)
        def _():
            da_ref[0, rows, :] += da

        prod = lax.dot_general(dubuf[h0:h0 + tm, :].astype(BF16), dfbuf[h0:h0 + tm, :], tn, preferred_element_type=F32)

        @pl.when(i == 0)
        def _():
            acc[...] = prod

        @pl.when(i > 0)
        def _():
            acc[...] += prod

        @pl.when(i == ni - 1)
        def _():
            ddn_ref[0] = acc[...].astype(ddn_ref.dtype)
            dup_ref[:, 0] = acc_up[...].astype(dup_ref.dtype)

    pair = lambda r, c: pl.BlockSpec((2, 1, r, c), lambda d, i: (0, d, 0, 0))
    return pl.pallas_call(
        body, name=name,
        out_shape=[jax.ShapeDtypeStruct((2, FFN_PAIRS, FFN_K, FFN_BLK), F32),
                   jax.ShapeDtypeStruct((2, FFN_PAIRS, 1, FFN_BLK), F32),
                   jax.ShapeDtypeStruct((FFN_PAIRS, FFN_BLK, D), BF16), jax.ShapeDtypeStruct((1, t, D), F32),
                   jax.ShapeDtypeStruct((2, FFN_PAIRS, FFN_BLK, D), BF16)],
        grid=(FFN_PAIRS, ni),
        in_specs=_halo_specs(t, tm, lambda r: (1, r, D), lambda d, i: (0, i, 0))
        + _halo_specs(t, tm, lambda r: (2, 1, r, FFN_BLK), lambda d, i: (0, d, i, 0))
        + [pair(FFN_K, FFN_BLK), pair(1, FFN_BLK), pl.BlockSpec((1, FFN_BLK, D), lambda d, i: (d, 0, 0)),
           pair(D, FFN_BLK), pl.BlockSpec((1, tm, D), lambda d, i: (0, i, 0))],
        out_specs=[pair(FFN_K, FFN_BLK),
                   pair(1, FFN_BLK), pl.BlockSpec((1, FFN_BLK, D), lambda d, i: (d, 0, 0)),
                   pl.BlockSpec((1, t, D), lambda d, i: (0, 0, 0)), pair(FFN_BLK, D)],
        scratch_shapes=[pltpu.VMEM((tm + 2 * h0, D), BF16), pltpu.VMEM((2, tm + 2 * h0, FFN_BLK), F32),
                        pltpu.VMEM((2, tm + 2 * h0, FFN_BLK), F32), pltpu.VMEM((FFN_BLK, D), F32),
                        pltpu.VMEM((tm + 2 * h0, FFN_BLK), F32),
                        pltpu.VMEM((tm, FFN_BLK), F32), pltpu.VMEM((2, FFN_BLK, D), F32)],
        compiler_params=_cparams(("arbitrary", "arbitrary")),
    )(df, df, df, z, z, z, cw, cb, down, up, a2)


def _ffn_fwd(i, h, mod, ng, wts):
    a2 = _rw_fwd(f_modnorm, [(h, 0)], [(ng[2], "one"), (mod["sc2"], "seg"), (mod["sh2"], "seg")],
                 [(D, BF16)], name=f"ffn{i}_norm")[0]
    z, f = _ffn_core_fwd(a2, wts["up"], wts["cw"], wts["cb"], wts["down"], f"ffn{i}_core")
    h2 = _rw_fwd(f_resgate, [(h, 0), (f, 0)], [(mod["g2"], "seg"), (ng[3], "one")], [(D, F32)],
                 name=f"ffn{i}_res")[0]
    return h2, (h, a2, z, f)


def _ffn_bwd(i, dh, res, mod, ng, wts):
    h, a2, z, f = res
    t = h.shape[1]
    df, dg2, dng3 = _rw_bwd(f_gate_rms, [(f, 0)], [(mod["g2"], "seg"), (ng[3], "one")], [dh],
                            name=f"ffn{i}_res_b", row_grad=[(0, BF16)], param_grad=[0, 1])
    dcw, dcb, d_down, da2, d_up = _ffn_core_bwd(df, z, wts["cw"], wts["cb"], wts["down"], wts["up"], a2,
                                                f"ffn{i}_core_b")
    dcw, dcb = dcw.reshape(NDEV, FFN_K, FFN_BLK), dcb.reshape(NDEV, 1, FFN_BLK)
    d_up = d_up.reshape(NDEV, FFN_BLK, D)
    dh_in, dng2, dsc2, dsh2 = _rw_bwd(
        f_modnorm, [(h, 0)], [(ng[2], "one"), (mod["sc2"], "seg"), (mod["sh2"], "seg")], [da2],
        name=f"ffn{i}_norm_b", row_grad=[(0, F32)], param_grad=[0, 1, 2], add=dh)
    grads = dict(up=d_up, down=d_down, cw=dcw, cb=dcb, ng2=dng2, ng3=dng3, sc2=dsc2, sh2=dsh2, g2=dg2)
    return dh_in, grads


def _mixer_norm_fwd(i, h, mod, ng):
    return _rw_fwd(f_modnorm, [(h, 0)], [(ng[0], "one"), (mod["sc1"], "seg"), (mod["sh1"], "seg")],
                   [(D, BF16)], name=f"mix{i}_norm")[0]


def _mixer_norm_bwd(i, h, mod, ng, da, dh):
    return _rw_bwd(f_modnorm, [(h, 0)], [(ng[0], "one"), (mod["sc1"], "seg"), (mod["sh1"], "seg")], [da],
                   name=f"mix{i}_norm_b", row_grad=[(0, F32)], param_grad=[0, 1, 2], add=dh)


def _conformer_fwd(i, h, mod, ng, wts):
    a = _mixer_norm_fwd(i, h, mod, ng)
    p = _mm(a, wts["w_in"], "nn", out_dtype=BF16, name=f"cm{i}_in")
    z = _rw_fwd(f_glu, [(p, 0, 2)], [(wts["b_in"], "one", 2)], [(D, F32)], name=f"cm{i}_glu")[0]
    zc = _dwconv(z, wts["dw_w"], wts["dw_b"], name=f"cm{i}_conv")
    r = _rw_fwd(f_lnsilu, [(zc, 0)], [(wts["ln_g"], "one"), (wts["ln_b"], "one")], [(D, BF16)],
                name=f"cm{i}_ln")[0]
    y = _mm(r, wts["w_out"], "nn", name=f"cm{i}_out")
    h2 = _rw_fwd(f_resgate_bias, [(h, 0), (y, 0)], [(mod["g1"], "seg"), (ng[1], "one"), (wts["b_out"], "one")],
                 [(D, F32)], name=f"cm{i}_res")[0]
    return h2, (h, a, p, z, zc, r, y)


def _conformer_bwd(i, dh, res, mod, ng, wts):
    h, a, p, z, zc, r, y = res
    dy, dg1, dng1, db_out = _rw_bwd(
        f_gate_rms_bias, [(y, 0)], [(mod["g1"], "seg"), (ng[1], "one"), (wts["b_out"], "one")], [dh],
        name=f"cm{i}_res_b", row_grad=[(0, BF16)], param_grad=[0, 1, 2])
    dr = _mm(dy, wts["w_out"], "nt", name=f"cm{i}_out_bx")
    d_w_out = _mm(r, dy, "tn", out_dtype=BF16, name=f"cm{i}_out_bw")
    dzc, dln_g, dln_b = _rw_bwd(f_lnsilu, [(zc, 0)], [(wts["ln_g"], "one"), (wts["ln_b"], "one")], [dr],
                                name=f"cm{i}_ln_b", row_grad=[(0, F32)], param_grad=[0, 1])
    ddw_w, ddw_b = _dwconv_wgrad(z, dzc, CM_K, name=f"cm{i}_conv_bw")
    dz = _dwconv(dzc, wts["dw_w"][:, ::-1, :], None, name=f"cm{i}_conv_bx")
    dp, db_in = _rw_bwd(f_glu, [(p, 0, 2)], [(wts["b_in"], "one", 2)], [dz], name=f"cm{i}_glu_b",
                        row_grad=[(0, BF16)], param_grad=[0])
    d_w_in = _mm(a, dp, "tn", out_dtype=BF16, name=f"cm{i}_in_bw")
    da = _mm(dp, wts["w_in"], "nt", name=f"cm{i}_in_bx")
    dh_in, dng0, dsc1, dsh1 = _mixer_norm_bwd(i, h, mod, ng, da, dh)
    grads = dict(w_in=d_w_in, w_out=d_w_out, b_in=db_in, dw_w=ddw_w, dw_b=ddw_b, ln_g=dln_g, ln_b=dln_b,
                 b_out=db_out, ng0=dng0, ng1=dng1, sc1=dsc1, sh1=dsh1, g1=dg1)
    return dh_in, grads


def _attention_fwd(i, h_all, mod, ng, wts, tables):
    a = _mixer_norm_fwd(i, h_all, mod, ng)
    qkv = _mm(a, wts["w_qkv"], "nn", name="attn_qkv")
    kv0 = N_Q * HEAD_DIM
    kv1 = kv0 + N_KV * HEAD_DIM
    q = _rope(qkv, tables, False, BF16, "attn_rope_q", cols=(kv0, 0))
    k = _rope(qkv, tables, False, BF16, "attn_rope_k", cols=(kv1 - kv0, kv0 // (kv1 - kv0)))
    pad = ((0, 0), (ATTN_BLOCK, ATTN_BLOCK), (0, 0))
    q_h = _heads(q, N_Q)
    k_h = jnp.pad(_heads(k, N_KV), pad)
    v_h = jnp.pad(_heads(qkv[:, :L, kv1:].astype(BF16), N_KV), pad)
    kc_h = _heads(qkv[:, L:, kv0:kv1].astype(BF16), N_KV)
    vc_h = _heads(qkv[:, L:, kv1:].astype(BF16), N_KV)
    o_h, lse = _attn_fwd(q_h, k_h, v_h, kc_h, vc_h, wts["sink"])
    o = _unheads(o_h)
    y = _mm(o, wts["w_o"], "nn", name="attn_o")
    h_lat = h_all[:, :L]
    mod_lat = {k_: v_[:1] for k_, v_ in mod.items()}
    h2 = _rw_fwd(f_resgate, [(h_lat, 0), (y, 0)], [(mod_lat["g1"], "seg"), (ng[1], "one")], [(D, F32)],
                 name="attn_res")[0]
    return h2, (h_all, a, q_h, k_h, v_h, kc_h, vc_h, o_h, lse, o, y)


def _attention_bwd(i, dh, res, mod, ng, wts, tables):
    h_all, a, q_h, k_h, v_h, kc_h, vc_h, o_h, lse, o, y = res
    mod_lat = {k_: v_[:1] for k_, v_ in mod.items()}
    dy, dg1, dng1 = _rw_bwd(f_gate_rms, [(y, 0)], [(mod_lat["g1"], "seg"), (ng[1], "one")], [dh],
                            name="attn_res_b", row_grad=[(0, BF16)], param_grad=[0, 1])
    do = _mm(dy, wts["w_o"], "nt", name="attn_o_bx")
    d_w_o = _mm(o, dy, "tn", out_dtype=BF16, name="attn_o_bw")
    dq_h, dk_h, dv_h, dkc_h, dvc_h, dsink = _attn_bwd(q_h, k_h, v_h, kc_h, vc_h, wts["sink"], o_h, lse,
                                                        _heads(do, N_Q))
    dq = _rope(_unheads(dq_h), tables, True, BF16, "attn_rope_q_b")
    dk = _rope(_unheads(dk_h[:, ATTN_BLOCK:-ATTN_BLOCK]), tables, True, BF16, "attn_rope_k_b")
    dv = _unheads(dv_h[:, ATTN_BLOCK:-ATTN_BLOCK]).astype(BF16)
    d_lat = jnp.concatenate([dq, dk, dv], axis=2)
    d_ctx = jnp.concatenate([jnp.zeros((1, LC, N_Q * HEAD_DIM), BF16), _unheads(dkc_h).astype(BF16),
                             _unheads(dvc_h).astype(BF16)], axis=2)
    dqkv = jnp.concatenate([d_lat, d_ctx], axis=1)
    d_w_qkv = _mm(a, dqkv, "tn", out_dtype=BF16, name="attn_qkv_bw")
    da = _mm(dqkv, wts["w_qkv"], "nt", name="attn_qkv_bx")
    dh_res = jnp.concatenate([dh, jnp.zeros((1, LC, D), F32)], axis=1)
    dh_in, dng0, dsc1, dsh1 = _mixer_norm_bwd(i, h_all, mod, ng, da, dh_res)
    grads = dict(w_qkv=d_w_qkv, w_o=d_w_o, sink=dsink, ng0=dng0, ng1=dng1, sc1=dsc1, sh1=dsh1, g1=dg1)
    return dh_in, grads


def _gmlp_fwd(i, h, mod, ng, wts):
    a = _mixer_norm_fwd(i, h, mod, ng)
    p = _mm(a, wts["w_in"], "nn", out_dtype=BF16, name="gm_in")
    u, v = _rw_fwd(f_gmlp_pre, [(p, 0, 2)], [(wts["b_in"], "one", 2), (wts["ln_g"], "one"), (wts["ln_b"], "one")],
                   [(GM_W, F32), (GM_W, BF16)], name="gm_pre")
    us = _gm_spatial_fwd(u, v, wts["w_s"], wts["b_s"])
    y = _mm(us, wts["w_out"], "nn", name="gm_out")
    h2 = _rw_fwd(f_resgate, [(h, 0), (y, 0)], [(mod["g1"], "seg"), (ng[1], "one")], [(D, F32)],
                 name="gm_res")[0]
    return h2, (h, a, p, u, v, us, y)


def _gmlp_bwd(i, dh, res, mod, ng, wts):
    h, a, p, u, v, us, y = res
    dy, dg1, dng1 = _rw_bwd(f_gate_rms, [(y, 0)], [(mod["g1"], "seg"), (ng[1], "one")], [dh],
                            name="gm_res_b", row_grad=[(0, BF16)], param_grad=[0, 1])
    dus = _mm(dy, wts["w_out"], "nt", name="gm_out_bx")
    d_w_out = _mm(us, dy, "tn", out_dtype=BF16, name="gm_out_bw")
    du, dv, dws, dbs = _gm_spatial_bwd(u, v, wts["w_s"], wts["b_s"], dus)
    dp, db_in, dln_g, dln_b = _rw_bwd(
        f_gmlp_pre, [(p, 0, 2)], [(wts["b_in"], "one", 2), (wts["ln_g"], "one"), (wts["ln_b"], "one")], [du, dv],
        name="gm_pre_b", row_grad=[(0, BF16)], param_grad=[0, 1, 2])
    d_w_in = _mm(a, dp, "tn", out_dtype=BF16, name="gm_in_bw")
    da = _mm(dp, wts["w_in"], "nt", name="gm_in_bx")
    dh_in, dng0, dsc1, dsh1 = _mixer_norm_bwd(i, h, mod, ng, da, dh)
    grads = dict(w_in=d_w_in, w_out=d_w_out, b_in=db_in, ln_g=dln_g, ln_b=dln_b, w_s=dws, b_s=dbs,
                 ng0=dng0, ng1=dng1, sc1=dsc1, sh1=dsh1, g1=dg1)
    return dh_in, grads


MOD_NAMES = ("sh1", "sc1", "g1", "sh2", "sc2", "g2")
SMALL = (
    ("norm_g", (4, 4, 128)), ("ffn_conv_w", (4, 3, 704)), ("cm_b_in", (2, 256)), ("cm_dw_w", (2, 31, 128)),
    ("cm_dw_b", (2, 128)), ("cm_ln_g", (2, 128)), ("cm_ln_b", (2, 128)), ("cm_b_out", (2, 128)),
    ("gm_b_in", (1, 512)), ("gm_ln_g", (1, 256)), ("gm_ln_b", (1, 256)))


def _mixer_weights(i, P):
    if i % 3 == 0:
        j = i // 3
        return dict(w_in=P["cm_w_in"][j], w_out=P["cm_w_out"][j], b_in=P["cm_b_in"][j].reshape(1, 1, 2 * D),
                    dw_w=P["cm_dw_w"][j][None], dw_b=P["cm_dw_b"][j].reshape(1, 1, D),
                    ln_g=P["cm_ln_g"][j].reshape(1, 1, D), ln_b=P["cm_ln_b"][j].reshape(1, 1, D),
                    b_out=P["cm_b_out"][j].reshape(1, 1, D))
    if i % 3 == 1:
        return dict(w_qkv=P["attn_w_qkv"], w_o=P["attn_w_o"], sink=P["attn_sink"].reshape(N_Q))
    return dict(w_in=P["gm_w_in"], w_out=P["gm_w_out"], b_in=P["gm_b_in"].reshape(1, 1, 2 * GM_W),
                ln_g=P["gm_ln_g"].reshape(1, 1, GM_W), ln_b=P["gm_ln_b"].reshape(1, 1, GM_W),
                w_s=P["gm_w_s"].reshape(GM_GROUPS, GM_CHUNK, GM_CHUNK).astype(BF16),
                b_s=P["gm_b_s"].reshape(GM_GROUPS, GM_CHUNK, 1))


def _ffn_weights(i, P):
    return dict(up=P["ffn_w_up"][i].reshape(2, FFN_PAIRS, D, FFN_BLK), down=P["ffn_w_down"][i],
                cw=P["ffn_conv_w"][i].reshape(2, FFN_PAIRS, FFN_K, FFN_BLK),
                cb=P["ffn_conv_b"][i].reshape(2, FFN_PAIRS, 1, FFN_BLK))


def _local_step(x, ctx, target, lat_mod, ctx_mod, norm_g, layer_weights, grads_ready):
    tables = _rope_tables()
    ng = [[norm_g[i, j].reshape(1, 1, D) for j in range(4)] for i in range(DEPTH)]

    def mods(i, with_ctx, token):
        out = {}
        for j, nme in enumerate(MOD_NAMES):
            rows = [lat_mod[i, j]] + ([ctx_mod[i, j]] if with_ctx else [])
            out[nme] = jnp.stack(rows).reshape(len(rows), 1, D) + token[0, 0]
        return out

    def after(mod, token):
        return mod if token is None else {k_: v_ + token[0, 0] for k_, v_ in mod.items()}

    h_all = jnp.concatenate([x, ctx], axis=1)
    wm0, wf0, tok = layer_weights(0, h_all)
    m0 = mods(0, True, tok)
    h, r0m = _conformer_fwd(0, h_all, m0, ng[0], wm0)
    wf0 = wf0(h) if callable(wf0) else wf0
    h, r0f = _ffn_fwd(0, h, m0, ng[0], wf0)
    wm1, wf1, tok = layer_weights(1, h)
    m1 = mods(1, True, tok)
    m1l = {k_: v_[:1] for k_, v_ in m1.items()}
    h, r1m = _attention_fwd(1, h, m1, ng[1], wm1, tables)
    wf1 = wf1(h) if callable(wf1) else wf1
    h, r1f = _ffn_fwd(1, h, m1l, ng[1], wf1)
    wm2, wf2, tok = layer_weights(2, h)
    m2 = mods(2, False, tok)
    h, r2m = _gmlp_fwd(2, h, m2, ng[2], wm2)
    wf2 = wf2(h) if callable(wf2) else wf2
    h, r2f = _ffn_fwd(2, h, m2, ng[2], wf2)
    wm3, wf3, tok = layer_weights(3, h)
    m3 = mods(3, False, tok)
    h, r3m = _conformer_fwd(3, h, m3, ng[3], wm3)
    wf3 = wf3(h) if callable(wf3) else wf3
    h, r3f = _ffn_fwd(3, h, m3, ng[3], wf3)
    dh, loss = _loss_head(h, target)

    G = {}
    dh, G["f3"] = _ffn_bwd(3, dh, r3f, m3, ng[3], wf3)
    tok = grads_ready("f3", G["f3"])
    dh, G["m3"] = _conformer_bwd(3, dh, r3m, after(m3, tok), ng[3], wm3)
    tok = grads_ready("m3", G["m3"])
    dh, G["f2"] = _ffn_bwd(2, dh, r2f, after(m2, tok), ng[2], wf2)
    tok = grads_ready("f2", G["f2"])
    dh, G["m2"] = _gmlp_bwd(2, dh, r2m, after(m2, tok), ng[2], wm2)
    tok = grads_ready("m2", G["m2"])
    dh, G["f1"] = _ffn_bwd(1, dh, r1f, after(m1l, tok), ng[1], wf1)
    tok = grads_ready("f1", G["f1"])
    dh, G["m1"] = _attention_bwd(1, dh, r1m, after(m1, tok), ng[1], wm1, tables)
    tok = grads_ready("m1", G["m1"])
    dh, G["f0"] = _ffn_bwd(0, dh, r0f, after(m0, tok), ng[0], wf0)
    tok = grads_ready("f0", G["f0"])
    dh, G["m0"] = _conformer_bwd(0, dh, r0m, after(m0, tok), ng[0], wm0)
    grads_ready("m0", G["m0"])
    grad_x = dh[:, :L]

    zero = jnp.zeros((D,), F32)
    dmod = []
    for seg in range(2):
        per_layer = []
        for i in range(DEPTH):
            vals = []
            for nme in MOD_NAMES:
                src = G[("m" if nme.endswith("1") else "f") + str(i)][nme]
                vals.append(src[seg, 0] if src.shape[0] > seg else zero)
            per_layer.append(jnp.concatenate(vals))
        dmod.append(jnp.stack(per_layer))
    dmod = jnp.stack(dmod)
    return loss, grad_x, G, dmod


def kernel(x, c, ctx, c_ctx, ada_w, ada_b, norm_g, ffn_w_up, ffn_conv_w, ffn_conv_b, ffn_w_down, cm_w_in, cm_b_in, cm_dw_w, cm_dw_b, cm_ln_g, cm_ln_b, cm_w_out, cm_b_out, attn_w_qkv, attn_sink, attn_w_o, gm_w_in, gm_b_in, gm_ln_g, gm_ln_b, gm_w_s, gm_b_s, gm_w_out, loss_target, m_c_ctx, m_ada_w, m_ada_b, m_norm_g, m_ffn_w_up, m_ffn_conv_w, m_ffn_conv_b, m_ffn_w_down, m_cm_w_in, m_cm_b_in, m_cm_dw_w, m_cm_dw_b, m_cm_ln_g, m_cm_ln_b, m_cm_w_out, m_cm_b_out, m_attn_w_qkv, m_attn_sink, m_attn_w_o, m_gm_w_in, m_gm_b_in, m_gm_ln_g, m_gm_ln_b, m_gm_w_s, m_gm_b_s, m_gm_w_out, v_c_ctx, v_ada_w, v_ada_b, v_norm_g, v_ffn_w_up, v_ffn_conv_w, v_ffn_conv_b, v_ffn_w_down, v_cm_w_in, v_cm_b_in, v_cm_dw_w, v_cm_dw_b, v_cm_ln_g, v_cm_ln_b, v_cm_w_out, v_cm_b_out, v_attn_w_qkv, v_attn_sink, v_attn_w_o, v_gm_w_in, v_gm_b_in, v_gm_ln_g, v_gm_ln_b, v_gm_w_s, v_gm_b_s, v_gm_w_out):
    W = dict(c_ctx=c_ctx, ada_w=ada_w, ada_b=ada_b, norm_g=norm_g, ffn_w_up=ffn_w_up, ffn_conv_w=ffn_conv_w, ffn_conv_b=ffn_conv_b, ffn_w_down=ffn_w_down, cm_w_in=cm_w_in, cm_b_in=cm_b_in, cm_dw_w=cm_dw_w, cm_dw_b=cm_dw_b, cm_ln_g=cm_ln_g, cm_ln_b=cm_ln_b, cm_w_out=cm_w_out, cm_b_out=cm_b_out, attn_w_qkv=attn_w_qkv, attn_sink=attn_sink, attn_w_o=attn_w_o, gm_w_in=gm_w_in, gm_b_in=gm_b_in, gm_ln_g=gm_ln_g, gm_ln_b=gm_ln_b, gm_w_s=gm_w_s, gm_b_s=gm_b_s, gm_w_out=gm_w_out)
    M = dict(c_ctx=m_c_ctx, ada_w=m_ada_w, ada_b=m_ada_b, norm_g=m_norm_g, ffn_w_up=m_ffn_w_up, ffn_conv_w=m_ffn_conv_w, ffn_conv_b=m_ffn_conv_b, ffn_w_down=m_ffn_w_down, cm_w_in=m_cm_w_in, cm_b_in=m_cm_b_in, cm_dw_w=m_cm_dw_w, cm_dw_b=m_cm_dw_b, cm_ln_g=m_cm_ln_g, cm_ln_b=m_cm_ln_b, cm_w_out=m_cm_w_out, cm_b_out=m_cm_b_out, attn_w_qkv=m_attn_w_qkv, attn_sink=m_attn_sink, attn_w_o=m_attn_w_o, gm_w_in=m_gm_w_in, gm_b_in=m_gm_b_in, gm_ln_g=m_gm_ln_g, gm_ln_b=m_gm_ln_b, gm_w_s=m_gm_w_s, gm_b_s=m_gm_b_s, gm_w_out=m_gm_w_out)
    V = dict(c_ctx=v_c_ctx, ada_w=v_ada_w, ada_b=v_ada_b, norm_g=v_norm_g, ffn_w_up=v_ffn_w_up, ffn_conv_w=v_ffn_conv_w, ffn_conv_b=v_ffn_conv_b, ffn_w_down=v_ffn_w_down, cm_w_in=v_cm_w_in, cm_b_in=v_cm_b_in, cm_dw_w=v_cm_dw_w, cm_dw_b=v_cm_dw_b, cm_ln_g=v_cm_ln_g, cm_ln_b=v_cm_ln_b, cm_w_out=v_cm_w_out, cm_b_out=v_cm_b_out, attn_w_qkv=v_attn_w_qkv, attn_sink=v_attn_sink, attn_w_o=v_attn_w_o, gm_w_in=v_gm_w_in, gm_b_in=v_gm_b_in, gm_ln_g=v_gm_ln_g, gm_ln_b=v_gm_ln_b, gm_w_s=v_gm_w_s, gm_b_s=v_gm_b_s, gm_w_out=v_gm_w_out)
    me = 4 * lax.axis_index("x") + 2 * lax.axis_index("y") + lax.axis_index("c")
    small_shapes = [s for _, s in SMALL]

    small = _pack_rows([W[n] for n, _ in SMALL] + [c])
    layer_mats = [("cm_w_in", 0, "cm_w_out", 0), ("attn_w_qkv", 0, "attn_w_o", 0), ("gm_w_in", 0, "gm_w_out", 0),
                  ("cm_w_in", 1, "cm_w_out", 1)]
    local_bf16 = [[W[a][ja].astype(BF16), W[b][jb].astype(BF16), ffn_w_up[i].astype(BF16), ffn_w_down[i].astype(BF16)]
                  for i, (a, ja, b, jb) in enumerate(layer_mats)]
    gathered0 = _all_gather([small] + local_bf16[0], "gather_params0")
    small_g = gathered0[0]
    col_to_full = lambda g: g.transpose(1, 0, 2).reshape(g.shape[1], NDEV * g.shape[2])
    P = {}
    unpacked = jax.vmap(lambda r: tuple(_unpack_rows(r, small_shapes + [(D,)])))(small_g)
    for (n, _), g in zip(SMALL, unpacked[:-1]):
        if n == "ffn_conv_w":
            P[n] = [g[:, i] for i in range(DEPTH)]
        else:
            P[n] = _unshard_last(g)
    c_all = unpacked[-1]
    P["ffn_conv_b"] = [ffn_conv_b[i].reshape(NDEV, 1, FFN_BLK) for i in range(DEPTH)]
    P["attn_sink"], P["gm_w_s"], P["gm_b_s"] = attn_sink, gm_w_s, gm_b_s

    cond = jnp.concatenate([c_all, c_ctx[None], jnp.zeros((7, D), F32)])[None]
    scond = _rw_fwd(f_silu, [(cond, 0)], [], [(D, BF16)], name="ada_silu")[0]
    ada_bf = ada_w.astype(BF16)
    ncol = ada_w.shape[2]
    mod_loc = _mm(scond, ada_bf, "nn", name="ada_proj")
    mod_loc = mod_loc + lax.dynamic_slice_in_dim(ada_b, me * ncol, ncol, axis=1)[:, None, :]
    mod_g = _all_gather([mod_loc], "gather_mod")[0]
    mod_full = mod_g.transpose(1, 2, 0, 3).reshape(DEPTH, 16, 6, D)
    lat_mod = lax.dynamic_index_in_dim(mod_full, me, axis=1, keepdims=False)
    ctx_mod = mod_full[:, NDEV]

    gathers, exchanges, pending = {}, {}, {}
    col_to_parts = lambda g: g[0].reshape(g.shape[1], NDEV, g.shape[2] // NDEV).transpose(1, 0, 2)
    row_to_parts = lambda g: g.reshape(NDEV, -1, g.shape[-1])
    no_order = jnp.zeros((8, 128), F32)

    def layer_weights(i, h):
        if i == 0:
            mats = gathered0[1:]
            gathers[1] = _xfer_start("gather", local_bf16[1], mod_g, "gather_params1")
        else:
            mats = _xfer_wait(gathers[i], h)
            if i + 1 < DEPTH:
                gathers[i + 1] = _xfer_start("gather", local_bf16[i + 1], mats[0], f"gather_params{i + 1}")
        token = gathers[i + 1]["token"] if i + 1 < DEPTH else no_order
        a, ja, b, jb = layer_mats[i]
        pi = dict(P)
        pi[a] = col_to_full(mats[0]) if a.startswith(("attn", "gm")) else {ja: col_to_full(mats[0])}
        pi[b] = mats[1].reshape(-1, D) if b.startswith(("attn", "gm")) else {jb: mats[1].reshape(-1, D)}
        pi["ffn_w_up"], pi["ffn_w_down"] = {i: mats[2]}, {i: mats[3].reshape(FFN_PAIRS, FFN_BLK, D)}
        return _mixer_weights(i, pi), _ffn_weights(i, pi), token

    def grads_ready(tag, g):
        pending[tag] = g
        i = int(tag[1])
        col_name, row_name = {0: ("w_in", "w_out"), 1: ("w_qkv", "w_o"), 2: ("w_in", "w_out")}[i % 3]
        if tag == "f0":
            arrs = [g["up"], row_to_parts(g["down"])]
        elif tag == "m0":
            arrs = [col_to_parts(g[col_name]), row_to_parts(g[row_name])]
        elif tag[0] == "m":
            gf = pending[f"f{i}"]
            arrs = [col_to_parts(g[col_name]), row_to_parts(g[row_name]), gf["up"], row_to_parts(gf["down"])]
        else:
            return None
        exchanges[tag] = _xfer_start("scatter", arrs, no_order, "exchange_" + tag)
        if tag == "m2":
            gathers["w_s"] = _xfer_start("gather", [g["w_s"].reshape(GM_GROUPS * GM_CHUNK, GM_CHUNK)],
                                         exchanges[tag]["token"], "gather_gm_w_s")
            return gathers["w_s"]["token"]
        return exchanges[tag]["token"]

    loss_part, grad_x, G, dmod = _local_step(x, ctx, loss_target, lat_mod, ctx_mod, P["norm_g"], layer_weights,
                                             grads_ready)
    recv = {tag: _xfer_wait(exchanges[tag], grad_x) for tag in ("m3", "m2", "m1")}

    dmod_g = _all_gather([dmod], "gather_dmod")[0]
    dm_cols = lax.dynamic_slice_in_dim(dmod_g, me * ncol, ncol, axis=3)
    dm_ext = dm_cols.transpose(2, 1, 0, 3).reshape(DEPTH, 16, ncol)
    cond_ext = jnp.concatenate([c_all, jnp.broadcast_to(c_ctx[None], (NDEV, D))])[None]
    scond_ext = _rw_fwd(f_silu, [(cond_ext, 0)], [], [(D, BF16)], name="ada_silu_ext")[0]
    g_ada_w = _mm(scond_ext, dm_ext, "tn", name="ada_proj_bw")
    dsil = _mm(dm_ext, ada_bf, "nt", reduce_blocks=True, name="ada_proj_bx")
    dcc = _rw_bwd(f_silu_rows, [(jnp.zeros((1, NDEV, D), F32), 0)], [(c_ctx.reshape(1, 1, D), "one")],
                  [dsil[:, NDEV:]], name="ada_silu_b", param_grad=[0])[0]

    out = {}

    def put(name, res):
        out[name] = res

    d_norm_g = jnp.stack([jnp.stack([G[f"m{i}"]["ng0"], G[f"m{i}"]["ng1"], G[f"f{i}"]["ng2"], G[f"f{i}"]["ng3"]])
                          for i in range(DEPTH)]).reshape(DEPTH, 4, D)
    small_full = dict(
        norm_g=d_norm_g,
        cm_b_in=jnp.stack([G["m0"]["b_in"], G["m3"]["b_in"]]).reshape(2, 2 * D),
        cm_dw_w=jnp.stack([G["m0"]["dw_w"][0], G["m3"]["dw_w"][0]]),
        cm_dw_b=jnp.stack([G["m0"]["dw_b"], G["m3"]["dw_b"]]).reshape(2, D),
        cm_ln_g=jnp.stack([G["m0"]["ln_g"], G["m3"]["ln_g"]]).reshape(2, D),
        cm_ln_b=jnp.stack([G["m0"]["ln_b"], G["m3"]["ln_b"]]).reshape(2, D),
        cm_b_out=jnp.stack([G["m0"]["b_out"], G["m3"]["b_out"]]).reshape(2, D),
        gm_b_in=G["m2"]["b_in"].reshape(1, 2 * GM_W),
        gm_ln_g=G["m2"]["ln_g"].reshape(1, GM_W), gm_ln_b=G["m2"]["ln_b"].reshape(1, GM_W))
    by_dest = []
    for n, _ in SMALL:
        if n == "ffn_conv_w":
            by_dest.append(jnp.stack([G[f"f{i}"]["cw"] for i in range(DEPTH)], axis=1))
        else:
            by_dest.append(_shard_last(small_full[n]))
    small_send = jax.vmap(lambda *vs: _pack_rows(list(vs)))(*by_dest)
    small_recv = _all_to_all([[small_send]], "exchange_small")[0]

    def shard3(a):
        return a.reshape(a.shape[0], -1, a.shape[-1])

    small_local = lambda d_: _pack_rows([d_[n] for n, _ in SMALL])[None]
    res = _adamw(small_recv, small_local(W), small_local(M), small_local(V), "adamw_small")
    unp = [_unpack_rows(r[0], small_shapes) for r in res]
    for q, (n, _) in enumerate(SMALL):
        put(n, [unp[t][q] for t in range(4)])

    repl_names = ["c_ctx", "ffn_conv_b", "attn_sink", "gm_b_s"]
    repl_part = dict(
        c_ctx=dcc.reshape(D),
        ffn_conv_b=jnp.stack([G[f"f{i}"]["cb"].reshape(2 * 2816) for i in range(DEPTH)]),
        attn_sink=G["m1"]["sink"].reshape(1, N_Q),
        gm_b_s=G["m2"]["b_s"].reshape(1, GM_GROUPS, GM_CHUNK))
    w_s_parts = _xfer_wait(gathers["w_s"], grad_x)[0]
    flat_s = lambda a: a.reshape(1, GM_GROUPS * GM_CHUNK, GM_CHUNK)
    res = _adamw(w_s_parts[None], flat_s(gm_w_s), flat_s(m_gm_w_s), flat_s(v_gm_w_s), "adamw_gm_w_s")
    put("gm_w_s", [r.reshape(gm_w_s.shape) for r in res])
    repl_shapes = [W[n].shape for n in repl_names]
    repl_sent = _pack_rows([repl_part[n] for n in repl_names] + [loss_part.reshape(1)], mult=256)
    repl_g = _all_gather([repl_sent], "gather_repl")[0]
    loss = jnp.sum(repl_g.reshape(NDEV, -1)[:, sum(W[n].size for n in repl_names)])
    repl_local = lambda d_: _pack_rows([d_[n] for n in repl_names], mult=256)[None]
    res = _adamw(repl_g[None], repl_local(W), repl_local(M), repl_local(V), "adamw_repl")
    unp = [_unpack_rows(r[0], repl_shapes) for r in res]
    for q, n in enumerate(repl_names):
        put(n, [unp[t][q] for t in range(4)])

    def update_big(n, parts):
        turn = (lambda a: jnp.swapaxes(a, 1, 2)) if n == "ffn_w_up" else shard3
        res = _adamw(parts, turn(W[n]), turn(M[n]), turn(V[n]), "adamw_" + n)
        put(n, [(turn(r) if n == "ffn_w_up" else r).reshape(W[n].shape) for r in res])

    put("ada_w", _adamw(g_ada_w[:, None], ada_w, m_ada_w, v_ada_w, "adamw_ada_w"))
    ada_b_parts = dmod_g.reshape(1, 2 * NDEV, DEPTH, 6 * D)
    res = _adamw(ada_b_parts, ada_b[None], m_ada_b[None], v_ada_b[None], "adamw_ada_b")
    put("ada_b", [r[0] for r in res])
    early = dict(attn_w_qkv=[recv["m1"][0]], attn_w_o=[recv["m1"][1]], gm_w_in=[recv["m2"][0]],
                 gm_w_out=[recv["m2"][1]])
    for n, parts in early.items():
        update_big(n, parts)
    done_first = sum(out[n][1].reshape(-1)[:1024] for n in ["ada_w", "ada_b", "gm_w_in", "gm_w_out", "attn_w_qkv"])
    recv.update({tag: _xfer_wait(exchanges[tag], done_first) for tag in ("f0", "m0")})
    late = dict(
        ffn_w_up=[recv["f0"][0]] + [recv[f"m{i}"][2] for i in (1, 2, 3)],
        ffn_w_down=[recv["f0"][1]] + [recv[f"m{i}"][3] for i in (1, 2, 3)],
        cm_w_in=[recv["m0"][0], recv["m3"][0]], cm_w_out=[recv["m0"][1], recv["m3"][1]])
    for n, parts in late.items():
        update_big(n, parts)

    names = ["c_ctx", "ada_w", "ada_b", "norm_g", "ffn_w_up", "ffn_conv_w", "ffn_conv_b", "ffn_w_down", "cm_w_in",
             "cm_b_in", "cm_dw_w", "cm_dw_b", "cm_ln_g", "cm_ln_b", "cm_w_out", "cm_b_out", "attn_w_qkv",
             "attn_sink", "attn_w_o", "gm_w_in", "gm_b_in", "gm_ln_g", "gm_ln_b", "gm_w_s", "gm_b_s", "gm_w_out"]
    return (loss, grad_x, *[out[n][0] for n in names], *[out[n][1] for n in names],
            *[out[n][2] for n in names], *[out[n][3] for n in names])
```
